```python
import math
import jax, jax.numpy as jnp
from jax import lax
import numpy as np

D_MODEL = 2048
BATCH = 8
SEQ = 4096
DEPTH = 4

HEAD_DIM = 64
N_HEADS_TOTAL = D_MODEL // HEAD_DIM
N_HEADS_A = N_HEADS_TOTAL // 4
N_KV_A = N_HEADS_A // 4
N_HEADS_B = N_HEADS_TOTAL // 4
N_HEADS_C = N_HEADS_TOTAL // 2
BLOCK = 128
WINDOW_A = 128
DILATED_PAIRS = ((128, 1), (512, 4), (2048, 16))
N_BUCKETS = 32
T5_MAX_DIST = 2048
D_FF = 256 * (-(-(8 * D_MODEL // 3) // 256))
CONV_WIDTH = 3
EPS = 1e-6
NEG_INF = -1e30

A_Q = N_HEADS_A * HEAD_DIM
A_KV = N_KV_A * HEAD_DIM
B_W = N_HEADS_B * HEAD_DIM
C_W = N_HEADS_C * HEAD_DIM
IN_WIDTH = A_Q + 2 * A_KV + 3 * B_W + 3 * C_W
MIX_WIDTH = A_Q + B_W + C_W

kernel_name = "hymba_style_swa_stickbreak_dilated_convffn"


def rmsnorm(x, g):
    xf = x.astype(jnp.float32)
    y = xf * lax.rsqrt(jnp.mean(xf * xf, axis=-1, keepdims=True) + EPS)
    return (y * g.astype(jnp.float32)).astype(x.dtype)


def t5_bucket(dist):
    max_exact = N_BUCKETS // 2
    d = jnp.maximum(dist, 0)
    large = max_exact + (jnp.log(jnp.maximum(d, 1).astype(jnp.float32) / max_exact)
                         / math.log(T5_MAX_DIST / max_exact) * (N_BUCKETS - max_exact)).astype(jnp.int32)
    large = jnp.minimum(large, N_BUCKETS - 1)
    return jnp.where(d < max_exact, d, large)


def block_rel_bias(table, dil):
    rel = jnp.arange(BLOCK)[:, None] + BLOCK - jnp.arange(2 * BLOCK)[None, :]
    buckets = t5_bucket(rel * dil)
    return jnp.transpose(table[buckets], (2, 0, 1)).astype(jnp.float32)


def banded_attention(q, k, v, bias, max_dist, sinks=None):
    n, length, hq, hd = q.shape
    hk = k.shape[2]
    grp = hq // hk
    lp = -(-length // BLOCK) * BLOCK
    if lp != length:
        padw = ((0, 0), (0, lp - length), (0, 0), (0, 0))
        q, k, v = jnp.pad(q, padw), jnp.pad(k, padw), jnp.pad(v, padw)
    nb = lp // BLOCK
    qb = q.reshape(n, nb, BLOCK, hk, grp, hd).astype(jnp.float32)
    kb = k.reshape(n, nb, BLOCK, hk, hd).astype(jnp.float32)
    vb = v.reshape(n, nb, BLOCK, hk, hd).astype(jnp.float32)
    prev = ((0, 0), (1, 0), (0, 0), (0, 0), (0, 0))
    kk = jnp.concatenate([jnp.pad(kb[:, :-1], prev), kb], axis=2)
    vv = jnp.concatenate([jnp.pad(vb[:, :-1], prev), vb], axis=2)
    logits = (jnp.einsum('nbqhgd,nbkhd->nbhgqk', qb, kk) * (hd ** -0.5)
              + bias.reshape(hk, grp, BLOCK, 2 * BLOCK))
    rel = jnp.arange(BLOCK)[:, None] + BLOCK - jnp.arange(2 * BLOCK)[None, :]
    key_abs = (jnp.arange(nb)[:, None] - 1) * BLOCK + jnp.arange(2 * BLOCK)[None, :]
    mask = ((rel >= 0) & (rel <= max_dist))[None] & (key_abs >= 0)[:, None, :]
    logits = jnp.where(mask[None, :, None, None], logits, NEG_INF)
    m = jnp.max(logits, axis=-1, keepdims=True)
    if sinks is not None:
        s = sinks.astype(jnp.float32).reshape(hk, grp, 1, 1)
        m = jnp.maximum(m, s)
    p = jnp.exp(logits - m)
    denom = jnp.sum(p, axis=-1, keepdims=True)
    if sinks is not None:
        denom = denom + jnp.exp(s - m)
    out = jnp.einsum('nbhgqk,nbkhd->nbhgqd', p / denom, vv)
    out = jnp.transpose(out, (0, 1, 4, 2, 3, 5)).reshape(n, lp, hq, hd)[:, :length]
    lse = jnp.transpose((m + jnp.log(denom))[..., 0], (0, 1, 4, 2, 3)).reshape(n, lp, hq)[:, :length]
    return out, lse


def stick_breaking_attention(q, k, v):
    b_, s_, h, hd = q.shape
    nb = s_ // BLOCK
    qb = jnp.transpose(q.reshape(b_, nb, BLOCK, h, hd), (1, 0, 2, 3, 4))
    kf = k.astype(jnp.float32)
    vf = v.astype(jnp.float32)
    s_pos = jnp.arange(s_)

    def one_block(args):
        qblk, blk = args
        z = jnp.einsum('bqhd,bkhd->bhqk', qblk.astype(jnp.float32), kf) * (hd ** -0.5)
        t_pos = blk * BLOCK + jnp.arange(BLOCK)
        causal = s_pos[None, :] < t_pos[:, None]
        log_rem = jnp.where(causal, jax.nn.log_sigmoid(-z), 0.0)
        suffix = lax.cumsum(log_rem, axis=3, reverse=True) - log_rem
        a = jnp.where(causal, jnp.exp(jax.nn.log_sigmoid(z) + suffix), 0.0)
        return jnp.einsum('bhqk,bkhd->bqhd', a, vf)

    out = lax.map(one_block, (qb, jnp.arange(nb)))
    return jnp.transpose(out, (1, 0, 2, 3, 4)).reshape(b_, s_, h, hd)


def dilated_attention(q, k, v, table_c):
    b_, s_, h, hd = q.shape
    outs, lses = [], []
    for window, dil in DILATED_PAIRS:
        def to_sub(t):
            return jnp.transpose(t.reshape(b_, s_ // dil, dil, h, hd), (0, 2, 1, 3, 4)).reshape(b_ * dil, s_ // dil, h, hd)
        o, lse = banded_attention(to_sub(q), to_sub(k), to_sub(v), block_rel_bias(table_c, dil), window // dil)
        outs.append(jnp.transpose(o.reshape(b_, dil, s_ // dil, h, hd), (0, 2, 1, 3, 4)).reshape(b_, s_, h, hd))
        lses.append(jnp.transpose(lse.reshape(b_, dil, s_ // dil, h), (0, 2, 1, 3)).reshape(b_, s_, h))
    w = jax.nn.softmax(jnp.stack(lses, axis=0), axis=0)
    return jnp.sum(w[..., None] * jnp.stack(outs, axis=0), axis=0)


def causal_dwconv(u, w, b):
    up = jnp.pad(u, ((0, 0), (CONV_WIDTH - 1, 0), (0, 0)))
    s_ = u.shape[1]
    acc = b
    for i in range(CONV_WIDTH):
        acc = acc + w[i] * up[:, i:i + s_]
    return acc


def _fwd_setup_inputs(seed: int = 0) -> dict:
    key = jax.random.key(seed)
    ks = jax.random.split(key, 20)
    f32 = jnp.float32

    def nrm(k, shape, scale):
        return jax.random.normal(k, shape, f32) * scale

    return {
        "x": nrm(ks[0], (BATCH, SEQ, D_MODEL), 1.0),
        "attn_norm": 1.0 + nrm(ks[1], (DEPTH, D_MODEL), 0.02),
        "w_in": nrm(ks[2], (DEPTH, D_MODEL, IN_WIDTH), D_MODEL ** -0.5),
        "a_q_gain": 1.0 + nrm(ks[3], (DEPTH, HEAD_DIM), 0.02),
        "a_k_gain": 1.0 + nrm(ks[4], (DEPTH, HEAD_DIM), 0.02),
        "a_sinks": nrm(ks[5], (DEPTH, N_HEADS_A), 0.5),
        "c_q_gain": 1.0 + nrm(ks[6], (DEPTH, HEAD_DIM), 0.02),
        "c_k_gain": 1.0 + nrm(ks[7], (DEPTH, HEAD_DIM), 0.02),
        "rel_bias_table": nrm(ks[8], (N_BUCKETS, N_HEADS_A + N_HEADS_C), 0.5),
        "mix_out_gain": 1.0 + nrm(ks[9], (DEPTH, MIX_WIDTH), 0.02),
        "w_out": nrm(ks[10], (DEPTH, MIX_WIDTH, D_MODEL), MIX_WIDTH ** -0.5),
        "ffn_norm": 1.0 + nrm(ks[11], (DEPTH, D_MODEL), 0.02),
        "w_up": nrm(ks[12], (DEPTH, D_MODEL, 2 * D_FF), D_MODEL ** -0.5),
        "conv_w": nrm(ks[13], (DEPTH, CONV_WIDTH, 2 * D_FF), CONV_WIDTH ** -0.5),
        "conv_b": nrm(ks[14], (DEPTH, 2 * D_FF), 0.02),
        "w_down": nrm(ks[15], (DEPTH, D_FF, D_MODEL), D_FF ** -0.5),
    }


def _fwd_reference(x, attn_norm, w_in, a_q_gain, a_k_gain, a_sinks, c_q_gain, c_k_gain, rel_bias_table,
              mix_out_gain, w_out, ffn_norm, w_up, conv_w, conv_b, w_down):
    b_, s_, _ = x.shape
    sizes = [A_Q, A_KV, A_KV, B_W, B_W, B_W, C_W, C_W, C_W]
    offsets = [int(o) for o in np.cumsum(sizes)[:-1]]
    table_a = rel_bias_table[:, :N_HEADS_A]
    table_c = rel_bias_table[:, N_HEADS_A:]
    bias_a = block_rel_bias(table_a, 1)
    for l in range(DEPTH):
        h = rmsnorm(x, attn_norm[l])
        proj = jnp.einsum('bsd,de->bse', h, w_in[l])
        aq, ak, av, bq, bk, bv, cq, ck, cv = jnp.split(proj, offsets, axis=-1)
        heads = lambda t, n: t.reshape(b_, s_, n, HEAD_DIM)
        out_a, _ = banded_attention(rmsnorm(heads(aq, N_HEADS_A), a_q_gain[l]),
                                    rmsnorm(heads(ak, N_KV_A), a_k_gain[l]),
                                    heads(av, N_KV_A), bias_a, WINDOW_A - 1, sinks=a_sinks[l])
        out_b = stick_breaking_attention(heads(bq, N_HEADS_B), heads(bk, N_HEADS_B), heads(bv, N_HEADS_B))
        out_c = dilated_attention(rmsnorm(heads(cq, N_HEADS_C), c_q_gain[l]),
                                  rmsnorm(heads(ck, N_HEADS_C), c_k_gain[l]),
                                  heads(cv, N_HEADS_C), table_c)
        g = mix_out_gain[l]
        ya = rmsnorm(out_a.reshape(b_, s_, A_Q), g[:A_Q])
        yb = rmsnorm(out_b.reshape(b_, s_, B_W), g[A_Q:A_Q + B_W])
        yc = rmsnorm(out_c.reshape(b_, s_, C_W), g[A_Q + B_W:])
        mix = jnp.concatenate([ya, yb, yc], axis=-1).astype(x.dtype)
        x = x + jnp.einsum('bse,ed->bsd', mix, w_out[l])
        h = rmsnorm(x, ffn_norm[l])
        u = causal_dwconv(jnp.einsum('bsd,df->bsf', h, w_up[l]), conv_w[l], conv_b[l])
        gate, up = jnp.split(u, [D_FF], axis=-1)
        x = x + jnp.einsum('bsf,fd->bsd', jax.nn.silu(gate) * up, w_down[l])
    return x


import jax as _jax
import jax.numpy as _jnp

TWIN_FORMAT = 'train_step'
FWD_PARAMS = ['x', 'attn_norm', 'w_in', 'a_q_gain', 'a_k_gain', 'a_sinks', 'c_q_gain', 'c_k_gain', 'rel_bias_table', 'mix_out_gain', 'w_out', 'ffn_norm', 'w_up', 'conv_w', 'conv_b', 'w_down']
TWIN_WEIGHTS = ['attn_norm', 'w_in', 'a_q_gain', 'a_k_gain', 'a_sinks', 'c_q_gain', 'c_k_gain', 'rel_bias_table', 'mix_out_gain', 'w_out', 'ffn_norm', 'w_up', 'conv_w', 'conv_b', 'w_down']
TWIN_DIFF_INPUT = 'x'
TWIN_INPUTS = ['x', 'attn_norm', 'w_in', 'a_q_gain', 'a_k_gain', 'a_sinks', 'c_q_gain', 'c_k_gain', 'rel_bias_table', 'mix_out_gain', 'w_out', 'ffn_norm', 'w_up', 'conv_w', 'conv_b', 'w_down', 'loss_target', 'm_attn_norm', 'm_w_in', 'm_a_q_gain', 'm_a_k_gain', 'm_a_sinks', 'm_c_q_gain', 'm_c_k_gain', 'm_rel_bias_table', 'm_mix_out_gain', 'm_w_out', 'm_ffn_norm', 'm_w_up', 'm_conv_w', 'm_conv_b', 'm_w_down', 'v_attn_norm', 'v_w_in', 'v_a_q_gain', 'v_a_k_gain', 'v_a_sinks', 'v_c_q_gain', 'v_c_k_gain', 'v_rel_bias_table', 'v_mix_out_gain', 'v_w_out', 'v_ffn_norm', 'v_w_up', 'v_conv_w', 'v_conv_b', 'v_w_down']
TWIN_OUTPUTS = ['loss', 'grad_x', 'grad_attn_norm', 'grad_w_in', 'grad_a_q_gain', 'grad_a_k_gain', 'grad_a_sinks', 'grad_c_q_gain', 'grad_c_k_gain', 'grad_rel_bias_table', 'grad_mix_out_gain', 'grad_w_out', 'grad_ffn_norm', 'grad_w_up', 'grad_conv_w', 'grad_conv_b', 'grad_w_down', 'delta_attn_norm', 'delta_w_in', 'delta_a_q_gain', 'delta_a_k_gain', 'delta_a_sinks', 'delta_c_q_gain', 'delta_c_k_gain', 'delta_rel_bias_table', 'delta_mix_out_gain', 'delta_w_out', 'delta_ffn_norm', 'delta_w_up', 'delta_conv_w', 'delta_conv_b', 'delta_w_down', 'new_m_attn_norm', 'new_m_w_in', 'new_m_a_q_gain', 'new_m_a_k_gain', 'new_m_a_sinks', 'new_m_c_q_gain', 'new_m_c_k_gain', 'new_m_rel_bias_table', 'new_m_mix_out_gain', 'new_m_w_out', 'new_m_ffn_norm', 'new_m_w_up', 'new_m_conv_w', 'new_m_conv_b', 'new_m_w_down', 'new_v_attn_norm', 'new_v_w_in', 'new_v_a_q_gain', 'new_v_a_k_gain', 'new_v_a_sinks', 'new_v_c_q_gain', 'new_v_c_k_gain', 'new_v_rel_bias_table', 'new_v_mix_out_gain', 'new_v_w_out', 'new_v_ffn_norm', 'new_v_w_up', 'new_v_conv_w', 'new_v_conv_b', 'new_v_w_down']
TWIN_LEAF_KINDS = {'loss': 'loss', 'grad_x': 'grad_x', 'grad_attn_norm': 'grad_w', 'grad_w_in': 'grad_w', 'grad_a_q_gain': 'grad_w', 'grad_a_k_gain': 'grad_w', 'grad_a_sinks': 'grad_w', 'grad_c_q_gain': 'grad_w', 'grad_c_k_gain': 'grad_w', 'grad_rel_bias_table': 'grad_w', 'grad_mix_out_gain': 'grad_w', 'grad_w_out': 'grad_w', 'grad_ffn_norm': 'grad_w', 'grad_w_up': 'grad_w', 'grad_conv_w': 'grad_w', 'grad_conv_b': 'grad_w', 'grad_w_down': 'grad_w', 'delta_attn_norm': 'delta_w', 'delta_w_in': 'delta_w', 'delta_a_q_gain': 'delta_w', 'delta_a_k_gain': 'delta_w', 'delta_a_sinks': 'delta_w', 'delta_c_q_gain': 'delta_w', 'delta_c_k_gain': 'delta_w', 'delta_rel_bias_table': 'delta_w', 'delta_mix_out_gain': 'delta_w', 'delta_w_out': 'delta_w', 'delta_ffn_norm': 'delta_w', 'delta_w_up': 'delta_w', 'delta_conv_w': 'delta_w', 'delta_conv_b': 'delta_w', 'delta_w_down': 'delta_w', 'new_m_attn_norm': 'new_m', 'new_m_w_in': 'new_m', 'new_m_a_q_gain': 'new_m', 'new_m_a_k_gain': 'new_m', 'new_m_a_sinks': 'new_m', 'new_m_c_q_gain': 'new_m', 'new_m_c_k_gain': 'new_m', 'new_m_rel_bias_table': 'new_m', 'new_m_mix_out_gain': 'new_m', 'new_m_w_out': 'new_m', 'new_m_ffn_norm': 'new_m', 'new_m_w_up': 'new_m', 'new_m_conv_w': 'new_m', 'new_m_conv_b': 'new_m', 'new_m_w_down': 'new_m', 'new_v_attn_norm': 'new_v', 'new_v_w_in': 'new_v', 'new_v_a_q_gain': 'new_v', 'new_v_a_k_gain': 'new_v', 'new_v_a_sinks': 'new_v', 'new_v_c_q_gain': 'new_v', 'new_v_c_k_gain': 'new_v', 'new_v_rel_bias_table': 'new_v', 'new_v_mix_out_gain': 'new_v', 'new_v_w_out': 'new_v', 'new_v_ffn_norm': 'new_v', 'new_v_w_up': 'new_v', 'new_v_conv_w': 'new_v', 'new_v_conv_b': 'new_v', 'new_v_w_down': 'new_v'}


def _forward(args):
    return _fwd_reference(*[args[k] for k in FWD_PARAMS])


def _output_shape():
    out = _jax.eval_shape(lambda: _forward(_fwd_setup_inputs(0)))
    return out.shape, out.dtype

N_MICROBATCH = 1
ADAM_LR = 0.001
ADAM_B1 = 0.9
ADAM_B2 = 0.999
ADAM_EPS = 1e-08
ADAM_WD = 0.01
ADAM_STEP = 10
PER_EXAMPLE_BATCH_AXIS = {'x': 0, 'loss_target': 0}
SHARED_INPUTS = []
_WEIGHT_DTYPES = {'attn_norm': _jnp.float32, 'w_in': _jnp.float32, 'a_q_gain': _jnp.float32, 'a_k_gain': _jnp.float32, 'a_sinks': _jnp.float32, 'c_q_gain': _jnp.float32, 'c_k_gain': _jnp.float32, 'rel_bias_table': _jnp.float32, 'mix_out_gain': _jnp.float32, 'w_out': _jnp.float32, 'ffn_norm': _jnp.float32, 'w_up': _jnp.float32, 'conv_w': _jnp.float32, 'conv_b': _jnp.float32, 'w_down': _jnp.float32}
MOMENT_SCALE = {'attn_norm': 4.776449e+00, 'w_in': 3.041162e+00, 'a_q_gain': 2.445753e+00, 'a_k_gain': 2.472326e+00, 'a_sinks': 9.571196e-01, 'c_q_gain': 4.391147e+00, 'c_k_gain': 4.398487e+00, 'rel_bias_table': 3.348553e+00, 'mix_out_gain': 1.490729e+01, 'w_out': 4.954233e+00, 'ffn_norm': 1.210441e+01, 'w_up': 5.598185e-01, 'conv_w': 1.755112e+00, 'conv_b': 2.822196e+00, 'w_down': 9.305714e-01}


def _to_microbatches(a, axis):
    t = _jnp.moveaxis(a, axis, 0)
    t = t.reshape((N_MICROBATCH, t.shape[0] // N_MICROBATCH) + t.shape[1:])
    return _jnp.moveaxis(t, 1, axis + 1)


def setup_inputs(seed: int = 0) -> dict:
    inp = _fwd_setup_inputs(seed)
    key = _jax.random.fold_in(_jax.random.key(seed), 7919)
    shape, _ = _output_shape()
    out = dict(inp)
    out["loss_target"] = _jax.random.normal(_jax.random.fold_in(key, 0), shape, _jnp.float32)
    for i, name in enumerate(TWIN_WEIGHTS):
        w = inp[name].astype(_jnp.float32)
        if MOMENT_SCALE is None:
            s = _jnp.sqrt(_jnp.mean(_jnp.square(w)) + 1e-30)
        else:
            s = MOMENT_SCALE[name]
        km, kv = _jax.random.split(_jax.random.fold_in(key, i + 1))
        out[name] = w
        out["m_" + name] = s * _jax.random.normal(km, w.shape, _jnp.float32)
        out["v_" + name] = (s * s) * _jax.random.uniform(kv, w.shape, _jnp.float32, 0.5, 1.5)
    if N_MICROBATCH > 1:
        for name, axis in PER_EXAMPLE_BATCH_AXIS.items():
            out[name] = _to_microbatches(out[name], axis)
    return {'x': out['x'], 'attn_norm': out['attn_norm'], 'w_in': out['w_in'], 'a_q_gain': out['a_q_gain'], 'a_k_gain': out['a_k_gain'], 'a_sinks': out['a_sinks'], 'c_q_gain': out['c_q_gain'], 'c_k_gain': out['c_k_gain'], 'rel_bias_table': out['rel_bias_table'], 'mix_out_gain': out['mix_out_gain'], 'w_out': out['w_out'], 'ffn_norm': out['ffn_norm'], 'w_up': out['w_up'], 'conv_w': out['conv_w'], 'conv_b': out['conv_b'], 'w_down': out['w_down'], 'loss_target': out['loss_target'], 'm_attn_norm': out['m_attn_norm'], 'm_w_in': out['m_w_in'], 'm_a_q_gain': out['m_a_q_gain'], 'm_a_k_gain': out['m_a_k_gain'], 'm_a_sinks': out['m_a_sinks'], 'm_c_q_gain': out['m_c_q_gain'], 'm_c_k_gain': out['m_c_k_gain'], 'm_rel_bias_table': out['m_rel_bias_table'], 'm_mix_out_gain': out['m_mix_out_gain'], 'm_w_out': out['m_w_out'], 'm_ffn_norm': out['m_ffn_norm'], 'm_w_up': out['m_w_up'], 'm_conv_w': out['m_conv_w'], 'm_conv_b': out['m_conv_b'], 'm_w_down': out['m_w_down'], 'v_attn_norm': out['v_attn_norm'], 'v_w_in': out['v_w_in'], 'v_a_q_gain': out['v_a_q_gain'], 'v_a_k_gain': out['v_a_k_gain'], 'v_a_sinks': out['v_a_sinks'], 'v_c_q_gain': out['v_c_q_gain'], 'v_c_k_gain': out['v_c_k_gain'], 'v_rel_bias_table': out['v_rel_bias_table'], 'v_mix_out_gain': out['v_mix_out_gain'], 'v_w_out': out['v_w_out'], 'v_ffn_norm': out['v_ffn_norm'], 'v_w_up': out['v_w_up'], 'v_conv_w': out['v_conv_w'], 'v_conv_b': out['v_conv_b'], 'v_w_down': out['v_w_down']}


def _loss(weights, diff, rest, loss_target):
    with _jax.named_scope("forward"):
        args = {**rest, TWIN_DIFF_INPUT: diff, **{k: w.astype(_WEIGHT_DTYPES[k]) for k, w in weights.items()}}
        y = _forward(args)
    with _jax.named_scope("loss_head"):
        err = _jnp.square(y.astype(_jnp.float32) - loss_target)
        return 0.5 * _jnp.sum(_jnp.mean(err, axis=-1)) if err.ndim else 0.5 * err


def _adamw(w, g, m, v):
    m = ADAM_B1 * m + (1.0 - ADAM_B1) * g
    v = ADAM_B2 * v + (1.0 - ADAM_B2) * _jnp.square(g)
    m_hat = m / (1.0 - ADAM_B1 ** ADAM_STEP)
    v_hat = v / (1.0 - ADAM_B2 ** ADAM_STEP)
    delta = -ADAM_LR * (m_hat / (_jnp.sqrt(v_hat) + ADAM_EPS) + ADAM_WD * w)
    return delta, m, v


def reference(x, attn_norm, w_in, a_q_gain, a_k_gain, a_sinks, c_q_gain, c_k_gain, rel_bias_table, mix_out_gain, w_out, ffn_norm, w_up, conv_w, conv_b, w_down, loss_target, m_attn_norm, m_w_in, m_a_q_gain, m_a_k_gain, m_a_sinks, m_c_q_gain, m_c_k_gain, m_rel_bias_table, m_mix_out_gain, m_w_out, m_ffn_norm, m_w_up, m_conv_w, m_conv_b, m_w_down, v_attn_norm, v_w_in, v_a_q_gain, v_a_k_gain, v_a_sinks, v_c_q_gain, v_c_k_gain, v_rel_bias_table, v_mix_out_gain, v_w_out, v_ffn_norm, v_w_up, v_conv_w, v_conv_b, v_w_down):
    given = dict(x=x, attn_norm=attn_norm, w_in=w_in, a_q_gain=a_q_gain, a_k_gain=a_k_gain, a_sinks=a_sinks, c_q_gain=c_q_gain, c_k_gain=c_k_gain, rel_bias_table=rel_bias_table, mix_out_gain=mix_out_gain, w_out=w_out, ffn_norm=ffn_norm, w_up=w_up, conv_w=conv_w, conv_b=conv_b, w_down=w_down, loss_target=loss_target, m_attn_norm=m_attn_norm, m_w_in=m_w_in, m_a_q_gain=m_a_q_gain, m_a_k_gain=m_a_k_gain, m_a_sinks=m_a_sinks, m_c_q_gain=m_c_q_gain, m_c_k_gain=m_c_k_gain, m_rel_bias_table=m_rel_bias_table, m_mix_out_gain=m_mix_out_gain, m_w_out=m_w_out, m_ffn_norm=m_ffn_norm, m_w_up=m_w_up, m_conv_w=m_conv_w, m_conv_b=m_conv_b, m_w_down=m_w_down, v_attn_norm=v_attn_norm, v_w_in=v_w_in, v_a_q_gain=v_a_q_gain, v_a_k_gain=v_a_k_gain, v_a_sinks=v_a_sinks, v_c_q_gain=v_c_q_gain, v_c_k_gain=v_c_k_gain, v_rel_bias_table=v_rel_bias_table, v_mix_out_gain=v_mix_out_gain, v_w_out=v_w_out, v_ffn_norm=v_ffn_norm, v_w_up=v_w_up, v_conv_w=v_conv_w, v_conv_b=v_conv_b, v_w_down=v_w_down)
    weights = {n: given[n] for n in TWIN_WEIGHTS}
    shared = {n: given[n] for n in SHARED_INPUTS}
    per_example = {n: given[n] for n in ['x']}
    grad_fn = _jax.value_and_grad(_loss, argnums=(0, 1))

    def one_microbatch(ex, loss_target):
        ex = dict(ex)
        diff = ex.pop(TWIN_DIFF_INPUT)
        return grad_fn(weights, diff, {**shared, **ex}, loss_target)

    if N_MICROBATCH == 1:
        loss, (grad_w, grad_x) = one_microbatch(per_example, given["loss_target"])
    else:
        def body(carry, xs):
            loss_sum, grad_sum = carry
            l_k, (gw_k, gx_k) = one_microbatch(xs[0], xs[1])
            with _jax.named_scope("update"):
                return (loss_sum + l_k, _jax.tree.map(_jnp.add, grad_sum, gw_k)), gx_k

        init = (_jnp.zeros((), _jnp.float32), _jax.tree.map(_jnp.zeros_like, weights))
        (loss, grad_w), grad_x = _jax.lax.scan(body, init, (per_example, given["loss_target"]))
    with _jax.named_scope("update"):
        delta_w, new_m, new_v = {}, {}, {}
        for n in TWIN_WEIGHTS:
            delta_w[n], new_m[n], new_v[n] = _adamw(weights[n], grad_w[n], given["m_" + n], given["v_" + n])
    return (loss, grad_x, *[grad_w[n] for n in TWIN_WEIGHTS], *[delta_w[n] for n in TWIN_WEIGHTS],
            *[new_m[n] for n in TWIN_WEIGHTS], *[new_v[n] for n in TWIN_WEIGHTS])
```

```python
import functools
import math

import jax
import jax.numpy as jnp
from jax import lax
from jax.experimental import pallas as pl
from jax.experimental.pallas import tpu as pltpu

F32, BF16, I32 = jnp.float32, jnp.bfloat16, jnp.int32
MESH = pl.DeviceIdType.MESH

HEAD_DIM = 64
LANES = 128
BLOCK = 128
EPS = 1e-6
NEG_INF = -1e30
N_BUCKETS = 32
T5_MAX_DIST = 2048
WINDOW_A = 128
DILATIONS = (1, 4, 16)
N_DEV = 8
VMEM_LIMIT = 56 * 1024 * 1024

ADAM_LR, ADAM_B1, ADAM_B2, ADAM_EPS, ADAM_WD, ADAM_STEP = 0.001, 0.9, 0.999, 1e-08, 0.01, 10


def _params(sem=None, vmem=None):
    return pltpu.CompilerParams(dimension_semantics=sem, vmem_limit_bytes=vmem)


def _pick(n, cands):
    for c in cands:
        if n % c == 0:
            return c
    raise ValueError(f"no tile for {n}")


def _dot(a, b):
    return lax.dot_general(a, b, (((1,), (0,)), ((), ())), preferred_element_type=F32)


def _dot_nt(a, b):
    return lax.dot_general(a, b, (((1,), (1,)), ((), ())), preferred_element_type=F32)


def _dot_tn(a, b):
    return lax.dot_general(a, b, (((0,), (0,)), ((), ())), preferred_element_type=F32)


def matmul(a, b, *, trans_b=False, out_dtype=F32, res=None, name):
    m, k = a.shape
    n = b.shape[0] if trans_b else b.shape[1]
    tm = _pick(m, (1024, 512, 256))
    tn = _pick(n, (1024, 768, 512, 256, 128))
    tk = k if k <= 2048 else _pick(k, (1024, 768, 512, 256))
    nk = k // tk
    dn = (((1,), (1,)), ((), ())) if trans_b else (((1,), (0,)), ((), ()))

    def body(*refs):
        if res is None:
            a_ref, b_ref, o_ref, acc = refs
        else:
            a_ref, b_ref, r_ref, o_ref, acc = refs
        kk = pl.program_id(2)

        @pl.when(kk == 0)
        def _():
            acc[...] = jnp.zeros_like(acc)

        acc[...] += lax.dot_general(a_ref[...].astype(BF16), b_ref[...].astype(BF16), dn,
                                    preferred_element_type=F32)

        @pl.when(kk == nk - 1)
        def _():
            r = acc[...]
            if res is not None:
                r = r_ref[...] + r
            o_ref[...] = r.astype(out_dtype)

    b_spec = (pl.BlockSpec((tn, tk), lambda i, j, kk: (j, kk)) if trans_b
              else pl.BlockSpec((tk, tn), lambda i, j, kk: (kk, j)))
    in_specs = [pl.BlockSpec((tm, tk), lambda i, j, kk: (i, kk)), b_spec]
    args = [a, b]
    if res is not None:
        in_specs.append(pl.BlockSpec((tm, tn), lambda i, j, kk: (i, j)))
        args.append(res)
    return pl.pallas_call(
        body, name=name, grid=(m // tm, n // tn, nk),
        in_specs=in_specs, out_specs=pl.BlockSpec((tm, tn), lambda i, j, kk: (i, j)),
        out_shape=jax.ShapeDtypeStruct((m, n), out_dtype),
        scratch_shapes=[pltpu.VMEM((tm, tn), F32)],
        compiler_params=_params(("parallel", "parallel", "arbitrary"), VMEM_LIMIT),
    )(*args)


def rmsnorm_fwd(x, g, *, name):
    s, d = x.shape
    tm = 512

    def body(x_ref, g_ref, o_ref):
        xv = x_ref[...]
        r = lax.rsqrt(jnp.mean(xv * xv, axis=-1, keepdims=True) + EPS)
        o_ref[...] = (xv * r * g_ref[...]).astype(BF16)

    return pl.pallas_call(
        body, name=name, grid=(s // tm,),
        in_specs=[pl.BlockSpec((tm, d), lambda i: (i, 0)), pl.BlockSpec((1, d), lambda i: (0, 0))],
        out_specs=pl.BlockSpec((tm, d), lambda i: (i, 0)),
        out_shape=jax.ShapeDtypeStruct((s, d), BF16),
        compiler_params=_params(("parallel",)),
    )(x, g.reshape(1, d))


def rmsnorm_bwd(dh, x, g, dres, *, name):
    s, d = x.shape
    tm = 256

    def body(dh_ref, x_ref, g_ref, dres_ref, dx_ref, dg_ref):
        @pl.when(pl.program_id(0) == 0)
        def _():
            dg_ref[...] = jnp.zeros_like(dg_ref)

        xv, dhv = x_ref[...], dh_ref[...]
        r = lax.rsqrt(jnp.mean(xv * xv, axis=-1, keepdims=True) + EPS)
        gd = dhv * g_ref[...]
        dot = jnp.mean(gd * xv, axis=-1, keepdims=True)
        dx_ref[...] = dres_ref[...] + (r * gd - xv * (r * r * r * dot))
        dg_ref[...] += jnp.sum(dhv * (xv * r), axis=0, keepdims=True)

    dx, dg = pl.pallas_call(
        body, name=name, grid=(s // tm,),
        in_specs=[pl.BlockSpec((tm, d), lambda i: (i, 0)), pl.BlockSpec((tm, d), lambda i: (i, 0)),
                  pl.BlockSpec((1, d), lambda i: (0, 0)), pl.BlockSpec((tm, d), lambda i: (i, 0))],
        out_specs=[pl.BlockSpec((tm, d), lambda i: (i, 0)), pl.BlockSpec((1, d), lambda i: (0, 0))],
        out_shape=[jax.ShapeDtypeStruct((s, d), F32), jax.ShapeDtypeStruct((1, d), F32)],
        compiler_params=_params(("arbitrary",)),
    )(dh, x, g.reshape(1, d), dres)
    return dx, dg[0]


def loss_head(y, target, *, name):
    s, d = y.shape
    tm = 512

    def body(y_ref, t_ref, dy_ref, l_ref):
        @pl.when(pl.program_id(0) == 0)
        def _():
            l_ref[...] = jnp.zeros_like(l_ref)

        e = y_ref[...] - t_ref[...]
        dy_ref[...] = e / float(d)
        per_tok = jnp.mean(e * e, axis=-1, keepdims=True)
        l_ref[...] += 0.5 * jnp.sum(per_tok, axis=0, keepdims=True)

    dy, l = pl.pallas_call(
        body, name=name, grid=(s // tm,),
        in_specs=[pl.BlockSpec((tm, d), lambda i: (i, 0)), pl.BlockSpec((tm, d), lambda i: (i, 0))],
        out_specs=[pl.BlockSpec((tm, d), lambda i: (i, 0)), pl.BlockSpec((8, LANES), lambda i: (0, 0))],
        out_shape=[jax.ShapeDtypeStruct((s, d), F32), jax.ShapeDtypeStruct((8, LANES), F32)],
        compiler_params=_params(("arbitrary",)),
    )(y, target)
    return dy, l[0, 0]


FFN_TN = 256
FFN_CH = 256


def _rows_before(ref, r0, first):
    if first:
        cur = ref[pl.ds(0, FFN_CH), :]
        row = lax.broadcasted_iota(I32, cur.shape, 0)
        sh1 = jnp.where(row < 1, 0.0, pltpu.roll(cur, 1, axis=0))
        sh2 = jnp.where(row < 2, 0.0, pltpu.roll(cur, 2, axis=0))
        return cur, sh1, sh2
    ext = ref[pl.ds(pl.multiple_of(r0 - 8, 8), FFN_CH + 8), :]
    return ext[8:], pltpu.roll(ext, 1, axis=0)[8:], pltpu.roll(ext, 2, axis=0)[8:]


def _rows_after(ref, r0, last):
    if last:
        cur = ref[pl.ds(r0, FFN_CH), :]
        row = lax.broadcasted_iota(I32, cur.shape, 0)
        up1 = jnp.where(row >= FFN_CH - 1, 0.0, pltpu.roll(cur, FFN_CH - 1, axis=0))
        up2 = jnp.where(row >= FFN_CH - 2, 0.0, pltpu.roll(cur, FFN_CH - 2, axis=0))
        return cur, up1, up2
    n = FFN_CH + 8
    ext = ref[pl.ds(r0, n), :]
    return ext[:FFN_CH], pltpu.roll(ext, n - 1, axis=0)[:FFN_CH], pltpu.roll(ext, n - 2, axis=0)[:FFN_CH]


def _sigmoid(x):
    return 1.0 / (1.0 + jnp.exp(-x))


def ffn_act_fwd(p, conv_w, conv_b, *, name):
    s, f2 = p.shape
    f = f2 // 2
    nj = f // FFN_TN
    nch = s // FFN_CH

    def body(pg_ref, pu_ref, wg_ref, wu_ref, bg_ref, bu_ref, a_ref):
        def conv(ref, w_ref, b_ref, r0, first):
            cur, sh1, sh2 = _rows_before(ref, r0, first)
            return ((b_ref[...] + w_ref[0:1, :] * sh2) + w_ref[1:2, :] * sh1) + w_ref[2:3, :] * cur

        def chunk(r0, first):
            gate = conv(pg_ref, wg_ref, bg_ref, r0, first)
            up = conv(pu_ref, wu_ref, bu_ref, r0, first)
            a_ref[pl.ds(r0, FFN_CH), :] = (gate * _sigmoid(gate) * up).astype(BF16)

        chunk(0, True)

        def step(c, carry):
            chunk(pl.multiple_of(c * FFN_CH, FFN_CH), False)
            return carry

        lax.fori_loop(1, nch, step, 0)

    col = lambda off: pl.BlockSpec((s, FFN_TN), lambda j: (0, j + off))
    wcol = lambda off: pl.BlockSpec((3, FFN_TN), lambda j: (0, j + off))
    bcol = lambda off: pl.BlockSpec((1, FFN_TN), lambda j: (0, j + off))
    return pl.pallas_call(
        body, name=name, grid=(nj,),
        in_specs=[col(0), col(nj), wcol(0), wcol(nj), bcol(0), bcol(nj)],
        out_specs=pl.BlockSpec((s, FFN_TN), lambda j: (0, j)),
        out_shape=jax.ShapeDtypeStruct((s, f), BF16),
        compiler_params=_params(("parallel",), VMEM_LIMIT),
    )(p, p, conv_w, conv_w, conv_b.reshape(1, f2), conv_b.reshape(1, f2))


def ffn_act_bwd(da, p, conv_w, conv_b, *, name):
    s, f2 = p.shape
    f = f2 // 2
    nj = f // FFN_TN
    nch = s // FFN_CH

    def body(da_ref, pg_ref, pu_ref, wg_ref, wu_ref, bg_ref, bu_ref,
             dpg_ref, dpu_ref, dwg_ref, dwu_ref, dbg_ref, dbu_ref, dug_s, duu_s):
        def conv(ref, w_ref, b_ref, r0, first):
            cur, sh1, sh2 = _rows_before(ref, r0, first)
            u = ((b_ref[...] + w_ref[0:1, :] * sh2) + w_ref[1:2, :] * sh1) + w_ref[2:3, :] * cur
            return u, (sh2, sh1, cur)

        def taps_sum(du, taps):
            return jnp.concatenate([jnp.sum(du * t, axis=0, keepdims=True) for t in taps], axis=0)

        def chunk(r0, first, acc):
            dwg, dwu, dbg, dbu = acc
            gate, tg = conv(pg_ref, wg_ref, bg_ref, r0, first)
            up, tu = conv(pu_ref, wu_ref, bu_ref, r0, first)
            dav = da_ref[pl.ds(r0, FFN_CH), :]
            sg = _sigmoid(gate)
            dgate = dav * up * (sg * (1.0 + gate * (1.0 - sg)))
            dup = dav * (gate * sg)
            dug_s[pl.ds(r0, FFN_CH), :] = dgate
            duu_s[pl.ds(r0, FFN_CH), :] = dup
            return (dwg + taps_sum(dgate, tg), dwu + taps_sum(dup, tu),
                    dbg + jnp.sum(dgate, axis=0, keepdims=True), dbu + jnp.sum(dup, axis=0, keepdims=True))

        z3 = jnp.zeros((3, FFN_TN), F32)
        z1 = jnp.zeros((1, FFN_TN), F32)
        acc = chunk(0, True, (z3, z3, z1, z1))
        acc = lax.fori_loop(1, nch, lambda c, a: chunk(pl.multiple_of(c * FFN_CH, FFN_CH), False, a), acc)
        dwg_ref[...], dwu_ref[...], dbg_ref[...], dbu_ref[...] = acc

        def back(src, w_ref, dst, r0, last):
            cur, up1, up2 = _rows_after(src, r0, last)
            dst[pl.ds(r0, FFN_CH), :] = (w_ref[2:3, :] * cur + w_ref[1:2, :] * up1 + w_ref[0:1, :] * up2).astype(BF16)

        def step(c, carry):
            r0 = pl.multiple_of(c * FFN_CH, FFN_CH)
            back(dug_s, wg_ref, dpg_ref, r0, False)
            back(duu_s, wu_ref, dpu_ref, r0, False)
            return carry

        lax.fori_loop(0, nch - 1, step, 0)
        back(dug_s, wg_ref, dpg_ref, (nch - 1) * FFN_CH, True)
        back(duu_s, wu_ref, dpu_ref, (nch - 1) * FFN_CH, True)

    col = lambda off: pl.BlockSpec((s, FFN_TN), lambda j: (0, j + off))
    wcol = lambda off: pl.BlockSpec((3, FFN_TN), lambda j: (0, j + off))
    bcol = lambda off: pl.BlockSpec((1, FFN_TN), lambda j: (0, j + off))
    outs = pl.pallas_call(
        body, name=name, grid=(nj,),
        in_specs=[col(0), col(0), col(nj), wcol(0), wcol(nj), bcol(0), bcol(nj)],
        out_specs=[col(0), col(0), wcol(0), wcol(0), bcol(0), bcol(0)],
        out_shape=[jax.ShapeDtypeStruct((s, f), BF16), jax.ShapeDtypeStruct((s, f), BF16),
                   jax.ShapeDtypeStruct((3, f), F32), jax.ShapeDtypeStruct((3, f), F32),
                   jax.ShapeDtypeStruct((1, f), F32), jax.ShapeDtypeStruct((1, f), F32)],
        scratch_shapes=[pltpu.VMEM((s, FFN_TN), F32), pltpu.VMEM((s, FFN_TN), F32)],
        compiler_params=_params(("parallel",), VMEM_LIMIT),
    )(da, p, p, conv_w, conv_w, conv_b.reshape(1, f2), conv_b.reshape(1, f2))
    dpg, dpu, dwg, dwu, dbg, dbu = outs
    return (jnp.concatenate([dpg, dpu], axis=1), jnp.concatenate([dwg, dwu], axis=1),
            jnp.concatenate([dbg, dbu], axis=1)[0])


def mixnorm_fwd(outs, gain, *, name):
    s = outs[0].shape[0]
    widths = [o.shape[1] for o in outs]
    total = sum(widths)
    tm = 512

    def body(*refs):
        o_refs, g_ref, m_ref = refs[:-2], refs[-2], refs[-1]
        off = 0
        for o_ref, w in zip(o_refs, widths):
            xv = o_ref[...]
            r = lax.rsqrt(jnp.mean(xv * xv, axis=-1, keepdims=True) + EPS)
            m_ref[:, off:off + w] = (xv * r * g_ref[:, off:off + w]).astype(BF16)
            off += w

    return pl.pallas_call(
        body, name=name, grid=(s // tm,),
        in_specs=[pl.BlockSpec((tm, w), lambda i: (i, 0)) for w in widths] + [pl.BlockSpec((1, total), lambda i: (0, 0))],
        out_specs=pl.BlockSpec((tm, total), lambda i: (i, 0)),
        out_shape=jax.ShapeDtypeStruct((s, total), BF16),
        compiler_params=_params(("parallel",)),
    )(*outs, gain.reshape(1, total))


def mixnorm_bwd(dmix, outs, gain, *, name):
    s = outs[0].shape[0]
    widths = [o.shape[1] for o in outs]
    total = sum(widths)
    n = len(outs)
    tm = 256

    def body(*refs):
        dm_ref, o_refs, g_ref = refs[0], refs[1:1 + n], refs[1 + n]
        d_refs, dg_ref = refs[2 + n:2 + 2 * n], refs[2 + 2 * n]

        @pl.when(pl.program_id(0) == 0)
        def _():
            dg_ref[...] = jnp.zeros_like(dg_ref)

        off = 0
        for o_ref, d_ref, w in zip(o_refs, d_refs, widths):
            xv = o_ref[...]
            dhv = dm_ref[:, off:off + w]
            r = lax.rsqrt(jnp.mean(xv * xv, axis=-1, keepdims=True) + EPS)
            gd = dhv * g_ref[:, off:off + w]
            dot = jnp.mean(gd * xv, axis=-1, keepdims=True)
            d_ref[...] = r * gd - xv * (r * r * r * dot)
            dg_ref[:, off:off + w] += jnp.sum(dhv * (xv * r), axis=0, keepdims=True)
            off += w

    res = pl.pallas_call(
        body, name=name, grid=(s // tm,),
        in_specs=[pl.BlockSpec((tm, total), lambda i: (i, 0))]
        + [pl.BlockSpec((tm, w), lambda i: (i, 0)) for w in widths] + [pl.BlockSpec((1, total), lambda i: (0, 0))],
        out_specs=[pl.BlockSpec((tm, w), lambda i: (i, 0)) for w in widths] + [pl.BlockSpec((1, total), lambda i: (0, 0))],
        out_shape=[jax.ShapeDtypeStruct((s, w), F32) for w in widths] + [jax.ShapeDtypeStruct((1, total), F32)],
        compiler_params=_params(("arbitrary",)),
    )(dmix, *outs, gain.reshape(1, total))
    return res[:n], res[n][0]


NORM_CH = 512


def _lo_mask(shape):
    return lax.broadcasted_iota(I32, shape, 1) < HEAD_DIM


def _head_stats(x, lo):
    y = x * x
    s_lo = jnp.sum(jnp.where(lo, y, 0.0), axis=1, keepdims=True)
    s_hi = jnp.sum(jnp.where(lo, 0.0, y), axis=1, keepdims=True)
    return lax.rsqrt(jnp.where(lo, s_lo, s_hi) * (1.0 / HEAD_DIM) + EPS)


def _head_sum(x, lo):
    s_lo = jnp.sum(jnp.where(lo, x, 0.0), axis=1, keepdims=True)
    s_hi = jnp.sum(jnp.where(lo, 0.0, x), axis=1, keepdims=True)
    return jnp.where(lo, s_lo, s_hi)


def _swap_halves(x):
    return pltpu.roll(x, HEAD_DIM, axis=1)


def _replicate_head(x, lo, use_lo_head):
    sw = _swap_halves(x)
    return jnp.where(use_lo_head, jnp.where(lo, x, sw), jnp.where(lo, sw, x))


def _tile_rows(i, s, d):
    nb = s // (BLOCK * d)
    r = i // nb
    b = i % nb
    start = r + (BLOCK * d) * b
    prev = start - (BLOCK * d) * jnp.minimum(b, 1)
    return start, prev, b > 0


def _rows(ref, start, d):
    if d == 1:
        return ref[pl.ds(pl.multiple_of(start, BLOCK), BLOCK), :]
    return ref[pl.ds(start, BLOCK, stride=d), :]


def _set_rows(ref, start, d, val):
    if d == 1:
        ref[pl.ds(pl.multiple_of(start, BLOCK), BLOCK), :] = val
    else:
        ref[pl.ds(start, BLOCK, stride=d), :] = val


def banded_fwd(proj, qb0, kb0, vb0, n_slabs, gq, gk, bias, dils, sinks, gqa, *, name):
    s = proj.shape[0]
    nbr = len(dils)
    nt = s // BLOCK
    nch = s // NORM_CH
    has_sink = sinks is not None

    def body(*refs):
        q_ref, k_ref, v_ref, gq_ref, gk_ref, b_ref = refs[:6]
        rest = refs[6:]
        if has_sink:
            sink_ref, rest = rest[0], rest[1:]
        out_ref, lse_ref, qn_s, kn_s, vv_s, o_s, l_s = rest
        p = pl.program_id(0)
        use_lo = (p // 2) == 0

        def prep(c, carry):
            rows = pl.ds(pl.multiple_of(c * NORM_CH, NORM_CH), NORM_CH)
            lo = _lo_mask((NORM_CH, LANES))
            qv, kv, vv = q_ref[rows, :], k_ref[rows, :], v_ref[rows, :]
            qn_s[rows, :] = qv * _head_stats(qv, lo) * gq_ref[...]
            kn = kv * _head_stats(kv, lo) * gk_ref[...]
            if gqa:
                kn = _replicate_head(kn, lo, use_lo)
                vv = _replicate_head(vv, lo, use_lo)
            kn_s[rows, :] = kn
            vv_s[rows, :] = vv
            return carry

        lax.fori_loop(0, nch, prep, 0)

        lo = _lo_mask((BLOCK, LANES))
        col2 = lax.broadcasted_iota(I32, (BLOCK, 2 * BLOCK), 1)
        for br, d in enumerate(dils):
            def tile(i, carry, br=br, d=d):
                start, prev, has_prev = _tile_rows(i, s, d)
                qt = _rows(qn_s, start, d)
                kcat = jnp.concatenate([_rows(kn_s, prev, d), _rows(kn_s, start, d)], axis=0).astype(BF16)
                vcat = jnp.concatenate([_rows(vv_s, prev, d), _rows(vv_s, start, d)], axis=0).astype(BF16)
                visible = jnp.logical_or(has_prev, col2 >= BLOCK)
                o_h, l_h = [], []
                for h in range(2):
                    qh = jnp.where(lo if h == 0 else jnp.logical_not(lo), qt, 0.0).astype(BF16)
                    sc = _dot_nt(qh, kcat) * (HEAD_DIM ** -0.5) + b_ref[br, h]
                    sc = jnp.where(visible, sc, NEG_INF)
                    m = jnp.max(sc, axis=1, keepdims=True)
                    pe = jnp.exp(sc - m)
                    den = jnp.sum(pe, axis=1, keepdims=True)
                    o_h.append(_dot(pe.astype(BF16), vcat) / den)
                    l_h.append(m + jnp.log(den))
                _set_rows(o_s.at[br], start, d, jnp.where(lo, o_h[0], o_h[1]))
                _set_rows(l_s.at[br], start, d, jnp.where(lo, l_h[0], l_h[1]))
                return carry

            lax.fori_loop(0, nt, tile, 0)

        def combine(c, carry):
            rows = pl.ds(pl.multiple_of(c * NORM_CH, NORM_CH), NORM_CH)
            ls = [l_s[br, rows, :] for br in range(nbr)]
            mx = functools.reduce(jnp.maximum, ls)
            if has_sink:
                mx = jnp.maximum(mx, sink_ref[...])
            tot = functools.reduce(jnp.add, [jnp.exp(l - mx) for l in ls])
            if has_sink:
                tot = tot + jnp.exp(sink_ref[...] - mx)
            lse = mx + jnp.log(tot)
            acc = jnp.exp(ls[0] - lse) * o_s[0, rows, :]
            for br in range(1, nbr):
                acc = acc + jnp.exp(ls[br] - lse) * o_s[br, rows, :]
            out_ref[rows, :] = acc
            lse_ref[rows, :] = lse
            return carry

        lax.fori_loop(0, nch, combine, 0)

    slab = lambda b0, shared: pl.BlockSpec((s, LANES), (lambda p: (0, b0)) if shared else (lambda p: (0, b0 + p)),
                                           pipeline_mode=pl.Buffered(1))
    vec = pl.BlockSpec((1, LANES), lambda p: (0, 0))
    in_specs = [slab(qb0, False), slab(kb0, gqa), slab(vb0, gqa), vec, vec,
                pl.BlockSpec((nbr, 2, BLOCK, 2 * BLOCK), lambda p: (0, p, 0, 0))]
    args = [proj, proj, proj, gq.reshape(1, LANES), gk.reshape(1, LANES), bias]
    if has_sink:
        in_specs.append(pl.BlockSpec((None, 1, LANES), lambda p: (p, 0, 0)))
        args.append(sinks)
    w = LANES * n_slabs
    return pl.pallas_call(
        body, name=name, grid=(n_slabs,),
        in_specs=in_specs,
        out_specs=[pl.BlockSpec((s, LANES), lambda p: (0, p)), pl.BlockSpec((s, LANES), lambda p: (0, p))],
        out_shape=[jax.ShapeDtypeStruct((s, w), F32), jax.ShapeDtypeStruct((s, w), F32)],
        scratch_shapes=[pltpu.VMEM((s, LANES), F32), pltpu.VMEM((s, LANES), F32), pltpu.VMEM((s, LANES), F32),
                        pltpu.VMEM((nbr, s, LANES), F32), pltpu.VMEM((nbr, s, LANES), F32)],
        compiler_params=_params(("parallel",), VMEM_LIMIT),
    )(*args)


def banded_bwd(proj, qb0, kb0, vb0, n_slabs, gq, gk, bias, dils, sinks, gqa, dout, out, lse, *, name):
    s = proj.shape[0]
    nbr = len(dils)
    nt = s // BLOCK
    nch = s // NORM_CH
    has_sink = sinks is not None
    scale = HEAD_DIM ** -0.5

    def body(*refs):
        q_ref, k_ref, v_ref, gq_ref, gk_ref, b_ref, do_ref, o_ref, lse_ref = refs[:9]
        rest = refs[9:]
        if has_sink:
            sink_ref, rest = rest[0], rest[1:]
        dq_ref, dk_ref, dv_ref, db_ref, dgq_ref, dgk_ref = rest[:6]
        rest = rest[6:]
        if has_sink:
            dsink_ref, rest = rest[0], rest[1:]
        qn_s, kn_s, vv_s, dl_s, dqn_s, dkn_s, dvv_s = rest
        p = pl.program_id(0)
        use_lo = (p // 2) == 0

        def prep(c, carry):
            rows = pl.ds(pl.multiple_of(c * NORM_CH, NORM_CH), NORM_CH)
            lo = _lo_mask((NORM_CH, LANES))
            qv, kv, vv = q_ref[rows, :], k_ref[rows, :], v_ref[rows, :]
            qn_s[rows, :] = qv * _head_stats(qv, lo) * gq_ref[...]
            kn = kv * _head_stats(kv, lo) * gk_ref[...]
            if gqa:
                kn = _replicate_head(kn, lo, use_lo)
                vv = _replicate_head(vv, lo, use_lo)
            kn_s[rows, :] = kn
            vv_s[rows, :] = vv
            delta = _head_sum(do_ref[rows, :] * o_ref[rows, :], lo)
            dl_s[rows, :] = delta
            z = jnp.zeros((NORM_CH, LANES), F32)
            dqn_s[rows, :] = z
            dkn_s[rows, :] = z
            dvv_s[rows, :] = z
            if has_sink:
                ps = jnp.exp(sink_ref[...] - lse_ref[rows, :])
                return carry - jnp.sum(ps * delta, axis=0, keepdims=True)
            return carry

        dsink = lax.fori_loop(0, nch, prep, jnp.zeros((1, LANES), F32))
        if has_sink:
            dsink_ref[...] = jnp.broadcast_to(dsink, (8, LANES))

        lo = _lo_mask((BLOCK, LANES))
        col2 = lax.broadcasted_iota(I32, (BLOCK, 2 * BLOCK), 1)
        for br, d in enumerate(dils):
            db_ref[br] = jnp.zeros((2, BLOCK, 2 * BLOCK), F32)

            def tile(i, carry, br=br, d=d):
                start, prev, has_prev = _tile_rows(i, s, d)
                qt = _rows(qn_s, start, d)
                kcat = jnp.concatenate([_rows(kn_s, prev, d), _rows(kn_s, start, d)], axis=0).astype(BF16)
                vcat = jnp.concatenate([_rows(vv_s, prev, d), _rows(vv_s, start, d)], axis=0).astype(BF16)
                dot_ = _rows(do_ref, start, d)
                lse_t = _rows(lse_ref, start, d)
                dl_t = _rows(dl_s, start, d)
                visible = jnp.logical_or(has_prev, col2 >= BLOCK)
                dq_t = jnp.zeros((BLOCK, LANES), F32)
                dk_t = jnp.zeros((2 * BLOCK, LANES), F32)
                dv_t = jnp.zeros((2 * BLOCK, LANES), F32)
                for h in range(2):
                    hm = lo if h == 0 else jnp.logical_not(lo)
                    qh = jnp.where(hm, qt, 0.0).astype(BF16)
                    doh = jnp.where(hm, dot_, 0.0).astype(BF16)
                    lane0 = 0 if h == 0 else HEAD_DIM
                    lse_h = lse_t[:, lane0:lane0 + 1]
                    delta_h = dl_t[:, lane0:lane0 + 1]
                    sc = _dot_nt(qh, kcat) * scale + b_ref[br, h]
                    sc = jnp.where(visible, sc, NEG_INF)
                    pr = jnp.exp(sc - lse_h)
                    dp = _dot_nt(doh, vcat)
                    dlog = pr * (dp - delta_h)
                    db_ref[br, h] += dlog
                    dlb = dlog.astype(BF16)
                    dq_t = dq_t + jnp.where(hm, _dot(dlb, kcat), 0.0)
                    dk_t = dk_t + _dot_tn(dlb, qh)
                    dv_t = dv_t + _dot_tn(pr.astype(BF16), doh)
                _set_rows(dqn_s, start, d, _rows(dqn_s, start, d) + dq_t * scale)
                _set_rows(dkn_s, prev, d, _rows(dkn_s, prev, d) + dk_t[:BLOCK] * scale)
                _set_rows(dkn_s, start, d, _rows(dkn_s, start, d) + dk_t[BLOCK:] * scale)
                _set_rows(dvv_s, prev, d, _rows(dvv_s, prev, d) + dv_t[:BLOCK])
                _set_rows(dvv_s, start, d, _rows(dvv_s, start, d) + dv_t[BLOCK:])
                return carry

            lax.fori_loop(0, nt, tile, 0)

        if gqa:
            @pl.when(p == 0)
            def _():
                dk_ref[...] = jnp.zeros_like(dk_ref)
                dv_ref[...] = jnp.zeros_like(dv_ref)

        def finish(c, carry):
            dgq, dgk = carry
            rows = pl.ds(pl.multiple_of(c * NORM_CH, NORM_CH), NORM_CH)
            lo = _lo_mask((NORM_CH, LANES))

            def norm_bwd(xv, dn, g_ref):
                r = _head_stats(xv, lo)
                gd = dn * g_ref[...]
                dot = _head_sum(gd * xv, lo) * (1.0 / HEAD_DIM)
                return r * gd - xv * (r * r * r * dot), dn * (xv * r)

            dq, gq_part = norm_bwd(q_ref[rows, :], dqn_s[rows, :], gq_ref)
            dq_ref[rows, :] = dq
            dgq = dgq + jnp.sum(gq_part, axis=0, keepdims=True)
            kv, dkn, dvv = k_ref[rows, :], dkn_s[rows, :], dvv_s[rows, :]
            if gqa:
                kv = _replicate_head(kv, lo, use_lo)
                dkn = dkn + _swap_halves(dkn)
                dvv = dvv + _swap_halves(dvv)
                lane = lax.broadcasted_iota(I32, (NORM_CH, LANES), 1)
                mine = (lane // HEAD_DIM) == (p // 2)
                dk, gk_part = norm_bwd(kv, dkn, gk_ref)
                dk_ref[rows, :] += jnp.where(mine, dk, 0.0)
                dv_ref[rows, :] += jnp.where(mine, dvv, 0.0)
                gk_part = jnp.where(lo, gk_part, 0.0)
            else:
                dk, gk_part = norm_bwd(kv, dkn, gk_ref)
                dk_ref[rows, :] = dk
                dv_ref[rows, :] = dvv
            dgk = dgk + jnp.sum(gk_part, axis=0, keepdims=True)
            return dgq, dgk

        z = jnp.zeros((1, LANES), F32)
        dgq, dgk = lax.fori_loop(0, nch, finish, (z, z))
        dgq_ref[...] = jnp.broadcast_to(dgq, (8, LANES))
        dgk_ref[...] = jnp.broadcast_to(dgk, (8, LANES))

    def slab_of(width_blocks, b0, shared):
        return pl.BlockSpec((s, LANES), (lambda p: (0, b0)) if shared else (lambda p: (0, b0 + p)),
                            pipeline_mode=pl.Buffered(1))

    vec = pl.BlockSpec((1, LANES), lambda p: (0, 0))
    own = pl.BlockSpec((s, LANES), lambda p: (0, p), pipeline_mode=pl.Buffered(1))
    in_specs = [slab_of(0, qb0, False), slab_of(0, kb0, gqa), slab_of(0, vb0, gqa), vec, vec,
                pl.BlockSpec((nbr, 2, BLOCK, 2 * BLOCK), lambda p: (0, p, 0, 0)), own, own, own]
    args = [proj, proj, proj, gq.reshape(1, LANES), gk.reshape(1, LANES), bias, dout, out, lse]
    if has_sink:
        in_specs.append(pl.BlockSpec((None, 1, LANES), lambda p: (p, 0, 0)))
        args.append(sinks)
    w = LANES * n_slabs
    kvw = LANES if gqa else w
    kv_spec = pl.BlockSpec((s, LANES), (lambda p: (0, 0)) if gqa else (lambda p: (0, p)))
    part = pl.BlockSpec((None, 8, LANES), lambda p: (p, 0, 0))
    out_specs = [pl.BlockSpec((s, LANES), lambda p: (0, p)), kv_spec, kv_spec,
                 pl.BlockSpec((nbr, 2, BLOCK, 2 * BLOCK), lambda p: (0, p, 0, 0)), part, part]
    out_shape = [jax.ShapeDtypeStruct((s, w), F32), jax.ShapeDtypeStruct((s, kvw), F32),
                 jax.ShapeDtypeStruct((s, kvw), F32),
                 jax.ShapeDtypeStruct((nbr, 2 * n_slabs, BLOCK, 2 * BLOCK), F32),
                 jax.ShapeDtypeStruct((n_slabs, 8, LANES), F32), jax.ShapeDtypeStruct((n_slabs, 8, LANES), F32)]
    if has_sink:
        out_specs.append(part)
        out_shape.append(jax.ShapeDtypeStruct((n_slabs, 8, LANES), F32))
    res = pl.pallas_call(
        body, name=name, grid=(n_slabs,),
        in_specs=in_specs, out_specs=out_specs, out_shape=out_shape,
        scratch_shapes=[pltpu.VMEM((s, LANES), F32) for _ in range(7)],
        compiler_params=_params(("arbitrary",), VMEM_LIMIT),
    )(*args)
    dq, dk, dv, db, dgq, dgk = res[:6]
    outs = [dq, dk, dv, db, dgq[:, 0, :], dgk[:, 0, :]]
    if has_sink:
        outs.append(res[6][:, 0, :])
    return outs


def bias_bwd(dbias, buckets, *, name):
    nbr, h = dbias.shape[:2]

    def body(db_ref, bk_ref, o_ref):
        lane = lax.broadcasted_iota(I32, (1, LANES), 1)
        acc = jnp.zeros((1, LANES), F32)
        for b in range(N_BUCKETS):
            tot = jnp.zeros((1, 1), F32)
            for br in range(nbr):
                sel = jnp.where(bk_ref[br] == b, db_ref[br], 0.0)
                tot = tot + jnp.sum(jnp.sum(sel, axis=0, keepdims=True), axis=1, keepdims=True)
            acc = jnp.where(lane == b, tot, acc)
        o_ref[...] = jnp.broadcast_to(acc, (8, LANES))

    res = pl.pallas_call(
        body, name=name, grid=(h,),
        in_specs=[pl.BlockSpec((nbr, None, BLOCK, 2 * BLOCK), lambda i: (0, i, 0, 0)),
                  pl.BlockSpec((nbr, BLOCK, 2 * BLOCK), lambda i: (0, 0, 0))],
        out_specs=pl.BlockSpec((None, 8, LANES), lambda i: (i, 0, 0)),
        out_shape=jax.ShapeDtypeStruct((h, 8, LANES), F32),
        compiler_params=_params(("parallel",)),
    )(dbias, buckets)
    return res[:, 0, :N_BUCKETS].T


def _softplus(z):
    return jnp.maximum(z, 0.0) + jnp.log(1.0 + jnp.exp(-jnp.abs(z)))


def _split_dot(x, t):
    hi = x.astype(BF16)
    lo = (x - hi.astype(F32)).astype(BF16)
    return _dot(hi, t) + _dot(lo, t)


def sb_fwd(proj, qb0, kb0, vb0, n_slabs, *, name):
    s = proj.shape[0]
    nq = s // BLOCK
    scale = HEAD_DIM ** -0.5

    def body(q_ref, k_ref, v_ref, o_ref, tot_ref):
        row = lax.broadcasted_iota(I32, (BLOCK, BLOCK), 0)
        col = lax.broadcasted_iota(I32, (BLOCK, BLOCK), 1)
        lo = col < HEAD_DIM
        t_ge = (row >= col).astype(BF16)

        def qloop(qi, carry):
            q0 = pl.multiple_of(qi * BLOCK, BLOCK)
            qt = q_ref[pl.ds(q0, BLOCK), :]
            qh = [jnp.where(lo, qt, 0.0).astype(BF16), jnp.where(lo, 0.0, qt).astype(BF16)]

            def kloop(t, st):
                kj = qi - t
                k0 = pl.multiple_of(kj * BLOCK, BLOCK)
                kt = k_ref[pl.ds(k0, BLOCK), :].astype(BF16)
                vt = v_ref[pl.ds(k0, BLOCK), :].astype(BF16)
                causal = (k0 + col) < (q0 + row)
                new = []
                for h in range(2):
                    c, o = st[2 * h], st[2 * h + 1]
                    z = _dot_nt(qh[h], kt) * scale
                    sp = _softplus(z)
                    lrem = jnp.where(causal, -sp, 0.0)
                    incl = _split_dot(lrem, t_ge)
                    a = jnp.where(causal, jnp.exp(z - sp + (c + incl - lrem)), 0.0)
                    new += [c + incl[:, 0:1], o + _dot(a.astype(BF16), vt)]
                return tuple(new)

            zc = jnp.zeros((BLOCK, 1), F32)
            zo = jnp.zeros((BLOCK, LANES), F32)
            c0, o0, c1, o1 = lax.fori_loop(0, qi + 1, kloop, (zc, zo, zc, zo))
            o_ref[pl.ds(q0, BLOCK), :] = jnp.where(lo, o0, o1)
            tot_ref[pl.ds(q0, BLOCK), :] = jnp.where(lo, c0, c1)
            return carry

        lax.fori_loop(0, nq, qloop, 0)

    slab = lambda b0: pl.BlockSpec((s, LANES), lambda p: (0, b0 + p), pipeline_mode=pl.Buffered(1))
    w = LANES * n_slabs
    return pl.pallas_call(
        body, name=name, grid=(n_slabs,),
        in_specs=[slab(qb0), slab(kb0), slab(vb0)],
        out_specs=[pl.BlockSpec((s, LANES), lambda p: (0, p)), pl.BlockSpec((s, LANES), lambda p: (0, p))],
        out_shape=[jax.ShapeDtypeStruct((s, w), F32), jax.ShapeDtypeStruct((s, w), F32)],
        compiler_params=_params(("parallel",), VMEM_LIMIT),
    )(proj, proj, proj)


def sb_bwd(proj, qb0, kb0, vb0, n_slabs, dout, tot, *, name):
    s = proj.shape[0]
    nq = s // BLOCK
    scale = HEAD_DIM ** -0.5

    def body(q_ref, k_ref, v_ref, do_ref, tot_ref, dq_ref, dk_ref, dv_ref):
        row = lax.broadcasted_iota(I32, (BLOCK, BLOCK), 0)
        col = lax.broadcasted_iota(I32, (BLOCK, BLOCK), 1)
        lo = col < HEAD_DIM
        t_le = (row <= col).astype(BF16)
        dk_ref[...] = jnp.zeros_like(dk_ref)
        dv_ref[...] = jnp.zeros_like(dv_ref)

        def qloop(qi, carry):
            q0 = pl.multiple_of(qi * BLOCK, BLOCK)
            qt = q_ref[pl.ds(q0, BLOCK), :]
            dot_ = do_ref[pl.ds(q0, BLOCK), :]
            tot_t = tot_ref[pl.ds(q0, BLOCK), :]
            hms = [lo, jnp.logical_not(lo)]
            qh = [jnp.where(m, qt, 0.0).astype(BF16) for m in hms]
            doh = [jnp.where(m, dot_, 0.0).astype(BF16) for m in hms]
            tots = [tot_t[:, 0:1], tot_t[:, HEAD_DIM:HEAD_DIM + 1]]

            def kloop(kj, st):
                k0 = pl.multiple_of(kj * BLOCK, BLOCK)
                kt = k_ref[pl.ds(k0, BLOCK), :].astype(BF16)
                vt = v_ref[pl.ds(k0, BLOCK), :].astype(BF16)
                causal = (k0 + col) < (q0 + row)
                dq = st[4]
                dk_t = jnp.zeros((BLOCK, LANES), F32)
                dv_t = jnp.zeros((BLOCK, LANES), F32)
                new = []
                for h in range(2):
                    cp, cg = st[2 * h], st[2 * h + 1]
                    z = _dot_nt(qh[h], kt) * scale
                    sp = _softplus(z)
                    lrem = jnp.where(causal, -sp, 0.0)
                    pre = _split_dot(lrem, t_le)
                    suffix = tots[h] - cp - pre
                    a = jnp.where(causal, jnp.exp(z - sp + suffix), 0.0)
                    beta = jnp.exp(z - sp)
                    g = a * _dot_nt(doh[h], vt)
                    gpre = _split_dot(g, t_le)
                    big_g = cg + gpre - g
                    dz = jnp.where(causal, g * (1.0 - beta) - beta * big_g, 0.0).astype(BF16)
                    dq = dq + jnp.where(hms[h], _dot(dz, kt), 0.0)
                    dk_t = dk_t + _dot_tn(dz, qh[h])
                    dv_t = dv_t + _dot_tn(a.astype(BF16), doh[h])
                    new += [cp + pre[:, BLOCK - 1:BLOCK], cg + gpre[:, BLOCK - 1:BLOCK]]
                dk_ref[pl.ds(k0, BLOCK), :] += dk_t * scale
                dv_ref[pl.ds(k0, BLOCK), :] += dv_t
                return tuple(new) + (dq,)

            zc = jnp.zeros((BLOCK, 1), F32)
            st = lax.fori_loop(0, qi + 1, kloop, (zc, zc, zc, zc, jnp.zeros((BLOCK, LANES), F32)))
            dq_ref[pl.ds(q0, BLOCK), :] = st[4] * scale
            return carry

        lax.fori_loop(0, nq, qloop, 0)

    slab = lambda b0: pl.BlockSpec((s, LANES), lambda p: (0, b0 + p), pipeline_mode=pl.Buffered(1))
    own = pl.BlockSpec((s, LANES), lambda p: (0, p), pipeline_mode=pl.Buffered(1))
    w = LANES * n_slabs
    outb = pl.BlockSpec((s, LANES), lambda p: (0, p))
    return pl.pallas_call(
        body, name=name, grid=(n_slabs,),
        in_specs=[slab(qb0), slab(kb0), slab(vb0), own, own],
        out_specs=[outb, outb, outb],
        out_shape=[jax.ShapeDtypeStruct((s, w), F32)] * 3,
        compiler_params=_params(("parallel",), VMEM_LIMIT),
    )(proj, proj, proj, dout, tot)


def _place():
    x, y, c = lax.axis_index("x"), lax.axis_index("y"), lax.axis_index("c")
    return x, y, c


def gather_blocks(shards, *, name):
    n = len(shards)

    def body(*refs):
        in_refs, out_refs = refs[:n], refs[n:2 * n]
        send_sems, recv_sems, local_sems = refs[2 * n:]
        x, y, c = _place()
        me, sibling = (x, y, c), (x, y, 1 - c)
        chips = [(1 - x, y), (x, 1 - y), (1 - x, 1 - y)]

        def blk(a, place):
            return out_refs[a].at[4 * place[0] + 2 * place[1] + place[2]]

        def copy(a, k, block, to, src=None):
            return pltpu.make_async_remote_copy(
                src_ref=blk(a, block) if src is None else src, dst_ref=blk(a, block),
                send_sem=send_sems.at[a, k], recv_sem=recv_sems.at[a, k], device_id=to, device_id_type=MESH)

        mine = [pltpu.make_async_copy(in_refs[a], blk(a, me), local_sems.at[a]) for a in range(n)]
        for cp in mine:
            cp.start()
        first = []
        for a in range(n):
            first.append(copy(a, 0, me, sibling, src=in_refs[a]))
            first += [copy(a, 1 + j, me, (*chip, c), src=in_refs[a]) for j, chip in enumerate(chips)]
        for cp in first:
            cp.start()
        passed = []
        for j, chip in enumerate(chips):
            for a in range(n):
                copy(a, 1 + j, (*chip, c), me).wait_recv()
                fw = copy(a, 4 + j, (*chip, c), sibling)
                fw.start()
                passed.append(fw)
        for a in range(n):
            copy(a, 0, sibling, me).wait_recv()
            for j, chip in enumerate(chips):
                copy(a, 4 + j, (*chip, 1 - c), me).wait_recv()
        for cp in first + passed:
            cp.wait_send()
        for cp in mine:
            cp.wait()

    any_spec = pl.BlockSpec(memory_space=pl.ANY)
    return pl.pallas_call(
        body, name=name,
        in_specs=[any_spec] * n, out_specs=[any_spec] * n,
        out_shape=[jax.ShapeDtypeStruct((N_DEV,) + sh.shape, sh.dtype) for sh in shards],
        scratch_shapes=[pltpu.SemaphoreType.DMA((n, 7)), pltpu.SemaphoreType.DMA((n, 7)),
                        pltpu.SemaphoreType.DMA((n,))],
    )(*shards)


def gather_small(v, *, name):
    m_per, n = v.shape

    def body(x_ref, out_ref, send_sems, recv_sems, local_sem):
        x, y, c = _place()
        me, sibling = (x, y, c), (x, y, 1 - c)
        chips = [(1 - x, y), (x, 1 - y), (1 - x, 1 - y)]

        def rows(px, py, pc):
            return out_ref.at[pl.ds((4 * px + 2 * py + pc) * m_per, m_per), :]

        def copy(k, block, to, src=None):
            return pltpu.make_async_remote_copy(
                src_ref=rows(*block) if src is None else src, dst_ref=rows(*block),
                send_sem=send_sems.at[k], recv_sem=recv_sems.at[k], device_id=to, device_id_type=MESH)

        mine = pltpu.make_async_copy(x_ref, rows(*me), local_sem)
        mine.start()
        first = [copy(0, me, sibling, src=x_ref)]
        first += [copy(1 + j, me, (*chip, c), src=x_ref) for j, chip in enumerate(chips)]
        for cp in first:
            cp.start()
        passed = [copy(4 + j, (*chip, c), sibling) for j, chip in enumerate(chips)]
        for j, chip in enumerate(chips):
            copy(1 + j, (*chip, c), me).wait_recv()
            passed[j].start()
        copy(0, sibling, me).wait_recv()
        for j, chip in enumerate(chips):
            copy(4 + j, (*chip, 1 - c), me).wait_recv()
        for cp in first + passed:
            cp.wait_send()
        mine.wait()

    return pl.pallas_call(
        body, name=name,
        out_shape=jax.ShapeDtypeStruct((N_DEV * m_per, n), v.dtype),
        in_specs=[pl.BlockSpec(memory_space=pltpu.VMEM)],
        out_specs=pl.BlockSpec(memory_space=pltpu.VMEM),
        scratch_shapes=[pltpu.SemaphoreType.DMA((7,)), pltpu.SemaphoreType.DMA((7,)), pltpu.SemaphoreType.DMA],
        compiler_params=_params(None, VMEM_LIMIT),
    )(v)


def swap_core_halves(parts, *, name):
    n = len(parts)

    def body(*refs):
        in_refs, out_refs = refs[:n], refs[n:2 * n]
        send_sems, recv_sems = refs[2 * n:]
        x, y, c = _place()
        cps = []
        for a in range(n):
            for q in range(4):
                cps.append(pltpu.make_async_remote_copy(
                    src_ref=in_refs[a].at[2 * q + (1 - c)], dst_ref=out_refs[a].at[q],
                    send_sem=send_sems.at[a, q], recv_sem=recv_sems.at[a, q],
                    device_id=(x, y, 1 - c), device_id_type=MESH))
        for cp in cps:
            cp.start()
        for cp in cps:
            cp.wait()

    any_spec = pl.BlockSpec(memory_space=pl.ANY)
    return pl.pallas_call(
        body, name=name,
        in_specs=[any_spec] * n, out_specs=[any_spec] * n,
        out_shape=[jax.ShapeDtypeStruct((4,) + p.shape[1:], p.dtype) for p in parts],
        scratch_shapes=[pltpu.SemaphoreType.DMA((n, 4)), pltpu.SemaphoreType.DMA((n, 4))],
    )(*parts)


def add_core_halves(part, recv, *, name):
    _, r, cdim = part.shape
    tr = _pick(r, (512, 256, 128, 64, 32, 16))
    c_idx = lax.axis_index("c").astype(I32).reshape(1)

    def body(c_ref, p_ref, r_ref, o_ref):
        o_ref[...] = (p_ref[...].astype(F32) + r_ref[...].astype(F32)).astype(BF16)

    return pl.pallas_call(
        body, name=name,
        grid_spec=pltpu.PrefetchScalarGridSpec(
            num_scalar_prefetch=1, grid=(4, r // tr),
            in_specs=[pl.BlockSpec((None, tr, cdim), lambda q, i, c_ref: (2 * q + c_ref[0], i, 0)),
                      pl.BlockSpec((None, tr, cdim), lambda q, i, c_ref: (q, i, 0))],
            out_specs=pl.BlockSpec((None, tr, cdim), lambda q, i, c_ref: (q, i, 0))),
        out_shape=jax.ShapeDtypeStruct((4, r, cdim), BF16),
        compiler_params=_params(("parallel", "parallel")),
    )(c_idx, part, recv)


def exchange_chips(sums, *, name):
    n = len(sums)

    def body(*refs):
        in_refs, out_refs = refs[:n], refs[n:2 * n]
        send_sems, recv_sems, local_sems = refs[2 * n:]
        x, y, c = _place()
        my_chip = 2 * x + y
        chips = [(1 - x, y), (x, 1 - y), (1 - x, 1 - y)]
        cps = []
        mine = []
        for a in range(n):
            mine.append(pltpu.make_async_copy(in_refs[a].at[my_chip], out_refs[a].at[my_chip], local_sems.at[a]))
            for j, (qx, qy) in enumerate(chips):
                cps.append(pltpu.make_async_remote_copy(
                    src_ref=in_refs[a].at[2 * qx + qy], dst_ref=out_refs[a].at[my_chip],
                    send_sem=send_sems.at[a, j], recv_sem=recv_sems.at[a, j],
                    device_id=(qx, qy, c), device_id_type=MESH))
        for cp in mine + cps:
            cp.start()
        for cp in cps:
            cp.wait()
        for cp in mine:
            cp.wait()

    any_spec = pl.BlockSpec(memory_space=pl.ANY)
    return pl.pallas_call(
        body, name=name,
        in_specs=[any_spec] * n, out_specs=[any_spec] * n,
        out_shape=[jax.ShapeDtypeStruct(p.shape, p.dtype) for p in sums],
        scratch_shapes=[pltpu.SemaphoreType.DMA((n, 3)), pltpu.SemaphoreType.DMA((n, 3)),
                        pltpu.SemaphoreType.DMA((n,))],
    )(*sums)


def _adamw_math(w, g, m, v):
    m = ADAM_B1 * m + (1.0 - ADAM_B1) * g
    v = ADAM_B2 * v + (1.0 - ADAM_B2) * (g * g)
    m_hat = m / (1.0 - ADAM_B1 ** ADAM_STEP)
    v_hat = v / (1.0 - ADAM_B2 ** ADAM_STEP)
    delta = -ADAM_LR * (m_hat / (jnp.sqrt(v_hat) + ADAM_EPS) + ADAM_WD * w)
    return delta, m, v


def adamw_parts(parts, w, m, v, *, name):
    r, cdim = w.shape
    tr = _pick(r, [t for t in (512, 256, 128, 64, 32, 16) if t * cdim <= 512 * 1024])

    def body(p_ref, w_ref, m_ref, v_ref, g_ref, d_ref, nm_ref, nv_ref):
        g = p_ref[0].astype(F32)
        for q in range(1, 4):
            g = g + p_ref[q].astype(F32)
        delta, nm, nv = _adamw_math(w_ref[...], g, m_ref[...], v_ref[...])
        g_ref[...], d_ref[...], nm_ref[...], nv_ref[...] = g, delta, nm, nv

    t = pl.BlockSpec((tr, cdim), lambda i: (i, 0))
    return pl.pallas_call(
        body, name=name, grid=(r // tr,),
        in_specs=[pl.BlockSpec((4, tr, cdim), lambda i: (0, i, 0)), t, t, t],
        out_specs=[t, t, t, t],
        out_shape=[jax.ShapeDtypeStruct((r, cdim), F32)] * 4,
        compiler_params=_params(("parallel",)),
    )(parts, w, m, v)


def sum_devices(gathered, *, name):
    m_rows = gathered.shape[1]

    def body(ga_ref, g_ref):
        g = ga_ref[0]
        for dev in range(1, N_DEV):
            g = g + ga_ref[dev]
        g_ref[...] = g

    return pl.pallas_call(
        body, name=name, out_shape=jax.ShapeDtypeStruct((m_rows, LANES), F32),
        compiler_params=_params(None, VMEM_LIMIT),
    )(gathered)


def adamw_small(g, w, m, v, *, name):
    m_rows = w.shape[0]

    def body(g_ref, w_ref, m_ref, v_ref, d_ref, nm_ref, nv_ref):
        d_ref[...], nm_ref[...], nv_ref[...] = _adamw_math(w_ref[...], g_ref[...], m_ref[...], v_ref[...])

    return pl.pallas_call(
        body, name=name, out_shape=[jax.ShapeDtypeStruct((m_rows, LANES), F32)] * 3,
        compiler_params=_params(None, VMEM_LIMIT),
    )(g, w, m, v)


def _t5_bucket(dist):
    max_exact = N_BUCKETS // 2
    d = jnp.maximum(dist, 0)
    large = max_exact + (jnp.log(jnp.maximum(d, 1).astype(F32) / max_exact)
                         / math.log(T5_MAX_DIST / max_exact) * (N_BUCKETS - max_exact)).astype(I32)
    large = jnp.minimum(large, N_BUCKETS - 1)
    return jnp.where(d < max_exact, d, large)


def _rel():
    return jnp.arange(BLOCK)[:, None] + BLOCK - jnp.arange(2 * BLOCK)[None, :]


def _band_bias(table, dils, max_dists):
    rel = _rel()
    biases, buckets = [], []
    for d, md in zip(dils, max_dists):
        bk = _t5_bucket(rel * d)
        vis = (rel >= 0) & (rel <= md)
        biases.append(jnp.where(vis[None], jnp.transpose(table[bk], (2, 0, 1)).astype(F32), NEG_INF))
        buckets.append(bk.astype(I32))
    return jnp.stack(biases), jnp.stack(buckets)


def _pack(pieces, rows):
    flat = jnp.concatenate([p.reshape(-1) for p in pieces])
    return jnp.pad(flat, (0, rows * LANES - flat.shape[0])).reshape(rows, LANES)


def _unpack(packed, shapes):
    flat = packed.reshape(-1)
    out, off = [], 0
    for sh in shapes:
        n = math.prod(sh)
        out.append(flat[off:off + n].reshape(sh))
        off += n
    return out


def _tile2(g):
    return jnp.concatenate([g, g])


def kernel(x, attn_norm, w_in, a_q_gain, a_k_gain, a_sinks, c_q_gain, c_k_gain, rel_bias_table, mix_out_gain, w_out, ffn_norm, w_up, conv_w, conv_b, w_down, loss_target, m_attn_norm, m_w_in, m_a_q_gain, m_a_k_gain, m_a_sinks, m_c_q_gain, m_c_k_gain, m_rel_bias_table, m_mix_out_gain, m_w_out, m_ffn_norm, m_w_up, m_conv_w, m_conv_b, m_w_down, v_attn_norm, v_w_in, v_a_q_gain, v_a_k_gain, v_a_sinks, v_c_q_gain, v_c_k_gain, v_rel_bias_table, v_mix_out_gain, v_w_out, v_ffn_norm, v_w_up, v_conv_w, v_conv_b, v_w_down):
    depth, d_model, in_shard = w_in.shape
    ff2_shard = w_up.shape[2]
    s = x.shape[1]
    in_width, ff2 = N_DEV * in_shard, N_DEV * ff2_shard
    n_heads = d_model // HEAD_DIM
    ha, hb, hc = n_heads // 4, n_heads // 4, n_heads // 2
    sa, sb, sc = ha // 2, hb // 2, hc // 2
    kv_a = ha // 4
    assert kv_a == 2 and BLOCK == LANES
    cb_aq, cb_ak, cb_av = 0, sa, sa + 1
    cb_bq = sa + 2
    cb_bk, cb_bv = cb_bq + sb, cb_bq + 2 * sb
    cb_cq = cb_bq + 3 * sb
    cb_ck, cb_cv = cb_cq + sc, cb_cq + 2 * sc
    assert (cb_cv + sc) * LANES == in_width
    dev = 4 * lax.axis_index("x") + 2 * lax.axis_index("y") + lax.axis_index("c")

    g_in, g_out, g_up, g_down, g_cw = gather_blocks(
        [w_in.astype(BF16), w_out.astype(BF16), w_up.astype(BF16), w_down.astype(BF16), conv_w],
        name="gather_weights")
    wi = jnp.transpose(g_in, (1, 2, 0, 3)).reshape(depth, d_model, in_width)
    wo = jnp.transpose(g_out, (1, 0, 2, 3)).reshape(depth, d_model, d_model)
    wu = jnp.transpose(g_up, (1, 2, 0, 3)).reshape(depth, d_model, ff2)
    wd = jnp.transpose(g_down, (1, 0, 2, 3)).reshape(depth, ff2 // 2, d_model)
    cw = jnp.transpose(g_cw, (1, 2, 0, 3)).reshape(depth, 3, ff2)

    bias_a, buckets_a = _band_bias(rel_bias_table[:, :ha], (1,), (WINDOW_A - 1,))
    bias_c, buckets_c = _band_bias(rel_bias_table[:, ha:], DILATIONS, (BLOCK,) * len(DILATIONS))

    xs = x[0]
    saved = []
    for l in range(depth):
        h1 = rmsnorm_fwd(xs, attn_norm[l], name="attn_norm_fwd")
        proj = matmul(h1, wi[l], name="in_proj")
        sinks = jnp.repeat(a_sinks[l], HEAD_DIM).reshape(sa, 1, LANES)
        gaq, gak = _tile2(a_q_gain[l]), _tile2(a_k_gain[l])
        gcq, gck = _tile2(c_q_gain[l]), _tile2(c_k_gain[l])
        out_a, lse_a = banded_fwd(proj, cb_aq, cb_ak, cb_av, sa, gaq, gak, bias_a, (1,), sinks, True, name="swa_fwd")
        out_b, tot_b = sb_fwd(proj, cb_bq, cb_bk, cb_bv, sb, name="stick_fwd")
        out_c, lse_c = banded_fwd(proj, cb_cq, cb_ck, cb_cv, sc, gcq, gck, bias_c, DILATIONS, None, False,
                                  name="dilated_fwd")
        mix = mixnorm_fwd([out_a, out_b, out_c], mix_out_gain[l], name="mix_norm_fwd")
        x_mid = matmul(mix, wo[l], res=xs, name="out_proj")
        h2 = rmsnorm_fwd(x_mid, ffn_norm[l], name="ffn_norm_fwd")
        p = matmul(h2, wu[l], name="up_proj")
        act = ffn_act_fwd(p, cw[l], conv_b[l], name="ffn_act_fwd")
        x_out = matmul(act, wd[l], res=x_mid, name="down_proj")
        saved.append(dict(x_in=xs, h1t=h1.T, proj=proj, out_a=out_a, lse_a=lse_a, out_b=out_b, tot_b=tot_b,
                          out_c=out_c, lse_c=lse_c, mixt=mix.T, x_mid=x_mid, h2t=h2.T, p=p, actt=act.T,
                          sinks=sinks, gains=(gaq, gak, gcq, gck)))
        xs = x_out

    dx, loss_part = loss_head(xs, loss_target[0], name="loss_head")

    small = {k: [None] * depth for k in ("attn_norm", "a_q_gain", "a_k_gain", "a_sinks", "c_q_gain", "c_k_gain",
                                         "mix_out_gain", "ffn_norm", "conv_w", "conv_b")}
    big = {k: [None] * depth for k in ("w_in", "w_out", "w_up", "w_down")}
    dbias_a = dbias_c = None
    for l in reversed(range(depth)):
        sv = saved[l]
        gaq, gak, gcq, gck = sv["gains"]
        da = matmul(dx, wd[l], trans_b=True, name="down_proj_dx")
        big["w_down"][l] = matmul(sv["actt"], dx, out_dtype=BF16, name="down_proj_dw")
        dp, small["conv_w"][l], small["conv_b"][l] = ffn_act_bwd(da, sv["p"], cw[l], conv_b[l], name="ffn_act_bwd")
        dh2 = matmul(dp, wu[l], trans_b=True, name="up_proj_dx")
        big["w_up"][l] = matmul(sv["h2t"], dp, out_dtype=BF16, name="up_proj_dw")
        dx_mid, small["ffn_norm"][l] = rmsnorm_bwd(dh2, sv["x_mid"], ffn_norm[l], dx, name="ffn_norm_bwd")
        dmix = matmul(dx_mid, wo[l], trans_b=True, name="out_proj_dx")
        big["w_out"][l] = matmul(sv["mixt"], dx_mid, out_dtype=BF16, name="out_proj_dw")
        (d_oa, d_ob, d_oc), small["mix_out_gain"][l] = mixnorm_bwd(
            dmix, [sv["out_a"], sv["out_b"], sv["out_c"]], mix_out_gain[l], name="mix_norm_bwd")
        dq_a, dk_a, dv_a, db_a, dgq_a, dgk_a, dsink = banded_bwd(
            sv["proj"], cb_aq, cb_ak, cb_av, sa, gaq, gak, bias_a, (1,), sv["sinks"], True,
            d_oa, sv["out_a"], sv["lse_a"], name="swa_bwd")
        dq_b, dk_b, dv_b = sb_bwd(sv["proj"], cb_bq, cb_bk, cb_bv, sb, d_ob, sv["tot_b"], name="stick_bwd")
        dq_c, dk_c, dv_c, db_c, dgq_c, dgk_c = banded_bwd(
            sv["proj"], cb_cq, cb_ck, cb_cv, sc, gcq, gck, bias_c, DILATIONS, None, False,
            d_oc, sv["out_c"], sv["lse_c"], name="dilated_bwd")
        fold = lambda g: g.reshape(-1, HEAD_DIM).sum(axis=0)
        small["a_q_gain"][l], small["a_k_gain"][l] = fold(dgq_a), fold(dgk_a)
        small["c_q_gain"][l], small["c_k_gain"][l] = fold(dgq_c), fold(dgk_c)
        small["a_sinks"][l] = dsink[:, ::HEAD_DIM].reshape(-1)
        dbias_a = db_a if dbias_a is None else dbias_a + db_a
        dbias_c = db_c if dbias_c is None else dbias_c + db_c
        dproj = jnp.concatenate([dq_a, dk_a, dv_a, dq_b, dk_b, dv_b, dq_c, dk_c, dv_c], axis=1)
        dh1 = matmul(dproj, wi[l], trans_b=True, name="in_proj_dx")
        big["w_in"][l] = matmul(sv["h1t"], dproj, out_dtype=BF16, name="in_proj_dw")
        dx, small["attn_norm"][l] = rmsnorm_bwd(dh1, sv["x_in"], attn_norm[l], dx_mid, name="attn_norm_bwd")

    dtable = jnp.concatenate([bias_bwd(dbias_a, buckets_a, name="swa_bias_bwd"),
                              bias_bwd(dbias_c, buckets_c, name="dilated_bias_bwd")], axis=1)

    order = ("attn_norm", "a_q_gain", "a_k_gain", "a_sinks", "c_q_gain", "c_k_gain", "rel_bias_table",
             "mix_out_gain", "ffn_norm", "conv_w", "conv_b")
    partial = {k: jnp.stack(v) for k, v in small.items()}
    partial["rel_bias_table"] = dtable
    pieces = [partial[k] for k in order] + [loss_part.reshape(1)]
    n_small = sum(math.prod(pc.shape) for pc in pieces)
    rows = -(-n_small // (8 * LANES)) * 8
    gathered = gather_small(_pack(pieces, rows), name="gather_small_grads")
    summed = _unpack(sum_devices(gathered.reshape(N_DEV, rows, LANES), name="sum_small_grads"),
                     [pc.shape for pc in pieces])
    g_small = dict(zip(order, summed[:-1]))
    loss = summed[-1][0]
    g_small["conv_w"] = lax.dynamic_slice_in_dim(g_small["conv_w"], dev * ff2_shard, ff2_shard, axis=2)

    w_small = dict(attn_norm=attn_norm, a_q_gain=a_q_gain, a_k_gain=a_k_gain, a_sinks=a_sinks, c_q_gain=c_q_gain,
                   c_k_gain=c_k_gain, rel_bias_table=rel_bias_table, mix_out_gain=mix_out_gain, ffn_norm=ffn_norm,
                   conv_w=conv_w, conv_b=conv_b)
    m_small = dict(attn_norm=m_attn_norm, a_q_gain=m_a_q_gain, a_k_gain=m_a_k_gain, a_sinks=m_a_sinks,
                   c_q_gain=m_c_q_gain, c_k_gain=m_c_k_gain, rel_bias_table=m_rel_bias_table,
                   mix_out_gain=m_mix_out_gain, ffn_norm=m_ffn_norm, conv_w=m_conv_w, conv_b=m_conv_b)
    v_small = dict(attn_norm=v_attn_norm, a_q_gain=v_a_q_gain, a_k_gain=v_a_k_gain, a_sinks=v_a_sinks,
                   c_q_gain=v_c_q_gain, c_k_gain=v_c_k_gain, rel_bias_table=v_rel_bias_table,
                   mix_out_gain=v_mix_out_gain, ffn_norm=v_ffn_norm, conv_w=v_conv_w, conv_b=v_conv_b)
    shapes = [w_small[k].shape for k in order]
    n_upd = sum(math.prod(sh) for sh in shapes)
    urows = -(-n_upd // (8 * LANES)) * 8
    packs = [_pack([d[k] for k in order], urows) for d in (g_small, w_small, m_small, v_small)]
    upd = adamw_small(*packs, name="adamw_small")
    delta_s, newm_s, newv_s = [dict(zip(order, _unpack(u, shapes))) for u in upd]

    def by_cols(lst):
        a = jnp.stack(lst)
        dd, kk, nn = a.shape
        return jnp.transpose(a.reshape(dd, kk, N_DEV, nn // N_DEV), (2, 0, 1, 3)).reshape(N_DEV, dd * kk, nn // N_DEV)

    def by_rows(lst):
        a = jnp.stack(lst)
        dd, kk, nn = a.shape
        return jnp.transpose(a.reshape(dd, N_DEV, kk // N_DEV, nn), (1, 0, 2, 3)).reshape(N_DEV, dd * kk // N_DEV, nn)

    names_big = ("w_in", "w_out", "w_up", "w_down")
    parts = [by_cols(big["w_in"]), by_rows(big["w_out"]), by_cols(big["w_up"]), by_rows(big["w_down"])]
    from_core = swap_core_halves(parts, name="grads_to_other_core")
    chip_sums = [add_core_halves(pt, rc, name="grads_add_cores") for pt, rc in zip(parts, from_core)]
    from_chips = exchange_chips(chip_sums, name="grads_to_other_chips")
    w_big = dict(w_in=(w_in, m_w_in, v_w_in), w_out=(w_out, m_w_out, v_w_out), w_up=(w_up, m_w_up, v_w_up),
                 w_down=(w_down, m_w_down, v_w_down))
    g_big, delta_b, newm_b, newv_b = {}, {}, {}, {}
    for k, fc in zip(names_big, from_chips):
        w, m, v = w_big[k]
        flat = lambda t: t.reshape(-1, t.shape[-1])
        res = adamw_parts(fc, flat(w), flat(m), flat(v), name="adamw_large")
        g_big[k], delta_b[k], newm_b[k], newv_b[k] = [r.reshape(w.shape) for r in res]

    all_names = ("attn_norm", "w_in", "a_q_gain", "a_k_gain", "a_sinks", "c_q_gain", "c_k_gain", "rel_bias_table",
                 "mix_out_gain", "w_out", "ffn_norm", "w_up", "conv_w", "conv_b", "w_down")
    pick = lambda sm, bg: [bg[k] if k in bg else sm[k] for k in all_names]
    return (loss, dx[None], *pick(g_small, g_big), *pick(delta_s, delta_b), *pick(newm_s, newm_b),
            *pick(newv_s, newv_b))
```

```python
import functools
import math

import jax
import jax.numpy as jnp
from jax import lax
from jax.experimental import pallas as pl
from jax.experimental.pallas import tpu as pltpu

F32, BF16, I32 = jnp.float32, jnp.bfloat16, jnp.int32
MESH = pl.DeviceIdType.MESH

HEAD_DIM = 64
LANES = 128
BLOCK = 128
EPS = 1e-6
NEG_INF = -1e30
N_BUCKETS = 32
T5_MAX_DIST = 2048
WINDOW_A = 128
DILATIONS = (1, 4, 16)
N_DEV = 8
VMEM_LIMIT = 56 * 1024 * 1024

ADAM_LR, ADAM_B1, ADAM_B2, ADAM_EPS, ADAM_WD, ADAM_STEP = 0.001, 0.9, 0.999, 1e-08, 0.01, 10


def _params(sem=None, vmem=None):
    return pltpu.CompilerParams(dimension_semantics=sem, vmem_limit_bytes=vmem)


def _pick(n, cands):
    for c in cands:
        if n % c == 0:
            return c
    raise ValueError(f"no tile for {n}")


def _dot(a, b):
    return lax.dot_general(a, b, (((1,), (0,)), ((), ())), preferred_element_type=F32)


def _dot_nt(a, b):
    return lax.dot_general(a, b, (((1,), (1,)), ((), ())), preferred_element_type=F32)


def _dot_tn(a, b):
    return lax.dot_general(a, b, (((0,), (0,)), ((), ())), preferred_element_type=F32)


def matmul(a, b, *, trans_b=False, out_dtype=F32, res=None, name):
    m, k = a.shape
    n = b.shape[0] if trans_b else b.shape[1]
    tm = _pick(m, (1024, 512, 256))
    tn = _pick(n, (1024, 768, 512, 256, 128))
    tk = k if k <= 2048 else _pick(k, (1024, 768, 512, 256))
    nk = k // tk
    dn = (((1,), (1,)), ((), ())) if trans_b else (((1,), (0,)), ((), ()))

    def body(*refs):
        if res is None:
            a_ref, b_ref, o_ref, acc = refs
        else:
            a_ref, b_ref, r_ref, o_ref, acc = refs
        kk = pl.program_id(2)

        @pl.when(kk == 0)
        def _():
            acc[...] = jnp.zeros_like(acc)

        acc[...] += lax.dot_general(a_ref[...].astype(BF16), b_ref[...].astype(BF16), dn,
                                    preferred_element_type=F32)

        @pl.when(kk == nk - 1)
        def _():
            r = acc[...]
            if res is not None:
                r = r_ref[...] + r
            o_ref[...] = r.astype(out_dtype)

    b_spec = (pl.BlockSpec((tn, tk), lambda i, j, kk: (j, kk)) if trans_b
              else pl.BlockSpec((tk, tn), lambda i, j, kk: (kk, j)))
    in_specs = [pl.BlockSpec((tm, tk), lambda i, j, kk: (i, kk)), b_spec]
    args = [a, b]
    if res is not None:
        in_specs.append(pl.BlockSpec((tm, tn), lambda i, j, kk: (i, j)))
        args.append(res)
    return pl.pallas_call(
        body, name=name, grid=(m // tm, n // tn, nk),
        in_specs=in_specs, out_specs=pl.BlockSpec((tm, tn), lambda i, j, kk: (i, j)),
        out_shape=jax.ShapeDtypeStruct((m, n), out_dtype),
        scratch_shapes=[pltpu.VMEM((tm, tn), F32)],
        compiler_params=_params(("parallel", "parallel", "arbitrary"), VMEM_LIMIT),
    )(*args)


def rmsnorm_fwd(x, g, *, name):
    s, d = x.shape
    tm = 512

    def body(x_ref, g_ref, o_ref):
        xv = x_ref[...]
        r = lax.rsqrt(jnp.mean(xv * xv, axis=-1, keepdims=True) + EPS)
        o_ref[...] = (xv * r * g_ref[...]).astype(BF16)

    return pl.pallas_call(
        body, name=name, grid=(s // tm,),
        in_specs=[pl.BlockSpec((tm, d), lambda i: (i, 0)), pl.BlockSpec((1, d), lambda i: (0, 0))],
        out_specs=pl.BlockSpec((tm, d), lambda i: (i, 0)),
        out_shape=jax.ShapeDtypeStruct((s, d), BF16),
        compiler_params=_params(("parallel",)),
    )(x, g.reshape(1, d))


def rmsnorm_bwd(dh, x, g, dres, *, name):
    s, d = x.shape
    tm = 256

    def body(dh_ref, x_ref, g_ref, dres_ref, dx_ref, dg_ref):
        @pl.when(pl.program_id(0) == 0)
        def _():
            dg_ref[...] = jnp.zeros_like(dg_ref)

        xv, dhv = x_ref[...], dh_ref[...]
        r = lax.rsqrt(jnp.mean(xv * xv, axis=-1, keepdims=True) + EPS)
        gd = dhv * g_ref[...]
        dot = jnp.mean(gd * xv, axis=-1, keepdims=True)
        dx_ref[...] = dres_ref[...] + (r * gd - xv * (r * r * r * dot))
        dg_ref[...] += jnp.sum(dhv * (xv * r), axis=0, keepdims=True)

    dx, dg = pl.pallas_call(
        body, name=name, grid=(s // tm,),
        in_specs=[pl.BlockSpec((tm, d), lambda i: (i, 0)), pl.BlockSpec((tm, d), lambda i: (i, 0)),
                  pl.BlockSpec((1, d), lambda i: (0, 0)), pl.BlockSpec((tm, d), lambda i: (i, 0))],
        out_specs=[pl.BlockSpec((tm, d), lambda i: (i, 0)), pl.BlockSpec((1, d), lambda i: (0, 0))],
        out_shape=[jax.ShapeDtypeStruct((s, d), F32), jax.ShapeDtypeStruct((1, d), F32)],
        compiler_params=_params(("arbitrary",)),
    )(dh, x, g.reshape(1, d), dres)
    return dx, dg[0]


def loss_head(y, target, *, name):
    s, d = y.shape
    tm = 512

    def body(y_ref, t_ref, dy_ref, l_ref):
        @pl.when(pl.program_id(0) == 0)
        def _():
            l_ref[...] = jnp.zeros_like(l_ref)

        e = y_ref[...] - t_ref[...]
        dy_ref[...] = e / float(d)
        per_tok = jnp.mean(e * e, axis=-1, keepdims=True)
        l_ref[...] += 0.5 * jnp.sum(per_tok, axis=0, keepdims=True)

    dy, l = pl.pallas_call(
        body, name=name, grid=(s // tm,),
        in_specs=[pl.BlockSpec((tm, d), lambda i: (i, 0)), pl.BlockSpec((tm, d), lambda i: (i, 0))],
        out_specs=[pl.BlockSpec((tm, d), lambda i: (i, 0)), pl.BlockSpec((8, LANES), lambda i: (0, 0))],
        out_shape=[jax.ShapeDtypeStruct((s, d), F32), jax.ShapeDtypeStruct((8, LANES), F32)],
        compiler_params=_params(("arbitrary",)),
    )(y, target)
    return dy, l[0, 0]


FFN_TN = 256
FFN_CH = 256


def _rows_before(ref, r0, first):
    if first:
        cur = ref[pl.ds(0, FFN_CH), :]
        row = lax.broadcasted_iota(I32, cur.shape, 0)
        sh1 = jnp.where(row < 1, 0.0, pltpu.roll(cur, 1, axis=0))
        sh2 = jnp.where(row < 2, 0.0, pltpu.roll(cur, 2, axis=0))
        return cur, sh1, sh2
    ext = ref[pl.ds(pl.multiple_of(r0 - 8, 8), FFN_CH + 8), :]
    return ext[8:], pltpu.roll(ext, 1, axis=0)[8:], pltpu.roll(ext, 2, axis=0)[8:]


def _rows_after(ref, r0, last):
    if last:
        cur = ref[pl.ds(r0, FFN_CH), :]
        row = lax.broadcasted_iota(I32, cur.shape, 0)
        up1 = jnp.where(row >= FFN_CH - 1, 0.0, pltpu.roll(cur, FFN_CH - 1, axis=0))
        up2 = jnp.where(row >= FFN_CH - 2, 0.0, pltpu.roll(cur, FFN_CH - 2, axis=0))
        return cur, up1, up2
    n = FFN_CH + 8
    ext = ref[pl.ds(r0, n), :]
    return ext[:FFN_CH], pltpu.roll(ext, n - 1, axis=0)[:FFN_CH], pltpu.roll(ext, n - 2, axis=0)[:FFN_CH]


def _sigmoid(x):
    return 1.0 / (1.0 + jnp.exp(-x))


def ffn_act_fwd(p, conv_w, conv_b, *, name):
    s, f2 = p.shape
    f = f2 // 2
    nj = f // FFN_TN
    nch = s // FFN_CH

    def body(pg_ref, pu_ref, wg_ref, wu_ref, bg_ref, bu_ref, a_ref):
        def conv(ref, w_ref, b_ref, r0, first):
            cur, sh1, sh2 = _rows_before(ref, r0, first)
            return ((b_ref[...] + w_ref[0:1, :] * sh2) + w_ref[1:2, :] * sh1) + w_ref[2:3, :] * cur

        def chunk(r0, first):
            gate = conv(pg_ref, wg_ref, bg_ref, r0, first)
            up = conv(pu_ref, wu_ref, bu_ref, r0, first)
            a_ref[pl.ds(r0, FFN_CH), :] = (gate * _sigmoid(gate) * up).astype(BF16)

        chunk(0, True)

        def step(c, carry):
            chunk(pl.multiple_of(c * FFN_CH, FFN_CH), False)
            return carry

        lax.fori_loop(1, nch, step, 0)

    col = lambda off: pl.BlockSpec((s, FFN_TN), lambda j: (0, j + off))
    wcol = lambda off: pl.BlockSpec((3, FFN_TN), lambda j: (0, j + off))
    bcol = lambda off: pl.BlockSpec((1, FFN_TN), lambda j: (0, j + off))
    return pl.pallas_call(
        body, name=name, grid=(nj,),
        in_specs=[col(0), col(nj), wcol(0), wcol(nj), bcol(0), bcol(nj)],
        out_specs=pl.BlockSpec((s, FFN_TN), lambda j: (0, j)),
        out_shape=jax.ShapeDtypeStruct((s, f), BF16),
        compiler_params=_params(("parallel",), VMEM_LIMIT),
    )(p, p, conv_w, conv_w, conv_b.reshape(1, f2), conv_b.reshape(1, f2))


def ffn_act_bwd(da, p, conv_w, conv_b, *, name):
    s, f2 = p.shape
    f = f2 // 2
    nj = f // FFN_TN
    nch = s // FFN_CH

    def body(da_ref, pg_ref, pu_ref, wg_ref, wu_ref, bg_ref, bu_ref,
             dpg_ref, dpu_ref, dwg_ref, dwu_ref, dbg_ref, dbu_ref, dug_s, duu_s):
        def conv(ref, w_ref, b_ref, r0, first):
            cur, sh1, sh2 = _rows_before(ref, r0, first)
            u = ((b_ref[...] + w_ref[0:1, :] * sh2) + w_ref[1:2, :] * sh1) + w_ref[2:3, :] * cur
            return u, (sh2, sh1, cur)

        def taps_sum(du, taps):
            return jnp.concatenate([jnp.sum(du * t, axis=0, keepdims=True) for t in taps], axis=0)

        def chunk(r0, first, acc):
            dwg, dwu, dbg, dbu = acc
            gate, tg = conv(pg_ref, wg_ref, bg_ref, r0, first)
            up, tu = conv(pu_ref, wu_ref, bu_ref, r0, first)
            dav = da_ref[pl.ds(r0, FFN_CH), :]
            sg = _sigmoid(gate)
            dgate = dav * up * (sg * (1.0 + gate * (1.0 - sg)))
            dup = dav * (gate * sg)
            dug_s[pl.ds(r0, FFN_CH), :] = dgate
            duu_s[pl.ds(r0, FFN_CH), :] = dup
            return (dwg + taps_sum(dgate, tg), dwu + taps_sum(dup, tu),
                    dbg + jnp.sum(dgate, axis=0, keepdims=True), dbu + jnp.sum(dup, axis=0, keepdims=True))

        z3 = jnp.zeros((3, FFN_TN), F32)
        z1 = jnp.zeros((1, FFN_TN), F32)
        acc = chunk(0, True, (z3, z3, z1, z1))
        acc = lax.fori_loop(1, nch, lambda c, a: chunk(pl.multiple_of(c * FFN_CH, FFN_CH), False, a), acc)
        dwg_ref[...], dwu_ref[...], dbg_ref[...], dbu_ref[...] = acc

        def back(src, w_ref, dst, r0, last):
            cur, up1, up2 = _rows_after(src, r0, last)
            dst[pl.ds(r0, FFN_CH), :] = (w_ref[2:3, :] * cur + w_ref[1:2, :] * up1 + w_ref[0:1, :] * up2).astype(BF16)

        def step(c, carry):
            r0 = pl.multiple_of(c * FFN_CH, FFN_CH)
            back(dug_s, wg_ref, dpg_ref, r0, False)
            back(duu_s, wu_ref, dpu_ref, r0, False)
            return carry

        lax.fori_loop(0, nch - 1, step, 0)
        back(dug_s, wg_ref, dpg_ref, (nch - 1) * FFN_CH, True)
        back(duu_s, wu_ref, dpu_ref, (nch - 1) * FFN_CH, True)

    col = lambda off: pl.BlockSpec((s, FFN_TN), lambda j: (0, j + off))
    wcol = lambda off: pl.BlockSpec((3, FFN_TN), lambda j: (0, j + off))
    bcol = lambda off: pl.BlockSpec((1, FFN_TN), lambda j: (0, j + off))
    outs = pl.pallas_call(
        body, name=name, grid=(nj,),
        in_specs=[col(0), col(0), col(nj), wcol(0), wcol(nj), bcol(0), bcol(nj)],
        out_specs=[col(0), col(0), wcol(0), wcol(0), bcol(0), bcol(0)],
        out_shape=[jax.ShapeDtypeStruct((s, f), BF16), jax.ShapeDtypeStruct((s, f), BF16),
                   jax.ShapeDtypeStruct((3, f), F32), jax.ShapeDtypeStruct((3, f), F32),
                   jax.ShapeDtypeStruct((1, f), F32), jax.ShapeDtypeStruct((1, f), F32)],
        scratch_shapes=[pltpu.VMEM((s, FFN_TN), F32), pltpu.VMEM((s, FFN_TN), F32)],
        compiler_params=_params(("parallel",), VMEM_LIMIT),
    )(da, p, p, conv_w, conv_w, conv_b.reshape(1, f2), conv_b.reshape(1, f2))
    dpg, dpu, dwg, dwu, dbg, dbu = outs
    return (jnp.concatenate([dpg, dpu], axis=1), jnp.concatenate([dwg, dwu], axis=1),
            jnp.concatenate([dbg, dbu], axis=1)[0])


def mixnorm_fwd(outs, gain, *, name):
    s = outs[0].shape[0]
    widths = [o.shape[1] for o in outs]
    total = sum(widths)
    tm = 512

    def body(*refs):
        o_refs, g_ref, m_ref = refs[:-2], refs[-2], refs[-1]
        off = 0
        for o_ref, w in zip(o_refs, widths):
            xv = o_ref[...]
            r = lax.rsqrt(jnp.mean(xv * xv, axis=-1, keepdims=True) + EPS)
            m_ref[:, off:off + w] = (xv * r * g_ref[:, off:off + w]).astype(BF16)
            off += w

    return pl.pallas_call(
        body, name=name, grid=(s // tm,),
        in_specs=[pl.BlockSpec((tm, w), lambda i: (i, 0)) for w in widths] + [pl.BlockSpec((1, total), lambda i: (0, 0))],
        out_specs=pl.BlockSpec((tm, total), lambda i: (i, 0)),
        out_shape=jax.ShapeDtypeStruct((s, total), BF16),
        compiler_params=_params(("parallel",)),
    )(*outs, gain.reshape(1, total))


def mixnorm_bwd(dmix, outs, gain, *, name):
    s = outs[0].shape[0]
    widths = [o.shape[1] for o in outs]
    total = sum(widths)
    n = len(outs)
    tm = 256

    def body(*refs):
        dm_ref, o_refs, g_ref = refs[0], refs[1:1 + n], refs[1 + n]
        d_refs, dg_ref = refs[2 + n:2 + 2 * n], refs[2 + 2 * n]

        @pl.when(pl.program_id(0) == 0)
        def _():
            dg_ref[...] = jnp.zeros_like(dg_ref)

        off = 0
        for o_ref, d_ref, w in zip(o_refs, d_refs, widths):
            xv = o_ref[...]
            dhv = dm_ref[:, off:off + w]
            r = lax.rsqrt(jnp.mean(xv * xv, axis=-1, keepdims=True) + EPS)
            gd = dhv * g_ref[:, off:off + w]
            dot = jnp.mean(gd * xv, axis=-1, keepdims=True)
            d_ref[...] = r * gd - xv * (r * r * r * dot)
            dg_ref[:, off:off + w] += jnp.sum(dhv * (xv * r), axis=0, keepdims=True)
            off += w

    res = pl.pallas_call(
        body, name=name, grid=(s // tm,),
        in_specs=[pl.BlockSpec((tm, total), lambda i: (i, 0))]
        + [pl.BlockSpec((tm, w), lambda i: (i, 0)) for w in widths] + [pl.BlockSpec((1, total), lambda i: (0, 0))],
        out_specs=[pl.BlockSpec((tm, w), lambda i: (i, 0)) for w in widths] + [pl.BlockSpec((1, total), lambda i: (0, 0))],
        out_shape=[jax.ShapeDtypeStruct((s, w), F32) for w in widths] + [jax.ShapeDtypeStruct((1, total), F32)],
        compiler_params=_params(("arbitrary",)),
    )(dmix, *outs, gain.reshape(1, total))
    return res[:n], res[n][0]


NORM_CH = 512
FWD_TILES = 4
BWD_TILES = 2


def _lo_mask(shape):
    return lax.broadcasted_iota(I32, shape, 1) < HEAD_DIM


def _head_sum(x, lo):
    del lo
    i = lax.broadcasted_iota(I32, (LANES, LANES), 0) // HEAD_DIM
    j = lax.broadcasted_iota(I32, (LANES, LANES), 1) // HEAD_DIM
    same = (i == j).astype(BF16)
    hi = x.astype(BF16)
    r1 = x - hi.astype(F32)
    mid = r1.astype(BF16)
    low = (r1 - mid.astype(F32)).astype(BF16)
    return _dot(hi, same) + _dot(mid, same) + _dot(low, same)


def _head_stats(x, lo):
    return lax.rsqrt(_head_sum(x * x, lo) * (1.0 / HEAD_DIM) + EPS)


def _swap_halves(x):
    return pltpu.roll(x, HEAD_DIM, axis=1)


def _replicate_head(x, lo, use_lo_head):
    sw = _swap_halves(x)
    return jnp.where(use_lo_head, jnp.where(lo, x, sw), jnp.where(lo, sw, x))


def _tile_rows(i, s, d):
    nb = s // (BLOCK * d)
    r = i // nb
    b = i % nb
    start = r + (BLOCK * d) * b
    prev = start - (BLOCK * d) * jnp.minimum(b, 1)
    return start, prev, b > 0


def _rows(ref, start, d):
    if d == 1:
        return ref[pl.ds(pl.multiple_of(start, BLOCK), BLOCK), :]
    return ref[pl.ds(start, BLOCK, stride=d), :]


def _set_rows(ref, start, d, val):
    if d == 1:
        ref[pl.ds(pl.multiple_of(start, BLOCK), BLOCK), :] = val
    else:
        ref[pl.ds(start, BLOCK, stride=d), :] = val


def banded_fwd(proj, qb0, kb0, vb0, n_slabs, gq, gk, bias, dils, sinks, gqa, *, name):
    s = proj.shape[0]
    nbr = len(dils)
    nt = s // BLOCK
    nch = s // NORM_CH
    has_sink = sinks is not None

    def body(*refs):
        q_ref, k_ref, v_ref, gq_ref, gk_ref, b_ref = refs[:6]
        rest = refs[6:]
        if has_sink:
            sink_ref, rest = rest[0], rest[1:]
        out_ref, lse_ref, qn_s, kn_s, vv_s, o_s, l_s = rest
        p = pl.program_id(0)
        use_lo = (p // 2) == 0

        def prep(c, carry):
            rows = pl.ds(pl.multiple_of(c * NORM_CH, NORM_CH), NORM_CH)
            lo = _lo_mask((NORM_CH, LANES))
            qv, kv, vv = q_ref[rows, :], k_ref[rows, :], v_ref[rows, :]
            qn_s[rows, :] = qv * _head_stats(qv, lo) * gq_ref[...] * (HEAD_DIM ** -0.5)
            kn = kv * _head_stats(kv, lo) * gk_ref[...]
            if gqa:
                kn = _replicate_head(kn, lo, use_lo)
                vv = _replicate_head(vv, lo, use_lo)
            kn_s[rows, :] = kn
            vv_s[rows, :] = vv
            return carry

        lax.fori_loop(0, nch, prep, 0)

        lo = _lo_mask((BLOCK, LANES))
        hms = [lo, jnp.logical_not(lo)]
        heads, tiles = range(2), range(FWD_TILES)
        for br, d in enumerate(dils):
            def step(ii, carry, br=br, d=d):
                pos = [_tile_rows(ii * FWD_TILES + u, s, d) for u in tiles]
                kcat = [jnp.concatenate([_rows(kn_s, pos[u][1], d), _rows(kn_s, pos[u][0], d)], axis=0).astype(BF16)
                        for u in tiles]
                vcat = [jnp.concatenate([_rows(vv_s, pos[u][1], d), _rows(vv_s, pos[u][0], d)], axis=0).astype(BF16)
                        for u in tiles]
                qt = [_rows(qn_s, pos[u][0], d) for u in tiles]
                sc = [[_dot_nt(jnp.where(hms[h], qt[u], 0.0).astype(BF16), kcat[u])
                       + b_ref[br, jnp.where(pos[u][2], 0, 1), h] for h in heads] for u in tiles]
                m = [[jnp.max(sc[u][h], axis=1, keepdims=True) for h in heads] for u in tiles]
                pe = [[jnp.exp(sc[u][h] - m[u][h]) for h in heads] for u in tiles]
                den = [[jnp.sum(pe[u][h], axis=1, keepdims=True) for h in heads] for u in tiles]
                o = [[_dot(pe[u][h].astype(BF16), vcat[u]) * (1.0 / den[u][h]) for h in heads] for u in tiles]
                for u in tiles:
                    _set_rows(o_s.at[br], pos[u][0], d, jnp.where(lo, o[u][0], o[u][1]))
                    _set_rows(l_s.at[br], pos[u][0], d,
                              jnp.where(lo, m[u][0] + jnp.log(den[u][0]), m[u][1] + jnp.log(den[u][1])))
                return carry

            lax.fori_loop(0, nt // FWD_TILES, step, 0)

        def combine(c, carry):
            rows = pl.ds(pl.multiple_of(c * NORM_CH, NORM_CH), NORM_CH)
            ls = [l_s[br, rows, :] for br in range(nbr)]
            mx = functools.reduce(jnp.maximum, ls)
            if has_sink:
                mx = jnp.maximum(mx, sink_ref[...])
            tot = functools.reduce(jnp.add, [jnp.exp(l - mx) for l in ls])
            if has_sink:
                tot = tot + jnp.exp(sink_ref[...] - mx)
            lse = mx + jnp.log(tot)
            acc = jnp.exp(ls[0] - lse) * o_s[0, rows, :]
            for br in range(1, nbr):
                acc = acc + jnp.exp(ls[br] - lse) * o_s[br, rows, :]
            out_ref[rows, :] = acc
            lse_ref[rows, :] = lse
            return carry

        lax.fori_loop(0, nch, combine, 0)

    slab = lambda b0, shared: pl.BlockSpec((s, LANES), (lambda p: (0, b0)) if shared else (lambda p: (0, b0 + p)),
                                           pipeline_mode=pl.Buffered(1))
    vec = pl.BlockSpec((1, LANES), lambda p: (0, 0))
    in_specs = [slab(qb0, False), slab(kb0, gqa), slab(vb0, gqa), vec, vec,
                pl.BlockSpec((nbr, 2, 2, BLOCK, 2 * BLOCK), lambda p: (0, 0, p, 0, 0))]
    args = [proj, proj, proj, gq.reshape(1, LANES), gk.reshape(1, LANES), bias]
    if has_sink:
        in_specs.append(pl.BlockSpec((None, 1, LANES), lambda p: (p, 0, 0)))
        args.append(sinks)
    w = LANES * n_slabs
    return pl.pallas_call(
        body, name=name, grid=(n_slabs,),
        in_specs=in_specs,
        out_specs=[pl.BlockSpec((s, LANES), lambda p: (0, p)), pl.BlockSpec((s, LANES), lambda p: (0, p))],
        out_shape=[jax.ShapeDtypeStruct((s, w), F32), jax.ShapeDtypeStruct((s, w), F32)],
        scratch_shapes=[pltpu.VMEM((s, LANES), F32), pltpu.VMEM((s, LANES), F32), pltpu.VMEM((s, LANES), F32),
                        pltpu.VMEM((nbr, s, LANES), F32), pltpu.VMEM((nbr, s, LANES), F32)],
        compiler_params=_params(("parallel",), VMEM_LIMIT),
    )(*args)


def banded_bwd(proj, qb0, kb0, vb0, n_slabs, gq, gk, bias, dils, sinks, gqa, dout, out, lse, *, name):
    s = proj.shape[0]
    nbr = len(dils)
    nt = s // BLOCK
    nch = s // NORM_CH
    has_sink = sinks is not None
    scale = HEAD_DIM ** -0.5

    def body(*refs):
        q_ref, k_ref, v_ref, gq_ref, gk_ref, b_ref, do_ref, o_ref, lse_ref = refs[:9]
        rest = refs[9:]
        if has_sink:
            sink_ref, rest = rest[0], rest[1:]
        dq_ref, dk_ref, dv_ref, db_ref, dgq_ref, dgk_ref = rest[:6]
        rest = rest[6:]
        if has_sink:
            dsink_ref, rest = rest[0], rest[1:]
        qn_s, kn_s, vv_s, dl_s, dqn_s, dkn_s, dvv_s = rest
        p = pl.program_id(0)
        use_lo = (p // 2) == 0

        def prep(c, carry):
            rows = pl.ds(pl.multiple_of(c * NORM_CH, NORM_CH), NORM_CH)
            lo = _lo_mask((NORM_CH, LANES))
            qv, kv, vv = q_ref[rows, :], k_ref[rows, :], v_ref[rows, :]
            qn_s[rows, :] = qv * _head_stats(qv, lo) * gq_ref[...] * scale
            kn = kv * _head_stats(kv, lo) * gk_ref[...]
            if gqa:
                kn = _replicate_head(kn, lo, use_lo)
                vv = _replicate_head(vv, lo, use_lo)
            kn_s[rows, :] = kn
            vv_s[rows, :] = vv
            delta = _head_sum(do_ref[rows, :] * o_ref[rows, :], lo)
            dl_s[rows, :] = delta
            z = jnp.zeros((NORM_CH, LANES), F32)
            dqn_s[rows, :] = z
            dkn_s[rows, :] = z
            dvv_s[rows, :] = z
            if has_sink:
                ps = jnp.exp(sink_ref[...] - lse_ref[rows, :])
                return carry - jnp.sum(ps * delta, axis=0, keepdims=True)
            return carry

        dsink = lax.fori_loop(0, nch, prep, jnp.zeros((1, LANES), F32))
        if has_sink:
            dsink_ref[...] = jnp.broadcast_to(dsink, (8, LANES))

        lo = _lo_mask((BLOCK, LANES))
        hms = [lo, jnp.logical_not(lo)]
        heads, tiles = range(2), range(BWD_TILES)
        for br, d in enumerate(dils):
            db_ref[br] = jnp.zeros((2, BLOCK, 2 * BLOCK), F32)

            def step(ii, carry, br=br, d=d):
                pos = [_tile_rows(ii * BWD_TILES + u, s, d) for u in tiles]
                kcat = [jnp.concatenate([_rows(kn_s, pos[u][1], d), _rows(kn_s, pos[u][0], d)], axis=0).astype(BF16)
                        for u in tiles]
                vcat = [jnp.concatenate([_rows(vv_s, pos[u][1], d), _rows(vv_s, pos[u][0], d)], axis=0).astype(BF16)
                        for u in tiles]
                qt = [_rows(qn_s, pos[u][0], d) for u in tiles]
                dot_ = [_rows(do_ref, pos[u][0], d) for u in tiles]
                lse_t = [_rows(lse_ref, pos[u][0], d) for u in tiles]
                dl_t = [_rows(dl_s, pos[u][0], d) for u in tiles]
                qh = [[jnp.where(hms[h], qt[u], 0.0).astype(BF16) for h in heads] for u in tiles]
                doh = [[jnp.where(hms[h], dot_[u], 0.0).astype(BF16) for h in heads] for u in tiles]
                sc = [[_dot_nt(qh[u][h], kcat[u]) + b_ref[br, jnp.where(pos[u][2], 0, 1), h] for h in heads]
                      for u in tiles]
                dp = [[_dot_nt(doh[u][h], vcat[u]) for h in heads] for u in tiles]
                lane0 = [0, HEAD_DIM]
                pr = [[jnp.exp(sc[u][h] - lse_t[u][:, lane0[h]:lane0[h] + 1]) for h in heads] for u in tiles]
                dlog = [[pr[u][h] * (dp[u][h] - dl_t[u][:, lane0[h]:lane0[h] + 1]) for h in heads] for u in tiles]
                for h in heads:
                    db_ref[br, h] += functools.reduce(jnp.add, [dlog[u][h] for u in tiles])
                dlb = [[dlog[u][h].astype(BF16) for h in heads] for u in tiles]
                prb = [[pr[u][h].astype(BF16) for h in heads] for u in tiles]
                dq_t = [jnp.where(lo, _dot(dlb[u][0], kcat[u]), _dot(dlb[u][1], kcat[u])) * scale for u in tiles]
                dk_t = [_dot_tn(dlb[u][0], qh[u][0]) + _dot_tn(dlb[u][1], qh[u][1]) for u in tiles]
                dv_t = [_dot_tn(prb[u][0], doh[u][0]) + _dot_tn(prb[u][1], doh[u][1]) for u in tiles]
                for u in tiles:
                    start, prev = pos[u][0], pos[u][1]
                    _set_rows(dqn_s, start, d, _rows(dqn_s, start, d) + dq_t[u])
                    _set_rows(dkn_s, prev, d, _rows(dkn_s, prev, d) + dk_t[u][:BLOCK])
                    _set_rows(dkn_s, start, d, _rows(dkn_s, start, d) + dk_t[u][BLOCK:])
                    _set_rows(dvv_s, prev, d, _rows(dvv_s, prev, d) + dv_t[u][:BLOCK])
                    _set_rows(dvv_s, start, d, _rows(dvv_s, start, d) + dv_t[u][BLOCK:])
                return carry

            lax.fori_loop(0, nt // BWD_TILES, step, 0)

        if gqa:
            @pl.when(p == 0)
            def _():
                dk_ref[...] = jnp.zeros_like(dk_ref)
                dv_ref[...] = jnp.zeros_like(dv_ref)

        def finish(c, carry):
            dgq, dgk = carry
            rows = pl.ds(pl.multiple_of(c * NORM_CH, NORM_CH), NORM_CH)
            lo = _lo_mask((NORM_CH, LANES))

            def norm_bwd(xv, dn, g_ref):
                r = _head_stats(xv, lo)
                gd = dn * g_ref[...]
                dot = _head_sum(gd * xv, lo) * (1.0 / HEAD_DIM)
                return r * gd - xv * (r * r * r * dot), dn * (xv * r)

            dq, gq_part = norm_bwd(q_ref[rows, :], dqn_s[rows, :], gq_ref)
            dq_ref[rows, :] = dq
            dgq = dgq + jnp.sum(gq_part, axis=0, keepdims=True)
            kv, dkn, dvv = k_ref[rows, :], dkn_s[rows, :], dvv_s[rows, :]
            if gqa:
                kv = _replicate_head(kv, lo, use_lo)
                dkn = dkn + _swap_halves(dkn)
                dvv = dvv + _swap_halves(dvv)
                lane = lax.broadcasted_iota(I32, (NORM_CH, LANES), 1)
                mine = (lane // HEAD_DIM) == (p // 2)
                dk, gk_part = norm_bwd(kv, dkn, gk_ref)
                dk_ref[rows, :] += jnp.where(mine, dk, 0.0)
                dv_ref[rows, :] += jnp.where(mine, dvv, 0.0)
                gk_part = jnp.where(lo, gk_part, 0.0)
            else:
                dk, gk_part = norm_bwd(kv, dkn, gk_ref)
                dk_ref[rows, :] = dk
                dv_ref[rows, :] = dvv
            dgk = dgk + jnp.sum(gk_part, axis=0, keepdims=True)
            return dgq, dgk

        z = jnp.zeros((1, LANES), F32)
        dgq, dgk = lax.fori_loop(0, nch, finish, (z, z))
        dgq_ref[...] = jnp.broadcast_to(dgq, (8, LANES))
        dgk_ref[...] = jnp.broadcast_to(dgk, (8, LANES))

    def slab_of(width_blocks, b0, shared):
        return pl.BlockSpec((s, LANES), (lambda p: (0, b0)) if shared else (lambda p: (0, b0 + p)),
                            pipeline_mode=pl.Buffered(1))

    vec = pl.BlockSpec((1, LANES), lambda p: (0, 0))
    own = pl.BlockSpec((s, LANES), lambda p: (0, p), pipeline_mode=pl.Buffered(1))
    in_specs = [slab_of(0, qb0, False), slab_of(0, kb0, gqa), slab_of(0, vb0, gqa), vec, vec,
                pl.BlockSpec((nbr, 2, 2, BLOCK, 2 * BLOCK), lambda p: (0, 0, p, 0, 0)), own, own, own]
    args = [proj, proj, proj, gq.reshape(1, LANES), gk.reshape(1, LANES), bias, dout, out, lse]
    if has_sink:
        in_specs.append(pl.BlockSpec((None, 1, LANES), lambda p: (p, 0, 0)))
        args.append(sinks)
    w = LANES * n_slabs
    kvw = LANES if gqa else w
    kv_spec = pl.BlockSpec((s, LANES), (lambda p: (0, 0)) if gqa else (lambda p: (0, p)))
    part = pl.BlockSpec((None, 8, LANES), lambda p: (p, 0, 0))
    out_specs = [pl.BlockSpec((s, LANES), lambda p: (0, p)), kv_spec, kv_spec,
                 pl.BlockSpec((nbr, 2, BLOCK, 2 * BLOCK), lambda p: (0, p, 0, 0)), part, part]
    out_shape = [jax.ShapeDtypeStruct((s, w), F32), jax.ShapeDtypeStruct((s, kvw), F32),
                 jax.ShapeDtypeStruct((s, kvw), F32),
                 jax.ShapeDtypeStruct((nbr, 2 * n_slabs, BLOCK, 2 * BLOCK), F32),
                 jax.ShapeDtypeStruct((n_slabs, 8, LANES), F32), jax.ShapeDtypeStruct((n_slabs, 8, LANES), F32)]
    if has_sink:
        out_specs.append(part)
        out_shape.append(jax.ShapeDtypeStruct((n_slabs, 8, LANES), F32))
    res = pl.pallas_call(
        body, name=name, grid=(n_slabs,),
        in_specs=in_specs, out_specs=out_specs, out_shape=out_shape,
        scratch_shapes=[pltpu.VMEM((s, LANES), F32) for _ in range(7)],
        compiler_params=_params(("arbitrary",), VMEM_LIMIT),
    )(*args)
    dq, dk, dv, db, dgq, dgk = res[:6]
    outs = [dq, dk, dv, db, dgq[:, 0, :], dgk[:, 0, :]]
    if has_sink:
        outs.append(res[6][:, 0, :])
    return outs


def bias_bwd(dbias, buckets, *, name):
    nbr, h = dbias.shape[:2]

    def body(db_ref, bk_ref, o_ref):
        lane = lax.broadcasted_iota(I32, (1, LANES), 1)
        acc = jnp.zeros((1, LANES), F32)
        for b in range(N_BUCKETS):
            tot = jnp.zeros((1, 1), F32)
            for br in range(nbr):
                sel = jnp.where(bk_ref[br] == b, db_ref[br], 0.0)
                tot = tot + jnp.sum(jnp.sum(sel, axis=0, keepdims=True), axis=1, keepdims=True)
            acc = jnp.where(lane == b, tot, acc)
        o_ref[...] = jnp.broadcast_to(acc, (8, LANES))

    res = pl.pallas_call(
        body, name=name, grid=(h,),
        in_specs=[pl.BlockSpec((nbr, None, BLOCK, 2 * BLOCK), lambda i: (0, i, 0, 0)),
                  pl.BlockSpec((nbr, BLOCK, 2 * BLOCK), lambda i: (0, 0, 0))],
        out_specs=pl.BlockSpec((None, 8, LANES), lambda i: (i, 0, 0)),
        out_shape=jax.ShapeDtypeStruct((h, 8, LANES), F32),
        compiler_params=_params(("parallel",)),
    )(dbias, buckets)
    return res[:, 0, :N_BUCKETS].T


SB_KG = 512


def _softplus(z):
    return jnp.maximum(z, 0.0) + jnp.log(1.0 + jnp.exp(-jnp.abs(z)))


def _split_dot(x, t):
    hi = x.astype(BF16)
    lo = (x - hi.astype(F32)).astype(BF16)
    return _dot(hi, t) + _dot(lo, t)


def sb_fwd(proj, qb0, kb0, vb0, n_slabs, *, name):
    s = proj.shape[0]
    nq = s // BLOCK
    nch = s // NORM_CH
    scale = HEAD_DIM ** -0.5

    def body(q_ref, k_ref, v_ref, o_ref, tot_ref, qlo_s, qhi_s, k_s, v_s):
        def prep(c, carry):
            rows = pl.ds(pl.multiple_of(c * NORM_CH, NORM_CH), NORM_CH)
            lo = _lo_mask((NORM_CH, LANES))
            qv = q_ref[rows, :] * scale
            qlo_s[rows, :] = jnp.where(lo, qv, 0.0).astype(BF16)
            qhi_s[rows, :] = jnp.where(lo, 0.0, qv).astype(BF16)
            k_s[rows, :] = k_ref[rows, :].astype(BF16)
            v_s[rows, :] = v_ref[rows, :].astype(BF16)
            return carry

        lax.fori_loop(0, nch, prep, 0)

        row = lax.broadcasted_iota(I32, (BLOCK, BLOCK), 0)
        col = lax.broadcasted_iota(I32, (BLOCK, BLOCK), 1)
        lo = col < HEAD_DIM
        t_ge = (row >= col).astype(BF16)
        rowg = lax.broadcasted_iota(I32, (BLOCK, SB_KG), 0)
        colg = lax.broadcasted_iota(I32, (BLOCK, SB_KG), 1)

        nsub = SB_KG // BLOCK
        heads = range(2)

        def qloop(qi, carry):
            q0 = pl.multiple_of(qi * BLOCK, BLOCK)
            qh = [qlo_s[pl.ds(q0, BLOCK), :], qhi_s[pl.ds(q0, BLOCK), :]]
            gd = qi // nsub

            def logits(gi):
                k0 = pl.multiple_of(gi * SB_KG, SB_KG)
                kg = k_s[pl.ds(k0, SB_KG), :]
                return [_dot_nt(qh[h], kg) for h in heads]

            def group(gi, st, mask):
                k0 = pl.multiple_of(gi * SB_KG, SB_KG)
                vg = v_s[pl.ds(k0, SB_KG), :]
                c = [st[0], st[2]]
                z = [st[4], st[5]]
                z_next = logits(jnp.maximum(gi - 1, 0))
                sp = [_softplus(z[h]) for h in heads]
                lrem = [-sp[h] if mask is None else jnp.where(mask, -sp[h], 0.0) for h in heads]
                piece = lambda x, j: x[:, j * BLOCK:(j + 1) * BLOCK]
                incl = [[_split_dot(piece(lrem[h], j), t_ge) for j in range(nsub)] for h in heads]
                a = []
                for h in heads:
                    suffix = [None] * nsub
                    for j in reversed(range(nsub)):
                        suffix[j] = c[h] + incl[h][j] - piece(lrem[h], j)
                        c[h] = c[h] + incl[h][j][:, 0:1]
                    ah = jnp.exp(z[h] - sp[h] + jnp.concatenate(suffix, axis=1))
                    a.append((ah if mask is None else jnp.where(mask, ah, 0.0)).astype(BF16))
                o = [st[1 + 2 * h] + _dot(a[h], vg) for h in heads]
                return c[0], o[0], c[1], o[1], z_next[0], z_next[1]

            zc = jnp.zeros((BLOCK, 1), F32)
            zo = jnp.zeros((BLOCK, LANES), F32)
            st = group(gd, (zc, zo, zc, zo, *logits(gd)), (gd * SB_KG + colg) < (q0 + rowg))
            c0, o0, c1, o1, _, _ = lax.fori_loop(0, gd, lambda t, st: group(gd - 1 - t, st, None), st)
            o_ref[pl.ds(q0, BLOCK), :] = jnp.where(lo, o0, o1)
            tot_ref[pl.ds(q0, BLOCK), :] = jnp.where(lo, c0, c1)
            return carry

        lax.fori_loop(0, nq, qloop, 0)

    slab = lambda b0: pl.BlockSpec((s, LANES), lambda p: (0, b0 + p), pipeline_mode=pl.Buffered(1))
    w = LANES * n_slabs
    return pl.pallas_call(
        body, name=name, grid=(n_slabs,),
        in_specs=[slab(qb0), slab(kb0), slab(vb0)],
        out_specs=[pl.BlockSpec((s, LANES), lambda p: (0, p)), pl.BlockSpec((s, LANES), lambda p: (0, p))],
        out_shape=[jax.ShapeDtypeStruct((s, w), F32), jax.ShapeDtypeStruct((s, w), F32)],
        scratch_shapes=[pltpu.VMEM((s, LANES), BF16) for _ in range(4)],
        compiler_params=_params(("parallel",), VMEM_LIMIT),
    )(proj, proj, proj)


def sb_bwd(proj, qb0, kb0, vb0, n_slabs, dout, tot, *, name):
    s = proj.shape[0]
    nq = s // BLOCK
    nch = s // NORM_CH
    nsub = SB_KG // BLOCK
    scale = HEAD_DIM ** -0.5

    def body(q_ref, k_ref, v_ref, do_ref, tot_ref, dq_ref, dk_ref, dv_ref,
             qlo_s, qhi_s, k_s, v_s, dlo_s, dhi_s):
        def prep(c, carry):
            rows = pl.ds(pl.multiple_of(c * NORM_CH, NORM_CH), NORM_CH)
            lo = _lo_mask((NORM_CH, LANES))
            qv = q_ref[rows, :] * scale
            dv = do_ref[rows, :]
            qlo_s[rows, :] = jnp.where(lo, qv, 0.0).astype(BF16)
            qhi_s[rows, :] = jnp.where(lo, 0.0, qv).astype(BF16)
            dlo_s[rows, :] = jnp.where(lo, dv, 0.0).astype(BF16)
            dhi_s[rows, :] = jnp.where(lo, 0.0, dv).astype(BF16)
            k_s[rows, :] = k_ref[rows, :].astype(BF16)
            v_s[rows, :] = v_ref[rows, :].astype(BF16)
            z = jnp.zeros((NORM_CH, LANES), F32)
            dk_ref[rows, :] = z
            dv_ref[rows, :] = z
            return carry

        lax.fori_loop(0, nch, prep, 0)

        row = lax.broadcasted_iota(I32, (BLOCK, BLOCK), 0)
        col = lax.broadcasted_iota(I32, (BLOCK, BLOCK), 1)
        lo = col < HEAD_DIM
        t_le = (row <= col).astype(BF16)
        rowg = lax.broadcasted_iota(I32, (BLOCK, SB_KG), 0)
        colg = lax.broadcasted_iota(I32, (BLOCK, SB_KG), 1)

        heads = range(2)
        piece = lambda x, j: x[:, j * BLOCK:(j + 1) * BLOCK]

        def prefixes(x):
            return [[_split_dot(piece(x[h], j), t_le) for j in range(nsub)] for h in heads]

        def chain(pre, run, total=None):
            out = []
            for j in range(nsub):
                out.append(run + pre[j] if total is None else total - run - pre[j])
                run = run + pre[j][:, BLOCK - 1:BLOCK]
            return jnp.concatenate(out, axis=1), run

        def qloop(qi, carry):
            q0 = pl.multiple_of(qi * BLOCK, BLOCK)
            qh = [qlo_s[pl.ds(q0, BLOCK), :], qhi_s[pl.ds(q0, BLOCK), :]]
            doh = [dlo_s[pl.ds(q0, BLOCK), :], dhi_s[pl.ds(q0, BLOCK), :]]
            tot_t = tot_ref[pl.ds(q0, BLOCK), :]
            tots = [tot_t[:, 0:1], tot_t[:, HEAD_DIM:HEAD_DIM + 1]]
            gd = qi // nsub

            def logits(gi):
                kg = k_s[pl.ds(pl.multiple_of(gi * SB_KG, SB_KG), SB_KG), :]
                return [_dot_nt(qh[h], kg) for h in heads]

            def group(gi, st, mask):
                k0 = pl.multiple_of(gi * SB_KG, SB_KG)
                kg, vg = k_s[pl.ds(k0, SB_KG), :], v_s[pl.ds(k0, SB_KG), :]
                cp, cg = [st[0], st[2]], [st[1], st[3]]
                z = [st[5], st[6]]
                z_next = logits(jnp.minimum(gi + 1, gd))
                da = [_dot_nt(doh[h], vg) for h in heads]
                sp = [_softplus(z[h]) for h in heads]
                lrem = [-sp[h] if mask is None else jnp.where(mask, -sp[h], 0.0) for h in heads]
                pre = prefixes(lrem)
                e, a, g = [], [], []
                for h in heads:
                    suffix, cp[h] = chain(pre[h], cp[h], tots[h])
                    e.append(z[h] - sp[h])
                    ah = jnp.exp(e[h] + suffix)
                    a.append(ah if mask is None else jnp.where(mask, ah, 0.0))
                    g.append(a[h] * da[h])
                gpre = prefixes(g)
                dz = []
                for h in heads:
                    ginc, cg[h] = chain(gpre[h], cg[h])
                    dzh = g[h] - jnp.exp(e[h]) * ginc
                    dz.append((dzh if mask is None else jnp.where(mask, dzh, 0.0)).astype(BF16))
                ab = [a[h].astype(BF16) for h in heads]
                dq = st[4] + jnp.where(lo, _dot(dz[0], kg), _dot(dz[1], kg))
                dk_ref[pl.ds(k0, SB_KG), :] += _dot_tn(dz[0], qh[0]) + _dot_tn(dz[1], qh[1])
                dv_ref[pl.ds(k0, SB_KG), :] += _dot_tn(ab[0], doh[0]) + _dot_tn(ab[1], doh[1])
                return cp[0], cg[0], cp[1], cg[1], dq, z_next[0], z_next[1]

            zc = jnp.zeros((BLOCK, 1), F32)
            st = lax.fori_loop(0, gd, lambda gi, st: group(gi, st, None),
                               (zc, zc, zc, zc, jnp.zeros((BLOCK, LANES), F32), *logits(0)))
            st = group(gd, st, (gd * SB_KG + colg) < (q0 + rowg))
            dq_ref[pl.ds(q0, BLOCK), :] = st[4] * scale
            return carry

        lax.fori_loop(0, nq, qloop, 0)

    slab = lambda b0: pl.BlockSpec((s, LANES), lambda p: (0, b0 + p), pipeline_mode=pl.Buffered(1))
    own = pl.BlockSpec((s, LANES), lambda p: (0, p), pipeline_mode=pl.Buffered(1))
    w = LANES * n_slabs
    outb = pl.BlockSpec((s, LANES), lambda p: (0, p))
    return pl.pallas_call(
        body, name=name, grid=(n_slabs,),
        in_specs=[slab(qb0), slab(kb0), slab(vb0), own, own],
        out_specs=[outb, outb, outb],
        out_shape=[jax.ShapeDtypeStruct((s, w), F32)] * 3,
        scratch_shapes=[pltpu.VMEM((s, LANES), BF16) for _ in range(6)],
        compiler_params=_params(("parallel",), VMEM_LIMIT),
    )(proj, proj, proj, dout, tot)


def _place():
    x, y, c = lax.axis_index("x"), lax.axis_index("y"), lax.axis_index("c")
    return x, y, c


def gather_blocks(shards, *, name):
    n = len(shards)

    def body(*refs):
        in_refs, out_refs = refs[:n], refs[n:2 * n]
        send_sems, recv_sems, local_sems = refs[2 * n:]
        x, y, c = _place()
        me, sibling = (x, y, c), (x, y, 1 - c)
        chips = [(1 - x, y), (x, 1 - y), (1 - x, 1 - y)]

        def blk(a, place):
            return out_refs[a].at[4 * place[0] + 2 * place[1] + place[2]]

        def copy(a, k, block, to, src=None):
            return pltpu.make_async_remote_copy(
                src_ref=blk(a, block) if src is None else src, dst_ref=blk(a, block),
                send_sem=send_sems.at[a, k], recv_sem=recv_sems.at[a, k], device_id=to, device_id_type=MESH)

        mine = [pltpu.make_async_copy(in_refs[a], blk(a, me), local_sems.at[a]) for a in range(n)]
        for cp in mine:
            cp.start()
        first = []
        for a in range(n):
            first.append(copy(a, 0, me, sibling, src=in_refs[a]))
            first += [copy(a, 1 + j, me, (*chip, c), src=in_refs[a]) for j, chip in enumerate(chips)]
        for cp in first:
            cp.start()
        passed = []
        for j, chip in enumerate(chips):
            for a in range(n):
                copy(a, 1 + j, (*chip, c), me).wait_recv()
                fw = copy(a, 4 + j, (*chip, c), sibling)
                fw.start()
                passed.append(fw)
        for a in range(n):
            copy(a, 0, sibling, me).wait_recv()
            for j, chip in enumerate(chips):
                copy(a, 4 + j, (*chip, 1 - c), me).wait_recv()
        for cp in first + passed:
            cp.wait_send()
        for cp in mine:
            cp.wait()

    any_spec = pl.BlockSpec(memory_space=pl.ANY)
    return pl.pallas_call(
        body, name=name,
        in_specs=[any_spec] * n, out_specs=[any_spec] * n,
        out_shape=[jax.ShapeDtypeStruct((N_DEV,) + sh.shape, sh.dtype) for sh in shards],
        scratch_shapes=[pltpu.SemaphoreType.DMA((n, 7)), pltpu.SemaphoreType.DMA((n, 7)),
                        pltpu.SemaphoreType.DMA((n,))],
    )(*shards)


def gather_small(v, *, name):
    m_per, n = v.shape

    def body(x_ref, out_ref, send_sems, recv_sems, local_sem):
        x, y, c = _place()
        me, sibling = (x, y, c), (x, y, 1 - c)
        chips = [(1 - x, y), (x, 1 - y), (1 - x, 1 - y)]

        def rows(px, py, pc):
            return out_ref.at[pl.ds((4 * px + 2 * py + pc) * m_per, m_per), :]

        def copy(k, block, to, src=None):
            return pltpu.make_async_remote_copy(
                src_ref=rows(*block) if src is None else src, dst_ref=rows(*block),
                send_sem=send_sems.at[k], recv_sem=recv_sems.at[k], device_id=to, device_id_type=MESH)

        mine = pltpu.make_async_copy(x_ref, rows(*me), local_sem)
        mine.start()
        first = [copy(0, me, sibling, src=x_ref)]
        first += [copy(1 + j, me, (*chip, c), src=x_ref) for j, chip in enumerate(chips)]
        for cp in first:
            cp.start()
        passed = [copy(4 + j, (*chip, c), sibling) for j, chip in enumerate(chips)]
        for j, chip in enumerate(chips):
            copy(1 + j, (*chip, c), me).wait_recv()
            passed[j].start()
        copy(0, sibling, me).wait_recv()
        for j, chip in enumerate(chips):
            copy(4 + j, (*chip, 1 - c), me).wait_recv()
        for cp in first + passed:
            cp.wait_send()
        mine.wait()

    return pl.pallas_call(
        body, name=name,
        out_shape=jax.ShapeDtypeStruct((N_DEV * m_per, n), v.dtype),
        in_specs=[pl.BlockSpec(memory_space=pltpu.VMEM)],
        out_specs=pl.BlockSpec(memory_space=pltpu.VMEM),
        scratch_shapes=[pltpu.SemaphoreType.DMA((7,)), pltpu.SemaphoreType.DMA((7,)), pltpu.SemaphoreType.DMA],
        compiler_params=_params(None, VMEM_LIMIT),
    )(v)


def swap_core_halves(parts, *, name):
    n = len(parts)

    def body(*refs):
        in_refs, out_refs = refs[:n], refs[n:2 * n]
        send_sems, recv_sems = refs[2 * n:]
        x, y, c = _place()
        cps = []
        for a in range(n):
            for q in range(4):
                cps.append(pltpu.make_async_remote_copy(
                    src_ref=in_refs[a].at[2 * q + (1 - c)], dst_ref=out_refs[a].at[q],
                    send_sem=send_sems.at[a, q], recv_sem=recv_sems.at[a, q],
                    device_id=(x, y, 1 - c), device_id_type=MESH))
        for cp in cps:
            cp.start()
        for cp in cps:
            cp.wait()

    any_spec = pl.BlockSpec(memory_space=pl.ANY)
    return pl.pallas_call(
        body, name=name,
        in_specs=[any_spec] * n, out_specs=[any_spec] * n,
        out_shape=[jax.ShapeDtypeStruct((4,) + p.shape[1:], p.dtype) for p in parts],
        scratch_shapes=[pltpu.SemaphoreType.DMA((n, 4)), pltpu.SemaphoreType.DMA((n, 4))],
    )(*parts)


def add_core_halves(part, recv, *, name):
    _, r, cdim = part.shape
    tr = _pick(r, (512, 256, 128, 64, 32, 16))
    c_idx = lax.axis_index("c").astype(I32).reshape(1)

    def body(c_ref, p_ref, r_ref, o_ref):
        o_ref[...] = (p_ref[...].astype(F32) + r_ref[...].astype(F32)).astype(BF16)

    return pl.pallas_call(
        body, name=name,
        grid_spec=pltpu.PrefetchScalarGridSpec(
            num_scalar_prefetch=1, grid=(4, r // tr),
            in_specs=[pl.BlockSpec((None, tr, cdim), lambda q, i, c_ref: (2 * q + c_ref[0], i, 0)),
                      pl.BlockSpec((None, tr, cdim), lambda q, i, c_ref: (q, i, 0))],
            out_specs=pl.BlockSpec((None, tr, cdim), lambda q, i, c_ref: (q, i, 0))),
        out_shape=jax.ShapeDtypeStruct((4, r, cdim), BF16),
        compiler_params=_params(("parallel", "parallel")),
    )(c_idx, part, recv)


def exchange_chips(sums, *, name):
    n = len(sums)

    def body(*refs):
        in_refs, out_refs = refs[:n], refs[n:2 * n]
        send_sems, recv_sems, local_sems = refs[2 * n:]
        x, y, c = _place()
        my_chip = 2 * x + y
        chips = [(1 - x, y), (x, 1 - y), (1 - x, 1 - y)]
        cps = []
        mine = []
        for a in range(n):
            mine.append(pltpu.make_async_copy(in_refs[a].at[my_chip], out_refs[a].at[my_chip], local_sems.at[a]))
            for j, (qx, qy) in enumerate(chips):
                cps.append(pltpu.make_async_remote_copy(
                    src_ref=in_refs[a].at[2 * qx + qy], dst_ref=out_refs[a].at[my_chip],
                    send_sem=send_sems.at[a, j], recv_sem=recv_sems.at[a, j],
                    device_id=(qx, qy, c), device_id_type=MESH))
        for cp in mine + cps:
            cp.start()
        for cp in cps:
            cp.wait()
        for cp in mine:
            cp.wait()

    any_spec = pl.BlockSpec(memory_space=pl.ANY)
    return pl.pallas_call(
        body, name=name,
        in_specs=[any_spec] * n, out_specs=[any_spec] * n,
        out_shape=[jax.ShapeDtypeStruct(p.shape, p.dtype) for p in sums],
        scratch_shapes=[pltpu.SemaphoreType.DMA((n, 3)), pltpu.SemaphoreType.DMA((n, 3)),
                        pltpu.SemaphoreType.DMA((n,))],
    )(*sums)


def _adamw_math(w, g, m, v):
    m = ADAM_B1 * m + (1.0 - ADAM_B1) * g
    v = ADAM_B2 * v + (1.0 - ADAM_B2) * (g * g)
    m_hat = m / (1.0 - ADAM_B1 ** ADAM_STEP)
    v_hat = v / (1.0 - ADAM_B2 ** ADAM_STEP)
    delta = -ADAM_LR * (m_hat / (jnp.sqrt(v_hat) + ADAM_EPS) + ADAM_WD * w)
    return delta, m, v


def adamw_parts(parts, w, m, v, *, name):
    r, cdim = w.shape
    tr = _pick(r, [t for t in (512, 256, 128, 64, 32, 16) if t * cdim <= 512 * 1024])

    def body(p_ref, w_ref, m_ref, v_ref, g_ref, d_ref, nm_ref, nv_ref):
        g = p_ref[0].astype(F32)
        for q in range(1, 4):
            g = g + p_ref[q].astype(F32)
        delta, nm, nv = _adamw_math(w_ref[...], g, m_ref[...], v_ref[...])
        g_ref[...], d_ref[...], nm_ref[...], nv_ref[...] = g, delta, nm, nv

    t = pl.BlockSpec((tr, cdim), lambda i: (i, 0))
    return pl.pallas_call(
        body, name=name, grid=(r // tr,),
        in_specs=[pl.BlockSpec((4, tr, cdim), lambda i: (0, i, 0)), t, t, t],
        out_specs=[t, t, t, t],
        out_shape=[jax.ShapeDtypeStruct((r, cdim), F32)] * 4,
        compiler_params=_params(("parallel",)),
    )(parts, w, m, v)


def sum_devices(gathered, *, name):
    m_rows = gathered.shape[1]

    def body(ga_ref, g_ref):
        g = ga_ref[0]
        for dev in range(1, N_DEV):
            g = g + ga_ref[dev]
        g_ref[...] = g

    return pl.pallas_call(
        body, name=name, out_shape=jax.ShapeDtypeStruct((m_rows, LANES), F32),
        compiler_params=_params(None, VMEM_LIMIT),
    )(gathered)


def adamw_small(g, w, m, v, *, name):
    m_rows = w.shape[0]

    def body(g_ref, w_ref, m_ref, v_ref, d_ref, nm_ref, nv_ref):
        d_ref[...], nm_ref[...], nv_ref[...] = _adamw_math(w_ref[...], g_ref[...], m_ref[...], v_ref[...])

    return pl.pallas_call(
        body, name=name, out_shape=[jax.ShapeDtypeStruct((m_rows, LANES), F32)] * 3,
        compiler_params=_params(None, VMEM_LIMIT),
    )(g, w, m, v)


def _t5_bucket(dist):
    max_exact = N_BUCKETS // 2
    d = jnp.maximum(dist, 0)
    large = max_exact + (jnp.log(jnp.maximum(d, 1).astype(F32) / max_exact)
                         / math.log(T5_MAX_DIST / max_exact) * (N_BUCKETS - max_exact)).astype(I32)
    large = jnp.minimum(large, N_BUCKETS - 1)
    return jnp.where(d < max_exact, d, large)


def _rel():
    return jnp.arange(BLOCK)[:, None] + BLOCK - jnp.arange(2 * BLOCK)[None, :]


def _band_bias(table, dils, max_dists):
    rel = _rel()
    biases, buckets = [], []
    for d, md in zip(dils, max_dists):
        bk = _t5_bucket(rel * d)
        vis = (rel >= 0) & (rel <= md)
        looked_up = jnp.zeros((table.shape[1],) + rel.shape, F32)
        for b in range(N_BUCKETS):
            looked_up = jnp.where((bk == b)[None], table[b][:, None, None], looked_up)
        with_prev = jnp.where(vis[None], looked_up, NEG_INF)
        first = jnp.arange(2 * BLOCK)[None, None, :] >= BLOCK
        biases.append(jnp.stack([with_prev, jnp.where(first, with_prev, NEG_INF)]))
        buckets.append(bk.astype(I32))
    return jnp.stack(biases), jnp.stack(buckets)


def _pack(pieces, rows):
    flat = jnp.concatenate([p.reshape(-1) for p in pieces])
    return jnp.pad(flat, (0, rows * LANES - flat.shape[0])).reshape(rows, LANES)


def _unpack(packed, shapes):
    flat = packed.reshape(-1)
    out, off = [], 0
    for sh in shapes:
        n = math.prod(sh)
        out.append(flat[off:off + n].reshape(sh))
        off += n
    return out


def _tile2(g):
    return jnp.concatenate([g, g])


def kernel(x, attn_norm, w_in, a_q_gain, a_k_gain, a_sinks, c_q_gain, c_k_gain, rel_bias_table, mix_out_gain, w_out, ffn_norm, w_up, conv_w, conv_b, w_down, loss_target, m_attn_norm, m_w_in, m_a_q_gain, m_a_k_gain, m_a_sinks, m_c_q_gain, m_c_k_gain, m_rel_bias_table, m_mix_out_gain, m_w_out, m_ffn_norm, m_w_up, m_conv_w, m_conv_b, m_w_down, v_attn_norm, v_w_in, v_a_q_gain, v_a_k_gain, v_a_sinks, v_c_q_gain, v_c_k_gain, v_rel_bias_table, v_mix_out_gain, v_w_out, v_ffn_norm, v_w_up, v_conv_w, v_conv_b, v_w_down):
    depth, d_model, in_shard = w_in.shape
    ff2_shard = w_up.shape[2]
    s = x.shape[1]
    in_width, ff2 = N_DEV * in_shard, N_DEV * ff2_shard
    n_heads = d_model // HEAD_DIM
    ha, hb, hc = n_heads // 4, n_heads // 4, n_heads // 2
    sa, sb, sc = ha // 2, hb // 2, hc // 2
    kv_a = ha // 4
    assert kv_a == 2 and BLOCK == LANES
    cb_aq, cb_ak, cb_av = 0, sa, sa + 1
    cb_bq = sa + 2
    cb_bk, cb_bv = cb_bq + sb, cb_bq + 2 * sb
    cb_cq = cb_bq + 3 * sb
    cb_ck, cb_cv = cb_cq + sc, cb_cq + 2 * sc
    assert (cb_cv + sc) * LANES == in_width
    dev = 4 * lax.axis_index("x") + 2 * lax.axis_index("y") + lax.axis_index("c")

    g_in, g_out, g_up, g_down, g_cw = gather_blocks(
        [w_in.astype(BF16), w_out.astype(BF16), w_up.astype(BF16), w_down.astype(BF16), conv_w],
        name="gather_weights")
    wi = jnp.transpose(g_in, (1, 2, 0, 3)).reshape(depth, d_model, in_width)
    wo = jnp.transpose(g_out, (1, 0, 2, 3)).reshape(depth, d_model, d_model)
    wu = jnp.transpose(g_up, (1, 2, 0, 3)).reshape(depth, d_model, ff2)
    wd = jnp.transpose(g_down, (1, 0, 2, 3)).reshape(depth, ff2 // 2, d_model)
    cw = jnp.transpose(g_cw, (1, 2, 0, 3)).reshape(depth, 3, ff2)

    bias_a, buckets_a = _band_bias(rel_bias_table[:, :ha], (1,), (WINDOW_A - 1,))
    bias_c, buckets_c = _band_bias(rel_bias_table[:, ha:], DILATIONS, (BLOCK,) * len(DILATIONS))

    xs = x[0]
    saved = []
    for l in range(depth):
        h1 = rmsnorm_fwd(xs, attn_norm[l], name="attn_norm_fwd")
        proj = matmul(h1, wi[l], name="in_proj")
        sinks = jnp.repeat(a_sinks[l], HEAD_DIM).reshape(sa, 1, LANES)
        gaq, gak = _tile2(a_q_gain[l]), _tile2(a_k_gain[l])
        gcq, gck = _tile2(c_q_gain[l]), _tile2(c_k_gain[l])
        out_a, lse_a = banded_fwd(proj, cb_aq, cb_ak, cb_av, sa, gaq, gak, bias_a, (1,), sinks, True, name="swa_fwd")
        out_b, tot_b = sb_fwd(proj, cb_bq, cb_bk, cb_bv, sb, name="stick_fwd")
        out_c, lse_c = banded_fwd(proj, cb_cq, cb_ck, cb_cv, sc, gcq, gck, bias_c, DILATIONS, None, False,
                                  name="dilated_fwd")
        mix = mixnorm_fwd([out_a, out_b, out_c], mix_out_gain[l], name="mix_norm_fwd")
        x_mid = matmul(mix, wo[l], res=xs, name="out_proj")
        h2 = rmsnorm_fwd(x_mid, ffn_norm[l], name="ffn_norm_fwd")
        p = matmul(h2, wu[l], name="up_proj")
        act = ffn_act_fwd(p, cw[l], conv_b[l], name="ffn_act_fwd")
        x_out = matmul(act, wd[l], res=x_mid, name="down_proj")
        saved.append(dict(x_in=xs, h1t=h1.T, proj=proj, out_a=out_a, lse_a=lse_a, out_b=out_b, tot_b=tot_b,
                          out_c=out_c, lse_c=lse_c, mixt=mix.T, x_mid=x_mid, h2t=h2.T, p=p, actt=act.T,
                          sinks=sinks, gains=(gaq, gak, gcq, gck)))
        xs = x_out

    dx, loss_part = loss_head(xs, loss_target[0], name="loss_head")

    small = {k: [None] * depth for k in ("attn_norm", "a_q_gain", "a_k_gain", "a_sinks", "c_q_gain", "c_k_gain",
                                         "mix_out_gain", "ffn_norm", "conv_w", "conv_b")}
    big = {k: [None] * depth for k in ("w_in", "w_out", "w_up", "w_down")}
    dbias_a = dbias_c = None
    for l in reversed(range(depth)):
        sv = saved[l]
        gaq, gak, gcq, gck = sv["gains"]
        da = matmul(dx, wd[l], trans_b=True, name="down_proj_dx")
        big["w_down"][l] = matmul(sv["actt"], dx, out_dtype=BF16, name="down_proj_dw")
        dp, small["conv_w"][l], small["conv_b"][l] = ffn_act_bwd(da, sv["p"], cw[l], conv_b[l], name="ffn_act_bwd")
        dh2 = matmul(dp, wu[l], trans_b=True, name="up_proj_dx")
        big["w_up"][l] = matmul(sv["h2t"], dp, out_dtype=BF16, name="up_proj_dw")
        dx_mid, small["ffn_norm"][l] = rmsnorm_bwd(dh2, sv["x_mid"], ffn_norm[l], dx, name="ffn_norm_bwd")
        dmix = matmul(dx_mid, wo[l], trans_b=True, name="out_proj_dx")
        big["w_out"][l] = matmul(sv["mixt"], dx_mid, out_dtype=BF16, name="out_proj_dw")
        (d_oa, d_ob, d_oc), small["mix_out_gain"][l] = mixnorm_bwd(
            dmix, [sv["out_a"], sv["out_b"], sv["out_c"]], mix_out_gain[l], name="mix_norm_bwd")
        dq_a, dk_a, dv_a, db_a, dgq_a, dgk_a, dsink = banded_bwd(
            sv["proj"], cb_aq, cb_ak, cb_av, sa, gaq, gak, bias_a, (1,), sv["sinks"], True,
            d_oa, sv["out_a"], sv["lse_a"], name="swa_bwd")
        dq_b, dk_b, dv_b = sb_bwd(sv["proj"], cb_bq, cb_bk, cb_bv, sb, d_ob, sv["tot_b"], name="stick_bwd")
        dq_c, dk_c, dv_c, db_c, dgq_c, dgk_c = banded_bwd(
            sv["proj"], cb_cq, cb_ck, cb_cv, sc, gcq, gck, bias_c, DILATIONS, None, False,
            d_oc, sv["out_c"], sv["lse_c"], name="dilated_bwd")
        fold = lambda g: g.reshape(-1, HEAD_DIM).sum(axis=0)
        small["a_q_gain"][l], small["a_k_gain"][l] = fold(dgq_a), fold(dgk_a)
        small["c_q_gain"][l], small["c_k_gain"][l] = fold(dgq_c), fold(dgk_c)
        small["a_sinks"][l] = dsink[:, ::HEAD_DIM].reshape(-1)
        dbias_a = db_a if dbias_a is None else dbias_a + db_a
        dbias_c = db_c if dbias_c is None else dbias_c + db_c
        dproj = jnp.concatenate([dq_a, dk_a, dv_a, dq_b, dk_b, dv_b, dq_c, dk_c, dv_c], axis=1)
        dh1 = matmul(dproj, wi[l], trans_b=True, name="in_proj_dx")
        big["w_in"][l] = matmul(sv["h1t"], dproj, out_dtype=BF16, name="in_proj_dw")
        dx, small["attn_norm"][l] = rmsnorm_bwd(dh1, sv["x_in"], attn_norm[l], dx_mid, name="attn_norm_bwd")

    dtable = jnp.concatenate([bias_bwd(dbias_a, buckets_a, name="swa_bias_bwd"),
                              bias_bwd(dbias_c, buckets_c, name="dilated_bias_bwd")], axis=1)

    order = ("attn_norm", "a_q_gain", "a_k_gain", "a_sinks", "c_q_gain", "c_k_gain", "rel_bias_table",
             "mix_out_gain", "ffn_norm", "conv_w", "conv_b")
    partial = {k: jnp.stack(v) for k, v in small.items()}
    partial["rel_bias_table"] = dtable
    pieces = [partial[k] for k in order] + [loss_part.reshape(1)]
    n_small = sum(math.prod(pc.shape) for pc in pieces)
    rows = -(-n_small // (8 * LANES)) * 8
    gathered = gather_small(_pack(pieces, rows), name="gather_small_grads")
    summed = _unpack(sum_devices(gathered.reshape(N_DEV, rows, LANES), name="sum_small_grads"),
                     [pc.shape for pc in pieces])
    g_small = dict(zip(order, summed[:-1]))
    loss = summed[-1][0]
    g_small["conv_w"] = lax.dynamic_slice_in_dim(g_small["conv_w"], dev * ff2_shard, ff2_shard, axis=2)

    w_small = dict(attn_norm=attn_norm, a_q_gain=a_q_gain, a_k_gain=a_k_gain, a_sinks=a_sinks, c_q_gain=c_q_gain,
                   c_k_gain=c_k_gain, rel_bias_table=rel_bias_table, mix_out_gain=mix_out_gain, ffn_norm=ffn_norm,
                   conv_w=conv_w, conv_b=conv_b)
    m_small = dict(attn_norm=m_attn_norm, a_q_gain=m_a_q_gain, a_k_gain=m_a_k_gain, a_sinks=m_a_sinks,
                   c_q_gain=m_c_q_gain, c_k_gain=m_c_k_gain, rel_bias_table=m_rel_bias_table,
                   mix_out_gain=m_mix_out_gain, ffn_norm=m_ffn_norm, conv_w=m_conv_w, conv_b=m_conv_b)
    v_small = dict(attn_norm=v_attn_norm, a_q_gain=v_a_q_gain, a_k_gain=v_a_k_gain, a_sinks=v_a_sinks,
                   c_q_gain=v_c_q_gain, c_k_gain=v_c_k_gain, rel_bias_table=v_rel_bias_table,
                   mix_out_gain=v_mix_out_gain, ffn_norm=v_ffn_norm, conv_w=v_conv_w, conv_b=v_conv_b)
    shapes = [w_small[k].shape for k in order]
    n_upd = sum(math.prod(sh) for sh in shapes)
    urows = -(-n_upd // (8 * LANES)) * 8
    packs = [_pack([d[k] for k in order], urows) for d in (g_small, w_small, m_small, v_small)]
    upd = adamw_small(*packs, name="adamw_small")
    delta_s, newm_s, newv_s = [dict(zip(order, _unpack(u, shapes))) for u in upd]

    def by_cols(lst):
        a = jnp.stack(lst)
        dd, kk, nn = a.shape
        return jnp.transpose(a.reshape(dd, kk, N_DEV, nn // N_DEV), (2, 0, 1, 3)).reshape(N_DEV, dd * kk, nn // N_DEV)

    def by_rows(lst):
        a = jnp.stack(lst)
        dd, kk, nn = a.shape
        return jnp.transpose(a.reshape(dd, N_DEV, kk // N_DEV, nn), (1, 0, 2, 3)).reshape(N_DEV, dd * kk // N_DEV, nn)

    names_big = ("w_in", "w_out", "w_up", "w_down")
    parts = [by_cols(big["w_in"]), by_rows(big["w_out"]), by_cols(big["w_up"]), by_rows(big["w_down"])]
    from_core = swap_core_halves(parts, name="grads_to_other_core")
    chip_sums = [add_core_halves(pt, rc, name="grads_add_cores") for pt, rc in zip(parts, from_core)]
    from_chips = exchange_chips(chip_sums, name="grads_to_other_chips")
    w_big = dict(w_in=(w_in, m_w_in, v_w_in), w_out=(w_out, m_w_out, v_w_out), w_up=(w_up, m_w_up, v_w_up),
                 w_down=(w_down, m_w_down, v_w_down))
    g_big, delta_b, newm_b, newv_b = {}, {}, {}, {}
    for k, fc in zip(names_big, from_chips):
        w, m, v = w_big[k]
        flat = lambda t: t.reshape(-1, t.shape[-1])
        res = adamw_parts(fc, flat(w), flat(m), flat(v), name="adamw_large")
        g_big[k], delta_b[k], newm_b[k], newv_b[k] = [r.reshape(w.shape) for r in res]

    all_names = ("attn_norm", "w_in", "a_q_gain", "a_k_gain", "a_sinks", "c_q_gain", "c_k_gain", "rel_bias_table",
                 "mix_out_gain", "w_out", "ffn_norm", "w_up", "conv_w", "conv_b", "w_down")
    pick = lambda sm, bg: [bg[k] if k in bg else sm[k] for k in all_names]
    return (loss, dx[None], *pick(g_small, g_big), *pick(delta_s, delta_b), *pick(newm_s, newm_b),
            *pick(newv_s, newv_b))
```

```python
import functools
import math

import jax
import jax.numpy as jnp
from jax import lax
from jax.experimental import pallas as pl
from jax.experimental.pallas import tpu as pltpu

F32, BF16, I32 = jnp.float32, jnp.bfloat16, jnp.int32
MESH = pl.DeviceIdType.MESH

HEAD_DIM = 64
LANES = 128
BLOCK = 128
EPS = 1e-6
NEG_INF = -1e30
N_BUCKETS = 32
T5_MAX_DIST = 2048
WINDOW_A = 128
DILATIONS = (1, 4, 16)
N_DEV = 8
VMEM_LIMIT = 56 * 1024 * 1024

ADAM_LR, ADAM_B1, ADAM_B2, ADAM_EPS, ADAM_WD, ADAM_STEP = 0.001, 0.9, 0.999, 1e-08, 0.01, 10


def _params(sem=None, vmem=None):
    return pltpu.CompilerParams(dimension_semantics=sem, vmem_limit_bytes=vmem)


def _pick(n, cands):
    for c in cands:
        if n % c == 0:
            return c
    raise ValueError(f"no tile for {n}")


def _dot(a, b):
    return lax.dot_general(a, b, (((1,), (0,)), ((), ())), preferred_element_type=F32)


def _dot_nt(a, b):
    return lax.dot_general(a, b, (((1,), (1,)), ((), ())), preferred_element_type=F32)


def _dot_tn(a, b):
    return lax.dot_general(a, b, (((0,), (0,)), ((), ())), preferred_element_type=F32)


def matmul(a, b, *, trans_b=False, out_dtype=F32, res=None, name):
    m, k = a.shape
    n = b.shape[0] if trans_b else b.shape[1]
    tm = _pick(m, (1024, 512, 256))
    tn = _pick(n, (1024, 768, 512, 256, 128))
    tk = k if k <= 2048 else _pick(k, (1024, 768, 512, 256))
    nk = k // tk
    dn = (((1,), (1,)), ((), ())) if trans_b else (((1,), (0,)), ((), ()))

    def body(*refs):
        if res is None:
            a_ref, b_ref, o_ref, acc = refs
        else:
            a_ref, b_ref, r_ref, o_ref, acc = refs
        kk = pl.program_id(2)

        @pl.when(kk == 0)
        def _():
            acc[...] = jnp.zeros_like(acc)

        acc[...] += lax.dot_general(a_ref[...].astype(BF16), b_ref[...].astype(BF16), dn,
                                    preferred_element_type=F32)

        @pl.when(kk == nk - 1)
        def _():
            r = acc[...]
            if res is not None:
                r = r_ref[...] + r
            o_ref[...] = r.astype(out_dtype)

    b_spec = (pl.BlockSpec((tn, tk), lambda i, j, kk: (j, kk)) if trans_b
              else pl.BlockSpec((tk, tn), lambda i, j, kk: (kk, j)))
    in_specs = [pl.BlockSpec((tm, tk), lambda i, j, kk: (i, kk)), b_spec]
    args = [a, b]
    if res is not None:
        in_specs.append(pl.BlockSpec((tm, tn), lambda i, j, kk: (i, j)))
        args.append(res)
    return pl.pallas_call(
        body, name=name, grid=(m // tm, n // tn, nk),
        in_specs=in_specs, out_specs=pl.BlockSpec((tm, tn), lambda i, j, kk: (i, j)),
        out_shape=jax.ShapeDtypeStruct((m, n), out_dtype),
        scratch_shapes=[pltpu.VMEM((tm, tn), F32)],
        compiler_params=_params(("parallel", "parallel", "arbitrary"), VMEM_LIMIT),
    )(*args)


def rmsnorm_fwd(x, g, *, name):
    s, d = x.shape
    tm = 512

    def body(x_ref, g_ref, o_ref):
        xv = x_ref[...]
        r = lax.rsqrt(jnp.mean(xv * xv, axis=-1, keepdims=True) + EPS)
        o_ref[...] = (xv * r * g_ref[...]).astype(BF16)

    return pl.pallas_call(
        body, name=name, grid=(s // tm,),
        in_specs=[pl.BlockSpec((tm, d), lambda i: (i, 0)), pl.BlockSpec((1, d), lambda i: (0, 0))],
        out_specs=pl.BlockSpec((tm, d), lambda i: (i, 0)),
        out_shape=jax.ShapeDtypeStruct((s, d), BF16),
        compiler_params=_params(("parallel",)),
    )(x, g.reshape(1, d))


def rmsnorm_bwd(dh, x, g, dres, *, name):
    s, d = x.shape
    tm = 256

    def body(dh_ref, x_ref, g_ref, dres_ref, dx_ref, dg_ref):
        @pl.when(pl.program_id(0) == 0)
        def _():
            dg_ref[...] = jnp.zeros_like(dg_ref)

        xv, dhv = x_ref[...], dh_ref[...]
        r = lax.rsqrt(jnp.mean(xv * xv, axis=-1, keepdims=True) + EPS)
        gd = dhv * g_ref[...]
        dot = jnp.mean(gd * xv, axis=-1, keepdims=True)
        dx_ref[...] = dres_ref[...] + (r * gd - xv * (r * r * r * dot))
        dg_ref[...] += jnp.sum(dhv * (xv * r), axis=0, keepdims=True)

    dx, dg = pl.pallas_call(
        body, name=name, grid=(s // tm,),
        in_specs=[pl.BlockSpec((tm, d), lambda i: (i, 0)), pl.BlockSpec((tm, d), lambda i: (i, 0)),
                  pl.BlockSpec((1, d), lambda i: (0, 0)), pl.BlockSpec((tm, d), lambda i: (i, 0))],
        out_specs=[pl.BlockSpec((tm, d), lambda i: (i, 0)), pl.BlockSpec((1, d), lambda i: (0, 0))],
        out_shape=[jax.ShapeDtypeStruct((s, d), F32), jax.ShapeDtypeStruct((1, d), F32)],
        compiler_params=_params(("arbitrary",)),
    )(dh, x, g.reshape(1, d), dres)
    return dx, dg[0]


def loss_head(y, target, *, name):
    s, d = y.shape
    tm = 512

    def body(y_ref, t_ref, dy_ref, l_ref):
        @pl.when(pl.program_id(0) == 0)
        def _():
            l_ref[...] = jnp.zeros_like(l_ref)

        e = y_ref[...] - t_ref[...]
        dy_ref[...] = e / float(d)
        per_tok = jnp.mean(e * e, axis=-1, keepdims=True)
        l_ref[...] += 0.5 * jnp.sum(per_tok, axis=0, keepdims=True)

    dy, l = pl.pallas_call(
        body, name=name, grid=(s // tm,),
        in_specs=[pl.BlockSpec((tm, d), lambda i: (i, 0)), pl.BlockSpec((tm, d), lambda i: (i, 0))],
        out_specs=[pl.BlockSpec((tm, d), lambda i: (i, 0)), pl.BlockSpec((8, LANES), lambda i: (0, 0))],
        out_shape=[jax.ShapeDtypeStruct((s, d), F32), jax.ShapeDtypeStruct((8, LANES), F32)],
        compiler_params=_params(("arbitrary",)),
    )(y, target)
    return dy, l[0, 0]


FFN_TN = 256
FFN_CH = 256


def _rows_before(ref, r0, first):
    if first:
        cur = ref[pl.ds(0, FFN_CH), :]
        row = lax.broadcasted_iota(I32, cur.shape, 0)
        sh1 = jnp.where(row < 1, 0.0, pltpu.roll(cur, 1, axis=0))
        sh2 = jnp.where(row < 2, 0.0, pltpu.roll(cur, 2, axis=0))
        return cur, sh1, sh2
    ext = ref[pl.ds(pl.multiple_of(r0 - 8, 8), FFN_CH + 8), :]
    return ext[8:], pltpu.roll(ext, 1, axis=0)[8:], pltpu.roll(ext, 2, axis=0)[8:]


def _rows_after(ref, r0, last):
    if last:
        cur = ref[pl.ds(r0, FFN_CH), :]
        row = lax.broadcasted_iota(I32, cur.shape, 0)
        up1 = jnp.where(row >= FFN_CH - 1, 0.0, pltpu.roll(cur, FFN_CH - 1, axis=0))
        up2 = jnp.where(row >= FFN_CH - 2, 0.0, pltpu.roll(cur, FFN_CH - 2, axis=0))
        return cur, up1, up2
    n = FFN_CH + 8
    ext = ref[pl.ds(r0, n), :]
    return ext[:FFN_CH], pltpu.roll(ext, n - 1, axis=0)[:FFN_CH], pltpu.roll(ext, n - 2, axis=0)[:FFN_CH]


def _sigmoid(x):
    return 1.0 / (1.0 + jnp.exp(-x))


def ffn_act_fwd(p, conv_w, conv_b, *, name):
    s, f2 = p.shape
    f = f2 // 2
    nj = f // FFN_TN
    nch = s // FFN_CH

    def body(pg_ref, pu_ref, wg_ref, wu_ref, bg_ref, bu_ref, a_ref):
        def conv(ref, w_ref, b_ref, r0, first):
            cur, sh1, sh2 = _rows_before(ref, r0, first)
            return ((b_ref[...] + w_ref[0:1, :] * sh2) + w_ref[1:2, :] * sh1) + w_ref[2:3, :] * cur

        def chunk(r0, first):
            gate = conv(pg_ref, wg_ref, bg_ref, r0, first)
            up = conv(pu_ref, wu_ref, bu_ref, r0, first)
            a_ref[pl.ds(r0, FFN_CH), :] = (gate * _sigmoid(gate) * up).astype(BF16)

        chunk(0, True)

        def step(c, carry):
            chunk(pl.multiple_of(c * FFN_CH, FFN_CH), False)
            return carry

        lax.fori_loop(1, nch, step, 0)

    col = lambda off: pl.BlockSpec((s, FFN_TN), lambda j: (0, j + off))
    wcol = lambda off: pl.BlockSpec((3, FFN_TN), lambda j: (0, j + off))
    bcol = lambda off: pl.BlockSpec((1, FFN_TN), lambda j: (0, j + off))
    return pl.pallas_call(
        body, name=name, grid=(nj,),
        in_specs=[col(0), col(nj), wcol(0), wcol(nj), bcol(0), bcol(nj)],
        out_specs=pl.BlockSpec((s, FFN_TN), lambda j: (0, j)),
        out_shape=jax.ShapeDtypeStruct((s, f), BF16),
        compiler_params=_params(("parallel",), VMEM_LIMIT),
    )(p, p, conv_w, conv_w, conv_b.reshape(1, f2), conv_b.reshape(1, f2))


def ffn_act_bwd(da, p, conv_w, conv_b, *, name):
    s, f2 = p.shape
    f = f2 // 2
    nj = f // FFN_TN
    nch = s // FFN_CH

    def body(da_ref, pg_ref, pu_ref, wg_ref, wu_ref, bg_ref, bu_ref,
             dpg_ref, dpu_ref, dwg_ref, dwu_ref, dbg_ref, dbu_ref, dug_s, duu_s):
        def conv(ref, w_ref, b_ref, r0, first):
            cur, sh1, sh2 = _rows_before(ref, r0, first)
            u = ((b_ref[...] + w_ref[0:1, :] * sh2) + w_ref[1:2, :] * sh1) + w_ref[2:3, :] * cur
            return u, (sh2, sh1, cur)

        def taps_sum(du, taps):
            return jnp.concatenate([jnp.sum(du * t, axis=0, keepdims=True) for t in taps], axis=0)

        def chunk(r0, first, acc):
            dwg, dwu, dbg, dbu = acc
            gate, tg = conv(pg_ref, wg_ref, bg_ref, r0, first)
            up, tu = conv(pu_ref, wu_ref, bu_ref, r0, first)
            dav = da_ref[pl.ds(r0, FFN_CH), :]
            sg = _sigmoid(gate)
            dgate = dav * up * (sg * (1.0 + gate * (1.0 - sg)))
            dup = dav * (gate * sg)
            dug_s[pl.ds(r0, FFN_CH), :] = dgate
            duu_s[pl.ds(r0, FFN_CH), :] = dup
            return (dwg + taps_sum(dgate, tg), dwu + taps_sum(dup, tu),
                    dbg + jnp.sum(dgate, axis=0, keepdims=True), dbu + jnp.sum(dup, axis=0, keepdims=True))

        z3 = jnp.zeros((3, FFN_TN), F32)
        z1 = jnp.zeros((1, FFN_TN), F32)
        acc = chunk(0, True, (z3, z3, z1, z1))
        acc = lax.fori_loop(1, nch, lambda c, a: chunk(pl.multiple_of(c * FFN_CH, FFN_CH), False, a), acc)
        dwg_ref[...], dwu_ref[...], dbg_ref[...], dbu_ref[...] = acc

        def back(src, w_ref, dst, r0, last):
            cur, up1, up2 = _rows_after(src, r0, last)
            dst[pl.ds(r0, FFN_CH), :] = (w_ref[2:3, :] * cur + w_ref[1:2, :] * up1 + w_ref[0:1, :] * up2).astype(BF16)

        def step(c, carry):
            r0 = pl.multiple_of(c * FFN_CH, FFN_CH)
            back(dug_s, wg_ref, dpg_ref, r0, False)
            back(duu_s, wu_ref, dpu_ref, r0, False)
            return carry

        lax.fori_loop(0, nch - 1, step, 0)
        back(dug_s, wg_ref, dpg_ref, (nch - 1) * FFN_CH, True)
        back(duu_s, wu_ref, dpu_ref, (nch - 1) * FFN_CH, True)

    col = lambda off: pl.BlockSpec((s, FFN_TN), lambda j: (0, j + off))
    wcol = lambda off: pl.BlockSpec((3, FFN_TN), lambda j: (0, j + off))
    bcol = lambda off: pl.BlockSpec((1, FFN_TN), lambda j: (0, j + off))
    outs = pl.pallas_call(
        body, name=name, grid=(nj,),
        in_specs=[col(0), col(0), col(nj), wcol(0), wcol(nj), bcol(0), bcol(nj)],
        out_specs=[col(0), col(0), wcol(0), wcol(0), bcol(0), bcol(0)],
        out_shape=[jax.ShapeDtypeStruct((s, f), BF16), jax.ShapeDtypeStruct((s, f), BF16),
                   jax.ShapeDtypeStruct((3, f), F32), jax.ShapeDtypeStruct((3, f), F32),
                   jax.ShapeDtypeStruct((1, f), F32), jax.ShapeDtypeStruct((1, f), F32)],
        scratch_shapes=[pltpu.VMEM((s, FFN_TN), F32), pltpu.VMEM((s, FFN_TN), F32)],
        compiler_params=_params(("parallel",), VMEM_LIMIT),
    )(da, p, p, conv_w, conv_w, conv_b.reshape(1, f2), conv_b.reshape(1, f2))
    dpg, dpu, dwg, dwu, dbg, dbu = outs
    return (jnp.concatenate([dpg, dpu], axis=1), jnp.concatenate([dwg, dwu], axis=1),
            jnp.concatenate([dbg, dbu], axis=1)[0])


def mixnorm_fwd(outs, gain, *, name):
    s = outs[0].shape[0]
    widths = [o.shape[1] for o in outs]
    total = sum(widths)
    tm = 512

    def body(*refs):
        o_refs, g_ref, m_ref = refs[:-2], refs[-2], refs[-1]
        off = 0
        for o_ref, w in zip(o_refs, widths):
            xv = o_ref[...]
            r = lax.rsqrt(jnp.mean(xv * xv, axis=-1, keepdims=True) + EPS)
            m_ref[:, off:off + w] = (xv * r * g_ref[:, off:off + w]).astype(BF16)
            off += w

    return pl.pallas_call(
        body, name=name, grid=(s // tm,),
        in_specs=[pl.BlockSpec((tm, w), lambda i: (i, 0)) for w in widths] + [pl.BlockSpec((1, total), lambda i: (0, 0))],
        out_specs=pl.BlockSpec((tm, total), lambda i: (i, 0)),
        out_shape=jax.ShapeDtypeStruct((s, total), BF16),
        compiler_params=_params(("parallel",)),
    )(*outs, gain.reshape(1, total))


def mixnorm_bwd(dmix, outs, gain, *, name):
    s = outs[0].shape[0]
    widths = [o.shape[1] for o in outs]
    total = sum(widths)
    n = len(outs)
    tm = 256

    def body(*refs):
        dm_ref, o_refs, g_ref = refs[0], refs[1:1 + n], refs[1 + n]
        d_refs, dg_ref = refs[2 + n:2 + 2 * n], refs[2 + 2 * n]

        @pl.when(pl.program_id(0) == 0)
        def _():
            dg_ref[...] = jnp.zeros_like(dg_ref)

        off = 0
        for o_ref, d_ref, w in zip(o_refs, d_refs, widths):
            xv = o_ref[...]
            dhv = dm_ref[:, off:off + w]
            r = lax.rsqrt(jnp.mean(xv * xv, axis=-1, keepdims=True) + EPS)
            gd = dhv * g_ref[:, off:off + w]
            dot = jnp.mean(gd * xv, axis=-1, keepdims=True)
            d_ref[...] = r * gd - xv * (r * r * r * dot)
            dg_ref[:, off:off + w] += jnp.sum(dhv * (xv * r), axis=0, keepdims=True)
            off += w

    res = pl.pallas_call(
        body, name=name, grid=(s // tm,),
        in_specs=[pl.BlockSpec((tm, total), lambda i: (i, 0))]
        + [pl.BlockSpec((tm, w), lambda i: (i, 0)) for w in widths] + [pl.BlockSpec((1, total), lambda i: (0, 0))],
        out_specs=[pl.BlockSpec((tm, w), lambda i: (i, 0)) for w in widths] + [pl.BlockSpec((1, total), lambda i: (0, 0))],
        out_shape=[jax.ShapeDtypeStruct((s, w), F32) for w in widths] + [jax.ShapeDtypeStruct((1, total), F32)],
        compiler_params=_params(("arbitrary",)),
    )(dmix, *outs, gain.reshape(1, total))
    return res[:n], res[n][0]


NORM_CH = 512
FWD_TILES = 4
BWD_TILES = 2


def _lo_mask(shape):
    return lax.broadcasted_iota(I32, shape, 1) < HEAD_DIM


def _head_sum(x, lo):
    del lo
    i = lax.broadcasted_iota(I32, (LANES, LANES), 0) // HEAD_DIM
    j = lax.broadcasted_iota(I32, (LANES, LANES), 1) // HEAD_DIM
    same = (i == j).astype(BF16)
    hi = x.astype(BF16)
    r1 = x - hi.astype(F32)
    mid = r1.astype(BF16)
    low = (r1 - mid.astype(F32)).astype(BF16)
    return _dot(hi, same) + _dot(mid, same) + _dot(low, same)


def _head_stats(x, lo):
    return lax.rsqrt(_head_sum(x * x, lo) * (1.0 / HEAD_DIM) + EPS)


def _swap_halves(x):
    return pltpu.roll(x, HEAD_DIM, axis=1)


def _replicate_head(x, lo, use_lo_head):
    sw = _swap_halves(x)
    return jnp.where(use_lo_head, jnp.where(lo, x, sw), jnp.where(lo, sw, x))


def _tile_rows(i, s, d):
    nb = s // (BLOCK * d)
    r = i // nb
    b = i % nb
    start = r + (BLOCK * d) * b
    prev = start - (BLOCK * d) * jnp.minimum(b, 1)
    return start, prev, b > 0


def _rows(ref, start, d):
    if d == 1:
        return ref[pl.ds(pl.multiple_of(start, BLOCK), BLOCK), :]
    return ref[pl.ds(start, BLOCK, stride=d), :]


def _set_rows(ref, start, d, val):
    if d == 1:
        ref[pl.ds(pl.multiple_of(start, BLOCK), BLOCK), :] = val
    else:
        ref[pl.ds(start, BLOCK, stride=d), :] = val


def banded_fwd(proj, qb0, kb0, vb0, n_slabs, gq, gk, bias, dils, sinks, gqa, *, name):
    s = proj.shape[0]
    nbr = len(dils)
    nt = s // BLOCK
    nch = s // NORM_CH
    has_sink = sinks is not None

    def body(*refs):
        q_ref, k_ref, v_ref, gq_ref, gk_ref, b_ref = refs[:6]
        rest = refs[6:]
        if has_sink:
            sink_ref, rest = rest[0], rest[1:]
        out_ref, lse_ref, qn_s, kn_s, vv_s, o_s, l_s = rest
        p = pl.program_id(0)
        use_lo = (p // 2) == 0

        def prep(c, carry):
            rows = pl.ds(pl.multiple_of(c * NORM_CH, NORM_CH), NORM_CH)
            lo = _lo_mask((NORM_CH, LANES))
            qv, kv, vv = q_ref[rows, :], k_ref[rows, :], v_ref[rows, :]
            qn_s[rows, :] = qv * _head_stats(qv, lo) * gq_ref[...] * (HEAD_DIM ** -0.5)
            kn = kv * _head_stats(kv, lo) * gk_ref[...]
            if gqa:
                kn = _replicate_head(kn, lo, use_lo)
                vv = _replicate_head(vv, lo, use_lo)
            kn_s[rows, :] = kn
            vv_s[rows, :] = vv
            return carry

        lax.fori_loop(0, nch, prep, 0)

        lo = _lo_mask((BLOCK, LANES))
        hms = [lo, jnp.logical_not(lo)]
        heads, tiles = range(2), range(FWD_TILES)
        for br, d in enumerate(dils):
            def step(ii, carry, br=br, d=d):
                pos = [_tile_rows(ii * FWD_TILES + u, s, d) for u in tiles]
                kcat = [jnp.concatenate([_rows(kn_s, pos[u][1], d), _rows(kn_s, pos[u][0], d)], axis=0).astype(BF16)
                        for u in tiles]
                vcat = [jnp.concatenate([_rows(vv_s, pos[u][1], d), _rows(vv_s, pos[u][0], d)], axis=0).astype(BF16)
                        for u in tiles]
                qt = [_rows(qn_s, pos[u][0], d) for u in tiles]
                sc = [[_dot_nt(jnp.where(hms[h], qt[u], 0.0).astype(BF16), kcat[u])
                       + b_ref[br, jnp.where(pos[u][2], 0, 1), h] for h in heads] for u in tiles]
                m = [[jnp.max(sc[u][h], axis=1, keepdims=True) for h in heads] for u in tiles]
                pe = [[jnp.exp(sc[u][h] - m[u][h]) for h in heads] for u in tiles]
                den = [[jnp.sum(pe[u][h], axis=1, keepdims=True) for h in heads] for u in tiles]
                o = [[_dot(pe[u][h].astype(BF16), vcat[u]) * (1.0 / den[u][h]) for h in heads] for u in tiles]
                for u in tiles:
                    _set_rows(o_s.at[br], pos[u][0], d, jnp.where(lo, o[u][0], o[u][1]))
                    _set_rows(l_s.at[br], pos[u][0], d,
                              jnp.where(lo, m[u][0] + jnp.log(den[u][0]), m[u][1] + jnp.log(den[u][1])))
                return carry

            lax.fori_loop(0, nt // FWD_TILES, step, 0)

        def combine(c, carry):
            rows = pl.ds(pl.multiple_of(c * NORM_CH, NORM_CH), NORM_CH)
            ls = [l_s[br, rows, :] for br in range(nbr)]
            mx = functools.reduce(jnp.maximum, ls)
            if has_sink:
                mx = jnp.maximum(mx, sink_ref[...])
            tot = functools.reduce(jnp.add, [jnp.exp(l - mx) for l in ls])
            if has_sink:
                tot = tot + jnp.exp(sink_ref[...] - mx)
            lse = mx + jnp.log(tot)
            acc = jnp.exp(ls[0] - lse) * o_s[0, rows, :]
            for br in range(1, nbr):
                acc = acc + jnp.exp(ls[br] - lse) * o_s[br, rows, :]
            out_ref[rows, :] = acc
            lse_ref[rows, :] = lse
            return carry

        lax.fori_loop(0, nch, combine, 0)

    slab = lambda b0, shared: pl.BlockSpec((s, LANES), (lambda p: (0, b0)) if shared else (lambda p: (0, b0 + p)),
                                           pipeline_mode=pl.Buffered(1))
    vec = pl.BlockSpec((1, LANES), lambda p: (0, 0))
    in_specs = [slab(qb0, False), slab(kb0, gqa), slab(vb0, gqa), vec, vec,
                pl.BlockSpec((nbr, 2, 2, BLOCK, 2 * BLOCK), lambda p: (0, 0, p, 0, 0))]
    args = [proj, proj, proj, gq.reshape(1, LANES), gk.reshape(1, LANES), bias]
    if has_sink:
        in_specs.append(pl.BlockSpec((None, 1, LANES), lambda p: (p, 0, 0)))
        args.append(sinks)
    w = LANES * n_slabs
    return pl.pallas_call(
        body, name=name, grid=(n_slabs,),
        in_specs=in_specs,
        out_specs=[pl.BlockSpec((s, LANES), lambda p: (0, p)), pl.BlockSpec((s, LANES), lambda p: (0, p))],
        out_shape=[jax.ShapeDtypeStruct((s, w), F32), jax.ShapeDtypeStruct((s, w), F32)],
        scratch_shapes=[pltpu.VMEM((s, LANES), F32), pltpu.VMEM((s, LANES), F32), pltpu.VMEM((s, LANES), F32),
                        pltpu.VMEM((nbr, s, LANES), F32), pltpu.VMEM((nbr, s, LANES), F32)],
        compiler_params=_params(("parallel",), VMEM_LIMIT),
    )(*args)


def banded_bwd(proj, qb0, kb0, vb0, n_slabs, gq, gk, bias, dils, sinks, gqa, dout, out, lse, *, name):
    s = proj.shape[0]
    nbr = len(dils)
    nt = s // BLOCK
    nch = s // NORM_CH
    has_sink = sinks is not None
    scale = HEAD_DIM ** -0.5

    def body(*refs):
        q_ref, k_ref, v_ref, gq_ref, gk_ref, b_ref, do_ref, o_ref, lse_ref = refs[:9]
        rest = refs[9:]
        if has_sink:
            sink_ref, rest = rest[0], rest[1:]
        dq_ref, dk_ref, dv_ref, db_ref, dgq_ref, dgk_ref = rest[:6]
        rest = rest[6:]
        if has_sink:
            dsink_ref, rest = rest[0], rest[1:]
        qn_s, kn_s, vv_s, dl_s, dqn_s, dkn_s, dvv_s = rest
        p = pl.program_id(0)
        use_lo = (p // 2) == 0

        def prep(c, carry):
            rows = pl.ds(pl.multiple_of(c * NORM_CH, NORM_CH), NORM_CH)
            lo = _lo_mask((NORM_CH, LANES))
            qv, kv, vv = q_ref[rows, :], k_ref[rows, :], v_ref[rows, :]
            qn_s[rows, :] = qv * _head_stats(qv, lo) * gq_ref[...] * scale
            kn = kv * _head_stats(kv, lo) * gk_ref[...]
            if gqa:
                kn = _replicate_head(kn, lo, use_lo)
                vv = _replicate_head(vv, lo, use_lo)
            kn_s[rows, :] = kn
            vv_s[rows, :] = vv
            delta = _head_sum(do_ref[rows, :] * o_ref[rows, :], lo)
            dl_s[rows, :] = delta
            z = jnp.zeros((NORM_CH, LANES), F32)
            dqn_s[rows, :] = z
            dkn_s[rows, :] = z
            dvv_s[rows, :] = z
            if has_sink:
                ps = jnp.exp(sink_ref[...] - lse_ref[rows, :])
                return carry - jnp.sum(ps * delta, axis=0, keepdims=True)
            return carry

        dsink = lax.fori_loop(0, nch, prep, jnp.zeros((1, LANES), F32))
        if has_sink:
            dsink_ref[...] = jnp.broadcast_to(dsink, (8, LANES))

        lo = _lo_mask((BLOCK, LANES))
        hms = [lo, jnp.logical_not(lo)]
        heads, tiles = range(2), range(BWD_TILES)
        for br, d in enumerate(dils):
            db_ref[br] = jnp.zeros((2, BLOCK, 2 * BLOCK), F32)

            def step(ii, carry, br=br, d=d):
                pos = [_tile_rows(ii * BWD_TILES + u, s, d) for u in tiles]
                kcat = [jnp.concatenate([_rows(kn_s, pos[u][1], d), _rows(kn_s, pos[u][0], d)], axis=0).astype(BF16)
                        for u in tiles]
                vcat = [jnp.concatenate([_rows(vv_s, pos[u][1], d), _rows(vv_s, pos[u][0], d)], axis=0).astype(BF16)
                        for u in tiles]
                qt = [_rows(qn_s, pos[u][0], d) for u in tiles]
                dot_ = [_rows(do_ref, pos[u][0], d) for u in tiles]
                lse_t = [_rows(lse_ref, pos[u][0], d) for u in tiles]
                dl_t = [_rows(dl_s, pos[u][0], d) for u in tiles]
                qh = [[jnp.where(hms[h], qt[u], 0.0).astype(BF16) for h in heads] for u in tiles]
                doh = [[jnp.where(hms[h], dot_[u], 0.0).astype(BF16) for h in heads] for u in tiles]
                sc = [[_dot_nt(qh[u][h], kcat[u]) + b_ref[br, jnp.where(pos[u][2], 0, 1), h] for h in heads]
                      for u in tiles]
                dp = [[_dot_nt(doh[u][h], vcat[u]) for h in heads] for u in tiles]
                lane0 = [0, HEAD_DIM]
                pr = [[jnp.exp(sc[u][h] - lse_t[u][:, lane0[h]:lane0[h] + 1]) for h in heads] for u in tiles]
                dlog = [[pr[u][h] * (dp[u][h] - dl_t[u][:, lane0[h]:lane0[h] + 1]) for h in heads] for u in tiles]
                for h in heads:
                    db_ref[br, h] += functools.reduce(jnp.add, [dlog[u][h] for u in tiles])
                dlb = [[dlog[u][h].astype(BF16) for h in heads] for u in tiles]
                prb = [[pr[u][h].astype(BF16) for h in heads] for u in tiles]
                dq_t = [jnp.where(lo, _dot(dlb[u][0], kcat[u]), _dot(dlb[u][1], kcat[u])) * scale for u in tiles]
                dk_t = [_dot_tn(dlb[u][0], qh[u][0]) + _dot_tn(dlb[u][1], qh[u][1]) for u in tiles]
                dv_t = [_dot_tn(prb[u][0], doh[u][0]) + _dot_tn(prb[u][1], doh[u][1]) for u in tiles]
                for u in tiles:
                    start, prev = pos[u][0], pos[u][1]
                    _set_rows(dqn_s, start, d, _rows(dqn_s, start, d) + dq_t[u])
                    _set_rows(dkn_s, prev, d, _rows(dkn_s, prev, d) + dk_t[u][:BLOCK])
                    _set_rows(dkn_s, start, d, _rows(dkn_s, start, d) + dk_t[u][BLOCK:])
                    _set_rows(dvv_s, prev, d, _rows(dvv_s, prev, d) + dv_t[u][:BLOCK])
                    _set_rows(dvv_s, start, d, _rows(dvv_s, start, d) + dv_t[u][BLOCK:])
                return carry

            lax.fori_loop(0, nt // BWD_TILES, step, 0)

        if gqa:
            @pl.when(p == 0)
            def _():
                dk_ref[...] = jnp.zeros_like(dk_ref)
                dv_ref[...] = jnp.zeros_like(dv_ref)

        def finish(c, carry):
            dgq, dgk = carry
            rows = pl.ds(pl.multiple_of(c * NORM_CH, NORM_CH), NORM_CH)
            lo = _lo_mask((NORM_CH, LANES))

            def norm_bwd(xv, dn, g_ref):
                r = _head_stats(xv, lo)
                gd = dn * g_ref[...]
                dot = _head_sum(gd * xv, lo) * (1.0 / HEAD_DIM)
                return r * gd - xv * (r * r * r * dot), dn * (xv * r)

            dq, gq_part = norm_bwd(q_ref[rows, :], dqn_s[rows, :], gq_ref)
            dq_ref[rows, :] = dq
            dgq = dgq + jnp.sum(gq_part, axis=0, keepdims=True)
            kv, dkn, dvv = k_ref[rows, :], dkn_s[rows, :], dvv_s[rows, :]
            if gqa:
                kv = _replicate_head(kv, lo, use_lo)
                dkn = dkn + _swap_halves(dkn)
                dvv = dvv + _swap_halves(dvv)
                lane = lax.broadcasted_iota(I32, (NORM_CH, LANES), 1)
                mine = (lane // HEAD_DIM) == (p // 2)
                dk, gk_part = norm_bwd(kv, dkn, gk_ref)
                dk_ref[rows, :] += jnp.where(mine, dk, 0.0)
                dv_ref[rows, :] += jnp.where(mine, dvv, 0.0)
                gk_part = jnp.where(lo, gk_part, 0.0)
            else:
                dk, gk_part = norm_bwd(kv, dkn, gk_ref)
                dk_ref[rows, :] = dk
                dv_ref[rows, :] = dvv
            dgk = dgk + jnp.sum(gk_part, axis=0, keepdims=True)
            return dgq, dgk

        z = jnp.zeros((1, LANES), F32)
        dgq, dgk = lax.fori_loop(0, nch, finish, (z, z))
        dgq_ref[...] = jnp.broadcast_to(dgq, (8, LANES))
        dgk_ref[...] = jnp.broadcast_to(dgk, (8, LANES))

    def slab_of(width_blocks, b0, shared):
        return pl.BlockSpec((s, LANES), (lambda p: (0, b0)) if shared else (lambda p: (0, b0 + p)),
                            pipeline_mode=pl.Buffered(1))

    vec = pl.BlockSpec((1, LANES), lambda p: (0, 0))
    own = pl.BlockSpec((s, LANES), lambda p: (0, p), pipeline_mode=pl.Buffered(1))
    in_specs = [slab_of(0, qb0, False), slab_of(0, kb0, gqa), slab_of(0, vb0, gqa), vec, vec,
                pl.BlockSpec((nbr, 2, 2, BLOCK, 2 * BLOCK), lambda p: (0, 0, p, 0, 0)), own, own, own]
    args = [proj, proj, proj, gq.reshape(1, LANES), gk.reshape(1, LANES), bias, dout, out, lse]
    if has_sink:
        in_specs.append(pl.BlockSpec((None, 1, LANES), lambda p: (p, 0, 0)))
        args.append(sinks)
    w = LANES * n_slabs
    kvw = LANES if gqa else w
    kv_spec = pl.BlockSpec((s, LANES), (lambda p: (0, 0)) if gqa else (lambda p: (0, p)))
    part = pl.BlockSpec((None, 8, LANES), lambda p: (p, 0, 0))
    out_specs = [pl.BlockSpec((s, LANES), lambda p: (0, p)), kv_spec, kv_spec,
                 pl.BlockSpec((nbr, 2, BLOCK, 2 * BLOCK), lambda p: (0, p, 0, 0)), part, part]
    out_shape = [jax.ShapeDtypeStruct((s, w), F32), jax.ShapeDtypeStruct((s, kvw), F32),
                 jax.ShapeDtypeStruct((s, kvw), F32),
                 jax.ShapeDtypeStruct((nbr, 2 * n_slabs, BLOCK, 2 * BLOCK), F32),
                 jax.ShapeDtypeStruct((n_slabs, 8, LANES), F32), jax.ShapeDtypeStruct((n_slabs, 8, LANES), F32)]
    if has_sink:
        out_specs.append(part)
        out_shape.append(jax.ShapeDtypeStruct((n_slabs, 8, LANES), F32))
    res = pl.pallas_call(
        body, name=name, grid=(n_slabs,),
        in_specs=in_specs, out_specs=out_specs, out_shape=out_shape,
        scratch_shapes=[pltpu.VMEM((s, LANES), F32) for _ in range(7)],
        compiler_params=_params(("arbitrary",), VMEM_LIMIT),
    )(*args)
    dq, dk, dv, db, dgq, dgk = res[:6]
    outs = [dq, dk, dv, db, dgq[:, 0, :], dgk[:, 0, :]]
    if has_sink:
        outs.append(res[6][:, 0, :])
    return outs


def bias_bwd(dbias, buckets, *, name):
    nbr, h = dbias.shape[:2]

    def body(db_ref, bk_ref, o_ref):
        lane = lax.broadcasted_iota(I32, (1, LANES), 1)
        acc = jnp.zeros((1, LANES), F32)
        for b in range(N_BUCKETS):
            tot = jnp.zeros((1, 1), F32)
            for br in range(nbr):
                sel = jnp.where(bk_ref[br] == b, db_ref[br], 0.0)
                tot = tot + jnp.sum(jnp.sum(sel, axis=0, keepdims=True), axis=1, keepdims=True)
            acc = jnp.where(lane == b, tot, acc)
        o_ref[...] = jnp.broadcast_to(acc, (8, LANES))

    res = pl.pallas_call(
        body, name=name, grid=(h,),
        in_specs=[pl.BlockSpec((nbr, None, BLOCK, 2 * BLOCK), lambda i: (0, i, 0, 0)),
                  pl.BlockSpec((nbr, BLOCK, 2 * BLOCK), lambda i: (0, 0, 0))],
        out_specs=pl.BlockSpec((None, 8, LANES), lambda i: (i, 0, 0)),
        out_shape=jax.ShapeDtypeStruct((h, 8, LANES), F32),
        compiler_params=_params(("parallel",)),
    )(dbias, buckets)
    return res[:, 0, :N_BUCKETS].T


SB_KG = 512


def _softplus(z):
    return jnp.maximum(z, 0.0) + jnp.log(1.0 + jnp.exp(-jnp.abs(z)))


def _split_dot(x, t):
    hi = x.astype(BF16)
    lo = (x - hi.astype(F32)).astype(BF16)
    return _dot(hi, t) + _dot(lo, t)


def sb_fwd(proj, qb0, kb0, vb0, n_slabs, *, name):
    s = proj.shape[0]
    nq = s // BLOCK
    nch = s // NORM_CH
    scale = HEAD_DIM ** -0.5

    def body(q_ref, k_ref, v_ref, o_ref, tot_ref, qlo_s, qhi_s, k_s, v_s):
        def prep(c, carry):
            rows = pl.ds(pl.multiple_of(c * NORM_CH, NORM_CH), NORM_CH)
            lo = _lo_mask((NORM_CH, LANES))
            qv = q_ref[rows, :] * scale
            qlo_s[rows, :] = jnp.where(lo, qv, 0.0).astype(BF16)
            qhi_s[rows, :] = jnp.where(lo, 0.0, qv).astype(BF16)
            k_s[rows, :] = k_ref[rows, :].astype(BF16)
            v_s[rows, :] = v_ref[rows, :].astype(BF16)
            return carry

        lax.fori_loop(0, nch, prep, 0)

        row = lax.broadcasted_iota(I32, (BLOCK, BLOCK), 0)
        col = lax.broadcasted_iota(I32, (BLOCK, BLOCK), 1)
        lo = col < HEAD_DIM
        t_ge = (row >= col).astype(BF16)
        rowg = lax.broadcasted_iota(I32, (BLOCK, SB_KG), 0)
        colg = lax.broadcasted_iota(I32, (BLOCK, SB_KG), 1)

        nsub = SB_KG // BLOCK
        heads = range(2)

        def qloop(qi, carry):
            q0 = pl.multiple_of(qi * BLOCK, BLOCK)
            qh = [qlo_s[pl.ds(q0, BLOCK), :], qhi_s[pl.ds(q0, BLOCK), :]]
            gd = qi // nsub

            def logits(gi):
                k0 = pl.multiple_of(gi * SB_KG, SB_KG)
                kg = k_s[pl.ds(k0, SB_KG), :]
                return [_dot_nt(qh[h], kg) for h in heads]

            def group(gi, st, mask):
                k0 = pl.multiple_of(gi * SB_KG, SB_KG)
                vg = v_s[pl.ds(k0, SB_KG), :]
                c = [st[0], st[2]]
                z = [st[4], st[5]]
                z_next = logits(jnp.maximum(gi - 1, 0))
                sp = [_softplus(z[h]) for h in heads]
                lrem = [-sp[h] if mask is None else jnp.where(mask, -sp[h], 0.0) for h in heads]
                piece = lambda x, j: x[:, j * BLOCK:(j + 1) * BLOCK]
                incl = [[_split_dot(piece(lrem[h], j), t_ge) for j in range(nsub)] for h in heads]
                a = []
                for h in heads:
                    suffix = [None] * nsub
                    for j in reversed(range(nsub)):
                        suffix[j] = c[h] + incl[h][j] - piece(lrem[h], j)
                        c[h] = c[h] + incl[h][j][:, 0:1]
                    ah = jnp.exp(z[h] - sp[h] + jnp.concatenate(suffix, axis=1))
                    a.append((ah if mask is None else jnp.where(mask, ah, 0.0)).astype(BF16))
                o = [st[1 + 2 * h] + _dot(a[h], vg) for h in heads]
                return c[0], o[0], c[1], o[1], z_next[0], z_next[1]

            zc = jnp.zeros((BLOCK, 1), F32)
            zo = jnp.zeros((BLOCK, LANES), F32)
            st = group(gd, (zc, zo, zc, zo, *logits(gd)), (gd * SB_KG + colg) < (q0 + rowg))
            c0, o0, c1, o1, _, _ = lax.fori_loop(0, gd, lambda t, st: group(gd - 1 - t, st, None), st)
            o_ref[pl.ds(q0, BLOCK), :] = jnp.where(lo, o0, o1)
            tot_ref[pl.ds(q0, BLOCK), :] = jnp.where(lo, c0, c1)
            return carry

        lax.fori_loop(0, nq, qloop, 0)

    slab = lambda b0: pl.BlockSpec((s, LANES), lambda p: (0, b0 + p), pipeline_mode=pl.Buffered(1))
    w = LANES * n_slabs
    return pl.pallas_call(
        body, name=name, grid=(n_slabs,),
        in_specs=[slab(qb0), slab(kb0), slab(vb0)],
        out_specs=[pl.BlockSpec((s, LANES), lambda p: (0, p)), pl.BlockSpec((s, LANES), lambda p: (0, p))],
        out_shape=[jax.ShapeDtypeStruct((s, w), F32), jax.ShapeDtypeStruct((s, w), F32)],
        scratch_shapes=[pltpu.VMEM((s, LANES), BF16) for _ in range(4)],
        compiler_params=_params(("parallel",), VMEM_LIMIT),
    )(proj, proj, proj)


def sb_bwd(proj, qb0, kb0, vb0, n_slabs, dout, tot, *, name):
    s = proj.shape[0]
    nq = s // BLOCK
    nch = s // NORM_CH
    nsub = SB_KG // BLOCK
    scale = HEAD_DIM ** -0.5

    def body(q_ref, k_ref, v_ref, do_ref, tot_ref, dq_ref, dk_ref, dv_ref,
             qlo_s, qhi_s, k_s, v_s, dlo_s, dhi_s):
        def prep(c, carry):
            rows = pl.ds(pl.multiple_of(c * NORM_CH, NORM_CH), NORM_CH)
            lo = _lo_mask((NORM_CH, LANES))
            qv = q_ref[rows, :] * scale
            dv = do_ref[rows, :]
            qlo_s[rows, :] = jnp.where(lo, qv, 0.0).astype(BF16)
            qhi_s[rows, :] = jnp.where(lo, 0.0, qv).astype(BF16)
            dlo_s[rows, :] = jnp.where(lo, dv, 0.0).astype(BF16)
            dhi_s[rows, :] = jnp.where(lo, 0.0, dv).astype(BF16)
            k_s[rows, :] = k_ref[rows, :].astype(BF16)
            v_s[rows, :] = v_ref[rows, :].astype(BF16)
            z = jnp.zeros((NORM_CH, LANES), F32)
            dk_ref[rows, :] = z
            dv_ref[rows, :] = z
            return carry

        lax.fori_loop(0, nch, prep, 0)

        row = lax.broadcasted_iota(I32, (BLOCK, BLOCK), 0)
        col = lax.broadcasted_iota(I32, (BLOCK, BLOCK), 1)
        lo = col < HEAD_DIM
        t_le = (row <= col).astype(BF16)
        rowg = lax.broadcasted_iota(I32, (BLOCK, SB_KG), 0)
        colg = lax.broadcasted_iota(I32, (BLOCK, SB_KG), 1)

        heads = range(2)
        piece = lambda x, j: x[:, j * BLOCK:(j + 1) * BLOCK]

        def prefixes(x):
            return [[_split_dot(piece(x[h], j), t_le) for j in range(nsub)] for h in heads]

        def chain(pre, run, total=None):
            out = []
            for j in range(nsub):
                out.append(run + pre[j] if total is None else total - run - pre[j])
                run = run + pre[j][:, BLOCK - 1:BLOCK]
            return jnp.concatenate(out, axis=1), run

        def qloop(qi, carry):
            q0 = pl.multiple_of(qi * BLOCK, BLOCK)
            qh = [qlo_s[pl.ds(q0, BLOCK), :], qhi_s[pl.ds(q0, BLOCK), :]]
            doh = [dlo_s[pl.ds(q0, BLOCK), :], dhi_s[pl.ds(q0, BLOCK), :]]
            tot_t = tot_ref[pl.ds(q0, BLOCK), :]
            tots = [tot_t[:, 0:1], tot_t[:, HEAD_DIM:HEAD_DIM + 1]]
            gd = qi // nsub

            def logits(gi):
                kg = k_s[pl.ds(pl.multiple_of(gi * SB_KG, SB_KG), SB_KG), :]
                return [_dot_nt(qh[h], kg) for h in heads]

            def group(gi, st, mask):
                k0 = pl.multiple_of(gi * SB_KG, SB_KG)
                kg, vg = k_s[pl.ds(k0, SB_KG), :], v_s[pl.ds(k0, SB_KG), :]
                cp, cg = [st[0], st[2]], [st[1], st[3]]
                z = [st[5], st[6]]
                z_next = logits(jnp.minimum(gi + 1, gd))
                da = [_dot_nt(doh[h], vg) for h in heads]
                sp = [_softplus(z[h]) for h in heads]
                lrem = [-sp[h] if mask is None else jnp.where(mask, -sp[h], 0.0) for h in heads]
                pre = prefixes(lrem)
                e, a, g = [], [], []
                for h in heads:
                    suffix, cp[h] = chain(pre[h], cp[h], tots[h])
                    e.append(z[h] - sp[h])
                    ah = jnp.exp(e[h] + suffix)
                    a.append(ah if mask is None else jnp.where(mask, ah, 0.0))
                    g.append(a[h] * da[h])
                gpre = prefixes(g)
                dz = []
                for h in heads:
                    ginc, cg[h] = chain(gpre[h], cg[h])
                    dzh = g[h] - jnp.exp(e[h]) * ginc
                    dz.append((dzh if mask is None else jnp.where(mask, dzh, 0.0)).astype(BF16))
                ab = [a[h].astype(BF16) for h in heads]
                dq = st[4] + jnp.where(lo, _dot(dz[0], kg), _dot(dz[1], kg))
                dk_ref[pl.ds(k0, SB_KG), :] += _dot_tn(dz[0], qh[0]) + _dot_tn(dz[1], qh[1])
                dv_ref[pl.ds(k0, SB_KG), :] += _dot_tn(ab[0], doh[0]) + _dot_tn(ab[1], doh[1])
                return cp[0], cg[0], cp[1], cg[1], dq, z_next[0], z_next[1]

            zc = jnp.zeros((BLOCK, 1), F32)
            st = lax.fori_loop(0, gd, lambda gi, st: group(gi, st, None),
                               (zc, zc, zc, zc, jnp.zeros((BLOCK, LANES), F32), *logits(0)))
            st = group(gd, st, (gd * SB_KG + colg) < (q0 + rowg))
            dq_ref[pl.ds(q0, BLOCK), :] = st[4] * scale
            return carry

        lax.fori_loop(0, nq, qloop, 0)

    slab = lambda b0: pl.BlockSpec((s, LANES), lambda p: (0, b0 + p), pipeline_mode=pl.Buffered(1))
    own = pl.BlockSpec((s, LANES), lambda p: (0, p), pipeline_mode=pl.Buffered(1))
    w = LANES * n_slabs
    outb = pl.BlockSpec((s, LANES), lambda p: (0, p))
    return pl.pallas_call(
        body, name=name, grid=(n_slabs,),
        in_specs=[slab(qb0), slab(kb0), slab(vb0), own, own],
        out_specs=[outb, outb, outb],
        out_shape=[jax.ShapeDtypeStruct((s, w), F32)] * 3,
        scratch_shapes=[pltpu.VMEM((s, LANES), BF16) for _ in range(6)],
        compiler_params=_params(("parallel",), VMEM_LIMIT),
    )(proj, proj, proj, dout, tot)


def _place():
    x, y, c = lax.axis_index("x"), lax.axis_index("y"), lax.axis_index("c")
    return x, y, c


def gather_blocks(shards, *, name):
    n = len(shards)

    def body(*refs):
        in_refs, out_refs = refs[:n], refs[n:2 * n]
        send_sems, recv_sems, local_sems = refs[2 * n:]
        x, y, c = _place()
        me, sibling = (x, y, c), (x, y, 1 - c)
        chips = [(1 - x, y), (x, 1 - y), (1 - x, 1 - y)]

        def blk(a, place):
            return out_refs[a].at[4 * place[0] + 2 * place[1] + place[2]]

        def copy(a, k, block, to, src=None):
            return pltpu.make_async_remote_copy(
                src_ref=blk(a, block) if src is None else src, dst_ref=blk(a, block),
                send_sem=send_sems.at[a, k], recv_sem=recv_sems.at[a, k], device_id=to, device_id_type=MESH)

        mine = [pltpu.make_async_copy(in_refs[a], blk(a, me), local_sems.at[a]) for a in range(n)]
        for cp in mine:
            cp.start()
        first = []
        for a in range(n):
            first.append(copy(a, 0, me, sibling, src=in_refs[a]))
            first += [copy(a, 1 + j, me, (*chip, c), src=in_refs[a]) for j, chip in enumerate(chips)]
        for cp in first:
            cp.start()
        passed = []
        for j, chip in enumerate(chips):
            for a in range(n):
                copy(a, 1 + j, (*chip, c), me).wait_recv()
                fw = copy(a, 4 + j, (*chip, c), sibling)
                fw.start()
                passed.append(fw)
        for a in range(n):
            copy(a, 0, sibling, me).wait_recv()
            for j, chip in enumerate(chips):
                copy(a, 4 + j, (*chip, 1 - c), me).wait_recv()
        for cp in first + passed:
            cp.wait_send()
        for cp in mine:
            cp.wait()

    any_spec = pl.BlockSpec(memory_space=pl.ANY)
    return pl.pallas_call(
        body, name=name,
        in_specs=[any_spec] * n, out_specs=[any_spec] * n,
        out_shape=[jax.ShapeDtypeStruct((N_DEV,) + sh.shape, sh.dtype) for sh in shards],
        scratch_shapes=[pltpu.SemaphoreType.DMA((n, 7)), pltpu.SemaphoreType.DMA((n, 7)),
                        pltpu.SemaphoreType.DMA((n,))],
    )(*shards)


def gather_small(v, *, name):
    m_per, n = v.shape

    def body(x_ref, out_ref, send_sems, recv_sems, local_sem):
        x, y, c = _place()
        me, sibling = (x, y, c), (x, y, 1 - c)
        chips = [(1 - x, y), (x, 1 - y), (1 - x, 1 - y)]

        def rows(px, py, pc):
            return out_ref.at[pl.ds((4 * px + 2 * py + pc) * m_per, m_per), :]

        def copy(k, block, to, src=None):
            return pltpu.make_async_remote_copy(
                src_ref=rows(*block) if src is None else src, dst_ref=rows(*block),
                send_sem=send_sems.at[k], recv_sem=recv_sems.at[k], device_id=to, device_id_type=MESH)

        mine = pltpu.make_async_copy(x_ref, rows(*me), local_sem)
        mine.start()
        first = [copy(0, me, sibling, src=x_ref)]
        first += [copy(1 + j, me, (*chip, c), src=x_ref) for j, chip in enumerate(chips)]
        for cp in first:
            cp.start()
        passed = [copy(4 + j, (*chip, c), sibling) for j, chip in enumerate(chips)]
        for j, chip in enumerate(chips):
            copy(1 + j, (*chip, c), me).wait_recv()
            passed[j].start()
        copy(0, sibling, me).wait_recv()
        for j, chip in enumerate(chips):
            copy(4 + j, (*chip, 1 - c), me).wait_recv()
        for cp in first + passed:
            cp.wait_send()
        mine.wait()

    return pl.pallas_call(
        body, name=name,
        out_shape=jax.ShapeDtypeStruct((N_DEV * m_per, n), v.dtype),
        in_specs=[pl.BlockSpec(memory_space=pltpu.VMEM)],
        out_specs=pl.BlockSpec(memory_space=pltpu.VMEM),
        scratch_shapes=[pltpu.SemaphoreType.DMA((7,)), pltpu.SemaphoreType.DMA((7,)), pltpu.SemaphoreType.DMA],
        compiler_params=_params(None, VMEM_LIMIT),
    )(v)


_HBM = pl.BlockSpec(memory_space=pltpu.HBM)
_SEM = pl.BlockSpec(memory_space=pltpu.SEMAPHORE)
_EFFECT = pltpu.SideEffectType.DATAFLOW_SIDE_EFFECTING


def _peer_copies(src_refs, land_refs, send_sems, recv_sems, per_dest):
    x, y, c = _place()
    me = 4 * x + 2 * y + c
    copies = []
    for a, (src, land) in enumerate(zip(src_refs, land_refs)):
        for k in (1, 2, 4, 3, 5, 6, 7):
            px, py, pc = x ^ (k >> 2 & 1), y ^ (k >> 1 & 1), c ^ (k & 1)
            copies.append(pltpu.make_async_remote_copy(
                src_ref=src.at[4 * px + 2 * py + pc] if per_dest else src, dst_ref=land.at[me],
                send_sem=send_sems.at[7 * a + k - 1], recv_sem=recv_sems.at[7 * a + k - 1],
                device_id=(px, py, pc), device_id_type=MESH))
    return copies


def exchange_start(srcs, lands, per_dest, *, name):
    n = len(srcs)

    def body(*refs):
        src_refs, land_refs = refs[:n], refs[n:2 * n]
        send_sems, recv_sems = refs[2 * n], refs[2 * n + 1]
        token = refs[-1]
        for cp in _peer_copies(src_refs, land_refs, send_sems, recv_sems, per_dest):
            cp.start()
        token[...] = jnp.zeros_like(token)

    hbm = lambda a: pltpu.HBM(a.shape, a.dtype)
    res = pl.pallas_call(
        body, name=name,
        out_shape=(pltpu.SemaphoreType.DMA((7 * n,)), pltpu.SemaphoreType.DMA((7 * n,)),
                   *[hbm(a) for a in srcs], *[hbm(a) for a in lands], jax.ShapeDtypeStruct((8, LANES), F32)),
        in_specs=[_HBM] * (2 * n),
        out_specs=(_SEM, _SEM, *[_HBM] * (2 * n), pl.BlockSpec(memory_space=pltpu.VMEM)),
        input_output_aliases={i: 2 + i for i in range(2 * n)},
        compiler_params=pltpu.CompilerParams(has_side_effects=_EFFECT),
    )(*[pltpu.with_memory_space_constraint(a, pltpu.HBM) for a in (*srcs, *lands)])
    return (res[0], res[1]), res[2:2 + n], res[2 + n:2 + 2 * n], res[-1]


def exchange_wait(sems, srcs, lands, per_dest, after, *, name):
    n = len(srcs)

    def body(*refs):
        src_refs, land_refs = refs[:n], refs[n:2 * n]
        send_sems, recv_sems = refs[2 * n], refs[2 * n + 1]
        for cp in _peer_copies(src_refs, land_refs, send_sems, recv_sems, per_dest):
            cp.wait_send()
            cp.wait_recv()

    hbm = lambda a: pltpu.HBM(a.shape, a.dtype)
    res = pl.pallas_call(
        body, name=name,
        out_shape=(*[hbm(a) for a in srcs], *[hbm(a) for a in lands]),
        in_specs=[*[_HBM] * (2 * n), _SEM, _SEM, pl.BlockSpec(memory_space=pl.ANY)],
        out_specs=tuple([_HBM] * (2 * n)),
        input_output_aliases={i: i for i in range(2 * n)},
        compiler_params=pltpu.CompilerParams(has_side_effects=_EFFECT),
    )(*srcs, *lands, *sems, after)
    return res[n:]


def swap_core_halves(parts, *, name):
    n = len(parts)

    def body(*refs):
        in_refs, out_refs = refs[:n], refs[n:2 * n]
        send_sems, recv_sems = refs[2 * n:]
        x, y, c = _place()
        cps = []
        for a in range(n):
            for q in range(4):
                cps.append(pltpu.make_async_remote_copy(
                    src_ref=in_refs[a].at[2 * q + (1 - c)], dst_ref=out_refs[a].at[q],
                    send_sem=send_sems.at[a, q], recv_sem=recv_sems.at[a, q],
                    device_id=(x, y, 1 - c), device_id_type=MESH))
        for cp in cps:
            cp.start()
        for cp in cps:
            cp.wait()

    any_spec = pl.BlockSpec(memory_space=pl.ANY)
    return pl.pallas_call(
        body, name=name,
        in_specs=[any_spec] * n, out_specs=[any_spec] * n,
        out_shape=[jax.ShapeDtypeStruct((4,) + p.shape[1:], p.dtype) for p in parts],
        scratch_shapes=[pltpu.SemaphoreType.DMA((n, 4)), pltpu.SemaphoreType.DMA((n, 4))],
    )(*parts)


def add_core_halves(part, recv, *, name):
    _, r, cdim = part.shape
    tr = _pick(r, (512, 256, 128, 64, 32, 16))
    c_idx = lax.axis_index("c").astype(I32).reshape(1)

    def body(c_ref, p_ref, r_ref, o_ref):
        o_ref[...] = (p_ref[...].astype(F32) + r_ref[...].astype(F32)).astype(BF16)

    return pl.pallas_call(
        body, name=name,
        grid_spec=pltpu.PrefetchScalarGridSpec(
            num_scalar_prefetch=1, grid=(4, r // tr),
            in_specs=[pl.BlockSpec((None, tr, cdim), lambda q, i, c_ref: (2 * q + c_ref[0], i, 0)),
                      pl.BlockSpec((None, tr, cdim), lambda q, i, c_ref: (q, i, 0))],
            out_specs=pl.BlockSpec((None, tr, cdim), lambda q, i, c_ref: (q, i, 0))),
        out_shape=jax.ShapeDtypeStruct((4, r, cdim), BF16),
        compiler_params=_params(("parallel", "parallel")),
    )(c_idx, part, recv)


def exchange_chips(sums, *, name):
    n = len(sums)

    def body(*refs):
        in_refs, out_refs = refs[:n], refs[n:2 * n]
        send_sems, recv_sems, local_sems = refs[2 * n:]
        x, y, c = _place()
        my_chip = 2 * x + y
        chips = [(1 - x, y), (x, 1 - y), (1 - x, 1 - y)]
        cps = []
        mine = []
        for a in range(n):
            mine.append(pltpu.make_async_copy(in_refs[a].at[my_chip], out_refs[a].at[my_chip], local_sems.at[a]))
            for j, (qx, qy) in enumerate(chips):
                cps.append(pltpu.make_async_remote_copy(
                    src_ref=in_refs[a].at[2 * qx + qy], dst_ref=out_refs[a].at[my_chip],
                    send_sem=send_sems.at[a, j], recv_sem=recv_sems.at[a, j],
                    device_id=(qx, qy, c), device_id_type=MESH))
        for cp in mine + cps:
            cp.start()
        for cp in cps:
            cp.wait()
        for cp in mine:
            cp.wait()

    any_spec = pl.BlockSpec(memory_space=pl.ANY)
    return pl.pallas_call(
        body, name=name,
        in_specs=[any_spec] * n, out_specs=[any_spec] * n,
        out_shape=[jax.ShapeDtypeStruct(p.shape, p.dtype) for p in sums],
        scratch_shapes=[pltpu.SemaphoreType.DMA((n, 3)), pltpu.SemaphoreType.DMA((n, 3)),
                        pltpu.SemaphoreType.DMA((n,))],
    )(*sums)


def _adamw_math(w, g, m, v):
    m = ADAM_B1 * m + (1.0 - ADAM_B1) * g
    v = ADAM_B2 * v + (1.0 - ADAM_B2) * (g * g)
    m_hat = m / (1.0 - ADAM_B1 ** ADAM_STEP)
    v_hat = v / (1.0 - ADAM_B2 ** ADAM_STEP)
    delta = -ADAM_LR * (m_hat / (jnp.sqrt(v_hat) + ADAM_EPS) + ADAM_WD * w)
    return delta, m, v


def adamw_parts(parts, w, m, v, layer, outs, *, name):
    depth, r, cdim = w.shape
    n_parts = parts.shape[0]
    tr = _pick(r, [t for t in (512, 256, 128, 64, 32, 16) if t * cdim <= 256 * 1024])

    def body(p_ref, w_ref, m_ref, v_ref, g0, d0, nm0, nv0, g_ref, d_ref, nm_ref, nv_ref):
        g = p_ref[0].astype(F32)
        for q in range(1, n_parts):
            g = g + p_ref[q].astype(F32)
        delta, nm, nv = _adamw_math(w_ref[...], g, m_ref[...], v_ref[...])
        g_ref[...], d_ref[...], nm_ref[...], nv_ref[...] = g, delta, nm, nv

    t = pl.BlockSpec((None, tr, cdim), lambda i: (layer, i, 0))
    held = pl.BlockSpec(memory_space=pl.ANY)
    return pl.pallas_call(
        body, name=name, grid=(r // tr,),
        in_specs=[pl.BlockSpec((n_parts, tr, cdim), lambda i: (0, i, 0)), t, t, t, held, held, held, held],
        out_specs=[t, t, t, t],
        out_shape=[jax.ShapeDtypeStruct((depth, r, cdim), F32)] * 4,
        input_output_aliases={4: 0, 5: 1, 6: 2, 7: 3},
        compiler_params=_params(("parallel",), VMEM_LIMIT),
    )(parts, w, m, v, *outs)


def sum_devices(gathered, *, name):
    m_rows = gathered.shape[1]

    def body(ga_ref, g_ref):
        g = ga_ref[0]
        for dev in range(1, N_DEV):
            g = g + ga_ref[dev]
        g_ref[...] = g

    return pl.pallas_call(
        body, name=name, out_shape=jax.ShapeDtypeStruct((m_rows, LANES), F32),
        compiler_params=_params(None, VMEM_LIMIT),
    )(gathered)


def adamw_small(g, w, m, v, *, name):
    m_rows = w.shape[0]

    def body(g_ref, w_ref, m_ref, v_ref, d_ref, nm_ref, nv_ref):
        d_ref[...], nm_ref[...], nv_ref[...] = _adamw_math(w_ref[...], g_ref[...], m_ref[...], v_ref[...])

    return pl.pallas_call(
        body, name=name, out_shape=[jax.ShapeDtypeStruct((m_rows, LANES), F32)] * 3,
        compiler_params=_params(None, VMEM_LIMIT),
    )(g, w, m, v)


def _t5_bucket(dist):
    max_exact = N_BUCKETS // 2
    d = jnp.maximum(dist, 0)
    large = max_exact + (jnp.log(jnp.maximum(d, 1).astype(F32) / max_exact)
                         / math.log(T5_MAX_DIST / max_exact) * (N_BUCKETS - max_exact)).astype(I32)
    large = jnp.minimum(large, N_BUCKETS - 1)
    return jnp.where(d < max_exact, d, large)


def _rel():
    return jnp.arange(BLOCK)[:, None] + BLOCK - jnp.arange(2 * BLOCK)[None, :]


def _band_bias(table, dils, max_dists):
    rel = _rel()
    biases, buckets = [], []
    for d, md in zip(dils, max_dists):
        bk = _t5_bucket(rel * d)
        vis = (rel >= 0) & (rel <= md)
        looked_up = jnp.zeros((table.shape[1],) + rel.shape, F32)
        for b in range(N_BUCKETS):
            looked_up = jnp.where((bk == b)[None], table[b][:, None, None], looked_up)
        with_prev = jnp.where(vis[None], looked_up, NEG_INF)
        first = jnp.arange(2 * BLOCK)[None, None, :] >= BLOCK
        biases.append(jnp.stack([with_prev, jnp.where(first, with_prev, NEG_INF)]))
        buckets.append(bk.astype(I32))
    return jnp.stack(biases), jnp.stack(buckets)


def _pack(pieces, rows):
    flat = jnp.concatenate([p.reshape(-1) for p in pieces])
    return jnp.pad(flat, (0, rows * LANES - flat.shape[0])).reshape(rows, LANES)


def _unpack(packed, shapes):
    flat = packed.reshape(-1)
    out, off = [], 0
    for sh in shapes:
        n = math.prod(sh)
        out.append(flat[off:off + n].reshape(sh))
        off += n
    return out


def _tile2(g):
    return jnp.concatenate([g, g])


def kernel(x, attn_norm, w_in, a_q_gain, a_k_gain, a_sinks, c_q_gain, c_k_gain, rel_bias_table, mix_out_gain, w_out, ffn_norm, w_up, conv_w, conv_b, w_down, loss_target, m_attn_norm, m_w_in, m_a_q_gain, m_a_k_gain, m_a_sinks, m_c_q_gain, m_c_k_gain, m_rel_bias_table, m_mix_out_gain, m_w_out, m_ffn_norm, m_w_up, m_conv_w, m_conv_b, m_w_down, v_attn_norm, v_w_in, v_a_q_gain, v_a_k_gain, v_a_sinks, v_c_q_gain, v_c_k_gain, v_rel_bias_table, v_mix_out_gain, v_w_out, v_ffn_norm, v_w_up, v_conv_w, v_conv_b, v_w_down):
    depth, d_model, in_shard = w_in.shape
    ff2_shard = w_up.shape[2]
    s = x.shape[1]
    in_width, ff2 = N_DEV * in_shard, N_DEV * ff2_shard
    n_heads = d_model // HEAD_DIM
    ha, hb, hc = n_heads // 4, n_heads // 4, n_heads // 2
    sa, sb, sc = ha // 2, hb // 2, hc // 2
    kv_a = ha // 4
    assert kv_a == 2 and BLOCK == LANES
    cb_aq, cb_ak, cb_av = 0, sa, sa + 1
    cb_bq = sa + 2
    cb_bk, cb_bv = cb_bq + sb, cb_bq + 2 * sb
    cb_cq = cb_bq + 3 * sb
    cb_ck, cb_cv = cb_cq + sc, cb_cq + 2 * sc
    assert (cb_cv + sc) * LANES == in_width
    dev = 4 * lax.axis_index("x") + 2 * lax.axis_index("y") + lax.axis_index("c")

    def landing(own):
        return lax.dynamic_update_slice_in_dim(lax.empty((N_DEV,) + own.shape, own.dtype), own[None], dev, axis=0)

    gathers = []
    token = jnp.zeros((8, LANES), F32)
    for l in range(depth):
        srcs = [w_in[l].astype(BF16), w_out[l].astype(BF16), w_up[l].astype(BF16), w_down[l].astype(BF16),
                conv_w[l] + token[0, 0]]
        sems, srcs, lands, token = exchange_start(srcs, [landing(a) for a in srcs], False, name=f"gather_start_{l}")
        gathers.append((sems, srcs, lands))

    def gathered(l, after):
        g_in, g_out, g_up, g_down, g_cw = exchange_wait(*gathers[l], False, after, name=f"gather_wait_{l}")
        return (jnp.transpose(g_in, (1, 0, 2)).reshape(d_model, in_width), g_out.reshape(d_model, d_model),
                jnp.transpose(g_up, (1, 0, 2)).reshape(d_model, ff2), g_down.reshape(ff2 // 2, d_model),
                jnp.transpose(g_cw, (1, 0, 2)).reshape(3, ff2))

    bias_a, buckets_a = _band_bias(rel_bias_table[:, :ha], (1,), (WINDOW_A - 1,))
    bias_c, buckets_c = _band_bias(rel_bias_table[:, ha:], DILATIONS, (BLOCK,) * len(DILATIONS))

    xs = x[0]
    saved = []
    wi, wo, wu, wd, cw = ([None] * depth for _ in range(5))
    for l in range(depth):
        wi[l], wo[l], wu[l], wd[l], cw[l] = gathered(l, token if l == 0 else xs)
        h1 = rmsnorm_fwd(xs, attn_norm[l], name="attn_norm_fwd")
        proj = matmul(h1, wi[l], name="in_proj")
        sinks = jnp.repeat(a_sinks[l], HEAD_DIM).reshape(sa, 1, LANES)
        gaq, gak = _tile2(a_q_gain[l]), _tile2(a_k_gain[l])
        gcq, gck = _tile2(c_q_gain[l]), _tile2(c_k_gain[l])
        out_a, lse_a = banded_fwd(proj, cb_aq, cb_ak, cb_av, sa, gaq, gak, bias_a, (1,), sinks, True, name="swa_fwd")
        out_b, tot_b = sb_fwd(proj, cb_bq, cb_bk, cb_bv, sb, name="stick_fwd")
        out_c, lse_c = banded_fwd(proj, cb_cq, cb_ck, cb_cv, sc, gcq, gck, bias_c, DILATIONS, None, False,
                                  name="dilated_fwd")
        mix = mixnorm_fwd([out_a, out_b, out_c], mix_out_gain[l], name="mix_norm_fwd")
        x_mid = matmul(mix, wo[l], res=xs, name="out_proj")
        h2 = rmsnorm_fwd(x_mid, ffn_norm[l], name="ffn_norm_fwd")
        p = matmul(h2, wu[l], name="up_proj")
        act = ffn_act_fwd(p, cw[l], conv_b[l], name="ffn_act_fwd")
        x_out = matmul(act, wd[l], res=x_mid, name="down_proj")
        saved.append(dict(x_in=xs, h1t=h1.T, proj=proj, out_a=out_a, lse_a=lse_a, out_b=out_b, tot_b=tot_b,
                          out_c=out_c, lse_c=lse_c, mixt=mix.T, x_mid=x_mid, h2t=h2.T, p=p, actt=act.T,
                          sinks=sinks, gains=(gaq, gak, gcq, gck)))
        xs = x_out

    dx, loss_part = loss_head(xs, loss_target[0], name="loss_head")

    small = {k: [None] * depth for k in ("attn_norm", "a_q_gain", "a_k_gain", "a_sinks", "c_q_gain", "c_k_gain",
                                         "mix_out_gain", "ffn_norm", "conv_w", "conv_b")}
    big = {k: [None] * depth for k in ("w_in", "w_out", "w_up", "w_down")}
    dbias_a = dbias_c = None
    scatters = [None] * depth
    token = jnp.zeros((8, LANES), F32)
    for l in reversed(range(depth)):
        sv = saved[l]
        gaq, gak, gcq, gck = sv["gains"]
        da = matmul(dx, wd[l], trans_b=True, name="down_proj_dx")
        big["w_down"][l] = matmul(sv["actt"], dx, out_dtype=BF16, name="down_proj_dw")
        dp, small["conv_w"][l], small["conv_b"][l] = ffn_act_bwd(da, sv["p"], cw[l], conv_b[l] + token[0, 0],
                                                                 name="ffn_act_bwd")
        dh2 = matmul(dp, wu[l], trans_b=True, name="up_proj_dx")
        big["w_up"][l] = matmul(sv["h2t"], dp, out_dtype=BF16, name="up_proj_dw")
        dx_mid, small["ffn_norm"][l] = rmsnorm_bwd(dh2, sv["x_mid"], ffn_norm[l], dx, name="ffn_norm_bwd")
        dmix = matmul(dx_mid, wo[l], trans_b=True, name="out_proj_dx")
        big["w_out"][l] = matmul(sv["mixt"], dx_mid, out_dtype=BF16, name="out_proj_dw")
        (d_oa, d_ob, d_oc), small["mix_out_gain"][l] = mixnorm_bwd(
            dmix, [sv["out_a"], sv["out_b"], sv["out_c"]], mix_out_gain[l], name="mix_norm_bwd")
        dq_a, dk_a, dv_a, db_a, dgq_a, dgk_a, dsink = banded_bwd(
            sv["proj"], cb_aq, cb_ak, cb_av, sa, gaq, gak, bias_a, (1,), sv["sinks"], True,
            d_oa, sv["out_a"], sv["lse_a"], name="swa_bwd")
        dq_b, dk_b, dv_b = sb_bwd(sv["proj"], cb_bq, cb_bk, cb_bv, sb, d_ob, sv["tot_b"], name="stick_bwd")
        dq_c, dk_c, dv_c, db_c, dgq_c, dgk_c = banded_bwd(
            sv["proj"], cb_cq, cb_ck, cb_cv, sc, gcq, gck, bias_c, DILATIONS, None, False,
            d_oc, sv["out_c"], sv["lse_c"], name="dilated_bwd")
        fold = lambda g: g.reshape(-1, HEAD_DIM).sum(axis=0)
        small["a_q_gain"][l], small["a_k_gain"][l] = fold(dgq_a), fold(dgk_a)
        small["c_q_gain"][l], small["c_k_gain"][l] = fold(dgq_c), fold(dgk_c)
        small["a_sinks"][l] = dsink[:, ::HEAD_DIM].reshape(-1)
        dbias_a = db_a if dbias_a is None else dbias_a + db_a
        dbias_c = db_c if dbias_c is None else dbias_c + db_c
        dproj = jnp.concatenate([dq_a, dk_a, dv_a, dq_b, dk_b, dv_b, dq_c, dk_c, dv_c], axis=1)
        dh1 = matmul(dproj, wi[l], trans_b=True, name="in_proj_dx")
        big["w_in"][l] = matmul(sv["h1t"], dproj, out_dtype=BF16, name="in_proj_dw")
        dx, small["attn_norm"][l] = rmsnorm_bwd(dh1, sv["x_in"], attn_norm[l], dx_mid, name="attn_norm_bwd")
        by_cols = lambda a: jnp.transpose(a.reshape(a.shape[0], N_DEV, a.shape[1] // N_DEV), (1, 0, 2))
        by_rows = lambda a: a.reshape(N_DEV, a.shape[0] // N_DEV, a.shape[1])
        parts = [by_cols(big["w_in"][l]), by_rows(big["w_out"][l]), by_cols(big["w_up"][l]), by_rows(big["w_down"][l])]
        lands = [landing(lax.dynamic_index_in_dim(pt, dev, axis=0, keepdims=False)) for pt in parts]
        sems, parts, lands, token = exchange_start(parts, lands, True, name=f"scatter_start_{l}")
        scatters[l] = (sems, parts, lands)

    dtable =jnp.concatenate([bias_bwd(dbias_a, buckets_a, name="swa_bias_bwd"),
                              bias_bwd(dbias_c, buckets_c, name="dilated_bias_bwd")], axis=1)

    order = ("attn_norm", "a_q_gain", "a_k_gain", "a_sinks", "c_q_gain", "c_k_gain", "rel_bias_table",
             "mix_out_gain", "ffn_norm", "conv_w", "conv_b")
    partial = {k: jnp.stack(v) for k, v in small.items()}
    partial["rel_bias_table"] = dtable
    pieces = [partial[k] for k in order] + [loss_part.reshape(1)]
    n_small = sum(math.prod(pc.shape) for pc in pieces)
    rows = -(-n_small // (8 * LANES)) * 8
    gathered = gather_small(_pack(pieces, rows), name="gather_small_grads")
    summed = _unpack(sum_devices(gathered.reshape(N_DEV, rows, LANES), name="sum_small_grads"),
                     [pc.shape for pc in pieces])
    g_small = dict(zip(order, summed[:-1]))
    loss = summed[-1][0]
    g_small["conv_w"] = lax.dynamic_slice_in_dim(g_small["conv_w"], dev * ff2_shard, ff2_shard, axis=2)

    w_small = dict(attn_norm=attn_norm, a_q_gain=a_q_gain, a_k_gain=a_k_gain, a_sinks=a_sinks, c_q_gain=c_q_gain,
                   c_k_gain=c_k_gain, rel_bias_table=rel_bias_table, mix_out_gain=mix_out_gain, ffn_norm=ffn_norm,
                   conv_w=conv_w, conv_b=conv_b)
    m_small = dict(attn_norm=m_attn_norm, a_q_gain=m_a_q_gain, a_k_gain=m_a_k_gain, a_sinks=m_a_sinks,
                   c_q_gain=m_c_q_gain, c_k_gain=m_c_k_gain, rel_bias_table=m_rel_bias_table,
                   mix_out_gain=m_mix_out_gain, ffn_norm=m_ffn_norm, conv_w=m_conv_w, conv_b=m_conv_b)
    v_small = dict(attn_norm=v_attn_norm, a_q_gain=v_a_q_gain, a_k_gain=v_a_k_gain, a_sinks=v_a_sinks,
                   c_q_gain=v_c_q_gain, c_k_gain=v_c_k_gain, rel_bias_table=v_rel_bias_table,
                   mix_out_gain=v_mix_out_gain, ffn_norm=v_ffn_norm, conv_w=v_conv_w, conv_b=v_conv_b)
    shapes = [w_small[k].shape for k in order]
    n_upd = sum(math.prod(sh) for sh in shapes)
    urows = -(-n_upd // (8 * LANES)) * 8
    packs = [_pack([d[k] for k in order], urows) for d in (g_small, w_small, m_small, v_small)]
    upd = adamw_small(*packs, name="adamw_small")
    delta_s, newm_s, newv_s = [dict(zip(order, _unpack(u, shapes))) for u in upd]

    names_big = ("w_in", "w_out", "w_up", "w_down")
    w_big = dict(w_in=(w_in, m_w_in, v_w_in), w_out=(w_out, m_w_out, v_w_out), w_up=(w_up, m_w_up, v_w_up),
                 w_down=(w_down, m_w_down, v_w_down))
    results = {k: [lax.empty(w_big[k][0].shape, F32) for _ in range(4)] for k in names_big}
    for l in reversed(range(depth)):
        landed = exchange_wait(*scatters[l], True, upd[0], name=f"scatter_wait_{l}")
        for k, parts in zip(names_big, landed):
            results[k] = adamw_parts(parts, *w_big[k], l, results[k], name="adamw_large")
    g_big, delta_b, newm_b, newv_b = [{k: results[k][i] for k in names_big} for i in range(4)]

    all_names = ("attn_norm", "w_in", "a_q_gain", "a_k_gain", "a_sinks", "c_q_gain", "c_k_gain", "rel_bias_table",
                 "mix_out_gain", "w_out", "ffn_norm", "w_up", "conv_w", "conv_b", "w_down")
    pick = lambda sm, bg: [bg[k] if k in bg else sm[k] for k in all_names]
    return (loss, dx[None], *pick(g_small, g_big), *pick(delta_s, delta_b), *pick(newm_s, newm_b),
            *pick(newv_s, newv_b))
```

```python
import functools
import math

import jax
import jax.numpy as jnp
from jax import lax
from jax.experimental import pallas as pl
from jax.experimental.pallas import tpu as pltpu

F32, BF16, I32 = jnp.float32, jnp.bfloat16, jnp.int32
MESH = pl.DeviceIdType.MESH

HEAD_DIM = 64
LANES = 128
BLOCK = 128
EPS = 1e-6
NEG_INF = -1e30
N_BUCKETS = 32
T5_MAX_DIST = 2048
WINDOW_A = 128
DILATIONS = (1, 4, 16)
N_DEV = 8
VMEM_LIMIT = 56 * 1024 * 1024

ADAM_LR, ADAM_B1, ADAM_B2, ADAM_EPS, ADAM_WD, ADAM_STEP = 0.001, 0.9, 0.999, 1e-08, 0.01, 10


def _params(sem=None, vmem=None):
    return pltpu.CompilerParams(dimension_semantics=sem, vmem_limit_bytes=vmem)


def _pick(n, cands):
    for c in cands:
        if n % c == 0:
            return c
    raise ValueError(f"no tile for {n}")


def _dot(a, b):
    return lax.dot_general(a, b, (((1,), (0,)), ((), ())), preferred_element_type=F32)


def _dot_nt(a, b):
    return lax.dot_general(a, b, (((1,), (1,)), ((), ())), preferred_element_type=F32)


def _dot_tn(a, b):
    return lax.dot_general(a, b, (((0,), (0,)), ((), ())), preferred_element_type=F32)


def matmul(a, b, *, trans_a=False, trans_b=False, out_dtype=F32, res=None, col_blocks=None, name):
    m, k = (a.shape[1], a.shape[0]) if trans_a else a.shape
    n = b.shape[0] if trans_b else b.shape[1]
    tm = _pick(m, (1024, 512, 256))
    tn = n // col_blocks if col_blocks else _pick(n, (1024, 768, 512, 256, 128))
    tk = k if k <= 2048 else _pick(k, (1024, 768, 512, 256))
    nk = k // tk
    dn = (((0 if trans_a else 1,), (1 if trans_b else 0,)), ((), ()))

    def body(*refs):
        if res is None:
            a_ref, b_ref, o_ref, acc = refs
        else:
            a_ref, b_ref, r_ref, o_ref, acc = refs
        kk = pl.program_id(2)

        @pl.when(kk == 0)
        def _():
            acc[...] = jnp.zeros_like(acc)

        acc[...] += lax.dot_general(a_ref[...].astype(BF16), b_ref[...].astype(BF16), dn,
                                    preferred_element_type=F32)

        @pl.when(kk == nk - 1)
        def _():
            r = acc[...]
            if res is not None:
                r = r_ref[...] + r
            o_ref[...] = r.astype(out_dtype)

    b_spec = (pl.BlockSpec((tn, tk), lambda i, j, kk: (j, kk)) if trans_b
              else pl.BlockSpec((tk, tn), lambda i, j, kk: (kk, j)))
    a_spec = (pl.BlockSpec((tk, tm), lambda i, j, kk: (kk, i)) if trans_a
              else pl.BlockSpec((tm, tk), lambda i, j, kk: (i, kk)))
    in_specs = [a_spec, b_spec]
    args = [a, b]
    if res is not None:
        in_specs.append(pl.BlockSpec((tm, tn), lambda i, j, kk: (i, j)))
        args.append(res)
    if col_blocks:
        out_spec = pl.BlockSpec((None, tm, tn), lambda i, j, kk: (j, i, 0))
        out_shape = jax.ShapeDtypeStruct((col_blocks, m, tn), out_dtype)
    else:
        out_spec = pl.BlockSpec((tm, tn), lambda i, j, kk: (i, j))
        out_shape = jax.ShapeDtypeStruct((m, n), out_dtype)
    return pl.pallas_call(
        body, name=name, grid=(m // tm, n // tn, nk),
        in_specs=in_specs, out_specs=out_spec, out_shape=out_shape,
        scratch_shapes=[pltpu.VMEM((tm, tn), F32)],
        compiler_params=_params(("parallel", "parallel", "arbitrary"), VMEM_LIMIT),
    )(*args)


def rmsnorm_fwd(x, g, *, name):
    s, d = x.shape
    tm = 512

    def body(x_ref, g_ref, o_ref):
        xv = x_ref[...]
        r = lax.rsqrt(jnp.mean(xv * xv, axis=-1, keepdims=True) + EPS)
        o_ref[...] = (xv * r * g_ref[...]).astype(BF16)

    return pl.pallas_call(
        body, name=name, grid=(s // tm,),
        in_specs=[pl.BlockSpec((tm, d), lambda i: (i, 0)), pl.BlockSpec((1, d), lambda i: (0, 0))],
        out_specs=pl.BlockSpec((tm, d), lambda i: (i, 0)),
        out_shape=jax.ShapeDtypeStruct((s, d), BF16),
        compiler_params=_params(("parallel",)),
    )(x, g.reshape(1, d))


def rmsnorm_bwd(dh, x, g, dres, *, name):
    s, d = x.shape
    tm = 256

    def body(dh_ref, x_ref, g_ref, dres_ref, dx_ref, dg_ref):
        @pl.when(pl.program_id(0) == 0)
        def _():
            dg_ref[...] = jnp.zeros_like(dg_ref)

        xv, dhv = x_ref[...], dh_ref[...]
        r = lax.rsqrt(jnp.mean(xv * xv, axis=-1, keepdims=True) + EPS)
        gd = dhv * g_ref[...]
        dot = jnp.mean(gd * xv, axis=-1, keepdims=True)
        dx_ref[...] = dres_ref[...] + (r * gd - xv * (r * r * r * dot))
        dg_ref[...] += jnp.sum(dhv * (xv * r), axis=0, keepdims=True)

    dx, dg = pl.pallas_call(
        body, name=name, grid=(s // tm,),
        in_specs=[pl.BlockSpec((tm, d), lambda i: (i, 0)), pl.BlockSpec((tm, d), lambda i: (i, 0)),
                  pl.BlockSpec((1, d), lambda i: (0, 0)), pl.BlockSpec((tm, d), lambda i: (i, 0))],
        out_specs=[pl.BlockSpec((tm, d), lambda i: (i, 0)), pl.BlockSpec((1, d), lambda i: (0, 0))],
        out_shape=[jax.ShapeDtypeStruct((s, d), F32), jax.ShapeDtypeStruct((1, d), F32)],
        compiler_params=_params(("arbitrary",)),
    )(dh, x, g.reshape(1, d), dres)
    return dx, dg[0]


def loss_head(y, target, *, name):
    s, d = y.shape
    tm = 512

    def body(y_ref, t_ref, dy_ref, l_ref):
        @pl.when(pl.program_id(0) == 0)
        def _():
            l_ref[...] = jnp.zeros_like(l_ref)

        e = y_ref[...] - t_ref[...]
        dy_ref[...] = e / float(d)
        per_tok = jnp.mean(e * e, axis=-1, keepdims=True)
        l_ref[...] += 0.5 * jnp.sum(per_tok, axis=0, keepdims=True)

    dy, l = pl.pallas_call(
        body, name=name, grid=(s // tm,),
        in_specs=[pl.BlockSpec((tm, d), lambda i: (i, 0)), pl.BlockSpec((tm, d), lambda i: (i, 0))],
        out_specs=[pl.BlockSpec((tm, d), lambda i: (i, 0)), pl.BlockSpec((8, LANES), lambda i: (0, 0))],
        out_shape=[jax.ShapeDtypeStruct((s, d), F32), jax.ShapeDtypeStruct((8, LANES), F32)],
        compiler_params=_params(("arbitrary",)),
    )(y, target)
    return dy, l[0, 0]


FFN_TN = 256
FFN_CH = 256


def _rows_before(ref, r0, first):
    if first:
        cur = ref[pl.ds(0, FFN_CH), :]
        row = lax.broadcasted_iota(I32, cur.shape, 0)
        sh1 = jnp.where(row < 1, 0.0, pltpu.roll(cur, 1, axis=0))
        sh2 = jnp.where(row < 2, 0.0, pltpu.roll(cur, 2, axis=0))
        return cur, sh1, sh2
    ext = ref[pl.ds(pl.multiple_of(r0 - 8, 8), FFN_CH + 8), :]
    return ext[8:], pltpu.roll(ext, 1, axis=0)[8:], pltpu.roll(ext, 2, axis=0)[8:]


def _rows_after(ref, r0, last):
    if last:
        cur = ref[pl.ds(r0, FFN_CH), :]
        row = lax.broadcasted_iota(I32, cur.shape, 0)
        up1 = jnp.where(row >= FFN_CH - 1, 0.0, pltpu.roll(cur, FFN_CH - 1, axis=0))
        up2 = jnp.where(row >= FFN_CH - 2, 0.0, pltpu.roll(cur, FFN_CH - 2, axis=0))
        return cur, up1, up2
    n = FFN_CH + 8
    ext = ref[pl.ds(r0, n), :]
    return ext[:FFN_CH], pltpu.roll(ext, n - 1, axis=0)[:FFN_CH], pltpu.roll(ext, n - 2, axis=0)[:FFN_CH]


def _sigmoid(x):
    return 1.0 / (1.0 + jnp.exp(-x))


def ffn_act_fwd(p, conv_w, conv_b, *, name):
    s, f2 = p.shape
    f = f2 // 2
    nj = f // FFN_TN
    nch = s // FFN_CH

    def body(pg_ref, pu_ref, wg_ref, wu_ref, bg_ref, bu_ref, a_ref):
        def conv(ref, w_ref, b_ref, r0, first):
            cur, sh1, sh2 = _rows_before(ref, r0, first)
            return ((b_ref[...] + w_ref[0:1, :] * sh2) + w_ref[1:2, :] * sh1) + w_ref[2:3, :] * cur

        def chunk(r0, first):
            gate = conv(pg_ref, wg_ref, bg_ref, r0, first)
            up = conv(pu_ref, wu_ref, bu_ref, r0, first)
            a_ref[pl.ds(r0, FFN_CH), :] = (gate * _sigmoid(gate) * up).astype(BF16)

        chunk(0, True)

        def step(c, carry):
            chunk(pl.multiple_of(c * FFN_CH, FFN_CH), False)
            return carry

        lax.fori_loop(1, nch, step, 0)

    col = lambda off: pl.BlockSpec((s, FFN_TN), lambda j: (0, j + off))
    wcol = lambda off: pl.BlockSpec((3, FFN_TN), lambda j: (0, j + off))
    bcol = lambda off: pl.BlockSpec((1, FFN_TN), lambda j: (0, j + off))
    return pl.pallas_call(
        body, name=name, grid=(nj,),
        in_specs=[col(0), col(nj), wcol(0), wcol(nj), bcol(0), bcol(nj)],
        out_specs=pl.BlockSpec((s, FFN_TN), lambda j: (0, j)),
        out_shape=jax.ShapeDtypeStruct((s, f), BF16),
        compiler_params=_params(("parallel",), VMEM_LIMIT),
    )(p, p, conv_w, conv_w, conv_b.reshape(1, f2), conv_b.reshape(1, f2))


def ffn_act_bwd(da, p, conv_w, conv_b, *, name):
    s, f2 = p.shape
    f = f2 // 2
    nj = f // FFN_TN
    nch = s // FFN_CH

    def body(da_ref, pg_ref, pu_ref, wg_ref, wu_ref, bg_ref, bu_ref,
             dpg_ref, dpu_ref, dwg_ref, dwu_ref, dbg_ref, dbu_ref, dug_s, duu_s):
        def conv(ref, w_ref, b_ref, r0, first):
            cur, sh1, sh2 = _rows_before(ref, r0, first)
            u = ((b_ref[...] + w_ref[0:1, :] * sh2) + w_ref[1:2, :] * sh1) + w_ref[2:3, :] * cur
            return u, (sh2, sh1, cur)

        def taps_sum(du, taps):
            return jnp.concatenate([jnp.sum(du * t, axis=0, keepdims=True) for t in taps], axis=0)

        def chunk(r0, first, acc):
            dwg, dwu, dbg, dbu = acc
            gate, tg = conv(pg_ref, wg_ref, bg_ref, r0, first)
            up, tu = conv(pu_ref, wu_ref, bu_ref, r0, first)
            dav = da_ref[pl.ds(r0, FFN_CH), :]
            sg = _sigmoid(gate)
            dgate = dav * up * (sg * (1.0 + gate * (1.0 - sg)))
            dup = dav * (gate * sg)
            dug_s[pl.ds(r0, FFN_CH), :] = dgate
            duu_s[pl.ds(r0, FFN_CH), :] = dup
            return (dwg + taps_sum(dgate, tg), dwu + taps_sum(dup, tu),
                    dbg + jnp.sum(dgate, axis=0, keepdims=True), dbu + jnp.sum(dup, axis=0, keepdims=True))

        z3 = jnp.zeros((3, FFN_TN), F32)
        z1 = jnp.zeros((1, FFN_TN), F32)
        acc = chunk(0, True, (z3, z3, z1, z1))
        acc = lax.fori_loop(1, nch, lambda c, a: chunk(pl.multiple_of(c * FFN_CH, FFN_CH), False, a), acc)
        dwg_ref[...], dwu_ref[...], dbg_ref[...], dbu_ref[...] = acc

        def back(src, w_ref, dst, r0, last):
            cur, up1, up2 = _rows_after(src, r0, last)
            dst[pl.ds(r0, FFN_CH), :] = (w_ref[2:3, :] * cur + w_ref[1:2, :] * up1 + w_ref[0:1, :] * up2).astype(BF16)

        def step(c, carry):
            r0 = pl.multiple_of(c * FFN_CH, FFN_CH)
            back(dug_s, wg_ref, dpg_ref, r0, False)
            back(duu_s, wu_ref, dpu_ref, r0, False)
            return carry

        lax.fori_loop(0, nch - 1, step, 0)
        back(dug_s, wg_ref, dpg_ref, (nch - 1) * FFN_CH, True)
        back(duu_s, wu_ref, dpu_ref, (nch - 1) * FFN_CH, True)

    col = lambda off: pl.BlockSpec((s, FFN_TN), lambda j: (0, j + off))
    wcol = lambda off: pl.BlockSpec((3, FFN_TN), lambda j: (0, j + off))
    bcol = lambda off: pl.BlockSpec((1, FFN_TN), lambda j: (0, j + off))
    outs = pl.pallas_call(
        body, name=name, grid=(nj,),
        in_specs=[col(0), col(0), col(nj), wcol(0), wcol(nj), bcol(0), bcol(nj)],
        out_specs=[col(0), col(0), wcol(0), wcol(0), bcol(0), bcol(0)],
        out_shape=[jax.ShapeDtypeStruct((s, f), BF16), jax.ShapeDtypeStruct((s, f), BF16),
                   jax.ShapeDtypeStruct((3, f), F32), jax.ShapeDtypeStruct((3, f), F32),
                   jax.ShapeDtypeStruct((1, f), F32), jax.ShapeDtypeStruct((1, f), F32)],
        scratch_shapes=[pltpu.VMEM((s, FFN_TN), F32), pltpu.VMEM((s, FFN_TN), F32)],
        compiler_params=_params(("parallel",), VMEM_LIMIT),
    )(da, p, p, conv_w, conv_w, conv_b.reshape(1, f2), conv_b.reshape(1, f2))
    dpg, dpu, dwg, dwu, dbg, dbu = outs
    return (jnp.concatenate([dpg, dpu], axis=1), jnp.concatenate([dwg, dwu], axis=1),
            jnp.concatenate([dbg, dbu], axis=1)[0])


def mixnorm_fwd(outs, gain, *, name):
    s = outs[0].shape[0]
    widths = [o.shape[1] for o in outs]
    total = sum(widths)
    tm = 512

    def body(*refs):
        o_refs, g_ref, m_ref = refs[:-2], refs[-2], refs[-1]
        off = 0
        for o_ref, w in zip(o_refs, widths):
            xv = o_ref[...]
            r = lax.rsqrt(jnp.mean(xv * xv, axis=-1, keepdims=True) + EPS)
            m_ref[:, off:off + w] = (xv * r * g_ref[:, off:off + w]).astype(BF16)
            off += w

    return pl.pallas_call(
        body, name=name, grid=(s // tm,),
        in_specs=[pl.BlockSpec((tm, w), lambda i: (i, 0)) for w in widths] + [pl.BlockSpec((1, total), lambda i: (0, 0))],
        out_specs=pl.BlockSpec((tm, total), lambda i: (i, 0)),
        out_shape=jax.ShapeDtypeStruct((s, total), BF16),
        compiler_params=_params(("parallel",)),
    )(*outs, gain.reshape(1, total))


def mixnorm_bwd(dmix, outs, gain, *, name):
    s = outs[0].shape[0]
    widths = [o.shape[1] for o in outs]
    total = sum(widths)
    n = len(outs)
    tm = 256

    def body(*refs):
        dm_ref, o_refs, g_ref = refs[0], refs[1:1 + n], refs[1 + n]
        d_refs, dg_ref = refs[2 + n:2 + 2 * n], refs[2 + 2 * n]

        @pl.when(pl.program_id(0) == 0)
        def _():
            dg_ref[...] = jnp.zeros_like(dg_ref)

        off = 0
        for o_ref, d_ref, w in zip(o_refs, d_refs, widths):
            xv = o_ref[...]
            dhv = dm_ref[:, off:off + w]
            r = lax.rsqrt(jnp.mean(xv * xv, axis=-1, keepdims=True) + EPS)
            gd = dhv * g_ref[:, off:off + w]
            dot = jnp.mean(gd * xv, axis=-1, keepdims=True)
            d_ref[...] = r * gd - xv * (r * r * r * dot)
            dg_ref[:, off:off + w] += jnp.sum(dhv * (xv * r), axis=0, keepdims=True)
            off += w

    res = pl.pallas_call(
        body, name=name, grid=(s // tm,),
        in_specs=[pl.BlockSpec((tm, total), lambda i: (i, 0))]
        + [pl.BlockSpec((tm, w), lambda i: (i, 0)) for w in widths] + [pl.BlockSpec((1, total), lambda i: (0, 0))],
        out_specs=[pl.BlockSpec((tm, w), lambda i: (i, 0)) for w in widths] + [pl.BlockSpec((1, total), lambda i: (0, 0))],
        out_shape=[jax.ShapeDtypeStruct((s, w), F32) for w in widths] + [jax.ShapeDtypeStruct((1, total), F32)],
        compiler_params=_params(("arbitrary",)),
    )(dmix, *outs, gain.reshape(1, total))
    return res[:n], res[n][0]


NORM_CH = 512
FWD_TILES = 4
BWD_TILES = 2


def _lo_mask(shape):
    return lax.broadcasted_iota(I32, shape, 1) < HEAD_DIM


def _head_sum(x, lo):
    del lo
    i = lax.broadcasted_iota(I32, (LANES, LANES), 0) // HEAD_DIM
    j = lax.broadcasted_iota(I32, (LANES, LANES), 1) // HEAD_DIM
    same = (i == j).astype(BF16)
    hi = x.astype(BF16)
    r1 = x - hi.astype(F32)
    mid = r1.astype(BF16)
    low = (r1 - mid.astype(F32)).astype(BF16)
    return _dot(hi, same) + _dot(mid, same) + _dot(low, same)


def _head_stats(x, lo):
    return lax.rsqrt(_head_sum(x * x, lo) * (1.0 / HEAD_DIM) + EPS)


def _swap_halves(x):
    return pltpu.roll(x, HEAD_DIM, axis=1)


def _replicate_head(x, lo, use_lo_head):
    sw = _swap_halves(x)
    return jnp.where(use_lo_head, jnp.where(lo, x, sw), jnp.where(lo, sw, x))


def _tile_rows(i, s, d):
    nb = s // (BLOCK * d)
    r = i // nb
    b = i % nb
    start = r + (BLOCK * d) * b
    prev = start - (BLOCK * d) * jnp.minimum(b, 1)
    return start, prev, b > 0


def _rows(ref, start, d):
    if d == 1:
        return ref[pl.ds(pl.multiple_of(start, BLOCK), BLOCK), :]
    return ref[pl.ds(start, BLOCK, stride=d), :]


def _set_rows(ref, start, d, val):
    if d == 1:
        ref[pl.ds(pl.multiple_of(start, BLOCK), BLOCK), :] = val
    else:
        ref[pl.ds(start, BLOCK, stride=d), :] = val


def banded_fwd(proj, qb0, kb0, vb0, n_slabs, gq, gk, bias, dils, sinks, gqa, *, name):
    s = proj.shape[0]
    nbr = len(dils)
    nt = s // BLOCK
    nch = s // NORM_CH
    has_sink = sinks is not None

    def body(*refs):
        q_ref, k_ref, v_ref, gq_ref, gk_ref, b_ref = refs[:6]
        rest = refs[6:]
        if has_sink:
            sink_ref, rest = rest[0], rest[1:]
        out_ref, lse_ref, qn_s, kn_s, vv_s, o_s, l_s = rest
        p = pl.program_id(0)
        use_lo = (p // 2) == 0

        def prep(c, carry):
            rows = pl.ds(pl.multiple_of(c * NORM_CH, NORM_CH), NORM_CH)
            lo = _lo_mask((NORM_CH, LANES))
            qv, kv, vv = q_ref[rows, :], k_ref[rows, :], v_ref[rows, :]
            qn_s[rows, :] = qv * _head_stats(qv, lo) * gq_ref[...] * (HEAD_DIM ** -0.5)
            kn = kv * _head_stats(kv, lo) * gk_ref[...]
            if gqa:
                kn = _replicate_head(kn, lo, use_lo)
                vv = _replicate_head(vv, lo, use_lo)
            kn_s[rows, :] = kn
            vv_s[rows, :] = vv
            return carry

        lax.fori_loop(0, nch, prep, 0)

        lo = _lo_mask((BLOCK, LANES))
        hms = [lo, jnp.logical_not(lo)]
        heads, tiles = range(2), range(FWD_TILES)
        for br, d in enumerate(dils):
            def step(ii, carry, br=br, d=d):
                pos = [_tile_rows(ii * FWD_TILES + u, s, d) for u in tiles]
                kcat = [jnp.concatenate([_rows(kn_s, pos[u][1], d), _rows(kn_s, pos[u][0], d)], axis=0).astype(BF16)
                        for u in tiles]
                vcat = [jnp.concatenate([_rows(vv_s, pos[u][1], d), _rows(vv_s, pos[u][0], d)], axis=0).astype(BF16)
                        for u in tiles]
                qt = [_rows(qn_s, pos[u][0], d) for u in tiles]
                sc = [[_dot_nt(jnp.where(hms[h], qt[u], 0.0).astype(BF16), kcat[u])
                       + b_ref[br, jnp.where(pos[u][2], 0, 1), h] for h in heads] for u in tiles]
                m = [[jnp.max(sc[u][h], axis=1, keepdims=True) for h in heads] for u in tiles]
                pe = [[jnp.exp(sc[u][h] - m[u][h]) for h in heads] for u in tiles]
                den = [[jnp.sum(pe[u][h], axis=1, keepdims=True) for h in heads] for u in tiles]
                o = [[_dot(pe[u][h].astype(BF16), vcat[u]) * (1.0 / den[u][h]) for h in heads] for u in tiles]
                for u in tiles:
                    _set_rows(o_s.at[br], pos[u][0], d, jnp.where(lo, o[u][0], o[u][1]))
                    _set_rows(l_s.at[br], pos[u][0], d,
                              jnp.where(lo, m[u][0] + jnp.log(den[u][0]), m[u][1] + jnp.log(den[u][1])))
                return carry

            lax.fori_loop(0, nt // FWD_TILES, step, 0)

        def combine(c, carry):
            rows = pl.ds(pl.multiple_of(c * NORM_CH, NORM_CH), NORM_CH)
            ls = [l_s[br, rows, :] for br in range(nbr)]
            mx = functools.reduce(jnp.maximum, ls)
            if has_sink:
                mx = jnp.maximum(mx, sink_ref[...])
            tot = functools.reduce(jnp.add, [jnp.exp(l - mx) for l in ls])
            if has_sink:
                tot = tot + jnp.exp(sink_ref[...] - mx)
            lse = mx + jnp.log(tot)
            acc = jnp.exp(ls[0] - lse) * o_s[0, rows, :]
            for br in range(1, nbr):
                acc = acc + jnp.exp(ls[br] - lse) * o_s[br, rows, :]
            out_ref[rows, :] = acc
            lse_ref[rows, :] = lse
            return carry

        lax.fori_loop(0, nch, combine, 0)

    slab = lambda b0, shared: pl.BlockSpec((s, LANES), (lambda p: (0, b0)) if shared else (lambda p: (0, b0 + p)),
                                           pipeline_mode=pl.Buffered(1))
    vec = pl.BlockSpec((1, LANES), lambda p: (0, 0))
    in_specs = [slab(qb0, False), slab(kb0, gqa), slab(vb0, gqa), vec, vec,
                pl.BlockSpec((nbr, 2, 2, BLOCK, 2 * BLOCK), lambda p: (0, 0, p, 0, 0))]
    args = [proj, proj, proj, gq.reshape(1, LANES), gk.reshape(1, LANES), bias]
    if has_sink:
        in_specs.append(pl.BlockSpec((None, 1, LANES), lambda p: (p, 0, 0)))
        args.append(sinks)
    w = LANES * n_slabs
    return pl.pallas_call(
        body, name=name, grid=(n_slabs,),
        in_specs=in_specs,
        out_specs=[pl.BlockSpec((s, LANES), lambda p: (0, p)), pl.BlockSpec((s, LANES), lambda p: (0, p))],
        out_shape=[jax.ShapeDtypeStruct((s, w), F32), jax.ShapeDtypeStruct((s, w), F32)],
        scratch_shapes=[pltpu.VMEM((s, LANES), F32), pltpu.VMEM((s, LANES), F32), pltpu.VMEM((s, LANES), F32),
                        pltpu.VMEM((nbr, s, LANES), F32), pltpu.VMEM((nbr, s, LANES), F32)],
        compiler_params=_params(("parallel",), VMEM_LIMIT),
    )(*args)


def banded_bwd(proj, qb0, kb0, vb0, n_slabs, gq, gk, bias, dils, sinks, gqa, dout, out, lse, *, name):
    s = proj.shape[0]
    nbr = len(dils)
    nt = s // BLOCK
    nch = s // NORM_CH
    has_sink = sinks is not None
    scale = HEAD_DIM ** -0.5

    def body(*refs):
        q_ref, k_ref, v_ref, gq_ref, gk_ref, b_ref, do_ref, o_ref, lse_ref = refs[:9]
        rest = refs[9:]
        if has_sink:
            sink_ref, rest = rest[0], rest[1:]
        dq_ref, dk_ref, dv_ref, db_ref, dgq_ref, dgk_ref = rest[:6]
        rest = rest[6:]
        if has_sink:
            dsink_ref, rest = rest[0], rest[1:]
        qn_s, kn_s, vv_s, dl_s, dqn_s, dkn_s, dvv_s = rest
        p = pl.program_id(0)
        use_lo = (p // 2) == 0

        def prep(c, carry):
            rows = pl.ds(pl.multiple_of(c * NORM_CH, NORM_CH), NORM_CH)
            lo = _lo_mask((NORM_CH, LANES))
            qv, kv, vv = q_ref[rows, :], k_ref[rows, :], v_ref[rows, :]
            qn_s[rows, :] = qv * _head_stats(qv, lo) * gq_ref[...] * scale
            kn = kv * _head_stats(kv, lo) * gk_ref[...]
            if gqa:
                kn = _replicate_head(kn, lo, use_lo)
                vv = _replicate_head(vv, lo, use_lo)
            kn_s[rows, :] = kn
            vv_s[rows, :] = vv
            delta = _head_sum(do_ref[rows, :] * o_ref[rows, :], lo)
            dl_s[rows, :] = delta
            z = jnp.zeros((NORM_CH, LANES), F32)
            dqn_s[rows, :] = z
            dkn_s[rows, :] = z
            dvv_s[rows, :] = z
            if has_sink:
                ps = jnp.exp(sink_ref[...] - lse_ref[rows, :])
                return carry - jnp.sum(ps * delta, axis=0, keepdims=True)
            return carry

        dsink = lax.fori_loop(0, nch, prep, jnp.zeros((1, LANES), F32))
        if has_sink:
            dsink_ref[...] = jnp.broadcast_to(dsink, (8, LANES))

        lo = _lo_mask((BLOCK, LANES))
        hms = [lo, jnp.logical_not(lo)]
        heads, tiles = range(2), range(BWD_TILES)
        for br, d in enumerate(dils):
            db_ref[br] = jnp.zeros((2, BLOCK, 2 * BLOCK), F32)

            def step(ii, carry, br=br, d=d):
                pos = [_tile_rows(ii * BWD_TILES + u, s, d) for u in tiles]
                kcat = [jnp.concatenate([_rows(kn_s, pos[u][1], d), _rows(kn_s, pos[u][0], d)], axis=0).astype(BF16)
                        for u in tiles]
                vcat = [jnp.concatenate([_rows(vv_s, pos[u][1], d), _rows(vv_s, pos[u][0], d)], axis=0).astype(BF16)
                        for u in tiles]
                qt = [_rows(qn_s, pos[u][0], d) for u in tiles]
                dot_ = [_rows(do_ref, pos[u][0], d) for u in tiles]
                lse_t = [_rows(lse_ref, pos[u][0], d) for u in tiles]
                dl_t = [_rows(dl_s, pos[u][0], d) for u in tiles]
                qh = [[jnp.where(hms[h], qt[u], 0.0).astype(BF16) for h in heads] for u in tiles]
                doh = [[jnp.where(hms[h], dot_[u], 0.0).astype(BF16) for h in heads] for u in tiles]
                sc = [[_dot_nt(qh[u][h], kcat[u]) + b_ref[br, jnp.where(pos[u][2], 0, 1), h] for h in heads]
                      for u in tiles]
                dp = [[_dot_nt(doh[u][h], vcat[u]) for h in heads] for u in tiles]
                lane0 = [0, HEAD_DIM]
                pr = [[jnp.exp(sc[u][h] - lse_t[u][:, lane0[h]:lane0[h] + 1]) for h in heads] for u in tiles]
                dlog = [[pr[u][h] * (dp[u][h] - dl_t[u][:, lane0[h]:lane0[h] + 1]) for h in heads] for u in tiles]
                for h in heads:
                    db_ref[br, h] += functools.reduce(jnp.add, [dlog[u][h] for u in tiles])
                dlb = [[dlog[u][h].astype(BF16) for h in heads] for u in tiles]
                prb = [[pr[u][h].astype(BF16) for h in heads] for u in tiles]
                dq_t = [jnp.where(lo, _dot(dlb[u][0], kcat[u]), _dot(dlb[u][1], kcat[u])) * scale for u in tiles]
                dk_t = [_dot_tn(dlb[u][0], qh[u][0]) + _dot_tn(dlb[u][1], qh[u][1]) for u in tiles]
                dv_t = [_dot_tn(prb[u][0], doh[u][0]) + _dot_tn(prb[u][1], doh[u][1]) for u in tiles]
                for u in tiles:
                    start, prev = pos[u][0], pos[u][1]
                    _set_rows(dqn_s, start, d, _rows(dqn_s, start, d) + dq_t[u])
                    _set_rows(dkn_s, prev, d, _rows(dkn_s, prev, d) + dk_t[u][:BLOCK])
                    _set_rows(dkn_s, start, d, _rows(dkn_s, start, d) + dk_t[u][BLOCK:])
                    _set_rows(dvv_s, prev, d, _rows(dvv_s, prev, d) + dv_t[u][:BLOCK])
                    _set_rows(dvv_s, start, d, _rows(dvv_s, start, d) + dv_t[u][BLOCK:])
                return carry

            lax.fori_loop(0, nt // BWD_TILES, step, 0)

        if gqa:
            @pl.when(p == 0)
            def _():
                dk_ref[...] = jnp.zeros_like(dk_ref)
                dv_ref[...] = jnp.zeros_like(dv_ref)

        def finish(c, carry):
            dgq, dgk = carry
            rows = pl.ds(pl.multiple_of(c * NORM_CH, NORM_CH), NORM_CH)
            lo = _lo_mask((NORM_CH, LANES))

            def norm_bwd(xv, dn, g_ref):
                r = _head_stats(xv, lo)
                gd = dn * g_ref[...]
                dot = _head_sum(gd * xv, lo) * (1.0 / HEAD_DIM)
                return r * gd - xv * (r * r * r * dot), dn * (xv * r)

            dq, gq_part = norm_bwd(q_ref[rows, :], dqn_s[rows, :], gq_ref)
            dq_ref[rows, :] = dq
            dgq = dgq + jnp.sum(gq_part, axis=0, keepdims=True)
            kv, dkn, dvv = k_ref[rows, :], dkn_s[rows, :], dvv_s[rows, :]
            if gqa:
                kv = _replicate_head(kv, lo, use_lo)
                dkn = dkn + _swap_halves(dkn)
                dvv = dvv + _swap_halves(dvv)
                lane = lax.broadcasted_iota(I32, (NORM_CH, LANES), 1)
                mine = (lane // HEAD_DIM) == (p // 2)
                dk, gk_part = norm_bwd(kv, dkn, gk_ref)
                dk_ref[rows, :] += jnp.where(mine, dk, 0.0)
                dv_ref[rows, :] += jnp.where(mine, dvv, 0.0)
                gk_part = jnp.where(lo, gk_part, 0.0)
            else:
                dk, gk_part = norm_bwd(kv, dkn, gk_ref)
                dk_ref[rows, :] = dk
                dv_ref[rows, :] = dvv
            dgk = dgk + jnp.sum(gk_part, axis=0, keepdims=True)
            return dgq, dgk

        z = jnp.zeros((1, LANES), F32)
        dgq, dgk = lax.fori_loop(0, nch, finish, (z, z))
        dgq_ref[...] = jnp.broadcast_to(dgq, (8, LANES))
        dgk_ref[...] = jnp.broadcast_to(dgk, (8, LANES))

    def slab_of(width_blocks, b0, shared):
        return pl.BlockSpec((s, LANES), (lambda p: (0, b0)) if shared else (lambda p: (0, b0 + p)),
                            pipeline_mode=pl.Buffered(1))

    vec = pl.BlockSpec((1, LANES), lambda p: (0, 0))
    own = pl.BlockSpec((s, LANES), lambda p: (0, p), pipeline_mode=pl.Buffered(1))
    in_specs = [slab_of(0, qb0, False), slab_of(0, kb0, gqa), slab_of(0, vb0, gqa), vec, vec,
                pl.BlockSpec((nbr, 2, 2, BLOCK, 2 * BLOCK), lambda p: (0, 0, p, 0, 0)), own, own, own]
    args = [proj, proj, proj, gq.reshape(1, LANES), gk.reshape(1, LANES), bias, dout, out, lse]
    if has_sink:
        in_specs.append(pl.BlockSpec((None, 1, LANES), lambda p: (p, 0, 0)))
        args.append(sinks)
    w = LANES * n_slabs
    kvw = LANES if gqa else w
    kv_spec = pl.BlockSpec((s, LANES), (lambda p: (0, 0)) if gqa else (lambda p: (0, p)))
    part = pl.BlockSpec((None, 8, LANES), lambda p: (p, 0, 0))
    out_specs = [pl.BlockSpec((s, LANES), lambda p: (0, p)), kv_spec, kv_spec,
                 pl.BlockSpec((nbr, 2, BLOCK, 2 * BLOCK), lambda p: (0, p, 0, 0)), part, part]
    out_shape = [jax.ShapeDtypeStruct((s, w), F32), jax.ShapeDtypeStruct((s, kvw), F32),
                 jax.ShapeDtypeStruct((s, kvw), F32),
                 jax.ShapeDtypeStruct((nbr, 2 * n_slabs, BLOCK, 2 * BLOCK), F32),
                 jax.ShapeDtypeStruct((n_slabs, 8, LANES), F32), jax.ShapeDtypeStruct((n_slabs, 8, LANES), F32)]
    if has_sink:
        out_specs.append(part)
        out_shape.append(jax.ShapeDtypeStruct((n_slabs, 8, LANES), F32))
    res = pl.pallas_call(
        body, name=name, grid=(n_slabs,),
        in_specs=in_specs, out_specs=out_specs, out_shape=out_shape,
        scratch_shapes=[pltpu.VMEM((s, LANES), F32) for _ in range(7)],
        compiler_params=_params(("arbitrary",), VMEM_LIMIT),
    )(*args)
    dq, dk, dv, db, dgq, dgk = res[:6]
    outs = [dq, dk, dv, db, dgq[:, 0, :], dgk[:, 0, :]]
    if has_sink:
        outs.append(res[6][:, 0, :])
    return outs


def bias_bwd(dbias, buckets, *, name):
    nbr, h = dbias.shape[:2]

    def body(db_ref, bk_ref, o_ref):
        lane = lax.broadcasted_iota(I32, (1, LANES), 1)
        acc = jnp.zeros((1, LANES), F32)
        for b in range(N_BUCKETS):
            tot = jnp.zeros((1, 1), F32)
            for br in range(nbr):
                sel = jnp.where(bk_ref[br] == b, db_ref[br], 0.0)
                tot = tot + jnp.sum(jnp.sum(sel, axis=0, keepdims=True), axis=1, keepdims=True)
            acc = jnp.where(lane == b, tot, acc)
        o_ref[...] = jnp.broadcast_to(acc, (8, LANES))

    res = pl.pallas_call(
        body, name=name, grid=(h,),
        in_specs=[pl.BlockSpec((nbr, None, BLOCK, 2 * BLOCK), lambda i: (0, i, 0, 0)),
                  pl.BlockSpec((nbr, BLOCK, 2 * BLOCK), lambda i: (0, 0, 0))],
        out_specs=pl.BlockSpec((None, 8, LANES), lambda i: (i, 0, 0)),
        out_shape=jax.ShapeDtypeStruct((h, 8, LANES), F32),
        compiler_params=_params(("parallel",)),
    )(dbias, buckets)
    return res[:, 0, :N_BUCKETS].T


SB_KG = 512


def _softplus(z):
    return jnp.maximum(z, 0.0) + jnp.log(1.0 + jnp.exp(-jnp.abs(z)))


def _split_dot(x, t):
    hi = x.astype(BF16)
    lo = (x - hi.astype(F32)).astype(BF16)
    return _dot(hi, t) + _dot(lo, t)


def sb_fwd(proj, qb0, kb0, vb0, n_slabs, *, name):
    s = proj.shape[0]
    nq = s // BLOCK
    nch = s // NORM_CH
    scale = HEAD_DIM ** -0.5

    def body(q_ref, k_ref, v_ref, o_ref, tot_ref, qlo_s, qhi_s, k_s, v_s):
        def prep(c, carry):
            rows = pl.ds(pl.multiple_of(c * NORM_CH, NORM_CH), NORM_CH)
            lo = _lo_mask((NORM_CH, LANES))
            qv = q_ref[rows, :] * scale
            qlo_s[rows, :] = jnp.where(lo, qv, 0.0).astype(BF16)
            qhi_s[rows, :] = jnp.where(lo, 0.0, qv).astype(BF16)
            k_s[rows, :] = k_ref[rows, :].astype(BF16)
            v_s[rows, :] = v_ref[rows, :].astype(BF16)
            return carry

        lax.fori_loop(0, nch, prep, 0)

        row = lax.broadcasted_iota(I32, (BLOCK, BLOCK), 0)
        col = lax.broadcasted_iota(I32, (BLOCK, BLOCK), 1)
        lo = col < HEAD_DIM
        t_ge = (row >= col).astype(BF16)
        rowg = lax.broadcasted_iota(I32, (BLOCK, SB_KG), 0)
        colg = lax.broadcasted_iota(I32, (BLOCK, SB_KG), 1)

        nsub = SB_KG // BLOCK
        heads = range(2)

        def qloop(qi, carry):
            q0 = pl.multiple_of(qi * BLOCK, BLOCK)
            qh = [qlo_s[pl.ds(q0, BLOCK), :], qhi_s[pl.ds(q0, BLOCK), :]]
            gd = qi // nsub

            def logits(gi):
                k0 = pl.multiple_of(gi * SB_KG, SB_KG)
                kg = k_s[pl.ds(k0, SB_KG), :]
                return [_dot_nt(qh[h], kg) for h in heads]

            def group(gi, st, mask):
                k0 = pl.multiple_of(gi * SB_KG, SB_KG)
                vg = v_s[pl.ds(k0, SB_KG), :]
                c = [st[0], st[2]]
                z = [st[4], st[5]]
                z_next = logits(jnp.maximum(gi - 1, 0))
                sp = [_softplus(z[h]) for h in heads]
                lrem = [-sp[h] if mask is None else jnp.where(mask, -sp[h], 0.0) for h in heads]
                piece = lambda x, j: x[:, j * BLOCK:(j + 1) * BLOCK]
                incl = [[_split_dot(piece(lrem[h], j), t_ge) for j in range(nsub)] for h in heads]
                a = []
                for h in heads:
                    suffix = [None] * nsub
                    for j in reversed(range(nsub)):
                        suffix[j] = c[h] + incl[h][j] - piece(lrem[h], j)
                        c[h] = c[h] + incl[h][j][:, 0:1]
                    ah = jnp.exp(z[h] - sp[h] + jnp.concatenate(suffix, axis=1))
                    a.append((ah if mask is None else jnp.where(mask, ah, 0.0)).astype(BF16))
                o = [st[1 + 2 * h] + _dot(a[h], vg) for h in heads]
                return c[0], o[0], c[1], o[1], z_next[0], z_next[1]

            zc = jnp.zeros((BLOCK, 1), F32)
            zo = jnp.zeros((BLOCK, LANES), F32)
            st = group(gd, (zc, zo, zc, zo, *logits(gd)), (gd * SB_KG + colg) < (q0 + rowg))
            c0, o0, c1, o1, _, _ = lax.fori_loop(0, gd, lambda t, st: group(gd - 1 - t, st, None), st)
            o_ref[pl.ds(q0, BLOCK), :] = jnp.where(lo, o0, o1)
            tot_ref[pl.ds(q0, BLOCK), :] = jnp.where(lo, c0, c1)
            return carry

        lax.fori_loop(0, nq, qloop, 0)

    slab = lambda b0: pl.BlockSpec((s, LANES), lambda p: (0, b0 + p), pipeline_mode=pl.Buffered(1))
    w = LANES * n_slabs
    return pl.pallas_call(
        body, name=name, grid=(n_slabs,),
        in_specs=[slab(qb0), slab(kb0), slab(vb0)],
        out_specs=[pl.BlockSpec((s, LANES), lambda p: (0, p)), pl.BlockSpec((s, LANES), lambda p: (0, p))],
        out_shape=[jax.ShapeDtypeStruct((s, w), F32), jax.ShapeDtypeStruct((s, w), F32)],
        scratch_shapes=[pltpu.VMEM((s, LANES), BF16) for _ in range(4)],
        compiler_params=_params(("parallel",), VMEM_LIMIT),
    )(proj, proj, proj)


def sb_bwd(proj, qb0, kb0, vb0, n_slabs, dout, tot, *, name):
    s = proj.shape[0]
    nq = s // BLOCK
    nch = s // NORM_CH
    nsub = SB_KG // BLOCK
    scale = HEAD_DIM ** -0.5

    def body(q_ref, k_ref, v_ref, do_ref, tot_ref, dq_ref, dk_ref, dv_ref,
             qlo_s, qhi_s, k_s, v_s, dlo_s, dhi_s):
        def prep(c, carry):
            rows = pl.ds(pl.multiple_of(c * NORM_CH, NORM_CH), NORM_CH)
            lo = _lo_mask((NORM_CH, LANES))
            qv = q_ref[rows, :] * scale
            dv = do_ref[rows, :]
            qlo_s[rows, :] = jnp.where(lo, qv, 0.0).astype(BF16)
            qhi_s[rows, :] = jnp.where(lo, 0.0, qv).astype(BF16)
            dlo_s[rows, :] = jnp.where(lo, dv, 0.0).astype(BF16)
            dhi_s[rows, :] = jnp.where(lo, 0.0, dv).astype(BF16)
            k_s[rows, :] = k_ref[rows, :].astype(BF16)
            v_s[rows, :] = v_ref[rows, :].astype(BF16)
            z = jnp.zeros((NORM_CH, LANES), F32)
            dk_ref[rows, :] = z
            dv_ref[rows, :] = z
            return carry

        lax.fori_loop(0, nch, prep, 0)

        row = lax.broadcasted_iota(I32, (BLOCK, BLOCK), 0)
        col = lax.broadcasted_iota(I32, (BLOCK, BLOCK), 1)
        lo = col < HEAD_DIM
        t_le = (row <= col).astype(BF16)
        rowg = lax.broadcasted_iota(I32, (BLOCK, SB_KG), 0)
        colg = lax.broadcasted_iota(I32, (BLOCK, SB_KG), 1)

        heads = range(2)
        piece = lambda x, j: x[:, j * BLOCK:(j + 1) * BLOCK]

        def prefixes(x):
            return [[_split_dot(piece(x[h], j), t_le) for j in range(nsub)] for h in heads]

        def chain(pre, run, total=None):
            out = []
            for j in range(nsub):
                out.append(run + pre[j] if total is None else total - run - pre[j])
                run = run + pre[j][:, BLOCK - 1:BLOCK]
            return jnp.concatenate(out, axis=1), run

        def qloop(qi, carry):
            q0 = pl.multiple_of(qi * BLOCK, BLOCK)
            qh = [qlo_s[pl.ds(q0, BLOCK), :], qhi_s[pl.ds(q0, BLOCK), :]]
            doh = [dlo_s[pl.ds(q0, BLOCK), :], dhi_s[pl.ds(q0, BLOCK), :]]
            tot_t = tot_ref[pl.ds(q0, BLOCK), :]
            tots = [tot_t[:, 0:1], tot_t[:, HEAD_DIM:HEAD_DIM + 1]]
            gd = qi // nsub

            def logits(gi):
                kg = k_s[pl.ds(pl.multiple_of(gi * SB_KG, SB_KG), SB_KG), :]
                return [_dot_nt(qh[h], kg) for h in heads]

            def group(gi, st, mask):
                k0 = pl.multiple_of(gi * SB_KG, SB_KG)
                kg, vg = k_s[pl.ds(k0, SB_KG), :], v_s[pl.ds(k0, SB_KG), :]
                cp, cg = [st[0], st[2]], [st[1], st[3]]
                z = [st[5], st[6]]
                z_next = logits(jnp.minimum(gi + 1, gd))
                da = [_dot_nt(doh[h], vg) for h in heads]
                sp = [_softplus(z[h]) for h in heads]
                lrem = [-sp[h] if mask is None else jnp.where(mask, -sp[h], 0.0) for h in heads]
                pre = prefixes(lrem)
                e, a, g = [], [], []
                for h in heads:
                    suffix, cp[h] = chain(pre[h], cp[h], tots[h])
                    e.append(z[h] - sp[h])
                    ah = jnp.exp(e[h] + suffix)
                    a.append(ah if mask is None else jnp.where(mask, ah, 0.0))
                    g.append(a[h] * da[h])
                gpre = prefixes(g)
                dz = []
                for h in heads:
                    ginc, cg[h] = chain(gpre[h], cg[h])
                    dzh = g[h] - jnp.exp(e[h]) * ginc
                    dz.append((dzh if mask is None else jnp.where(mask, dzh, 0.0)).astype(BF16))
                ab = [a[h].astype(BF16) for h in heads]
                dq = st[4] + jnp.where(lo, _dot(dz[0], kg), _dot(dz[1], kg))
                dk_ref[pl.ds(k0, SB_KG), :] += _dot_tn(dz[0], qh[0]) + _dot_tn(dz[1], qh[1])
                dv_ref[pl.ds(k0, SB_KG), :] += _dot_tn(ab[0], doh[0]) + _dot_tn(ab[1], doh[1])
                return cp[0], cg[0], cp[1], cg[1], dq, z_next[0], z_next[1]

            zc = jnp.zeros((BLOCK, 1), F32)
            st = lax.fori_loop(0, gd, lambda gi, st: group(gi, st, None),
                               (zc, zc, zc, zc, jnp.zeros((BLOCK, LANES), F32), *logits(0)))
            st = group(gd, st, (gd * SB_KG + colg) < (q0 + rowg))
            dq_ref[pl.ds(q0, BLOCK), :] = st[4] * scale
            return carry

        lax.fori_loop(0, nq, qloop, 0)

    slab = lambda b0: pl.BlockSpec((s, LANES), lambda p: (0, b0 + p), pipeline_mode=pl.Buffered(1))
    own = pl.BlockSpec((s, LANES), lambda p: (0, p), pipeline_mode=pl.Buffered(1))
    w = LANES * n_slabs
    outb = pl.BlockSpec((s, LANES), lambda p: (0, p))
    return pl.pallas_call(
        body, name=name, grid=(n_slabs,),
        in_specs=[slab(qb0), slab(kb0), slab(vb0), own, own],
        out_specs=[outb, outb, outb],
        out_shape=[jax.ShapeDtypeStruct((s, w), F32)] * 3,
        scratch_shapes=[pltpu.VMEM((s, LANES), BF16) for _ in range(6)],
        compiler_params=_params(("parallel",), VMEM_LIMIT),
    )(proj, proj, proj, dout, tot)


def _place():
    x, y, c = lax.axis_index("x"), lax.axis_index("y"), lax.axis_index("c")
    return x, y, c


def gather_small(v, *, name):
    m_per, n = v.shape

    def body(x_ref, out_ref, send_sems, recv_sems, local_sem):
        x, y, c = _place()
        me, sibling = (x, y, c), (x, y, 1 - c)
        chips = [(1 - x, y), (x, 1 - y), (1 - x, 1 - y)]

        def rows(px, py, pc):
            return out_ref.at[pl.ds((4 * px + 2 * py + pc) * m_per, m_per), :]

        def copy(k, block, to, src=None):
            return pltpu.make_async_remote_copy(
                src_ref=rows(*block) if src is None else src, dst_ref=rows(*block),
                send_sem=send_sems.at[k], recv_sem=recv_sems.at[k], device_id=to, device_id_type=MESH)

        mine = pltpu.make_async_copy(x_ref, rows(*me), local_sem)
        mine.start()
        first = [copy(0, me, sibling, src=x_ref)]
        first += [copy(1 + j, me, (*chip, c), src=x_ref) for j, chip in enumerate(chips)]
        for cp in first:
            cp.start()
        passed = [copy(4 + j, (*chip, c), sibling) for j, chip in enumerate(chips)]
        for j, chip in enumerate(chips):
            copy(1 + j, (*chip, c), me).wait_recv()
            passed[j].start()
        copy(0, sibling, me).wait_recv()
        for j, chip in enumerate(chips):
            copy(4 + j, (*chip, 1 - c), me).wait_recv()
        for cp in first + passed:
            cp.wait_send()
        mine.wait()

    return pl.pallas_call(
        body, name=name,
        out_shape=jax.ShapeDtypeStruct((N_DEV * m_per, n), v.dtype),
        in_specs=[pl.BlockSpec(memory_space=pltpu.VMEM)],
        out_specs=pl.BlockSpec(memory_space=pltpu.VMEM),
        scratch_shapes=[pltpu.SemaphoreType.DMA((7,)), pltpu.SemaphoreType.DMA((7,)), pltpu.SemaphoreType.DMA],
        compiler_params=_params(None, VMEM_LIMIT),
    )(v)


_HBM = pl.BlockSpec(memory_space=pltpu.HBM)
_SEM = pl.BlockSpec(memory_space=pltpu.SEMAPHORE)
_EFFECT = pltpu.SideEffectType.DATAFLOW_SIDE_EFFECTING


def _peer_copies(src_refs, land_refs, send_sems, recv_sems, per_dest):
    x, y, c = _place()
    me = 4 * x + 2 * y + c
    copies = []
    for src, land, ssem, rsem in zip(src_refs, land_refs, send_sems, recv_sems):
        for k in (1, 2, 4, 3, 5, 6, 7):
            px, py, pc = x ^ (k >> 2 & 1), y ^ (k >> 1 & 1), c ^ (k & 1)
            copies.append(pltpu.make_async_remote_copy(
                src_ref=src.at[4 * px + 2 * py + pc] if per_dest else src, dst_ref=land.at[me],
                send_sem=ssem.at[k - 1], recv_sem=rsem.at[k - 1], device_id=(px, py, pc), device_id_type=MESH))
    return copies


def exchange_start(srcs, lands, per_dest, *, name):
    n = len(srcs)

    def body(*refs):
        src_refs, land_refs = refs[:n], refs[n:2 * n]
        send_sems, recv_sems = refs[2 * n:3 * n], refs[3 * n:4 * n]
        token = refs[-1]
        for cp in _peer_copies(src_refs, land_refs, send_sems, recv_sems, per_dest):
            cp.start()
        token[...] = jnp.zeros_like(token)

    hbm = lambda a: pltpu.HBM(a.shape, a.dtype)
    res = pl.pallas_call(
        body, name=name,
        out_shape=(*[pltpu.SemaphoreType.DMA((7,))] * (2 * n),
                   *[hbm(a) for a in srcs], *[hbm(a) for a in lands], jax.ShapeDtypeStruct((8, LANES), F32)),
        in_specs=[_HBM] * (2 * n),
        out_specs=(*[_SEM] * (2 * n), *[_HBM] * (2 * n), pl.BlockSpec(memory_space=pltpu.VMEM)),
        input_output_aliases={i: 2 * n + i for i in range(2 * n)},
        compiler_params=pltpu.CompilerParams(has_side_effects=_EFFECT),
    )(*[pltpu.with_memory_space_constraint(a, pltpu.HBM) for a in (*srcs, *lands)])
    handles = [(res[a], res[n + a], res[2 * n + a], res[3 * n + a]) for a in range(n)]
    return handles, res[-1]


def exchange_wait(handles, per_dest, after, *, name):
    n = len(handles)

    def body(*refs):
        src_refs, land_refs = refs[:n], refs[n:2 * n]
        send_sems, recv_sems = refs[2 * n:3 * n], refs[3 * n:4 * n]
        for cp in _peer_copies(src_refs, land_refs, send_sems, recv_sems, per_dest):
            cp.wait_send()
            cp.wait_recv()

    srcs, lands = [h[2] for h in handles], [h[3] for h in handles]
    hbm = lambda a: pltpu.HBM(a.shape, a.dtype)
    res = pl.pallas_call(
        body, name=name,
        out_shape=(*[hbm(a) for a in srcs], *[hbm(a) for a in lands]),
        in_specs=[*[_HBM] * (2 * n), *[_SEM] * (2 * n), pl.BlockSpec(memory_space=pl.ANY)],
        out_specs=tuple([_HBM] * (2 * n)),
        input_output_aliases={i: i for i in range(2 * n)},
        compiler_params=pltpu.CompilerParams(has_side_effects=_EFFECT),
    )(*srcs, *lands, *[h[0] for h in handles], *[h[1] for h in handles], after)
    return res[n:]


def _adamw_math(w, g, m, v):
    m = ADAM_B1 * m + (1.0 - ADAM_B1) * g
    v = ADAM_B2 * v + (1.0 - ADAM_B2) * (g * g)
    m_hat = m / (1.0 - ADAM_B1 ** ADAM_STEP)
    v_hat = v / (1.0 - ADAM_B2 ** ADAM_STEP)
    delta = -ADAM_LR * (m_hat / (jnp.sqrt(v_hat) + ADAM_EPS) + ADAM_WD * w)
    return delta, m, v


def adamw_parts(parts, w, m, v, layer, outs, *, name):
    depth, r, cdim = w.shape
    n_parts = parts.shape[0]
    tr = _pick(r, [t for t in (512, 256, 128, 64, 32, 16) if t * cdim <= 256 * 1024])

    def body(p_ref, w_ref, m_ref, v_ref, g0, d0, nm0, nv0, g_ref, d_ref, nm_ref, nv_ref):
        g = p_ref[0].astype(F32)
        for q in range(1, n_parts):
            g = g + p_ref[q].astype(F32)
        delta, nm, nv = _adamw_math(w_ref[...], g, m_ref[...], v_ref[...])
        g_ref[...], d_ref[...], nm_ref[...], nv_ref[...] = g, delta, nm, nv

    t = pl.BlockSpec((None, tr, cdim), lambda i: (layer, i, 0))
    held = pl.BlockSpec(memory_space=pl.ANY)
    return pl.pallas_call(
        body, name=name, grid=(r // tr,),
        in_specs=[pl.BlockSpec((n_parts, tr, cdim), lambda i: (0, i, 0)), t, t, t, held, held, held, held],
        out_specs=[t, t, t, t],
        out_shape=[jax.ShapeDtypeStruct((depth, r, cdim), F32)] * 4,
        input_output_aliases={4: 0, 5: 1, 6: 2, 7: 3},
        compiler_params=_params(("parallel",), VMEM_LIMIT),
    )(parts, w, m, v, *outs)


def sum_devices(gathered, *, name):
    m_rows = gathered.shape[1]

    def body(ga_ref, g_ref):
        g = ga_ref[0]
        for dev in range(1, N_DEV):
            g = g + ga_ref[dev]
        g_ref[...] = g

    return pl.pallas_call(
        body, name=name, out_shape=jax.ShapeDtypeStruct((m_rows, LANES), F32),
        compiler_params=_params(None, VMEM_LIMIT),
    )(gathered)


def adamw_small(g, w, m, v, *, name):
    m_rows = w.shape[0]

    def body(g_ref, w_ref, m_ref, v_ref, d_ref, nm_ref, nv_ref):
        d_ref[...], nm_ref[...], nv_ref[...] = _adamw_math(w_ref[...], g_ref[...], m_ref[...], v_ref[...])

    return pl.pallas_call(
        body, name=name, out_shape=[jax.ShapeDtypeStruct((m_rows, LANES), F32)] * 3,
        compiler_params=_params(None, VMEM_LIMIT),
    )(g, w, m, v)


def _t5_bucket(dist):
    max_exact = N_BUCKETS // 2
    d = jnp.maximum(dist, 0)
    large = max_exact + (jnp.log(jnp.maximum(d, 1).astype(F32) / max_exact)
                         / math.log(T5_MAX_DIST / max_exact) * (N_BUCKETS - max_exact)).astype(I32)
    large = jnp.minimum(large, N_BUCKETS - 1)
    return jnp.where(d < max_exact, d, large)


def _rel():
    return jnp.arange(BLOCK)[:, None] + BLOCK - jnp.arange(2 * BLOCK)[None, :]


def _band_bias(table, dils, max_dists):
    rel = _rel()
    biases, buckets = [], []
    for d, md in zip(dils, max_dists):
        bk = _t5_bucket(rel * d)
        vis = (rel >= 0) & (rel <= md)
        looked_up = jnp.zeros((table.shape[1],) + rel.shape, F32)
        for b in range(N_BUCKETS):
            looked_up = jnp.where((bk == b)[None], table[b][:, None, None], looked_up)
        with_prev = jnp.where(vis[None], looked_up, NEG_INF)
        first = jnp.arange(2 * BLOCK)[None, None, :] >= BLOCK
        biases.append(jnp.stack([with_prev, jnp.where(first, with_prev, NEG_INF)]))
        buckets.append(bk.astype(I32))
    return jnp.stack(biases), jnp.stack(buckets)


def _pack(pieces, rows):
    flat = jnp.concatenate([p.reshape(-1) for p in pieces])
    return jnp.pad(flat, (0, rows * LANES - flat.shape[0])).reshape(rows, LANES)


def _unpack(packed, shapes):
    flat = packed.reshape(-1)
    out, off = [], 0
    for sh in shapes:
        n = math.prod(sh)
        out.append(flat[off:off + n].reshape(sh))
        off += n
    return out


def _tile2(g):
    return jnp.concatenate([g, g])


def kernel(x, attn_norm, w_in, a_q_gain, a_k_gain, a_sinks, c_q_gain, c_k_gain, rel_bias_table, mix_out_gain, w_out, ffn_norm, w_up, conv_w, conv_b, w_down, loss_target, m_attn_norm, m_w_in, m_a_q_gain, m_a_k_gain, m_a_sinks, m_c_q_gain, m_c_k_gain, m_rel_bias_table, m_mix_out_gain, m_w_out, m_ffn_norm, m_w_up, m_conv_w, m_conv_b, m_w_down, v_attn_norm, v_w_in, v_a_q_gain, v_a_k_gain, v_a_sinks, v_c_q_gain, v_c_k_gain, v_rel_bias_table, v_mix_out_gain, v_w_out, v_ffn_norm, v_w_up, v_conv_w, v_conv_b, v_w_down):
    depth, d_model, in_shard = w_in.shape
    ff2_shard = w_up.shape[2]
    s = x.shape[1]
    in_width, ff2 = N_DEV * in_shard, N_DEV * ff2_shard
    n_heads = d_model // HEAD_DIM
    ha, hb, hc = n_heads // 4, n_heads // 4, n_heads // 2
    sa, sb, sc = ha // 2, hb // 2, hc // 2
    kv_a = ha // 4
    assert kv_a == 2 and BLOCK == LANES
    cb_aq, cb_ak, cb_av = 0, sa, sa + 1
    cb_bq = sa + 2
    cb_bk, cb_bv = cb_bq + sb, cb_bq + 2 * sb
    cb_cq = cb_bq + 3 * sb
    cb_ck, cb_cv = cb_cq + sc, cb_cq + 2 * sc
    assert (cb_cv + sc) * LANES == in_width
    dev = 4 * lax.axis_index("x") + 2 * lax.axis_index("y") + lax.axis_index("c")

    def landing(own):
        return lax.dynamic_update_slice_in_dim(lax.empty((N_DEV,) + own.shape, own.dtype), own[None], dev, axis=0)

    wnames = ("w_in", "w_out", "w_up", "w_down", "conv_w")
    cols_to_rows = lambda g: jnp.transpose(g, (1, 0, 2)).reshape(g.shape[1], N_DEV * g.shape[2])
    whole = dict(w_in=cols_to_rows, w_up=cols_to_rows, conv_w=cols_to_rows,
                 w_out=lambda g: g.reshape(d_model, d_model), w_down=lambda g: g.reshape(ff2 // 2, d_model))
    gathers = {}
    token = jnp.zeros((8, LANES), F32)
    for l in range(depth):
        for gi, group in enumerate([[n] for n in wnames] if l == 0 else [wnames]):
            srcs = [(dict(w_in=w_in, w_out=w_out, w_up=w_up, w_down=w_down, conv_w=conv_w)[n][l] + token[0, 0])
                    .astype(F32 if n == "conv_w" else BF16) for n in group]
            handles, token = exchange_start(srcs, [landing(a) for a in srcs], False, name=f"gather_start_{l}_{gi}")
            gathers.update({(l, n): h for n, h in zip(group, handles)})

    def gathered(l, names, after):
        landed = exchange_wait([gathers[l, n] for n in names], False, after,
                               name=f"gather_wait_{l}_{wnames.index(names[0])}")
        return {n: whole[n](g) for n, g in zip(names, landed)}

    bias_a, buckets_a = _band_bias(rel_bias_table[:, :ha], (1,), (WINDOW_A - 1,))
    bias_c, buckets_c = _band_bias(rel_bias_table[:, ha:], DILATIONS, (BLOCK,) * len(DILATIONS))

    xs = x[0]
    saved = []
    wi, wo, wu, wd, cw = ([None] * depth for _ in range(5))
    for l in range(depth):
        if l == 0:
            need = lambda n, after: gathered(0, (n,), after)[n]
        else:
            layer_w = gathered(l, wnames, xs)
            need = lambda n, after: layer_w[n]
        wi[l] = need("w_in", token)
        h1 = rmsnorm_fwd(xs, attn_norm[l], name="attn_norm_fwd")
        proj = matmul(h1, wi[l], name="in_proj")
        sinks = jnp.repeat(a_sinks[l], HEAD_DIM).reshape(sa, 1, LANES)
        gaq, gak = _tile2(a_q_gain[l]), _tile2(a_k_gain[l])
        gcq, gck = _tile2(c_q_gain[l]), _tile2(c_k_gain[l])
        out_a, lse_a = banded_fwd(proj, cb_aq, cb_ak, cb_av, sa, gaq, gak, bias_a, (1,), sinks, True, name="swa_fwd")
        out_b, tot_b = sb_fwd(proj, cb_bq, cb_bk, cb_bv, sb, name="stick_fwd")
        out_c, lse_c = banded_fwd(proj, cb_cq, cb_ck, cb_cv, sc, gcq, gck, bias_c, DILATIONS, None, False,
                                  name="dilated_fwd")
        mix = mixnorm_fwd([out_a, out_b, out_c], mix_out_gain[l], name="mix_norm_fwd")
        wo[l] = need("w_out", mix)
        x_mid = matmul(mix, wo[l], res=xs, name="out_proj")
        h2 = rmsnorm_fwd(x_mid, ffn_norm[l], name="ffn_norm_fwd")
        wu[l] = need("w_up", h2)
        p = matmul(h2, wu[l], name="up_proj")
        cw[l] = need("conv_w", p)
        act = ffn_act_fwd(p, cw[l], conv_b[l], name="ffn_act_fwd")
        wd[l] = need("w_down", act)
        x_out = matmul(act, wd[l], res=x_mid, name="down_proj")
        saved.append(dict(x_in=xs, h1=h1, proj=proj, out_a=out_a, lse_a=lse_a, out_b=out_b, tot_b=tot_b,
                          out_c=out_c, lse_c=lse_c, mix=mix, x_mid=x_mid, h2=h2, p=p, act=act,
                          sinks=sinks, gains=(gaq, gak, gcq, gck)))
        xs = x_out

    dx, loss_part = loss_head(xs, loss_target[0], name="loss_head")

    small = {k: [None] * depth for k in ("attn_norm", "a_q_gain", "a_k_gain", "a_sinks", "c_q_gain", "c_k_gain",
                                         "mix_out_gain", "ffn_norm", "conv_w", "conv_b")}
    big = {k: [None] * depth for k in ("w_in", "w_out", "w_up", "w_down")}
    dbias_a = dbias_c = None
    scatters = {}
    token = jnp.zeros((8, LANES), F32)
    names_big = ("w_in", "w_out", "w_up", "w_down")

    def scatter(l, names):
        parts = [big[n][l] for n in names]
        lands = [landing(lax.dynamic_index_in_dim(pt, dev, axis=0, keepdims=False)) for pt in parts]
        handles, tok = exchange_start(parts, lands, True, name=f"scatter_start_{l}_{names_big.index(names[0])}")
        scatters.update({(l, n): h for n, h in zip(names, handles)})
        return tok

    by_cols = lambda a: jnp.transpose(a.reshape(a.shape[0], N_DEV, a.shape[1] // N_DEV), (1, 0, 2))
    by_rows = lambda a: a.reshape(N_DEV, a.shape[0] // N_DEV, a.shape[1])
    for l in reversed(range(depth)):
        each = l == 0
        sv = saved[l]
        gaq, gak, gcq, gck = sv["gains"]
        da = matmul(dx, wd[l], trans_b=True, name="down_proj_dx")
        big["w_down"][l] = by_rows(matmul(sv["act"], dx, trans_a=True, out_dtype=BF16, name="down_proj_dw"))
        if each:
            token = scatter(l, ("w_down",))
        dp, small["conv_w"][l], small["conv_b"][l] = ffn_act_bwd(da, sv["p"], cw[l], conv_b[l] + token[0, 0],
                                                                 name="ffn_act_bwd")
        dh2 = matmul(dp, wu[l], trans_b=True, name="up_proj_dx")
        big["w_up"][l] = matmul(sv["h2"], dp, trans_a=True, out_dtype=BF16, col_blocks=N_DEV, name="up_proj_dw")
        if each:
            token = scatter(l, ("w_up",))
        dx_mid, small["ffn_norm"][l] = rmsnorm_bwd(dh2, sv["x_mid"], ffn_norm[l] + token[0, 0], dx,
                                                   name="ffn_norm_bwd")
        dmix = matmul(dx_mid, wo[l], trans_b=True, name="out_proj_dx")
        big["w_out"][l] = by_rows(matmul(sv["mix"], dx_mid, trans_a=True, out_dtype=BF16, name="out_proj_dw"))
        if each:
            token = scatter(l, ("w_out",))
        (d_oa, d_ob, d_oc), small["mix_out_gain"][l] = mixnorm_bwd(
            dmix, [sv["out_a"], sv["out_b"], sv["out_c"]], mix_out_gain[l] + token[0, 0], name="mix_norm_bwd")
        dq_a, dk_a, dv_a, db_a, dgq_a, dgk_a, dsink = banded_bwd(
            sv["proj"], cb_aq, cb_ak, cb_av, sa, gaq, gak, bias_a, (1,), sv["sinks"], True,
            d_oa, sv["out_a"], sv["lse_a"], name="swa_bwd")
        dq_b, dk_b, dv_b = sb_bwd(sv["proj"], cb_bq, cb_bk, cb_bv, sb, d_ob, sv["tot_b"], name="stick_bwd")
        dq_c, dk_c, dv_c, db_c, dgq_c, dgk_c = banded_bwd(
            sv["proj"], cb_cq, cb_ck, cb_cv, sc, gcq, gck, bias_c, DILATIONS, None, False,
            d_oc, sv["out_c"], sv["lse_c"], name="dilated_bwd")
        fold = lambda g: g.reshape(-1, HEAD_DIM).sum(axis=0)
        small["a_q_gain"][l], small["a_k_gain"][l] = fold(dgq_a), fold(dgk_a)
        small["c_q_gain"][l], small["c_k_gain"][l] = fold(dgq_c), fold(dgk_c)
        small["a_sinks"][l] = dsink[:, ::HEAD_DIM].reshape(-1)
        dbias_a = db_a if dbias_a is None else dbias_a + db_a
        dbias_c = db_c if dbias_c is None else dbias_c + db_c
        dproj = jnp.concatenate([dq_a, dk_a, dv_a, dq_b, dk_b, dv_b, dq_c, dk_c, dv_c], axis=1)
        big["w_in"][l] = by_cols(matmul(sv["h1"], dproj, trans_a=True, out_dtype=BF16, name="in_proj_dw"))
        token = scatter(l, ("w_in",) if each else names_big)
        dh1 = matmul(dproj, wi[l], trans_b=True, name="in_proj_dx")
        dx, small["attn_norm"][l] = rmsnorm_bwd(dh1, sv["x_in"], attn_norm[l] + token[0, 0], dx_mid,
                                                name="attn_norm_bwd")

    dtable = jnp.concatenate([bias_bwd(dbias_a, buckets_a, name="swa_bias_bwd"),
                              bias_bwd(dbias_c, buckets_c, name="dilated_bias_bwd")], axis=1)

    order = ("attn_norm", "a_q_gain", "a_k_gain", "a_sinks", "c_q_gain", "c_k_gain", "rel_bias_table",
             "mix_out_gain", "ffn_norm", "conv_w", "conv_b")
    partial = {k: jnp.stack(v) for k, v in small.items()}
    partial["rel_bias_table"] = dtable
    pieces = [partial[k] for k in order] + [loss_part.reshape(1)]
    n_small = sum(math.prod(pc.shape) for pc in pieces)
    rows = -(-n_small // (8 * LANES)) * 8
    gathered = gather_small(_pack(pieces, rows), name="gather_small_grads")
    summed = _unpack(sum_devices(gathered.reshape(N_DEV, rows, LANES), name="sum_small_grads"),
                     [pc.shape for pc in pieces])
    g_small = dict(zip(order, summed[:-1]))
    loss = summed[-1][0]
    g_small["conv_w"] = lax.dynamic_slice_in_dim(g_small["conv_w"], dev * ff2_shard, ff2_shard, axis=2)

    w_small = dict(attn_norm=attn_norm, a_q_gain=a_q_gain, a_k_gain=a_k_gain, a_sinks=a_sinks, c_q_gain=c_q_gain,
                   c_k_gain=c_k_gain, rel_bias_table=rel_bias_table, mix_out_gain=mix_out_gain, ffn_norm=ffn_norm,
                   conv_w=conv_w, conv_b=conv_b)
    m_small = dict(attn_norm=m_attn_norm, a_q_gain=m_a_q_gain, a_k_gain=m_a_k_gain, a_sinks=m_a_sinks,
                   c_q_gain=m_c_q_gain, c_k_gain=m_c_k_gain, rel_bias_table=m_rel_bias_table,
                   mix_out_gain=m_mix_out_gain, ffn_norm=m_ffn_norm, conv_w=m_conv_w, conv_b=m_conv_b)
    v_small = dict(attn_norm=v_attn_norm, a_q_gain=v_a_q_gain, a_k_gain=v_a_k_gain, a_sinks=v_a_sinks,
                   c_q_gain=v_c_q_gain, c_k_gain=v_c_k_gain, rel_bias_table=v_rel_bias_table,
                   mix_out_gain=v_mix_out_gain, ffn_norm=v_ffn_norm, conv_w=v_conv_w, conv_b=v_conv_b)
    shapes = [w_small[k].shape for k in order]
    n_upd = sum(math.prod(sh) for sh in shapes)
    urows = -(-n_upd // (8 * LANES)) * 8
    packs = [_pack([d[k] for k in order], urows) for d in (g_small, w_small, m_small, v_small)]
    upd = adamw_small(*packs, name="adamw_small")
    delta_s, newm_s, newv_s = [dict(zip(order, _unpack(u, shapes))) for u in upd]

    w_big = dict(w_in=(w_in, m_w_in, v_w_in), w_out=(w_out, m_w_out, v_w_out), w_up=(w_up, m_w_up, v_w_up),
                 w_down=(w_down, m_w_down, v_w_down))
    results = {k: [lax.empty(w_big[k][0].shape, F32) for _ in range(4)] for k in names_big}
    after = upd[0]
    for l in reversed(range(depth)):
        landed = exchange_wait([scatters[l, n] for n in names_big], True, after, name=f"scatter_wait_{l}")
        for k, parts in zip(names_big, landed):
            results[k] = adamw_parts(parts, *w_big[k], l, results[k], name="adamw_large")
        after = results[names_big[-1]][0]
    g_big, delta_b, newm_b, newv_b = [{k: results[k][i] for k in names_big} for i in range(4)]

    all_names = ("attn_norm", "w_in", "a_q_gain", "a_k_gain", "a_sinks", "c_q_gain", "c_k_gain", "rel_bias_table",
                 "mix_out_gain", "w_out", "ffn_norm", "w_up", "conv_w", "conv_b", "w_down")
    pick = lambda sm, bg: [bg[k] if k in bg else sm[k] for k in all_names]
    return (loss, dx[None], *pick(g_small, g_big), *pick(delta_s, delta_b), *pick(newm_s, newm_b),
            *pick(newv_s, newv_b))
```

```python
import functools
import math

import jax
import jax.numpy as jnp
from jax import lax
from jax.experimental import pallas as pl
from jax.experimental.pallas import tpu as pltpu

F32, BF16, I32 = jnp.float32, jnp.bfloat16, jnp.int32
MESH = pl.DeviceIdType.MESH

HEAD_DIM = 64
LANES = 128
BLOCK = 128
EPS = 1e-6
NEG_INF = -1e30
N_BUCKETS = 32
T5_MAX_DIST = 2048
WINDOW_A = 128
DILATIONS = (1, 4, 16)
N_DEV = 8
VMEM_LIMIT = 56 * 1024 * 1024
MATMUL_VMEM = 46 * 1024 * 1024

ADAM_LR, ADAM_B1, ADAM_B2, ADAM_EPS, ADAM_WD, ADAM_STEP = 0.001, 0.9, 0.999, 1e-08, 0.01, 10


def _params(sem=None, vmem=None):
    return pltpu.CompilerParams(dimension_semantics=sem, vmem_limit_bytes=vmem)


def _pick(n, cands):
    for c in cands:
        if n % c == 0:
            return c
    raise ValueError(f"no tile for {n}")


def _dot(a, b):
    return lax.dot_general(a, b, (((1,), (0,)), ((), ())), preferred_element_type=F32)


def _dot_nt(a, b):
    return lax.dot_general(a, b, (((1,), (1,)), ((), ())), preferred_element_type=F32)


def _dot_tn(a, b):
    return lax.dot_general(a, b, (((0,), (0,)), ((), ())), preferred_element_type=F32)


def matmul(a, b, *, trans_a=False, trans_b=False, out_dtype=F32, res=None, col_blocks=None, name):
    m, k = (a.shape[1], a.shape[0]) if trans_a else a.shape
    n = b.shape[0] if trans_b else b.shape[1]
    tm = _pick(m, (1408, 1024, 512, 256))
    tn_cands = (n // col_blocks,) if col_blocks else tuple(t for t in (1408, 1024, 768, 512, 256, 128) if n % t == 0)

    def footprint(tk, tn):
        tiles = 2 * (tm * tk * a.dtype.itemsize + tk * tn * b.dtype.itemsize)
        return tiles + tm * tn * (4 + 2 * jnp.dtype(out_dtype).itemsize + (8 if res is not None else 0))

    tk, tn = next((tk, tn) for tk in (2816, 2048, 1792, 1024, 768, 512, 256) if k % tk == 0
                  for tn in tn_cands if footprint(tk, tn) <= MATMUL_VMEM)
    nk = k // tk
    dn = (((0 if trans_a else 1,), (1 if trans_b else 0,)), ((), ()))

    def body(*refs):
        if res is None:
            a_ref, b_ref, o_ref, acc = refs
        else:
            a_ref, b_ref, r_ref, o_ref, acc = refs
        kk = pl.program_id(2)

        @pl.when(kk == 0)
        def _():
            acc[...] = jnp.zeros_like(acc)

        acc[...] += lax.dot_general(a_ref[...].astype(BF16), b_ref[...].astype(BF16), dn,
                                    preferred_element_type=F32)

        @pl.when(kk == nk - 1)
        def _():
            r = acc[...]
            if res is not None:
                r = r_ref[...] + r
            o_ref[...] = r.astype(out_dtype)

    b_spec = (pl.BlockSpec((tn, tk), lambda i, j, kk: (j, kk)) if trans_b
              else pl.BlockSpec((tk, tn), lambda i, j, kk: (kk, j)))
    a_spec = (pl.BlockSpec((tk, tm), lambda i, j, kk: (kk, i)) if trans_a
              else pl.BlockSpec((tm, tk), lambda i, j, kk: (i, kk)))
    in_specs = [a_spec, b_spec]
    args = [a, b]
    if res is not None:
        in_specs.append(pl.BlockSpec((tm, tn), lambda i, j, kk: (i, j)))
        args.append(res)
    if col_blocks:
        out_spec = pl.BlockSpec((None, tm, tn), lambda i, j, kk: (j, i, 0))
        out_shape = jax.ShapeDtypeStruct((col_blocks, m, tn), out_dtype)
    else:
        out_spec = pl.BlockSpec((tm, tn), lambda i, j, kk: (i, j))
        out_shape = jax.ShapeDtypeStruct((m, n), out_dtype)
    return pl.pallas_call(
        body, name=name, grid=(m // tm, n // tn, nk),
        in_specs=in_specs, out_specs=out_spec, out_shape=out_shape,
        scratch_shapes=[pltpu.VMEM((tm, tn), F32)],
        compiler_params=_params(("parallel", "parallel", "arbitrary"), VMEM_LIMIT),
    )(*args)


def rmsnorm_fwd(x, g, *, name):
    s, d = x.shape
    tm = 512

    def body(x_ref, g_ref, o_ref):
        xv = x_ref[...]
        r = lax.rsqrt(jnp.mean(xv * xv, axis=-1, keepdims=True) + EPS)
        o_ref[...] = (xv * r * g_ref[...]).astype(BF16)

    return pl.pallas_call(
        body, name=name, grid=(s // tm,),
        in_specs=[pl.BlockSpec((tm, d), lambda i: (i, 0)), pl.BlockSpec((1, d), lambda i: (0, 0))],
        out_specs=pl.BlockSpec((tm, d), lambda i: (i, 0)),
        out_shape=jax.ShapeDtypeStruct((s, d), BF16),
        compiler_params=_params(("parallel",)),
    )(x, g.reshape(1, d))


def rmsnorm_bwd(dh, x, g, dres, *, name):
    s, d = x.shape
    tm = 256

    def body(dh_ref, x_ref, g_ref, dres_ref, dx_ref, dxb_ref, dg_ref):
        @pl.when(pl.program_id(0) == 0)
        def _():
            dg_ref[...] = jnp.zeros_like(dg_ref)

        xv, dhv = x_ref[...], dh_ref[...]
        r = lax.rsqrt(jnp.mean(xv * xv, axis=-1, keepdims=True) + EPS)
        gd = dhv * g_ref[...]
        dot = jnp.mean(gd * xv, axis=-1, keepdims=True)
        dx = dres_ref[...] + (r * gd - xv * (r * r * r * dot))
        dx_ref[...] = dx
        dxb_ref[...] = dx.astype(BF16)
        dg_ref[...] += jnp.sum(dhv * (xv * r), axis=0, keepdims=True)

    row = pl.BlockSpec((tm, d), lambda i: (i, 0))
    dx, dxb, dg = pl.pallas_call(
        body, name=name, grid=(s // tm,),
        in_specs=[row, row, pl.BlockSpec((1, d), lambda i: (0, 0)), row],
        out_specs=[row, row, pl.BlockSpec((1, d), lambda i: (0, 0))],
        out_shape=[jax.ShapeDtypeStruct((s, d), F32), jax.ShapeDtypeStruct((s, d), BF16),
                   jax.ShapeDtypeStruct((1, d), F32)],
        compiler_params=_params(("arbitrary",)),
    )(dh, x, g.reshape(1, d), dres)
    return dx, dxb, dg[0]


def loss_head(y, target, *, name):
    s, d = y.shape
    tm = 512

    def body(y_ref, t_ref, dy_ref, dyb_ref, l_ref):
        @pl.when(pl.program_id(0) == 0)
        def _():
            l_ref[...] = jnp.zeros_like(l_ref)

        e = y_ref[...] - t_ref[...]
        dy = e / float(d)
        dy_ref[...] = dy
        dyb_ref[...] = dy.astype(BF16)
        per_tok = jnp.mean(e * e, axis=-1, keepdims=True)
        l_ref[...] += 0.5 * jnp.sum(per_tok, axis=0, keepdims=True)

    row = pl.BlockSpec((tm, d), lambda i: (i, 0))
    dy, dyb, l = pl.pallas_call(
        body, name=name, grid=(s // tm,),
        in_specs=[row, row],
        out_specs=[row, row, pl.BlockSpec((8, LANES), lambda i: (0, 0))],
        out_shape=[jax.ShapeDtypeStruct((s, d), F32), jax.ShapeDtypeStruct((s, d), BF16),
                   jax.ShapeDtypeStruct((8, LANES), F32)],
        compiler_params=_params(("arbitrary",)),
    )(y, target)
    return dy, dyb, l[0, 0]


FFN_TN = 256
FFN_CH = 256


def _rows_before(ref, r0, first):
    if first:
        cur = ref[pl.ds(0, FFN_CH), :]
        row = lax.broadcasted_iota(I32, cur.shape, 0)
        sh1 = jnp.where(row < 1, 0.0, pltpu.roll(cur, 1, axis=0))
        sh2 = jnp.where(row < 2, 0.0, pltpu.roll(cur, 2, axis=0))
        return cur, sh1, sh2
    ext = ref[pl.ds(pl.multiple_of(r0 - 8, 8), FFN_CH + 8), :]
    return ext[8:], pltpu.roll(ext, 1, axis=0)[8:], pltpu.roll(ext, 2, axis=0)[8:]


def _rows_after(ref, r0, last):
    if last:
        cur = ref[pl.ds(r0, FFN_CH), :]
        row = lax.broadcasted_iota(I32, cur.shape, 0)
        up1 = jnp.where(row >= FFN_CH - 1, 0.0, pltpu.roll(cur, FFN_CH - 1, axis=0))
        up2 = jnp.where(row >= FFN_CH - 2, 0.0, pltpu.roll(cur, FFN_CH - 2, axis=0))
        return cur, up1, up2
    n = FFN_CH + 8
    ext = ref[pl.ds(r0, n), :]
    return ext[:FFN_CH], pltpu.roll(ext, n - 1, axis=0)[:FFN_CH], pltpu.roll(ext, n - 2, axis=0)[:FFN_CH]


def _sigmoid(x):
    return 0.5 * jnp.tanh(0.5 * x) + 0.5


def ffn_act_fwd(p, conv_w, conv_b, *, name):
    s, f2 = p.shape
    f = f2 // 2
    nj = f // FFN_TN
    nch = s // FFN_CH

    def body(pg_ref, pu_ref, wg_ref, wu_ref, bg_ref, bu_ref, a_ref):
        def conv(ref, w_ref, b_ref, r0, first):
            cur, sh1, sh2 = _rows_before(ref, r0, first)
            return ((b_ref[...] + w_ref[0:1, :] * sh2) + w_ref[1:2, :] * sh1) + w_ref[2:3, :] * cur

        def chunk(r0, first):
            gate = conv(pg_ref, wg_ref, bg_ref, r0, first)
            up = conv(pu_ref, wu_ref, bu_ref, r0, first)
            a_ref[pl.ds(r0, FFN_CH), :] = (gate * _sigmoid(gate) * up).astype(BF16)

        chunk(0, True)

        def step(c, carry):
            chunk(pl.multiple_of(c * FFN_CH, FFN_CH), False)
            return carry

        lax.fori_loop(1, nch, step, 0)

    col = lambda off: pl.BlockSpec((s, FFN_TN), lambda j: (0, j + off))
    wcol = lambda off: pl.BlockSpec((3, FFN_TN), lambda j: (0, j + off))
    bcol = lambda off: pl.BlockSpec((1, FFN_TN), lambda j: (0, j + off))
    return pl.pallas_call(
        body, name=name, grid=(nj,),
        in_specs=[col(0), col(nj), wcol(0), wcol(nj), bcol(0), bcol(nj)],
        out_specs=pl.BlockSpec((s, FFN_TN), lambda j: (0, j)),
        out_shape=jax.ShapeDtypeStruct((s, f), BF16),
        compiler_params=_params(("parallel",), VMEM_LIMIT),
    )(p, p, conv_w, conv_w, conv_b.reshape(1, f2), conv_b.reshape(1, f2))


def ffn_act_bwd(da, p, conv_w, conv_b, *, name):
    s, f2 = p.shape
    f = f2 // 2
    nj = f // FFN_TN
    nch = s // FFN_CH

    def body(da_ref, pg_ref, pu_ref, wg_ref, wu_ref, bg_ref, bu_ref,
             dpg_ref, dpu_ref, dwg_ref, dwu_ref, dbg_ref, dbu_ref, dug_s, duu_s):
        def conv(ref, w_ref, b_ref, r0, first):
            cur, sh1, sh2 = _rows_before(ref, r0, first)
            u = ((b_ref[...] + w_ref[0:1, :] * sh2) + w_ref[1:2, :] * sh1) + w_ref[2:3, :] * cur
            return u, (sh2, sh1, cur)

        def taps_sum(du, taps):
            return jnp.concatenate([jnp.sum(du * t, axis=0, keepdims=True) for t in taps], axis=0)

        def chunk(r0, first, acc):
            dwg, dwu, dbg, dbu = acc
            gate, tg = conv(pg_ref, wg_ref, bg_ref, r0, first)
            up, tu = conv(pu_ref, wu_ref, bu_ref, r0, first)
            dav = da_ref[pl.ds(r0, FFN_CH), :]
            sg = _sigmoid(gate)
            dgate = dav * up * (sg * (1.0 + gate * (1.0 - sg)))
            dup = dav * (gate * sg)
            dug_s[pl.ds(r0, FFN_CH), :] = dgate
            duu_s[pl.ds(r0, FFN_CH), :] = dup
            return (dwg + taps_sum(dgate, tg), dwu + taps_sum(dup, tu),
                    dbg + jnp.sum(dgate, axis=0, keepdims=True), dbu + jnp.sum(dup, axis=0, keepdims=True))

        z3 = jnp.zeros((3, FFN_TN), F32)
        z1 = jnp.zeros((1, FFN_TN), F32)
        acc = chunk(0, True, (z3, z3, z1, z1))
        acc = lax.fori_loop(1, nch, lambda c, a: chunk(pl.multiple_of(c * FFN_CH, FFN_CH), False, a), acc)
        dwg_ref[...], dwu_ref[...], dbg_ref[...], dbu_ref[...] = acc

        def back(src, w_ref, dst, r0, last):
            cur, up1, up2 = _rows_after(src, r0, last)
            dst[pl.ds(r0, FFN_CH), :] = (w_ref[2:3, :] * cur + w_ref[1:2, :] * up1 + w_ref[0:1, :] * up2).astype(BF16)

        def step(c, carry):
            r0 = pl.multiple_of(c * FFN_CH, FFN_CH)
            back(dug_s, wg_ref, dpg_ref, r0, False)
            back(duu_s, wu_ref, dpu_ref, r0, False)
            return carry

        lax.fori_loop(0, nch - 1, step, 0)
        back(dug_s, wg_ref, dpg_ref, (nch - 1) * FFN_CH, True)
        back(duu_s, wu_ref, dpu_ref, (nch - 1) * FFN_CH, True)

    col = lambda off: pl.BlockSpec((s, FFN_TN), lambda j: (0, j + off))
    wcol = lambda off: pl.BlockSpec((3, FFN_TN), lambda j: (0, j + off))
    bcol = lambda off: pl.BlockSpec((1, FFN_TN), lambda j: (0, j + off))
    outs = pl.pallas_call(
        body, name=name, grid=(nj,),
        in_specs=[col(0), col(0), col(nj), wcol(0), wcol(nj), bcol(0), bcol(nj)],
        out_specs=[col(0), col(0), wcol(0), wcol(0), bcol(0), bcol(0)],
        out_shape=[jax.ShapeDtypeStruct((s, f), BF16), jax.ShapeDtypeStruct((s, f), BF16),
                   jax.ShapeDtypeStruct((3, f), F32), jax.ShapeDtypeStruct((3, f), F32),
                   jax.ShapeDtypeStruct((1, f), F32), jax.ShapeDtypeStruct((1, f), F32)],
        scratch_shapes=[pltpu.VMEM((s, FFN_TN), F32), pltpu.VMEM((s, FFN_TN), F32)],
        compiler_params=_params(("parallel",), VMEM_LIMIT),
    )(da, p, p, conv_w, conv_w, conv_b.reshape(1, f2), conv_b.reshape(1, f2))
    dpg, dpu, dwg, dwu, dbg, dbu = outs
    return (jnp.concatenate([dpg, dpu], axis=1), jnp.concatenate([dwg, dwu], axis=1),
            jnp.concatenate([dbg, dbu], axis=1)[0])


def mixnorm_fwd(outs, gain, *, name):
    s = outs[0].shape[0]
    widths = [o.shape[1] for o in outs]
    total = sum(widths)
    tm = 512

    def body(*refs):
        o_refs, g_ref, m_ref = refs[:-2], refs[-2], refs[-1]
        off = 0
        for o_ref, w in zip(o_refs, widths):
            xv = o_ref[...]
            r = lax.rsqrt(jnp.mean(xv * xv, axis=-1, keepdims=True) + EPS)
            m_ref[:, off:off + w] = (xv * r * g_ref[:, off:off + w]).astype(BF16)
            off += w

    return pl.pallas_call(
        body, name=name, grid=(s // tm,),
        in_specs=[pl.BlockSpec((tm, w), lambda i: (i, 0)) for w in widths] + [pl.BlockSpec((1, total), lambda i: (0, 0))],
        out_specs=pl.BlockSpec((tm, total), lambda i: (i, 0)),
        out_shape=jax.ShapeDtypeStruct((s, total), BF16),
        compiler_params=_params(("parallel",)),
    )(*outs, gain.reshape(1, total))


def mixnorm_bwd(dmix, outs, gain, *, name):
    s = outs[0].shape[0]
    widths = [o.shape[1] for o in outs]
    total = sum(widths)
    n = len(outs)
    tm = 256

    def body(*refs):
        dm_ref, o_refs, g_ref = refs[0], refs[1:1 + n], refs[1 + n]
        d_refs, dg_ref = refs[2 + n:2 + 2 * n], refs[2 + 2 * n]

        @pl.when(pl.program_id(0) == 0)
        def _():
            dg_ref[...] = jnp.zeros_like(dg_ref)

        off = 0
        for o_ref, d_ref, w in zip(o_refs, d_refs, widths):
            xv = o_ref[...]
            dhv = dm_ref[:, off:off + w]
            r = lax.rsqrt(jnp.mean(xv * xv, axis=-1, keepdims=True) + EPS)
            gd = dhv * g_ref[:, off:off + w]
            dot = jnp.mean(gd * xv, axis=-1, keepdims=True)
            d_ref[...] = r * gd - xv * (r * r * r * dot)
            dg_ref[:, off:off + w] += jnp.sum(dhv * (xv * r), axis=0, keepdims=True)
            off += w

    res = pl.pallas_call(
        body, name=name, grid=(s // tm,),
        in_specs=[pl.BlockSpec((tm, total), lambda i: (i, 0))]
        + [pl.BlockSpec((tm, w), lambda i: (i, 0)) for w in widths] + [pl.BlockSpec((1, total), lambda i: (0, 0))],
        out_specs=[pl.BlockSpec((tm, w), lambda i: (i, 0)) for w in widths] + [pl.BlockSpec((1, total), lambda i: (0, 0))],
        out_shape=[jax.ShapeDtypeStruct((s, w), F32) for w in widths] + [jax.ShapeDtypeStruct((1, total), F32)],
        compiler_params=_params(("arbitrary",)),
    )(dmix, *outs, gain.reshape(1, total))
    return res[:n], res[n][0]


NORM_CH = 512
FWD_TILES = 4
BWD_TILES = 2


def _lo_mask(shape):
    return lax.broadcasted_iota(I32, shape, 1) < HEAD_DIM


def _head_sum(x, lo):
    del lo
    i = lax.broadcasted_iota(I32, (LANES, LANES), 0) // HEAD_DIM
    j = lax.broadcasted_iota(I32, (LANES, LANES), 1) // HEAD_DIM
    return _split_dot(x, (i == j).astype(BF16))


def _head_stats(x, lo):
    return lax.rsqrt(_head_sum(x * x, lo) * (1.0 / HEAD_DIM) + EPS)


def _swap_halves(x):
    return pltpu.roll(x, HEAD_DIM, axis=1)


def _replicate_head(x, lo, use_lo_head):
    sw = _swap_halves(x)
    return jnp.where(use_lo_head, jnp.where(lo, x, sw), jnp.where(lo, sw, x))


def _tile_rows(i, s, d):
    nb = s // (BLOCK * d)
    r = i // nb
    b = i % nb
    start = r + (BLOCK * d) * b
    prev = start - (BLOCK * d) * jnp.minimum(b, 1)
    return start, prev, b > 0


def _rows(ref, start, d):
    if d == 1:
        return ref[pl.ds(pl.multiple_of(start, BLOCK), BLOCK), :]
    return ref[pl.ds(start, BLOCK, stride=d), :]


def _set_rows(ref, start, d, val):
    if d == 1:
        ref[pl.ds(pl.multiple_of(start, BLOCK), BLOCK), :] = val
    else:
        ref[pl.ds(start, BLOCK, stride=d), :] = val


def banded_fwd(proj, qb0, kb0, vb0, n_slabs, gq, gk, bias, dils, sinks, gqa, *, name):
    s = proj.shape[0]
    nbr = len(dils)
    nt = s // BLOCK
    nch = s // NORM_CH
    has_sink = sinks is not None

    def body(*refs):
        q_ref, k_ref, v_ref, gq_ref, gk_ref, b_ref = refs[:6]
        rest = refs[6:]
        if has_sink:
            sink_ref, rest = rest[0], rest[1:]
        out_ref, lse_ref, qn_s, kn_s, vv_s, o_s, l_s = rest
        p = pl.program_id(0)
        use_lo = (p // 2) == 0

        def prep(c, carry):
            rows = pl.ds(pl.multiple_of(c * NORM_CH, NORM_CH), NORM_CH)
            lo = _lo_mask((NORM_CH, LANES))
            qv, kv, vv = q_ref[rows, :], k_ref[rows, :], v_ref[rows, :]
            qn_s[rows, :] = qv * _head_stats(qv, lo) * gq_ref[...] * (HEAD_DIM ** -0.5)
            kn = kv * _head_stats(kv, lo) * gk_ref[...]
            if gqa:
                kn = _replicate_head(kn, lo, use_lo)
                vv = _replicate_head(vv, lo, use_lo)
            kn_s[rows, :] = kn
            vv_s[rows, :] = vv
            return carry

        lax.fori_loop(0, nch, prep, 0)

        lo = _lo_mask((BLOCK, LANES))
        hms = [lo, jnp.logical_not(lo)]
        heads, tiles = range(2), range(FWD_TILES)
        for br, d in enumerate(dils):
            def step(ii, carry, br=br, d=d):
                pos = [_tile_rows(ii * FWD_TILES + u, s, d) for u in tiles]
                kcat = [jnp.concatenate([_rows(kn_s, pos[u][1], d), _rows(kn_s, pos[u][0], d)], axis=0).astype(BF16)
                        for u in tiles]
                vcat = [jnp.concatenate([_rows(vv_s, pos[u][1], d), _rows(vv_s, pos[u][0], d)], axis=0).astype(BF16)
                        for u in tiles]
                qt = [_rows(qn_s, pos[u][0], d) for u in tiles]
                sc = [[_dot_nt(jnp.where(hms[h], qt[u], 0.0).astype(BF16), kcat[u])
                       + b_ref[br, jnp.where(pos[u][2], 0, 1), h] for h in heads] for u in tiles]
                m = [[jnp.max(sc[u][h], axis=1, keepdims=True) for h in heads] for u in tiles]
                pe = [[jnp.exp(sc[u][h] - m[u][h]) for h in heads] for u in tiles]
                den = [[jnp.sum(pe[u][h], axis=1, keepdims=True) for h in heads] for u in tiles]
                o = [[_dot(pe[u][h].astype(BF16), vcat[u]) * (1.0 / den[u][h]) for h in heads] for u in tiles]
                for u in tiles:
                    _set_rows(o_s.at[br], pos[u][0], d, jnp.where(lo, o[u][0], o[u][1]))
                    _set_rows(l_s.at[br], pos[u][0], d,
                              jnp.where(lo, m[u][0] + jnp.log(den[u][0]), m[u][1] + jnp.log(den[u][1])))
                return carry

            lax.fori_loop(0, nt // FWD_TILES, step, 0)

        def combine(c, carry):
            rows = pl.ds(pl.multiple_of(c * NORM_CH, NORM_CH), NORM_CH)
            ls = [l_s[br, rows, :] for br in range(nbr)]
            mx = functools.reduce(jnp.maximum, ls)
            if has_sink:
                mx = jnp.maximum(mx, sink_ref[...])
            tot = functools.reduce(jnp.add, [jnp.exp(l - mx) for l in ls])
            if has_sink:
                tot = tot + jnp.exp(sink_ref[...] - mx)
            lse = mx + jnp.log(tot)
            acc = jnp.exp(ls[0] - lse) * o_s[0, rows, :]
            for br in range(1, nbr):
                acc = acc + jnp.exp(ls[br] - lse) * o_s[br, rows, :]
            out_ref[rows, :] = acc
            lse_ref[rows, :] = lse
            return carry

        lax.fori_loop(0, nch, combine, 0)

    slab = lambda b0, shared: pl.BlockSpec((s, LANES), (lambda p: (0, b0)) if shared else (lambda p: (0, b0 + p)),
                                           pipeline_mode=pl.Buffered(1))
    vec = pl.BlockSpec((1, LANES), lambda p: (0, 0))
    in_specs = [slab(qb0, False), slab(kb0, gqa), slab(vb0, gqa), vec, vec,
                pl.BlockSpec((nbr, 2, 2, BLOCK, 2 * BLOCK), lambda p: (0, 0, p, 0, 0))]
    args = [proj, proj, proj, gq.reshape(1, LANES), gk.reshape(1, LANES), bias]
    if has_sink:
        in_specs.append(pl.BlockSpec((None, 1, LANES), lambda p: (p, 0, 0)))
        args.append(sinks)
    w = LANES * n_slabs
    return pl.pallas_call(
        body, name=name, grid=(n_slabs,),
        in_specs=in_specs,
        out_specs=[pl.BlockSpec((s, LANES), lambda p: (0, p)), pl.BlockSpec((s, LANES), lambda p: (0, p))],
        out_shape=[jax.ShapeDtypeStruct((s, w), F32), jax.ShapeDtypeStruct((s, w), F32)],
        scratch_shapes=[pltpu.VMEM((s, LANES), F32), pltpu.VMEM((s, LANES), F32), pltpu.VMEM((s, LANES), F32),
                        pltpu.VMEM((nbr, s, LANES), F32), pltpu.VMEM((nbr, s, LANES), F32)],
        compiler_params=_params(("parallel",), VMEM_LIMIT),
    )(*args)


def banded_bwd(proj, qb0, kb0, vb0, n_slabs, gq, gk, bias, dils, sinks, gqa, dout, out, lse, *, name):
    s = proj.shape[0]
    nbr = len(dils)
    nt = s // BLOCK
    nch = s // NORM_CH
    has_sink = sinks is not None
    scale = HEAD_DIM ** -0.5

    def body(*refs):
        q_ref, k_ref, v_ref, gq_ref, gk_ref, b_ref, do_ref, o_ref, lse_ref = refs[:9]
        rest = refs[9:]
        if has_sink:
            sink_ref, rest = rest[0], rest[1:]
        dq_ref, dk_ref, dv_ref, db_ref, dgq_ref, dgk_ref = rest[:6]
        rest = rest[6:]
        if has_sink:
            dsink_ref, rest = rest[0], rest[1:]
        qn_s, kn_s, vv_s, dl_s, dqn_s, dkn_s, dvv_s = rest
        p = pl.program_id(0)
        use_lo = (p // 2) == 0

        def prep(c, carry):
            rows = pl.ds(pl.multiple_of(c * NORM_CH, NORM_CH), NORM_CH)
            lo = _lo_mask((NORM_CH, LANES))
            qv, kv, vv = q_ref[rows, :], k_ref[rows, :], v_ref[rows, :]
            qn_s[rows, :] = qv * _head_stats(qv, lo) * gq_ref[...] * scale
            kn = kv * _head_stats(kv, lo) * gk_ref[...]
            if gqa:
                kn = _replicate_head(kn, lo, use_lo)
                vv = _replicate_head(vv, lo, use_lo)
            kn_s[rows, :] = kn
            vv_s[rows, :] = vv
            delta = _head_sum(do_ref[rows, :] * o_ref[rows, :], lo)
            dl_s[rows, :] = delta
            z = jnp.zeros((NORM_CH, LANES), F32)
            dqn_s[rows, :] = z
            dkn_s[rows, :] = z
            dvv_s[rows, :] = z
            if has_sink:
                ps = jnp.exp(sink_ref[...] - lse_ref[rows, :])
                return carry - jnp.sum(ps * delta, axis=0, keepdims=True)
            return carry

        dsink = lax.fori_loop(0, nch, prep, jnp.zeros((1, LANES), F32))
        if has_sink:
            dsink_ref[...] = jnp.broadcast_to(dsink, (8, LANES))

        lo = _lo_mask((BLOCK, LANES))
        hms = [lo, jnp.logical_not(lo)]
        heads, tiles = range(2), range(BWD_TILES)
        for br, d in enumerate(dils):
            db_ref[br] = jnp.zeros((2, BLOCK, 2 * BLOCK), F32)

            def step(ii, carry, br=br, d=d):
                pos = [_tile_rows(ii * BWD_TILES + u, s, d) for u in tiles]
                kcat = [jnp.concatenate([_rows(kn_s, pos[u][1], d), _rows(kn_s, pos[u][0], d)], axis=0).astype(BF16)
                        for u in tiles]
                vcat = [jnp.concatenate([_rows(vv_s, pos[u][1], d), _rows(vv_s, pos[u][0], d)], axis=0).astype(BF16)
                        for u in tiles]
                qt = [_rows(qn_s, pos[u][0], d) for u in tiles]
                dot_ = [_rows(do_ref, pos[u][0], d) for u in tiles]
                lse_t = [_rows(lse_ref, pos[u][0], d) for u in tiles]
                dl_t = [_rows(dl_s, pos[u][0], d) for u in tiles]
                qh = [[jnp.where(hms[h], qt[u], 0.0).astype(BF16) for h in heads] for u in tiles]
                doh = [[jnp.where(hms[h], dot_[u], 0.0).astype(BF16) for h in heads] for u in tiles]
                sc = [[_dot_nt(qh[u][h], kcat[u]) + b_ref[br, jnp.where(pos[u][2], 0, 1), h] for h in heads]
                      for u in tiles]
                dp = [[_dot_nt(doh[u][h], vcat[u]) for h in heads] for u in tiles]
                lane0 = [0, HEAD_DIM]
                pr = [[jnp.exp(sc[u][h] - lse_t[u][:, lane0[h]:lane0[h] + 1]) for h in heads] for u in tiles]
                dlog = [[pr[u][h] * (dp[u][h] - dl_t[u][:, lane0[h]:lane0[h] + 1]) for h in heads] for u in tiles]
                for h in heads:
                    db_ref[br, h] += functools.reduce(jnp.add, [dlog[u][h] for u in tiles])
                dlb = [[dlog[u][h].astype(BF16) for h in heads] for u in tiles]
                prb = [[pr[u][h].astype(BF16) for h in heads] for u in tiles]
                dq_t = [jnp.where(lo, _dot(dlb[u][0], kcat[u]), _dot(dlb[u][1], kcat[u])) * scale for u in tiles]
                dk_t = [_dot_tn(dlb[u][0], qh[u][0]) + _dot_tn(dlb[u][1], qh[u][1]) for u in tiles]
                dv_t = [_dot_tn(prb[u][0], doh[u][0]) + _dot_tn(prb[u][1], doh[u][1]) for u in tiles]
                for u in tiles:
                    start, prev = pos[u][0], pos[u][1]
                    _set_rows(dqn_s, start, d, _rows(dqn_s, start, d) + dq_t[u])
                    _set_rows(dkn_s, prev, d, _rows(dkn_s, prev, d) + dk_t[u][:BLOCK])
                    _set_rows(dkn_s, start, d, _rows(dkn_s, start, d) + dk_t[u][BLOCK:])
                    _set_rows(dvv_s, prev, d, _rows(dvv_s, prev, d) + dv_t[u][:BLOCK])
                    _set_rows(dvv_s, start, d, _rows(dvv_s, start, d) + dv_t[u][BLOCK:])
                return carry

            lax.fori_loop(0, nt // BWD_TILES, step, 0)

        if gqa:
            @pl.when(p == 0)
            def _():
                dk_ref[...] = jnp.zeros_like(dk_ref)
                dv_ref[...] = jnp.zeros_like(dv_ref)

        def finish(c, carry):
            dgq, dgk = carry
            rows = pl.ds(pl.multiple_of(c * NORM_CH, NORM_CH), NORM_CH)
            lo = _lo_mask((NORM_CH, LANES))

            def norm_bwd(xv, dn, g_ref):
                r = _head_stats(xv, lo)
                gd = dn * g_ref[...]
                dot = _head_sum(gd * xv, lo) * (1.0 / HEAD_DIM)
                return r * gd - xv * (r * r * r * dot), dn * (xv * r)

            dq, gq_part = norm_bwd(q_ref[rows, :], dqn_s[rows, :], gq_ref)
            dq_ref[rows, :] = dq
            dgq = dgq + jnp.sum(gq_part, axis=0, keepdims=True)
            kv, dkn, dvv = k_ref[rows, :], dkn_s[rows, :], dvv_s[rows, :]
            if gqa:
                kv = _replicate_head(kv, lo, use_lo)
                dkn = dkn + _swap_halves(dkn)
                dvv = dvv + _swap_halves(dvv)
                lane = lax.broadcasted_iota(I32, (NORM_CH, LANES), 1)
                mine = (lane // HEAD_DIM) == (p // 2)
                dk, gk_part = norm_bwd(kv, dkn, gk_ref)
                dk_ref[rows, :] += jnp.where(mine, dk, 0.0)
                dv_ref[rows, :] += jnp.where(mine, dvv, 0.0)
                gk_part = jnp.where(lo, gk_part, 0.0)
            else:
                dk, gk_part = norm_bwd(kv, dkn, gk_ref)
                dk_ref[rows, :] = dk
                dv_ref[rows, :] = dvv
            dgk = dgk + jnp.sum(gk_part, axis=0, keepdims=True)
            return dgq, dgk

        z = jnp.zeros((1, LANES), F32)
        dgq, dgk = lax.fori_loop(0, nch, finish, (z, z))
        dgq_ref[...] = jnp.broadcast_to(dgq, (8, LANES))
        dgk_ref[...] = jnp.broadcast_to(dgk, (8, LANES))

    def slab_of(width_blocks, b0, shared):
        return pl.BlockSpec((s, LANES), (lambda p: (0, b0)) if shared else (lambda p: (0, b0 + p)),
                            pipeline_mode=pl.Buffered(1))

    vec = pl.BlockSpec((1, LANES), lambda p: (0, 0))
    own = pl.BlockSpec((s, LANES), lambda p: (0, p), pipeline_mode=pl.Buffered(1))
    in_specs = [slab_of(0, qb0, False), slab_of(0, kb0, gqa), slab_of(0, vb0, gqa), vec, vec,
                pl.BlockSpec((nbr, 2, 2, BLOCK, 2 * BLOCK), lambda p: (0, 0, p, 0, 0)), own, own, own]
    args = [proj, proj, proj, gq.reshape(1, LANES), gk.reshape(1, LANES), bias, dout, out, lse]
    if has_sink:
        in_specs.append(pl.BlockSpec((None, 1, LANES), lambda p: (p, 0, 0)))
        args.append(sinks)
    w = LANES * n_slabs
    kvw = LANES if gqa else w
    kv_spec = pl.BlockSpec((s, LANES), (lambda p: (0, 0)) if gqa else (lambda p: (0, p)))
    part = pl.BlockSpec((None, 8, LANES), lambda p: (p, 0, 0))
    out_specs = [pl.BlockSpec((s, LANES), lambda p: (0, p)), kv_spec, kv_spec,
                 pl.BlockSpec((nbr, 2, BLOCK, 2 * BLOCK), lambda p: (0, p, 0, 0)), part, part]
    out_shape = [jax.ShapeDtypeStruct((s, w), F32), jax.ShapeDtypeStruct((s, kvw), F32),
                 jax.ShapeDtypeStruct((s, kvw), F32),
                 jax.ShapeDtypeStruct((nbr, 2 * n_slabs, BLOCK, 2 * BLOCK), F32),
                 jax.ShapeDtypeStruct((n_slabs, 8, LANES), F32), jax.ShapeDtypeStruct((n_slabs, 8, LANES), F32)]
    if has_sink:
        out_specs.append(part)
        out_shape.append(jax.ShapeDtypeStruct((n_slabs, 8, LANES), F32))
    res = pl.pallas_call(
        body, name=name, grid=(n_slabs,),
        in_specs=in_specs, out_specs=out_specs, out_shape=out_shape,
        scratch_shapes=[pltpu.VMEM((s, LANES), F32) for _ in range(7)],
        compiler_params=_params(("arbitrary",), VMEM_LIMIT),
    )(*args)
    dq, dk, dv, db, dgq, dgk = res[:6]
    outs = [dq, dk, dv, db, dgq[:, 0, :], dgk[:, 0, :]]
    if has_sink:
        outs.append(res[6][:, 0, :])
    return outs


def bias_bwd(dbias, buckets, *, name):
    nbr, h = dbias.shape[:2]

    def body(db_ref, bk_ref, o_ref):
        lane = lax.broadcasted_iota(I32, (1, LANES), 1)
        acc = jnp.zeros((1, LANES), F32)
        for b in range(N_BUCKETS):
            tot = jnp.zeros((1, 1), F32)
            for br in range(nbr):
                sel = jnp.where(bk_ref[br] == b, db_ref[br], 0.0)
                tot = tot + jnp.sum(jnp.sum(sel, axis=0, keepdims=True), axis=1, keepdims=True)
            acc = jnp.where(lane == b, tot, acc)
        o_ref[...] = jnp.broadcast_to(acc, (8, LANES))

    res = pl.pallas_call(
        body, name=name, grid=(h,),
        in_specs=[pl.BlockSpec((nbr, None, BLOCK, 2 * BLOCK), lambda i: (0, i, 0, 0)),
                  pl.BlockSpec((nbr, BLOCK, 2 * BLOCK), lambda i: (0, 0, 0))],
        out_specs=pl.BlockSpec((None, 8, LANES), lambda i: (i, 0, 0)),
        out_shape=jax.ShapeDtypeStruct((h, 8, LANES), F32),
        compiler_params=_params(("parallel",)),
    )(dbias, buckets)
    return res[:, 0, :N_BUCKETS].T


SB_KG = 512
SB_QT = 2


def _softplus(z):
    return jnp.maximum(z, 0.0) + jnp.log(1.0 + jnp.exp(-jnp.abs(z)))


def _split_dot(x, t):
    hi = x.astype(BF16)
    lo = (x - hi.astype(F32)).astype(BF16)
    return _dot(hi, t) + _dot(lo, t)


def sb_fwd(proj, qb0, kb0, vb0, n_slabs, *, name):
    s = proj.shape[0]
    nq = s // BLOCK
    nch = s // NORM_CH
    scale = HEAD_DIM ** -0.5

    def body(q_ref, k_ref, v_ref, o_ref, tot_ref, qlo_s, qhi_s, k_s, v_s):
        def prep(c, carry):
            rows = pl.ds(pl.multiple_of(c * NORM_CH, NORM_CH), NORM_CH)
            lo = _lo_mask((NORM_CH, LANES))
            qv = q_ref[rows, :] * scale
            qlo_s[rows, :] = jnp.where(lo, qv, 0.0).astype(BF16)
            qhi_s[rows, :] = jnp.where(lo, 0.0, qv).astype(BF16)
            k_s[rows, :] = k_ref[rows, :].astype(BF16)
            v_s[rows, :] = v_ref[rows, :].astype(BF16)
            return carry

        lax.fori_loop(0, nch, prep, 0)

        row = lax.broadcasted_iota(I32, (BLOCK, BLOCK), 0)
        col = lax.broadcasted_iota(I32, (BLOCK, BLOCK), 1)
        lo = col < HEAD_DIM
        t_ge = (row >= col).astype(BF16)
        rowg = lax.broadcasted_iota(I32, (BLOCK, SB_KG), 0)
        colg = lax.broadcasted_iota(I32, (BLOCK, SB_KG), 1)

        nsub = SB_KG // BLOCK
        chains = range(2 * SB_QT)
        nc = len(chains)

        def qloop(qs, carry):
            q0 = pl.multiple_of(qs * (SB_QT * BLOCK), SB_QT * BLOCK)
            qh = [(qlo_s, qhi_s)[i % 2][pl.ds(q0 + (i // 2) * BLOCK, BLOCK), :] for i in chains]
            gd = (qs * SB_QT) // nsub

            def logits(gi):
                k0 = pl.multiple_of(gi * SB_KG, SB_KG)
                kg = k_s[pl.ds(k0, SB_KG), :]
                return [_dot_nt(qh[i], kg) for i in chains]

            def group(gi, st, masks):
                k0 = pl.multiple_of(gi * SB_KG, SB_KG)
                vg = v_s[pl.ds(k0, SB_KG), :]
                c, o, z = list(st[:nc]), st[nc:2 * nc], st[2 * nc:]
                z_next = logits(jnp.maximum(gi - 1, 0))
                sp = [_softplus(z[i]) for i in chains]
                lrem = [-sp[i] if masks is None else jnp.where(masks[i // 2], -sp[i], 0.0) for i in chains]
                piece = lambda x, j: x[:, j * BLOCK:(j + 1) * BLOCK]
                incl = [[_split_dot(piece(lrem[i], j), t_ge) for j in range(nsub)] for i in chains]
                a = []
                for i in chains:
                    suffix = [None] * nsub
                    for j in reversed(range(nsub)):
                        suffix[j] = c[i] + incl[i][j] - piece(lrem[i], j)
                        c[i] = c[i] + incl[i][j][:, 0:1]
                    ai = jnp.exp(z[i] - sp[i] + jnp.concatenate(suffix, axis=1))
                    a.append((ai if masks is None else jnp.where(masks[i // 2], ai, 0.0)).astype(BF16))
                o = [o[i] + _dot(a[i], vg) for i in chains]
                return (*c, *o, *z_next)

            zc = [jnp.zeros((BLOCK, 1), F32)] * nc
            zo = [jnp.zeros((BLOCK, LANES), F32)] * nc
            masks = [(gd * SB_KG + colg) < (q0 + t * BLOCK + rowg) for t in range(SB_QT)]
            st = group(gd, (*zc, *zo, *logits(gd)), masks)
            st = lax.fori_loop(0, gd, lambda t, st: group(gd - 1 - t, st, None), st)
            for t in range(SB_QT):
                rows = pl.ds(q0 + t * BLOCK, BLOCK)
                o_ref[rows, :] = jnp.where(lo, st[nc + 2 * t], st[nc + 2 * t + 1])
                tot_ref[rows, :] = jnp.where(lo, st[2 * t], st[2 * t + 1])
            return carry

        lax.fori_loop(0, nq // SB_QT, qloop, 0)

    slab = lambda b0: pl.BlockSpec((s, LANES), lambda p: (0, b0 + p), pipeline_mode=pl.Buffered(1))
    w = LANES * n_slabs
    return pl.pallas_call(
        body, name=name, grid=(n_slabs,),
        in_specs=[slab(qb0), slab(kb0), slab(vb0)],
        out_specs=[pl.BlockSpec((s, LANES), lambda p: (0, p)), pl.BlockSpec((s, LANES), lambda p: (0, p))],
        out_shape=[jax.ShapeDtypeStruct((s, w), F32), jax.ShapeDtypeStruct((s, w), F32)],
        scratch_shapes=[pltpu.VMEM((s, LANES), BF16) for _ in range(4)],
        compiler_params=_params(("parallel",), VMEM_LIMIT),
    )(proj, proj, proj)


def sb_bwd(proj, qb0, kb0, vb0, n_slabs, dout, tot, *, name):
    s = proj.shape[0]
    nq = s // BLOCK
    nch = s // NORM_CH
    nsub = SB_KG // BLOCK
    scale = HEAD_DIM ** -0.5

    def body(q_ref, k_ref, v_ref, do_ref, tot_ref, dq_ref, dk_ref, dv_ref,
             qlo_s, qhi_s, k_s, v_s, dlo_s, dhi_s):
        def prep(c, carry):
            rows = pl.ds(pl.multiple_of(c * NORM_CH, NORM_CH), NORM_CH)
            lo = _lo_mask((NORM_CH, LANES))
            qv = q_ref[rows, :] * scale
            dv = do_ref[rows, :]
            qlo_s[rows, :] = jnp.where(lo, qv, 0.0).astype(BF16)
            qhi_s[rows, :] = jnp.where(lo, 0.0, qv).astype(BF16)
            dlo_s[rows, :] = jnp.where(lo, dv, 0.0).astype(BF16)
            dhi_s[rows, :] = jnp.where(lo, 0.0, dv).astype(BF16)
            k_s[rows, :] = k_ref[rows, :].astype(BF16)
            v_s[rows, :] = v_ref[rows, :].astype(BF16)
            z = jnp.zeros((NORM_CH, LANES), F32)
            dk_ref[rows, :] = z
            dv_ref[rows, :] = z
            return carry

        lax.fori_loop(0, nch, prep, 0)

        row = lax.broadcasted_iota(I32, (BLOCK, BLOCK), 0)
        col = lax.broadcasted_iota(I32, (BLOCK, BLOCK), 1)
        lo = col < HEAD_DIM
        t_le = (row <= col).astype(BF16)
        rowg = lax.broadcasted_iota(I32, (BLOCK, SB_KG), 0)
        colg = lax.broadcasted_iota(I32, (BLOCK, SB_KG), 1)

        piece = lambda x, j: x[:, j * BLOCK:(j + 1) * BLOCK]
        chains = range(2 * SB_QT)
        nc = len(chains)

        def prefixes(x):
            return [[_split_dot(piece(x[i], j), t_le) for j in range(nsub)] for i in chains]

        def chain(pre, run, total=None):
            out = []
            for j in range(nsub):
                out.append(run + pre[j] if total is None else total - run - pre[j])
                run = run + pre[j][:, BLOCK - 1:BLOCK]
            return jnp.concatenate(out, axis=1), run

        def qloop(qs, carry):
            q0 = pl.multiple_of(qs * (SB_QT * BLOCK), SB_QT * BLOCK)
            tile = lambda ref, i: ref[pl.ds(q0 + (i // 2) * BLOCK, BLOCK), :]
            qh = [tile((qlo_s, qhi_s)[i % 2], i) for i in chains]
            doh = [tile((dlo_s, dhi_s)[i % 2], i) for i in chains]
            tots = [tile(tot_ref, i)[:, (i % 2) * HEAD_DIM:(i % 2) * HEAD_DIM + 1] for i in chains]
            gd = (qs * SB_QT) // nsub

            def logits(gi):
                kg = k_s[pl.ds(pl.multiple_of(gi * SB_KG, SB_KG), SB_KG), :]
                return [_dot_nt(qh[i], kg) for i in chains]

            def group(gi, st, masks):
                k0 = pl.multiple_of(gi * SB_KG, SB_KG)
                kg, vg = k_s[pl.ds(k0, SB_KG), :], v_s[pl.ds(k0, SB_KG), :]
                cp, cg, dq, z = list(st[:nc]), list(st[nc:2 * nc]), st[2 * nc:2 * nc + SB_QT], st[2 * nc + SB_QT:]
                masked = lambda x, i: x if masks is None else jnp.where(masks[i // 2], x, 0.0)
                z_next = logits(jnp.minimum(gi + 1, gd))
                da = [_dot_nt(doh[i], vg) for i in chains]
                sp = [_softplus(z[i]) for i in chains]
                lrem = [masked(-sp[i], i) for i in chains]
                pre = prefixes(lrem)
                e, a, g = [], [], []
                for i in chains:
                    suffix, cp[i] = chain(pre[i], cp[i], tots[i])
                    e.append(z[i] - sp[i])
                    a.append(masked(jnp.exp(e[i] + suffix), i))
                    g.append(a[i] * da[i])
                gpre = prefixes(g)
                dz = []
                for i in chains:
                    ginc, cg[i] = chain(gpre[i], cg[i])
                    dz.append(masked(g[i] - jnp.exp(e[i]) * ginc, i).astype(BF16))
                ab = [a[i].astype(BF16) for i in chains]
                dq = [dq[t] + jnp.where(lo, _dot(dz[2 * t], kg), _dot(dz[2 * t + 1], kg)) for t in range(SB_QT)]
                dk_ref[pl.ds(k0, SB_KG), :] += functools.reduce(jnp.add, [_dot_tn(dz[i], qh[i]) for i in chains])
                dv_ref[pl.ds(k0, SB_KG), :] += functools.reduce(jnp.add, [_dot_tn(ab[i], doh[i]) for i in chains])
                return (*cp, *cg, *dq, *z_next)

            zc = [jnp.zeros((BLOCK, 1), F32)] * (2 * nc)
            zq = [jnp.zeros((BLOCK, LANES), F32)] * SB_QT
            st = lax.fori_loop(0, gd, lambda gi, st: group(gi, st, None), (*zc, *zq, *logits(0)))
            st = group(gd, st, [(gd * SB_KG + colg) < (q0 + t * BLOCK + rowg) for t in range(SB_QT)])
            for t in range(SB_QT):
                dq_ref[pl.ds(q0 + t * BLOCK, BLOCK), :] = st[2 * nc + t] * scale
            return carry

        lax.fori_loop(0, nq // SB_QT, qloop, 0)

    slab = lambda b0: pl.BlockSpec((s, LANES), lambda p: (0, b0 + p), pipeline_mode=pl.Buffered(1))
    own = pl.BlockSpec((s, LANES), lambda p: (0, p), pipeline_mode=pl.Buffered(1))
    w = LANES * n_slabs
    outb = pl.BlockSpec((s, LANES), lambda p: (0, p))
    return pl.pallas_call(
        body, name=name, grid=(n_slabs,),
        in_specs=[slab(qb0), slab(kb0), slab(vb0), own, own],
        out_specs=[outb, outb, outb],
        out_shape=[jax.ShapeDtypeStruct((s, w), F32)] * 3,
        scratch_shapes=[pltpu.VMEM((s, LANES), BF16) for _ in range(6)],
        compiler_params=_params(("parallel",), VMEM_LIMIT),
    )(proj, proj, proj, dout, tot)


def _place():
    x, y, c = lax.axis_index("x"), lax.axis_index("y"), lax.axis_index("c")
    return x, y, c


def gather_small(v, *, name):
    m_per, n = v.shape

    def body(x_ref, out_ref, send_sems, recv_sems, local_sem):
        x, y, c = _place()
        me, sibling = (x, y, c), (x, y, 1 - c)
        chips = [(1 - x, y), (x, 1 - y), (1 - x, 1 - y)]

        def rows(px, py, pc):
            return out_ref.at[pl.ds((4 * px + 2 * py + pc) * m_per, m_per), :]

        def copy(k, block, to, src=None):
            return pltpu.make_async_remote_copy(
                src_ref=rows(*block) if src is None else src, dst_ref=rows(*block),
                send_sem=send_sems.at[k], recv_sem=recv_sems.at[k], device_id=to, device_id_type=MESH)

        mine = pltpu.make_async_copy(x_ref, rows(*me), local_sem)
        mine.start()
        first = [copy(0, me, sibling, src=x_ref)]
        first += [copy(1 + j, me, (*chip, c), src=x_ref) for j, chip in enumerate(chips)]
        for cp in first:
            cp.start()
        passed = [copy(4 + j, (*chip, c), sibling) for j, chip in enumerate(chips)]
        for j, chip in enumerate(chips):
            copy(1 + j, (*chip, c), me).wait_recv()
            passed[j].start()
        copy(0, sibling, me).wait_recv()
        for j, chip in enumerate(chips):
            copy(4 + j, (*chip, 1 - c), me).wait_recv()
        for cp in first + passed:
            cp.wait_send()
        mine.wait()

    return pl.pallas_call(
        body, name=name,
        out_shape=jax.ShapeDtypeStruct((N_DEV * m_per, n), v.dtype),
        in_specs=[pl.BlockSpec(memory_space=pltpu.VMEM)],
        out_specs=pl.BlockSpec(memory_space=pltpu.VMEM),
        scratch_shapes=[pltpu.SemaphoreType.DMA((7,)), pltpu.SemaphoreType.DMA((7,)), pltpu.SemaphoreType.DMA],
        compiler_params=_params(None, VMEM_LIMIT),
    )(v)


_HBM = pl.BlockSpec(memory_space=pltpu.HBM)
_SEM = pl.BlockSpec(memory_space=pltpu.SEMAPHORE)
_EFFECT = pltpu.SideEffectType.DATAFLOW_SIDE_EFFECTING


def _peer_copies(src_refs, land_refs, send_sems, recv_sems, per_dest):
    x, y, c = _place()
    me = 4 * x + 2 * y + c
    copies = []
    for src, land, ssem, rsem in zip(src_refs, land_refs, send_sems, recv_sems):
        for k in (1, 2, 4, 3, 5, 6, 7):
            px, py, pc = x ^ (k >> 2 & 1), y ^ (k >> 1 & 1), c ^ (k & 1)
            copies.append(pltpu.make_async_remote_copy(
                src_ref=src.at[4 * px + 2 * py + pc] if per_dest else src, dst_ref=land.at[me],
                send_sem=ssem.at[k - 1], recv_sem=rsem.at[k - 1], device_id=(px, py, pc), device_id_type=MESH))
    return copies


def exchange_start(srcs, lands, per_dest, *, name):
    n = len(srcs)

    def body(*refs):
        src_refs, land_refs = refs[:n], refs[n:2 * n]
        send_sems, recv_sems = refs[2 * n:3 * n], refs[3 * n:4 * n]
        token = refs[-1]
        for cp in _peer_copies(src_refs, land_refs, send_sems, recv_sems, per_dest):
            cp.start()
        token[...] = jnp.zeros_like(token)

    hbm = lambda a: pltpu.HBM(a.shape, a.dtype)
    res = pl.pallas_call(
        body, name=name,
        out_shape=(*[pltpu.SemaphoreType.DMA((7,))] * (2 * n),
                   *[hbm(a) for a in srcs], *[hbm(a) for a in lands], jax.ShapeDtypeStruct((8, LANES), F32)),
        in_specs=[_HBM] * (2 * n),
        out_specs=(*[_SEM] * (2 * n), *[_HBM] * (2 * n), pl.BlockSpec(memory_space=pltpu.VMEM)),
        input_output_aliases={i: 2 * n + i for i in range(2 * n)},
        compiler_params=pltpu.CompilerParams(has_side_effects=_EFFECT),
    )(*[pltpu.with_memory_space_constraint(a, pltpu.HBM) for a in (*srcs, *lands)])
    handles = [(res[a], res[n + a], res[2 * n + a], res[3 * n + a]) for a in range(n)]
    return handles, res[-1]


def exchange_wait(handles, per_dest, after, *, name):
    n = len(handles)

    def body(*refs):
        src_refs, land_refs = refs[:n], refs[n:2 * n]
        send_sems, recv_sems = refs[2 * n:3 * n], refs[3 * n:4 * n]
        for cp in _peer_copies(src_refs, land_refs, send_sems, recv_sems, per_dest):
            cp.wait_send()
            cp.wait_recv()

    srcs, lands = [h[2] for h in handles], [h[3] for h in handles]
    hbm = lambda a: pltpu.HBM(a.shape, a.dtype)
    res = pl.pallas_call(
        body, name=name,
        out_shape=(*[hbm(a) for a in srcs], *[hbm(a) for a in lands]),
        in_specs=[*[_HBM] * (2 * n), *[_SEM] * (2 * n), pl.BlockSpec(memory_space=pl.ANY)],
        out_specs=tuple([_HBM] * (2 * n)),
        input_output_aliases={i: i for i in range(2 * n)},
        compiler_params=pltpu.CompilerParams(has_side_effects=_EFFECT),
    )(*srcs, *lands, *[h[0] for h in handles], *[h[1] for h in handles], after)
    return res[n:]


def _adamw_math(w, g, m, v):
    m = ADAM_B1 * m + (1.0 - ADAM_B1) * g
    v = ADAM_B2 * v + (1.0 - ADAM_B2) * (g * g)
    m_hat = m / (1.0 - ADAM_B1 ** ADAM_STEP)
    v_hat = v / (1.0 - ADAM_B2 ** ADAM_STEP)
    delta = -ADAM_LR * (m_hat / (jnp.sqrt(v_hat) + ADAM_EPS) + ADAM_WD * w)
    return delta, m, v


def adamw_parts(parts, w, m, v, layer, outs, *, name):
    depth, r, cdim = w.shape
    n_parts = parts.shape[0]
    tr = _pick(r, [t for t in (512, 256, 128, 64, 32, 16) if t * cdim <= 256 * 1024])

    def body(p_ref, w_ref, m_ref, v_ref, g0, d0, nm0, nv0, g_ref, d_ref, nm_ref, nv_ref):
        g = p_ref[0].astype(F32)
        for q in range(1, n_parts):
            g = g + p_ref[q].astype(F32)
        delta, nm, nv = _adamw_math(w_ref[...], g, m_ref[...], v_ref[...])
        g_ref[...], d_ref[...], nm_ref[...], nv_ref[...] = g, delta, nm, nv

    t = pl.BlockSpec((None, tr, cdim), lambda i: (layer, i, 0))
    held = pl.BlockSpec(memory_space=pl.ANY)
    return pl.pallas_call(
        body, name=name, grid=(r // tr,),
        in_specs=[pl.BlockSpec((n_parts, tr, cdim), lambda i: (0, i, 0)), t, t, t, held, held, held, held],
        out_specs=[t, t, t, t],
        out_shape=[jax.ShapeDtypeStruct((depth, r, cdim), F32)] * 4,
        input_output_aliases={4: 0, 5: 1, 6: 2, 7: 3},
        compiler_params=_params(("parallel",), VMEM_LIMIT),
    )(parts, w, m, v, *outs)


def sum_devices(gathered, *, name):
    m_rows = gathered.shape[1]

    def body(ga_ref, g_ref):
        g = ga_ref[0]
        for dev in range(1, N_DEV):
            g = g + ga_ref[dev]
        g_ref[...] = g

    return pl.pallas_call(
        body, name=name, out_shape=jax.ShapeDtypeStruct((m_rows, LANES), F32),
        compiler_params=_params(None, VMEM_LIMIT),
    )(gathered)


def adamw_small(g, w, m, v, *, name):
    m_rows = w.shape[0]

    def body(g_ref, w_ref, m_ref, v_ref, d_ref, nm_ref, nv_ref):
        d_ref[...], nm_ref[...], nv_ref[...] = _adamw_math(w_ref[...], g_ref[...], m_ref[...], v_ref[...])

    return pl.pallas_call(
        body, name=name, out_shape=[jax.ShapeDtypeStruct((m_rows, LANES), F32)] * 3,
        compiler_params=_params(None, VMEM_LIMIT),
    )(g, w, m, v)


def _t5_bucket(dist):
    max_exact = N_BUCKETS // 2
    d = jnp.maximum(dist, 0)
    large = max_exact + (jnp.log(jnp.maximum(d, 1).astype(F32) / max_exact)
                         / math.log(T5_MAX_DIST / max_exact) * (N_BUCKETS - max_exact)).astype(I32)
    large = jnp.minimum(large, N_BUCKETS - 1)
    return jnp.where(d < max_exact, d, large)


def _rel():
    return jnp.arange(BLOCK)[:, None] + BLOCK - jnp.arange(2 * BLOCK)[None, :]


def _band_bias(table, dils, max_dists):
    rel = _rel()
    biases, buckets = [], []
    for d, md in zip(dils, max_dists):
        bk = _t5_bucket(rel * d)
        vis = (rel >= 0) & (rel <= md)
        looked_up = jnp.zeros((table.shape[1],) + rel.shape, F32)
        for b in range(N_BUCKETS):
            looked_up = jnp.where((bk == b)[None], table[b][:, None, None], looked_up)
        with_prev = jnp.where(vis[None], looked_up, NEG_INF)
        first = jnp.arange(2 * BLOCK)[None, None, :] >= BLOCK
        biases.append(jnp.stack([with_prev, jnp.where(first, with_prev, NEG_INF)]))
        buckets.append(bk.astype(I32))
    return jnp.stack(biases), jnp.stack(buckets)


def _pack(pieces, rows):
    flat = jnp.concatenate([p.reshape(-1) for p in pieces])
    return jnp.pad(flat, (0, rows * LANES - flat.shape[0])).reshape(rows, LANES)


def _unpack(packed, shapes):
    flat = packed.reshape(-1)
    out, off = [], 0
    for sh in shapes:
        n = math.prod(sh)
        out.append(flat[off:off + n].reshape(sh))
        off += n
    return out


def _tile2(g):
    return jnp.concatenate([g, g])


def kernel(x, attn_norm, w_in, a_q_gain, a_k_gain, a_sinks, c_q_gain, c_k_gain, rel_bias_table, mix_out_gain, w_out, ffn_norm, w_up, conv_w, conv_b, w_down, loss_target, m_attn_norm, m_w_in, m_a_q_gain, m_a_k_gain, m_a_sinks, m_c_q_gain, m_c_k_gain, m_rel_bias_table, m_mix_out_gain, m_w_out, m_ffn_norm, m_w_up, m_conv_w, m_conv_b, m_w_down, v_attn_norm, v_w_in, v_a_q_gain, v_a_k_gain, v_a_sinks, v_c_q_gain, v_c_k_gain, v_rel_bias_table, v_mix_out_gain, v_w_out, v_ffn_norm, v_w_up, v_conv_w, v_conv_b, v_w_down):
    depth, d_model, in_shard = w_in.shape
    ff2_shard = w_up.shape[2]
    s = x.shape[1]
    in_width, ff2 = N_DEV * in_shard, N_DEV * ff2_shard
    n_heads = d_model // HEAD_DIM
    ha, hb, hc = n_heads // 4, n_heads // 4, n_heads // 2
    sa, sb, sc = ha // 2, hb // 2, hc // 2
    kv_a = ha // 4
    assert kv_a == 2 and BLOCK == LANES
    cb_aq, cb_ak, cb_av = 0, sa, sa + 1
    cb_bq = sa + 2
    cb_bk, cb_bv = cb_bq + sb, cb_bq + 2 * sb
    cb_cq = cb_bq + 3 * sb
    cb_ck, cb_cv = cb_cq + sc, cb_cq + 2 * sc
    assert (cb_cv + sc) * LANES == in_width
    dev = 4 * lax.axis_index("x") + 2 * lax.axis_index("y") + lax.axis_index("c")

    def landing(own):
        return lax.dynamic_update_slice_in_dim(lax.empty((N_DEV,) + own.shape, own.dtype), own[None], dev, axis=0)

    per_array = 2
    wnames = ("w_in", "w_out", "w_up", "w_down", "conv_w")
    cols_to_rows = lambda g: jnp.transpose(g, (1, 0, 2)).reshape(g.shape[1], N_DEV * g.shape[2])
    whole = dict(w_in=cols_to_rows, w_up=cols_to_rows, conv_w=cols_to_rows,
                 w_out=lambda g: g.reshape(d_model, d_model), w_down=lambda g: g.reshape(ff2 // 2, d_model))
    gathers = {}
    token = jnp.zeros((8, LANES), F32)
    for l in range(depth):
        for gi, group in enumerate([[n] for n in wnames] if l < per_array else [wnames]):
            srcs = [(dict(w_in=w_in, w_out=w_out, w_up=w_up, w_down=w_down, conv_w=conv_w)[n][l] + token[0, 0])
                    .astype(F32 if n == "conv_w" else BF16) for n in group]
            handles, token = exchange_start(srcs, [landing(a) for a in srcs], False, name=f"gather_start_{l}_{gi}")
            gathers.update({(l, n): h for n, h in zip(group, handles)})

    def gathered(l, names, after):
        landed = exchange_wait([gathers[l, n] for n in names], False, after,
                               name=f"gather_wait_{l}_{wnames.index(names[0])}")
        return {n: whole[n](g) for n, g in zip(names, landed)}

    bias_a, buckets_a = _band_bias(rel_bias_table[:, :ha], (1,), (WINDOW_A - 1,))
    bias_c, buckets_c = _band_bias(rel_bias_table[:, ha:], DILATIONS, (BLOCK,) * len(DILATIONS))

    xs = x[0]
    saved = []
    wi, wo, wu, wd, cw = ([None] * depth for _ in range(5))
    for l in range(depth):
        if l < per_array:
            need = lambda n, after, l=l: gathered(l, (n,), after)[n]
        else:
            layer_w = gathered(l, wnames, xs)
            need = lambda n, after: layer_w[n]
        wi[l] = need("w_in", token if l == 0 else xs)
        h1 = rmsnorm_fwd(xs, attn_norm[l], name="attn_norm_fwd")
        proj = matmul(h1, wi[l], name="in_proj")
        sinks = jnp.repeat(a_sinks[l], HEAD_DIM).reshape(sa, 1, LANES)
        gaq, gak = _tile2(a_q_gain[l]), _tile2(a_k_gain[l])
        gcq, gck = _tile2(c_q_gain[l]), _tile2(c_k_gain[l])
        out_a, lse_a = banded_fwd(proj, cb_aq, cb_ak, cb_av, sa, gaq, gak, bias_a, (1,), sinks, True, name="swa_fwd")
        out_b, tot_b = sb_fwd(proj, cb_bq, cb_bk, cb_bv, sb, name="stick_fwd")
        out_c, lse_c = banded_fwd(proj, cb_cq, cb_ck, cb_cv, sc, gcq, gck, bias_c, DILATIONS, None, False,
                                  name="dilated_fwd")
        mix = mixnorm_fwd([out_a, out_b, out_c], mix_out_gain[l], name="mix_norm_fwd")
        wo[l] = need("w_out", mix)
        x_mid = matmul(mix, wo[l], res=xs, name="out_proj")
        h2 = rmsnorm_fwd(x_mid, ffn_norm[l], name="ffn_norm_fwd")
        wu[l] = need("w_up", h2)
        p = matmul(h2, wu[l], name="up_proj")
        cw[l] = need("conv_w", p)
        act = ffn_act_fwd(p, cw[l], conv_b[l], name="ffn_act_fwd")
        wd[l] = need("w_down", act)
        x_out = matmul(act, wd[l], res=x_mid, name="down_proj")
        saved.append(dict(x_in=xs, h1=h1, proj=proj, out_a=out_a, lse_a=lse_a, out_b=out_b, tot_b=tot_b,
                          out_c=out_c, lse_c=lse_c, mix=mix, x_mid=x_mid, h2=h2, p=p, act=act,
                          sinks=sinks, gains=(gaq, gak, gcq, gck)))
        xs = x_out

    dx, dx_b, loss_part = loss_head(xs, loss_target[0], name="loss_head")

    small = {k: [None] * depth for k in ("attn_norm", "a_q_gain", "a_k_gain", "a_sinks", "c_q_gain", "c_k_gain",
                                         "mix_out_gain", "ffn_norm", "conv_w", "conv_b")}
    big = {k: [None] * depth for k in ("w_in", "w_out", "w_up", "w_down")}
    dbias_a = dbias_c = None
    scatters = {}
    token = jnp.zeros((8, LANES), F32)
    names_big = ("w_in", "w_out", "w_up", "w_down")

    def scatter(l, names):
        parts = [big[n][l] for n in names]
        lands = [landing(lax.dynamic_index_in_dim(pt, dev, axis=0, keepdims=False)) for pt in parts]
        handles, tok = exchange_start(parts, lands, True, name=f"scatter_start_{l}_{names_big.index(names[0])}")
        scatters.update({(l, n): h for n, h in zip(names, handles)})
        return tok

    by_cols = lambda a: jnp.transpose(a.reshape(a.shape[0], N_DEV, a.shape[1] // N_DEV), (1, 0, 2))
    by_rows = lambda a: a.reshape(N_DEV, a.shape[0] // N_DEV, a.shape[1])
    for l in reversed(range(depth)):
        each = l == 0
        sv = saved[l]
        gaq, gak, gcq, gck = sv["gains"]
        da = matmul(dx_b, wd[l], trans_b=True, name="down_proj_dx")
        big["w_down"][l] = by_rows(matmul(sv["act"], dx_b, trans_a=True, out_dtype=BF16, name="down_proj_dw"))
        if each:
            token = scatter(l, ("w_down",))
        dp, small["conv_w"][l], small["conv_b"][l] = ffn_act_bwd(da, sv["p"], cw[l], conv_b[l] + token[0, 0],
                                                                 name="ffn_act_bwd")
        dh2 = matmul(dp, wu[l], trans_b=True, name="up_proj_dx")
        big["w_up"][l] = matmul(sv["h2"], dp, trans_a=True, out_dtype=BF16, col_blocks=N_DEV, name="up_proj_dw")
        if each:
            token = scatter(l, ("w_up",))
        dx_mid, dx_mid_b, small["ffn_norm"][l] = rmsnorm_bwd(dh2, sv["x_mid"], ffn_norm[l] + token[0, 0], dx,
                                                   name="ffn_norm_bwd")
        dmix = matmul(dx_mid_b, wo[l], trans_b=True, name="out_proj_dx")
        big["w_out"][l] = by_rows(matmul(sv["mix"], dx_mid_b, trans_a=True, out_dtype=BF16, name="out_proj_dw"))
        if each:
            token = scatter(l, ("w_out",))
        (d_oa, d_ob, d_oc), small["mix_out_gain"][l] = mixnorm_bwd(
            dmix, [sv["out_a"], sv["out_b"], sv["out_c"]], mix_out_gain[l] + token[0, 0], name="mix_norm_bwd")
        dq_a, dk_a, dv_a, db_a, dgq_a, dgk_a, dsink = banded_bwd(
            sv["proj"], cb_aq, cb_ak, cb_av, sa, gaq, gak, bias_a, (1,), sv["sinks"], True,
            d_oa, sv["out_a"], sv["lse_a"], name="swa_bwd")
        dq_b, dk_b, dv_b = sb_bwd(sv["proj"], cb_bq, cb_bk, cb_bv, sb, d_ob, sv["tot_b"], name="stick_bwd")
        dq_c, dk_c, dv_c, db_c, dgq_c, dgk_c = banded_bwd(
            sv["proj"], cb_cq, cb_ck, cb_cv, sc, gcq, gck, bias_c, DILATIONS, None, False,
            d_oc, sv["out_c"], sv["lse_c"], name="dilated_bwd")
        fold = lambda g: g.reshape(-1, HEAD_DIM).sum(axis=0)
        small["a_q_gain"][l], small["a_k_gain"][l] = fold(dgq_a), fold(dgk_a)
        small["c_q_gain"][l], small["c_k_gain"][l] = fold(dgq_c), fold(dgk_c)
        small["a_sinks"][l] = dsink[:, ::HEAD_DIM].reshape(-1)
        dbias_a = db_a if dbias_a is None else dbias_a + db_a
        dbias_c = db_c if dbias_c is None else dbias_c + db_c
        dproj = jnp.concatenate([dq_a, dk_a, dv_a, dq_b, dk_b, dv_b, dq_c, dk_c, dv_c], axis=1).astype(BF16)
        big["w_in"][l] = by_cols(matmul(sv["h1"], dproj, trans_a=True, out_dtype=BF16, name="in_proj_dw"))
        if not each:
            token = scatter(l, names_big)
        dh1 = matmul(dproj, wi[l], trans_b=True, name="in_proj_dx")
        dx, dx_b, small["attn_norm"][l] = rmsnorm_bwd(dh1, sv["x_in"], attn_norm[l] + token[0, 0], dx_mid,
                                                name="attn_norm_bwd")

    dtable = jnp.concatenate([bias_bwd(dbias_a, buckets_a, name="swa_bias_bwd"),
                              bias_bwd(dbias_c, buckets_c, name="dilated_bias_bwd")], axis=1)

    order = ("attn_norm", "a_q_gain", "a_k_gain", "a_sinks", "c_q_gain", "c_k_gain", "rel_bias_table",
             "mix_out_gain", "ffn_norm", "conv_w", "conv_b")
    partial = {k: jnp.stack(v) for k, v in small.items()}
    partial["rel_bias_table"] = dtable
    pieces = [partial[k] for k in order] + [loss_part.reshape(1)]
    n_small = sum(math.prod(pc.shape) for pc in pieces)
    rows = -(-n_small // (8 * LANES)) * 8
    gathered = gather_small(_pack(pieces, rows), name="gather_small_grads")
    big["w_in"][0], gathered = lax.optimization_barrier((big["w_in"][0], gathered))
    scatter(0, ("w_in",))
    summed = _unpack(sum_devices(gathered.reshape(N_DEV, rows, LANES), name="sum_small_grads"),
                     [pc.shape for pc in pieces])
    g_small = dict(zip(order, summed[:-1]))
    loss = summed[-1][0]
    g_small["conv_w"] = lax.dynamic_slice_in_dim(g_small["conv_w"], dev * ff2_shard, ff2_shard, axis=2)

    w_small = dict(attn_norm=attn_norm, a_q_gain=a_q_gain, a_k_gain=a_k_gain, a_sinks=a_sinks, c_q_gain=c_q_gain,
                   c_k_gain=c_k_gain, rel_bias_table=rel_bias_table, mix_out_gain=mix_out_gain, ffn_norm=ffn_norm,
                   conv_w=conv_w, conv_b=conv_b)
    m_small = dict(attn_norm=m_attn_norm, a_q_gain=m_a_q_gain, a_k_gain=m_a_k_gain, a_sinks=m_a_sinks,
                   c_q_gain=m_c_q_gain, c_k_gain=m_c_k_gain, rel_bias_table=m_rel_bias_table,
                   mix_out_gain=m_mix_out_gain, ffn_norm=m_ffn_norm, conv_w=m_conv_w, conv_b=m_conv_b)
    v_small = dict(attn_norm=v_attn_norm, a_q_gain=v_a_q_gain, a_k_gain=v_a_k_gain, a_sinks=v_a_sinks,
                   c_q_gain=v_c_q_gain, c_k_gain=v_c_k_gain, rel_bias_table=v_rel_bias_table,
                   mix_out_gain=v_mix_out_gain, ffn_norm=v_ffn_norm, conv_w=v_conv_w, conv_b=v_conv_b)
    shapes = [w_small[k].shape for k in order]
    n_upd = sum(math.prod(sh) for sh in shapes)
    urows = -(-n_upd // (8 * LANES)) * 8
    packs = [_pack([d[k] for k in order], urows) for d in (g_small, w_small, m_small, v_small)]
    upd = adamw_small(*packs, name="adamw_small")
    delta_s, newm_s, newv_s = [dict(zip(order, _unpack(u, shapes))) for u in upd]

    w_big = dict(w_in=(w_in, m_w_in, v_w_in), w_out=(w_out, m_w_out, v_w_out), w_up=(w_up, m_w_up, v_w_up),
                 w_down=(w_down, m_w_down, v_w_down))
    results = {k: [lax.empty(w_big[k][0].shape, F32) for _ in range(4)] for k in names_big}
    after = upd[0]
    for l, names in [(l, names_big) for l in reversed(range(1, depth))] + [(0, names_big[1:]), (0, names_big[:1])]:
        landed = exchange_wait([scatters[l, n] for n in names], True, after,
                               name=f"scatter_wait_{l}_{names_big.index(names[0])}")
        for k, parts in zip(names, landed):
            results[k] = adamw_parts(parts, *w_big[k], l, results[k], name="adamw_large")
        after = results[names[-1]][0]
    g_big, delta_b, newm_b, newv_b = [{k: results[k][i] for k in names_big} for i in range(4)]

    all_names = ("attn_norm", "w_in", "a_q_gain", "a_k_gain", "a_sinks", "c_q_gain", "c_k_gain", "rel_bias_table",
                 "mix_out_gain", "w_out", "ffn_norm", "w_up", "conv_w", "conv_b", "w_down")
    pick = lambda sm, bg: [bg[k] if k in bg else sm[k] for k in all_names]
    return (loss, dx[None], *pick(g_small, g_big), *pick(delta_s, delta_b), *pick(newm_s, newm_b),
            *pick(newv_s, newv_b))
```

```python
import functools
import math

import jax
import jax.numpy as jnp
from jax import lax
from jax.experimental import pallas as pl
from jax.experimental.pallas import tpu as pltpu

F32, BF16, I32 = jnp.float32, jnp.bfloat16, jnp.int32
MESH = pl.DeviceIdType.MESH

HEAD_DIM = 64
LANES = 128
BLOCK = 128
EPS = 1e-6
NEG_INF = -1e30
N_BUCKETS = 32
T5_MAX_DIST = 2048
WINDOW_A = 128
DILATIONS = (1, 4, 16)
N_DEV = 8
VMEM_LIMIT = 56 * 1024 * 1024
MATMUL_VMEM = 46 * 1024 * 1024

ADAM_LR, ADAM_B1, ADAM_B2, ADAM_EPS, ADAM_WD, ADAM_STEP = 0.001, 0.9, 0.999, 1e-08, 0.01, 10


def _params(sem=None, vmem=None):
    return pltpu.CompilerParams(dimension_semantics=sem, vmem_limit_bytes=vmem)


def _pick(n, cands):
    for c in cands:
        if n % c == 0:
            return c
    raise ValueError(f"no tile for {n}")


def _dot(a, b):
    return lax.dot_general(a, b, (((1,), (0,)), ((), ())), preferred_element_type=F32)


def _dot_nt(a, b):
    return lax.dot_general(a, b, (((1,), (1,)), ((), ())), preferred_element_type=F32)


def _dot_tn(a, b):
    return lax.dot_general(a, b, (((0,), (0,)), ((), ())), preferred_element_type=F32)


def matmul(a, b, *, trans_a=False, trans_b=False, out_dtype=F32, res=None, col_blocks=None, name):
    m, k = (a.shape[1], a.shape[0]) if trans_a else a.shape
    n = b.shape[0] if trans_b else b.shape[1]
    tm = _pick(m, (1408, 1024, 512, 256))
    tn_cands = (n // col_blocks,) if col_blocks else tuple(t for t in (1408, 1024, 768, 512, 256, 128) if n % t == 0)

    def footprint(tk, tn):
        tiles = 2 * (tm * tk * a.dtype.itemsize + tk * tn * b.dtype.itemsize)
        return tiles + tm * tn * (4 + 2 * jnp.dtype(out_dtype).itemsize + (8 if res is not None else 0))

    tk, tn = next((tk, tn) for tk in (2816, 2048, 1792, 1024, 768, 512, 256) if k % tk == 0
                  for tn in tn_cands if footprint(tk, tn) <= MATMUL_VMEM)
    nk = k // tk
    dn = (((0 if trans_a else 1,), (1 if trans_b else 0,)), ((), ()))

    def body(*refs):
        if res is None:
            a_ref, b_ref, o_ref, acc = refs
        else:
            a_ref, b_ref, r_ref, o_ref, acc = refs
        kk = pl.program_id(2)

        @pl.when(kk == 0)
        def _():
            acc[...] = jnp.zeros_like(acc)

        acc[...] += lax.dot_general(a_ref[...].astype(BF16), b_ref[...].astype(BF16), dn,
                                    preferred_element_type=F32)

        @pl.when(kk == nk - 1)
        def _():
            r = acc[...]
            if res is not None:
                r = r_ref[...] + r
            o_ref[...] = r.astype(out_dtype)

    b_spec = (pl.BlockSpec((tn, tk), lambda i, j, kk: (j, kk)) if trans_b
              else pl.BlockSpec((tk, tn), lambda i, j, kk: (kk, j)))
    a_spec = (pl.BlockSpec((tk, tm), lambda i, j, kk: (kk, i)) if trans_a
              else pl.BlockSpec((tm, tk), lambda i, j, kk: (i, kk)))
    in_specs = [a_spec, b_spec]
    args = [a, b]
    if res is not None:
        in_specs.append(pl.BlockSpec((tm, tn), lambda i, j, kk: (i, j)))
        args.append(res)
    if col_blocks:
        out_spec = pl.BlockSpec((None, tm, tn), lambda i, j, kk: (j, i, 0))
        out_shape = jax.ShapeDtypeStruct((col_blocks, m, tn), out_dtype)
    else:
        out_spec = pl.BlockSpec((tm, tn), lambda i, j, kk: (i, j))
        out_shape = jax.ShapeDtypeStruct((m, n), out_dtype)
    return pl.pallas_call(
        body, name=name, grid=(m // tm, n // tn, nk),
        in_specs=in_specs, out_specs=out_spec, out_shape=out_shape,
        scratch_shapes=[pltpu.VMEM((tm, tn), F32)],
        compiler_params=_params(("parallel", "parallel", "arbitrary"), VMEM_LIMIT),
    )(*args)


def rmsnorm_fwd(x, g, *, name):
    s, d = x.shape
    tm = 512

    def body(x_ref, g_ref, o_ref):
        xv = x_ref[...]
        r = lax.rsqrt(jnp.mean(xv * xv, axis=-1, keepdims=True) + EPS)
        o_ref[...] = (xv * r * g_ref[...]).astype(BF16)

    return pl.pallas_call(
        body, name=name, grid=(s // tm,),
        in_specs=[pl.BlockSpec((tm, d), lambda i: (i, 0)), pl.BlockSpec((1, d), lambda i: (0, 0))],
        out_specs=pl.BlockSpec((tm, d), lambda i: (i, 0)),
        out_shape=jax.ShapeDtypeStruct((s, d), BF16),
        compiler_params=_params(("parallel",)),
    )(x, g.reshape(1, d))


def rmsnorm_bwd(dh, x, g, dres, *, name):
    s, d = x.shape
    tm = 256

    def body(dh_ref, x_ref, g_ref, dres_ref, dx_ref, dxb_ref, dg_ref):
        @pl.when(pl.program_id(0) == 0)
        def _():
            dg_ref[...] = jnp.zeros_like(dg_ref)

        xv, dhv = x_ref[...], dh_ref[...]
        r = lax.rsqrt(jnp.mean(xv * xv, axis=-1, keepdims=True) + EPS)
        gd = dhv * g_ref[...]
        dot = jnp.mean(gd * xv, axis=-1, keepdims=True)
        dx = dres_ref[...] + (r * gd - xv * (r * r * r * dot))
        dx_ref[...] = dx
        dxb_ref[...] = dx.astype(BF16)
        dg_ref[...] += jnp.sum(dhv * (xv * r), axis=0, keepdims=True)

    row = pl.BlockSpec((tm, d), lambda i: (i, 0))
    dx, dxb, dg = pl.pallas_call(
        body, name=name, grid=(s // tm,),
        in_specs=[row, row, pl.BlockSpec((1, d), lambda i: (0, 0)), row],
        out_specs=[row, row, pl.BlockSpec((1, d), lambda i: (0, 0))],
        out_shape=[jax.ShapeDtypeStruct((s, d), F32), jax.ShapeDtypeStruct((s, d), BF16),
                   jax.ShapeDtypeStruct((1, d), F32)],
        compiler_params=_params(("arbitrary",)),
    )(dh, x, g.reshape(1, d), dres)
    return dx, dxb, dg[0]


def loss_head(y, target, *, name):
    s, d = y.shape
    tm = 512

    def body(y_ref, t_ref, dy_ref, dyb_ref, l_ref):
        @pl.when(pl.program_id(0) == 0)
        def _():
            l_ref[...] = jnp.zeros_like(l_ref)

        e = y_ref[...] - t_ref[...]
        dy = e / float(d)
        dy_ref[...] = dy
        dyb_ref[...] = dy.astype(BF16)
        per_tok = jnp.mean(e * e, axis=-1, keepdims=True)
        l_ref[...] += 0.5 * jnp.sum(per_tok, axis=0, keepdims=True)

    row = pl.BlockSpec((tm, d), lambda i: (i, 0))
    dy, dyb, l = pl.pallas_call(
        body, name=name, grid=(s // tm,),
        in_specs=[row, row],
        out_specs=[row, row, pl.BlockSpec((8, LANES), lambda i: (0, 0))],
        out_shape=[jax.ShapeDtypeStruct((s, d), F32), jax.ShapeDtypeStruct((s, d), BF16),
                   jax.ShapeDtypeStruct((8, LANES), F32)],
        compiler_params=_params(("arbitrary",)),
    )(y, target)
    return dy, dyb, l[0, 0]


FFN_TN = 256
FFN_CH = 256


def _rows_before(ref, r0, first):
    if first:
        cur = ref[pl.ds(0, FFN_CH), :]
        row = lax.broadcasted_iota(I32, cur.shape, 0)
        sh1 = jnp.where(row < 1, 0.0, pltpu.roll(cur, 1, axis=0))
        sh2 = jnp.where(row < 2, 0.0, pltpu.roll(cur, 2, axis=0))
        return cur, sh1, sh2
    ext = ref[pl.ds(pl.multiple_of(r0 - 8, 8), FFN_CH + 8), :]
    return ext[8:], pltpu.roll(ext, 1, axis=0)[8:], pltpu.roll(ext, 2, axis=0)[8:]


def _rows_after(ref, r0, last):
    if last:
        cur = ref[pl.ds(r0, FFN_CH), :]
        row = lax.broadcasted_iota(I32, cur.shape, 0)
        up1 = jnp.where(row >= FFN_CH - 1, 0.0, pltpu.roll(cur, FFN_CH - 1, axis=0))
        up2 = jnp.where(row >= FFN_CH - 2, 0.0, pltpu.roll(cur, FFN_CH - 2, axis=0))
        return cur, up1, up2
    n = FFN_CH + 8
    ext = ref[pl.ds(r0, n), :]
    return ext[:FFN_CH], pltpu.roll(ext, n - 1, axis=0)[:FFN_CH], pltpu.roll(ext, n - 2, axis=0)[:FFN_CH]


def _sigmoid(x):
    return 0.5 * jnp.tanh(0.5 * x) + 0.5


def ffn_act_fwd(p, conv_w, conv_b, *, name):
    s, f2 = p.shape
    f = f2 // 2
    nj = f // FFN_TN
    nch = s // FFN_CH

    def body(pg_ref, pu_ref, wg_ref, wu_ref, bg_ref, bu_ref, a_ref):
        def conv(ref, w_ref, b_ref, r0, first):
            cur, sh1, sh2 = _rows_before(ref, r0, first)
            return ((b_ref[...] + w_ref[0:1, :] * sh2) + w_ref[1:2, :] * sh1) + w_ref[2:3, :] * cur

        def chunk(r0, first):
            gate = conv(pg_ref, wg_ref, bg_ref, r0, first)
            up = conv(pu_ref, wu_ref, bu_ref, r0, first)
            a_ref[pl.ds(r0, FFN_CH), :] = (gate * _sigmoid(gate) * up).astype(BF16)

        chunk(0, True)

        def step(c, carry):
            chunk(pl.multiple_of(c * FFN_CH, FFN_CH), False)
            return carry

        lax.fori_loop(1, nch, step, 0)

    col = lambda off: pl.BlockSpec((s, FFN_TN), lambda j: (0, j + off))
    wcol = lambda off: pl.BlockSpec((3, FFN_TN), lambda j: (0, j + off))
    bcol = lambda off: pl.BlockSpec((1, FFN_TN), lambda j: (0, j + off))
    return pl.pallas_call(
        body, name=name, grid=(nj,),
        in_specs=[col(0), col(nj), wcol(0), wcol(nj), bcol(0), bcol(nj)],
        out_specs=pl.BlockSpec((s, FFN_TN), lambda j: (0, j)),
        out_shape=jax.ShapeDtypeStruct((s, f), BF16),
        compiler_params=_params(("parallel",), VMEM_LIMIT),
    )(p, p, conv_w, conv_w, conv_b.reshape(1, f2), conv_b.reshape(1, f2))


def ffn_act_bwd(da, p, conv_w, conv_b, *, name):
    s, f2 = p.shape
    f = f2 // 2
    nj = f // FFN_TN
    nch = s // FFN_CH

    def body(da_ref, pg_ref, pu_ref, wg_ref, wu_ref, bg_ref, bu_ref,
             dpg_ref, dpu_ref, dwg_ref, dwu_ref, dbg_ref, dbu_ref, dug_s, duu_s):
        def conv(ref, w_ref, b_ref, r0, first):
            cur, sh1, sh2 = _rows_before(ref, r0, first)
            u = ((b_ref[...] + w_ref[0:1, :] * sh2) + w_ref[1:2, :] * sh1) + w_ref[2:3, :] * cur
            return u, (sh2, sh1, cur)

        def taps_sum(du, taps):
            return jnp.concatenate([jnp.sum(du * t, axis=0, keepdims=True) for t in taps], axis=0)

        def chunk(r0, first, acc):
            dwg, dwu, dbg, dbu = acc
            gate, tg = conv(pg_ref, wg_ref, bg_ref, r0, first)
            up, tu = conv(pu_ref, wu_ref, bu_ref, r0, first)
            dav = da_ref[pl.ds(r0, FFN_CH), :]
            sg = _sigmoid(gate)
            dgate = dav * up * (sg * (1.0 + gate * (1.0 - sg)))
            dup = dav * (gate * sg)
            dug_s[pl.ds(r0, FFN_CH), :] = dgate
            duu_s[pl.ds(r0, FFN_CH), :] = dup
            return (dwg + taps_sum(dgate, tg), dwu + taps_sum(dup, tu),
                    dbg + jnp.sum(dgate, axis=0, keepdims=True), dbu + jnp.sum(dup, axis=0, keepdims=True))

        z3 = jnp.zeros((3, FFN_TN), F32)
        z1 = jnp.zeros((1, FFN_TN), F32)
        acc = chunk(0, True, (z3, z3, z1, z1))
        acc = lax.fori_loop(1, nch, lambda c, a: chunk(pl.multiple_of(c * FFN_CH, FFN_CH), False, a), acc)
        dwg_ref[...], dwu_ref[...], dbg_ref[...], dbu_ref[...] = acc

        def back(src, w_ref, dst, r0, last):
            cur, up1, up2 = _rows_after(src, r0, last)
            dst[pl.ds(r0, FFN_CH), :] = (w_ref[2:3, :] * cur + w_ref[1:2, :] * up1 + w_ref[0:1, :] * up2).astype(BF16)

        def step(c, carry):
            r0 = pl.multiple_of(c * FFN_CH, FFN_CH)
            back(dug_s, wg_ref, dpg_ref, r0, False)
            back(duu_s, wu_ref, dpu_ref, r0, False)
            return carry

        lax.fori_loop(0, nch - 1, step, 0)
        back(dug_s, wg_ref, dpg_ref, (nch - 1) * FFN_CH, True)
        back(duu_s, wu_ref, dpu_ref, (nch - 1) * FFN_CH, True)

    col = lambda off: pl.BlockSpec((s, FFN_TN), lambda j: (0, j + off))
    wcol = lambda off: pl.BlockSpec((3, FFN_TN), lambda j: (0, j + off))
    bcol = lambda off: pl.BlockSpec((1, FFN_TN), lambda j: (0, j + off))
    outs = pl.pallas_call(
        body, name=name, grid=(nj,),
        in_specs=[col(0), col(0), col(nj), wcol(0), wcol(nj), bcol(0), bcol(nj)],
        out_specs=[col(0), col(0), wcol(0), wcol(0), bcol(0), bcol(0)],
        out_shape=[jax.ShapeDtypeStruct((s, f), BF16), jax.ShapeDtypeStruct((s, f), BF16),
                   jax.ShapeDtypeStruct((3, f), F32), jax.ShapeDtypeStruct((3, f), F32),
                   jax.ShapeDtypeStruct((1, f), F32), jax.ShapeDtypeStruct((1, f), F32)],
        scratch_shapes=[pltpu.VMEM((s, FFN_TN), F32), pltpu.VMEM((s, FFN_TN), F32)],
        compiler_params=_params(("parallel",), VMEM_LIMIT),
    )(da, p, p, conv_w, conv_w, conv_b.reshape(1, f2), conv_b.reshape(1, f2))
    dpg, dpu, dwg, dwu, dbg, dbu = outs
    return (jnp.concatenate([dpg, dpu], axis=1), jnp.concatenate([dwg, dwu], axis=1),
            jnp.concatenate([dbg, dbu], axis=1)[0])


def mixnorm_fwd(outs, gain, *, name):
    s = outs[0].shape[0]
    widths = [o.shape[1] for o in outs]
    total = sum(widths)
    tm = 512

    def body(*refs):
        o_refs, g_ref, m_ref = refs[:-2], refs[-2], refs[-1]
        off = 0
        for o_ref, w in zip(o_refs, widths):
            xv = o_ref[...]
            r = lax.rsqrt(jnp.mean(xv * xv, axis=-1, keepdims=True) + EPS)
            m_ref[:, off:off + w] = (xv * r * g_ref[:, off:off + w]).astype(BF16)
            off += w

    return pl.pallas_call(
        body, name=name, grid=(s // tm,),
        in_specs=[pl.BlockSpec((tm, w), lambda i: (i, 0)) for w in widths] + [pl.BlockSpec((1, total), lambda i: (0, 0))],
        out_specs=pl.BlockSpec((tm, total), lambda i: (i, 0)),
        out_shape=jax.ShapeDtypeStruct((s, total), BF16),
        compiler_params=_params(("parallel",)),
    )(*outs, gain.reshape(1, total))


def mixnorm_bwd(dmix, outs, gain, *, name):
    s = outs[0].shape[0]
    widths = [o.shape[1] for o in outs]
    total = sum(widths)
    n = len(outs)
    tm = 256

    def body(*refs):
        dm_ref, o_refs, g_ref = refs[0], refs[1:1 + n], refs[1 + n]
        d_refs, dg_ref = refs[2 + n:2 + 2 * n], refs[2 + 2 * n]

        @pl.when(pl.program_id(0) == 0)
        def _():
            dg_ref[...] = jnp.zeros_like(dg_ref)

        off = 0
        for o_ref, d_ref, w in zip(o_refs, d_refs, widths):
            xv = o_ref[...]
            dhv = dm_ref[:, off:off + w]
            r = lax.rsqrt(jnp.mean(xv * xv, axis=-1, keepdims=True) + EPS)
            gd = dhv * g_ref[:, off:off + w]
            dot = jnp.mean(gd * xv, axis=-1, keepdims=True)
            d_ref[...] = r * gd - xv * (r * r * r * dot)
            dg_ref[:, off:off + w] += jnp.sum(dhv * (xv * r), axis=0, keepdims=True)
            off += w

    res = pl.pallas_call(
        body, name=name, grid=(s // tm,),
        in_specs=[pl.BlockSpec((tm, total), lambda i: (i, 0))]
        + [pl.BlockSpec((tm, w), lambda i: (i, 0)) for w in widths] + [pl.BlockSpec((1, total), lambda i: (0, 0))],
        out_specs=[pl.BlockSpec((tm, w), lambda i: (i, 0)) for w in widths] + [pl.BlockSpec((1, total), lambda i: (0, 0))],
        out_shape=[jax.ShapeDtypeStruct((s, w), F32) for w in widths] + [jax.ShapeDtypeStruct((1, total), F32)],
        compiler_params=_params(("arbitrary",)),
    )(dmix, *outs, gain.reshape(1, total))
    return res[:n], res[n][0]


NORM_CH = 512
FWD_TILES = 4
BWD_TILES = 2


def _lo_mask(shape):
    return lax.broadcasted_iota(I32, shape, 1) < HEAD_DIM


def _head_sum(x, lo):
    del lo
    i = lax.broadcasted_iota(I32, (LANES, LANES), 0) // HEAD_DIM
    j = lax.broadcasted_iota(I32, (LANES, LANES), 1) // HEAD_DIM
    return _split_dot(x, _twice(i == j))


def _head_stats(x, lo):
    return lax.rsqrt(_head_sum(x * x, lo) * (1.0 / HEAD_DIM) + EPS)


def _swap_halves(x):
    return pltpu.roll(x, HEAD_DIM, axis=1)


def _replicate_head(x, lo, use_lo_head):
    sw = _swap_halves(x)
    return jnp.where(use_lo_head, jnp.where(lo, x, sw), jnp.where(lo, sw, x))


def _tile_rows(i, s, d):
    nb = s // (BLOCK * d)
    r = i // nb
    b = i % nb
    start = r + (BLOCK * d) * b
    prev = start - (BLOCK * d) * jnp.minimum(b, 1)
    return start, prev, b > 0


def _rows(ref, start, d):
    if d == 1:
        return ref[pl.ds(pl.multiple_of(start, BLOCK), BLOCK), :]
    return ref[pl.ds(start, BLOCK, stride=d), :]


def _set_rows(ref, start, d, val):
    if d == 1:
        ref[pl.ds(pl.multiple_of(start, BLOCK), BLOCK), :] = val
    else:
        ref[pl.ds(start, BLOCK, stride=d), :] = val


def banded_fwd(proj, qb0, kb0, vb0, n_slabs, gq, gk, bias, dils, sinks, gqa, *, name):
    s = proj.shape[0]
    nbr = len(dils)
    nt = s // BLOCK
    nch = s // NORM_CH
    has_sink = sinks is not None

    def body(*refs):
        q_ref, k_ref, v_ref, gq_ref, gk_ref, b_ref = refs[:6]
        rest = refs[6:]
        if has_sink:
            sink_ref, rest = rest[0], rest[1:]
        out_ref, lse_ref, qn_s, kn_s, vv_s, o_s, l_s = rest
        p = pl.program_id(0)
        use_lo = (p // 2) == 0

        def prep(c, carry):
            rows = pl.ds(pl.multiple_of(c * NORM_CH, NORM_CH), NORM_CH)
            lo = _lo_mask((NORM_CH, LANES))
            qv, kv, vv = q_ref[rows, :], k_ref[rows, :], v_ref[rows, :]
            qn_s[rows, :] = qv * _head_stats(qv, lo) * gq_ref[...] * (HEAD_DIM ** -0.5)
            kn = kv * _head_stats(kv, lo) * gk_ref[...]
            if gqa:
                kn = _replicate_head(kn, lo, use_lo)
                vv = _replicate_head(vv, lo, use_lo)
            kn_s[rows, :] = kn
            vv_s[rows, :] = vv
            return carry

        lax.fori_loop(0, nch, prep, 0)

        lo = _lo_mask((BLOCK, LANES))
        hms = [lo, jnp.logical_not(lo)]
        heads, tiles = range(2), range(FWD_TILES)
        for br, d in enumerate(dils):
            def step(ii, carry, br=br, d=d):
                pos = [_tile_rows(ii * FWD_TILES + u, s, d) for u in tiles]
                kc = [carry[0]] + [_rows(kn_s, pos[u][0], d).astype(BF16) for u in tiles]
                vc = [carry[1]] + [_rows(vv_s, pos[u][0], d).astype(BF16) for u in tiles]
                kcat = [jnp.concatenate([kc[u], kc[u + 1]], axis=0) for u in tiles]
                vcat = [jnp.concatenate([vc[u], vc[u + 1]], axis=0) for u in tiles]
                qt = [_rows(qn_s, pos[u][0], d) for u in tiles]
                sc = [[_dot_nt(jnp.where(hms[h], qt[u], 0.0).astype(BF16), kcat[u])
                       + b_ref[br, jnp.where(pos[u][2], 0, 1), h] for h in heads] for u in tiles]
                m = [[jnp.max(sc[u][h], axis=1, keepdims=True) for h in heads] for u in tiles]
                pe = [[jnp.exp(sc[u][h] - m[u][h]) for h in heads] for u in tiles]
                den = [[jnp.sum(pe[u][h], axis=1, keepdims=True) for h in heads] for u in tiles]
                o = [[_dot(pe[u][h].astype(BF16), vcat[u]) * (1.0 / den[u][h]) for h in heads] for u in tiles]
                for u in tiles:
                    _set_rows(o_s.at[br], pos[u][0], d, jnp.where(lo, o[u][0], o[u][1]))
                    _set_rows(l_s.at[br], pos[u][0], d,
                              jnp.where(lo, m[u][0] + jnp.log(den[u][0]), m[u][1] + jnp.log(den[u][1])))
                return kc[-1], vc[-1]

            none_yet = jnp.zeros((BLOCK, LANES), BF16)
            lax.fori_loop(0, nt // FWD_TILES, step, (none_yet, none_yet))

        def combine(c, carry):
            rows = pl.ds(pl.multiple_of(c * NORM_CH, NORM_CH), NORM_CH)
            ls = [l_s[br, rows, :] for br in range(nbr)]
            mx = functools.reduce(jnp.maximum, ls)
            if has_sink:
                mx = jnp.maximum(mx, sink_ref[...])
            tot = functools.reduce(jnp.add, [jnp.exp(l - mx) for l in ls])
            if has_sink:
                tot = tot + jnp.exp(sink_ref[...] - mx)
            lse = mx + jnp.log(tot)
            acc = jnp.exp(ls[0] - lse) * o_s[0, rows, :]
            for br in range(1, nbr):
                acc = acc + jnp.exp(ls[br] - lse) * o_s[br, rows, :]
            out_ref[rows, :] = acc
            lse_ref[rows, :] = lse
            return carry

        lax.fori_loop(0, nch, combine, 0)

    slab = lambda b0, shared: pl.BlockSpec((s, LANES), (lambda p: (0, b0)) if shared else (lambda p: (0, b0 + p)),
                                           pipeline_mode=pl.Buffered(1))
    vec = pl.BlockSpec((1, LANES), lambda p: (0, 0))
    in_specs = [slab(qb0, False), slab(kb0, gqa), slab(vb0, gqa), vec, vec,
                pl.BlockSpec((nbr, 2, 2, BLOCK, 2 * BLOCK), lambda p: (0, 0, p, 0, 0))]
    args = [proj, proj, proj, gq.reshape(1, LANES), gk.reshape(1, LANES), bias]
    if has_sink:
        in_specs.append(pl.BlockSpec((None, 1, LANES), lambda p: (p, 0, 0)))
        args.append(sinks)
    w = LANES * n_slabs
    return pl.pallas_call(
        body, name=name, grid=(n_slabs,),
        in_specs=in_specs,
        out_specs=[pl.BlockSpec((s, LANES), lambda p: (0, p)), pl.BlockSpec((s, LANES), lambda p: (0, p))],
        out_shape=[jax.ShapeDtypeStruct((s, w), F32), jax.ShapeDtypeStruct((s, w), F32)],
        scratch_shapes=[pltpu.VMEM((s, LANES), F32), pltpu.VMEM((s, LANES), F32), pltpu.VMEM((s, LANES), F32),
                        pltpu.VMEM((nbr, s, LANES), F32), pltpu.VMEM((nbr, s, LANES), F32)],
        compiler_params=_params(("parallel",), VMEM_LIMIT),
    )(*args)


def banded_bwd(proj, qb0, kb0, vb0, n_slabs, gq, gk, bias, dils, sinks, gqa, dout, out, lse, *, name):
    s = proj.shape[0]
    nbr = len(dils)
    nt = s // BLOCK
    nch = s // NORM_CH
    has_sink = sinks is not None
    scale = HEAD_DIM ** -0.5

    def body(*refs):
        q_ref, k_ref, v_ref, gq_ref, gk_ref, b_ref, do_ref, o_ref, lse_ref = refs[:9]
        rest = refs[9:]
        if has_sink:
            sink_ref, rest = rest[0], rest[1:]
        dq_ref, dk_ref, dv_ref, db_ref, dgq_ref, dgk_ref = rest[:6]
        rest = rest[6:]
        if has_sink:
            dsink_ref, rest = rest[0], rest[1:]
        qn_s, kn_s, vv_s, dl_s, dqn_s, dkn_s, dvv_s = rest
        p = pl.program_id(0)
        use_lo = (p // 2) == 0

        def prep(c, carry):
            rows = pl.ds(pl.multiple_of(c * NORM_CH, NORM_CH), NORM_CH)
            lo = _lo_mask((NORM_CH, LANES))
            qv, kv, vv = q_ref[rows, :], k_ref[rows, :], v_ref[rows, :]
            qn_s[rows, :] = qv * _head_stats(qv, lo) * gq_ref[...] * scale
            kn = kv * _head_stats(kv, lo) * gk_ref[...]
            if gqa:
                kn = _replicate_head(kn, lo, use_lo)
                vv = _replicate_head(vv, lo, use_lo)
            kn_s[rows, :] = kn
            vv_s[rows, :] = vv
            delta = _head_sum(do_ref[rows, :] * o_ref[rows, :], lo)
            odd = lax.broadcasted_iota(I32, (NORM_CH, LANES), 1) % 2 == 1
            dl_s[rows, :] = jnp.where(odd, delta, lse_ref[rows, :])
            z = jnp.zeros((NORM_CH, LANES), F32)
            dqn_s[rows, :] = z
            dkn_s[rows, :] = z
            dvv_s[rows, :] = z
            if has_sink:
                ps = jnp.exp(sink_ref[...] - lse_ref[rows, :])
                return carry - jnp.sum(ps * delta, axis=0, keepdims=True)
            return carry

        dsink = lax.fori_loop(0, nch, prep, jnp.zeros((1, LANES), F32))
        if has_sink:
            dsink_ref[...] = jnp.broadcast_to(dsink, (8, LANES))

        lo = _lo_mask((BLOCK, LANES))
        hms = [lo, jnp.logical_not(lo)]
        heads, tiles = range(2), range(BWD_TILES)
        for br, d in enumerate(dils):
            db_ref[br] = jnp.zeros((2, BLOCK, 2 * BLOCK), F32)

            def step(ii, carry, br=br, d=d):
                pos = [_tile_rows(ii * BWD_TILES + u, s, d) for u in tiles]
                kc = [carry[0]] + [_rows(kn_s, pos[u][0], d).astype(BF16) for u in tiles]
                vc = [carry[1]] + [_rows(vv_s, pos[u][0], d).astype(BF16) for u in tiles]
                kcat = [jnp.concatenate([kc[u], kc[u + 1]], axis=0) for u in tiles]
                vcat = [jnp.concatenate([vc[u], vc[u + 1]], axis=0) for u in tiles]
                qt = [_rows(qn_s, pos[u][0], d) for u in tiles]
                dot_ = [_rows(do_ref, pos[u][0], d) for u in tiles]
                st_t = [_rows(dl_s, pos[u][0], d) for u in tiles]
                qh = [[jnp.where(hms[h], qt[u], 0.0).astype(BF16) for h in heads] for u in tiles]
                doh = [[jnp.where(hms[h], dot_[u], 0.0).astype(BF16) for h in heads] for u in tiles]
                sc = [[_dot_nt(qh[u][h], kcat[u]) + b_ref[br, jnp.where(pos[u][2], 0, 1), h] for h in heads]
                      for u in tiles]
                dp = [[_dot_nt(doh[u][h], vcat[u]) for h in heads] for u in tiles]
                lane0 = [0, HEAD_DIM]
                pr = [[jnp.exp(sc[u][h] - st_t[u][:, lane0[h]:lane0[h] + 1]) for h in heads] for u in tiles]
                dlog = [[pr[u][h] * (dp[u][h] - st_t[u][:, lane0[h] + 1:lane0[h] + 2]) for h in heads] for u in tiles]
                for h in heads:
                    db_ref[br, h] += functools.reduce(jnp.add, [dlog[u][h] for u in tiles])
                dlb = [[dlog[u][h].astype(BF16) for h in heads] for u in tiles]
                prb = [[pr[u][h].astype(BF16) for h in heads] for u in tiles]
                dq_t = [jnp.where(lo, _dot(dlb[u][0], kcat[u]), _dot(dlb[u][1], kcat[u])) * scale for u in tiles]
                dk_t = [_dot_tn(dlb[u][0], qh[u][0]) + _dot_tn(dlb[u][1], qh[u][1]) for u in tiles]
                dv_t = [_dot_tn(prb[u][0], doh[u][0]) + _dot_tn(prb[u][1], doh[u][1]) for u in tiles]
                for u in tiles:
                    start, prev = pos[u][0], pos[u][1]
                    _set_rows(dqn_s, start, d, _rows(dqn_s, start, d) + dq_t[u])
                    _set_rows(dkn_s, prev, d, _rows(dkn_s, prev, d) + dk_t[u][:BLOCK])
                    _set_rows(dkn_s, start, d, _rows(dkn_s, start, d) + dk_t[u][BLOCK:])
                    _set_rows(dvv_s, prev, d, _rows(dvv_s, prev, d) + dv_t[u][:BLOCK])
                    _set_rows(dvv_s, start, d, _rows(dvv_s, start, d) + dv_t[u][BLOCK:])
                return kc[-1], vc[-1]

            none_yet = jnp.zeros((BLOCK, LANES), BF16)
            lax.fori_loop(0, nt // BWD_TILES, step, (none_yet, none_yet))

        if gqa:
            @pl.when(p == 0)
            def _():
                dk_ref[...] = jnp.zeros_like(dk_ref)
                dv_ref[...] = jnp.zeros_like(dv_ref)

        def finish(c, carry):
            dgq, dgk = carry
            rows = pl.ds(pl.multiple_of(c * NORM_CH, NORM_CH), NORM_CH)
            lo = _lo_mask((NORM_CH, LANES))

            def norm_bwd(xv, dn, g_ref):
                r = _head_stats(xv, lo)
                gd = dn * g_ref[...]
                dot = _head_sum(gd * xv, lo) * (1.0 / HEAD_DIM)
                return r * gd - xv * (r * r * r * dot), dn * (xv * r)

            dq, gq_part = norm_bwd(q_ref[rows, :], dqn_s[rows, :], gq_ref)
            dq_ref[rows, :] = dq
            dgq = dgq + jnp.sum(gq_part, axis=0, keepdims=True)
            kv, dkn, dvv = k_ref[rows, :], dkn_s[rows, :], dvv_s[rows, :]
            if gqa:
                kv = _replicate_head(kv, lo, use_lo)
                dkn = dkn + _swap_halves(dkn)
                dvv = dvv + _swap_halves(dvv)
                lane = lax.broadcasted_iota(I32, (NORM_CH, LANES), 1)
                mine = (lane // HEAD_DIM) == (p // 2)
                dk, gk_part = norm_bwd(kv, dkn, gk_ref)
                dk_ref[rows, :] += jnp.where(mine, dk, 0.0)
                dv_ref[rows, :] += jnp.where(mine, dvv, 0.0)
                gk_part = jnp.where(lo, gk_part, 0.0)
            else:
                dk, gk_part = norm_bwd(kv, dkn, gk_ref)
                dk_ref[rows, :] = dk
                dv_ref[rows, :] = dvv
            dgk = dgk + jnp.sum(gk_part, axis=0, keepdims=True)
            return dgq, dgk

        z = jnp.zeros((1, LANES), F32)
        dgq, dgk = lax.fori_loop(0, nch, finish, (z, z))
        dgq_ref[...] = jnp.broadcast_to(dgq, (8, LANES))
        dgk_ref[...] = jnp.broadcast_to(dgk, (8, LANES))

    def slab_of(width_blocks, b0, shared):
        return pl.BlockSpec((s, LANES), (lambda p: (0, b0)) if shared else (lambda p: (0, b0 + p)),
                            pipeline_mode=pl.Buffered(1))

    vec = pl.BlockSpec((1, LANES), lambda p: (0, 0))
    own = pl.BlockSpec((s, LANES), lambda p: (0, p), pipeline_mode=pl.Buffered(1))
    in_specs = [slab_of(0, qb0, False), slab_of(0, kb0, gqa), slab_of(0, vb0, gqa), vec, vec,
                pl.BlockSpec((nbr, 2, 2, BLOCK, 2 * BLOCK), lambda p: (0, 0, p, 0, 0)), own, own, own]
    args = [proj, proj, proj, gq.reshape(1, LANES), gk.reshape(1, LANES), bias, dout, out, lse]
    if has_sink:
        in_specs.append(pl.BlockSpec((None, 1, LANES), lambda p: (p, 0, 0)))
        args.append(sinks)
    w = LANES * n_slabs
    kvw = LANES if gqa else w
    kv_spec = pl.BlockSpec((s, LANES), (lambda p: (0, 0)) if gqa else (lambda p: (0, p)))
    part = pl.BlockSpec((None, 8, LANES), lambda p: (p, 0, 0))
    out_specs = [pl.BlockSpec((s, LANES), lambda p: (0, p)), kv_spec, kv_spec,
                 pl.BlockSpec((nbr, 2, BLOCK, 2 * BLOCK), lambda p: (0, p, 0, 0)), part, part]
    out_shape = [jax.ShapeDtypeStruct((s, w), F32), jax.ShapeDtypeStruct((s, kvw), F32),
                 jax.ShapeDtypeStruct((s, kvw), F32),
                 jax.ShapeDtypeStruct((nbr, 2 * n_slabs, BLOCK, 2 * BLOCK), F32),
                 jax.ShapeDtypeStruct((n_slabs, 8, LANES), F32), jax.ShapeDtypeStruct((n_slabs, 8, LANES), F32)]
    if has_sink:
        out_specs.append(part)
        out_shape.append(jax.ShapeDtypeStruct((n_slabs, 8, LANES), F32))
    res = pl.pallas_call(
        body, name=name, grid=(n_slabs,),
        in_specs=in_specs, out_specs=out_specs, out_shape=out_shape,
        scratch_shapes=[pltpu.VMEM((s, LANES), F32) for _ in range(7)],
        compiler_params=_params(("arbitrary",), VMEM_LIMIT),
    )(*args)
    dq, dk, dv, db, dgq, dgk = res[:6]
    outs = [dq, dk, dv, db, dgq[:, 0, :], dgk[:, 0, :]]
    if has_sink:
        outs.append(res[6][:, 0, :])
    return outs


def bias_bwd(dbias, buckets, *, name):
    nbr, h = dbias.shape[:2]

    def body(db_ref, bk_ref, o_ref):
        lane = lax.broadcasted_iota(I32, (1, LANES), 1)
        acc = jnp.zeros((1, LANES), F32)
        for b in range(N_BUCKETS):
            tot = jnp.zeros((1, 1), F32)
            for br in range(nbr):
                sel = jnp.where(bk_ref[br] == b, db_ref[br], 0.0)
                tot = tot + jnp.sum(jnp.sum(sel, axis=0, keepdims=True), axis=1, keepdims=True)
            acc = jnp.where(lane == b, tot, acc)
        o_ref[...] = jnp.broadcast_to(acc, (8, LANES))

    res = pl.pallas_call(
        body, name=name, grid=(h,),
        in_specs=[pl.BlockSpec((nbr, None, BLOCK, 2 * BLOCK), lambda i: (0, i, 0, 0)),
                  pl.BlockSpec((nbr, BLOCK, 2 * BLOCK), lambda i: (0, 0, 0))],
        out_specs=pl.BlockSpec((None, 8, LANES), lambda i: (i, 0, 0)),
        out_shape=jax.ShapeDtypeStruct((h, 8, LANES), F32),
        compiler_params=_params(("parallel",)),
    )(dbias, buckets)
    return res[:, 0, :N_BUCKETS].T


SB_KG = 512
SB_QT = 2


def _softplus(z):
    return jnp.maximum(z, 0.0) + jnp.log(1.0 + jnp.exp(-jnp.abs(z)))


def _twice(t):
    t = t.astype(BF16)
    return jnp.concatenate([t, t], axis=0)


def _split_dot(x, t2):
    hi = x.astype(BF16)
    lo = (x - hi.astype(F32)).astype(BF16)
    return _dot(jnp.concatenate([hi, lo], axis=1), t2)


def sb_fwd(proj, qb0, kb0, vb0, n_slabs, *, name):
    s = proj.shape[0]
    nq = s // BLOCK
    nch = s // NORM_CH
    scale = HEAD_DIM ** -0.5

    def body(q_ref, k_ref, v_ref, o_ref, tot_ref, qlo_s, qhi_s, k_s, v_s):
        def prep(c, carry):
            rows = pl.ds(pl.multiple_of(c * NORM_CH, NORM_CH), NORM_CH)
            lo = _lo_mask((NORM_CH, LANES))
            qv = q_ref[rows, :] * scale
            qlo_s[rows, :] = jnp.where(lo, qv, 0.0).astype(BF16)
            qhi_s[rows, :] = jnp.where(lo, 0.0, qv).astype(BF16)
            k_s[rows, :] = k_ref[rows, :].astype(BF16)
            v_s[rows, :] = v_ref[rows, :].astype(BF16)
            return carry

        lax.fori_loop(0, nch, prep, 0)

        row = lax.broadcasted_iota(I32, (BLOCK, BLOCK), 0)
        col = lax.broadcasted_iota(I32, (BLOCK, BLOCK), 1)
        lo = col < HEAD_DIM
        t_ge = _twice(row >= col)
        rowg = lax.broadcasted_iota(I32, (BLOCK, SB_KG), 0)
        colg = lax.broadcasted_iota(I32, (BLOCK, SB_KG), 1)

        nsub = SB_KG // BLOCK
        chains = range(2 * SB_QT)
        nc = len(chains)

        def qloop(qs, carry):
            q0 = pl.multiple_of(qs * (SB_QT * BLOCK), SB_QT * BLOCK)
            qh = [(qlo_s, qhi_s)[i % 2][pl.ds(q0 + (i // 2) * BLOCK, BLOCK), :] for i in chains]
            gd = (qs * SB_QT) // nsub

            def logits(gi):
                k0 = pl.multiple_of(gi * SB_KG, SB_KG)
                kg = k_s[pl.ds(k0, SB_KG), :]
                return [_dot_nt(qh[i], kg) for i in chains]

            def group(gi, st, masks):
                k0 = pl.multiple_of(gi * SB_KG, SB_KG)
                vg = v_s[pl.ds(k0, SB_KG), :]
                c, o, z = list(st[:nc]), st[nc:2 * nc], st[2 * nc:]
                z_next = logits(jnp.maximum(gi - 1, 0))
                piece = lambda x, j: x[:, j * BLOCK:(j + 1) * BLOCK]
                a = [[None] * nsub for _ in chains]
                for j in reversed(range(nsub)):
                    zj = [piece(z[i], j) for i in chains]
                    lrem = [-_softplus(zj[i]) for i in chains]
                    if masks is not None:
                        lrem = [jnp.where(piece(masks[i // 2], j), lrem[i], 0.0) for i in chains]
                    incl = [_split_dot(lrem[i], t_ge) for i in chains]
                    for i in chains:
                        aij = jnp.exp(zj[i] + (c[i] + incl[i]))
                        if masks is not None:
                            aij = jnp.where(piece(masks[i // 2], j), aij, 0.0)
                        a[i][j] = aij.astype(BF16)
                        c[i] = c[i] + incl[i][:, 0:1]
                o = [o[i] + _dot(jnp.concatenate(a[i], axis=1), vg) for i in chains]
                return (*c, *o, *z_next)

            zc = [jnp.zeros((BLOCK, 1), F32)] * nc
            zo = [jnp.zeros((BLOCK, LANES), F32)] * nc
            masks = [(gd * SB_KG + colg) < (q0 + t * BLOCK + rowg) for t in range(SB_QT)]
            st = group(gd, (*zc, *zo, *logits(gd)), masks)
            st = lax.fori_loop(0, gd, lambda t, st: group(gd - 1 - t, st, None), st)
            for t in range(SB_QT):
                rows = pl.ds(q0 + t * BLOCK, BLOCK)
                o_ref[rows, :] = jnp.where(lo, st[nc + 2 * t], st[nc + 2 * t + 1])
                tot_ref[rows, :] = jnp.where(lo, st[2 * t], st[2 * t + 1])
            return carry

        lax.fori_loop(0, nq // SB_QT, qloop, 0)

    slab = lambda b0: pl.BlockSpec((s, LANES), lambda p: (0, b0 + p), pipeline_mode=pl.Buffered(1))
    w = LANES * n_slabs
    return pl.pallas_call(
        body, name=name, grid=(n_slabs,),
        in_specs=[slab(qb0), slab(kb0), slab(vb0)],
        out_specs=[pl.BlockSpec((s, LANES), lambda p: (0, p)), pl.BlockSpec((s, LANES), lambda p: (0, p))],
        out_shape=[jax.ShapeDtypeStruct((s, w), F32), jax.ShapeDtypeStruct((s, w), F32)],
        scratch_shapes=[pltpu.VMEM((s, LANES), BF16) for _ in range(4)],
        compiler_params=_params(("parallel",), VMEM_LIMIT),
    )(proj, proj, proj)


def sb_bwd(proj, qb0, kb0, vb0, n_slabs, dout, tot, *, name):
    s = proj.shape[0]
    nq = s // BLOCK
    nch = s // NORM_CH
    nsub = SB_KG // BLOCK
    scale = HEAD_DIM ** -0.5

    def body(q_ref, k_ref, v_ref, do_ref, tot_ref, dq_ref, dk_ref, dv_ref,
             qlo_s, qhi_s, k_s, v_s, dlo_s, dhi_s):
        def prep(c, carry):
            rows = pl.ds(pl.multiple_of(c * NORM_CH, NORM_CH), NORM_CH)
            lo = _lo_mask((NORM_CH, LANES))
            qv = q_ref[rows, :] * scale
            dv = do_ref[rows, :]
            qlo_s[rows, :] = jnp.where(lo, qv, 0.0).astype(BF16)
            qhi_s[rows, :] = jnp.where(lo, 0.0, qv).astype(BF16)
            dlo_s[rows, :] = jnp.where(lo, dv, 0.0).astype(BF16)
            dhi_s[rows, :] = jnp.where(lo, 0.0, dv).astype(BF16)
            k_s[rows, :] = k_ref[rows, :].astype(BF16)
            v_s[rows, :] = v_ref[rows, :].astype(BF16)
            z = jnp.zeros((NORM_CH, LANES), F32)
            dk_ref[rows, :] = z
            dv_ref[rows, :] = z
            return carry

        lax.fori_loop(0, nch, prep, 0)

        row = lax.broadcasted_iota(I32, (BLOCK, BLOCK), 0)
        col = lax.broadcasted_iota(I32, (BLOCK, BLOCK), 1)
        lo = col < HEAD_DIM
        t_le = _twice(row <= col)
        rowg = lax.broadcasted_iota(I32, (BLOCK, SB_KG), 0)
        colg = lax.broadcasted_iota(I32, (BLOCK, SB_KG), 1)

        piece = lambda x, j: x[:, j * BLOCK:(j + 1) * BLOCK]
        chains = range(2 * SB_QT)
        nc = len(chains)

        def prefixes(x):
            return [[_split_dot(piece(x[i], j), t_le) for j in range(nsub)] for i in chains]

        def chain(pre, run, total=None):
            out = []
            for j in range(nsub):
                out.append(run + pre[j] if total is None else total - run - pre[j])
                run = run + pre[j][:, BLOCK - 1:BLOCK]
            return jnp.concatenate(out, axis=1), run

        def qloop(qs, carry):
            q0 = pl.multiple_of(qs * (SB_QT * BLOCK), SB_QT * BLOCK)
            tile = lambda ref, i: ref[pl.ds(q0 + (i // 2) * BLOCK, BLOCK), :]
            qh = [tile((qlo_s, qhi_s)[i % 2], i) for i in chains]
            doh = [tile((dlo_s, dhi_s)[i % 2], i) for i in chains]
            tots = [tile(tot_ref, i)[:, (i % 2) * HEAD_DIM:(i % 2) * HEAD_DIM + 1] for i in chains]
            gd = (qs * SB_QT) // nsub

            def logits(gi):
                kg = k_s[pl.ds(pl.multiple_of(gi * SB_KG, SB_KG), SB_KG), :]
                return [_dot_nt(qh[i], kg) for i in chains]

            def group(gi, st, masks):
                k0 = pl.multiple_of(gi * SB_KG, SB_KG)
                kg, vg = k_s[pl.ds(k0, SB_KG), :], v_s[pl.ds(k0, SB_KG), :]
                cp, cg, dq, z = list(st[:nc]), list(st[nc:2 * nc]), st[2 * nc:2 * nc + SB_QT], st[2 * nc + SB_QT:]
                masked = lambda x, i: x if masks is None else jnp.where(masks[i // 2], x, 0.0)
                z_next = logits(jnp.minimum(gi + 1, gd))
                da = [_dot_nt(doh[i], vg) for i in chains]
                sp = [_softplus(z[i]) for i in chains]
                lrem = [masked(-sp[i], i) for i in chains]
                pre = prefixes(lrem)
                e, a, g = [], [], []
                for i in chains:
                    suffix, cp[i] = chain(pre[i], cp[i], tots[i])
                    e.append(z[i] - sp[i])
                    a.append(masked(jnp.exp(e[i] + suffix), i))
                    g.append(a[i] * da[i])
                gpre = prefixes(g)
                dz = []
                for i in chains:
                    ginc, cg[i] = chain(gpre[i], cg[i])
                    dz.append(masked(g[i] - jnp.exp(e[i]) * ginc, i).astype(BF16))
                ab = [a[i].astype(BF16) for i in chains]
                dq = [dq[t] + jnp.where(lo, _dot(dz[2 * t], kg), _dot(dz[2 * t + 1], kg)) for t in range(SB_QT)]
                dk_ref[pl.ds(k0, SB_KG), :] += functools.reduce(jnp.add, [_dot_tn(dz[i], qh[i]) for i in chains])
                dv_ref[pl.ds(k0, SB_KG), :] += functools.reduce(jnp.add, [_dot_tn(ab[i], doh[i]) for i in chains])
                return (*cp, *cg, *dq, *z_next)

            zc = [jnp.zeros((BLOCK, 1), F32)] * (2 * nc)
            zq = [jnp.zeros((BLOCK, LANES), F32)] * SB_QT
            st = lax.fori_loop(0, gd, lambda gi, st: group(gi, st, None), (*zc, *zq, *logits(0)))
            st = group(gd, st, [(gd * SB_KG + colg) < (q0 + t * BLOCK + rowg) for t in range(SB_QT)])
            for t in range(SB_QT):
                dq_ref[pl.ds(q0 + t * BLOCK, BLOCK), :] = st[2 * nc + t] * scale
            return carry

        lax.fori_loop(0, nq // SB_QT, qloop, 0)

    slab = lambda b0: pl.BlockSpec((s, LANES), lambda p: (0, b0 + p), pipeline_mode=pl.Buffered(1))
    own = pl.BlockSpec((s, LANES), lambda p: (0, p), pipeline_mode=pl.Buffered(1))
    w = LANES * n_slabs
    outb = pl.BlockSpec((s, LANES), lambda p: (0, p))
    return pl.pallas_call(
        body, name=name, grid=(n_slabs,),
        in_specs=[slab(qb0), slab(kb0), slab(vb0), own, own],
        out_specs=[outb, outb, outb],
        out_shape=[jax.ShapeDtypeStruct((s, w), F32)] * 3,
        scratch_shapes=[pltpu.VMEM((s, LANES), BF16) for _ in range(6)],
        compiler_params=_params(("parallel",), VMEM_LIMIT),
    )(proj, proj, proj, dout, tot)


def _place():
    x, y, c = lax.axis_index("x"), lax.axis_index("y"), lax.axis_index("c")
    return x, y, c


def gather_small(v, *, name):
    m_per, n = v.shape

    def body(x_ref, out_ref, send_sems, recv_sems, local_sem):
        x, y, c = _place()
        me, sibling = (x, y, c), (x, y, 1 - c)
        chips = [(1 - x, y), (x, 1 - y), (1 - x, 1 - y)]

        def rows(px, py, pc):
            return out_ref.at[pl.ds((4 * px + 2 * py + pc) * m_per, m_per), :]

        def copy(k, block, to, src=None):
            return pltpu.make_async_remote_copy(
                src_ref=rows(*block) if src is None else src, dst_ref=rows(*block),
                send_sem=send_sems.at[k], recv_sem=recv_sems.at[k], device_id=to, device_id_type=MESH)

        mine = pltpu.make_async_copy(x_ref, rows(*me), local_sem)
        mine.start()
        first = [copy(0, me, sibling, src=x_ref)]
        first += [copy(1 + j, me, (*chip, c), src=x_ref) for j, chip in enumerate(chips)]
        for cp in first:
            cp.start()
        passed = [copy(4 + j, (*chip, c), sibling) for j, chip in enumerate(chips)]
        for j, chip in enumerate(chips):
            copy(1 + j, (*chip, c), me).wait_recv()
            passed[j].start()
        copy(0, sibling, me).wait_recv()
        for j, chip in enumerate(chips):
            copy(4 + j, (*chip, 1 - c), me).wait_recv()
        for cp in first + passed:
            cp.wait_send()
        mine.wait()

    return pl.pallas_call(
        body, name=name,
        out_shape=jax.ShapeDtypeStruct((N_DEV * m_per, n), v.dtype),
        in_specs=[pl.BlockSpec(memory_space=pltpu.VMEM)],
        out_specs=pl.BlockSpec(memory_space=pltpu.VMEM),
        scratch_shapes=[pltpu.SemaphoreType.DMA((7,)), pltpu.SemaphoreType.DMA((7,)), pltpu.SemaphoreType.DMA],
        compiler_params=_params(None, VMEM_LIMIT),
    )(v)


_HBM = pl.BlockSpec(memory_space=pltpu.HBM)
_SEM = pl.BlockSpec(memory_space=pltpu.SEMAPHORE)
_EFFECT = pltpu.SideEffectType.DATAFLOW_SIDE_EFFECTING


def _peer_copies(src_refs, land_refs, send_sems, recv_sems, per_dest):
    x, y, c = _place()
    me = 4 * x + 2 * y + c
    copies = []
    for src, land, ssem, rsem in zip(src_refs, land_refs, send_sems, recv_sems):
        for k in (1, 2, 4, 3, 5, 6, 7):
            px, py, pc = x ^ (k >> 2 & 1), y ^ (k >> 1 & 1), c ^ (k & 1)
            copies.append(pltpu.make_async_remote_copy(
                src_ref=src.at[4 * px + 2 * py + pc] if per_dest else src, dst_ref=land.at[me],
                send_sem=ssem.at[k - 1], recv_sem=rsem.at[k - 1], device_id=(px, py, pc), device_id_type=MESH))
    return copies


def exchange_start(srcs, lands, per_dest, *, name):
    n = len(srcs)

    def body(*refs):
        src_refs, land_refs = refs[:n], refs[n:2 * n]
        send_sems, recv_sems = refs[2 * n:3 * n], refs[3 * n:4 * n]
        token = refs[-1]
        for cp in _peer_copies(src_refs, land_refs, send_sems, recv_sems, per_dest):
            cp.start()
        token[...] = jnp.zeros_like(token)

    hbm = lambda a: pltpu.HBM(a.shape, a.dtype)
    res = pl.pallas_call(
        body, name=name,
        out_shape=(*[pltpu.SemaphoreType.DMA((7,))] * (2 * n),
                   *[hbm(a) for a in srcs], *[hbm(a) for a in lands], jax.ShapeDtypeStruct((8, LANES), F32)),
        in_specs=[_HBM] * (2 * n),
        out_specs=(*[_SEM] * (2 * n), *[_HBM] * (2 * n), pl.BlockSpec(memory_space=pltpu.VMEM)),
        input_output_aliases={i: 2 * n + i for i in range(2 * n)},
        compiler_params=pltpu.CompilerParams(has_side_effects=_EFFECT),
    )(*[pltpu.with_memory_space_constraint(a, pltpu.HBM) for a in (*srcs, *lands)])
    handles = [(res[a], res[n + a], res[2 * n + a], res[3 * n + a]) for a in range(n)]
    return handles, res[-1]


def exchange_wait(handles, per_dest, after, *, name):
    n = len(handles)

    def body(*refs):
        src_refs, land_refs = refs[:n], refs[n:2 * n]
        send_sems, recv_sems = refs[2 * n:3 * n], refs[3 * n:4 * n]
        for cp in _peer_copies(src_refs, land_refs, send_sems, recv_sems, per_dest):
            cp.wait_send()
            cp.wait_recv()

    srcs, lands = [h[2] for h in handles], [h[3] for h in handles]
    hbm = lambda a: pltpu.HBM(a.shape, a.dtype)
    res = pl.pallas_call(
        body, name=name,
        out_shape=(*[hbm(a) for a in srcs], *[hbm(a) for a in lands]),
        in_specs=[*[_HBM] * (2 * n), *[_SEM] * (2 * n), pl.BlockSpec(memory_space=pl.ANY)],
        out_specs=tuple([_HBM] * (2 * n)),
        input_output_aliases={i: i for i in range(2 * n)},
        compiler_params=pltpu.CompilerParams(has_side_effects=_EFFECT),
    )(*srcs, *lands, *[h[0] for h in handles], *[h[1] for h in handles], after)
    return res[n:]


def _adamw_math(w, g, m, v):
    m = ADAM_B1 * m + (1.0 - ADAM_B1) * g
    v = ADAM_B2 * v + (1.0 - ADAM_B2) * (g * g)
    m_hat = m / (1.0 - ADAM_B1 ** ADAM_STEP)
    v_hat = v / (1.0 - ADAM_B2 ** ADAM_STEP)
    delta = -ADAM_LR * (m_hat / (jnp.sqrt(v_hat) + ADAM_EPS) + ADAM_WD * w)
    return delta, m, v


def adamw_parts(parts, w, m, v, layer, outs, *, name):
    depth, r, cdim = w.shape
    n_parts = parts.shape[0]
    tr = _pick(r, [t for t in (512, 256, 128, 64, 32, 16) if t * cdim <= 256 * 1024])

    def body(p_ref, w_ref, m_ref, v_ref, g0, d0, nm0, nv0, g_ref, d_ref, nm_ref, nv_ref):
        g = p_ref[0].astype(F32)
        for q in range(1, n_parts):
            g = g + p_ref[q].astype(F32)
        delta, nm, nv = _adamw_math(w_ref[...], g, m_ref[...], v_ref[...])
        g_ref[...], d_ref[...], nm_ref[...], nv_ref[...] = g, delta, nm, nv

    t = pl.BlockSpec((None, tr, cdim), lambda i: (layer, i, 0))
    held = pl.BlockSpec(memory_space=pl.ANY)
    return pl.pallas_call(
        body, name=name, grid=(r // tr,),
        in_specs=[pl.BlockSpec((n_parts, tr, cdim), lambda i: (0, i, 0)), t, t, t, held, held, held, held],
        out_specs=[t, t, t, t],
        out_shape=[jax.ShapeDtypeStruct((depth, r, cdim), F32)] * 4,
        input_output_aliases={4: 0, 5: 1, 6: 2, 7: 3},
        compiler_params=_params(("parallel",), VMEM_LIMIT),
    )(parts, w, m, v, *outs)


def sum_devices(gathered, *, name):
    m_rows = gathered.shape[1]

    def body(ga_ref, g_ref):
        g = ga_ref[0]
        for dev in range(1, N_DEV):
            g = g + ga_ref[dev]
        g_ref[...] = g

    return pl.pallas_call(
        body, name=name, out_shape=jax.ShapeDtypeStruct((m_rows, LANES), F32),
        compiler_params=_params(None, VMEM_LIMIT),
    )(gathered)


def adamw_small(g, w, m, v, *, name):
    m_rows = w.shape[0]

    def body(g_ref, w_ref, m_ref, v_ref, d_ref, nm_ref, nv_ref):
        d_ref[...], nm_ref[...], nv_ref[...] = _adamw_math(w_ref[...], g_ref[...], m_ref[...], v_ref[...])

    return pl.pallas_call(
        body, name=name, out_shape=[jax.ShapeDtypeStruct((m_rows, LANES), F32)] * 3,
        compiler_params=_params(None, VMEM_LIMIT),
    )(g, w, m, v)


def _t5_bucket(dist):
    max_exact = N_BUCKETS // 2
    d = jnp.maximum(dist, 0)
    large = max_exact + (jnp.log(jnp.maximum(d, 1).astype(F32) / max_exact)
                         / math.log(T5_MAX_DIST / max_exact) * (N_BUCKETS - max_exact)).astype(I32)
    large = jnp.minimum(large, N_BUCKETS - 1)
    return jnp.where(d < max_exact, d, large)


def _rel():
    return jnp.arange(BLOCK)[:, None] + BLOCK - jnp.arange(2 * BLOCK)[None, :]


def _band_bias(table, dils, max_dists):
    rel = _rel()
    biases, buckets = [], []
    for d, md in zip(dils, max_dists):
        bk = _t5_bucket(rel * d)
        vis = (rel >= 0) & (rel <= md)
        looked_up = jnp.zeros((table.shape[1],) + rel.shape, F32)
        for b in range(N_BUCKETS):
            looked_up = jnp.where((bk == b)[None], table[b][:, None, None], looked_up)
        with_prev = jnp.where(vis[None], looked_up, NEG_INF)
        first = jnp.arange(2 * BLOCK)[None, None, :] >= BLOCK
        biases.append(jnp.stack([with_prev, jnp.where(first, with_prev, NEG_INF)]))
        buckets.append(bk.astype(I32))
    return jnp.stack(biases), jnp.stack(buckets)


def _pack(pieces, rows):
    flat = jnp.concatenate([p.reshape(-1) for p in pieces])
    return jnp.pad(flat, (0, rows * LANES - flat.shape[0])).reshape(rows, LANES)


def _unpack(packed, shapes):
    flat = packed.reshape(-1)
    out, off = [], 0
    for sh in shapes:
        n = math.prod(sh)
        out.append(flat[off:off + n].reshape(sh))
        off += n
    return out


def _tile2(g):
    return jnp.concatenate([g, g])


def kernel(x, attn_norm, w_in, a_q_gain, a_k_gain, a_sinks, c_q_gain, c_k_gain, rel_bias_table, mix_out_gain, w_out, ffn_norm, w_up, conv_w, conv_b, w_down, loss_target, m_attn_norm, m_w_in, m_a_q_gain, m_a_k_gain, m_a_sinks, m_c_q_gain, m_c_k_gain, m_rel_bias_table, m_mix_out_gain, m_w_out, m_ffn_norm, m_w_up, m_conv_w, m_conv_b, m_w_down, v_attn_norm, v_w_in, v_a_q_gain, v_a_k_gain, v_a_sinks, v_c_q_gain, v_c_k_gain, v_rel_bias_table, v_mix_out_gain, v_w_out, v_ffn_norm, v_w_up, v_conv_w, v_conv_b, v_w_down):
    depth, d_model, in_shard = w_in.shape
    ff2_shard = w_up.shape[2]
    s = x.shape[1]
    in_width, ff2 = N_DEV * in_shard, N_DEV * ff2_shard
    n_heads = d_model // HEAD_DIM
    ha, hb, hc = n_heads // 4, n_heads // 4, n_heads // 2
    sa, sb, sc = ha // 2, hb // 2, hc // 2
    kv_a = ha // 4
    assert kv_a == 2 and BLOCK == LANES
    cb_aq, cb_ak, cb_av = 0, sa, sa + 1
    cb_bq = sa + 2
    cb_bk, cb_bv = cb_bq + sb, cb_bq + 2 * sb
    cb_cq = cb_bq + 3 * sb
    cb_ck, cb_cv = cb_cq + sc, cb_cq + 2 * sc
    assert (cb_cv + sc) * LANES == in_width
    dev = 4 * lax.axis_index("x") + 2 * lax.axis_index("y") + lax.axis_index("c")

    def landing(own):
        return lax.dynamic_update_slice_in_dim(lax.empty((N_DEV,) + own.shape, own.dtype), own[None], dev, axis=0)

    per_array = 2
    wnames = ("w_in", "w_out", "w_up", "w_down", "conv_w")
    cols_to_rows = lambda g: jnp.transpose(g, (1, 0, 2)).reshape(g.shape[1], N_DEV * g.shape[2])
    whole = dict(w_in=cols_to_rows, w_up=cols_to_rows, conv_w=cols_to_rows,
                 w_out=lambda g: g.reshape(d_model, d_model), w_down=lambda g: g.reshape(ff2 // 2, d_model))
    gathers = {}
    token = jnp.zeros((8, LANES), F32)
    for l in range(depth):
        for gi, group in enumerate([[n] for n in wnames] if l < per_array else [wnames]):
            srcs = [(dict(w_in=w_in, w_out=w_out, w_up=w_up, w_down=w_down, conv_w=conv_w)[n][l] + token[0, 0])
                    .astype(F32 if n == "conv_w" else BF16) for n in group]
            handles, token = exchange_start(srcs, [landing(a) for a in srcs], False, name=f"gather_start_{l}_{gi}")
            gathers.update({(l, n): h for n, h in zip(group, handles)})

    def gathered(l, names, after):
        landed = exchange_wait([gathers[l, n] for n in names], False, after,
                               name=f"gather_wait_{l}_{wnames.index(names[0])}")
        return {n: whole[n](g) for n, g in zip(names, landed)}

    bias_a, buckets_a = _band_bias(rel_bias_table[:, :ha], (1,), (WINDOW_A - 1,))
    bias_c, buckets_c = _band_bias(rel_bias_table[:, ha:], DILATIONS, (BLOCK,) * len(DILATIONS))

    xs = x[0]
    saved = []
    wi, wo, wu, wd, cw = ([None] * depth for _ in range(5))
    for l in range(depth):
        if l < per_array:
            need = lambda n, after, l=l: gathered(l, (n,), after)[n]
        else:
            layer_w = gathered(l, wnames, xs)
            need = lambda n, after: layer_w[n]
        wi[l] = need("w_in", token if l == 0 else xs)
        h1 = rmsnorm_fwd(xs, attn_norm[l], name="attn_norm_fwd")
        proj = matmul(h1, wi[l], name="in_proj")
        sinks = jnp.repeat(a_sinks[l], HEAD_DIM).reshape(sa, 1, LANES)
        gaq, gak = _tile2(a_q_gain[l]), _tile2(a_k_gain[l])
        gcq, gck = _tile2(c_q_gain[l]), _tile2(c_k_gain[l])
        out_a, lse_a = banded_fwd(proj, cb_aq, cb_ak, cb_av, sa, gaq, gak, bias_a, (1,), sinks, True, name="swa_fwd")
        out_b, tot_b = sb_fwd(proj, cb_bq, cb_bk, cb_bv, sb, name="stick_fwd")
        out_c, lse_c = banded_fwd(proj, cb_cq, cb_ck, cb_cv, sc, gcq, gck, bias_c, DILATIONS, None, False,
                                  name="dilated_fwd")
        mix = mixnorm_fwd([out_a, out_b, out_c], mix_out_gain[l], name="mix_norm_fwd")
        wo[l] = need("w_out", mix)
        x_mid = matmul(mix, wo[l], res=xs, name="out_proj")
        h2 = rmsnorm_fwd(x_mid, ffn_norm[l], name="ffn_norm_fwd")
        wu[l] = need("w_up", h2)
        p = matmul(h2, wu[l], name="up_proj")
        cw[l] = need("conv_w", p)
        act = ffn_act_fwd(p, cw[l], conv_b[l], name="ffn_act_fwd")
        wd[l] = need("w_down", act)
        x_out = matmul(act, wd[l], res=x_mid, name="down_proj")
        saved.append(dict(x_in=xs, h1=h1, proj=proj, out_a=out_a, lse_a=lse_a, out_b=out_b, tot_b=tot_b,
                          out_c=out_c, lse_c=lse_c, mix=mix, x_mid=x_mid, h2=h2, p=p, act=act,
                          sinks=sinks, gains=(gaq, gak, gcq, gck)))
        xs = x_out

    dx, dx_b, loss_part = loss_head(xs, loss_target[0], name="loss_head")

    small = {k: [None] * depth for k in ("attn_norm", "a_q_gain", "a_k_gain", "a_sinks", "c_q_gain", "c_k_gain",
                                         "mix_out_gain", "ffn_norm", "conv_w", "conv_b")}
    big = {k: [None] * depth for k in ("w_in", "w_out", "w_up", "w_down")}
    dbias_a = dbias_c = None
    scatters = {}
    token = jnp.zeros((8, LANES), F32)
    names_big = ("w_in", "w_out", "w_up", "w_down")

    def scatter(l, names):
        parts = [big[n][l] for n in names]
        lands = [landing(lax.dynamic_index_in_dim(pt, dev, axis=0, keepdims=False)) for pt in parts]
        handles, tok = exchange_start(parts, lands, True, name=f"scatter_start_{l}_{names_big.index(names[0])}")
        scatters.update({(l, n): h for n, h in zip(names, handles)})
        return tok

    by_cols = lambda a: jnp.transpose(a.reshape(a.shape[0], N_DEV, a.shape[1] // N_DEV), (1, 0, 2))
    by_rows = lambda a: a.reshape(N_DEV, a.shape[0] // N_DEV, a.shape[1])
    for l in reversed(range(depth)):
        each = l == 0
        sv = saved[l]
        gaq, gak, gcq, gck = sv["gains"]
        da = matmul(dx_b, wd[l], trans_b=True, name="down_proj_dx")
        big["w_down"][l] = by_rows(matmul(sv["act"], dx_b, trans_a=True, out_dtype=BF16, name="down_proj_dw"))
        if each:
            token = scatter(l, ("w_down",))
        dp, small["conv_w"][l], small["conv_b"][l] = ffn_act_bwd(da, sv["p"], cw[l], conv_b[l] + token[0, 0],
                                                                 name="ffn_act_bwd")
        dh2 = matmul(dp, wu[l], trans_b=True, name="up_proj_dx")
        big["w_up"][l] = matmul(sv["h2"], dp, trans_a=True, out_dtype=BF16, col_blocks=N_DEV, name="up_proj_dw")
        if each:
            token = scatter(l, ("w_up",))
        dx_mid, dx_mid_b, small["ffn_norm"][l] = rmsnorm_bwd(dh2, sv["x_mid"], ffn_norm[l] + token[0, 0], dx,
                                                   name="ffn_norm_bwd")
        dmix = matmul(dx_mid_b, wo[l], trans_b=True, name="out_proj_dx")
        big["w_out"][l] = by_rows(matmul(sv["mix"], dx_mid_b, trans_a=True, out_dtype=BF16, name="out_proj_dw"))
        if each:
            token = scatter(l, ("w_out",))
        (d_oa, d_ob, d_oc), small["mix_out_gain"][l] = mixnorm_bwd(
            dmix, [sv["out_a"], sv["out_b"], sv["out_c"]], mix_out_gain[l] + token[0, 0], name="mix_norm_bwd")
        dq_a, dk_a, dv_a, db_a, dgq_a, dgk_a, dsink = banded_bwd(
            sv["proj"], cb_aq, cb_ak, cb_av, sa, gaq, gak, bias_a, (1,), sv["sinks"], True,
            d_oa, sv["out_a"], sv["lse_a"], name="swa_bwd")
        dq_b, dk_b, dv_b = sb_bwd(sv["proj"], cb_bq, cb_bk, cb_bv, sb, d_ob, sv["tot_b"], name="stick_bwd")
        dq_c, dk_c, dv_c, db_c, dgq_c, dgk_c = banded_bwd(
            sv["proj"], cb_cq, cb_ck, cb_cv, sc, gcq, gck, bias_c, DILATIONS, None, False,
            d_oc, sv["out_c"], sv["lse_c"], name="dilated_bwd")
        fold = lambda g: g.reshape(-1, HEAD_DIM).sum(axis=0)
        small["a_q_gain"][l], small["a_k_gain"][l] = fold(dgq_a), fold(dgk_a)
        small["c_q_gain"][l], small["c_k_gain"][l] = fold(dgq_c), fold(dgk_c)
        small["a_sinks"][l] = dsink[:, ::HEAD_DIM].reshape(-1)
        dbias_a = db_a if dbias_a is None else dbias_a + db_a
        dbias_c = db_c if dbias_c is None else dbias_c + db_c
        dproj = jnp.concatenate([dq_a, dk_a, dv_a, dq_b, dk_b, dv_b, dq_c, dk_c, dv_c], axis=1).astype(BF16)
        big["w_in"][l] = by_cols(matmul(sv["h1"], dproj, trans_a=True, out_dtype=BF16, name="in_proj_dw"))
        if not each:
            token = scatter(l, names_big)
        dh1 = matmul(dproj, wi[l], trans_b=True, name="in_proj_dx")
        dx, dx_b, small["attn_norm"][l] = rmsnorm_bwd(dh1, sv["x_in"], attn_norm[l] + token[0, 0], dx_mid,
                                                name="attn_norm_bwd")

    dtable = jnp.concatenate([bias_bwd(dbias_a, buckets_a, name="swa_bias_bwd"),
                              bias_bwd(dbias_c, buckets_c, name="dilated_bias_bwd")], axis=1)

    order = ("attn_norm", "a_q_gain", "a_k_gain", "a_sinks", "c_q_gain", "c_k_gain", "rel_bias_table",
             "mix_out_gain", "ffn_norm", "conv_w", "conv_b")
    partial = {k: jnp.stack(v) for k, v in small.items()}
    partial["rel_bias_table"] = dtable
    pieces = [partial[k] for k in order] + [loss_part.reshape(1)]
    n_small = sum(math.prod(pc.shape) for pc in pieces)
    rows = -(-n_small // (8 * LANES)) * 8
    gathered = gather_small(_pack(pieces, rows), name="gather_small_grads")
    big["w_in"][0], gathered = lax.optimization_barrier((big["w_in"][0], gathered))
    scatter(0, ("w_in",))
    summed = _unpack(sum_devices(gathered.reshape(N_DEV, rows, LANES), name="sum_small_grads"),
                     [pc.shape for pc in pieces])
    g_small = dict(zip(order, summed[:-1]))
    loss = summed[-1][0]
    g_small["conv_w"] = lax.dynamic_slice_in_dim(g_small["conv_w"], dev * ff2_shard, ff2_shard, axis=2)

    w_small = dict(attn_norm=attn_norm, a_q_gain=a_q_gain, a_k_gain=a_k_gain, a_sinks=a_sinks, c_q_gain=c_q_gain,
                   c_k_gain=c_k_gain, rel_bias_table=rel_bias_table, mix_out_gain=mix_out_gain, ffn_norm=ffn_norm,
                   conv_w=conv_w, conv_b=conv_b)
    m_small = dict(attn_norm=m_attn_norm, a_q_gain=m_a_q_gain, a_k_gain=m_a_k_gain, a_sinks=m_a_sinks,
                   c_q_gain=m_c_q_gain, c_k_gain=m_c_k_gain, rel_bias_table=m_rel_bias_table,
                   mix_out_gain=m_mix_out_gain, ffn_norm=m_ffn_norm, conv_w=m_conv_w, conv_b=m_conv_b)
    v_small = dict(attn_norm=v_attn_norm, a_q_gain=v_a_q_gain, a_k_gain=v_a_k_gain, a_sinks=v_a_sinks,
                   c_q_gain=v_c_q_gain, c_k_gain=v_c_k_gain, rel_bias_table=v_rel_bias_table,
                   mix_out_gain=v_mix_out_gain, ffn_norm=v_ffn_norm, conv_w=v_conv_w, conv_b=v_conv_b)
    shapes = [w_small[k].shape for k in order]
    n_upd = sum(math.prod(sh) for sh in shapes)
    urows = -(-n_upd // (8 * LANES)) * 8
    packs = [_pack([d[k] for k in order], urows) for d in (g_small, w_small, m_small, v_small)]
    upd = adamw_small(*packs, name="adamw_small")
    delta_s, newm_s, newv_s = [dict(zip(order, _unpack(u, shapes))) for u in upd]

    w_big = dict(w_in=(w_in, m_w_in, v_w_in), w_out=(w_out, m_w_out, v_w_out), w_up=(w_up, m_w_up, v_w_up),
                 w_down=(w_down, m_w_down, v_w_down))
    results = {k: [lax.empty(w_big[k][0].shape, F32) for _ in range(4)] for k in names_big}
    after = upd[0]
    for l, names in [(l, names_big) for l in reversed(range(1, depth))] + [(0, names_big[1:]), (0, names_big[:1])]:
        landed = exchange_wait([scatters[l, n] for n in names], True, after,
                               name=f"scatter_wait_{l}_{names_big.index(names[0])}")
        for k, parts in zip(names, landed):
            results[k] = adamw_parts(parts, *w_big[k], l, results[k], name="adamw_large")
        after = results[names[-1]][0]
    g_big, delta_b, newm_b, newv_b = [{k: results[k][i] for k in names_big} for i in range(4)]

    all_names = ("attn_norm", "w_in", "a_q_gain", "a_k_gain", "a_sinks", "c_q_gain", "c_k_gain", "rel_bias_table",
                 "mix_out_gain", "w_out", "ffn_norm", "w_up", "conv_w", "conv_b", "w_down")
    pick = lambda sm, bg: [bg[k] if k in bg else sm[k] for k in all_names]
    return (loss, dx[None], *pick(g_small, g_big), *pick(delta_s, delta_b), *pick(newm_s, newm_b),
            *pick(newv_s, newv_b))
```

```python
import functools
import math

import jax
import jax.numpy as jnp
from jax import lax
from jax.experimental import pallas as pl
from jax.experimental.pallas import tpu as pltpu

F32, BF16, I32 = jnp.float32, jnp.bfloat16, jnp.int32
MESH = pl.DeviceIdType.MESH

HEAD_DIM = 64
LANES = 128
BLOCK = 128
EPS = 1e-6
NEG_INF = -1e30
N_BUCKETS = 32
T5_MAX_DIST = 2048
WINDOW_A = 128
DILATIONS = (1, 4, 16)
N_DEV = 8
VMEM_LIMIT = 56 * 1024 * 1024
MATMUL_VMEM = 46 * 1024 * 1024

ADAM_LR, ADAM_B1, ADAM_B2, ADAM_EPS, ADAM_WD, ADAM_STEP = 0.001, 0.9, 0.999, 1e-08, 0.01, 10


def _params(sem=None, vmem=None):
    return pltpu.CompilerParams(dimension_semantics=sem, vmem_limit_bytes=vmem)


def _pick(n, cands):
    for c in cands:
        if n % c == 0:
            return c
    raise ValueError(f"no tile for {n}")


def _dot(a, b):
    return lax.dot_general(a, b, (((1,), (0,)), ((), ())), preferred_element_type=F32)


def _dot_nt(a, b):
    return lax.dot_general(a, b, (((1,), (1,)), ((), ())), preferred_element_type=F32)


def _dot_tn(a, b):
    return lax.dot_general(a, b, (((0,), (0,)), ((), ())), preferred_element_type=F32)


def matmul(a, b, *, trans_a=False, trans_b=False, out_dtype=F32, res=None, col_blocks=None, name):
    m, k = (a.shape[1], a.shape[0]) if trans_a else a.shape
    n = b.shape[0] if trans_b else b.shape[1]
    tm = _pick(m, (1408, 1024, 512, 256))
    tn_cands = (n // col_blocks,) if col_blocks else tuple(t for t in (1408, 1024, 768, 512, 256, 128) if n % t == 0)

    def footprint(tk, tn):
        tiles = 2 * (tm * tk * a.dtype.itemsize + tk * tn * b.dtype.itemsize)
        return tiles + tm * tn * (4 + 2 * jnp.dtype(out_dtype).itemsize + (8 if res is not None else 0))

    tk, tn = next((tk, tn) for tk in (2816, 2048, 1792, 1024, 768, 512, 256) if k % tk == 0
                  for tn in tn_cands if footprint(tk, tn) <= MATMUL_VMEM)
    nk = k // tk
    dn = (((0 if trans_a else 1,), (1 if trans_b else 0,)), ((), ()))

    def body(*refs):
        if res is None:
            a_ref, b_ref, o_ref, acc = refs
        else:
            a_ref, b_ref, r_ref, o_ref, acc = refs
        kk = pl.program_id(2)

        @pl.when(kk == 0)
        def _():
            acc[...] = jnp.zeros_like(acc)

        acc[...] += lax.dot_general(a_ref[...].astype(BF16), b_ref[...].astype(BF16), dn,
                                    preferred_element_type=F32)

        @pl.when(kk == nk - 1)
        def _():
            r = acc[...]
            if res is not None:
                r = r_ref[...] + r
            o_ref[...] = r.astype(out_dtype)

    b_spec = (pl.BlockSpec((tn, tk), lambda i, j, kk: (j, kk)) if trans_b
              else pl.BlockSpec((tk, tn), lambda i, j, kk: (kk, j)))
    a_spec = (pl.BlockSpec((tk, tm), lambda i, j, kk: (kk, i)) if trans_a
              else pl.BlockSpec((tm, tk), lambda i, j, kk: (i, kk)))
    in_specs = [a_spec, b_spec]
    args = [a, b]
    if res is not None:
        in_specs.append(pl.BlockSpec((tm, tn), lambda i, j, kk: (i, j)))
        args.append(res)
    if col_blocks:
        out_spec = pl.BlockSpec((None, tm, tn), lambda i, j, kk: (j, i, 0))
        out_shape = jax.ShapeDtypeStruct((col_blocks, m, tn), out_dtype)
    else:
        out_spec = pl.BlockSpec((tm, tn), lambda i, j, kk: (i, j))
        out_shape = jax.ShapeDtypeStruct((m, n), out_dtype)
    return pl.pallas_call(
        body, name=name, grid=(m // tm, n // tn, nk),
        in_specs=in_specs, out_specs=out_spec, out_shape=out_shape,
        scratch_shapes=[pltpu.VMEM((tm, tn), F32)],
        compiler_params=_params(("parallel", "parallel", "arbitrary"), VMEM_LIMIT),
    )(*args)


def rmsnorm_fwd(x, g, *, name):
    s, d = x.shape
    tm = 512

    def body(x_ref, g_ref, o_ref):
        xv = x_ref[...]
        r = lax.rsqrt(jnp.mean(xv * xv, axis=-1, keepdims=True) + EPS)
        o_ref[...] = (xv * r * g_ref[...]).astype(BF16)

    return pl.pallas_call(
        body, name=name, grid=(s // tm,),
        in_specs=[pl.BlockSpec((tm, d), lambda i: (i, 0)), pl.BlockSpec((1, d), lambda i: (0, 0))],
        out_specs=pl.BlockSpec((tm, d), lambda i: (i, 0)),
        out_shape=jax.ShapeDtypeStruct((s, d), BF16),
        compiler_params=_params(("parallel",)),
    )(x, g.reshape(1, d))


def rmsnorm_bwd(dh, x, g, dres, *, name):
    s, d = x.shape
    tm = 256

    def body(dh_ref, x_ref, g_ref, dres_ref, dx_ref, dxb_ref, dg_ref):
        @pl.when(pl.program_id(0) == 0)
        def _():
            dg_ref[...] = jnp.zeros_like(dg_ref)

        xv, dhv = x_ref[...], dh_ref[...]
        r = lax.rsqrt(jnp.mean(xv * xv, axis=-1, keepdims=True) + EPS)
        gd = dhv * g_ref[...]
        dot = jnp.mean(gd * xv, axis=-1, keepdims=True)
        dx = dres_ref[...] + (r * gd - xv * (r * r * r * dot))
        dx_ref[...] = dx
        dxb_ref[...] = dx.astype(BF16)
        dg_ref[...] += jnp.sum(dhv * (xv * r), axis=0, keepdims=True)

    row = pl.BlockSpec((tm, d), lambda i: (i, 0))
    dx, dxb, dg = pl.pallas_call(
        body, name=name, grid=(s // tm,),
        in_specs=[row, row, pl.BlockSpec((1, d), lambda i: (0, 0)), row],
        out_specs=[row, row, pl.BlockSpec((1, d), lambda i: (0, 0))],
        out_shape=[jax.ShapeDtypeStruct((s, d), F32), jax.ShapeDtypeStruct((s, d), BF16),
                   jax.ShapeDtypeStruct((1, d), F32)],
        compiler_params=_params(("arbitrary",)),
    )(dh, x, g.reshape(1, d), dres)
    return dx, dxb, dg[0]


def loss_head(y, target, *, name):
    s, d = y.shape
    tm = 512

    def body(y_ref, t_ref, dy_ref, dyb_ref, l_ref):
        @pl.when(pl.program_id(0) == 0)
        def _():
            l_ref[...] = jnp.zeros_like(l_ref)

        e = y_ref[...] - t_ref[...]
        dy = e / float(d)
        dy_ref[...] = dy
        dyb_ref[...] = dy.astype(BF16)
        per_tok = jnp.mean(e * e, axis=-1, keepdims=True)
        l_ref[...] += 0.5 * jnp.sum(per_tok, axis=0, keepdims=True)

    row = pl.BlockSpec((tm, d), lambda i: (i, 0))
    dy, dyb, l = pl.pallas_call(
        body, name=name, grid=(s // tm,),
        in_specs=[row, row],
        out_specs=[row, row, pl.BlockSpec((8, LANES), lambda i: (0, 0))],
        out_shape=[jax.ShapeDtypeStruct((s, d), F32), jax.ShapeDtypeStruct((s, d), BF16),
                   jax.ShapeDtypeStruct((8, LANES), F32)],
        compiler_params=_params(("arbitrary",)),
    )(y, target)
    return dy, dyb, l[0, 0]


FFN_TN = 256
FFN_CH = 256


def _rows_before(ref, r0, first):
    if first:
        cur = ref[pl.ds(0, FFN_CH), :]
        row = lax.broadcasted_iota(I32, cur.shape, 0)
        sh1 = jnp.where(row < 1, 0.0, pltpu.roll(cur, 1, axis=0))
        sh2 = jnp.where(row < 2, 0.0, pltpu.roll(cur, 2, axis=0))
        return cur, sh1, sh2
    ext = ref[pl.ds(pl.multiple_of(r0 - 8, 8), FFN_CH + 8), :]
    return ext[8:], pltpu.roll(ext, 1, axis=0)[8:], pltpu.roll(ext, 2, axis=0)[8:]


def _rows_after(ref, r0, last):
    if last:
        cur = ref[pl.ds(r0, FFN_CH), :]
        row = lax.broadcasted_iota(I32, cur.shape, 0)
        up1 = jnp.where(row >= FFN_CH - 1, 0.0, pltpu.roll(cur, FFN_CH - 1, axis=0))
        up2 = jnp.where(row >= FFN_CH - 2, 0.0, pltpu.roll(cur, FFN_CH - 2, axis=0))
        return cur, up1, up2
    n = FFN_CH + 8
    ext = ref[pl.ds(r0, n), :]
    return ext[:FFN_CH], pltpu.roll(ext, n - 1, axis=0)[:FFN_CH], pltpu.roll(ext, n - 2, axis=0)[:FFN_CH]


def _sigmoid(x):
    return 0.5 * jnp.tanh(0.5 * x) + 0.5


def ffn_act_fwd(p, conv_w, conv_b, *, name):
    s, f2 = p.shape
    f = f2 // 2
    nj = f // FFN_TN
    nch = s // FFN_CH

    def body(pg_ref, pu_ref, wg_ref, wu_ref, bg_ref, bu_ref, a_ref):
        def conv(ref, w_ref, b_ref, r0, first):
            cur, sh1, sh2 = _rows_before(ref, r0, first)
            return ((b_ref[...] + w_ref[0:1, :] * sh2) + w_ref[1:2, :] * sh1) + w_ref[2:3, :] * cur

        def chunk(r0, first):
            gate = conv(pg_ref, wg_ref, bg_ref, r0, first)
            up = conv(pu_ref, wu_ref, bu_ref, r0, first)
            a_ref[pl.ds(r0, FFN_CH), :] = (gate * _sigmoid(gate) * up).astype(BF16)

        chunk(0, True)

        def step(c, carry):
            chunk(pl.multiple_of(c * FFN_CH, FFN_CH), False)
            return carry

        lax.fori_loop(1, nch, step, 0)

    col = lambda off: pl.BlockSpec((s, FFN_TN), lambda j: (0, j + off))
    wcol = lambda off: pl.BlockSpec((3, FFN_TN), lambda j: (0, j + off))
    bcol = lambda off: pl.BlockSpec((1, FFN_TN), lambda j: (0, j + off))
    return pl.pallas_call(
        body, name=name, grid=(nj,),
        in_specs=[col(0), col(nj), wcol(0), wcol(nj), bcol(0), bcol(nj)],
        out_specs=pl.BlockSpec((s, FFN_TN), lambda j: (0, j)),
        out_shape=jax.ShapeDtypeStruct((s, f), BF16),
        compiler_params=_params(("parallel",), VMEM_LIMIT),
    )(p, p, conv_w, conv_w, conv_b.reshape(1, f2), conv_b.reshape(1, f2))


def ffn_act_bwd(da, p, conv_w, conv_b, *, name):
    s, f2 = p.shape
    f = f2 // 2
    nj = f // FFN_TN
    nch = s // FFN_CH

    def body(da_ref, pg_ref, pu_ref, wg_ref, wu_ref, bg_ref, bu_ref,
             dpg_ref, dpu_ref, dwg_ref, dwu_ref, dbg_ref, dbu_ref, dug_s, duu_s):
        def conv(ref, w_ref, b_ref, r0, first):
            cur, sh1, sh2 = _rows_before(ref, r0, first)
            u = ((b_ref[...] + w_ref[0:1, :] * sh2) + w_ref[1:2, :] * sh1) + w_ref[2:3, :] * cur
            return u, (sh2, sh1, cur)

        def taps_sum(du, taps):
            return jnp.concatenate([jnp.sum(du * t, axis=0, keepdims=True) for t in taps], axis=0)

        def chunk(r0, first, acc):
            dwg, dwu, dbg, dbu = acc
            gate, tg = conv(pg_ref, wg_ref, bg_ref, r0, first)
            up, tu = conv(pu_ref, wu_ref, bu_ref, r0, first)
            dav = da_ref[pl.ds(r0, FFN_CH), :]
            sg = _sigmoid(gate)
            dgate = dav * up * (sg * (1.0 + gate * (1.0 - sg)))
            dup = dav * (gate * sg)
            dug_s[pl.ds(r0, FFN_CH), :] = dgate
            duu_s[pl.ds(r0, FFN_CH), :] = dup
            return (dwg + taps_sum(dgate, tg), dwu + taps_sum(dup, tu),
                    dbg + jnp.sum(dgate, axis=0, keepdims=True), dbu + jnp.sum(dup, axis=0, keepdims=True))

        z3 = jnp.zeros((3, FFN_TN), F32)
        z1 = jnp.zeros((1, FFN_TN), F32)
        acc = chunk(0, True, (z3, z3, z1, z1))
        acc = lax.fori_loop(1, nch, lambda c, a: chunk(pl.multiple_of(c * FFN_CH, FFN_CH), False, a), acc)
        dwg_ref[...], dwu_ref[...], dbg_ref[...], dbu_ref[...] = acc

        def back(src, w_ref, dst, r0, last):
            cur, up1, up2 = _rows_after(src, r0, last)
            dst[pl.ds(r0, FFN_CH), :] = (w_ref[2:3, :] * cur + w_ref[1:2, :] * up1 + w_ref[0:1, :] * up2).astype(BF16)

        def step(c, carry):
            r0 = pl.multiple_of(c * FFN_CH, FFN_CH)
            back(dug_s, wg_ref, dpg_ref, r0, False)
            back(duu_s, wu_ref, dpu_ref, r0, False)
            return carry

        lax.fori_loop(0, nch - 1, step, 0)
        back(dug_s, wg_ref, dpg_ref, (nch - 1) * FFN_CH, True)
        back(duu_s, wu_ref, dpu_ref, (nch - 1) * FFN_CH, True)

    col = lambda off: pl.BlockSpec((s, FFN_TN), lambda j: (0, j + off))
    wcol = lambda off: pl.BlockSpec((3, FFN_TN), lambda j: (0, j + off))
    bcol = lambda off: pl.BlockSpec((1, FFN_TN), lambda j: (0, j + off))
    outs = pl.pallas_call(
        body, name=name, grid=(nj,),
        in_specs=[col(0), col(0), col(nj), wcol(0), wcol(nj), bcol(0), bcol(nj)],
        out_specs=[col(0), col(0), wcol(0), wcol(0), bcol(0), bcol(0)],
        out_shape=[jax.ShapeDtypeStruct((s, f), BF16), jax.ShapeDtypeStruct((s, f), BF16),
                   jax.ShapeDtypeStruct((3, f), F32), jax.ShapeDtypeStruct((3, f), F32),
                   jax.ShapeDtypeStruct((1, f), F32), jax.ShapeDtypeStruct((1, f), F32)],
        scratch_shapes=[pltpu.VMEM((s, FFN_TN), F32), pltpu.VMEM((s, FFN_TN), F32)],
        compiler_params=_params(("parallel",), VMEM_LIMIT),
    )(da, p, p, conv_w, conv_w, conv_b.reshape(1, f2), conv_b.reshape(1, f2))
    dpg, dpu, dwg, dwu, dbg, dbu = outs
    return (jnp.concatenate([dpg, dpu], axis=1), jnp.concatenate([dwg, dwu], axis=1),
            jnp.concatenate([dbg, dbu], axis=1)[0])


def mixnorm_fwd(outs, gain, *, name):
    s = outs[0].shape[0]
    widths = [o.shape[1] for o in outs]
    total = sum(widths)
    tm = 512

    def body(*refs):
        o_refs, g_ref, m_ref = refs[:-2], refs[-2], refs[-1]
        off = 0
        for o_ref, w in zip(o_refs, widths):
            xv = o_ref[...]
            r = lax.rsqrt(jnp.mean(xv * xv, axis=-1, keepdims=True) + EPS)
            m_ref[:, off:off + w] = (xv * r * g_ref[:, off:off + w]).astype(BF16)
            off += w

    return pl.pallas_call(
        body, name=name, grid=(s // tm,),
        in_specs=[pl.BlockSpec((tm, w), lambda i: (i, 0)) for w in widths] + [pl.BlockSpec((1, total), lambda i: (0, 0))],
        out_specs=pl.BlockSpec((tm, total), lambda i: (i, 0)),
        out_shape=jax.ShapeDtypeStruct((s, total), BF16),
        compiler_params=_params(("parallel",)),
    )(*outs, gain.reshape(1, total))


def mixnorm_bwd(dmix, outs, gain, *, name):
    s = outs[0].shape[0]
    widths = [o.shape[1] for o in outs]
    total = sum(widths)
    n = len(outs)
    tm = 256

    def body(*refs):
        dm_ref, o_refs, g_ref = refs[0], refs[1:1 + n], refs[1 + n]
        d_refs, dg_ref = refs[2 + n:2 + 2 * n], refs[2 + 2 * n]

        @pl.when(pl.program_id(0) == 0)
        def _():
            dg_ref[...] = jnp.zeros_like(dg_ref)

        off = 0
        for o_ref, d_ref, w in zip(o_refs, d_refs, widths):
            xv = o_ref[...]
            dhv = dm_ref[:, off:off + w]
            r = lax.rsqrt(jnp.mean(xv * xv, axis=-1, keepdims=True) + EPS)
            gd = dhv * g_ref[:, off:off + w]
            dot = jnp.mean(gd * xv, axis=-1, keepdims=True)
            d_ref[...] = r * gd - xv * (r * r * r * dot)
            dg_ref[:, off:off + w] += jnp.sum(dhv * (xv * r), axis=0, keepdims=True)
            off += w

    res = pl.pallas_call(
        body, name=name, grid=(s // tm,),
        in_specs=[pl.BlockSpec((tm, total), lambda i: (i, 0))]
        + [pl.BlockSpec((tm, w), lambda i: (i, 0)) for w in widths] + [pl.BlockSpec((1, total), lambda i: (0, 0))],
        out_specs=[pl.BlockSpec((tm, w), lambda i: (i, 0)) for w in widths] + [pl.BlockSpec((1, total), lambda i: (0, 0))],
        out_shape=[jax.ShapeDtypeStruct((s, w), F32) for w in widths] + [jax.ShapeDtypeStruct((1, total), F32)],
        compiler_params=_params(("arbitrary",)),
    )(dmix, *outs, gain.reshape(1, total))
    return res[:n], res[n][0]


NORM_CH = 512
FWD_TILES = 4
BWD_TILES = 2


def _lo_mask(shape):
    return lax.broadcasted_iota(I32, shape, 1) < HEAD_DIM


def _head_sum(x, lo):
    del lo
    i = lax.broadcasted_iota(I32, (LANES, LANES), 0) // HEAD_DIM
    j = lax.broadcasted_iota(I32, (LANES, LANES), 1) // HEAD_DIM
    return _split_dot(x, _twice(i == j))


def _head_stats(x, lo):
    return lax.rsqrt(_head_sum(x * x, lo) * (1.0 / HEAD_DIM) + EPS)


def _swap_halves(x):
    return pltpu.roll(x, HEAD_DIM, axis=1)


def _replicate_head(x, lo, use_lo_head):
    sw = _swap_halves(x)
    return jnp.where(use_lo_head, jnp.where(lo, x, sw), jnp.where(lo, sw, x))


def _tile_rows(i, s, d):
    nb = s // (BLOCK * d)
    r = i // nb
    b = i % nb
    start = r + (BLOCK * d) * b
    prev = start - (BLOCK * d) * jnp.minimum(b, 1)
    return start, prev, b > 0


def _rows(ref, start, d):
    if d == 1:
        return ref[pl.ds(pl.multiple_of(start, BLOCK), BLOCK), :]
    return ref[pl.ds(start, BLOCK, stride=d), :]


def _set_rows(ref, start, d, val):
    if d == 1:
        ref[pl.ds(pl.multiple_of(start, BLOCK), BLOCK), :] = val
    else:
        ref[pl.ds(start, BLOCK, stride=d), :] = val


def banded_fwd(proj, qb0, kb0, vb0, n_slabs, gq, gk, bias, dils, sinks, gqa, *, name):
    s = proj.shape[0]
    nbr = len(dils)
    nt = s // BLOCK
    nch = s // NORM_CH
    has_sink = sinks is not None

    def body(*refs):
        q_ref, k_ref, v_ref, gq_ref, gk_ref, b_ref = refs[:6]
        rest = refs[6:]
        if has_sink:
            sink_ref, rest = rest[0], rest[1:]
        out_ref, lse_ref, qn_s, kn_s, vv_s, o_s, l_s = rest
        p = pl.program_id(0)
        use_lo = (p // 2) == 0

        def prep(c, carry):
            rows = pl.ds(pl.multiple_of(c * NORM_CH, NORM_CH), NORM_CH)
            lo = _lo_mask((NORM_CH, LANES))
            qv, kv, vv = q_ref[rows, :], k_ref[rows, :], v_ref[rows, :]
            qn_s[rows, :] = qv * _head_stats(qv, lo) * gq_ref[...] * (HEAD_DIM ** -0.5)
            kn = kv * _head_stats(kv, lo) * gk_ref[...]
            if gqa:
                kn = _replicate_head(kn, lo, use_lo)
                vv = _replicate_head(vv, lo, use_lo)
            kn_s[rows, :] = kn
            vv_s[rows, :] = vv
            return carry

        lax.fori_loop(0, nch, prep, 0)

        lo = _lo_mask((BLOCK, LANES))
        hms = [lo, jnp.logical_not(lo)]
        heads, tiles = range(2), range(FWD_TILES)
        for br, d in enumerate(dils):
            def step(ii, carry, br=br, d=d):
                pos = [_tile_rows(ii * FWD_TILES + u, s, d) for u in tiles]
                kc = [carry[0]] + [_rows(kn_s, pos[u][0], d).astype(BF16) for u in tiles]
                vc = [carry[1]] + [_rows(vv_s, pos[u][0], d).astype(BF16) for u in tiles]
                kcat = [jnp.concatenate([kc[u], kc[u + 1]], axis=0) for u in tiles]
                vcat = [jnp.concatenate([vc[u], vc[u + 1]], axis=0) for u in tiles]
                qt = [_rows(qn_s, pos[u][0], d) for u in tiles]
                sc = [[_dot_nt(jnp.where(hms[h], qt[u], 0.0).astype(BF16), kcat[u])
                       + b_ref[br, jnp.where(pos[u][2], 0, 1), h] for h in heads] for u in tiles]
                m = [[jnp.max(sc[u][h], axis=1, keepdims=True) for h in heads] for u in tiles]
                pe = [[jnp.exp(sc[u][h] - m[u][h]) for h in heads] for u in tiles]
                den = [[jnp.sum(pe[u][h], axis=1, keepdims=True) for h in heads] for u in tiles]
                o = [[_dot(pe[u][h].astype(BF16), vcat[u]) * (1.0 / den[u][h]) for h in heads] for u in tiles]
                for u in tiles:
                    _set_rows(o_s.at[br], pos[u][0], d, jnp.where(lo, o[u][0], o[u][1]))
                    _set_rows(l_s.at[br], pos[u][0], d,
                              jnp.where(lo, m[u][0] + jnp.log(den[u][0]), m[u][1] + jnp.log(den[u][1])))
                return kc[-1], vc[-1]

            none_yet = jnp.zeros((BLOCK, LANES), BF16)
            lax.fori_loop(0, nt // FWD_TILES, step, (none_yet, none_yet))

        def combine(c, carry):
            rows = pl.ds(pl.multiple_of(c * NORM_CH, NORM_CH), NORM_CH)
            ls = [l_s[br, rows, :] for br in range(nbr)]
            mx = functools.reduce(jnp.maximum, ls)
            if has_sink:
                mx = jnp.maximum(mx, sink_ref[...])
            tot = functools.reduce(jnp.add, [jnp.exp(l - mx) for l in ls])
            if has_sink:
                tot = tot + jnp.exp(sink_ref[...] - mx)
            lse = mx + jnp.log(tot)
            acc = jnp.exp(ls[0] - lse) * o_s[0, rows, :]
            for br in range(1, nbr):
                acc = acc + jnp.exp(ls[br] - lse) * o_s[br, rows, :]
            out_ref[rows, :] = acc
            lse_ref[rows, :] = lse
            return carry

        lax.fori_loop(0, nch, combine, 0)

    slab = lambda b0, shared: pl.BlockSpec((s, LANES), (lambda p: (0, b0)) if shared else (lambda p: (0, b0 + p)),
                                           pipeline_mode=pl.Buffered(1))
    vec = pl.BlockSpec((1, LANES), lambda p: (0, 0))
    in_specs = [slab(qb0, False), slab(kb0, gqa), slab(vb0, gqa), vec, vec,
                pl.BlockSpec((nbr, 2, 2, BLOCK, 2 * BLOCK), lambda p: (0, 0, p, 0, 0))]
    args = [proj, proj, proj, gq.reshape(1, LANES), gk.reshape(1, LANES), bias]
    if has_sink:
        in_specs.append(pl.BlockSpec((None, 1, LANES), lambda p: (p, 0, 0)))
        args.append(sinks)
    w = LANES * n_slabs
    return pl.pallas_call(
        body, name=name, grid=(n_slabs,),
        in_specs=in_specs,
        out_specs=[pl.BlockSpec((s, LANES), lambda p: (0, p)), pl.BlockSpec((s, LANES), lambda p: (0, p))],
        out_shape=[jax.ShapeDtypeStruct((s, w), F32), jax.ShapeDtypeStruct((s, w), F32)],
        scratch_shapes=[pltpu.VMEM((s, LANES), F32), pltpu.VMEM((s, LANES), F32), pltpu.VMEM((s, LANES), F32),
                        pltpu.VMEM((nbr, s, LANES), F32), pltpu.VMEM((nbr, s, LANES), F32)],
        compiler_params=_params(("parallel",), VMEM_LIMIT),
    )(*args)


def banded_bwd(proj, qb0, kb0, vb0, n_slabs, gq, gk, bias, dils, sinks, gqa, dout, out, lse, dproj, *, name):
    s = proj.shape[0]
    nbr = len(dils)
    nt = s // BLOCK
    nch = s // NORM_CH
    has_sink = sinks is not None
    scale = HEAD_DIM ** -0.5

    def body(*refs):
        q_ref, k_ref, v_ref, gq_ref, gk_ref, b_ref, do_ref, o_ref, lse_ref = refs[:9]
        rest = refs[9:]
        if has_sink:
            sink_ref, rest = rest[0], rest[1:]
        dproj_ref, db_ref, dgq_ref, dgk_ref = rest[1:5]
        rest = rest[5:]
        if has_sink:
            dsink_ref, rest = rest[0], rest[1:]
        qn_s, kn_s, vv_s, dl_s, dqn_s, dkn_s, dvv_s, dq_ref, dk_ref, dv_ref, stage, sems = rest
        p = pl.program_id(0)
        use_lo = (p // 2) == 0

        def prep(c, carry):
            rows = pl.ds(pl.multiple_of(c * NORM_CH, NORM_CH), NORM_CH)
            lo = _lo_mask((NORM_CH, LANES))
            qv, kv, vv = q_ref[rows, :], k_ref[rows, :], v_ref[rows, :]
            qn_s[rows, :] = qv * _head_stats(qv, lo) * gq_ref[...] * scale
            kn = kv * _head_stats(kv, lo) * gk_ref[...]
            if gqa:
                kn = _replicate_head(kn, lo, use_lo)
                vv = _replicate_head(vv, lo, use_lo)
            kn_s[rows, :] = kn
            vv_s[rows, :] = vv
            delta = _head_sum(do_ref[rows, :] * o_ref[rows, :], lo)
            odd = lax.broadcasted_iota(I32, (NORM_CH, LANES), 1) % 2 == 1
            dl_s[rows, :] = jnp.where(odd, delta, lse_ref[rows, :])
            z = jnp.zeros((NORM_CH, LANES), F32)
            dqn_s[rows, :] = z
            dkn_s[rows, :] = z
            dvv_s[rows, :] = z
            if has_sink:
                ps = jnp.exp(sink_ref[...] - lse_ref[rows, :])
                return carry - jnp.sum(ps * delta, axis=0, keepdims=True)
            return carry

        dsink = lax.fori_loop(0, nch, prep, jnp.zeros((1, LANES), F32))
        if has_sink:
            dsink_ref[...] = jnp.broadcast_to(dsink, (8, LANES))

        lo = _lo_mask((BLOCK, LANES))
        hms = [lo, jnp.logical_not(lo)]
        heads, tiles = range(2), range(BWD_TILES)
        for br, d in enumerate(dils):
            db_ref[br] = jnp.zeros((2, BLOCK, 2 * BLOCK), F32)

            def step(ii, carry, br=br, d=d):
                pos = [_tile_rows(ii * BWD_TILES + u, s, d) for u in tiles]
                kc = [carry[0]] + [_rows(kn_s, pos[u][0], d).astype(BF16) for u in tiles]
                vc = [carry[1]] + [_rows(vv_s, pos[u][0], d).astype(BF16) for u in tiles]
                kcat = [jnp.concatenate([kc[u], kc[u + 1]], axis=0) for u in tiles]
                vcat = [jnp.concatenate([vc[u], vc[u + 1]], axis=0) for u in tiles]
                qt = [_rows(qn_s, pos[u][0], d) for u in tiles]
                dot_ = [_rows(do_ref, pos[u][0], d) for u in tiles]
                st_t = [_rows(dl_s, pos[u][0], d) for u in tiles]
                qh = [[jnp.where(hms[h], qt[u], 0.0).astype(BF16) for h in heads] for u in tiles]
                doh = [[jnp.where(hms[h], dot_[u], 0.0).astype(BF16) for h in heads] for u in tiles]
                sc = [[_dot_nt(qh[u][h], kcat[u]) + b_ref[br, jnp.where(pos[u][2], 0, 1), h] for h in heads]
                      for u in tiles]
                dp = [[_dot_nt(doh[u][h], vcat[u]) for h in heads] for u in tiles]
                lane0 = [0, HEAD_DIM]
                pr = [[jnp.exp(sc[u][h] - st_t[u][:, lane0[h]:lane0[h] + 1]) for h in heads] for u in tiles]
                dlog = [[pr[u][h] * (dp[u][h] - st_t[u][:, lane0[h] + 1:lane0[h] + 2]) for h in heads] for u in tiles]
                for h in heads:
                    db_ref[br, h] += functools.reduce(jnp.add, [dlog[u][h] for u in tiles])
                dlb = [[dlog[u][h].astype(BF16) for h in heads] for u in tiles]
                prb = [[pr[u][h].astype(BF16) for h in heads] for u in tiles]
                dq_t = [jnp.where(lo, _dot(dlb[u][0], kcat[u]), _dot(dlb[u][1], kcat[u])) * scale for u in tiles]
                dk_t = [_dot_tn(dlb[u][0], qh[u][0]) + _dot_tn(dlb[u][1], qh[u][1]) for u in tiles]
                dv_t = [_dot_tn(prb[u][0], doh[u][0]) + _dot_tn(prb[u][1], doh[u][1]) for u in tiles]
                for u in tiles:
                    start, prev = pos[u][0], pos[u][1]
                    _set_rows(dqn_s, start, d, _rows(dqn_s, start, d) + dq_t[u])
                    _set_rows(dkn_s, prev, d, _rows(dkn_s, prev, d) + dk_t[u][:BLOCK])
                    _set_rows(dkn_s, start, d, _rows(dkn_s, start, d) + dk_t[u][BLOCK:])
                    _set_rows(dvv_s, prev, d, _rows(dvv_s, prev, d) + dv_t[u][:BLOCK])
                    _set_rows(dvv_s, start, d, _rows(dvv_s, start, d) + dv_t[u][BLOCK:])
                return kc[-1], vc[-1]

            none_yet = jnp.zeros((BLOCK, LANES), BF16)
            lax.fori_loop(0, nt // BWD_TILES, step, (none_yet, none_yet))

        if gqa:
            @pl.when(p == 0)
            def _():
                dk_ref[...] = jnp.zeros_like(dk_ref)
                dv_ref[...] = jnp.zeros_like(dv_ref)

        def finish(c, carry):
            dgq, dgk = carry
            rows = pl.ds(pl.multiple_of(c * NORM_CH, NORM_CH), NORM_CH)
            lo = _lo_mask((NORM_CH, LANES))

            def norm_bwd(xv, dn, g_ref):
                r = _head_stats(xv, lo)
                gd = dn * g_ref[...]
                dot = _head_sum(gd * xv, lo) * (1.0 / HEAD_DIM)
                return r * gd - xv * (r * r * r * dot), dn * (xv * r)

            dq, gq_part = norm_bwd(q_ref[rows, :], dqn_s[rows, :], gq_ref)
            dq_ref[rows, :] = dq
            dgq = dgq + jnp.sum(gq_part, axis=0, keepdims=True)
            kv, dkn, dvv = k_ref[rows, :], dkn_s[rows, :], dvv_s[rows, :]
            if gqa:
                kv = _replicate_head(kv, lo, use_lo)
                dkn = dkn + _swap_halves(dkn)
                dvv = dvv + _swap_halves(dvv)
                lane = lax.broadcasted_iota(I32, (NORM_CH, LANES), 1)
                mine = (lane // HEAD_DIM) == (p // 2)
                dk, gk_part = norm_bwd(kv, dkn, gk_ref)
                dk_ref[rows, :] += jnp.where(mine, dk, 0.0)
                dv_ref[rows, :] += jnp.where(mine, dvv, 0.0)
                gk_part = jnp.where(lo, gk_part, 0.0)
            else:
                dk, gk_part = norm_bwd(kv, dkn, gk_ref)
                dk_ref[rows, :] = dk
                dv_ref[rows, :] = dvv
            dgk = dgk + jnp.sum(gk_part, axis=0, keepdims=True)
            return dgq, dgk

        z = jnp.zeros((1, LANES), F32)
        dgq, dgk = lax.fori_loop(0, nch, finish, (z, z))
        dgq_ref[...] = jnp.broadcast_to(dgq, (8, LANES))
        dgk_ref[...] = jnp.broadcast_to(dgk, (8, LANES))
        if gqa:
            _store_slabs((dq_ref,), stage, dproj_ref, sems, (qb0 + p,))

            @pl.when(p == n_slabs - 1)
            def _():
                _store_slabs((dk_ref, dv_ref), stage, dproj_ref, sems, (kb0, vb0))
        else:
            _store_slabs((dq_ref, dk_ref, dv_ref), stage, dproj_ref, sems, (qb0 + p, kb0 + p, vb0 + p))

    def slab_of(width_blocks, b0, shared):
        return pl.BlockSpec((s, LANES), (lambda p: (0, b0)) if shared else (lambda p: (0, b0 + p)),
                            pipeline_mode=pl.Buffered(1))

    vec = pl.BlockSpec((1, LANES), lambda p: (0, 0))
    own = pl.BlockSpec((s, LANES), lambda p: (0, p), pipeline_mode=pl.Buffered(1))
    in_specs = [slab_of(0, qb0, False), slab_of(0, kb0, gqa), slab_of(0, vb0, gqa), vec, vec,
                pl.BlockSpec((nbr, 2, 2, BLOCK, 2 * BLOCK), lambda p: (0, 0, p, 0, 0)), own, own, own]
    args = [proj, proj, proj, gq.reshape(1, LANES), gk.reshape(1, LANES), bias, dout, out, lse]
    if has_sink:
        in_specs.append(pl.BlockSpec((None, 1, LANES), lambda p: (p, 0, 0)))
        args.append(sinks)
    held = pl.BlockSpec(memory_space=pl.ANY)
    in_specs.append(held)
    args.append(dproj)
    part = pl.BlockSpec((None, 8, LANES), lambda p: (p, 0, 0))
    out_specs = [held, pl.BlockSpec((nbr, 2, BLOCK, 2 * BLOCK), lambda p: (0, p, 0, 0)), part, part]
    out_shape = [jax.ShapeDtypeStruct(dproj.shape, dproj.dtype),
                 jax.ShapeDtypeStruct((nbr, 2 * n_slabs, BLOCK, 2 * BLOCK), F32),
                 jax.ShapeDtypeStruct((n_slabs, 8, LANES), F32), jax.ShapeDtypeStruct((n_slabs, 8, LANES), F32)]
    if has_sink:
        out_specs.append(part)
        out_shape.append(jax.ShapeDtypeStruct((n_slabs, 8, LANES), F32))
    res = pl.pallas_call(
        body, name=name, grid=(n_slabs,),
        in_specs=in_specs, out_specs=out_specs, out_shape=out_shape,
        input_output_aliases={len(args) - 1: 0},
        scratch_shapes=[pltpu.VMEM((s, LANES), F32) for _ in range(10)]
        + [pltpu.VMEM((3, s, LANES), BF16), pltpu.SemaphoreType.DMA((3,))],
        compiler_params=_params(("arbitrary",), VMEM_LIMIT),
    )(*args)
    outs = [res[0], res[1], res[2][:, 0, :], res[3][:, 0, :]]
    if has_sink:
        outs.append(res[4][:, 0, :])
    return outs


def bias_bwd(dbias, buckets, *, name):
    nbr, h = dbias.shape[:2]

    def body(db_ref, bk_ref, o_ref):
        lane = lax.broadcasted_iota(I32, (1, LANES), 1)
        acc = jnp.zeros((1, LANES), F32)
        for b in range(N_BUCKETS):
            tot = jnp.zeros((1, 1), F32)
            for br in range(nbr):
                sel = jnp.where(bk_ref[br] == b, db_ref[br], 0.0)
                tot = tot + jnp.sum(jnp.sum(sel, axis=0, keepdims=True), axis=1, keepdims=True)
            acc = jnp.where(lane == b, tot, acc)
        o_ref[...] = jnp.broadcast_to(acc, (8, LANES))

    res = pl.pallas_call(
        body, name=name, grid=(h,),
        in_specs=[pl.BlockSpec((nbr, None, BLOCK, 2 * BLOCK), lambda i: (0, i, 0, 0)),
                  pl.BlockSpec((nbr, BLOCK, 2 * BLOCK), lambda i: (0, 0, 0))],
        out_specs=pl.BlockSpec((None, 8, LANES), lambda i: (i, 0, 0)),
        out_shape=jax.ShapeDtypeStruct((h, 8, LANES), F32),
        compiler_params=_params(("parallel",)),
    )(dbias, buckets)
    return res[:, 0, :N_BUCKETS].T


SB_KG = 512
SB_QT = 2


def _softplus(z):
    return jnp.maximum(z, 0.0) + jnp.log(1.0 + jnp.exp(-jnp.abs(z)))


def _twice(t):
    t = t.astype(BF16)
    return jnp.concatenate([t, t], axis=0)


def _split_dot(x, t2):
    hi = x.astype(BF16)
    lo = (x - hi.astype(F32)).astype(BF16)
    return _dot(jnp.concatenate([hi, lo], axis=1), t2)


def sb_fwd(proj, qb0, kb0, vb0, n_slabs, *, name):
    s = proj.shape[0]
    nq = s // BLOCK
    nch = s // NORM_CH
    scale = HEAD_DIM ** -0.5

    def body(q_ref, k_ref, v_ref, o_ref, tot_ref, qlo_s, qhi_s, k_s, v_s):
        def prep(c, carry):
            rows = pl.ds(pl.multiple_of(c * NORM_CH, NORM_CH), NORM_CH)
            lo = _lo_mask((NORM_CH, LANES))
            qv = q_ref[rows, :] * scale
            qlo_s[rows, :] = jnp.where(lo, qv, 0.0).astype(BF16)
            qhi_s[rows, :] = jnp.where(lo, 0.0, qv).astype(BF16)
            k_s[rows, :] = k_ref[rows, :].astype(BF16)
            v_s[rows, :] = v_ref[rows, :].astype(BF16)
            return carry

        lax.fori_loop(0, nch, prep, 0)

        row = lax.broadcasted_iota(I32, (BLOCK, BLOCK), 0)
        col = lax.broadcasted_iota(I32, (BLOCK, BLOCK), 1)
        lo = col < HEAD_DIM
        t_ge = _twice(row >= col)
        rowg = lax.broadcasted_iota(I32, (BLOCK, SB_KG), 0)
        colg = lax.broadcasted_iota(I32, (BLOCK, SB_KG), 1)

        nsub = SB_KG // BLOCK
        chains = range(2 * SB_QT)
        nc = len(chains)

        def qloop(qs, phase):
            q0 = pl.multiple_of(qs * (SB_QT * BLOCK), SB_QT * BLOCK)
            qh = [(qlo_s, qhi_s)[i % 2][pl.ds(q0 + (i // 2) * BLOCK, BLOCK), :] for i in chains]
            gd = (qs * SB_QT) // nsub

            def logits(gi):
                k0 = pl.multiple_of(gi * SB_KG, SB_KG)
                kg = k_s[pl.ds(k0, SB_KG), :]
                return [_dot_nt(qh[i], kg) for i in chains]

            def group(gi, st, masks, npiece=nsub):
                k0 = pl.multiple_of(gi * SB_KG, SB_KG)
                vg = v_s[pl.ds(k0, npiece * BLOCK), :]
                c, o, z = list(st[:nc]), st[nc:2 * nc], st[2 * nc:]
                z_next = logits(jnp.maximum(gi - 1, 0))
                piece = lambda x, j: x[:, j * BLOCK:(j + 1) * BLOCK]
                a = [[None] * npiece for _ in chains]
                for j in reversed(range(npiece)):
                    zj = [piece(z[i], j) for i in chains]
                    lrem = [-_softplus(zj[i]) for i in chains]
                    if masks is not None:
                        lrem = [jnp.where(piece(masks[i // 2], j), lrem[i], 0.0) for i in chains]
                    incl = [_split_dot(lrem[i], t_ge) for i in chains]
                    for i in chains:
                        aij = jnp.exp(zj[i] + (c[i] + incl[i]))
                        if masks is not None:
                            aij = jnp.where(piece(masks[i // 2], j), aij, 0.0)
                        a[i][j] = aij.astype(BF16)
                        c[i] = c[i] + incl[i][:, 0:1]
                o = [o[i] + _dot(jnp.concatenate(a[i], axis=1), vg) for i in chains]
                return (*c, *o, *z_next)

            zc = [jnp.zeros((BLOCK, 1), F32)] * nc
            zo = [jnp.zeros((BLOCK, LANES), F32)] * nc
            masks = [(gd * SB_KG + colg) < (q0 + t * BLOCK + rowg) for t in range(SB_QT)]
            st = group(gd, (*zc, *zo, *logits(gd)), masks, (phase + 1) * SB_QT)
            st = lax.fori_loop(0, gd, lambda t, st: group(gd - 1 - t, st, None), st)
            for t in range(SB_QT):
                rows = pl.ds(q0 + t * BLOCK, BLOCK)
                o_ref[rows, :] = jnp.where(lo, st[nc + 2 * t], st[nc + 2 * t + 1])
                tot_ref[rows, :] = jnp.where(lo, st[2 * t], st[2 * t + 1])

        steps_per_group = nsub // SB_QT

        def per_group(g, carry):
            for phase in range(steps_per_group):
                qloop(g * steps_per_group + phase, phase)
            return carry

        lax.fori_loop(0, nq // nsub, per_group, 0)

    slab = lambda b0: pl.BlockSpec((s, LANES), lambda p: (0, b0 + p), pipeline_mode=pl.Buffered(1))
    w = LANES * n_slabs
    return pl.pallas_call(
        body, name=name, grid=(n_slabs,),
        in_specs=[slab(qb0), slab(kb0), slab(vb0)],
        out_specs=[pl.BlockSpec((s, LANES), lambda p: (0, p)), pl.BlockSpec((s, LANES), lambda p: (0, p))],
        out_shape=[jax.ShapeDtypeStruct((s, w), F32), jax.ShapeDtypeStruct((s, w), F32)],
        scratch_shapes=[pltpu.VMEM((s, LANES), BF16) for _ in range(4)],
        compiler_params=_params(("parallel",), VMEM_LIMIT),
    )(proj, proj, proj)


def _store_slabs(slabs, stage, dproj_ref, sems, blocks):
    s = stage.shape[1]

    def cast(c, carry):
        rows = pl.ds(pl.multiple_of(c * NORM_CH, NORM_CH), NORM_CH)
        for i, slab in enumerate(slabs):
            stage[i, rows, :] = slab[rows, :].astype(BF16)
        return carry

    lax.fori_loop(0, s // NORM_CH, cast, 0)
    copies = [pltpu.make_async_copy(stage.at[i], dproj_ref.at[:, pl.ds(pl.multiple_of(b * LANES, LANES), LANES)],
                                    sems.at[i]) for i, b in enumerate(blocks)]
    for cp in copies:
        cp.start()
    for cp in copies:
        cp.wait()


def sb_bwd(proj, qb0, kb0, vb0, n_slabs, dout, tot, dproj, *, name):
    s = proj.shape[0]
    nq = s // BLOCK
    nch = s // NORM_CH
    nsub = SB_KG // BLOCK
    scale = HEAD_DIM ** -0.5

    def body(q_ref, k_ref, v_ref, do_ref, tot_ref, dproj_in, dproj_ref,
             qlo_s, qhi_s, k_s, v_s, dlo_s, dhi_s, dq_ref, dk_ref, dv_ref, stage, sems):
        del dproj_in
        def prep(c, carry):
            rows = pl.ds(pl.multiple_of(c * NORM_CH, NORM_CH), NORM_CH)
            lo = _lo_mask((NORM_CH, LANES))
            qv = q_ref[rows, :] * scale
            dv = do_ref[rows, :]
            qlo_s[rows, :] = jnp.where(lo, qv, 0.0).astype(BF16)
            qhi_s[rows, :] = jnp.where(lo, 0.0, qv).astype(BF16)
            dlo_s[rows, :] = jnp.where(lo, dv, 0.0).astype(BF16)
            dhi_s[rows, :] = jnp.where(lo, 0.0, dv).astype(BF16)
            k_s[rows, :] = k_ref[rows, :].astype(BF16)
            v_s[rows, :] = v_ref[rows, :].astype(BF16)
            z = jnp.zeros((NORM_CH, LANES), F32)
            dk_ref[rows, :] = z
            dv_ref[rows, :] = z
            return carry

        lax.fori_loop(0, nch, prep, 0)

        row = lax.broadcasted_iota(I32, (BLOCK, BLOCK), 0)
        col = lax.broadcasted_iota(I32, (BLOCK, BLOCK), 1)
        lo = col < HEAD_DIM
        t_le = _twice(row <= col)
        rowg = lax.broadcasted_iota(I32, (BLOCK, SB_KG), 0)
        colg = lax.broadcasted_iota(I32, (BLOCK, SB_KG), 1)

        piece = lambda x, j: x[:, j * BLOCK:(j + 1) * BLOCK]
        chains = range(2 * SB_QT)
        nc = len(chains)

        def prefixes(x):
            return [[_split_dot(piece(x[i], j), t_le) for j in range(x[i].shape[1] // BLOCK)] for i in chains]

        def chain(pre, run, total=None):
            out = []
            for pj in pre:
                out.append(run + pj if total is None else total - run - pj)
                run = run + pj[:, BLOCK - 1:BLOCK]
            return jnp.concatenate(out, axis=1), run

        def qloop(qs, phase):
            q0 = pl.multiple_of(qs * (SB_QT * BLOCK), SB_QT * BLOCK)
            tile = lambda ref, i: ref[pl.ds(q0 + (i // 2) * BLOCK, BLOCK), :]
            qh = [tile((qlo_s, qhi_s)[i % 2], i) for i in chains]
            doh = [tile((dlo_s, dhi_s)[i % 2], i) for i in chains]
            tots = [tile(tot_ref, i)[:, (i % 2) * HEAD_DIM:(i % 2) * HEAD_DIM + 1] for i in chains]
            gd = (qs * SB_QT) // nsub

            def logits(gi):
                kg = k_s[pl.ds(pl.multiple_of(gi * SB_KG, SB_KG), SB_KG), :]
                return [_dot_nt(qh[i], kg) for i in chains]

            def group(gi, st, masks, npiece=nsub):
                k0 = pl.multiple_of(gi * SB_KG, SB_KG)
                wide = npiece * BLOCK
                kg, vg = k_s[pl.ds(k0, wide), :], v_s[pl.ds(k0, wide), :]
                cp, cg, dq = list(st[:nc]), list(st[nc:2 * nc]), st[2 * nc:2 * nc + SB_QT]
                z = [zi[:, :wide] for zi in st[2 * nc + SB_QT:]]
                masked = lambda x, i: x if masks is None else jnp.where(masks[i // 2][:, :wide], x, 0.0)
                z_next = logits(jnp.minimum(gi + 1, gd))
                da = [_dot_nt(doh[i], vg) for i in chains]
                sp = [_softplus(z[i]) for i in chains]
                lrem = [masked(-sp[i], i) for i in chains]
                pre = prefixes(lrem)
                e, a, g = [], [], []
                for i in chains:
                    suffix, cp[i] = chain(pre[i], cp[i], tots[i])
                    e.append(z[i] - sp[i])
                    a.append(masked(jnp.exp(e[i] + suffix), i))
                    g.append(a[i] * da[i])
                gpre = prefixes(g)
                dz = []
                for i in chains:
                    ginc, cg[i] = chain(gpre[i], cg[i])
                    dz.append(masked(g[i] - jnp.exp(e[i]) * ginc, i).astype(BF16))
                ab = [a[i].astype(BF16) for i in chains]
                dq = [dq[t] + jnp.where(lo, _dot(dz[2 * t], kg), _dot(dz[2 * t + 1], kg)) for t in range(SB_QT)]
                dk_ref[pl.ds(k0, wide), :] += functools.reduce(jnp.add, [_dot_tn(dz[i], qh[i]) for i in chains])
                dv_ref[pl.ds(k0, wide), :] += functools.reduce(jnp.add, [_dot_tn(ab[i], doh[i]) for i in chains])
                return (*cp, *cg, *dq, *z_next)

            zc = [jnp.zeros((BLOCK, 1), F32)] * (2 * nc)
            zq = [jnp.zeros((BLOCK, LANES), F32)] * SB_QT
            st = lax.fori_loop(0, gd, lambda gi, st: group(gi, st, None), (*zc, *zq, *logits(0)))
            st = group(gd, st, [(gd * SB_KG + colg) < (q0 + t * BLOCK + rowg) for t in range(SB_QT)],
                       (phase + 1) * SB_QT)
            for t in range(SB_QT):
                dq_ref[pl.ds(q0 + t * BLOCK, BLOCK), :] = st[2 * nc + t] * scale

        steps_per_group = nsub // SB_QT

        def per_group(g, carry):
            for phase in range(steps_per_group):
                qloop(g * steps_per_group + phase, phase)
            return carry

        lax.fori_loop(0, nq // nsub, per_group, 0)
        p = pl.program_id(0)
        _store_slabs((dq_ref, dk_ref, dv_ref), stage, dproj_ref, sems, (qb0 + p, kb0 + p, vb0 + p))

    slab = lambda b0: pl.BlockSpec((s, LANES), lambda p: (0, b0 + p), pipeline_mode=pl.Buffered(1))
    own = pl.BlockSpec((s, LANES), lambda p: (0, p), pipeline_mode=pl.Buffered(1))
    held = pl.BlockSpec(memory_space=pl.ANY)
    return pl.pallas_call(
        body, name=name, grid=(n_slabs,),
        in_specs=[slab(qb0), slab(kb0), slab(vb0), own, own, held],
        out_specs=held, out_shape=jax.ShapeDtypeStruct(dproj.shape, dproj.dtype),
        input_output_aliases={5: 0},
        scratch_shapes=[pltpu.VMEM((s, LANES), BF16) for _ in range(6)]
        + [pltpu.VMEM((s, LANES), F32) for _ in range(3)]
        + [pltpu.VMEM((3, s, LANES), BF16), pltpu.SemaphoreType.DMA((3,))],
        compiler_params=_params(("arbitrary",), VMEM_LIMIT),
    )(proj, proj, proj, dout, tot, dproj)


def _place():
    x, y, c = lax.axis_index("x"), lax.axis_index("y"), lax.axis_index("c")
    return x, y, c


def gather_small(v, *, name):
    m_per, n = v.shape

    def body(x_ref, out_ref, send_sems, recv_sems, local_sem):
        x, y, c = _place()
        me, sibling = (x, y, c), (x, y, 1 - c)
        chips = [(1 - x, y), (x, 1 - y), (1 - x, 1 - y)]

        def rows(px, py, pc):
            return out_ref.at[pl.ds((4 * px + 2 * py + pc) * m_per, m_per), :]

        def copy(k, block, to, src=None):
            return pltpu.make_async_remote_copy(
                src_ref=rows(*block) if src is None else src, dst_ref=rows(*block),
                send_sem=send_sems.at[k], recv_sem=recv_sems.at[k], device_id=to, device_id_type=MESH)

        mine = pltpu.make_async_copy(x_ref, rows(*me), local_sem)
        mine.start()
        first = [copy(0, me, sibling, src=x_ref)]
        first += [copy(1 + j, me, (*chip, c), src=x_ref) for j, chip in enumerate(chips)]
        for cp in first:
            cp.start()
        passed = [copy(4 + j, (*chip, c), sibling) for j, chip in enumerate(chips)]
        for j, chip in enumerate(chips):
            copy(1 + j, (*chip, c), me).wait_recv()
            passed[j].start()
        copy(0, sibling, me).wait_recv()
        for j, chip in enumerate(chips):
            copy(4 + j, (*chip, 1 - c), me).wait_recv()
        for cp in first + passed:
            cp.wait_send()
        mine.wait()

    return pl.pallas_call(
        body, name=name,
        out_shape=jax.ShapeDtypeStruct((N_DEV * m_per, n), v.dtype),
        in_specs=[pl.BlockSpec(memory_space=pltpu.VMEM)],
        out_specs=pl.BlockSpec(memory_space=pltpu.VMEM),
        scratch_shapes=[pltpu.SemaphoreType.DMA((7,)), pltpu.SemaphoreType.DMA((7,)), pltpu.SemaphoreType.DMA],
        compiler_params=_params(None, VMEM_LIMIT),
    )(v)


_HBM = pl.BlockSpec(memory_space=pltpu.HBM)
_SEM = pl.BlockSpec(memory_space=pltpu.SEMAPHORE)
_EFFECT = pltpu.SideEffectType.DATAFLOW_SIDE_EFFECTING


def _peer_copies(src_refs, land_refs, send_sems, recv_sems, per_dest):
    x, y, c = _place()
    me = 4 * x + 2 * y + c
    copies = []
    for src, land, ssem, rsem in zip(src_refs, land_refs, send_sems, recv_sems):
        for k in (1, 2, 4, 3, 5, 6, 7):
            px, py, pc = x ^ (k >> 2 & 1), y ^ (k >> 1 & 1), c ^ (k & 1)
            copies.append(pltpu.make_async_remote_copy(
                src_ref=src.at[4 * px + 2 * py + pc] if per_dest else src, dst_ref=land.at[me],
                send_sem=ssem.at[k - 1], recv_sem=rsem.at[k - 1], device_id=(px, py, pc), device_id_type=MESH))
    return copies


def exchange_start(srcs, lands, per_dest, *, name):
    n = len(srcs)

    def body(*refs):
        src_refs, land_refs = refs[:n], refs[n:2 * n]
        send_sems, recv_sems = refs[2 * n:3 * n], refs[3 * n:4 * n]
        token = refs[-1]
        for cp in _peer_copies(src_refs, land_refs, send_sems, recv_sems, per_dest):
            cp.start()
        token[...] = jnp.zeros_like(token)

    hbm = lambda a: pltpu.HBM(a.shape, a.dtype)
    res = pl.pallas_call(
        body, name=name,
        out_shape=(*[pltpu.SemaphoreType.DMA((7,))] * (2 * n),
                   *[hbm(a) for a in srcs], *[hbm(a) for a in lands], jax.ShapeDtypeStruct((8, LANES), F32)),
        in_specs=[_HBM] * (2 * n),
        out_specs=(*[_SEM] * (2 * n), *[_HBM] * (2 * n), pl.BlockSpec(memory_space=pltpu.VMEM)),
        input_output_aliases={i: 2 * n + i for i in range(2 * n)},
        compiler_params=pltpu.CompilerParams(has_side_effects=_EFFECT),
    )(*[pltpu.with_memory_space_constraint(a, pltpu.HBM) for a in (*srcs, *lands)])
    handles = [(res[a], res[n + a], res[2 * n + a], res[3 * n + a]) for a in range(n)]
    return handles, res[-1]


def exchange_wait(handles, per_dest, after, *, name):
    n = len(handles)

    def body(*refs):
        src_refs, land_refs = refs[:n], refs[n:2 * n]
        send_sems, recv_sems = refs[2 * n:3 * n], refs[3 * n:4 * n]
        for cp in _peer_copies(src_refs, land_refs, send_sems, recv_sems, per_dest):
            cp.wait_send()
            cp.wait_recv()

    srcs, lands = [h[2] for h in handles], [h[3] for h in handles]
    hbm = lambda a: pltpu.HBM(a.shape, a.dtype)
    res = pl.pallas_call(
        body, name=name,
        out_shape=(*[hbm(a) for a in srcs], *[hbm(a) for a in lands]),
        in_specs=[*[_HBM] * (2 * n), *[_SEM] * (2 * n), pl.BlockSpec(memory_space=pl.ANY)],
        out_specs=tuple([_HBM] * (2 * n)),
        input_output_aliases={i: i for i in range(2 * n)},
        compiler_params=pltpu.CompilerParams(has_side_effects=_EFFECT),
    )(*srcs, *lands, *[h[0] for h in handles], *[h[1] for h in handles], after)
    return res[n:]


def _adamw_math(w, g, m, v):
    m = ADAM_B1 * m + (1.0 - ADAM_B1) * g
    v = ADAM_B2 * v + (1.0 - ADAM_B2) * (g * g)
    m_hat = m / (1.0 - ADAM_B1 ** ADAM_STEP)
    v_hat = v / (1.0 - ADAM_B2 ** ADAM_STEP)
    delta = -ADAM_LR * (m_hat / (jnp.sqrt(v_hat) + ADAM_EPS) + ADAM_WD * w)
    return delta, m, v


def adamw_parts(parts, w, m, v, layer, outs, *, name):
    depth, r, cdim = w.shape
    n_parts = parts.shape[0]
    tr = _pick(r, [t for t in (512, 256, 128, 64, 32, 16) if t * cdim <= 256 * 1024])

    def body(p_ref, w_ref, m_ref, v_ref, g0, d0, nm0, nv0, g_ref, d_ref, nm_ref, nv_ref):
        g = p_ref[0].astype(F32)
        for q in range(1, n_parts):
            g = g + p_ref[q].astype(F32)
        delta, nm, nv = _adamw_math(w_ref[...], g, m_ref[...], v_ref[...])
        g_ref[...], d_ref[...], nm_ref[...], nv_ref[...] = g, delta, nm, nv

    t = pl.BlockSpec((None, tr, cdim), lambda i: (layer, i, 0))
    held = pl.BlockSpec(memory_space=pl.ANY)
    return pl.pallas_call(
        body, name=name, grid=(r // tr,),
        in_specs=[pl.BlockSpec((n_parts, tr, cdim), lambda i: (0, i, 0)), t, t, t, held, held, held, held],
        out_specs=[t, t, t, t],
        out_shape=[jax.ShapeDtypeStruct((depth, r, cdim), F32)] * 4,
        input_output_aliases={4: 0, 5: 1, 6: 2, 7: 3},
        compiler_params=_params(("parallel",), VMEM_LIMIT),
    )(parts, w, m, v, *outs)


def sum_devices(gathered, *, name):
    m_rows = gathered.shape[1]

    def body(ga_ref, g_ref):
        g = ga_ref[0]
        for dev in range(1, N_DEV):
            g = g + ga_ref[dev]
        g_ref[...] = g

    return pl.pallas_call(
        body, name=name, out_shape=jax.ShapeDtypeStruct((m_rows, LANES), F32),
        compiler_params=_params(None, VMEM_LIMIT),
    )(gathered)


def adamw_small(g, w, m, v, *, name):
    m_rows = w.shape[0]

    def body(g_ref, w_ref, m_ref, v_ref, d_ref, nm_ref, nv_ref):
        d_ref[...], nm_ref[...], nv_ref[...] = _adamw_math(w_ref[...], g_ref[...], m_ref[...], v_ref[...])

    return pl.pallas_call(
        body, name=name, out_shape=[jax.ShapeDtypeStruct((m_rows, LANES), F32)] * 3,
        compiler_params=_params(None, VMEM_LIMIT),
    )(g, w, m, v)


def _t5_bucket(dist):
    max_exact = N_BUCKETS // 2
    d = jnp.maximum(dist, 0)
    large = max_exact + (jnp.log(jnp.maximum(d, 1).astype(F32) / max_exact)
                         / math.log(T5_MAX_DIST / max_exact) * (N_BUCKETS - max_exact)).astype(I32)
    large = jnp.minimum(large, N_BUCKETS - 1)
    return jnp.where(d < max_exact, d, large)


def _rel():
    return jnp.arange(BLOCK)[:, None] + BLOCK - jnp.arange(2 * BLOCK)[None, :]


def _band_bias(table, dils, max_dists):
    rel = _rel()
    biases, buckets = [], []
    for d, md in zip(dils, max_dists):
        bk = _t5_bucket(rel * d)
        vis = (rel >= 0) & (rel <= md)
        looked_up = jnp.zeros((table.shape[1],) + rel.shape, F32)
        for b in range(N_BUCKETS):
            looked_up = jnp.where((bk == b)[None], table[b][:, None, None], looked_up)
        with_prev = jnp.where(vis[None], looked_up, NEG_INF)
        first = jnp.arange(2 * BLOCK)[None, None, :] >= BLOCK
        biases.append(jnp.stack([with_prev, jnp.where(first, with_prev, NEG_INF)]))
        buckets.append(bk.astype(I32))
    return jnp.stack(biases), jnp.stack(buckets)


def _pack(pieces, rows):
    flat = jnp.concatenate([p.reshape(-1) for p in pieces])
    return jnp.pad(flat, (0, rows * LANES - flat.shape[0])).reshape(rows, LANES)


def _unpack(packed, shapes):
    flat = packed.reshape(-1)
    out, off = [], 0
    for sh in shapes:
        n = math.prod(sh)
        out.append(flat[off:off + n].reshape(sh))
        off += n
    return out


def _tile2(g):
    return jnp.concatenate([g, g])


def kernel(x, attn_norm, w_in, a_q_gain, a_k_gain, a_sinks, c_q_gain, c_k_gain, rel_bias_table, mix_out_gain, w_out, ffn_norm, w_up, conv_w, conv_b, w_down, loss_target, m_attn_norm, m_w_in, m_a_q_gain, m_a_k_gain, m_a_sinks, m_c_q_gain, m_c_k_gain, m_rel_bias_table, m_mix_out_gain, m_w_out, m_ffn_norm, m_w_up, m_conv_w, m_conv_b, m_w_down, v_attn_norm, v_w_in, v_a_q_gain, v_a_k_gain, v_a_sinks, v_c_q_gain, v_c_k_gain, v_rel_bias_table, v_mix_out_gain, v_w_out, v_ffn_norm, v_w_up, v_conv_w, v_conv_b, v_w_down):
    depth, d_model, in_shard = w_in.shape
    ff2_shard = w_up.shape[2]
    s = x.shape[1]
    in_width, ff2 = N_DEV * in_shard, N_DEV * ff2_shard
    n_heads = d_model // HEAD_DIM
    ha, hb, hc = n_heads // 4, n_heads // 4, n_heads // 2
    sa, sb, sc = ha // 2, hb // 2, hc // 2
    kv_a = ha // 4
    assert kv_a == 2 and BLOCK == LANES
    cb_aq, cb_ak, cb_av = 0, sa, sa + 1
    cb_bq = sa + 2
    cb_bk, cb_bv = cb_bq + sb, cb_bq + 2 * sb
    cb_cq = cb_bq + 3 * sb
    cb_ck, cb_cv = cb_cq + sc, cb_cq + 2 * sc
    assert (cb_cv + sc) * LANES == in_width
    dev = 4 * lax.axis_index("x") + 2 * lax.axis_index("y") + lax.axis_index("c")

    def landing(own):
        return lax.dynamic_update_slice_in_dim(lax.empty((N_DEV,) + own.shape, own.dtype), own[None], dev, axis=0)

    per_array = 2
    wnames = ("w_in", "w_out", "w_up", "w_down", "conv_w")
    cols_to_rows = lambda g: jnp.transpose(g, (1, 0, 2)).reshape(g.shape[1], N_DEV * g.shape[2])
    whole = dict(w_in=cols_to_rows, w_up=cols_to_rows, conv_w=cols_to_rows,
                 w_out=lambda g: g.reshape(d_model, d_model), w_down=lambda g: g.reshape(ff2 // 2, d_model))
    gathers = {}
    token = jnp.zeros((8, LANES), F32)
    for l in range(depth):
        for gi, group in enumerate([[n] for n in wnames] if l < per_array else [wnames]):
            srcs = [(dict(w_in=w_in, w_out=w_out, w_up=w_up, w_down=w_down, conv_w=conv_w)[n][l] + token[0, 0])
                    .astype(F32 if n == "conv_w" else BF16) for n in group]
            handles, token = exchange_start(srcs, [landing(a) for a in srcs], False, name=f"gather_start_{l}_{gi}")
            gathers.update({(l, n): h for n, h in zip(group, handles)})

    def gathered(l, names, after):
        landed = exchange_wait([gathers[l, n] for n in names], False, after,
                               name=f"gather_wait_{l}_{wnames.index(names[0])}")
        return {n: whole[n](g) for n, g in zip(names, landed)}

    bias_a, buckets_a = _band_bias(rel_bias_table[:, :ha], (1,), (WINDOW_A - 1,))
    bias_c, buckets_c = _band_bias(rel_bias_table[:, ha:], DILATIONS, (BLOCK,) * len(DILATIONS))

    xs = x[0]
    saved = []
    wi, wo, wu, wd, cw = ([None] * depth for _ in range(5))
    for l in range(depth):
        if l < per_array:
            need = lambda n, after, l=l: gathered(l, (n,), after)[n]
        else:
            layer_w = gathered(l, wnames, xs)
            need = lambda n, after: layer_w[n]
        wi[l] = need("w_in", token if l == 0 else xs)
        h1 = rmsnorm_fwd(xs, attn_norm[l], name="attn_norm_fwd")
        proj = matmul(h1, wi[l], name="in_proj")
        sinks = jnp.repeat(a_sinks[l], HEAD_DIM).reshape(sa, 1, LANES)
        gaq, gak = _tile2(a_q_gain[l]), _tile2(a_k_gain[l])
        gcq, gck = _tile2(c_q_gain[l]), _tile2(c_k_gain[l])
        out_a, lse_a = banded_fwd(proj, cb_aq, cb_ak, cb_av, sa, gaq, gak, bias_a, (1,), sinks, True, name="swa_fwd")
        out_b, tot_b = sb_fwd(proj, cb_bq, cb_bk, cb_bv, sb, name="stick_fwd")
        out_c, lse_c = banded_fwd(proj, cb_cq, cb_ck, cb_cv, sc, gcq, gck, bias_c, DILATIONS, None, False,
                                  name="dilated_fwd")
        mix = mixnorm_fwd([out_a, out_b, out_c], mix_out_gain[l], name="mix_norm_fwd")
        wo[l] = need("w_out", mix)
        x_mid = matmul(mix, wo[l], res=xs, name="out_proj")
        h2 = rmsnorm_fwd(x_mid, ffn_norm[l], name="ffn_norm_fwd")
        wu[l] = need("w_up", h2)
        p = matmul(h2, wu[l], name="up_proj")
        cw[l] = need("conv_w", p)
        act = ffn_act_fwd(p, cw[l], conv_b[l], name="ffn_act_fwd")
        wd[l] = need("w_down", act)
        x_out = matmul(act, wd[l], res=x_mid, name="down_proj")
        saved.append(dict(x_in=xs, h1=h1, proj=proj, out_a=out_a, lse_a=lse_a, out_b=out_b, tot_b=tot_b,
                          out_c=out_c, lse_c=lse_c, mix=mix, x_mid=x_mid, h2=h2, p=p, act=act,
                          sinks=sinks, gains=(gaq, gak, gcq, gck)))
        xs = x_out

    dx, dx_b, loss_part = loss_head(xs, loss_target[0], name="loss_head")

    small = {k: [None] * depth for k in ("attn_norm", "a_q_gain", "a_k_gain", "a_sinks", "c_q_gain", "c_k_gain",
                                         "mix_out_gain", "ffn_norm", "conv_w", "conv_b")}
    big = {k: [None] * depth for k in ("w_in", "w_out", "w_up", "w_down")}
    dbias_a = dbias_c = None
    scatters = {}
    token = jnp.zeros((8, LANES), F32)
    names_big = ("w_in", "w_out", "w_up", "w_down")

    def scatter(l, names):
        parts = [big[n][l] for n in names]
        lands = [landing(lax.dynamic_index_in_dim(pt, dev, axis=0, keepdims=False)) for pt in parts]
        handles, tok = exchange_start(parts, lands, True, name=f"scatter_start_{l}_{names_big.index(names[0])}")
        scatters.update({(l, n): h for n, h in zip(names, handles)})
        return tok

    by_cols = lambda a: jnp.transpose(a.reshape(a.shape[0], N_DEV, a.shape[1] // N_DEV), (1, 0, 2))
    by_rows = lambda a: a.reshape(N_DEV, a.shape[0] // N_DEV, a.shape[1])
    for l in reversed(range(depth)):
        each = l == 0
        sv = saved[l]
        gaq, gak, gcq, gck = sv["gains"]
        da = matmul(dx_b, wd[l], trans_b=True, name="down_proj_dx")
        big["w_down"][l] = by_rows(matmul(sv["act"], dx_b, trans_a=True, out_dtype=BF16, name="down_proj_dw"))
        if each:
            token = scatter(l, ("w_down",))
        dp, small["conv_w"][l], small["conv_b"][l] = ffn_act_bwd(da, sv["p"], cw[l], conv_b[l] + token[0, 0],
                                                                 name="ffn_act_bwd")
        dh2 = matmul(dp, wu[l], trans_b=True, name="up_proj_dx")
        big["w_up"][l] = matmul(sv["h2"], dp, trans_a=True, out_dtype=BF16, col_blocks=N_DEV, name="up_proj_dw")
        if each:
            token = scatter(l, ("w_up",))
        dx_mid, dx_mid_b, small["ffn_norm"][l] = rmsnorm_bwd(dh2, sv["x_mid"], ffn_norm[l] + token[0, 0], dx,
                                                   name="ffn_norm_bwd")
        dmix = matmul(dx_mid_b, wo[l], trans_b=True, name="out_proj_dx")
        big["w_out"][l] = by_rows(matmul(sv["mix"], dx_mid_b, trans_a=True, out_dtype=BF16, name="out_proj_dw"))
        if each:
            token = scatter(l, ("w_out",))
        (d_oa, d_ob, d_oc), small["mix_out_gain"][l] = mixnorm_bwd(
            dmix, [sv["out_a"], sv["out_b"], sv["out_c"]], mix_out_gain[l] + token[0, 0], name="mix_norm_bwd")
        dproj = lax.empty((s, in_width), BF16)
        dproj, db_a, dgq_a, dgk_a, dsink = banded_bwd(
            sv["proj"], cb_aq, cb_ak, cb_av, sa, gaq, gak, bias_a, (1,), sv["sinks"], True,
            d_oa, sv["out_a"], sv["lse_a"], dproj, name="swa_bwd")
        dproj = sb_bwd(sv["proj"], cb_bq, cb_bk, cb_bv, sb, d_ob, sv["tot_b"], dproj, name="stick_bwd")
        dproj, db_c, dgq_c, dgk_c = banded_bwd(
            sv["proj"], cb_cq, cb_ck, cb_cv, sc, gcq, gck, bias_c, DILATIONS, None, False,
            d_oc, sv["out_c"], sv["lse_c"], dproj, name="dilated_bwd")
        fold = lambda g: g.reshape(-1, HEAD_DIM).sum(axis=0)
        small["a_q_gain"][l], small["a_k_gain"][l] = fold(dgq_a), fold(dgk_a)
        small["c_q_gain"][l], small["c_k_gain"][l] = fold(dgq_c), fold(dgk_c)
        small["a_sinks"][l] = dsink[:, ::HEAD_DIM].reshape(-1)
        dbias_a = db_a if dbias_a is None else dbias_a + db_a
        dbias_c = db_c if dbias_c is None else dbias_c + db_c
        big["w_in"][l] = by_cols(matmul(sv["h1"], dproj, trans_a=True, out_dtype=BF16, name="in_proj_dw"))
        if not each:
            token = scatter(l, names_big)
        dh1 = matmul(dproj, wi[l], trans_b=True, name="in_proj_dx")
        dx, dx_b, small["attn_norm"][l] = rmsnorm_bwd(dh1, sv["x_in"], attn_norm[l] + token[0, 0], dx_mid,
                                                name="attn_norm_bwd")

    dtable = jnp.concatenate([bias_bwd(dbias_a, buckets_a, name="swa_bias_bwd"),
                              bias_bwd(dbias_c, buckets_c, name="dilated_bias_bwd")], axis=1)

    order = ("attn_norm", "a_q_gain", "a_k_gain", "a_sinks", "c_q_gain", "c_k_gain", "rel_bias_table",
             "mix_out_gain", "ffn_norm", "conv_w", "conv_b")
    partial = {k: jnp.stack(v) for k, v in small.items()}
    partial["rel_bias_table"] = dtable
    pieces = [partial[k] for k in order] + [loss_part.reshape(1)]
    n_small = sum(math.prod(pc.shape) for pc in pieces)
    rows = -(-n_small // (8 * LANES)) * 8
    gathered = gather_small(_pack(pieces, rows), name="gather_small_grads")
    big["w_in"][0], gathered = lax.optimization_barrier((big["w_in"][0], gathered))
    scatter(0, ("w_in",))
    summed = _unpack(sum_devices(gathered.reshape(N_DEV, rows, LANES), name="sum_small_grads"),
                     [pc.shape for pc in pieces])
    g_small = dict(zip(order, summed[:-1]))
    loss = summed[-1][0]
    g_small["conv_w"] = lax.dynamic_slice_in_dim(g_small["conv_w"], dev * ff2_shard, ff2_shard, axis=2)

    w_small = dict(attn_norm=attn_norm, a_q_gain=a_q_gain, a_k_gain=a_k_gain, a_sinks=a_sinks, c_q_gain=c_q_gain,
                   c_k_gain=c_k_gain, rel_bias_table=rel_bias_table, mix_out_gain=mix_out_gain, ffn_norm=ffn_norm,
                   conv_w=conv_w, conv_b=conv_b)
    m_small = dict(attn_norm=m_attn_norm, a_q_gain=m_a_q_gain, a_k_gain=m_a_k_gain, a_sinks=m_a_sinks,
                   c_q_gain=m_c_q_gain, c_k_gain=m_c_k_gain, rel_bias_table=m_rel_bias_table,
                   mix_out_gain=m_mix_out_gain, ffn_norm=m_ffn_norm, conv_w=m_conv_w, conv_b=m_conv_b)
    v_small = dict(attn_norm=v_attn_norm, a_q_gain=v_a_q_gain, a_k_gain=v_a_k_gain, a_sinks=v_a_sinks,
                   c_q_gain=v_c_q_gain, c_k_gain=v_c_k_gain, rel_bias_table=v_rel_bias_table,
                   mix_out_gain=v_mix_out_gain, ffn_norm=v_ffn_norm, conv_w=v_conv_w, conv_b=v_conv_b)
    shapes = [w_small[k].shape for k in order]
    n_upd = sum(math.prod(sh) for sh in shapes)
    urows = -(-n_upd // (8 * LANES)) * 8
    packs = [_pack([d[k] for k in order], urows) for d in (g_small, w_small, m_small, v_small)]
    upd = adamw_small(*packs, name="adamw_small")
    delta_s, newm_s, newv_s = [dict(zip(order, _unpack(u, shapes))) for u in upd]

    w_big = dict(w_in=(w_in, m_w_in, v_w_in), w_out=(w_out, m_w_out, v_w_out), w_up=(w_up, m_w_up, v_w_up),
                 w_down=(w_down, m_w_down, v_w_down))
    results = {k: [lax.empty(w_big[k][0].shape, F32) for _ in range(4)] for k in names_big}
    after = upd[0]
    for l, names in [(l, names_big) for l in reversed(range(1, depth))] + [(0, names_big[1:]), (0, names_big[:1])]:
        landed = exchange_wait([scatters[l, n] for n in names], True, after,
                               name=f"scatter_wait_{l}_{names_big.index(names[0])}")
        for k, parts in zip(names, landed):
            results[k] = adamw_parts(parts, *w_big[k], l, results[k], name="adamw_large")
        after = results[names[-1]][0]
    g_big, delta_b, newm_b, newv_b = [{k: results[k][i] for k in names_big} for i in range(4)]

    all_names = ("attn_norm", "w_in", "a_q_gain", "a_k_gain", "a_sinks", "c_q_gain", "c_k_gain", "rel_bias_table",
                 "mix_out_gain", "w_out", "ffn_norm", "w_up", "conv_w", "conv_b", "w_down")
    pick = lambda sm, bg: [bg[k] if k in bg else sm[k] for k in all_names]
    return (loss, dx[None], *pick(g_small, g_big), *pick(delta_s, delta_b), *pick(newm_s, newm_b),
            *pick(newv_s, newv_b))
```

```python
import functools
import math

import jax
import jax.numpy as jnp
from jax import lax
from jax.experimental import pallas as pl
from jax.experimental.pallas import tpu as pltpu

F32, BF16, I32 = jnp.float32, jnp.bfloat16, jnp.int32
MESH = pl.DeviceIdType.MESH

HEAD_DIM = 64
LANES = 128
BLOCK = 128
EPS = 1e-6
NEG_INF = -1e30
N_BUCKETS = 32
T5_MAX_DIST = 2048
WINDOW_A = 128
DILATIONS = (1, 4, 16)
N_DEV = 8
VMEM_LIMIT = 56 * 1024 * 1024
MATMUL_VMEM = 46 * 1024 * 1024

ADAM_LR, ADAM_B1, ADAM_B2, ADAM_EPS, ADAM_WD, ADAM_STEP = 0.001, 0.9, 0.999, 1e-08, 0.01, 10


def _params(sem=None, vmem=None):
    return pltpu.CompilerParams(dimension_semantics=sem, vmem_limit_bytes=vmem)


def _pick(n, cands):
    for c in cands:
        if n % c == 0:
            return c
    raise ValueError(f"no tile for {n}")


def _dot(a, b):
    return lax.dot_general(a, b, (((1,), (0,)), ((), ())), preferred_element_type=F32)


def _dot_nt(a, b):
    return lax.dot_general(a, b, (((1,), (1,)), ((), ())), preferred_element_type=F32)


def _dot_tn(a, b):
    return lax.dot_general(a, b, (((0,), (0,)), ((), ())), preferred_element_type=F32)


def matmul(a, b, *, trans_a=False, trans_b=False, out_dtype=F32, res=None, col_blocks=None, name):
    m, k = (a.shape[1], a.shape[0]) if trans_a else a.shape
    n = b.shape[0] if trans_b else b.shape[1]
    tm = _pick(m, (1408, 1024, 512, 256))
    tn_cands = (n // col_blocks,) if col_blocks else tuple(t for t in (1408, 1024, 768, 512, 256, 128) if n % t == 0)

    def footprint(tk, tn):
        tiles = 2 * (tm * tk * a.dtype.itemsize + tk * tn * b.dtype.itemsize)
        return tiles + tm * tn * (4 + 2 * jnp.dtype(out_dtype).itemsize + (8 if res is not None else 0))

    tk, tn = next((tk, tn) for tk in (2816, 2048, 1792, 1024, 768, 512, 256) if k % tk == 0
                  for tn in tn_cands if footprint(tk, tn) <= MATMUL_VMEM)
    nk = k // tk
    dn = (((0 if trans_a else 1,), (1 if trans_b else 0,)), ((), ()))

    def body(*refs):
        if res is None:
            a_ref, b_ref, o_ref, acc = refs
        else:
            a_ref, b_ref, r_ref, o_ref, acc = refs
        kk = pl.program_id(2)

        @pl.when(kk == 0)
        def _():
            acc[...] = jnp.zeros_like(acc)

        acc[...] += lax.dot_general(a_ref[...].astype(BF16), b_ref[...].astype(BF16), dn,
                                    preferred_element_type=F32)

        @pl.when(kk == nk - 1)
        def _():
            r = acc[...]
            if res is not None:
                r = r_ref[...] + r
            o_ref[...] = r.astype(out_dtype)

    b_spec = (pl.BlockSpec((tn, tk), lambda i, j, kk: (j, kk)) if trans_b
              else pl.BlockSpec((tk, tn), lambda i, j, kk: (kk, j)))
    a_spec = (pl.BlockSpec((tk, tm), lambda i, j, kk: (kk, i)) if trans_a
              else pl.BlockSpec((tm, tk), lambda i, j, kk: (i, kk)))
    in_specs = [a_spec, b_spec]
    args = [a, b]
    if res is not None:
        in_specs.append(pl.BlockSpec((tm, tn), lambda i, j, kk: (i, j)))
        args.append(res)
    if col_blocks:
        out_spec = pl.BlockSpec((None, tm, tn), lambda i, j, kk: (j, i, 0))
        out_shape = jax.ShapeDtypeStruct((col_blocks, m, tn), out_dtype)
    else:
        out_spec = pl.BlockSpec((tm, tn), lambda i, j, kk: (i, j))
        out_shape = jax.ShapeDtypeStruct((m, n), out_dtype)
    return pl.pallas_call(
        body, name=name, grid=(m // tm, n // tn, nk),
        in_specs=in_specs, out_specs=out_spec, out_shape=out_shape,
        scratch_shapes=[pltpu.VMEM((tm, tn), F32)],
        compiler_params=_params(("parallel", "parallel", "arbitrary"), VMEM_LIMIT),
    )(*args)


def rmsnorm_fwd(x, g, *, name):
    s, d = x.shape
    tm = 512

    def body(x_ref, g_ref, o_ref):
        xv = x_ref[...]
        r = lax.rsqrt(jnp.mean(xv * xv, axis=-1, keepdims=True) + EPS)
        o_ref[...] = (xv * r * g_ref[...]).astype(BF16)

    return pl.pallas_call(
        body, name=name, grid=(s // tm,),
        in_specs=[pl.BlockSpec((tm, d), lambda i: (i, 0)), pl.BlockSpec((1, d), lambda i: (0, 0))],
        out_specs=pl.BlockSpec((tm, d), lambda i: (i, 0)),
        out_shape=jax.ShapeDtypeStruct((s, d), BF16),
        compiler_params=_params(("parallel",)),
    )(x, g.reshape(1, d))


def rmsnorm_bwd(dh, x, g, dres, *, name):
    s, d = x.shape
    tm = 256

    def body(dh_ref, x_ref, g_ref, dres_ref, dx_ref, dxb_ref, dg_ref):
        @pl.when(pl.program_id(0) == 0)
        def _():
            dg_ref[...] = jnp.zeros_like(dg_ref)

        xv, dhv = x_ref[...], dh_ref[...]
        r = lax.rsqrt(jnp.mean(xv * xv, axis=-1, keepdims=True) + EPS)
        gd = dhv * g_ref[...]
        dot = jnp.mean(gd * xv, axis=-1, keepdims=True)
        dx = dres_ref[...] + (r * gd - xv * (r * r * r * dot))
        dx_ref[...] = dx
        dxb_ref[...] = dx.astype(BF16)
        dg_ref[...] += jnp.sum(dhv * (xv * r), axis=0, keepdims=True)

    row = pl.BlockSpec((tm, d), lambda i: (i, 0))
    dx, dxb, dg = pl.pallas_call(
        body, name=name, grid=(s // tm,),
        in_specs=[row, row, pl.BlockSpec((1, d), lambda i: (0, 0)), row],
        out_specs=[row, row, pl.BlockSpec((1, d), lambda i: (0, 0))],
        out_shape=[jax.ShapeDtypeStruct((s, d), F32), jax.ShapeDtypeStruct((s, d), BF16),
                   jax.ShapeDtypeStruct((1, d), F32)],
        compiler_params=_params(("arbitrary",)),
    )(dh, x, g.reshape(1, d), dres)
    return dx, dxb, dg[0]


def loss_head(y, target, *, name):
    s, d = y.shape
    tm = 512

    def body(y_ref, t_ref, dy_ref, dyb_ref, l_ref):
        @pl.when(pl.program_id(0) == 0)
        def _():
            l_ref[...] = jnp.zeros_like(l_ref)

        e = y_ref[...] - t_ref[...]
        dy = e / float(d)
        dy_ref[...] = dy
        dyb_ref[...] = dy.astype(BF16)
        per_tok = jnp.mean(e * e, axis=-1, keepdims=True)
        l_ref[...] += 0.5 * jnp.sum(per_tok, axis=0, keepdims=True)

    row = pl.BlockSpec((tm, d), lambda i: (i, 0))
    dy, dyb, l = pl.pallas_call(
        body, name=name, grid=(s // tm,),
        in_specs=[row, row],
        out_specs=[row, row, pl.BlockSpec((8, LANES), lambda i: (0, 0))],
        out_shape=[jax.ShapeDtypeStruct((s, d), F32), jax.ShapeDtypeStruct((s, d), BF16),
                   jax.ShapeDtypeStruct((8, LANES), F32)],
        compiler_params=_params(("arbitrary",)),
    )(y, target)
    return dy, dyb, l[0, 0]


FFN_TN = 256
FFN_CH = 256


def _rows_before(ref, r0, first):
    if first:
        cur = ref[pl.ds(0, FFN_CH), :]
        row = lax.broadcasted_iota(I32, cur.shape, 0)
        sh1 = jnp.where(row < 1, 0.0, pltpu.roll(cur, 1, axis=0))
        sh2 = jnp.where(row < 2, 0.0, pltpu.roll(cur, 2, axis=0))
        return cur, sh1, sh2
    ext = ref[pl.ds(pl.multiple_of(r0 - 8, 8), FFN_CH + 8), :]
    return ext[8:], pltpu.roll(ext, 1, axis=0)[8:], pltpu.roll(ext, 2, axis=0)[8:]


def _rows_after(ref, r0, last):
    if last:
        cur = ref[pl.ds(r0, FFN_CH), :]
        row = lax.broadcasted_iota(I32, cur.shape, 0)
        up1 = jnp.where(row >= FFN_CH - 1, 0.0, pltpu.roll(cur, FFN_CH - 1, axis=0))
        up2 = jnp.where(row >= FFN_CH - 2, 0.0, pltpu.roll(cur, FFN_CH - 2, axis=0))
        return cur, up1, up2
    n = FFN_CH + 8
    ext = ref[pl.ds(r0, n), :]
    return ext[:FFN_CH], pltpu.roll(ext, n - 1, axis=0)[:FFN_CH], pltpu.roll(ext, n - 2, axis=0)[:FFN_CH]


def _sigmoid(x):
    return 0.5 * jnp.tanh(0.5 * x) + 0.5


def ffn_act_fwd(p, conv_w, conv_b, *, name):
    s, f2 = p.shape
    f = f2 // 2
    nj = f // FFN_TN
    nch = s // FFN_CH

    def body(pg_ref, pu_ref, wg_ref, wu_ref, bg_ref, bu_ref, a_ref):
        def conv(ref, w_ref, b_ref, r0, first):
            cur, sh1, sh2 = _rows_before(ref, r0, first)
            return ((b_ref[...] + w_ref[0:1, :] * sh2) + w_ref[1:2, :] * sh1) + w_ref[2:3, :] * cur

        def chunk(r0, first):
            gate = conv(pg_ref, wg_ref, bg_ref, r0, first)
            up = conv(pu_ref, wu_ref, bu_ref, r0, first)
            a_ref[pl.ds(r0, FFN_CH), :] = (gate * _sigmoid(gate) * up).astype(BF16)

        chunk(0, True)

        def step(c, carry):
            chunk(pl.multiple_of(c * FFN_CH, FFN_CH), False)
            return carry

        lax.fori_loop(1, nch, step, 0)

    col = lambda off: pl.BlockSpec((s, FFN_TN), lambda j: (0, j + off))
    wcol = lambda off: pl.BlockSpec((3, FFN_TN), lambda j: (0, j + off))
    bcol = lambda off: pl.BlockSpec((1, FFN_TN), lambda j: (0, j + off))
    return pl.pallas_call(
        body, name=name, grid=(nj,),
        in_specs=[col(0), col(nj), wcol(0), wcol(nj), bcol(0), bcol(nj)],
        out_specs=pl.BlockSpec((s, FFN_TN), lambda j: (0, j)),
        out_shape=jax.ShapeDtypeStruct((s, f), BF16),
        compiler_params=_params(("parallel",), VMEM_LIMIT),
    )(p, p, conv_w, conv_w, conv_b.reshape(1, f2), conv_b.reshape(1, f2))


def ffn_act_bwd(da, p, conv_w, conv_b, *, name):
    s, f2 = p.shape
    f = f2 // 2
    nj = f // FFN_TN
    nch = s // FFN_CH

    def body(da_ref, pg_ref, pu_ref, wg_ref, wu_ref, bg_ref, bu_ref,
             dpg_ref, dpu_ref, dwg_ref, dwu_ref, dbg_ref, dbu_ref, dug_s, duu_s):
        def conv(ref, w_ref, b_ref, r0, first):
            cur, sh1, sh2 = _rows_before(ref, r0, first)
            u = ((b_ref[...] + w_ref[0:1, :] * sh2) + w_ref[1:2, :] * sh1) + w_ref[2:3, :] * cur
            return u, (sh2, sh1, cur)

        def taps_sum(du, taps):
            return jnp.concatenate([jnp.sum(du * t, axis=0, keepdims=True) for t in taps], axis=0)

        def chunk(r0, first, acc):
            dwg, dwu, dbg, dbu = acc
            gate, tg = conv(pg_ref, wg_ref, bg_ref, r0, first)
            up, tu = conv(pu_ref, wu_ref, bu_ref, r0, first)
            dav = da_ref[pl.ds(r0, FFN_CH), :]
            sg = _sigmoid(gate)
            dgate = dav * up * (sg * (1.0 + gate * (1.0 - sg)))
            dup = dav * (gate * sg)
            dug_s[pl.ds(r0, FFN_CH), :] = dgate
            duu_s[pl.ds(r0, FFN_CH), :] = dup
            return (dwg + taps_sum(dgate, tg), dwu + taps_sum(dup, tu),
                    dbg + jnp.sum(dgate, axis=0, keepdims=True), dbu + jnp.sum(dup, axis=0, keepdims=True))

        z3 = jnp.zeros((3, FFN_TN), F32)
        z1 = jnp.zeros((1, FFN_TN), F32)
        acc = chunk(0, True, (z3, z3, z1, z1))
        acc = lax.fori_loop(1, nch, lambda c, a: chunk(pl.multiple_of(c * FFN_CH, FFN_CH), False, a), acc)
        dwg_ref[...], dwu_ref[...], dbg_ref[...], dbu_ref[...] = acc

        def back(src, w_ref, dst, r0, last):
            cur, up1, up2 = _rows_after(src, r0, last)
            dst[pl.ds(r0, FFN_CH), :] = (w_ref[2:3, :] * cur + w_ref[1:2, :] * up1 + w_ref[0:1, :] * up2).astype(BF16)

        def step(c, carry):
            r0 = pl.multiple_of(c * FFN_CH, FFN_CH)
            back(dug_s, wg_ref, dpg_ref, r0, False)
            back(duu_s, wu_ref, dpu_ref, r0, False)
            return carry

        lax.fori_loop(0, nch - 1, step, 0)
        back(dug_s, wg_ref, dpg_ref, (nch - 1) * FFN_CH, True)
        back(duu_s, wu_ref, dpu_ref, (nch - 1) * FFN_CH, True)

    col = lambda off: pl.BlockSpec((s, FFN_TN), lambda j: (0, j + off))
    wcol = lambda off: pl.BlockSpec((3, FFN_TN), lambda j: (0, j + off))
    bcol = lambda off: pl.BlockSpec((1, FFN_TN), lambda j: (0, j + off))
    outs = pl.pallas_call(
        body, name=name, grid=(nj,),
        in_specs=[col(0), col(0), col(nj), wcol(0), wcol(nj), bcol(0), bcol(nj)],
        out_specs=[col(0), col(0), wcol(0), wcol(0), bcol(0), bcol(0)],
        out_shape=[jax.ShapeDtypeStruct((s, f), BF16), jax.ShapeDtypeStruct((s, f), BF16),
                   jax.ShapeDtypeStruct((3, f), F32), jax.ShapeDtypeStruct((3, f), F32),
                   jax.ShapeDtypeStruct((1, f), F32), jax.ShapeDtypeStruct((1, f), F32)],
        scratch_shapes=[pltpu.VMEM((s, FFN_TN), F32), pltpu.VMEM((s, FFN_TN), F32)],
        compiler_params=_params(("parallel",), VMEM_LIMIT),
    )(da, p, p, conv_w, conv_w, conv_b.reshape(1, f2), conv_b.reshape(1, f2))
    dpg, dpu, dwg, dwu, dbg, dbu = outs
    return (jnp.concatenate([dpg, dpu], axis=1), jnp.concatenate([dwg, dwu], axis=1),
            jnp.concatenate([dbg, dbu], axis=1)[0])


def mixnorm_fwd(outs, gain, *, name):
    s = outs[0].shape[0]
    widths = [o.shape[1] for o in outs]
    total = sum(widths)
    tm = 512

    def body(*refs):
        o_refs, g_ref, m_ref = refs[:-2], refs[-2], refs[-1]
        off = 0
        for o_ref, w in zip(o_refs, widths):
            xv = o_ref[...]
            r = lax.rsqrt(jnp.mean(xv * xv, axis=-1, keepdims=True) + EPS)
            m_ref[:, off:off + w] = (xv * r * g_ref[:, off:off + w]).astype(BF16)
            off += w

    return pl.pallas_call(
        body, name=name, grid=(s // tm,),
        in_specs=[pl.BlockSpec((tm, w), lambda i: (i, 0)) for w in widths] + [pl.BlockSpec((1, total), lambda i: (0, 0))],
        out_specs=pl.BlockSpec((tm, total), lambda i: (i, 0)),
        out_shape=jax.ShapeDtypeStruct((s, total), BF16),
        compiler_params=_params(("parallel",)),
    )(*outs, gain.reshape(1, total))


def mixnorm_bwd(dmix, outs, gain, *, name):
    s = outs[0].shape[0]
    widths = [o.shape[1] for o in outs]
    total = sum(widths)
    n = len(outs)
    tm = 256

    def body(*refs):
        dm_ref, o_refs, g_ref = refs[0], refs[1:1 + n], refs[1 + n]
        d_refs, dg_ref = refs[2 + n:2 + 2 * n], refs[2 + 2 * n]

        @pl.when(pl.program_id(0) == 0)
        def _():
            dg_ref[...] = jnp.zeros_like(dg_ref)

        off = 0
        for o_ref, d_ref, w in zip(o_refs, d_refs, widths):
            xv = o_ref[...]
            dhv = dm_ref[:, off:off + w]
            r = lax.rsqrt(jnp.mean(xv * xv, axis=-1, keepdims=True) + EPS)
            gd = dhv * g_ref[:, off:off + w]
            dot = jnp.mean(gd * xv, axis=-1, keepdims=True)
            d_ref[...] = r * gd - xv * (r * r * r * dot)
            dg_ref[:, off:off + w] += jnp.sum(dhv * (xv * r), axis=0, keepdims=True)
            off += w

    res = pl.pallas_call(
        body, name=name, grid=(s // tm,),
        in_specs=[pl.BlockSpec((tm, total), lambda i: (i, 0))]
        + [pl.BlockSpec((tm, w), lambda i: (i, 0)) for w in widths] + [pl.BlockSpec((1, total), lambda i: (0, 0))],
        out_specs=[pl.BlockSpec((tm, w), lambda i: (i, 0)) for w in widths] + [pl.BlockSpec((1, total), lambda i: (0, 0))],
        out_shape=[jax.ShapeDtypeStruct((s, w), F32) for w in widths] + [jax.ShapeDtypeStruct((1, total), F32)],
        compiler_params=_params(("arbitrary",)),
    )(dmix, *outs, gain.reshape(1, total))
    return res[:n], res[n][0]


NORM_CH = 512
FWD_TILES = 4
BWD_TILES = 2


def _lo_mask(shape):
    return lax.broadcasted_iota(I32, shape, 1) < HEAD_DIM


def _head_sum(x, lo):
    del lo
    i = lax.broadcasted_iota(I32, (LANES, LANES), 0) // HEAD_DIM
    j = lax.broadcasted_iota(I32, (LANES, LANES), 1) // HEAD_DIM
    return _split_dot(x, _twice(i == j))


def _head_stats(x, lo):
    return lax.rsqrt(_head_sum(x * x, lo) * (1.0 / HEAD_DIM) + EPS)


def _swap_halves(x):
    return pltpu.roll(x, HEAD_DIM, axis=1)


def _replicate_head(x, lo, use_lo_head):
    sw = _swap_halves(x)
    return jnp.where(use_lo_head, jnp.where(lo, x, sw), jnp.where(lo, sw, x))


def _tile_rows(i, s, d):
    nb = s // (BLOCK * d)
    r = i // nb
    b = i % nb
    start = r + (BLOCK * d) * b
    prev = start - (BLOCK * d) * jnp.minimum(b, 1)
    return start, prev, b > 0


def _rows(ref, start, d):
    if d == 1:
        return ref[pl.ds(pl.multiple_of(start, BLOCK), BLOCK), :]
    return ref[pl.ds(start, BLOCK, stride=d), :]


def _set_rows(ref, start, d, val):
    if d == 1:
        ref[pl.ds(pl.multiple_of(start, BLOCK), BLOCK), :] = val
    else:
        ref[pl.ds(start, BLOCK, stride=d), :] = val


def banded_fwd(proj, qb0, kb0, vb0, n_slabs, gq, gk, bias, dils, sinks, gqa, *, name):
    s = proj.shape[0]
    nbr = len(dils)
    nt = s // BLOCK
    nch = s // NORM_CH
    has_sink = sinks is not None

    def body(*refs):
        q_ref, k_ref, v_ref, gq_ref, gk_ref, b_ref = refs[:6]
        rest = refs[6:]
        if has_sink:
            sink_ref, rest = rest[0], rest[1:]
        out_ref, lse_ref, qn_s, kn_s, vv_s, o_s, l_s = rest
        p = pl.program_id(0)
        use_lo = (p // 2) == 0

        def prep(c, carry):
            rows = pl.ds(pl.multiple_of(c * NORM_CH, NORM_CH), NORM_CH)
            lo = _lo_mask((NORM_CH, LANES))
            qv, kv, vv = q_ref[rows, :], k_ref[rows, :], v_ref[rows, :]
            qn_s[rows, :] = qv * _head_stats(qv, lo) * gq_ref[...] * (HEAD_DIM ** -0.5)
            kn = kv * _head_stats(kv, lo) * gk_ref[...]
            if gqa:
                kn = _replicate_head(kn, lo, use_lo)
                vv = _replicate_head(vv, lo, use_lo)
            kn_s[rows, :] = kn
            vv_s[rows, :] = vv
            return carry

        lax.fori_loop(0, nch, prep, 0)

        lo = _lo_mask((BLOCK, LANES))
        hms = [lo, jnp.logical_not(lo)]
        heads, tiles = range(2), range(FWD_TILES)
        for br, d in enumerate(dils):
            def step(ii, carry, br=br, d=d):
                pos = [_tile_rows(ii * FWD_TILES + u, s, d) for u in tiles]
                kc = [carry[0]] + [_rows(kn_s, pos[u][0], d).astype(BF16) for u in tiles]
                vc = [carry[1]] + [_rows(vv_s, pos[u][0], d).astype(BF16) for u in tiles]
                kcat = [jnp.concatenate([kc[u], kc[u + 1]], axis=0) for u in tiles]
                vcat = [jnp.concatenate([vc[u], vc[u + 1]], axis=0) for u in tiles]
                qt = [_rows(qn_s, pos[u][0], d) for u in tiles]
                sc = [[_dot_nt(jnp.where(hms[h], qt[u], 0.0).astype(BF16), kcat[u])
                       + b_ref[br, jnp.where(pos[u][2], 0, 1), h] for h in heads] for u in tiles]
                m = [[jnp.max(sc[u][h], axis=1, keepdims=True) for h in heads] for u in tiles]
                pe = [[jnp.exp(sc[u][h] - m[u][h]) for h in heads] for u in tiles]
                den = [[jnp.sum(pe[u][h], axis=1, keepdims=True) for h in heads] for u in tiles]
                o = [[_dot(pe[u][h].astype(BF16), vcat[u]) * (1.0 / den[u][h]) for h in heads] for u in tiles]
                for u in tiles:
                    _set_rows(o_s.at[br], pos[u][0], d, jnp.where(lo, o[u][0], o[u][1]))
                    _set_rows(l_s.at[br], pos[u][0], d,
                              jnp.where(lo, m[u][0] + jnp.log(den[u][0]), m[u][1] + jnp.log(den[u][1])))
                return kc[-1], vc[-1]

            none_yet = jnp.zeros((BLOCK, LANES), BF16)
            lax.fori_loop(0, nt // FWD_TILES, step, (none_yet, none_yet))

        def combine(c, carry):
            rows = pl.ds(pl.multiple_of(c * NORM_CH, NORM_CH), NORM_CH)
            ls = [l_s[br, rows, :] for br in range(nbr)]
            mx = functools.reduce(jnp.maximum, ls)
            if has_sink:
                mx = jnp.maximum(mx, sink_ref[...])
            tot = functools.reduce(jnp.add, [jnp.exp(l - mx) for l in ls])
            if has_sink:
                tot = tot + jnp.exp(sink_ref[...] - mx)
            lse = mx + jnp.log(tot)
            acc = jnp.exp(ls[0] - lse) * o_s[0, rows, :]
            for br in range(1, nbr):
                acc = acc + jnp.exp(ls[br] - lse) * o_s[br, rows, :]
            out_ref[rows, :] = acc
            lse_ref[rows, :] = lse
            return carry

        lax.fori_loop(0, nch, combine, 0)

    slab = lambda b0, shared: pl.BlockSpec((s, LANES), (lambda p: (0, b0)) if shared else (lambda p: (0, b0 + p)),
                                           pipeline_mode=pl.Buffered(1))
    vec = pl.BlockSpec((1, LANES), lambda p: (0, 0))
    in_specs = [slab(qb0, False), slab(kb0, gqa), slab(vb0, gqa), vec, vec,
                pl.BlockSpec((nbr, 2, 2, BLOCK, 2 * BLOCK), lambda p: (0, 0, p, 0, 0))]
    args = [proj, proj, proj, gq.reshape(1, LANES), gk.reshape(1, LANES), bias]
    if has_sink:
        in_specs.append(pl.BlockSpec((None, 1, LANES), lambda p: (p, 0, 0)))
        args.append(sinks)
    w = LANES * n_slabs
    return pl.pallas_call(
        body, name=name, grid=(n_slabs,),
        in_specs=in_specs,
        out_specs=[pl.BlockSpec((s, LANES), lambda p: (0, p)), pl.BlockSpec((s, LANES), lambda p: (0, p))],
        out_shape=[jax.ShapeDtypeStruct((s, w), F32), jax.ShapeDtypeStruct((s, w), F32)],
        scratch_shapes=[pltpu.VMEM((s, LANES), F32), pltpu.VMEM((s, LANES), F32), pltpu.VMEM((s, LANES), F32),
                        pltpu.VMEM((nbr, s, LANES), F32), pltpu.VMEM((nbr, s, LANES), F32)],
        compiler_params=_params(("parallel",), VMEM_LIMIT),
    )(*args)


def banded_bwd(proj, qb0, kb0, vb0, n_slabs, gq, gk, bias, dils, sinks, gqa, dout, out, lse, dproj, *, name):
    s = proj.shape[0]
    nbr = len(dils)
    nt = s // BLOCK
    nch = s // NORM_CH
    has_sink = sinks is not None
    scale = HEAD_DIM ** -0.5

    def body(*refs):
        q_ref, k_ref, v_ref, gq_ref, gk_ref, b_ref, do_ref, o_ref, lse_ref = refs[:9]
        rest = refs[9:]
        if has_sink:
            sink_ref, rest = rest[0], rest[1:]
        dproj_ref, db_ref, dgq_ref, dgk_ref = rest[1:5]
        rest = rest[5:]
        if has_sink:
            dsink_ref, rest = rest[0], rest[1:]
        qn_s, kn_s, vv_s, dl_s, dqn_s, dkn_s, dvv_s, dq_ref, dk_ref, dv_ref, stage, sems = rest
        p = pl.program_id(0)
        use_lo = (p // 2) == 0

        def prep(c, carry):
            rows = pl.ds(pl.multiple_of(c * NORM_CH, NORM_CH), NORM_CH)
            lo = _lo_mask((NORM_CH, LANES))
            qv, kv, vv = q_ref[rows, :], k_ref[rows, :], v_ref[rows, :]
            qn_s[rows, :] = qv * _head_stats(qv, lo) * gq_ref[...] * scale
            kn = kv * _head_stats(kv, lo) * gk_ref[...]
            if gqa:
                kn = _replicate_head(kn, lo, use_lo)
                vv = _replicate_head(vv, lo, use_lo)
            kn_s[rows, :] = kn
            vv_s[rows, :] = vv
            delta = _head_sum(do_ref[rows, :] * o_ref[rows, :], lo)
            odd = lax.broadcasted_iota(I32, (NORM_CH, LANES), 1) % 2 == 1
            dl_s[rows, :] = jnp.where(odd, delta, lse_ref[rows, :])
            z = jnp.zeros((NORM_CH, LANES), F32)
            dqn_s[rows, :] = z
            dkn_s[rows, :] = z
            dvv_s[rows, :] = z
            if has_sink:
                ps = jnp.exp(sink_ref[...] - lse_ref[rows, :])
                return carry - jnp.sum(ps * delta, axis=0, keepdims=True)
            return carry

        dsink = lax.fori_loop(0, nch, prep, jnp.zeros((1, LANES), F32))
        if has_sink:
            dsink_ref[...] = jnp.broadcast_to(dsink, (8, LANES))

        lo = _lo_mask((BLOCK, LANES))
        hms = [lo, jnp.logical_not(lo)]
        heads, tiles = range(2), range(BWD_TILES)
        for br, d in enumerate(dils):
            db_ref[br] = jnp.zeros((2, BLOCK, 2 * BLOCK), F32)

            def step(ii, carry, br=br, d=d):
                pos = [_tile_rows(ii * BWD_TILES + u, s, d) for u in tiles]
                kc = [carry[0]] + [_rows(kn_s, pos[u][0], d).astype(BF16) for u in tiles]
                vc = [carry[1]] + [_rows(vv_s, pos[u][0], d).astype(BF16) for u in tiles]
                kcat = [jnp.concatenate([kc[u], kc[u + 1]], axis=0) for u in tiles]
                vcat = [jnp.concatenate([vc[u], vc[u + 1]], axis=0) for u in tiles]
                qt = [_rows(qn_s, pos[u][0], d) for u in tiles]
                dot_ = [_rows(do_ref, pos[u][0], d) for u in tiles]
                st_t = [_rows(dl_s, pos[u][0], d) for u in tiles]
                qh = [[jnp.where(hms[h], qt[u], 0.0).astype(BF16) for h in heads] for u in tiles]
                doh = [[jnp.where(hms[h], dot_[u], 0.0).astype(BF16) for h in heads] for u in tiles]
                sc = [[_dot_nt(qh[u][h], kcat[u]) + b_ref[br, jnp.where(pos[u][2], 0, 1), h] for h in heads]
                      for u in tiles]
                dp = [[_dot_nt(doh[u][h], vcat[u]) for h in heads] for u in tiles]
                lane0 = [0, HEAD_DIM]
                pr = [[jnp.exp(sc[u][h] - st_t[u][:, lane0[h]:lane0[h] + 1]) for h in heads] for u in tiles]
                dlog = [[pr[u][h] * (dp[u][h] - st_t[u][:, lane0[h] + 1:lane0[h] + 2]) for h in heads] for u in tiles]
                for h in heads:
                    db_ref[br, h] += functools.reduce(jnp.add, [dlog[u][h] for u in tiles])
                dlb = [[dlog[u][h].astype(BF16) for h in heads] for u in tiles]
                prb = [[pr[u][h].astype(BF16) for h in heads] for u in tiles]
                dq_t = [jnp.where(lo, _dot(dlb[u][0], kcat[u]), _dot(dlb[u][1], kcat[u])) * scale for u in tiles]
                rows2 = lambda x: jnp.concatenate(x, axis=0)
                dk_t = [_dot_tn(rows2(dlb[u]), rows2(qh[u])) for u in tiles]
                dv_t = [_dot_tn(rows2(prb[u]), rows2(doh[u])) for u in tiles]
                for u in tiles:
                    start, prev = pos[u][0], pos[u][1]
                    _set_rows(dqn_s, start, d, _rows(dqn_s, start, d) + dq_t[u])
                    _set_rows(dkn_s, prev, d, _rows(dkn_s, prev, d) + dk_t[u][:BLOCK])
                    _set_rows(dkn_s, start, d, _rows(dkn_s, start, d) + dk_t[u][BLOCK:])
                    _set_rows(dvv_s, prev, d, _rows(dvv_s, prev, d) + dv_t[u][:BLOCK])
                    _set_rows(dvv_s, start, d, _rows(dvv_s, start, d) + dv_t[u][BLOCK:])
                return kc[-1], vc[-1]

            none_yet = jnp.zeros((BLOCK, LANES), BF16)
            lax.fori_loop(0, nt // BWD_TILES, step, (none_yet, none_yet))

        if gqa:
            @pl.when(p == 0)
            def _():
                dk_ref[...] = jnp.zeros_like(dk_ref)
                dv_ref[...] = jnp.zeros_like(dv_ref)

        def finish(c, carry):
            dgq, dgk = carry
            rows = pl.ds(pl.multiple_of(c * NORM_CH, NORM_CH), NORM_CH)
            lo = _lo_mask((NORM_CH, LANES))

            def norm_bwd(xv, dn, g_ref):
                r = _head_stats(xv, lo)
                gd = dn * g_ref[...]
                dot = _head_sum(gd * xv, lo) * (1.0 / HEAD_DIM)
                return r * gd - xv * (r * r * r * dot), dn * (xv * r)

            dq, gq_part = norm_bwd(q_ref[rows, :], dqn_s[rows, :], gq_ref)
            dq_ref[rows, :] = dq
            dgq = dgq + jnp.sum(gq_part, axis=0, keepdims=True)
            kv, dkn, dvv = k_ref[rows, :], dkn_s[rows, :], dvv_s[rows, :]
            if gqa:
                kv = _replicate_head(kv, lo, use_lo)
                dkn = dkn + _swap_halves(dkn)
                dvv = dvv + _swap_halves(dvv)
                lane = lax.broadcasted_iota(I32, (NORM_CH, LANES), 1)
                mine = (lane // HEAD_DIM) == (p // 2)
                dk, gk_part = norm_bwd(kv, dkn, gk_ref)
                dk_ref[rows, :] += jnp.where(mine, dk, 0.0)
                dv_ref[rows, :] += jnp.where(mine, dvv, 0.0)
                gk_part = jnp.where(lo, gk_part, 0.0)
            else:
                dk, gk_part = norm_bwd(kv, dkn, gk_ref)
                dk_ref[rows, :] = dk
                dv_ref[rows, :] = dvv
            dgk = dgk + jnp.sum(gk_part, axis=0, keepdims=True)
            return dgq, dgk

        z = jnp.zeros((1, LANES), F32)
        dgq, dgk = lax.fori_loop(0, nch, finish, (z, z))
        dgq_ref[...] = jnp.broadcast_to(dgq, (8, LANES))
        dgk_ref[...] = jnp.broadcast_to(dgk, (8, LANES))
        if gqa:
            _store_slabs((dq_ref,), stage, dproj_ref, sems, (qb0 + p,))

            @pl.when(p == n_slabs - 1)
            def _():
                _store_slabs((dk_ref, dv_ref), stage, dproj_ref, sems, (kb0, vb0))
        else:
            _store_slabs((dq_ref, dk_ref, dv_ref), stage, dproj_ref, sems, (qb0 + p, kb0 + p, vb0 + p))

    def slab_of(width_blocks, b0, shared):
        return pl.BlockSpec((s, LANES), (lambda p: (0, b0)) if shared else (lambda p: (0, b0 + p)),
                            pipeline_mode=pl.Buffered(1))

    vec = pl.BlockSpec((1, LANES), lambda p: (0, 0))
    own = pl.BlockSpec((s, LANES), lambda p: (0, p), pipeline_mode=pl.Buffered(1))
    in_specs = [slab_of(0, qb0, False), slab_of(0, kb0, gqa), slab_of(0, vb0, gqa), vec, vec,
                pl.BlockSpec((nbr, 2, 2, BLOCK, 2 * BLOCK), lambda p: (0, 0, p, 0, 0)), own, own, own]
    args = [proj, proj, proj, gq.reshape(1, LANES), gk.reshape(1, LANES), bias, dout, out, lse]
    if has_sink:
        in_specs.append(pl.BlockSpec((None, 1, LANES), lambda p: (p, 0, 0)))
        args.append(sinks)
    held = pl.BlockSpec(memory_space=pl.ANY)
    in_specs.append(held)
    args.append(dproj)
    part = pl.BlockSpec((None, 8, LANES), lambda p: (p, 0, 0))
    out_specs = [held, pl.BlockSpec((nbr, 2, BLOCK, 2 * BLOCK), lambda p: (0, p, 0, 0)), part, part]
    out_shape = [jax.ShapeDtypeStruct(dproj.shape, dproj.dtype),
                 jax.ShapeDtypeStruct((nbr, 2 * n_slabs, BLOCK, 2 * BLOCK), F32),
                 jax.ShapeDtypeStruct((n_slabs, 8, LANES), F32), jax.ShapeDtypeStruct((n_slabs, 8, LANES), F32)]
    if has_sink:
        out_specs.append(part)
        out_shape.append(jax.ShapeDtypeStruct((n_slabs, 8, LANES), F32))
    res = pl.pallas_call(
        body, name=name, grid=(n_slabs,),
        in_specs=in_specs, out_specs=out_specs, out_shape=out_shape,
        input_output_aliases={len(args) - 1: 0},
        scratch_shapes=[pltpu.VMEM((s, LANES), F32) for _ in range(10)]
        + [pltpu.VMEM((3, s, LANES), BF16), pltpu.SemaphoreType.DMA((3,))],
        compiler_params=_params(("arbitrary",), VMEM_LIMIT),
    )(*args)
    outs = [res[0], res[1], res[2][:, 0, :], res[3][:, 0, :]]
    if has_sink:
        outs.append(res[4][:, 0, :])
    return outs


def bias_bwd(dbias, buckets, *, name):
    nbr, h = dbias.shape[:2]

    def body(db_ref, bk_ref, o_ref):
        lane = lax.broadcasted_iota(I32, (1, LANES), 1)
        acc = jnp.zeros((1, LANES), F32)
        for b in range(N_BUCKETS):
            tot = jnp.zeros((1, 1), F32)
            for br in range(nbr):
                sel = jnp.where(bk_ref[br] == b, db_ref[br], 0.0)
                tot = tot + jnp.sum(jnp.sum(sel, axis=0, keepdims=True), axis=1, keepdims=True)
            acc = jnp.where(lane == b, tot, acc)
        o_ref[...] = jnp.broadcast_to(acc, (8, LANES))

    res = pl.pallas_call(
        body, name=name, grid=(h,),
        in_specs=[pl.BlockSpec((nbr, None, BLOCK, 2 * BLOCK), lambda i: (0, i, 0, 0)),
                  pl.BlockSpec((nbr, BLOCK, 2 * BLOCK), lambda i: (0, 0, 0))],
        out_specs=pl.BlockSpec((None, 8, LANES), lambda i: (i, 0, 0)),
        out_shape=jax.ShapeDtypeStruct((h, 8, LANES), F32),
        compiler_params=_params(("parallel",)),
    )(dbias, buckets)
    return res[:, 0, :N_BUCKETS].T


SB_KG = 512
SB_QT = 2


def _softplus(z):
    return jnp.maximum(z, 0.0) + jnp.log(1.0 + jnp.exp(-jnp.abs(z)))


def _twice(t):
    t = t.astype(BF16)
    return jnp.concatenate([t, t], axis=0)


def _split_dot(x, t2):
    hi = x.astype(BF16)
    lo = (x - hi.astype(F32)).astype(BF16)
    return _dot(jnp.concatenate([hi, lo], axis=1), t2)


def sb_fwd(proj, qb0, kb0, vb0, n_slabs, *, name):
    s = proj.shape[0]
    nq = s // BLOCK
    nch = s // NORM_CH
    scale = HEAD_DIM ** -0.5

    def body(q_ref, k_ref, v_ref, o_ref, tot_ref, qlo_s, qhi_s, k_s, v_s):
        def prep(c, carry):
            rows = pl.ds(pl.multiple_of(c * NORM_CH, NORM_CH), NORM_CH)
            lo = _lo_mask((NORM_CH, LANES))
            qv = q_ref[rows, :] * scale
            qlo_s[rows, :] = jnp.where(lo, qv, 0.0).astype(BF16)
            qhi_s[rows, :] = jnp.where(lo, 0.0, qv).astype(BF16)
            k_s[rows, :] = k_ref[rows, :].astype(BF16)
            v_s[rows, :] = v_ref[rows, :].astype(BF16)
            return carry

        lax.fori_loop(0, nch, prep, 0)

        row = lax.broadcasted_iota(I32, (BLOCK, BLOCK), 0)
        col = lax.broadcasted_iota(I32, (BLOCK, BLOCK), 1)
        lo = col < HEAD_DIM
        t_ge = _twice(row >= col)
        rowg = lax.broadcasted_iota(I32, (BLOCK, SB_KG), 0)
        colg = lax.broadcasted_iota(I32, (BLOCK, SB_KG), 1)

        nsub = SB_KG // BLOCK
        chains = range(2 * SB_QT)
        nc = len(chains)

        def qloop(qs, phase):
            q0 = pl.multiple_of(qs * (SB_QT * BLOCK), SB_QT * BLOCK)
            qh = [(qlo_s, qhi_s)[i % 2][pl.ds(q0 + (i // 2) * BLOCK, BLOCK), :] for i in chains]
            gd = (qs * SB_QT) // nsub

            def logits(gi):
                k0 = pl.multiple_of(gi * SB_KG, SB_KG)
                kg = k_s[pl.ds(k0, SB_KG), :]
                return [_dot_nt(qh[i], kg) for i in chains]

            def group(gi, st, masks, npiece=nsub):
                k0 = pl.multiple_of(gi * SB_KG, SB_KG)
                vg = v_s[pl.ds(k0, npiece * BLOCK), :]
                c, o, z = list(st[:nc]), st[nc:2 * nc], st[2 * nc:]
                z_next = logits(jnp.maximum(gi - 1, 0))
                piece = lambda x, j: x[:, j * BLOCK:(j + 1) * BLOCK]
                a = [[None] * npiece for _ in chains]
                for j in reversed(range(npiece)):
                    zj = [piece(z[i], j) for i in chains]
                    lrem = [-_softplus(zj[i]) for i in chains]
                    if masks is not None:
                        lrem = [jnp.where(piece(masks[i // 2], j), lrem[i], 0.0) for i in chains]
                    incl = [_split_dot(lrem[i], t_ge) for i in chains]
                    for i in chains:
                        aij = jnp.exp(zj[i] + (c[i] + incl[i]))
                        if masks is not None:
                            aij = jnp.where(piece(masks[i // 2], j), aij, 0.0)
                        a[i][j] = aij.astype(BF16)
                        c[i] = c[i] + incl[i][:, 0:1]
                o = [o[i] + _dot(jnp.concatenate(a[i], axis=1), vg) for i in chains]
                return (*c, *o, *z_next)

            zc = [jnp.zeros((BLOCK, 1), F32)] * nc
            zo = [jnp.zeros((BLOCK, LANES), F32)] * nc
            masks = [(gd * SB_KG + colg) < (q0 + t * BLOCK + rowg) for t in range(SB_QT)]
            st = group(gd, (*zc, *zo, *logits(gd)), masks, (phase + 1) * SB_QT)
            st = lax.fori_loop(0, gd, lambda t, st: group(gd - 1 - t, st, None), st)
            for t in range(SB_QT):
                rows = pl.ds(q0 + t * BLOCK, BLOCK)
                o_ref[rows, :] = jnp.where(lo, st[nc + 2 * t], st[nc + 2 * t + 1])
                tot_ref[rows, :] = jnp.where(lo, st[2 * t], st[2 * t + 1])

        steps_per_group = nsub // SB_QT

        def per_group(g, carry):
            for phase in range(steps_per_group):
                qloop(g * steps_per_group + phase, phase)
            return carry

        lax.fori_loop(0, nq // nsub, per_group, 0)

    slab = lambda b0: pl.BlockSpec((s, LANES), lambda p: (0, b0 + p), pipeline_mode=pl.Buffered(1))
    w = LANES * n_slabs
    return pl.pallas_call(
        body, name=name, grid=(n_slabs,),
        in_specs=[slab(qb0), slab(kb0), slab(vb0)],
        out_specs=[pl.BlockSpec((s, LANES), lambda p: (0, p)), pl.BlockSpec((s, LANES), lambda p: (0, p))],
        out_shape=[jax.ShapeDtypeStruct((s, w), F32), jax.ShapeDtypeStruct((s, w), F32)],
        scratch_shapes=[pltpu.VMEM((s, LANES), BF16) for _ in range(4)],
        compiler_params=_params(("parallel",), VMEM_LIMIT),
    )(proj, proj, proj)


def _store_slabs(slabs, stage, dproj_ref, sems, blocks):
    s = stage.shape[1]

    def cast(c, carry):
        rows = pl.ds(pl.multiple_of(c * NORM_CH, NORM_CH), NORM_CH)
        for i, slab in enumerate(slabs):
            stage[i, rows, :] = slab[rows, :].astype(BF16)
        return carry

    lax.fori_loop(0, s // NORM_CH, cast, 0)
    copies = [pltpu.make_async_copy(stage.at[i], dproj_ref.at[:, pl.ds(pl.multiple_of(b * LANES, LANES), LANES)],
                                    sems.at[i]) for i, b in enumerate(blocks)]
    for cp in copies:
        cp.start()
    for cp in copies:
        cp.wait()


def sb_bwd(proj, qb0, kb0, vb0, n_slabs, dout, tot, dproj, *, name):
    s = proj.shape[0]
    nq = s // BLOCK
    nch = s // NORM_CH
    nsub = SB_KG // BLOCK
    scale = HEAD_DIM ** -0.5

    def body(q_ref, k_ref, v_ref, do_ref, tot_ref, dproj_in, dproj_ref,
             qlo_s, qhi_s, k_s, v_s, dlo_s, dhi_s, dq_ref, dk_ref, dv_ref, stage, sems):
        del dproj_in
        def prep(c, carry):
            rows = pl.ds(pl.multiple_of(c * NORM_CH, NORM_CH), NORM_CH)
            lo = _lo_mask((NORM_CH, LANES))
            qv = q_ref[rows, :] * scale
            dv = do_ref[rows, :]
            qlo_s[rows, :] = jnp.where(lo, qv, 0.0).astype(BF16)
            qhi_s[rows, :] = jnp.where(lo, 0.0, qv).astype(BF16)
            dlo_s[rows, :] = jnp.where(lo, dv, 0.0).astype(BF16)
            dhi_s[rows, :] = jnp.where(lo, 0.0, dv).astype(BF16)
            k_s[rows, :] = k_ref[rows, :].astype(BF16)
            v_s[rows, :] = v_ref[rows, :].astype(BF16)
            z = jnp.zeros((NORM_CH, LANES), F32)
            dk_ref[rows, :] = z
            dv_ref[rows, :] = z
            return carry

        lax.fori_loop(0, nch, prep, 0)

        row = lax.broadcasted_iota(I32, (BLOCK, BLOCK), 0)
        col = lax.broadcasted_iota(I32, (BLOCK, BLOCK), 1)
        lo = col < HEAD_DIM
        t_le = _twice(row <= col)
        rowg = lax.broadcasted_iota(I32, (BLOCK, SB_KG), 0)
        colg = lax.broadcasted_iota(I32, (BLOCK, SB_KG), 1)

        piece = lambda x, j: x[:, j * BLOCK:(j + 1) * BLOCK]
        chains = range(2 * SB_QT)
        nc = len(chains)

        def prefixes(x):
            return [[_split_dot(piece(x[i], j), t_le) for j in range(x[i].shape[1] // BLOCK)] for i in chains]

        def chain(pre, run, total=None):
            out = []
            for pj in pre:
                out.append(run + pj if total is None else total - run - pj)
                run = run + pj[:, BLOCK - 1:BLOCK]
            return jnp.concatenate(out, axis=1), run

        def qloop(qs, phase):
            q0 = pl.multiple_of(qs * (SB_QT * BLOCK), SB_QT * BLOCK)
            tile = lambda ref, i: ref[pl.ds(q0 + (i // 2) * BLOCK, BLOCK), :]
            qh = [tile((qlo_s, qhi_s)[i % 2], i) for i in chains]
            doh = [tile((dlo_s, dhi_s)[i % 2], i) for i in chains]
            tots = [tile(tot_ref, i)[:, (i % 2) * HEAD_DIM:(i % 2) * HEAD_DIM + 1] for i in chains]
            gd = (qs * SB_QT) // nsub

            def logits(gi):
                kg = k_s[pl.ds(pl.multiple_of(gi * SB_KG, SB_KG), SB_KG), :]
                return [_dot_nt(qh[i], kg) for i in chains]

            def group(gi, st, masks, npiece=nsub):
                k0 = pl.multiple_of(gi * SB_KG, SB_KG)
                wide = npiece * BLOCK
                kg, vg = k_s[pl.ds(k0, wide), :], v_s[pl.ds(k0, wide), :]
                cp, cg, dq = list(st[:nc]), list(st[nc:2 * nc]), st[2 * nc:2 * nc + SB_QT]
                z = [zi[:, :wide] for zi in st[2 * nc + SB_QT:]]
                masked = lambda x, i: x if masks is None else jnp.where(masks[i // 2][:, :wide], x, 0.0)
                z_next = logits(jnp.minimum(gi + 1, gd))
                da = [_dot_nt(doh[i], vg) for i in chains]
                sp = [_softplus(z[i]) for i in chains]
                lrem = [masked(-sp[i], i) for i in chains]
                pre = prefixes(lrem)
                e, a, g = [], [], []
                for i in chains:
                    suffix, cp[i] = chain(pre[i], cp[i], tots[i])
                    e.append(z[i] - sp[i])
                    a.append(masked(jnp.exp(e[i] + suffix), i))
                    g.append(a[i] * da[i])
                gpre = prefixes(g)
                dz = []
                for i in chains:
                    ginc, cg[i] = chain(gpre[i], cg[i])
                    dz.append(masked(g[i] - jnp.exp(e[i]) * ginc, i).astype(BF16))
                ab = [a[i].astype(BF16) for i in chains]
                dq = [dq[t] + jnp.where(lo, _dot(dz[2 * t], kg), _dot(dz[2 * t + 1], kg)) for t in range(SB_QT)]
                rows_of = lambda x: jnp.concatenate(x, axis=0)
                dk_ref[pl.ds(k0, wide), :] += _dot_tn(rows_of(dz), rows_of(qh))
                dv_ref[pl.ds(k0, wide), :] += _dot_tn(rows_of(ab), rows_of(doh))
                return (*cp, *cg, *dq, *z_next)

            zc = [jnp.zeros((BLOCK, 1), F32)] * (2 * nc)
            zq = [jnp.zeros((BLOCK, LANES), F32)] * SB_QT
            st = lax.fori_loop(0, gd, lambda gi, st: group(gi, st, None), (*zc, *zq, *logits(0)))
            st = group(gd, st, [(gd * SB_KG + colg) < (q0 + t * BLOCK + rowg) for t in range(SB_QT)],
                       (phase + 1) * SB_QT)
            for t in range(SB_QT):
                dq_ref[pl.ds(q0 + t * BLOCK, BLOCK), :] = st[2 * nc + t] * scale

        steps_per_group = nsub // SB_QT

        def per_group(g, carry):
            for phase in range(steps_per_group):
                qloop(g * steps_per_group + phase, phase)
            return carry

        lax.fori_loop(0, nq // nsub, per_group, 0)
        p = pl.program_id(0)
        _store_slabs((dq_ref, dk_ref, dv_ref), stage, dproj_ref, sems, (qb0 + p, kb0 + p, vb0 + p))

    slab = lambda b0: pl.BlockSpec((s, LANES), lambda p: (0, b0 + p), pipeline_mode=pl.Buffered(1))
    own = pl.BlockSpec((s, LANES), lambda p: (0, p), pipeline_mode=pl.Buffered(1))
    held = pl.BlockSpec(memory_space=pl.ANY)
    return pl.pallas_call(
        body, name=name, grid=(n_slabs,),
        in_specs=[slab(qb0), slab(kb0), slab(vb0), own, own, held],
        out_specs=held, out_shape=jax.ShapeDtypeStruct(dproj.shape, dproj.dtype),
        input_output_aliases={5: 0},
        scratch_shapes=[pltpu.VMEM((s, LANES), BF16) for _ in range(6)]
        + [pltpu.VMEM((s, LANES), F32) for _ in range(3)]
        + [pltpu.VMEM((3, s, LANES), BF16), pltpu.SemaphoreType.DMA((3,))],
        compiler_params=_params(("arbitrary",), VMEM_LIMIT),
    )(proj, proj, proj, dout, tot, dproj)


def _place():
    x, y, c = lax.axis_index("x"), lax.axis_index("y"), lax.axis_index("c")
    return x, y, c


def gather_small(v, *, name):
    m_per, n = v.shape

    def body(x_ref, out_ref, send_sems, recv_sems, local_sem):
        x, y, c = _place()
        me, sibling = (x, y, c), (x, y, 1 - c)
        chips = [(1 - x, y), (x, 1 - y), (1 - x, 1 - y)]

        def rows(px, py, pc):
            return out_ref.at[pl.ds((4 * px + 2 * py + pc) * m_per, m_per), :]

        def copy(k, block, to, src=None):
            return pltpu.make_async_remote_copy(
                src_ref=rows(*block) if src is None else src, dst_ref=rows(*block),
                send_sem=send_sems.at[k], recv_sem=recv_sems.at[k], device_id=to, device_id_type=MESH)

        mine = pltpu.make_async_copy(x_ref, rows(*me), local_sem)
        mine.start()
        first = [copy(0, me, sibling, src=x_ref)]
        first += [copy(1 + j, me, (*chip, c), src=x_ref) for j, chip in enumerate(chips)]
        for cp in first:
            cp.start()
        passed = [copy(4 + j, (*chip, c), sibling) for j, chip in enumerate(chips)]
        for j, chip in enumerate(chips):
            copy(1 + j, (*chip, c), me).wait_recv()
            passed[j].start()
        copy(0, sibling, me).wait_recv()
        for j, chip in enumerate(chips):
            copy(4 + j, (*chip, 1 - c), me).wait_recv()
        for cp in first + passed:
            cp.wait_send()
        mine.wait()

    return pl.pallas_call(
        body, name=name,
        out_shape=jax.ShapeDtypeStruct((N_DEV * m_per, n), v.dtype),
        in_specs=[pl.BlockSpec(memory_space=pltpu.VMEM)],
        out_specs=pl.BlockSpec(memory_space=pltpu.VMEM),
        scratch_shapes=[pltpu.SemaphoreType.DMA((7,)), pltpu.SemaphoreType.DMA((7,)), pltpu.SemaphoreType.DMA],
        compiler_params=_params(None, VMEM_LIMIT),
    )(v)


_HBM = pl.BlockSpec(memory_space=pltpu.HBM)
_SEM = pl.BlockSpec(memory_space=pltpu.SEMAPHORE)
_EFFECT = pltpu.SideEffectType.DATAFLOW_SIDE_EFFECTING


def _peer_copies(src_refs, land_refs, send_sems, recv_sems, per_dest):
    x, y, c = _place()
    me = 4 * x + 2 * y + c
    copies = []
    for src, land, ssem, rsem in zip(src_refs, land_refs, send_sems, recv_sems):
        for k in (1, 2, 4, 3, 5, 6, 7):
            px, py, pc = x ^ (k >> 2 & 1), y ^ (k >> 1 & 1), c ^ (k & 1)
            copies.append(pltpu.make_async_remote_copy(
                src_ref=src.at[4 * px + 2 * py + pc] if per_dest else src, dst_ref=land.at[me],
                send_sem=ssem.at[k - 1], recv_sem=rsem.at[k - 1], device_id=(px, py, pc), device_id_type=MESH))
    return copies


def exchange_start(srcs, per_dest, *, name):
    n = len(srcs)
    lands = [lax.empty(a.shape if per_dest else (N_DEV,) + a.shape, a.dtype) for a in srcs]

    def body(*refs):
        src_refs, land_refs = refs[:n], refs[n:2 * n]
        send_sems, recv_sems = refs[2 * n:3 * n], refs[3 * n:4 * n]
        token, own_sems = refs[-2], refs[-1]
        x, y, c = _place()
        me = 4 * x + 2 * y + c
        own = [pltpu.make_async_copy(src.at[me] if per_dest else src, land.at[me], own_sems.at[a])
               for a, (src, land) in enumerate(zip(src_refs, land_refs))]
        for cp in own:
            cp.start()
        for cp in _peer_copies(src_refs, land_refs, send_sems, recv_sems, per_dest):
            cp.start()
        token[...] = jnp.zeros_like(token)
        for cp in own:
            cp.wait()

    hbm = lambda a: pltpu.HBM(a.shape, a.dtype)
    res = pl.pallas_call(
        body, name=name,
        out_shape=(*[pltpu.SemaphoreType.DMA((7,))] * (2 * n),
                   *[hbm(a) for a in srcs], *[hbm(a) for a in lands], jax.ShapeDtypeStruct((8, LANES), F32)),
        in_specs=[_HBM] * (2 * n),
        out_specs=(*[_SEM] * (2 * n), *[_HBM] * (2 * n), pl.BlockSpec(memory_space=pltpu.VMEM)),
        input_output_aliases={i: 2 * n + i for i in range(2 * n)},
        scratch_shapes=[pltpu.SemaphoreType.DMA((n,))],
        compiler_params=pltpu.CompilerParams(has_side_effects=_EFFECT),
    )(*[pltpu.with_memory_space_constraint(a, pltpu.HBM) for a in (*srcs, *lands)])
    handles = [(res[a], res[n + a], res[2 * n + a], res[3 * n + a]) for a in range(n)]
    return handles, res[-1]


def exchange_wait(handles, per_dest, after, *, name):
    n = len(handles)

    def body(*refs):
        src_refs, land_refs = refs[:n], refs[n:2 * n]
        send_sems, recv_sems = refs[2 * n:3 * n], refs[3 * n:4 * n]
        for cp in _peer_copies(src_refs, land_refs, send_sems, recv_sems, per_dest):
            cp.wait_send()
            cp.wait_recv()

    srcs, lands = [h[2] for h in handles], [h[3] for h in handles]
    hbm = lambda a: pltpu.HBM(a.shape, a.dtype)
    res = pl.pallas_call(
        body, name=name,
        out_shape=(*[hbm(a) for a in srcs], *[hbm(a) for a in lands]),
        in_specs=[*[_HBM] * (2 * n), *[_SEM] * (2 * n), pl.BlockSpec(memory_space=pl.ANY)],
        out_specs=tuple([_HBM] * (2 * n)),
        input_output_aliases={i: i for i in range(2 * n)},
        compiler_params=pltpu.CompilerParams(has_side_effects=_EFFECT),
    )(*srcs, *lands, *[h[0] for h in handles], *[h[1] for h in handles], after)
    return res[n:]


def _adamw_math(w, g, m, v):
    m = ADAM_B1 * m + (1.0 - ADAM_B1) * g
    v = ADAM_B2 * v + (1.0 - ADAM_B2) * (g * g)
    m_hat = m / (1.0 - ADAM_B1 ** ADAM_STEP)
    v_hat = v / (1.0 - ADAM_B2 ** ADAM_STEP)
    delta = -ADAM_LR * (m_hat / (jnp.sqrt(v_hat) + ADAM_EPS) + ADAM_WD * w)
    return delta, m, v


def adamw_parts(parts, w, m, v, layer, outs, *, name):
    depth, r, cdim = w.shape
    n_parts = parts.shape[0]
    tr = _pick(r, [t for t in (512, 256, 128, 64, 32, 16) if t * cdim <= 256 * 1024])

    def body(p_ref, w_ref, m_ref, v_ref, g0, d0, nm0, nv0, g_ref, d_ref, nm_ref, nv_ref):
        g = p_ref[0].astype(F32)
        for q in range(1, n_parts):
            g = g + p_ref[q].astype(F32)
        delta, nm, nv = _adamw_math(w_ref[...], g, m_ref[...], v_ref[...])
        g_ref[...], d_ref[...], nm_ref[...], nv_ref[...] = g, delta, nm, nv

    t = pl.BlockSpec((None, tr, cdim), lambda i: (layer, i, 0))
    held = pl.BlockSpec(memory_space=pl.ANY)
    return pl.pallas_call(
        body, name=name, grid=(r // tr,),
        in_specs=[pl.BlockSpec((n_parts, tr, cdim), lambda i: (0, i, 0)), t, t, t, held, held, held, held],
        out_specs=[t, t, t, t],
        out_shape=[jax.ShapeDtypeStruct((depth, r, cdim), F32)] * 4,
        input_output_aliases={4: 0, 5: 1, 6: 2, 7: 3},
        compiler_params=_params(("parallel",), VMEM_LIMIT),
    )(parts, w, m, v, *outs)


def sum_devices(gathered, *, name):
    m_rows = gathered.shape[1]

    def body(ga_ref, g_ref):
        g = ga_ref[0]
        for dev in range(1, N_DEV):
            g = g + ga_ref[dev]
        g_ref[...] = g

    return pl.pallas_call(
        body, name=name, out_shape=jax.ShapeDtypeStruct((m_rows, LANES), F32),
        compiler_params=_params(None, VMEM_LIMIT),
    )(gathered)


def adamw_small(g, w, m, v, *, name):
    m_rows = w.shape[0]

    def body(g_ref, w_ref, m_ref, v_ref, d_ref, nm_ref, nv_ref):
        d_ref[...], nm_ref[...], nv_ref[...] = _adamw_math(w_ref[...], g_ref[...], m_ref[...], v_ref[...])

    return pl.pallas_call(
        body, name=name, out_shape=[jax.ShapeDtypeStruct((m_rows, LANES), F32)] * 3,
        compiler_params=_params(None, VMEM_LIMIT),
    )(g, w, m, v)


def _t5_bucket(dist):
    max_exact = N_BUCKETS // 2
    d = jnp.maximum(dist, 0)
    large = max_exact + (jnp.log(jnp.maximum(d, 1).astype(F32) / max_exact)
                         / math.log(T5_MAX_DIST / max_exact) * (N_BUCKETS - max_exact)).astype(I32)
    large = jnp.minimum(large, N_BUCKETS - 1)
    return jnp.where(d < max_exact, d, large)


def _rel():
    return jnp.arange(BLOCK)[:, None] + BLOCK - jnp.arange(2 * BLOCK)[None, :]


def _band_bias(table, dils, max_dists):
    rel = _rel()
    biases, buckets = [], []
    for d, md in zip(dils, max_dists):
        bk = _t5_bucket(rel * d)
        vis = (rel >= 0) & (rel <= md)
        looked_up = jnp.zeros((table.shape[1],) + rel.shape, F32)
        for b in range(N_BUCKETS):
            looked_up = jnp.where((bk == b)[None], table[b][:, None, None], looked_up)
        with_prev = jnp.where(vis[None], looked_up, NEG_INF)
        first = jnp.arange(2 * BLOCK)[None, None, :] >= BLOCK
        biases.append(jnp.stack([with_prev, jnp.where(first, with_prev, NEG_INF)]))
        buckets.append(bk.astype(I32))
    return jnp.stack(biases), jnp.stack(buckets)


def _pack(pieces, rows):
    flat = jnp.concatenate([p.reshape(-1) for p in pieces])
    return jnp.pad(flat, (0, rows * LANES - flat.shape[0])).reshape(rows, LANES)


def _unpack(packed, shapes):
    flat = packed.reshape(-1)
    out, off = [], 0
    for sh in shapes:
        n = math.prod(sh)
        out.append(flat[off:off + n].reshape(sh))
        off += n
    return out


def _tile2(g):
    return jnp.concatenate([g, g])


def kernel(x, attn_norm, w_in, a_q_gain, a_k_gain, a_sinks, c_q_gain, c_k_gain, rel_bias_table, mix_out_gain, w_out, ffn_norm, w_up, conv_w, conv_b, w_down, loss_target, m_attn_norm, m_w_in, m_a_q_gain, m_a_k_gain, m_a_sinks, m_c_q_gain, m_c_k_gain, m_rel_bias_table, m_mix_out_gain, m_w_out, m_ffn_norm, m_w_up, m_conv_w, m_conv_b, m_w_down, v_attn_norm, v_w_in, v_a_q_gain, v_a_k_gain, v_a_sinks, v_c_q_gain, v_c_k_gain, v_rel_bias_table, v_mix_out_gain, v_w_out, v_ffn_norm, v_w_up, v_conv_w, v_conv_b, v_w_down):
    depth, d_model, in_shard = w_in.shape
    ff2_shard = w_up.shape[2]
    s = x.shape[1]
    in_width, ff2 = N_DEV * in_shard, N_DEV * ff2_shard
    n_heads = d_model // HEAD_DIM
    ha, hb, hc = n_heads // 4, n_heads // 4, n_heads // 2
    sa, sb, sc = ha // 2, hb // 2, hc // 2
    kv_a = ha // 4
    assert kv_a == 2 and BLOCK == LANES
    cb_aq, cb_ak, cb_av = 0, sa, sa + 1
    cb_bq = sa + 2
    cb_bk, cb_bv = cb_bq + sb, cb_bq + 2 * sb
    cb_cq = cb_bq + 3 * sb
    cb_ck, cb_cv = cb_cq + sc, cb_cq + 2 * sc
    assert (cb_cv + sc) * LANES == in_width
    dev = 4 * lax.axis_index("x") + 2 * lax.axis_index("y") + lax.axis_index("c")

    per_array = 3
    wnames = ("w_in", "w_out", "w_up", "w_down", "conv_w")
    cols_to_rows = lambda g: jnp.transpose(g, (1, 0, 2)).reshape(g.shape[1], N_DEV * g.shape[2])
    whole = dict(w_in=cols_to_rows, w_up=cols_to_rows, conv_w=cols_to_rows,
                 w_out=lambda g: g.reshape(d_model, d_model), w_down=lambda g: g.reshape(ff2 // 2, d_model))
    gathers = {}
    token = jnp.zeros((8, LANES), F32)
    for l in range(depth):
        for gi, group in enumerate([[n] for n in wnames] if l < per_array else [wnames]):
            srcs = [(dict(w_in=w_in, w_out=w_out, w_up=w_up, w_down=w_down, conv_w=conv_w)[n][l] + token[0, 0])
                    .astype(F32 if n == "conv_w" else BF16) for n in group]
            handles, token = exchange_start(srcs, False, name=f"gather_start_{l}_{gi}")
            gathers.update({(l, n): h for n, h in zip(group, handles)})

    def gathered(l, names, after):
        landed = exchange_wait([gathers[l, n] for n in names], False, after,
                               name=f"gather_wait_{l}_{wnames.index(names[0])}")
        return {n: whole[n](g) for n, g in zip(names, landed)}

    bias_a, buckets_a = _band_bias(rel_bias_table[:, :ha], (1,), (WINDOW_A - 1,))
    bias_c, buckets_c = _band_bias(rel_bias_table[:, ha:], DILATIONS, (BLOCK,) * len(DILATIONS))

    xs = x[0]
    saved = []
    wi, wo, wu, wd, cw = ([None] * depth for _ in range(5))
    for l in range(depth):
        if l < per_array:
            need = lambda n, after, l=l: gathered(l, (n,), after)[n]
        else:
            layer_w = gathered(l, wnames, xs)
            need = lambda n, after: layer_w[n]
        wi[l] = need("w_in", token if l == 0 else xs)
        h1 = rmsnorm_fwd(xs, attn_norm[l], name="attn_norm_fwd")
        proj = matmul(h1, wi[l], name="in_proj")
        sinks = jnp.repeat(a_sinks[l], HEAD_DIM).reshape(sa, 1, LANES)
        gaq, gak = _tile2(a_q_gain[l]), _tile2(a_k_gain[l])
        gcq, gck = _tile2(c_q_gain[l]), _tile2(c_k_gain[l])
        out_a, lse_a = banded_fwd(proj, cb_aq, cb_ak, cb_av, sa, gaq, gak, bias_a, (1,), sinks, True, name="swa_fwd")
        out_b, tot_b = sb_fwd(proj, cb_bq, cb_bk, cb_bv, sb, name="stick_fwd")
        out_c, lse_c = banded_fwd(proj, cb_cq, cb_ck, cb_cv, sc, gcq, gck, bias_c, DILATIONS, None, False,
                                  name="dilated_fwd")
        mix = mixnorm_fwd([out_a, out_b, out_c], mix_out_gain[l], name="mix_norm_fwd")
        wo[l] = need("w_out", mix)
        x_mid = matmul(mix, wo[l], res=xs, name="out_proj")
        h2 = rmsnorm_fwd(x_mid, ffn_norm[l], name="ffn_norm_fwd")
        wu[l] = need("w_up", h2)
        p = matmul(h2, wu[l], name="up_proj")
        cw[l] = need("conv_w", p)
        act = ffn_act_fwd(p, cw[l], conv_b[l], name="ffn_act_fwd")
        wd[l] = need("w_down", act)
        x_out = matmul(act, wd[l], res=x_mid, name="down_proj")
        saved.append(dict(x_in=xs, h1=h1, proj=proj, out_a=out_a, lse_a=lse_a, out_b=out_b, tot_b=tot_b,
                          out_c=out_c, lse_c=lse_c, mix=mix, x_mid=x_mid, h2=h2, p=p, act=act,
                          sinks=sinks, gains=(gaq, gak, gcq, gck)))
        xs = x_out

    dx, dx_b, loss_part = loss_head(xs, loss_target[0], name="loss_head")

    small = {k: [None] * depth for k in ("attn_norm", "a_q_gain", "a_k_gain", "a_sinks", "c_q_gain", "c_k_gain",
                                         "mix_out_gain", "ffn_norm", "conv_w", "conv_b")}
    big = {k: [None] * depth for k in ("w_in", "w_out", "w_up", "w_down")}
    dbias_a = dbias_c = None
    scatters = {}
    token = jnp.zeros((8, LANES), F32)
    names_big = ("w_in", "w_out", "w_up", "w_down")

    def scatter(l, names):
        parts = [big[n][l] for n in names]
        handles, tok = exchange_start(parts, True, name=f"scatter_start_{l}_{names_big.index(names[0])}")
        scatters.update({(l, n): h for n, h in zip(names, handles)})
        return tok

    by_cols = lambda a: jnp.transpose(a.reshape(a.shape[0], N_DEV, a.shape[1] // N_DEV), (1, 0, 2))
    by_rows = lambda a: a.reshape(N_DEV, a.shape[0] // N_DEV, a.shape[1])
    for l in reversed(range(depth)):
        each = l == 0
        sv = saved[l]
        gaq, gak, gcq, gck = sv["gains"]
        da = matmul(dx_b, wd[l], trans_b=True, name="down_proj_dx")
        big["w_down"][l] = by_rows(matmul(sv["act"], dx_b, trans_a=True, out_dtype=BF16, name="down_proj_dw"))
        if each:
            token = scatter(l, ("w_down",))
        dp, small["conv_w"][l], small["conv_b"][l] = ffn_act_bwd(da, sv["p"], cw[l], conv_b[l] + token[0, 0],
                                                                 name="ffn_act_bwd")
        dh2 = matmul(dp, wu[l], trans_b=True, name="up_proj_dx")
        big["w_up"][l] = matmul(sv["h2"], dp, trans_a=True, out_dtype=BF16, col_blocks=N_DEV, name="up_proj_dw")
        if each:
            token = scatter(l, ("w_up",))
        dx_mid, dx_mid_b, small["ffn_norm"][l] = rmsnorm_bwd(dh2, sv["x_mid"], ffn_norm[l] + token[0, 0], dx,
                                                   name="ffn_norm_bwd")
        dmix = matmul(dx_mid_b, wo[l], trans_b=True, name="out_proj_dx")
        big["w_out"][l] = by_rows(matmul(sv["mix"], dx_mid_b, trans_a=True, out_dtype=BF16, name="out_proj_dw"))
        if each:
            token = scatter(l, ("w_out",))
        (d_oa, d_ob, d_oc), small["mix_out_gain"][l] = mixnorm_bwd(
            dmix, [sv["out_a"], sv["out_b"], sv["out_c"]], mix_out_gain[l] + token[0, 0], name="mix_norm_bwd")
        dproj = lax.empty((s, in_width), BF16)
        dproj, db_a, dgq_a, dgk_a, dsink = banded_bwd(
            sv["proj"], cb_aq, cb_ak, cb_av, sa, gaq, gak, bias_a, (1,), sv["sinks"], True,
            d_oa, sv["out_a"], sv["lse_a"], dproj, name="swa_bwd")
        dproj = sb_bwd(sv["proj"], cb_bq, cb_bk, cb_bv, sb, d_ob, sv["tot_b"], dproj, name="stick_bwd")
        dproj, db_c, dgq_c, dgk_c = banded_bwd(
            sv["proj"], cb_cq, cb_ck, cb_cv, sc, gcq, gck, bias_c, DILATIONS, None, False,
            d_oc, sv["out_c"], sv["lse_c"], dproj, name="dilated_bwd")
        fold = lambda g: g.reshape(-1, HEAD_DIM).sum(axis=0)
        small["a_q_gain"][l], small["a_k_gain"][l] = fold(dgq_a), fold(dgk_a)
        small["c_q_gain"][l], small["c_k_gain"][l] = fold(dgq_c), fold(dgk_c)
        small["a_sinks"][l] = dsink[:, ::HEAD_DIM].reshape(-1)
        dbias_a = db_a if dbias_a is None else dbias_a + db_a
        dbias_c = db_c if dbias_c is None else dbias_c + db_c
        big["w_in"][l] = by_cols(matmul(sv["h1"], dproj, trans_a=True, out_dtype=BF16, name="in_proj_dw"))
        if not each:
            token = scatter(l, names_big)
        dh1 = matmul(dproj, wi[l], trans_b=True, name="in_proj_dx")
        dx, dx_b, small["attn_norm"][l] = rmsnorm_bwd(dh1, sv["x_in"], attn_norm[l] + token[0, 0], dx_mid,
                                                name="attn_norm_bwd")

    dtable = jnp.concatenate([bias_bwd(dbias_a, buckets_a, name="swa_bias_bwd"),
                              bias_bwd(dbias_c, buckets_c, name="dilated_bias_bwd")], axis=1)

    order = ("attn_norm", "a_q_gain", "a_k_gain", "a_sinks", "c_q_gain", "c_k_gain", "rel_bias_table",
             "mix_out_gain", "ffn_norm", "conv_w", "conv_b")
    partial = {k: jnp.stack(v) for k, v in small.items()}
    partial["rel_bias_table"] = dtable
    pieces = [partial[k] for k in order] + [loss_part.reshape(1)]
    n_small = sum(math.prod(pc.shape) for pc in pieces)
    rows = -(-n_small // (8 * LANES)) * 8
    gathered = gather_small(_pack(pieces, rows), name="gather_small_grads")
    big["w_in"][0], gathered = lax.optimization_barrier((big["w_in"][0], gathered))
    scatter(0, ("w_in",))
    summed = _unpack(sum_devices(gathered.reshape(N_DEV, rows, LANES), name="sum_small_grads"),
                     [pc.shape for pc in pieces])
    g_small = dict(zip(order, summed[:-1]))
    loss = summed[-1][0]
    g_small["conv_w"] = lax.dynamic_slice_in_dim(g_small["conv_w"], dev * ff2_shard, ff2_shard, axis=2)

    w_small = dict(attn_norm=attn_norm, a_q_gain=a_q_gain, a_k_gain=a_k_gain, a_sinks=a_sinks, c_q_gain=c_q_gain,
                   c_k_gain=c_k_gain, rel_bias_table=rel_bias_table, mix_out_gain=mix_out_gain, ffn_norm=ffn_norm,
                   conv_w=conv_w, conv_b=conv_b)
    m_small = dict(attn_norm=m_attn_norm, a_q_gain=m_a_q_gain, a_k_gain=m_a_k_gain, a_sinks=m_a_sinks,
                   c_q_gain=m_c_q_gain, c_k_gain=m_c_k_gain, rel_bias_table=m_rel_bias_table,
                   mix_out_gain=m_mix_out_gain, ffn_norm=m_ffn_norm, conv_w=m_conv_w, conv_b=m_conv_b)
    v_small = dict(attn_norm=v_attn_norm, a_q_gain=v_a_q_gain, a_k_gain=v_a_k_gain, a_sinks=v_a_sinks,
                   c_q_gain=v_c_q_gain, c_k_gain=v_c_k_gain, rel_bias_table=v_rel_bias_table,
                   mix_out_gain=v_mix_out_gain, ffn_norm=v_ffn_norm, conv_w=v_conv_w, conv_b=v_conv_b)
    shapes = [w_small[k].shape for k in order]
    n_upd = sum(math.prod(sh) for sh in shapes)
    urows = -(-n_upd // (8 * LANES)) * 8
    packs = [_pack([d[k] for k in order], urows) for d in (g_small, w_small, m_small, v_small)]
    upd = adamw_small(*packs, name="adamw_small")
    delta_s, newm_s, newv_s = [dict(zip(order, _unpack(u, shapes))) for u in upd]

    w_big = dict(w_in=(w_in, m_w_in, v_w_in), w_out=(w_out, m_w_out, v_w_out), w_up=(w_up, m_w_up, v_w_up),
                 w_down=(w_down, m_w_down, v_w_down))
    results = {k: [lax.empty(w_big[k][0].shape, F32) for _ in range(4)] for k in names_big}
    after = upd[0]
    for l, names in [(l, names_big) for l in reversed(range(1, depth))] + [(0, names_big[1:]), (0, names_big[:1])]:
        landed = exchange_wait([scatters[l, n] for n in names], True, after,
                               name=f"scatter_wait_{l}_{names_big.index(names[0])}")
        for k, parts in zip(names, landed):
            results[k] = adamw_parts(parts, *w_big[k], l, results[k], name="adamw_large")
        after = results[names[-1]][0]
    g_big, delta_b, newm_b, newv_b = [{k: results[k][i] for k in names_big} for i in range(4)]

    all_names = ("attn_norm", "w_in", "a_q_gain", "a_k_gain", "a_sinks", "c_q_gain", "c_k_gain", "rel_bias_table",
                 "mix_out_gain", "w_out", "ffn_norm", "w_up", "conv_w", "conv_b", "w_down")
    pick = lambda sm, bg: [bg[k] if k in bg else sm[k] for k in all_names]
    return (loss, dx[None], *pick(g_small, g_big), *pick(delta_s, delta_b), *pick(newm_s, newm_b),
            *pick(newv_s, newv_b))
```

```python
import functools
import math

import jax
import jax.numpy as jnp
from jax import lax
from jax.experimental import pallas as pl
from jax.experimental.pallas import tpu as pltpu

F32, BF16, I32 = jnp.float32, jnp.bfloat16, jnp.int32
MESH = pl.DeviceIdType.MESH

HEAD_DIM = 64
LANES = 128
BLOCK = 128
EPS = 1e-6
NEG_INF = -1e30
N_BUCKETS = 32
T5_MAX_DIST = 2048
WINDOW_A = 128
DILATIONS = (1, 4, 16)
N_DEV = 8
VMEM_LIMIT = 56 * 1024 * 1024
MATMUL_VMEM = 46 * 1024 * 1024

ADAM_LR, ADAM_B1, ADAM_B2, ADAM_EPS, ADAM_WD, ADAM_STEP = 0.001, 0.9, 0.999, 1e-08, 0.01, 10


def _params(sem=None, vmem=None):
    return pltpu.CompilerParams(dimension_semantics=sem, vmem_limit_bytes=vmem)


def _pick(n, cands):
    for c in cands:
        if n % c == 0:
            return c
    raise ValueError(f"no tile for {n}")


def _dot(a, b):
    return lax.dot_general(a, b, (((1,), (0,)), ((), ())), preferred_element_type=F32)


def _dot_nt(a, b):
    return lax.dot_general(a, b, (((1,), (1,)), ((), ())), preferred_element_type=F32)


def _dot_tn(a, b):
    return lax.dot_general(a, b, (((0,), (0,)), ((), ())), preferred_element_type=F32)


def matmul(a, b, *, trans_a=False, trans_b=False, out_dtype=F32, res=None, col_blocks=None, name):
    m, k = (a.shape[1], a.shape[0]) if trans_a else a.shape
    n = b.shape[0] if trans_b else b.shape[1]
    tm = _pick(m, (1408, 1024, 512, 256))
    tn_cands = (n // col_blocks,) if col_blocks else tuple(t for t in (1408, 1024, 768, 512, 256, 128) if n % t == 0)

    def footprint(tk, tn):
        tiles = 2 * (tm * tk * a.dtype.itemsize + tk * tn * b.dtype.itemsize)
        return tiles + tm * tn * (4 + 2 * jnp.dtype(out_dtype).itemsize + (8 if res is not None else 0))

    tk, tn = next((tk, tn) for tk in (2816, 2048, 1792, 1024, 768, 512, 256) if k % tk == 0
                  for tn in tn_cands if footprint(tk, tn) <= MATMUL_VMEM)
    nk = k // tk
    dn = (((0 if trans_a else 1,), (1 if trans_b else 0,)), ((), ()))

    def body(*refs):
        if res is None:
            a_ref, b_ref, o_ref, acc = refs
        else:
            a_ref, b_ref, r_ref, o_ref, acc = refs
        kk = pl.program_id(2)

        @pl.when(kk == 0)
        def _():
            acc[...] = jnp.zeros_like(acc)

        acc[...] += lax.dot_general(a_ref[...].astype(BF16), b_ref[...].astype(BF16), dn,
                                    preferred_element_type=F32)

        @pl.when(kk == nk - 1)
        def _():
            r = acc[...]
            if res is not None:
                r = r_ref[...] + r
            o_ref[...] = r.astype(out_dtype)

    b_spec = (pl.BlockSpec((tn, tk), lambda i, j, kk: (j, kk)) if trans_b
              else pl.BlockSpec((tk, tn), lambda i, j, kk: (kk, j)))
    a_spec = (pl.BlockSpec((tk, tm), lambda i, j, kk: (kk, i)) if trans_a
              else pl.BlockSpec((tm, tk), lambda i, j, kk: (i, kk)))
    in_specs = [a_spec, b_spec]
    args = [a, b]
    if res is not None:
        in_specs.append(pl.BlockSpec((tm, tn), lambda i, j, kk: (i, j)))
        args.append(res)
    if col_blocks:
        out_spec = pl.BlockSpec((None, tm, tn), lambda i, j, kk: (j, i, 0))
        out_shape = jax.ShapeDtypeStruct((col_blocks, m, tn), out_dtype)
    else:
        out_spec = pl.BlockSpec((tm, tn), lambda i, j, kk: (i, j))
        out_shape = jax.ShapeDtypeStruct((m, n), out_dtype)
    return pl.pallas_call(
        body, name=name, grid=(m // tm, n // tn, nk),
        in_specs=in_specs, out_specs=out_spec, out_shape=out_shape,
        scratch_shapes=[pltpu.VMEM((tm, tn), F32)],
        compiler_params=_params(("parallel", "parallel", "arbitrary"), VMEM_LIMIT),
    )(*args)


def rmsnorm_fwd(x, g, *, name):
    s, d = x.shape
    tm = 512

    def body(x_ref, g_ref, o_ref):
        xv = x_ref[...]
        r = lax.rsqrt(jnp.mean(xv * xv, axis=-1, keepdims=True) + EPS)
        o_ref[...] = (xv * r * g_ref[...]).astype(BF16)

    return pl.pallas_call(
        body, name=name, grid=(s // tm,),
        in_specs=[pl.BlockSpec((tm, d), lambda i: (i, 0)), pl.BlockSpec((1, d), lambda i: (0, 0))],
        out_specs=pl.BlockSpec((tm, d), lambda i: (i, 0)),
        out_shape=jax.ShapeDtypeStruct((s, d), BF16),
        compiler_params=_params(("parallel",)),
    )(x, g.reshape(1, d))


def rmsnorm_bwd(dh, x, g, dres, *, name):
    s, d = x.shape
    tm = 256

    def body(dh_ref, x_ref, g_ref, dres_ref, dx_ref, dxb_ref, dg_ref):
        @pl.when(pl.program_id(0) == 0)
        def _():
            dg_ref[...] = jnp.zeros_like(dg_ref)

        xv, dhv = x_ref[...], dh_ref[...]
        r = lax.rsqrt(jnp.mean(xv * xv, axis=-1, keepdims=True) + EPS)
        gd = dhv * g_ref[...]
        dot = jnp.mean(gd * xv, axis=-1, keepdims=True)
        dx = dres_ref[...] + (r * gd - xv * (r * r * r * dot))
        dx_ref[...] = dx
        dxb_ref[...] = dx.astype(BF16)
        dg_ref[...] += jnp.sum(dhv * (xv * r), axis=0, keepdims=True)

    row = pl.BlockSpec((tm, d), lambda i: (i, 0))
    dx, dxb, dg = pl.pallas_call(
        body, name=name, grid=(s // tm,),
        in_specs=[row, row, pl.BlockSpec((1, d), lambda i: (0, 0)), row],
        out_specs=[row, row, pl.BlockSpec((1, d), lambda i: (0, 0))],
        out_shape=[jax.ShapeDtypeStruct((s, d), F32), jax.ShapeDtypeStruct((s, d), BF16),
                   jax.ShapeDtypeStruct((1, d), F32)],
        compiler_params=_params(("arbitrary",)),
    )(dh, x, g.reshape(1, d), dres)
    return dx, dxb, dg[0]


def loss_head(y, target, *, name):
    s, d = y.shape
    tm = 512

    def body(y_ref, t_ref, dy_ref, dyb_ref, l_ref):
        @pl.when(pl.program_id(0) == 0)
        def _():
            l_ref[...] = jnp.zeros_like(l_ref)

        e = y_ref[...] - t_ref[...]
        dy = e / float(d)
        dy_ref[...] = dy
        dyb_ref[...] = dy.astype(BF16)
        per_tok = jnp.mean(e * e, axis=-1, keepdims=True)
        l_ref[...] += 0.5 * jnp.sum(per_tok, axis=0, keepdims=True)

    row = pl.BlockSpec((tm, d), lambda i: (i, 0))
    dy, dyb, l = pl.pallas_call(
        body, name=name, grid=(s // tm,),
        in_specs=[row, row],
        out_specs=[row, row, pl.BlockSpec((8, LANES), lambda i: (0, 0))],
        out_shape=[jax.ShapeDtypeStruct((s, d), F32), jax.ShapeDtypeStruct((s, d), BF16),
                   jax.ShapeDtypeStruct((8, LANES), F32)],
        compiler_params=_params(("arbitrary",)),
    )(y, target)
    return dy, dyb, l[0, 0]


FFN_TN = 256
FFN_CH = 256


def _rows_before(ref, r0, first):
    if first:
        cur = ref[pl.ds(0, FFN_CH), :]
        row = lax.broadcasted_iota(I32, cur.shape, 0)
        sh1 = jnp.where(row < 1, 0.0, pltpu.roll(cur, 1, axis=0))
        sh2 = jnp.where(row < 2, 0.0, pltpu.roll(cur, 2, axis=0))
        return cur, sh1, sh2
    ext = ref[pl.ds(pl.multiple_of(r0 - 8, 8), FFN_CH + 8), :]
    return ext[8:], pltpu.roll(ext, 1, axis=0)[8:], pltpu.roll(ext, 2, axis=0)[8:]


def _rows_after(ref, r0, last):
    if last:
        cur = ref[pl.ds(r0, FFN_CH), :]
        row = lax.broadcasted_iota(I32, cur.shape, 0)
        up1 = jnp.where(row >= FFN_CH - 1, 0.0, pltpu.roll(cur, FFN_CH - 1, axis=0))
        up2 = jnp.where(row >= FFN_CH - 2, 0.0, pltpu.roll(cur, FFN_CH - 2, axis=0))
        return cur, up1, up2
    n = FFN_CH + 8
    ext = ref[pl.ds(r0, n), :]
    return ext[:FFN_CH], pltpu.roll(ext, n - 1, axis=0)[:FFN_CH], pltpu.roll(ext, n - 2, axis=0)[:FFN_CH]


def _sigmoid(x):
    return 0.5 * jnp.tanh(0.5 * x) + 0.5


def ffn_act_fwd(p, conv_w, conv_b, *, name):
    s, f2 = p.shape
    f = f2 // 2
    nj = f // FFN_TN
    nch = s // FFN_CH

    def body(pg_ref, pu_ref, wg_ref, wu_ref, bg_ref, bu_ref, a_ref):
        def conv(ref, w_ref, b_ref, r0, first):
            cur, sh1, sh2 = _rows_before(ref, r0, first)
            return ((b_ref[...] + w_ref[0:1, :] * sh2) + w_ref[1:2, :] * sh1) + w_ref[2:3, :] * cur

        def chunk(r0, first):
            gate = conv(pg_ref, wg_ref, bg_ref, r0, first)
            up = conv(pu_ref, wu_ref, bu_ref, r0, first)
            a_ref[pl.ds(r0, FFN_CH), :] = (gate * _sigmoid(gate) * up).astype(BF16)

        chunk(0, True)

        def step(c, carry):
            chunk(pl.multiple_of(c * FFN_CH, FFN_CH), False)
            return carry

        lax.fori_loop(1, nch, step, 0)

    col = lambda off: pl.BlockSpec((s, FFN_TN), lambda j: (0, j + off))
    wcol = lambda off: pl.BlockSpec((3, FFN_TN), lambda j: (0, j + off))
    bcol = lambda off: pl.BlockSpec((1, FFN_TN), lambda j: (0, j + off))
    return pl.pallas_call(
        body, name=name, grid=(nj,),
        in_specs=[col(0), col(nj), wcol(0), wcol(nj), bcol(0), bcol(nj)],
        out_specs=pl.BlockSpec((s, FFN_TN), lambda j: (0, j)),
        out_shape=jax.ShapeDtypeStruct((s, f), BF16),
        compiler_params=_params(("parallel",), VMEM_LIMIT),
    )(p, p, conv_w, conv_w, conv_b.reshape(1, f2), conv_b.reshape(1, f2))


def ffn_act_bwd(da, p, conv_w, conv_b, *, name):
    s, f2 = p.shape
    f = f2 // 2
    nj = f // FFN_TN
    nch = s // FFN_CH

    def body(da_ref, pg_ref, pu_ref, wg_ref, wu_ref, bg_ref, bu_ref,
             dpg_ref, dpu_ref, dwg_ref, dwu_ref, dbg_ref, dbu_ref, dug_s, duu_s):
        def conv(ref, w_ref, b_ref, r0, first):
            cur, sh1, sh2 = _rows_before(ref, r0, first)
            u = ((b_ref[...] + w_ref[0:1, :] * sh2) + w_ref[1:2, :] * sh1) + w_ref[2:3, :] * cur
            return u, (sh2, sh1, cur)

        def taps_sum(du, taps):
            return jnp.concatenate([jnp.sum(du * t, axis=0, keepdims=True) for t in taps], axis=0)

        def chunk(r0, first, acc):
            dwg, dwu, dbg, dbu = acc
            gate, tg = conv(pg_ref, wg_ref, bg_ref, r0, first)
            up, tu = conv(pu_ref, wu_ref, bu_ref, r0, first)
            dav = da_ref[pl.ds(r0, FFN_CH), :]
            sg = _sigmoid(gate)
            dgate = dav * up * (sg * (1.0 + gate * (1.0 - sg)))
            dup = dav * (gate * sg)
            dug_s[pl.ds(r0, FFN_CH), :] = dgate
            duu_s[pl.ds(r0, FFN_CH), :] = dup
            return (dwg + taps_sum(dgate, tg), dwu + taps_sum(dup, tu),
                    dbg + jnp.sum(dgate, axis=0, keepdims=True), dbu + jnp.sum(dup, axis=0, keepdims=True))

        z3 = jnp.zeros((3, FFN_TN), F32)
        z1 = jnp.zeros((1, FFN_TN), F32)
        acc = chunk(0, True, (z3, z3, z1, z1))
        acc = lax.fori_loop(1, nch, lambda c, a: chunk(pl.multiple_of(c * FFN_CH, FFN_CH), False, a), acc)
        dwg_ref[...], dwu_ref[...], dbg_ref[...], dbu_ref[...] = acc

        def back(src, w_ref, dst, r0, last):
            cur, up1, up2 = _rows_after(src, r0, last)
            dst[pl.ds(r0, FFN_CH), :] = (w_ref[2:3, :] * cur + w_ref[1:2, :] * up1 + w_ref[0:1, :] * up2).astype(BF16)

        def step(c, carry):
            r0 = pl.multiple_of(c * FFN_CH, FFN_CH)
            back(dug_s, wg_ref, dpg_ref, r0, False)
            back(duu_s, wu_ref, dpu_ref, r0, False)
            return carry

        lax.fori_loop(0, nch - 1, step, 0)
        back(dug_s, wg_ref, dpg_ref, (nch - 1) * FFN_CH, True)
        back(duu_s, wu_ref, dpu_ref, (nch - 1) * FFN_CH, True)

    col = lambda off: pl.BlockSpec((s, FFN_TN), lambda j: (0, j + off))
    wcol = lambda off: pl.BlockSpec((3, FFN_TN), lambda j: (0, j + off))
    bcol = lambda off: pl.BlockSpec((1, FFN_TN), lambda j: (0, j + off))
    outs = pl.pallas_call(
        body, name=name, grid=(nj,),
        in_specs=[col(0), col(0), col(nj), wcol(0), wcol(nj), bcol(0), bcol(nj)],
        out_specs=[col(0), col(0), wcol(0), wcol(0), bcol(0), bcol(0)],
        out_shape=[jax.ShapeDtypeStruct((s, f), BF16), jax.ShapeDtypeStruct((s, f), BF16),
                   jax.ShapeDtypeStruct((3, f), F32), jax.ShapeDtypeStruct((3, f), F32),
                   jax.ShapeDtypeStruct((1, f), F32), jax.ShapeDtypeStruct((1, f), F32)],
        scratch_shapes=[pltpu.VMEM((s, FFN_TN), F32), pltpu.VMEM((s, FFN_TN), F32)],
        compiler_params=_params(("parallel",), VMEM_LIMIT),
    )(da, p, p, conv_w, conv_w, conv_b.reshape(1, f2), conv_b.reshape(1, f2))
    dpg, dpu, dwg, dwu, dbg, dbu = outs
    return (jnp.concatenate([dpg, dpu], axis=1), jnp.concatenate([dwg, dwu], axis=1),
            jnp.concatenate([dbg, dbu], axis=1)[0])


def mixnorm_fwd(outs, gain, *, name):
    s = outs[0].shape[0]
    widths = [o.shape[1] for o in outs]
    total = sum(widths)
    tm = 512

    def body(*refs):
        o_refs, g_ref, m_ref = refs[:-2], refs[-2], refs[-1]
        off = 0
        for o_ref, w in zip(o_refs, widths):
            xv = o_ref[...]
            r = lax.rsqrt(jnp.mean(xv * xv, axis=-1, keepdims=True) + EPS)
            m_ref[:, off:off + w] = (xv * r * g_ref[:, off:off + w]).astype(BF16)
            off += w

    return pl.pallas_call(
        body, name=name, grid=(s // tm,),
        in_specs=[pl.BlockSpec((tm, w), lambda i: (i, 0)) for w in widths] + [pl.BlockSpec((1, total), lambda i: (0, 0))],
        out_specs=pl.BlockSpec((tm, total), lambda i: (i, 0)),
        out_shape=jax.ShapeDtypeStruct((s, total), BF16),
        compiler_params=_params(("parallel",)),
    )(*outs, gain.reshape(1, total))


def mixnorm_bwd(dmix, outs, gain, *, name):
    s = outs[0].shape[0]
    widths = [o.shape[1] for o in outs]
    total = sum(widths)
    n = len(outs)
    tm = 256

    def body(*refs):
        dm_ref, o_refs, g_ref = refs[0], refs[1:1 + n], refs[1 + n]
        d_refs, dg_ref = refs[2 + n:2 + 2 * n], refs[2 + 2 * n]

        @pl.when(pl.program_id(0) == 0)
        def _():
            dg_ref[...] = jnp.zeros_like(dg_ref)

        off = 0
        for o_ref, d_ref, w in zip(o_refs, d_refs, widths):
            xv = o_ref[...]
            dhv = dm_ref[:, off:off + w]
            r = lax.rsqrt(jnp.mean(xv * xv, axis=-1, keepdims=True) + EPS)
            gd = dhv * g_ref[:, off:off + w]
            dot = jnp.mean(gd * xv, axis=-1, keepdims=True)
            d_ref[...] = r * gd - xv * (r * r * r * dot)
            dg_ref[:, off:off + w] += jnp.sum(dhv * (xv * r), axis=0, keepdims=True)
            off += w

    res = pl.pallas_call(
        body, name=name, grid=(s // tm,),
        in_specs=[pl.BlockSpec((tm, total), lambda i: (i, 0))]
        + [pl.BlockSpec((tm, w), lambda i: (i, 0)) for w in widths] + [pl.BlockSpec((1, total), lambda i: (0, 0))],
        out_specs=[pl.BlockSpec((tm, w), lambda i: (i, 0)) for w in widths] + [pl.BlockSpec((1, total), lambda i: (0, 0))],
        out_shape=[jax.ShapeDtypeStruct((s, w), F32) for w in widths] + [jax.ShapeDtypeStruct((1, total), F32)],
        compiler_params=_params(("arbitrary",)),
    )(dmix, *outs, gain.reshape(1, total))
    return res[:n], res[n][0]


NORM_CH = 512
FWD_TILES = 4
BWD_TILES = 2


def _lo_mask(shape):
    return lax.broadcasted_iota(I32, shape, 1) < HEAD_DIM


def _head_sum(x, lo):
    del lo
    i = lax.broadcasted_iota(I32, (LANES, LANES), 0) // HEAD_DIM
    j = lax.broadcasted_iota(I32, (LANES, LANES), 1) // HEAD_DIM
    return _split_dot(x, _twice(i == j))


def _head_stats(x, lo):
    return lax.rsqrt(_head_sum(x * x, lo) * (1.0 / HEAD_DIM) + EPS)


def _swap_halves(x):
    return pltpu.roll(x, HEAD_DIM, axis=1)


def _replicate_head(x, lo, use_lo_head):
    sw = _swap_halves(x)
    return jnp.where(use_lo_head, jnp.where(lo, x, sw), jnp.where(lo, sw, x))


def _tile_rows(i, s, d):
    nb = s // (BLOCK * d)
    r = i // nb
    b = i % nb
    start = r + (BLOCK * d) * b
    prev = start - (BLOCK * d) * jnp.minimum(b, 1)
    return start, prev, b > 0


def _rows(ref, start, d):
    if d == 1:
        return ref[pl.ds(pl.multiple_of(start, BLOCK), BLOCK), :]
    return ref[pl.ds(start, BLOCK, stride=d), :]


def _set_rows(ref, start, d, val):
    if d == 1:
        ref[pl.ds(pl.multiple_of(start, BLOCK), BLOCK), :] = val
    else:
        ref[pl.ds(start, BLOCK, stride=d), :] = val


def banded_fwd(proj, qb0, kb0, vb0, n_slabs, gq, gk, bias, dils, sinks, gqa, *, name):
    s = proj.shape[0]
    nbr = len(dils)
    nt = s // BLOCK
    nch = s // NORM_CH
    has_sink = sinks is not None

    def body(*refs):
        q_ref, k_ref, v_ref, gq_ref, gk_ref, b_ref = refs[:6]
        rest = refs[6:]
        if has_sink:
            sink_ref, rest = rest[0], rest[1:]
        out_ref, lse_ref, qn_s, kn_s, vv_s, o_s, l_s = rest
        p = pl.program_id(0)
        use_lo = (p // 2) == 0

        def prep(c, carry):
            rows = pl.ds(pl.multiple_of(c * NORM_CH, NORM_CH), NORM_CH)
            lo = _lo_mask((NORM_CH, LANES))
            qv, kv, vv = q_ref[rows, :], k_ref[rows, :], v_ref[rows, :]
            qn_s[rows, :] = qv * _head_stats(qv, lo) * gq_ref[...] * (HEAD_DIM ** -0.5)
            kn = kv * _head_stats(kv, lo) * gk_ref[...]
            if gqa:
                kn = _replicate_head(kn, lo, use_lo)
                vv = _replicate_head(vv, lo, use_lo)
            kn_s[rows, :] = kn
            vv_s[rows, :] = vv
            return carry

        lax.fori_loop(0, nch, prep, 0)

        lo = _lo_mask((BLOCK, LANES))
        hms = [lo, jnp.logical_not(lo)]
        heads, tiles = range(2), range(FWD_TILES)
        for br, d in enumerate(dils):
            def step(ii, carry, br=br, d=d):
                pos = [_tile_rows(ii * FWD_TILES + u, s, d) for u in tiles]
                kc = [carry[0]] + [_rows(kn_s, pos[u][0], d).astype(BF16) for u in tiles]
                vc = [carry[1]] + [_rows(vv_s, pos[u][0], d).astype(BF16) for u in tiles]
                kcat = [jnp.concatenate([kc[u], kc[u + 1]], axis=0) for u in tiles]
                vcat = [jnp.concatenate([vc[u], vc[u + 1]], axis=0) for u in tiles]
                qt = [_rows(qn_s, pos[u][0], d) for u in tiles]
                sc = [[_dot_nt(jnp.where(hms[h], qt[u], 0.0).astype(BF16), kcat[u])
                       + b_ref[br, jnp.where(pos[u][2], 0, 1), h] for h in heads] for u in tiles]
                m = [[jnp.max(sc[u][h], axis=1, keepdims=True) for h in heads] for u in tiles]
                pe = [[jnp.exp(sc[u][h] - m[u][h]) for h in heads] for u in tiles]
                den = [[jnp.sum(pe[u][h], axis=1, keepdims=True) for h in heads] for u in tiles]
                o = [[_dot(pe[u][h].astype(BF16), vcat[u]) * (1.0 / den[u][h]) for h in heads] for u in tiles]
                for u in tiles:
                    _set_rows(o_s.at[br], pos[u][0], d, jnp.where(lo, o[u][0], o[u][1]))
                    _set_rows(l_s.at[br], pos[u][0], d,
                              jnp.where(lo, m[u][0] + jnp.log(den[u][0]), m[u][1] + jnp.log(den[u][1])))
                return kc[-1], vc[-1]

            none_yet = jnp.zeros((BLOCK, LANES), BF16)
            lax.fori_loop(0, nt // FWD_TILES, step, (none_yet, none_yet))

        def combine(c, carry):
            rows = pl.ds(pl.multiple_of(c * NORM_CH, NORM_CH), NORM_CH)
            ls = [l_s[br, rows, :] for br in range(nbr)]
            mx = functools.reduce(jnp.maximum, ls)
            if has_sink:
                mx = jnp.maximum(mx, sink_ref[...])
            tot = functools.reduce(jnp.add, [jnp.exp(l - mx) for l in ls])
            if has_sink:
                tot = tot + jnp.exp(sink_ref[...] - mx)
            lse = mx + jnp.log(tot)
            acc = jnp.exp(ls[0] - lse) * o_s[0, rows, :]
            for br in range(1, nbr):
                acc = acc + jnp.exp(ls[br] - lse) * o_s[br, rows, :]
            out_ref[rows, :] = acc
            lse_ref[rows, :] = lse
            return carry

        lax.fori_loop(0, nch, combine, 0)

    slab = lambda b0, shared: pl.BlockSpec((s, LANES), (lambda p: (0, b0)) if shared else (lambda p: (0, b0 + p)),
                                           pipeline_mode=pl.Buffered(1))
    vec = pl.BlockSpec((1, LANES), lambda p: (0, 0))
    in_specs = [slab(qb0, False), slab(kb0, gqa), slab(vb0, gqa), vec, vec,
                pl.BlockSpec((nbr, 2, 2, BLOCK, 2 * BLOCK), lambda p: (0, 0, p, 0, 0))]
    args = [proj, proj, proj, gq.reshape(1, LANES), gk.reshape(1, LANES), bias]
    if has_sink:
        in_specs.append(pl.BlockSpec((None, 1, LANES), lambda p: (p, 0, 0)))
        args.append(sinks)
    w = LANES * n_slabs
    return pl.pallas_call(
        body, name=name, grid=(n_slabs,),
        in_specs=in_specs,
        out_specs=[pl.BlockSpec((s, LANES), lambda p: (0, p)), pl.BlockSpec((s, LANES), lambda p: (0, p))],
        out_shape=[jax.ShapeDtypeStruct((s, w), F32), jax.ShapeDtypeStruct((s, w), F32)],
        scratch_shapes=[pltpu.VMEM((s, LANES), F32), pltpu.VMEM((s, LANES), F32), pltpu.VMEM((s, LANES), F32),
                        pltpu.VMEM((nbr, s, LANES), F32), pltpu.VMEM((nbr, s, LANES), F32)],
        compiler_params=_params(("parallel",), VMEM_LIMIT),
    )(*args)


def banded_bwd(proj, qb0, kb0, vb0, n_slabs, gq, gk, bias, dils, sinks, gqa, dout, out, lse, dproj, *, name):
    s = proj.shape[0]
    nbr = len(dils)
    nt = s // BLOCK
    nch = s // NORM_CH
    has_sink = sinks is not None
    scale = HEAD_DIM ** -0.5

    def body(*refs):
        q_ref, k_ref, v_ref, gq_ref, gk_ref, b_ref, do_ref, o_ref, lse_ref = refs[:9]
        rest = refs[9:]
        if has_sink:
            sink_ref, rest = rest[0], rest[1:]
        dproj_ref, db_ref, dgq_ref, dgk_ref = rest[1:5]
        rest = rest[5:]
        if has_sink:
            dsink_ref, rest = rest[0], rest[1:]
        qn_s, kn_s, vv_s, dl_s, dqn_s, dkn_s, dvv_s, dq_ref, dk_ref, dv_ref, stage, sems = rest
        p = pl.program_id(0)
        use_lo = (p // 2) == 0

        def prep(c, carry):
            rows = pl.ds(pl.multiple_of(c * NORM_CH, NORM_CH), NORM_CH)
            lo = _lo_mask((NORM_CH, LANES))
            qv, kv, vv = q_ref[rows, :], k_ref[rows, :], v_ref[rows, :]
            qn_s[rows, :] = qv * _head_stats(qv, lo) * gq_ref[...] * scale
            kn = kv * _head_stats(kv, lo) * gk_ref[...]
            if gqa:
                kn = _replicate_head(kn, lo, use_lo)
                vv = _replicate_head(vv, lo, use_lo)
            kn_s[rows, :] = kn
            vv_s[rows, :] = vv
            delta = _head_sum(do_ref[rows, :] * o_ref[rows, :], lo)
            odd = lax.broadcasted_iota(I32, (NORM_CH, LANES), 1) % 2 == 1
            dl_s[rows, :] = jnp.where(odd, delta, lse_ref[rows, :])
            z = jnp.zeros((NORM_CH, LANES), F32)
            dqn_s[rows, :] = z
            dkn_s[rows, :] = z
            dvv_s[rows, :] = z
            if has_sink:
                ps = jnp.exp(sink_ref[...] - lse_ref[rows, :])
                return carry - jnp.sum(ps * delta, axis=0, keepdims=True)
            return carry

        dsink = lax.fori_loop(0, nch, prep, jnp.zeros((1, LANES), F32))
        if has_sink:
            dsink_ref[...] = jnp.broadcast_to(dsink, (8, LANES))

        lo = _lo_mask((BLOCK, LANES))
        hms = [lo, jnp.logical_not(lo)]
        heads, tiles = range(2), range(BWD_TILES)
        for br, d in enumerate(dils):
            db_ref[br] = jnp.zeros((2, BLOCK, 2 * BLOCK), F32)

            def step(ii, carry, br=br, d=d):
                pos = [_tile_rows(ii * BWD_TILES + u, s, d) for u in tiles]
                kc = [carry[0]] + [_rows(kn_s, pos[u][0], d).astype(BF16) for u in tiles]
                vc = [carry[1]] + [_rows(vv_s, pos[u][0], d).astype(BF16) for u in tiles]
                kcat = [jnp.concatenate([kc[u], kc[u + 1]], axis=0) for u in tiles]
                vcat = [jnp.concatenate([vc[u], vc[u + 1]], axis=0) for u in tiles]
                qt = [_rows(qn_s, pos[u][0], d) for u in tiles]
                dot_ = [_rows(do_ref, pos[u][0], d) for u in tiles]
                st_t = [_rows(dl_s, pos[u][0], d) for u in tiles]
                qh = [[jnp.where(hms[h], qt[u], 0.0).astype(BF16) for h in heads] for u in tiles]
                doh = [[jnp.where(hms[h], dot_[u], 0.0).astype(BF16) for h in heads] for u in tiles]
                sc = [[_dot_nt(qh[u][h], kcat[u]) + b_ref[br, jnp.where(pos[u][2], 0, 1), h] for h in heads]
                      for u in tiles]
                dp = [[_dot_nt(doh[u][h], vcat[u]) for h in heads] for u in tiles]
                lane0 = [0, HEAD_DIM]
                pr = [[jnp.exp(sc[u][h] - st_t[u][:, lane0[h]:lane0[h] + 1]) for h in heads] for u in tiles]
                dlog = [[pr[u][h] * (dp[u][h] - st_t[u][:, lane0[h] + 1:lane0[h] + 2]) for h in heads] for u in tiles]
                for h in heads:
                    db_ref[br, h] += functools.reduce(jnp.add, [dlog[u][h] for u in tiles])
                dlb = [[dlog[u][h].astype(BF16) for h in heads] for u in tiles]
                prb = [[pr[u][h].astype(BF16) for h in heads] for u in tiles]
                dq_t = [jnp.where(lo, _dot(dlb[u][0], kcat[u]), _dot(dlb[u][1], kcat[u])) * scale for u in tiles]
                rows2 = lambda x: jnp.concatenate(x, axis=0)
                dk_t = [_dot_tn(rows2(dlb[u]), rows2(qh[u])) for u in tiles]
                dv_t = [_dot_tn(rows2(prb[u]), rows2(doh[u])) for u in tiles]
                for u in tiles:
                    start, prev = pos[u][0], pos[u][1]
                    _set_rows(dqn_s, start, d, _rows(dqn_s, start, d) + dq_t[u])
                    _set_rows(dkn_s, prev, d, _rows(dkn_s, prev, d) + dk_t[u][:BLOCK])
                    _set_rows(dkn_s, start, d, _rows(dkn_s, start, d) + dk_t[u][BLOCK:])
                    _set_rows(dvv_s, prev, d, _rows(dvv_s, prev, d) + dv_t[u][:BLOCK])
                    _set_rows(dvv_s, start, d, _rows(dvv_s, start, d) + dv_t[u][BLOCK:])
                return kc[-1], vc[-1]

            none_yet = jnp.zeros((BLOCK, LANES), BF16)
            lax.fori_loop(0, nt // BWD_TILES, step, (none_yet, none_yet))

        if gqa:
            @pl.when(p == 0)
            def _():
                dk_ref[...] = jnp.zeros_like(dk_ref)
                dv_ref[...] = jnp.zeros_like(dv_ref)

        def finish(c, carry):
            dgq, dgk = carry
            rows = pl.ds(pl.multiple_of(c * NORM_CH, NORM_CH), NORM_CH)
            lo = _lo_mask((NORM_CH, LANES))

            def norm_bwd(xv, dn, g_ref):
                r = _head_stats(xv, lo)
                gd = dn * g_ref[...]
                dot = _head_sum(gd * xv, lo) * (1.0 / HEAD_DIM)
                return r * gd - xv * (r * r * r * dot), dn * (xv * r)

            dq, gq_part = norm_bwd(q_ref[rows, :], dqn_s[rows, :], gq_ref)
            dq_ref[rows, :] = dq
            dgq = dgq + jnp.sum(gq_part, axis=0, keepdims=True)
            kv, dkn, dvv = k_ref[rows, :], dkn_s[rows, :], dvv_s[rows, :]
            if gqa:
                kv = _replicate_head(kv, lo, use_lo)
                dkn = dkn + _swap_halves(dkn)
                dvv = dvv + _swap_halves(dvv)
                lane = lax.broadcasted_iota(I32, (NORM_CH, LANES), 1)
                mine = (lane // HEAD_DIM) == (p // 2)
                dk, gk_part = norm_bwd(kv, dkn, gk_ref)
                dk_ref[rows, :] += jnp.where(mine, dk, 0.0)
                dv_ref[rows, :] += jnp.where(mine, dvv, 0.0)
                gk_part = jnp.where(lo, gk_part, 0.0)
            else:
                dk, gk_part = norm_bwd(kv, dkn, gk_ref)
                dk_ref[rows, :] = dk
                dv_ref[rows, :] = dvv
            dgk = dgk + jnp.sum(gk_part, axis=0, keepdims=True)
            return dgq, dgk

        z = jnp.zeros((1, LANES), F32)
        dgq, dgk = lax.fori_loop(0, nch, finish, (z, z))
        dgq_ref[...] = jnp.broadcast_to(dgq, (8, LANES))
        dgk_ref[...] = jnp.broadcast_to(dgk, (8, LANES))
        if gqa:
            _store_slabs((dq_ref,), stage, dproj_ref, sems, (qb0 + p,))

            @pl.when(p == n_slabs - 1)
            def _():
                _store_slabs((dk_ref, dv_ref), stage, dproj_ref, sems, (kb0, vb0))
        else:
            _store_slabs((dq_ref, dk_ref, dv_ref), stage, dproj_ref, sems, (qb0 + p, kb0 + p, vb0 + p))

    def slab_of(width_blocks, b0, shared):
        return pl.BlockSpec((s, LANES), (lambda p: (0, b0)) if shared else (lambda p: (0, b0 + p)),
                            pipeline_mode=pl.Buffered(1))

    vec = pl.BlockSpec((1, LANES), lambda p: (0, 0))
    own = pl.BlockSpec((s, LANES), lambda p: (0, p), pipeline_mode=pl.Buffered(1))
    in_specs = [slab_of(0, qb0, False), slab_of(0, kb0, gqa), slab_of(0, vb0, gqa), vec, vec,
                pl.BlockSpec((nbr, 2, 2, BLOCK, 2 * BLOCK), lambda p: (0, 0, p, 0, 0)), own, own, own]
    args = [proj, proj, proj, gq.reshape(1, LANES), gk.reshape(1, LANES), bias, dout, out, lse]
    if has_sink:
        in_specs.append(pl.BlockSpec((None, 1, LANES), lambda p: (p, 0, 0)))
        args.append(sinks)
    held = pl.BlockSpec(memory_space=pl.ANY)
    in_specs.append(held)
    args.append(dproj)
    part = pl.BlockSpec((None, 8, LANES), lambda p: (p, 0, 0))
    out_specs = [held, pl.BlockSpec((nbr, 2, BLOCK, 2 * BLOCK), lambda p: (0, p, 0, 0)), part, part]
    out_shape = [jax.ShapeDtypeStruct(dproj.shape, dproj.dtype),
                 jax.ShapeDtypeStruct((nbr, 2 * n_slabs, BLOCK, 2 * BLOCK), F32),
                 jax.ShapeDtypeStruct((n_slabs, 8, LANES), F32), jax.ShapeDtypeStruct((n_slabs, 8, LANES), F32)]
    if has_sink:
        out_specs.append(part)
        out_shape.append(jax.ShapeDtypeStruct((n_slabs, 8, LANES), F32))
    res = pl.pallas_call(
        body, name=name, grid=(n_slabs,),
        in_specs=in_specs, out_specs=out_specs, out_shape=out_shape,
        input_output_aliases={len(args) - 1: 0},
        scratch_shapes=[pltpu.VMEM((s, LANES), F32) for _ in range(10)]
        + [pltpu.VMEM((3, s, LANES), BF16), pltpu.SemaphoreType.DMA((3,))],
        compiler_params=_params(("arbitrary",), VMEM_LIMIT),
    )(*args)
    outs = [res[0], res[1], res[2][:, 0, :], res[3][:, 0, :]]
    if has_sink:
        outs.append(res[4][:, 0, :])
    return outs


def bias_bwd(dbias, buckets, *, name):
    nbr, h = dbias.shape[:2]

    def body(db_ref, bk_ref, o_ref):
        lane = lax.broadcasted_iota(I32, (1, LANES), 1)
        acc = jnp.zeros((1, LANES), F32)
        for b in range(N_BUCKETS):
            tot = jnp.zeros((1, 1), F32)
            for br in range(nbr):
                sel = jnp.where(bk_ref[br] == b, db_ref[br], 0.0)
                tot = tot + jnp.sum(jnp.sum(sel, axis=0, keepdims=True), axis=1, keepdims=True)
            acc = jnp.where(lane == b, tot, acc)
        o_ref[...] = jnp.broadcast_to(acc, (8, LANES))

    res = pl.pallas_call(
        body, name=name, grid=(h,),
        in_specs=[pl.BlockSpec((nbr, None, BLOCK, 2 * BLOCK), lambda i: (0, i, 0, 0)),
                  pl.BlockSpec((nbr, BLOCK, 2 * BLOCK), lambda i: (0, 0, 0))],
        out_specs=pl.BlockSpec((None, 8, LANES), lambda i: (i, 0, 0)),
        out_shape=jax.ShapeDtypeStruct((h, 8, LANES), F32),
        compiler_params=_params(("parallel",)),
    )(dbias, buckets)
    return res[:, 0, :N_BUCKETS].T


SB_KG = 512
SB_QT = 2


def _softplus(z):
    return jnp.maximum(z, 0.0) + jnp.log(1.0 + jnp.exp(-jnp.abs(z)))


def _twice(t):
    t = t.astype(BF16)
    return jnp.concatenate([t, t], axis=0)


def _split_dot(x, t2):
    hi = x.astype(BF16)
    lo = (x - hi.astype(F32)).astype(BF16)
    return _dot(jnp.concatenate([hi, lo], axis=1), t2)


def sb_fwd(proj, qb0, kb0, vb0, n_slabs, *, name):
    s = proj.shape[0]
    nq = s // BLOCK
    nch = s // NORM_CH
    scale = HEAD_DIM ** -0.5

    def body(q_ref, k_ref, v_ref, o_ref, tot_ref, qlo_s, qhi_s, k_s, v_s):
        def prep(c, carry):
            rows = pl.ds(pl.multiple_of(c * NORM_CH, NORM_CH), NORM_CH)
            lo = _lo_mask((NORM_CH, LANES))
            qv = q_ref[rows, :] * scale
            qlo_s[rows, :] = jnp.where(lo, qv, 0.0).astype(BF16)
            qhi_s[rows, :] = jnp.where(lo, 0.0, qv).astype(BF16)
            k_s[rows, :] = k_ref[rows, :].astype(BF16)
            v_s[rows, :] = v_ref[rows, :].astype(BF16)
            return carry

        lax.fori_loop(0, nch, prep, 0)

        row = lax.broadcasted_iota(I32, (BLOCK, BLOCK), 0)
        col = lax.broadcasted_iota(I32, (BLOCK, BLOCK), 1)
        lo = col < HEAD_DIM
        t_ge = _twice(row >= col)
        rowg = lax.broadcasted_iota(I32, (BLOCK, SB_KG), 0)
        colg = lax.broadcasted_iota(I32, (BLOCK, SB_KG), 1)

        nsub = SB_KG // BLOCK
        chains = range(2 * SB_QT)
        nc = len(chains)

        def qloop(qs, phase):
            q0 = pl.multiple_of(qs * (SB_QT * BLOCK), SB_QT * BLOCK)
            qh = [(qlo_s, qhi_s)[i % 2][pl.ds(q0 + (i // 2) * BLOCK, BLOCK), :] for i in chains]
            gd = (qs * SB_QT) // nsub

            def logits(gi):
                k0 = pl.multiple_of(gi * SB_KG, SB_KG)
                kg = k_s[pl.ds(k0, SB_KG), :]
                return [_dot_nt(qh[i], kg) for i in chains]

            def group(gi, st, masks, npiece=nsub):
                k0 = pl.multiple_of(gi * SB_KG, SB_KG)
                vg = v_s[pl.ds(k0, npiece * BLOCK), :]
                c, o, z = list(st[:nc]), st[nc:2 * nc], st[2 * nc:]
                z_next = logits(jnp.maximum(gi - 1, 0))
                piece = lambda x, j: x[:, j * BLOCK:(j + 1) * BLOCK]
                a = [[None] * npiece for _ in chains]
                for j in reversed(range(npiece)):
                    zj = [piece(z[i], j) for i in chains]
                    lrem = [-_softplus(zj[i]) for i in chains]
                    if masks is not None:
                        lrem = [jnp.where(piece(masks[i // 2], j), lrem[i], 0.0) for i in chains]
                    incl = [_split_dot(lrem[i], t_ge) for i in chains]
                    for i in chains:
                        aij = jnp.exp(zj[i] + (c[i] + incl[i]))
                        if masks is not None:
                            aij = jnp.where(piece(masks[i // 2], j), aij, 0.0)
                        a[i][j] = aij.astype(BF16)
                        c[i] = c[i] + incl[i][:, 0:1]
                o = [o[i] + _dot(jnp.concatenate(a[i], axis=1), vg) for i in chains]
                return (*c, *o, *z_next)

            zc = [jnp.zeros((BLOCK, 1), F32)] * nc
            zo = [jnp.zeros((BLOCK, LANES), F32)] * nc
            masks = [(gd * SB_KG + colg) < (q0 + t * BLOCK + rowg) for t in range(SB_QT)]
            st = group(gd, (*zc, *zo, *logits(gd)), masks, (phase + 1) * SB_QT)
            st = lax.fori_loop(0, gd, lambda t, st: group(gd - 1 - t, st, None), st)
            for t in range(SB_QT):
                rows = pl.ds(q0 + t * BLOCK, BLOCK)
                o_ref[rows, :] = jnp.where(lo, st[nc + 2 * t], st[nc + 2 * t + 1])
                tot_ref[rows, :] = jnp.where(lo, st[2 * t], st[2 * t + 1])

        steps_per_group = nsub // SB_QT

        def per_group(g, carry):
            for phase in range(steps_per_group):
                qloop(g * steps_per_group + phase, phase)
            return carry

        lax.fori_loop(0, nq // nsub, per_group, 0)

    slab = lambda b0: pl.BlockSpec((s, LANES), lambda p: (0, b0 + p), pipeline_mode=pl.Buffered(1))
    w = LANES * n_slabs
    return pl.pallas_call(
        body, name=name, grid=(n_slabs,),
        in_specs=[slab(qb0), slab(kb0), slab(vb0)],
        out_specs=[pl.BlockSpec((s, LANES), lambda p: (0, p)), pl.BlockSpec((s, LANES), lambda p: (0, p))],
        out_shape=[jax.ShapeDtypeStruct((s, w), F32), jax.ShapeDtypeStruct((s, w), F32)],
        scratch_shapes=[pltpu.VMEM((s, LANES), BF16) for _ in range(4)],
        compiler_params=_params(("parallel",), VMEM_LIMIT),
    )(proj, proj, proj)


def _store_slabs(slabs, stage, dproj_ref, sems, blocks):
    s = stage.shape[1]

    def cast(c, carry):
        rows = pl.ds(pl.multiple_of(c * NORM_CH, NORM_CH), NORM_CH)
        for i, slab in enumerate(slabs):
            stage[i, rows, :] = slab[rows, :].astype(BF16)
        return carry

    lax.fori_loop(0, s // NORM_CH, cast, 0)
    copies = [pltpu.make_async_copy(stage.at[i], dproj_ref.at[:, pl.ds(pl.multiple_of(b * LANES, LANES), LANES)],
                                    sems.at[i]) for i, b in enumerate(blocks)]
    for cp in copies:
        cp.start()
    for cp in copies:
        cp.wait()


def sb_bwd(proj, qb0, kb0, vb0, n_slabs, dout, tot, dproj, *, name):
    s = proj.shape[0]
    nq = s // BLOCK
    nch = s // NORM_CH
    nsub = SB_KG // BLOCK
    scale = HEAD_DIM ** -0.5

    def body(q_ref, k_ref, v_ref, do_ref, tot_ref, dproj_in, dproj_ref,
             qlo_s, qhi_s, k_s, v_s, dlo_s, dhi_s, dq_ref, dk_ref, dv_ref, stage, sems):
        del dproj_in
        def prep(c, carry):
            rows = pl.ds(pl.multiple_of(c * NORM_CH, NORM_CH), NORM_CH)
            lo = _lo_mask((NORM_CH, LANES))
            qv = q_ref[rows, :] * scale
            dv = do_ref[rows, :]
            qlo_s[rows, :] = jnp.where(lo, qv, 0.0).astype(BF16)
            qhi_s[rows, :] = jnp.where(lo, 0.0, qv).astype(BF16)
            dlo_s[rows, :] = jnp.where(lo, dv, 0.0).astype(BF16)
            dhi_s[rows, :] = jnp.where(lo, 0.0, dv).astype(BF16)
            k_s[rows, :] = k_ref[rows, :].astype(BF16)
            v_s[rows, :] = v_ref[rows, :].astype(BF16)
            z = jnp.zeros((NORM_CH, LANES), F32)
            dk_ref[rows, :] = z
            dv_ref[rows, :] = z
            return carry

        lax.fori_loop(0, nch, prep, 0)

        row = lax.broadcasted_iota(I32, (BLOCK, BLOCK), 0)
        col = lax.broadcasted_iota(I32, (BLOCK, BLOCK), 1)
        lo = col < HEAD_DIM
        t_le = _twice(row <= col)
        rowg = lax.broadcasted_iota(I32, (BLOCK, SB_KG), 0)
        colg = lax.broadcasted_iota(I32, (BLOCK, SB_KG), 1)

        piece = lambda x, j: x[:, j * BLOCK:(j + 1) * BLOCK]
        chains = range(2 * SB_QT)
        nc = len(chains)

        def prefixes(x):
            return [[_split_dot(piece(x[i], j), t_le) for j in range(x[i].shape[1] // BLOCK)] for i in chains]

        def chain(pre, run, total=None):
            out = []
            for pj in pre:
                out.append(run + pj if total is None else total - run - pj)
                run = run + pj[:, BLOCK - 1:BLOCK]
            return jnp.concatenate(out, axis=1), run

        def qloop(qs, phase):
            q0 = pl.multiple_of(qs * (SB_QT * BLOCK), SB_QT * BLOCK)
            tile = lambda ref, i: ref[pl.ds(q0 + (i // 2) * BLOCK, BLOCK), :]
            qh = [tile((qlo_s, qhi_s)[i % 2], i) for i in chains]
            doh = [tile((dlo_s, dhi_s)[i % 2], i) for i in chains]
            tots = [tile(tot_ref, i)[:, (i % 2) * HEAD_DIM:(i % 2) * HEAD_DIM + 1] for i in chains]
            gd = (qs * SB_QT) // nsub

            def logits(gi):
                kg = k_s[pl.ds(pl.multiple_of(gi * SB_KG, SB_KG), SB_KG), :]
                return [_dot_nt(qh[i], kg) for i in chains]

            def group(gi, st, masks, npiece=nsub):
                k0 = pl.multiple_of(gi * SB_KG, SB_KG)
                wide = npiece * BLOCK
                kg, vg = k_s[pl.ds(k0, wide), :], v_s[pl.ds(k0, wide), :]
                cp, cg, dq = list(st[:nc]), list(st[nc:2 * nc]), st[2 * nc:2 * nc + SB_QT]
                z = [zi[:, :wide] for zi in st[2 * nc + SB_QT:]]
                masked = lambda x, i: x if masks is None else jnp.where(masks[i // 2][:, :wide], x, 0.0)
                z_next = logits(jnp.minimum(gi + 1, gd))
                da = [_dot_nt(doh[i], vg) for i in chains]
                sp = [_softplus(z[i]) for i in chains]
                lrem = [masked(-sp[i], i) for i in chains]
                pre = prefixes(lrem)
                e, a, g = [], [], []
                for i in chains:
                    suffix, cp[i] = chain(pre[i], cp[i], tots[i])
                    e.append(z[i] - sp[i])
                    a.append(masked(jnp.exp(e[i] + suffix), i))
                    g.append(a[i] * da[i])
                gpre = prefixes(g)
                dz = []
                for i in chains:
                    ginc, cg[i] = chain(gpre[i], cg[i])
                    dz.append(masked(g[i] - jnp.exp(e[i]) * ginc, i).astype(BF16))
                ab = [a[i].astype(BF16) for i in chains]
                dq = [dq[t] + jnp.where(lo, _dot(dz[2 * t], kg), _dot(dz[2 * t + 1], kg)) for t in range(SB_QT)]
                rows_of = lambda x: jnp.concatenate(x, axis=0)
                dk_ref[pl.ds(k0, wide), :] += _dot_tn(rows_of(dz), rows_of(qh))
                dv_ref[pl.ds(k0, wide), :] += _dot_tn(rows_of(ab), rows_of(doh))
                return (*cp, *cg, *dq, *z_next)

            zc = [jnp.zeros((BLOCK, 1), F32)] * (2 * nc)
            zq = [jnp.zeros((BLOCK, LANES), F32)] * SB_QT
            st = lax.fori_loop(0, gd, lambda gi, st: group(gi, st, None), (*zc, *zq, *logits(0)))
            st = group(gd, st, [(gd * SB_KG + colg) < (q0 + t * BLOCK + rowg) for t in range(SB_QT)],
                       (phase + 1) * SB_QT)
            for t in range(SB_QT):
                dq_ref[pl.ds(q0 + t * BLOCK, BLOCK), :] = st[2 * nc + t] * scale

        steps_per_group = nsub // SB_QT

        def per_group(g, carry):
            for phase in range(steps_per_group):
                qloop(g * steps_per_group + phase, phase)
            return carry

        lax.fori_loop(0, nq // nsub, per_group, 0)
        p = pl.program_id(0)
        _store_slabs((dq_ref, dk_ref, dv_ref), stage, dproj_ref, sems, (qb0 + p, kb0 + p, vb0 + p))

    slab = lambda b0: pl.BlockSpec((s, LANES), lambda p: (0, b0 + p), pipeline_mode=pl.Buffered(1))
    own = pl.BlockSpec((s, LANES), lambda p: (0, p), pipeline_mode=pl.Buffered(1))
    held = pl.BlockSpec(memory_space=pl.ANY)
    return pl.pallas_call(
        body, name=name, grid=(n_slabs,),
        in_specs=[slab(qb0), slab(kb0), slab(vb0), own, own, held],
        out_specs=held, out_shape=jax.ShapeDtypeStruct(dproj.shape, dproj.dtype),
        input_output_aliases={5: 0},
        scratch_shapes=[pltpu.VMEM((s, LANES), BF16) for _ in range(6)]
        + [pltpu.VMEM((s, LANES), F32) for _ in range(3)]
        + [pltpu.VMEM((3, s, LANES), BF16), pltpu.SemaphoreType.DMA((3,))],
        compiler_params=_params(("arbitrary",), VMEM_LIMIT),
    )(proj, proj, proj, dout, tot, dproj)


def _place():
    x, y, c = lax.axis_index("x"), lax.axis_index("y"), lax.axis_index("c")
    return x, y, c


def gather_small(v, *, name):
    m_per, n = v.shape

    def body(x_ref, out_ref, send_sems, recv_sems, local_sem):
        x, y, c = _place()
        me, sibling = (x, y, c), (x, y, 1 - c)
        chips = [(1 - x, y), (x, 1 - y), (1 - x, 1 - y)]

        def rows(px, py, pc):
            return out_ref.at[pl.ds((4 * px + 2 * py + pc) * m_per, m_per), :]

        def copy(k, block, to, src=None):
            return pltpu.make_async_remote_copy(
                src_ref=rows(*block) if src is None else src, dst_ref=rows(*block),
                send_sem=send_sems.at[k], recv_sem=recv_sems.at[k], device_id=to, device_id_type=MESH)

        mine = pltpu.make_async_copy(x_ref, rows(*me), local_sem)
        mine.start()
        first = [copy(0, me, sibling, src=x_ref)]
        first += [copy(1 + j, me, (*chip, c), src=x_ref) for j, chip in enumerate(chips)]
        for cp in first:
            cp.start()
        passed = [copy(4 + j, (*chip, c), sibling) for j, chip in enumerate(chips)]
        for j, chip in enumerate(chips):
            copy(1 + j, (*chip, c), me).wait_recv()
            passed[j].start()
        copy(0, sibling, me).wait_recv()
        for j, chip in enumerate(chips):
            copy(4 + j, (*chip, 1 - c), me).wait_recv()
        for cp in first + passed:
            cp.wait_send()
        mine.wait()

    return pl.pallas_call(
        body, name=name,
        out_shape=jax.ShapeDtypeStruct((N_DEV * m_per, n), v.dtype),
        in_specs=[pl.BlockSpec(memory_space=pltpu.VMEM)],
        out_specs=pl.BlockSpec(memory_space=pltpu.VMEM),
        scratch_shapes=[pltpu.SemaphoreType.DMA((7,)), pltpu.SemaphoreType.DMA((7,)), pltpu.SemaphoreType.DMA],
        compiler_params=_params(None, VMEM_LIMIT),
    )(v)


_HBM = pl.BlockSpec(memory_space=pltpu.HBM)
_SEM = pl.BlockSpec(memory_space=pltpu.SEMAPHORE)
_EFFECT = pltpu.SideEffectType.DATAFLOW_SIDE_EFFECTING


def _peer_copies(src_refs, land_refs, send_sems, recv_sems, per_dest):
    x, y, c = _place()
    me = 4 * x + 2 * y + c
    copies = []
    for src, land, ssem, rsem in zip(src_refs, land_refs, send_sems, recv_sems):
        for k in (1, 2, 4, 3, 5, 6, 7):
            px, py, pc = x ^ (k >> 2 & 1), y ^ (k >> 1 & 1), c ^ (k & 1)
            copies.append(pltpu.make_async_remote_copy(
                src_ref=src.at[4 * px + 2 * py + pc] if per_dest else src, dst_ref=land.at[me],
                send_sem=ssem.at[k - 1], recv_sem=rsem.at[k - 1], device_id=(px, py, pc), device_id_type=MESH))
    return copies


def _own_copies(src_refs, land_refs, send_sems, per_dest):
    x, y, c = _place()
    me = 4 * x + 2 * y + c
    return [pltpu.make_async_copy(src.at[me] if per_dest else src, land.at[me], ssem.at[7])
            for src, land, ssem in zip(src_refs, land_refs, send_sems)]


def exchange_start(srcs, per_dest, *, name):
    n = len(srcs)
    lands = [lax.empty(a.shape if per_dest else (N_DEV,) + a.shape, a.dtype) for a in srcs]

    def body(*refs):
        src_refs, land_refs = refs[:n], refs[n:2 * n]
        send_sems, recv_sems = refs[2 * n:3 * n], refs[3 * n:4 * n]
        token = refs[-1]
        for cp in _peer_copies(src_refs, land_refs, send_sems, recv_sems, per_dest):
            cp.start()
        for cp in _own_copies(src_refs, land_refs, send_sems, per_dest):
            cp.start()
        token[...] = jnp.zeros_like(token)

    hbm = lambda a: pltpu.HBM(a.shape, a.dtype)
    res = pl.pallas_call(
        body, name=name,
        out_shape=(*[pltpu.SemaphoreType.DMA((8,))] * n, *[pltpu.SemaphoreType.DMA((7,))] * n,
                   *[hbm(a) for a in srcs], *[hbm(a) for a in lands], jax.ShapeDtypeStruct((8, LANES), F32)),
        in_specs=[_HBM] * (2 * n),
        out_specs=(*[_SEM] * (2 * n), *[_HBM] * (2 * n), pl.BlockSpec(memory_space=pltpu.VMEM)),
        input_output_aliases={i: 2 * n + i for i in range(2 * n)},
        compiler_params=pltpu.CompilerParams(has_side_effects=_EFFECT),
    )(*[pltpu.with_memory_space_constraint(a, pltpu.HBM) for a in (*srcs, *lands)])
    handles = [(res[a], res[n + a], res[2 * n + a], res[3 * n + a]) for a in range(n)]
    return handles, res[-1]


def exchange_wait(handles, per_dest, after, *, name):
    n = len(handles)

    def body(*refs):
        src_refs, land_refs = refs[:n], refs[n:2 * n]
        send_sems, recv_sems = refs[2 * n:3 * n], refs[3 * n:4 * n]
        for cp in _peer_copies(src_refs, land_refs, send_sems, recv_sems, per_dest):
            cp.wait_send()
            cp.wait_recv()
        for cp in _own_copies(src_refs, land_refs, send_sems, per_dest):
            cp.wait()

    srcs, lands = [h[2] for h in handles], [h[3] for h in handles]
    hbm = lambda a: pltpu.HBM(a.shape, a.dtype)
    res = pl.pallas_call(
        body, name=name,
        out_shape=(*[hbm(a) for a in srcs], *[hbm(a) for a in lands]),
        in_specs=[*[_HBM] * (2 * n), *[_SEM] * (2 * n), pl.BlockSpec(memory_space=pl.ANY)],
        out_specs=tuple([_HBM] * (2 * n)),
        input_output_aliases={i: i for i in range(2 * n)},
        compiler_params=pltpu.CompilerParams(has_side_effects=_EFFECT),
    )(*srcs, *lands, *[h[0] for h in handles], *[h[1] for h in handles], after)
    return res[n:]


def _adamw_math(w, g, m, v):
    m = ADAM_B1 * m + (1.0 - ADAM_B1) * g
    v = ADAM_B2 * v + (1.0 - ADAM_B2) * (g * g)
    m_hat = m / (1.0 - ADAM_B1 ** ADAM_STEP)
    v_hat = v / (1.0 - ADAM_B2 ** ADAM_STEP)
    delta = -ADAM_LR * (m_hat / (jnp.sqrt(v_hat) + ADAM_EPS) + ADAM_WD * w)
    return delta, m, v


def adamw_parts(parts, w, m, v, layer, outs, *, name):
    depth, r, cdim = w.shape
    n_parts = parts.shape[0]
    tr = _pick(r, [t for t in (512, 256, 128, 64, 32, 16) if t * cdim <= 256 * 1024])

    def body(p_ref, w_ref, m_ref, v_ref, g0, d0, nm0, nv0, g_ref, d_ref, nm_ref, nv_ref):
        g = p_ref[0].astype(F32)
        for q in range(1, n_parts):
            g = g + p_ref[q].astype(F32)
        delta, nm, nv = _adamw_math(w_ref[...], g, m_ref[...], v_ref[...])
        g_ref[...], d_ref[...], nm_ref[...], nv_ref[...] = g, delta, nm, nv

    t = pl.BlockSpec((None, tr, cdim), lambda i: (layer, i, 0))
    held = pl.BlockSpec(memory_space=pl.ANY)
    return pl.pallas_call(
        body, name=name, grid=(r // tr,),
        in_specs=[pl.BlockSpec((n_parts, tr, cdim), lambda i: (0, i, 0)), t, t, t, held, held, held, held],
        out_specs=[t, t, t, t],
        out_shape=[jax.ShapeDtypeStruct((depth, r, cdim), F32)] * 4,
        input_output_aliases={4: 0, 5: 1, 6: 2, 7: 3},
        compiler_params=_params(("parallel",), VMEM_LIMIT),
    )(parts, w, m, v, *outs)


def sum_devices(gathered, *, name):
    m_rows = gathered.shape[1]

    def body(ga_ref, g_ref):
        g = ga_ref[0]
        for dev in range(1, N_DEV):
            g = g + ga_ref[dev]
        g_ref[...] = g

    return pl.pallas_call(
        body, name=name, out_shape=jax.ShapeDtypeStruct((m_rows, LANES), F32),
        compiler_params=_params(None, VMEM_LIMIT),
    )(gathered)


def adamw_small(g, w, m, v, *, name):
    m_rows = w.shape[0]

    def body(g_ref, w_ref, m_ref, v_ref, d_ref, nm_ref, nv_ref):
        d_ref[...], nm_ref[...], nv_ref[...] = _adamw_math(w_ref[...], g_ref[...], m_ref[...], v_ref[...])

    return pl.pallas_call(
        body, name=name, out_shape=[jax.ShapeDtypeStruct((m_rows, LANES), F32)] * 3,
        compiler_params=_params(None, VMEM_LIMIT),
    )(g, w, m, v)


def _t5_bucket(dist):
    max_exact = N_BUCKETS // 2
    d = jnp.maximum(dist, 0)
    large = max_exact + (jnp.log(jnp.maximum(d, 1).astype(F32) / max_exact)
                         / math.log(T5_MAX_DIST / max_exact) * (N_BUCKETS - max_exact)).astype(I32)
    large = jnp.minimum(large, N_BUCKETS - 1)
    return jnp.where(d < max_exact, d, large)


def _rel():
    return jnp.arange(BLOCK)[:, None] + BLOCK - jnp.arange(2 * BLOCK)[None, :]


def _band_bias(table, dils, max_dists):
    rel = _rel()
    biases, buckets = [], []
    for d, md in zip(dils, max_dists):
        bk = _t5_bucket(rel * d)
        vis = (rel >= 0) & (rel <= md)
        looked_up = jnp.zeros((table.shape[1],) + rel.shape, F32)
        for b in range(N_BUCKETS):
            looked_up = jnp.where((bk == b)[None], table[b][:, None, None], looked_up)
        with_prev = jnp.where(vis[None], looked_up, NEG_INF)
        first = jnp.arange(2 * BLOCK)[None, None, :] >= BLOCK
        biases.append(jnp.stack([with_prev, jnp.where(first, with_prev, NEG_INF)]))
        buckets.append(bk.astype(I32))
    return jnp.stack(biases), jnp.stack(buckets)


def _pack(pieces, rows):
    flat = jnp.concatenate([p.reshape(-1) for p in pieces])
    return jnp.pad(flat, (0, rows * LANES - flat.shape[0])).reshape(rows, LANES)


def _unpack(packed, shapes):
    flat = packed.reshape(-1)
    out, off = [], 0
    for sh in shapes:
        n = math.prod(sh)
        out.append(flat[off:off + n].reshape(sh))
        off += n
    return out


def _tile2(g):
    return jnp.concatenate([g, g])


def kernel(x, attn_norm, w_in, a_q_gain, a_k_gain, a_sinks, c_q_gain, c_k_gain, rel_bias_table, mix_out_gain, w_out, ffn_norm, w_up, conv_w, conv_b, w_down, loss_target, m_attn_norm, m_w_in, m_a_q_gain, m_a_k_gain, m_a_sinks, m_c_q_gain, m_c_k_gain, m_rel_bias_table, m_mix_out_gain, m_w_out, m_ffn_norm, m_w_up, m_conv_w, m_conv_b, m_w_down, v_attn_norm, v_w_in, v_a_q_gain, v_a_k_gain, v_a_sinks, v_c_q_gain, v_c_k_gain, v_rel_bias_table, v_mix_out_gain, v_w_out, v_ffn_norm, v_w_up, v_conv_w, v_conv_b, v_w_down):
    depth, d_model, in_shard = w_in.shape
    ff2_shard = w_up.shape[2]
    s = x.shape[1]
    in_width, ff2 = N_DEV * in_shard, N_DEV * ff2_shard
    n_heads = d_model // HEAD_DIM
    ha, hb, hc = n_heads // 4, n_heads // 4, n_heads // 2
    sa, sb, sc = ha // 2, hb // 2, hc // 2
    kv_a = ha // 4
    assert kv_a == 2 and BLOCK == LANES
    cb_aq, cb_ak, cb_av = 0, sa, sa + 1
    cb_bq = sa + 2
    cb_bk, cb_bv = cb_bq + sb, cb_bq + 2 * sb
    cb_cq = cb_bq + 3 * sb
    cb_ck, cb_cv = cb_cq + sc, cb_cq + 2 * sc
    assert (cb_cv + sc) * LANES == in_width
    dev = 4 * lax.axis_index("x") + 2 * lax.axis_index("y") + lax.axis_index("c")

    per_array = 3
    wnames = ("w_in", "w_out", "w_up", "w_down", "conv_w")
    cols_to_rows = lambda g: jnp.transpose(g, (1, 0, 2)).reshape(g.shape[1], N_DEV * g.shape[2])
    whole = dict(w_in=cols_to_rows, w_up=cols_to_rows, conv_w=cols_to_rows,
                 w_out=lambda g: g.reshape(d_model, d_model), w_down=lambda g: g.reshape(ff2 // 2, d_model))
    gathers = {}
    token = jnp.zeros((8, LANES), F32)
    for l in range(depth):
        for gi, group in enumerate([[n] for n in wnames] if l < per_array else [wnames]):
            srcs = [(dict(w_in=w_in, w_out=w_out, w_up=w_up, w_down=w_down, conv_w=conv_w)[n][l] + token[0, 0])
                    .astype(F32 if n == "conv_w" else BF16) for n in group]
            handles, token = exchange_start(srcs, False, name=f"gather_start_{l}_{gi}")
            gathers.update({(l, n): h for n, h in zip(group, handles)})

    def gathered(l, names, after):
        landed = exchange_wait([gathers[l, n] for n in names], False, after,
                               name=f"gather_wait_{l}_{wnames.index(names[0])}")
        return {n: whole[n](g) for n, g in zip(names, landed)}

    bias_a, buckets_a = _band_bias(rel_bias_table[:, :ha], (1,), (WINDOW_A - 1,))
    bias_c, buckets_c = _band_bias(rel_bias_table[:, ha:], DILATIONS, (BLOCK,) * len(DILATIONS))

    xs = x[0]
    saved = []
    wi, wo, wu, wd, cw = ([None] * depth for _ in range(5))
    for l in range(depth):
        if l < per_array:
            need = lambda n, after, l=l: gathered(l, (n,), after)[n]
        else:
            layer_w = gathered(l, wnames, xs)
            need = lambda n, after: layer_w[n]
        wi[l] = need("w_in", token if l == 0 else xs)
        h1 = rmsnorm_fwd(xs, attn_norm[l], name="attn_norm_fwd")
        proj = matmul(h1, wi[l], name="in_proj")
        sinks = jnp.repeat(a_sinks[l], HEAD_DIM).reshape(sa, 1, LANES)
        gaq, gak = _tile2(a_q_gain[l]), _tile2(a_k_gain[l])
        gcq, gck = _tile2(c_q_gain[l]), _tile2(c_k_gain[l])
        out_a, lse_a = banded_fwd(proj, cb_aq, cb_ak, cb_av, sa, gaq, gak, bias_a, (1,), sinks, True, name="swa_fwd")
        out_b, tot_b = sb_fwd(proj, cb_bq, cb_bk, cb_bv, sb, name="stick_fwd")
        out_c, lse_c = banded_fwd(proj, cb_cq, cb_ck, cb_cv, sc, gcq, gck, bias_c, DILATIONS, None, False,
                                  name="dilated_fwd")
        mix = mixnorm_fwd([out_a, out_b, out_c], mix_out_gain[l], name="mix_norm_fwd")
        wo[l] = need("w_out", mix)
        x_mid = matmul(mix, wo[l], res=xs, name="out_proj")
        h2 = rmsnorm_fwd(x_mid, ffn_norm[l], name="ffn_norm_fwd")
        wu[l] = need("w_up", h2)
        p = matmul(h2, wu[l], name="up_proj")
        cw[l] = need("conv_w", p)
        act = ffn_act_fwd(p, cw[l], conv_b[l], name="ffn_act_fwd")
        wd[l] = need("w_down", act)
        x_out = matmul(act, wd[l], res=x_mid, name="down_proj")
        saved.append(dict(x_in=xs, h1=h1, proj=proj, out_a=out_a, lse_a=lse_a, out_b=out_b, tot_b=tot_b,
                          out_c=out_c, lse_c=lse_c, mix=mix, x_mid=x_mid, h2=h2, p=p, act=act,
                          sinks=sinks, gains=(gaq, gak, gcq, gck)))
        xs = x_out

    dx, dx_b, loss_part = loss_head(xs, loss_target[0], name="loss_head")

    small = {k: [None] * depth for k in ("attn_norm", "a_q_gain", "a_k_gain", "a_sinks", "c_q_gain", "c_k_gain",
                                         "mix_out_gain", "ffn_norm", "conv_w", "conv_b")}
    big = {k: [None] * depth for k in ("w_in", "w_out", "w_up", "w_down")}
    dbias_a = dbias_c = None
    scatters = {}
    token = jnp.zeros((8, LANES), F32)
    names_big = ("w_in", "w_out", "w_up", "w_down")

    def scatter(l, names):
        parts = [big[n][l] for n in names]
        handles, tok = exchange_start(parts, True, name=f"scatter_start_{l}_{names_big.index(names[0])}")
        scatters.update({(l, n): h for n, h in zip(names, handles)})
        return tok

    by_cols = lambda a: jnp.transpose(a.reshape(a.shape[0], N_DEV, a.shape[1] // N_DEV), (1, 0, 2))
    by_rows = lambda a: a.reshape(N_DEV, a.shape[0] // N_DEV, a.shape[1])
    for l in reversed(range(depth)):
        each = l == 0
        sv = saved[l]
        gaq, gak, gcq, gck = sv["gains"]
        da = matmul(dx_b, wd[l], trans_b=True, name="down_proj_dx")
        big["w_down"][l] = by_rows(matmul(sv["act"], dx_b, trans_a=True, out_dtype=BF16, name="down_proj_dw"))
        if each:
            token = scatter(l, ("w_down",))
        dp, small["conv_w"][l], small["conv_b"][l] = ffn_act_bwd(da, sv["p"], cw[l], conv_b[l] + token[0, 0],
                                                                 name="ffn_act_bwd")
        dh2 = matmul(dp, wu[l], trans_b=True, name="up_proj_dx")
        big["w_up"][l] = matmul(sv["h2"], dp, trans_a=True, out_dtype=BF16, col_blocks=N_DEV, name="up_proj_dw")
        if each:
            token = scatter(l, ("w_up",))
        dx_mid, dx_mid_b, small["ffn_norm"][l] = rmsnorm_bwd(dh2, sv["x_mid"], ffn_norm[l] + token[0, 0], dx,
                                                   name="ffn_norm_bwd")
        dmix = matmul(dx_mid_b, wo[l], trans_b=True, name="out_proj_dx")
        big["w_out"][l] = by_rows(matmul(sv["mix"], dx_mid_b, trans_a=True, out_dtype=BF16, name="out_proj_dw"))
        if each:
            token = scatter(l, ("w_out",))
        (d_oa, d_ob, d_oc), small["mix_out_gain"][l] = mixnorm_bwd(
            dmix, [sv["out_a"], sv["out_b"], sv["out_c"]], mix_out_gain[l] + token[0, 0], name="mix_norm_bwd")
        dproj = lax.empty((s, in_width), BF16)
        dproj, db_a, dgq_a, dgk_a, dsink = banded_bwd(
            sv["proj"], cb_aq, cb_ak, cb_av, sa, gaq, gak, bias_a, (1,), sv["sinks"], True,
            d_oa, sv["out_a"], sv["lse_a"], dproj, name="swa_bwd")
        dproj = sb_bwd(sv["proj"], cb_bq, cb_bk, cb_bv, sb, d_ob, sv["tot_b"], dproj, name="stick_bwd")
        dproj, db_c, dgq_c, dgk_c = banded_bwd(
            sv["proj"], cb_cq, cb_ck, cb_cv, sc, gcq, gck, bias_c, DILATIONS, None, False,
            d_oc, sv["out_c"], sv["lse_c"], dproj, name="dilated_bwd")
        fold = lambda g: g.reshape(-1, HEAD_DIM).sum(axis=0)
        small["a_q_gain"][l], small["a_k_gain"][l] = fold(dgq_a), fold(dgk_a)
        small["c_q_gain"][l], small["c_k_gain"][l] = fold(dgq_c), fold(dgk_c)
        small["a_sinks"][l] = dsink[:, ::HEAD_DIM].reshape(-1)
        dbias_a = db_a if dbias_a is None else dbias_a + db_a
        dbias_c = db_c if dbias_c is None else dbias_c + db_c
        big["w_in"][l] = by_cols(matmul(sv["h1"], dproj, trans_a=True, out_dtype=BF16, name="in_proj_dw"))
        if not each:
            token = scatter(l, names_big)
        dh1 = matmul(dproj, wi[l], trans_b=True, name="in_proj_dx")
        dx, dx_b, small["attn_norm"][l] = rmsnorm_bwd(dh1, sv["x_in"], attn_norm[l] + token[0, 0], dx_mid,
                                                name="attn_norm_bwd")

    dtable = jnp.concatenate([bias_bwd(dbias_a, buckets_a, name="swa_bias_bwd"),
                              bias_bwd(dbias_c, buckets_c, name="dilated_bias_bwd")], axis=1)

    order = ("attn_norm", "a_q_gain", "a_k_gain", "a_sinks", "c_q_gain", "c_k_gain", "rel_bias_table",
             "mix_out_gain", "ffn_norm", "conv_w", "conv_b")
    partial = {k: jnp.stack(v) for k, v in small.items()}
    partial["rel_bias_table"] = dtable
    pieces = [partial[k] for k in order] + [loss_part.reshape(1)]
    n_small = sum(math.prod(pc.shape) for pc in pieces)
    rows = -(-n_small // (8 * LANES)) * 8
    gathered = gather_small(_pack(pieces, rows), name="gather_small_grads")
    big["w_in"][0], gathered = lax.optimization_barrier((big["w_in"][0], gathered))
    scatter(0, ("w_in",))
    summed = _unpack(sum_devices(gathered.reshape(N_DEV, rows, LANES), name="sum_small_grads"),
                     [pc.shape for pc in pieces])
    g_small = dict(zip(order, summed[:-1]))
    loss = summed[-1][0]
    g_small["conv_w"] = lax.dynamic_slice_in_dim(g_small["conv_w"], dev * ff2_shard, ff2_shard, axis=2)

    w_small = dict(attn_norm=attn_norm, a_q_gain=a_q_gain, a_k_gain=a_k_gain, a_sinks=a_sinks, c_q_gain=c_q_gain,
                   c_k_gain=c_k_gain, rel_bias_table=rel_bias_table, mix_out_gain=mix_out_gain, ffn_norm=ffn_norm,
                   conv_w=conv_w, conv_b=conv_b)
    m_small = dict(attn_norm=m_attn_norm, a_q_gain=m_a_q_gain, a_k_gain=m_a_k_gain, a_sinks=m_a_sinks,
                   c_q_gain=m_c_q_gain, c_k_gain=m_c_k_gain, rel_bias_table=m_rel_bias_table,
                   mix_out_gain=m_mix_out_gain, ffn_norm=m_ffn_norm, conv_w=m_conv_w, conv_b=m_conv_b)
    v_small = dict(attn_norm=v_attn_norm, a_q_gain=v_a_q_gain, a_k_gain=v_a_k_gain, a_sinks=v_a_sinks,
                   c_q_gain=v_c_q_gain, c_k_gain=v_c_k_gain, rel_bias_table=v_rel_bias_table,
                   mix_out_gain=v_mix_out_gain, ffn_norm=v_ffn_norm, conv_w=v_conv_w, conv_b=v_conv_b)
    shapes = [w_small[k].shape for k in order]
    n_upd = sum(math.prod(sh) for sh in shapes)
    urows = -(-n_upd // (8 * LANES)) * 8
    packs = [_pack([d[k] for k in order], urows) for d in (g_small, w_small, m_small, v_small)]
    upd = adamw_small(*packs, name="adamw_small")
    delta_s, newm_s, newv_s = [dict(zip(order, _unpack(u, shapes))) for u in upd]

    w_big = dict(w_in=(w_in, m_w_in, v_w_in), w_out=(w_out, m_w_out, v_w_out), w_up=(w_up, m_w_up, v_w_up),
                 w_down=(w_down, m_w_down, v_w_down))
    results = {k: [lax.empty(w_big[k][0].shape, F32) for _ in range(4)] for k in names_big}
    after = upd[0]
    for l, names in [(l, names_big) for l in reversed(range(1, depth))] + [(0, names_big[1:]), (0, names_big[:1])]:
        landed = exchange_wait([scatters[l, n] for n in names], True, after,
                               name=f"scatter_wait_{l}_{names_big.index(names[0])}")
        for k, parts in zip(names, landed):
            results[k] = adamw_parts(parts, *w_big[k], l, results[k], name="adamw_large")
        after = results[names[-1]][0]
    g_big, delta_b, newm_b, newv_b = [{k: results[k][i] for k in names_big} for i in range(4)]

    all_names = ("attn_norm", "w_in", "a_q_gain", "a_k_gain", "a_sinks", "c_q_gain", "c_k_gain", "rel_bias_table",
                 "mix_out_gain", "w_out", "ffn_norm", "w_up", "conv_w", "conv_b", "w_down")
    pick = lambda sm, bg: [bg[k] if k in bg else sm[k] for k in all_names]
    return (loss, dx[None], *pick(g_small, g_big), *pick(delta_s, delta_b), *pick(newm_s, newm_b),
            *pick(newv_s, newv_b))
```

```python
import functools
import math

import jax
import jax.numpy as jnp
from jax import lax
from jax.experimental import pallas as pl
from jax.experimental.pallas import tpu as pltpu

F32, BF16, I32 = jnp.float32, jnp.bfloat16, jnp.int32
MESH = pl.DeviceIdType.MESH

HEAD_DIM = 64
LANES = 128
BLOCK = 128
EPS = 1e-6
NEG_INF = -1e30
N_BUCKETS = 32
T5_MAX_DIST = 2048
WINDOW_A = 128
DILATIONS = (1, 4, 16)
N_DEV = 8
VMEM_LIMIT = 56 * 1024 * 1024
MATMUL_VMEM = 46 * 1024 * 1024

ADAM_LR, ADAM_B1, ADAM_B2, ADAM_EPS, ADAM_WD, ADAM_STEP = 0.001, 0.9, 0.999, 1e-08, 0.01, 10


def _params(sem=None, vmem=None):
    return pltpu.CompilerParams(dimension_semantics=sem, vmem_limit_bytes=vmem)


def _pick(n, cands):
    for c in cands:
        if n % c == 0:
            return c
    raise ValueError(f"no tile for {n}")


def _dot(a, b):
    return lax.dot_general(a, b, (((1,), (0,)), ((), ())), preferred_element_type=F32)


def _dot_nt(a, b):
    return lax.dot_general(a, b, (((1,), (1,)), ((), ())), preferred_element_type=F32)


def _dot_tn(a, b):
    return lax.dot_general(a, b, (((0,), (0,)), ((), ())), preferred_element_type=F32)


def matmul(a, b, *, trans_a=False, trans_b=False, out_dtype=F32, res=None, col_blocks=None, name):
    a_halves, b_halves = a.ndim == 3, b.ndim == 3
    assert not (a_halves and trans_a) and not (b_halves and trans_b)
    m, k = (a.shape[1], 2 * a.shape[2]) if a_halves else (a.shape[1], a.shape[0]) if trans_a else a.shape
    n = 2 * b.shape[2] if b_halves else b.shape[0] if trans_b else b.shape[1]
    k_unit, n_unit = (k // 2 if a_halves else k), (n // 2 if b_halves else n)
    tm = _pick(m, (1408, 1024, 512, 256))
    tn_cands = ((n // col_blocks,) if col_blocks
                else tuple(t for t in (1408, 1024, 768, 512, 256, 128) if n_unit % t == 0))

    def footprint(tk, tn):
        tiles = 2 * (tm * tk * a.dtype.itemsize + tk * tn * b.dtype.itemsize)
        return tiles + tm * tn * (4 + 2 * jnp.dtype(out_dtype).itemsize + (8 if res is not None else 0))

    tk, tn = next((tk, tn) for tk in (2816, 2048, 1792, 1024, 768, 512, 256) if k_unit % tk == 0
                  for tn in tn_cands if footprint(tk, tn) <= MATMUL_VMEM)
    nk = k // tk
    nk_half, nj_half = k_unit // tk, n_unit // tn
    dn = (((0 if trans_a else 1,), (1 if trans_b else 0,)), ((), ()))

    def body(*refs):
        if res is None:
            a_ref, b_ref, o_ref, acc = refs
        else:
            a_ref, b_ref, r_ref, o_ref, acc = refs
        kk = pl.program_id(2)

        @pl.when(kk == 0)
        def _():
            acc[...] = jnp.zeros_like(acc)

        acc[...] += lax.dot_general(a_ref[...].astype(BF16), b_ref[...].astype(BF16), dn,
                                    preferred_element_type=F32)

        @pl.when(kk == nk - 1)
        def _():
            r = acc[...]
            if res is not None:
                r = r_ref[...] + r
            o_ref[...] = r.astype(out_dtype)

    b_spec = (pl.BlockSpec((tn, tk), lambda i, j, kk: (j, kk)) if trans_b
              else pl.BlockSpec((None, tk, tn), lambda i, j, kk: (j // nj_half, kk, j % nj_half)) if b_halves
              else pl.BlockSpec((tk, tn), lambda i, j, kk: (kk, j)))
    a_spec = (pl.BlockSpec((tk, tm), lambda i, j, kk: (kk, i)) if trans_a
              else pl.BlockSpec((None, tm, tk), lambda i, j, kk: (kk // nk_half, i, kk % nk_half)) if a_halves
              else pl.BlockSpec((tm, tk), lambda i, j, kk: (i, kk)))
    in_specs = [a_spec, b_spec]
    args = [a, b]
    if res is not None:
        in_specs.append(pl.BlockSpec((tm, tn), lambda i, j, kk: (i, j)))
        args.append(res)
    if col_blocks:
        out_spec = pl.BlockSpec((None, tm, tn), lambda i, j, kk: (j, i, 0))
        out_shape = jax.ShapeDtypeStruct((col_blocks, m, tn), out_dtype)
    else:
        out_spec = pl.BlockSpec((tm, tn), lambda i, j, kk: (i, j))
        out_shape = jax.ShapeDtypeStruct((m, n), out_dtype)
    return pl.pallas_call(
        body, name=name, grid=(m // tm, n // tn, nk),
        in_specs=in_specs, out_specs=out_spec, out_shape=out_shape,
        scratch_shapes=[pltpu.VMEM((tm, tn), F32)],
        compiler_params=_params(("parallel", "parallel", "arbitrary"), VMEM_LIMIT),
    )(*args)


def rmsnorm_fwd(x, g, *, name):
    s, d = x.shape
    tm = 512

    def body(x_ref, g_ref, o_ref):
        xv = x_ref[...]
        r = lax.rsqrt(jnp.mean(xv * xv, axis=-1, keepdims=True) + EPS)
        o_ref[...] = (xv * r * g_ref[...]).astype(BF16)

    return pl.pallas_call(
        body, name=name, grid=(s // tm,),
        in_specs=[pl.BlockSpec((tm, d), lambda i: (i, 0)), pl.BlockSpec((1, d), lambda i: (0, 0))],
        out_specs=pl.BlockSpec((tm, d), lambda i: (i, 0)),
        out_shape=jax.ShapeDtypeStruct((s, d), BF16),
        compiler_params=_params(("parallel",)),
    )(x, g.reshape(1, d))


def rmsnorm_bwd(dh, x, g, dres, *, name):
    s, d = x.shape
    tm = 256

    def body(dh_ref, x_ref, g_ref, dres_ref, dx_ref, dxb_ref, dg_ref):
        @pl.when(pl.program_id(0) == 0)
        def _():
            dg_ref[...] = jnp.zeros_like(dg_ref)

        xv, dhv = x_ref[...], dh_ref[...]
        r = lax.rsqrt(jnp.mean(xv * xv, axis=-1, keepdims=True) + EPS)
        gd = dhv * g_ref[...]
        dot = jnp.mean(gd * xv, axis=-1, keepdims=True)
        dx = dres_ref[...] + (r * gd - xv * (r * r * r * dot))
        dx_ref[...] = dx
        dxb_ref[...] = dx.astype(BF16)
        dg_ref[...] += jnp.sum(dhv * (xv * r), axis=0, keepdims=True)

    row = pl.BlockSpec((tm, d), lambda i: (i, 0))
    dx, dxb, dg = pl.pallas_call(
        body, name=name, grid=(s // tm,),
        in_specs=[row, row, pl.BlockSpec((1, d), lambda i: (0, 0)), row],
        out_specs=[row, row, pl.BlockSpec((1, d), lambda i: (0, 0))],
        out_shape=[jax.ShapeDtypeStruct((s, d), F32), jax.ShapeDtypeStruct((s, d), BF16),
                   jax.ShapeDtypeStruct((1, d), F32)],
        compiler_params=_params(("arbitrary",)),
    )(dh, x, g.reshape(1, d), dres)
    return dx, dxb, dg[0]


def loss_head(y, target, *, name):
    s, d = y.shape
    tm = 512

    def body(y_ref, t_ref, dy_ref, dyb_ref, l_ref):
        @pl.when(pl.program_id(0) == 0)
        def _():
            l_ref[...] = jnp.zeros_like(l_ref)

        e = y_ref[...] - t_ref[...]
        dy = e / float(d)
        dy_ref[...] = dy
        dyb_ref[...] = dy.astype(BF16)
        per_tok = jnp.mean(e * e, axis=-1, keepdims=True)
        l_ref[...] += 0.5 * jnp.sum(per_tok, axis=0, keepdims=True)

    row = pl.BlockSpec((tm, d), lambda i: (i, 0))
    dy, dyb, l = pl.pallas_call(
        body, name=name, grid=(s // tm,),
        in_specs=[row, row],
        out_specs=[row, row, pl.BlockSpec((8, LANES), lambda i: (0, 0))],
        out_shape=[jax.ShapeDtypeStruct((s, d), F32), jax.ShapeDtypeStruct((s, d), BF16),
                   jax.ShapeDtypeStruct((8, LANES), F32)],
        compiler_params=_params(("arbitrary",)),
    )(y, target)
    return dy, dyb, l[0, 0]


FFN_TN = 256
FFN_CH = 256


def _rows_before(ref, r0, first):
    if first:
        cur = ref[pl.ds(0, FFN_CH), :]
        row = lax.broadcasted_iota(I32, cur.shape, 0)
        sh1 = jnp.where(row < 1, 0.0, pltpu.roll(cur, 1, axis=0))
        sh2 = jnp.where(row < 2, 0.0, pltpu.roll(cur, 2, axis=0))
        return cur, sh1, sh2
    ext = ref[pl.ds(pl.multiple_of(r0 - 8, 8), FFN_CH + 8), :]
    return ext[8:], pltpu.roll(ext, 1, axis=0)[8:], pltpu.roll(ext, 2, axis=0)[8:]


def _rows_after(ref, r0, last):
    if last:
        cur = ref[pl.ds(r0, FFN_CH), :]
        row = lax.broadcasted_iota(I32, cur.shape, 0)
        up1 = jnp.where(row >= FFN_CH - 1, 0.0, pltpu.roll(cur, FFN_CH - 1, axis=0))
        up2 = jnp.where(row >= FFN_CH - 2, 0.0, pltpu.roll(cur, FFN_CH - 2, axis=0))
        return cur, up1, up2
    n = FFN_CH + 8
    ext = ref[pl.ds(r0, n), :]
    return ext[:FFN_CH], pltpu.roll(ext, n - 1, axis=0)[:FFN_CH], pltpu.roll(ext, n - 2, axis=0)[:FFN_CH]


def _sigmoid(x):
    return 0.5 * jnp.tanh(0.5 * x) + 0.5


def ffn_act_fwd(p, conv_w, conv_b, *, name):
    s, f2 = p.shape
    f = f2 // 2
    nj = f // FFN_TN
    nch = s // FFN_CH

    def body(pg_ref, pu_ref, wg_ref, wu_ref, bg_ref, bu_ref, a_ref):
        def conv(ref, w_ref, b_ref, r0, first):
            cur, sh1, sh2 = _rows_before(ref, r0, first)
            return ((b_ref[...] + w_ref[0:1, :] * sh2) + w_ref[1:2, :] * sh1) + w_ref[2:3, :] * cur

        def chunk(r0, first):
            gate = conv(pg_ref, wg_ref, bg_ref, r0, first)
            up = conv(pu_ref, wu_ref, bu_ref, r0, first)
            a_ref[pl.ds(r0, FFN_CH), :] = (gate * _sigmoid(gate) * up).astype(BF16)

        chunk(0, True)

        def step(c, carry):
            chunk(pl.multiple_of(c * FFN_CH, FFN_CH), False)
            return carry

        lax.fori_loop(1, nch, step, 0)

    col = lambda off: pl.BlockSpec((s, FFN_TN), lambda j: (0, j + off))
    wcol = lambda off: pl.BlockSpec((3, FFN_TN), lambda j: (0, j + off))
    bcol = lambda off: pl.BlockSpec((1, FFN_TN), lambda j: (0, j + off))
    return pl.pallas_call(
        body, name=name, grid=(nj,),
        in_specs=[col(0), col(nj), wcol(0), wcol(nj), bcol(0), bcol(nj)],
        out_specs=pl.BlockSpec((s, FFN_TN), lambda j: (0, j)),
        out_shape=jax.ShapeDtypeStruct((s, f), BF16),
        compiler_params=_params(("parallel",), VMEM_LIMIT),
    )(p, p, conv_w, conv_w, conv_b.reshape(1, f2), conv_b.reshape(1, f2))


def ffn_act_bwd(da, p, conv_w, conv_b, *, name):
    s, f2 = p.shape
    f = f2 // 2
    nj = f // FFN_TN
    nch = s // FFN_CH

    def body(da_ref, pg_ref, pu_ref, wg_ref, wu_ref, bg_ref, bu_ref,
             dp_ref, dwg_ref, dwu_ref, dbg_ref, dbu_ref, dug_s, duu_s):
        dpg_ref, dpu_ref = dp_ref.at[0], dp_ref.at[1]
        def conv(ref, w_ref, b_ref, r0, first):
            cur, sh1, sh2 = _rows_before(ref, r0, first)
            u = ((b_ref[...] + w_ref[0:1, :] * sh2) + w_ref[1:2, :] * sh1) + w_ref[2:3, :] * cur
            return u, (sh2, sh1, cur)

        def taps_sum(du, taps):
            return jnp.concatenate([jnp.sum(du * t, axis=0, keepdims=True) for t in taps], axis=0)

        def chunk(r0, first, acc):
            dwg, dwu, dbg, dbu = acc
            gate, tg = conv(pg_ref, wg_ref, bg_ref, r0, first)
            up, tu = conv(pu_ref, wu_ref, bu_ref, r0, first)
            dav = da_ref[pl.ds(r0, FFN_CH), :]
            sg = _sigmoid(gate)
            dgate = dav * up * (sg * (1.0 + gate * (1.0 - sg)))
            dup = dav * (gate * sg)
            dug_s[pl.ds(r0, FFN_CH), :] = dgate
            duu_s[pl.ds(r0, FFN_CH), :] = dup
            return (dwg + taps_sum(dgate, tg), dwu + taps_sum(dup, tu),
                    dbg + jnp.sum(dgate, axis=0, keepdims=True), dbu + jnp.sum(dup, axis=0, keepdims=True))

        z3 = jnp.zeros((3, FFN_TN), F32)
        z1 = jnp.zeros((1, FFN_TN), F32)
        acc = chunk(0, True, (z3, z3, z1, z1))
        acc = lax.fori_loop(1, nch, lambda c, a: chunk(pl.multiple_of(c * FFN_CH, FFN_CH), False, a), acc)
        dwg_ref[...], dwu_ref[...], dbg_ref[...], dbu_ref[...] = acc

        def back(src, w_ref, dst, r0, last):
            cur, up1, up2 = _rows_after(src, r0, last)
            dst[pl.ds(r0, FFN_CH), :] = (w_ref[2:3, :] * cur + w_ref[1:2, :] * up1 + w_ref[0:1, :] * up2).astype(BF16)

        def step(c, carry):
            r0 = pl.multiple_of(c * FFN_CH, FFN_CH)
            back(dug_s, wg_ref, dpg_ref, r0, False)
            back(duu_s, wu_ref, dpu_ref, r0, False)
            return carry

        lax.fori_loop(0, nch - 1, step, 0)
        back(dug_s, wg_ref, dpg_ref, (nch - 1) * FFN_CH, True)
        back(duu_s, wu_ref, dpu_ref, (nch - 1) * FFN_CH, True)

    col = lambda off: pl.BlockSpec((s, FFN_TN), lambda j: (0, j + off))
    wcol = lambda off: pl.BlockSpec((3, FFN_TN), lambda j: (0, j + off))
    bcol = lambda off: pl.BlockSpec((1, FFN_TN), lambda j: (0, j + off))
    outs = pl.pallas_call(
        body, name=name, grid=(nj,),
        in_specs=[col(0), col(0), col(nj), wcol(0), wcol(nj), bcol(0), bcol(nj)],
        out_specs=[pl.BlockSpec((2, s, FFN_TN), lambda j: (0, 0, j)), wcol(0), wcol(0), bcol(0), bcol(0)],
        out_shape=[jax.ShapeDtypeStruct((2, s, f), BF16),
                   jax.ShapeDtypeStruct((3, f), F32), jax.ShapeDtypeStruct((3, f), F32),
                   jax.ShapeDtypeStruct((1, f), F32), jax.ShapeDtypeStruct((1, f), F32)],
        scratch_shapes=[pltpu.VMEM((s, FFN_TN), F32), pltpu.VMEM((s, FFN_TN), F32)],
        compiler_params=_params(("parallel",), VMEM_LIMIT),
    )(da, p, p, conv_w, conv_w, conv_b.reshape(1, f2), conv_b.reshape(1, f2))
    dp, dwg, dwu, dbg, dbu = outs
    return dp, jnp.concatenate([dwg, dwu], axis=1), jnp.concatenate([dbg, dbu], axis=1)[0]


def mixnorm_fwd(outs, gain, *, name):
    s = outs[0].shape[0]
    widths = [o.shape[1] for o in outs]
    total = sum(widths)
    tm = 512

    def body(*refs):
        o_refs, g_ref, m_ref = refs[:-2], refs[-2], refs[-1]
        off = 0
        for o_ref, w in zip(o_refs, widths):
            xv = o_ref[...]
            r = lax.rsqrt(jnp.mean(xv * xv, axis=-1, keepdims=True) + EPS)
            m_ref[:, off:off + w] = (xv * r * g_ref[:, off:off + w]).astype(BF16)
            off += w

    return pl.pallas_call(
        body, name=name, grid=(s // tm,),
        in_specs=[pl.BlockSpec((tm, w), lambda i: (i, 0)) for w in widths] + [pl.BlockSpec((1, total), lambda i: (0, 0))],
        out_specs=pl.BlockSpec((tm, total), lambda i: (i, 0)),
        out_shape=jax.ShapeDtypeStruct((s, total), BF16),
        compiler_params=_params(("parallel",)),
    )(*outs, gain.reshape(1, total))


def mixnorm_bwd(dmix, outs, gain, *, name):
    s = outs[0].shape[0]
    widths = [o.shape[1] for o in outs]
    total = sum(widths)
    n = len(outs)
    tm = 256

    def body(*refs):
        dm_ref, o_refs, g_ref = refs[0], refs[1:1 + n], refs[1 + n]
        d_refs, dg_ref = refs[2 + n:2 + 2 * n], refs[2 + 2 * n]

        @pl.when(pl.program_id(0) == 0)
        def _():
            dg_ref[...] = jnp.zeros_like(dg_ref)

        off = 0
        for o_ref, d_ref, w in zip(o_refs, d_refs, widths):
            xv = o_ref[...]
            dhv = dm_ref[:, off:off + w]
            r = lax.rsqrt(jnp.mean(xv * xv, axis=-1, keepdims=True) + EPS)
            gd = dhv * g_ref[:, off:off + w]
            dot = jnp.mean(gd * xv, axis=-1, keepdims=True)
            d_ref[...] = r * gd - xv * (r * r * r * dot)
            dg_ref[:, off:off + w] += jnp.sum(dhv * (xv * r), axis=0, keepdims=True)
            off += w

    res = pl.pallas_call(
        body, name=name, grid=(s // tm,),
        in_specs=[pl.BlockSpec((tm, total), lambda i: (i, 0))]
        + [pl.BlockSpec((tm, w), lambda i: (i, 0)) for w in widths] + [pl.BlockSpec((1, total), lambda i: (0, 0))],
        out_specs=[pl.BlockSpec((tm, w), lambda i: (i, 0)) for w in widths] + [pl.BlockSpec((1, total), lambda i: (0, 0))],
        out_shape=[jax.ShapeDtypeStruct((s, w), F32) for w in widths] + [jax.ShapeDtypeStruct((1, total), F32)],
        compiler_params=_params(("arbitrary",)),
    )(dmix, *outs, gain.reshape(1, total))
    return res[:n], res[n][0]


NORM_CH = 512
FWD_TILES = 4
BWD_TILES = 2


def _lo_mask(shape):
    return lax.broadcasted_iota(I32, shape, 1) < HEAD_DIM


def _head_sum(x, lo):
    del lo
    i = lax.broadcasted_iota(I32, (LANES, LANES), 0) // HEAD_DIM
    j = lax.broadcasted_iota(I32, (LANES, LANES), 1) // HEAD_DIM
    return _split_dot(x, _twice(i == j))


def _head_stats(x, lo):
    return lax.rsqrt(_head_sum(x * x, lo) * (1.0 / HEAD_DIM) + EPS)


def _swap_halves(x):
    return pltpu.roll(x, HEAD_DIM, axis=1)


def _replicate_head(x, lo, use_lo_head):
    sw = _swap_halves(x)
    return jnp.where(use_lo_head, jnp.where(lo, x, sw), jnp.where(lo, sw, x))


def _tile_rows(i, s, d):
    nb = s // (BLOCK * d)
    r = i // nb
    b = i % nb
    start = r + (BLOCK * d) * b
    prev = start - (BLOCK * d) * jnp.minimum(b, 1)
    return start, prev, b > 0


def _rows(ref, start, d):
    if d == 1:
        return ref[pl.ds(pl.multiple_of(start, BLOCK), BLOCK), :]
    return ref[pl.ds(start, BLOCK, stride=d), :]


def _set_rows(ref, start, d, val):
    if d == 1:
        ref[pl.ds(pl.multiple_of(start, BLOCK), BLOCK), :] = val
    else:
        ref[pl.ds(start, BLOCK, stride=d), :] = val


def banded_fwd(proj, qb0, kb0, vb0, n_slabs, gq, gk, bias, dils, sinks, gqa, *, name):
    s = proj.shape[0]
    nbr = len(dils)
    nt = s // BLOCK
    nch = s // NORM_CH
    has_sink = sinks is not None

    def body(*refs):
        q_ref, k_ref, v_ref, gq_ref, gk_ref, b_ref = refs[:6]
        rest = refs[6:]
        if has_sink:
            sink_ref, rest = rest[0], rest[1:]
        out_ref, lse_ref, qn_s, kn_s, vv_s, o_s, l_s = rest
        p = pl.program_id(0)
        use_lo = (p // 2) == 0

        def prep(c, carry):
            rows = pl.ds(pl.multiple_of(c * NORM_CH, NORM_CH), NORM_CH)
            lo = _lo_mask((NORM_CH, LANES))
            qv, kv, vv = q_ref[rows, :], k_ref[rows, :], v_ref[rows, :]
            qn_s[rows, :] = qv * _head_stats(qv, lo) * gq_ref[...] * (HEAD_DIM ** -0.5)
            kn = kv * _head_stats(kv, lo) * gk_ref[...]
            if gqa:
                kn = _replicate_head(kn, lo, use_lo)
                vv = _replicate_head(vv, lo, use_lo)
            kn_s[rows, :] = kn
            vv_s[rows, :] = vv
            return carry

        lax.fori_loop(0, nch, prep, 0)

        lo = _lo_mask((BLOCK, LANES))
        hms = [lo, jnp.logical_not(lo)]
        heads, tiles = range(2), range(FWD_TILES)
        for br, d in enumerate(dils):
            def step(ii, carry, br=br, d=d):
                pos = [_tile_rows(ii * FWD_TILES + u, s, d) for u in tiles]
                kc = [carry[0]] + [_rows(kn_s, pos[u][0], d).astype(BF16) for u in tiles]
                vc = [carry[1]] + [_rows(vv_s, pos[u][0], d).astype(BF16) for u in tiles]
                kcat = [jnp.concatenate([kc[u], kc[u + 1]], axis=0) for u in tiles]
                vcat = [jnp.concatenate([vc[u], vc[u + 1]], axis=0) for u in tiles]
                qt = [_rows(qn_s, pos[u][0], d) for u in tiles]
                sc = [[_dot_nt(jnp.where(hms[h], qt[u], 0.0).astype(BF16), kcat[u])
                       + b_ref[br, jnp.where(pos[u][2], 0, 1), h] for h in heads] for u in tiles]
                m = [[jnp.max(sc[u][h], axis=1, keepdims=True) for h in heads] for u in tiles]
                pe = [[jnp.exp(sc[u][h] - m[u][h]) for h in heads] for u in tiles]
                den = [[jnp.sum(pe[u][h], axis=1, keepdims=True) for h in heads] for u in tiles]
                o = [[_dot(pe[u][h].astype(BF16), vcat[u]) * (1.0 / den[u][h]) for h in heads] for u in tiles]
                for u in tiles:
                    _set_rows(o_s.at[br], pos[u][0], d, jnp.where(lo, o[u][0], o[u][1]))
                    _set_rows(l_s.at[br], pos[u][0], d,
                              jnp.where(lo, m[u][0] + jnp.log(den[u][0]), m[u][1] + jnp.log(den[u][1])))
                return kc[-1], vc[-1]

            none_yet = jnp.zeros((BLOCK, LANES), BF16)
            lax.fori_loop(0, nt // FWD_TILES, step, (none_yet, none_yet))

        def combine(c, carry):
            rows = pl.ds(pl.multiple_of(c * NORM_CH, NORM_CH), NORM_CH)
            ls = [l_s[br, rows, :] for br in range(nbr)]
            mx = functools.reduce(jnp.maximum, ls)
            if has_sink:
                mx = jnp.maximum(mx, sink_ref[...])
            tot = functools.reduce(jnp.add, [jnp.exp(l - mx) for l in ls])
            if has_sink:
                tot = tot + jnp.exp(sink_ref[...] - mx)
            lse = mx + jnp.log(tot)
            acc = jnp.exp(ls[0] - lse) * o_s[0, rows, :]
            for br in range(1, nbr):
                acc = acc + jnp.exp(ls[br] - lse) * o_s[br, rows, :]
            out_ref[rows, :] = acc
            lse_ref[rows, :] = lse
            return carry

        lax.fori_loop(0, nch, combine, 0)

    slab = lambda b0, shared: pl.BlockSpec((s, LANES), (lambda p: (0, b0)) if shared else (lambda p: (0, b0 + p)),
                                           pipeline_mode=pl.Buffered(1))
    vec = pl.BlockSpec((1, LANES), lambda p: (0, 0))
    in_specs = [slab(qb0, False), slab(kb0, gqa), slab(vb0, gqa), vec, vec,
                pl.BlockSpec((nbr, 2, 2, BLOCK, 2 * BLOCK), lambda p: (0, 0, p, 0, 0))]
    args = [proj, proj, proj, gq.reshape(1, LANES), gk.reshape(1, LANES), bias]
    if has_sink:
        in_specs.append(pl.BlockSpec((None, 1, LANES), lambda p: (p, 0, 0)))
        args.append(sinks)
    w = LANES * n_slabs
    return pl.pallas_call(
        body, name=name, grid=(n_slabs,),
        in_specs=in_specs,
        out_specs=[pl.BlockSpec((s, LANES), lambda p: (0, p)), pl.BlockSpec((s, LANES), lambda p: (0, p))],
        out_shape=[jax.ShapeDtypeStruct((s, w), F32), jax.ShapeDtypeStruct((s, w), F32)],
        scratch_shapes=[pltpu.VMEM((s, LANES), F32), pltpu.VMEM((s, LANES), F32), pltpu.VMEM((s, LANES), F32),
                        pltpu.VMEM((nbr, s, LANES), F32), pltpu.VMEM((nbr, s, LANES), F32)],
        compiler_params=_params(("parallel",), VMEM_LIMIT),
    )(*args)


def banded_bwd(proj, qb0, kb0, vb0, n_slabs, gq, gk, bias, dils, sinks, gqa, dout, out, lse, dproj, *, name):
    s = proj.shape[0]
    nbr = len(dils)
    nt = s // BLOCK
    nch = s // NORM_CH
    has_sink = sinks is not None
    scale = HEAD_DIM ** -0.5

    def body(*refs):
        q_ref, k_ref, v_ref, gq_ref, gk_ref, b_ref, do_ref, o_ref, lse_ref = refs[:9]
        rest = refs[9:]
        if has_sink:
            sink_ref, rest = rest[0], rest[1:]
        dproj_ref, db_ref, dgq_ref, dgk_ref = rest[1:5]
        rest = rest[5:]
        if has_sink:
            dsink_ref, rest = rest[0], rest[1:]
        qn_s, kn_s, vv_s, dl_s, dqn_s, dkn_s, dvv_s, dq_ref, dk_ref, dv_ref, stage, sems = rest
        p = pl.program_id(0)
        use_lo = (p // 2) == 0

        def prep(c, carry):
            rows = pl.ds(pl.multiple_of(c * NORM_CH, NORM_CH), NORM_CH)
            lo = _lo_mask((NORM_CH, LANES))
            qv, kv, vv = q_ref[rows, :], k_ref[rows, :], v_ref[rows, :]
            qn_s[rows, :] = qv * _head_stats(qv, lo) * gq_ref[...] * scale
            kn = kv * _head_stats(kv, lo) * gk_ref[...]
            if gqa:
                kn = _replicate_head(kn, lo, use_lo)
                vv = _replicate_head(vv, lo, use_lo)
            kn_s[rows, :] = kn
            vv_s[rows, :] = vv
            delta = _head_sum(do_ref[rows, :] * o_ref[rows, :], lo)
            odd = lax.broadcasted_iota(I32, (NORM_CH, LANES), 1) % 2 == 1
            dl_s[rows, :] = jnp.where(odd, delta, lse_ref[rows, :])
            z = jnp.zeros((NORM_CH, LANES), F32)
            dqn_s[rows, :] = z
            dkn_s[rows, :] = z
            dvv_s[rows, :] = z
            if has_sink:
                ps = jnp.exp(sink_ref[...] - lse_ref[rows, :])
                return carry - jnp.sum(ps * delta, axis=0, keepdims=True)
            return carry

        dsink = lax.fori_loop(0, nch, prep, jnp.zeros((1, LANES), F32))
        if has_sink:
            dsink_ref[...] = jnp.broadcast_to(dsink, (8, LANES))

        lo = _lo_mask((BLOCK, LANES))
        hms = [lo, jnp.logical_not(lo)]
        heads, tiles = range(2), range(BWD_TILES)
        for br, d in enumerate(dils):
            db_ref[br] = jnp.zeros((2, BLOCK, 2 * BLOCK), F32)

            def step(ii, carry, br=br, d=d):
                pos = [_tile_rows(ii * BWD_TILES + u, s, d) for u in tiles]
                kc = [carry[0]] + [_rows(kn_s, pos[u][0], d).astype(BF16) for u in tiles]
                vc = [carry[1]] + [_rows(vv_s, pos[u][0], d).astype(BF16) for u in tiles]
                kcat = [jnp.concatenate([kc[u], kc[u + 1]], axis=0) for u in tiles]
                vcat = [jnp.concatenate([vc[u], vc[u + 1]], axis=0) for u in tiles]
                qt = [_rows(qn_s, pos[u][0], d) for u in tiles]
                dot_ = [_rows(do_ref, pos[u][0], d) for u in tiles]
                st_t = [_rows(dl_s, pos[u][0], d) for u in tiles]
                qh = [[jnp.where(hms[h], qt[u], 0.0).astype(BF16) for h in heads] for u in tiles]
                doh = [[jnp.where(hms[h], dot_[u], 0.0).astype(BF16) for h in heads] for u in tiles]
                sc = [[_dot_nt(qh[u][h], kcat[u]) + b_ref[br, jnp.where(pos[u][2], 0, 1), h] for h in heads]
                      for u in tiles]
                dp = [[_dot_nt(doh[u][h], vcat[u]) for h in heads] for u in tiles]
                lane0 = [0, HEAD_DIM]
                pr = [[jnp.exp(sc[u][h] - st_t[u][:, lane0[h]:lane0[h] + 1]) for h in heads] for u in tiles]
                dlog = [[pr[u][h] * (dp[u][h] - st_t[u][:, lane0[h] + 1:lane0[h] + 2]) for h in heads] for u in tiles]
                for h in heads:
                    db_ref[br, h] += functools.reduce(jnp.add, [dlog[u][h] for u in tiles])
                dlb = [[dlog[u][h].astype(BF16) for h in heads] for u in tiles]
                prb = [[pr[u][h].astype(BF16) for h in heads] for u in tiles]
                dq_t = [jnp.where(lo, _dot(dlb[u][0], kcat[u]), _dot(dlb[u][1], kcat[u])) * scale for u in tiles]
                rows2 = lambda x: jnp.concatenate(x, axis=0)
                dk_t = [_dot_tn(rows2(dlb[u]), rows2(qh[u])) for u in tiles]
                dv_t = [_dot_tn(rows2(prb[u]), rows2(doh[u])) for u in tiles]
                for u in tiles:
                    start, prev = pos[u][0], pos[u][1]
                    _set_rows(dqn_s, start, d, _rows(dqn_s, start, d) + dq_t[u])
                    _set_rows(dkn_s, prev, d, _rows(dkn_s, prev, d) + dk_t[u][:BLOCK])
                    _set_rows(dkn_s, start, d, _rows(dkn_s, start, d) + dk_t[u][BLOCK:])
                    _set_rows(dvv_s, prev, d, _rows(dvv_s, prev, d) + dv_t[u][:BLOCK])
                    _set_rows(dvv_s, start, d, _rows(dvv_s, start, d) + dv_t[u][BLOCK:])
                return kc[-1], vc[-1]

            none_yet = jnp.zeros((BLOCK, LANES), BF16)
            lax.fori_loop(0, nt // BWD_TILES, step, (none_yet, none_yet))

        if gqa:
            @pl.when(p == 0)
            def _():
                dk_ref[...] = jnp.zeros_like(dk_ref)
                dv_ref[...] = jnp.zeros_like(dv_ref)

        def finish(c, carry):
            dgq, dgk = carry
            rows = pl.ds(pl.multiple_of(c * NORM_CH, NORM_CH), NORM_CH)
            lo = _lo_mask((NORM_CH, LANES))

            def norm_bwd(xv, dn, g_ref):
                r = _head_stats(xv, lo)
                gd = dn * g_ref[...]
                dot = _head_sum(gd * xv, lo) * (1.0 / HEAD_DIM)
                return r * gd - xv * (r * r * r * dot), dn * (xv * r)

            dq, gq_part = norm_bwd(q_ref[rows, :], dqn_s[rows, :], gq_ref)
            dq_ref[rows, :] = dq
            dgq = dgq + jnp.sum(gq_part, axis=0, keepdims=True)
            kv, dkn, dvv = k_ref[rows, :], dkn_s[rows, :], dvv_s[rows, :]
            if gqa:
                kv = _replicate_head(kv, lo, use_lo)
                dkn = dkn + _swap_halves(dkn)
                dvv = dvv + _swap_halves(dvv)
                lane = lax.broadcasted_iota(I32, (NORM_CH, LANES), 1)
                mine = (lane // HEAD_DIM) == (p // 2)
                dk, gk_part = norm_bwd(kv, dkn, gk_ref)
                dk_ref[rows, :] += jnp.where(mine, dk, 0.0)
                dv_ref[rows, :] += jnp.where(mine, dvv, 0.0)
                gk_part = jnp.where(lo, gk_part, 0.0)
            else:
                dk, gk_part = norm_bwd(kv, dkn, gk_ref)
                dk_ref[rows, :] = dk
                dv_ref[rows, :] = dvv
            dgk = dgk + jnp.sum(gk_part, axis=0, keepdims=True)
            return dgq, dgk

        z = jnp.zeros((1, LANES), F32)
        dgq, dgk = lax.fori_loop(0, nch, finish, (z, z))
        dgq_ref[...] = jnp.broadcast_to(dgq, (8, LANES))
        dgk_ref[...] = jnp.broadcast_to(dgk, (8, LANES))
        if gqa:
            _store_slabs((dq_ref,), stage, dproj_ref, sems, (qb0 + p,))

            @pl.when(p == n_slabs - 1)
            def _():
                _store_slabs((dk_ref, dv_ref), stage, dproj_ref, sems, (kb0, vb0))
        else:
            _store_slabs((dq_ref, dk_ref, dv_ref), stage, dproj_ref, sems, (qb0 + p, kb0 + p, vb0 + p))

    def slab_of(width_blocks, b0, shared):
        return pl.BlockSpec((s, LANES), (lambda p: (0, b0)) if shared else (lambda p: (0, b0 + p)),
                            pipeline_mode=pl.Buffered(1))

    vec = pl.BlockSpec((1, LANES), lambda p: (0, 0))
    own = pl.BlockSpec((s, LANES), lambda p: (0, p), pipeline_mode=pl.Buffered(1))
    in_specs = [slab_of(0, qb0, False), slab_of(0, kb0, gqa), slab_of(0, vb0, gqa), vec, vec,
                pl.BlockSpec((nbr, 2, 2, BLOCK, 2 * BLOCK), lambda p: (0, 0, p, 0, 0)), own, own, own]
    args = [proj, proj, proj, gq.reshape(1, LANES), gk.reshape(1, LANES), bias, dout, out, lse]
    if has_sink:
        in_specs.append(pl.BlockSpec((None, 1, LANES), lambda p: (p, 0, 0)))
        args.append(sinks)
    held = pl.BlockSpec(memory_space=pl.ANY)
    in_specs.append(held)
    args.append(dproj)
    part = pl.BlockSpec((None, 8, LANES), lambda p: (p, 0, 0))
    out_specs = [held, pl.BlockSpec((nbr, 2, BLOCK, 2 * BLOCK), lambda p: (0, p, 0, 0)), part, part]
    out_shape = [jax.ShapeDtypeStruct(dproj.shape, dproj.dtype),
                 jax.ShapeDtypeStruct((nbr, 2 * n_slabs, BLOCK, 2 * BLOCK), F32),
                 jax.ShapeDtypeStruct((n_slabs, 8, LANES), F32), jax.ShapeDtypeStruct((n_slabs, 8, LANES), F32)]
    if has_sink:
        out_specs.append(part)
        out_shape.append(jax.ShapeDtypeStruct((n_slabs, 8, LANES), F32))
    res = pl.pallas_call(
        body, name=name, grid=(n_slabs,),
        in_specs=in_specs, out_specs=out_specs, out_shape=out_shape,
        input_output_aliases={len(args) - 1: 0},
        scratch_shapes=[pltpu.VMEM((s, LANES), F32) for _ in range(10)]
        + [pltpu.VMEM((3, s, LANES), BF16), pltpu.SemaphoreType.DMA((3,))],
        compiler_params=_params(("arbitrary",), VMEM_LIMIT),
    )(*args)
    outs = [res[0], res[1], res[2][:, 0, :], res[3][:, 0, :]]
    if has_sink:
        outs.append(res[4][:, 0, :])
    return outs


def bias_bwd(dbias, buckets, *, name):
    nbr, h = dbias.shape[:2]

    def body(db_ref, bk_ref, o_ref):
        lane = lax.broadcasted_iota(I32, (1, LANES), 1)
        acc = jnp.zeros((1, LANES), F32)
        for b in range(N_BUCKETS):
            tot = jnp.zeros((1, 1), F32)
            for br in range(nbr):
                sel = jnp.where(bk_ref[br] == b, db_ref[br], 0.0)
                tot = tot + jnp.sum(jnp.sum(sel, axis=0, keepdims=True), axis=1, keepdims=True)
            acc = jnp.where(lane == b, tot, acc)
        o_ref[...] = jnp.broadcast_to(acc, (8, LANES))

    res = pl.pallas_call(
        body, name=name, grid=(h,),
        in_specs=[pl.BlockSpec((nbr, None, BLOCK, 2 * BLOCK), lambda i: (0, i, 0, 0)),
                  pl.BlockSpec((nbr, BLOCK, 2 * BLOCK), lambda i: (0, 0, 0))],
        out_specs=pl.BlockSpec((None, 8, LANES), lambda i: (i, 0, 0)),
        out_shape=jax.ShapeDtypeStruct((h, 8, LANES), F32),
        compiler_params=_params(("parallel",)),
    )(dbias, buckets)
    return res[:, 0, :N_BUCKETS].T


SB_KG = 512
SB_QT = 2


def _softplus(z):
    return jnp.maximum(z, 0.0) + jnp.log(1.0 + jnp.exp(-jnp.abs(z)))


def _twice(t):
    t = t.astype(BF16)
    return jnp.concatenate([t, t], axis=0)


def _split_dot(x, t2):
    hi = x.astype(BF16)
    lo = (x - hi.astype(F32)).astype(BF16)
    return _dot(jnp.concatenate([hi, lo], axis=1), t2)


def sb_fwd(proj, qb0, kb0, vb0, n_slabs, *, name):
    s = proj.shape[0]
    nq = s // BLOCK
    nch = s // NORM_CH
    scale = HEAD_DIM ** -0.5

    def body(q_ref, k_ref, v_ref, o_ref, tot_ref, qlo_s, qhi_s, k_s, v_s):
        def prep(c, carry):
            rows = pl.ds(pl.multiple_of(c * NORM_CH, NORM_CH), NORM_CH)
            lo = _lo_mask((NORM_CH, LANES))
            qv = q_ref[rows, :] * scale
            qlo_s[rows, :] = jnp.where(lo, qv, 0.0).astype(BF16)
            qhi_s[rows, :] = jnp.where(lo, 0.0, qv).astype(BF16)
            k_s[rows, :] = k_ref[rows, :].astype(BF16)
            v_s[rows, :] = v_ref[rows, :].astype(BF16)
            return carry

        lax.fori_loop(0, nch, prep, 0)

        row = lax.broadcasted_iota(I32, (BLOCK, BLOCK), 0)
        col = lax.broadcasted_iota(I32, (BLOCK, BLOCK), 1)
        lo = col < HEAD_DIM
        t_ge = _twice(row >= col)
        rowg = lax.broadcasted_iota(I32, (BLOCK, SB_KG), 0)
        colg = lax.broadcasted_iota(I32, (BLOCK, SB_KG), 1)

        nsub = SB_KG // BLOCK
        chains = range(2 * SB_QT)
        nc = len(chains)

        def qloop(qs, phase):
            q0 = pl.multiple_of(qs * (SB_QT * BLOCK), SB_QT * BLOCK)
            qh = [(qlo_s, qhi_s)[i % 2][pl.ds(q0 + (i // 2) * BLOCK, BLOCK), :] for i in chains]
            gd = (qs * SB_QT) // nsub

            def logits(gi):
                k0 = pl.multiple_of(gi * SB_KG, SB_KG)
                kg = k_s[pl.ds(k0, SB_KG), :]
                return [_dot_nt(qh[i], kg) for i in chains]

            def group(gi, st, masks, npiece=nsub):
                k0 = pl.multiple_of(gi * SB_KG, SB_KG)
                vg = v_s[pl.ds(k0, npiece * BLOCK), :]
                c, o, z = list(st[:nc]), st[nc:2 * nc], st[2 * nc:]
                z_next = logits(jnp.maximum(gi - 1, 0))
                piece = lambda x, j: x[:, j * BLOCK:(j + 1) * BLOCK]
                a = [[None] * npiece for _ in chains]
                for j in reversed(range(npiece)):
                    zj = [piece(z[i], j) for i in chains]
                    lrem = [-_softplus(zj[i]) for i in chains]
                    if masks is not None:
                        lrem = [jnp.where(piece(masks[i // 2], j), lrem[i], 0.0) for i in chains]
                    incl = [_split_dot(lrem[i], t_ge) for i in chains]
                    for i in chains:
                        aij = jnp.exp(zj[i] + (c[i] + incl[i]))
                        if masks is not None:
                            aij = jnp.where(piece(masks[i // 2], j), aij, 0.0)
                        a[i][j] = aij.astype(BF16)
                        c[i] = c[i] + incl[i][:, 0:1]
                o = [o[i] + _dot(jnp.concatenate(a[i], axis=1), vg) for i in chains]
                return (*c, *o, *z_next)

            zc = [jnp.zeros((BLOCK, 1), F32)] * nc
            zo = [jnp.zeros((BLOCK, LANES), F32)] * nc
            masks = [(gd * SB_KG + colg) < (q0 + t * BLOCK + rowg) for t in range(SB_QT)]
            st = group(gd, (*zc, *zo, *logits(gd)), masks, (phase + 1) * SB_QT)
            st = lax.fori_loop(0, gd, lambda t, st: group(gd - 1 - t, st, None), st)
            for t in range(SB_QT):
                rows = pl.ds(q0 + t * BLOCK, BLOCK)
                o_ref[rows, :] = jnp.where(lo, st[nc + 2 * t], st[nc + 2 * t + 1])
                tot_ref[rows, :] = jnp.where(lo, st[2 * t], st[2 * t + 1])

        steps_per_group = nsub // SB_QT

        def per_group(g, carry):
            for phase in range(steps_per_group):
                qloop(g * steps_per_group + phase, phase)
            return carry

        lax.fori_loop(0, nq // nsub, per_group, 0)

    slab = lambda b0: pl.BlockSpec((s, LANES), lambda p: (0, b0 + p), pipeline_mode=pl.Buffered(1))
    w = LANES * n_slabs
    return pl.pallas_call(
        body, name=name, grid=(n_slabs,),
        in_specs=[slab(qb0), slab(kb0), slab(vb0)],
        out_specs=[pl.BlockSpec((s, LANES), lambda p: (0, p)), pl.BlockSpec((s, LANES), lambda p: (0, p))],
        out_shape=[jax.ShapeDtypeStruct((s, w), F32), jax.ShapeDtypeStruct((s, w), F32)],
        scratch_shapes=[pltpu.VMEM((s, LANES), BF16) for _ in range(4)],
        compiler_params=_params(("parallel",), VMEM_LIMIT),
    )(proj, proj, proj)


def _store_slabs(slabs, stage, dproj_ref, sems, blocks):
    s = stage.shape[1]

    def cast(c, carry):
        rows = pl.ds(pl.multiple_of(c * NORM_CH, NORM_CH), NORM_CH)
        for i, slab in enumerate(slabs):
            stage[i, rows, :] = slab[rows, :].astype(BF16)
        return carry

    lax.fori_loop(0, s // NORM_CH, cast, 0)
    copies = [pltpu.make_async_copy(stage.at[i], dproj_ref.at[:, pl.ds(pl.multiple_of(b * LANES, LANES), LANES)],
                                    sems.at[i]) for i, b in enumerate(blocks)]
    for cp in copies:
        cp.start()
    for cp in copies:
        cp.wait()


def sb_bwd(proj, qb0, kb0, vb0, n_slabs, dout, tot, dproj, *, name):
    s = proj.shape[0]
    nq = s // BLOCK
    nch = s // NORM_CH
    nsub = SB_KG // BLOCK
    scale = HEAD_DIM ** -0.5

    def body(q_ref, k_ref, v_ref, do_ref, tot_ref, dproj_in, dproj_ref,
             qlo_s, qhi_s, k_s, v_s, dlo_s, dhi_s, dq_ref, dk_ref, dv_ref, stage, sems):
        del dproj_in
        def prep(c, carry):
            rows = pl.ds(pl.multiple_of(c * NORM_CH, NORM_CH), NORM_CH)
            lo = _lo_mask((NORM_CH, LANES))
            qv = q_ref[rows, :] * scale
            dv = do_ref[rows, :]
            qlo_s[rows, :] = jnp.where(lo, qv, 0.0).astype(BF16)
            qhi_s[rows, :] = jnp.where(lo, 0.0, qv).astype(BF16)
            dlo_s[rows, :] = jnp.where(lo, dv, 0.0).astype(BF16)
            dhi_s[rows, :] = jnp.where(lo, 0.0, dv).astype(BF16)
            k_s[rows, :] = k_ref[rows, :].astype(BF16)
            v_s[rows, :] = v_ref[rows, :].astype(BF16)
            z = jnp.zeros((NORM_CH, LANES), F32)
            dk_ref[rows, :] = z
            dv_ref[rows, :] = z
            return carry

        lax.fori_loop(0, nch, prep, 0)

        row = lax.broadcasted_iota(I32, (BLOCK, BLOCK), 0)
        col = lax.broadcasted_iota(I32, (BLOCK, BLOCK), 1)
        lo = col < HEAD_DIM
        t_le = _twice(row <= col)
        rowg = lax.broadcasted_iota(I32, (BLOCK, SB_KG), 0)
        colg = lax.broadcasted_iota(I32, (BLOCK, SB_KG), 1)

        piece = lambda x, j: x[:, j * BLOCK:(j + 1) * BLOCK]
        chains = range(2 * SB_QT)
        nc = len(chains)

        def prefixes(x):
            return [[_split_dot(piece(x[i], j), t_le) for j in range(x[i].shape[1] // BLOCK)] for i in chains]

        def chain(pre, run, total=None):
            out = []
            for pj in pre:
                out.append(run + pj if total is None else total - run - pj)
                run = run + pj[:, BLOCK - 1:BLOCK]
            return jnp.concatenate(out, axis=1), run

        def qloop(qs, phase):
            q0 = pl.multiple_of(qs * (SB_QT * BLOCK), SB_QT * BLOCK)
            tile = lambda ref, i: ref[pl.ds(q0 + (i // 2) * BLOCK, BLOCK), :]
            qh = [tile((qlo_s, qhi_s)[i % 2], i) for i in chains]
            doh = [tile((dlo_s, dhi_s)[i % 2], i) for i in chains]
            tots = [tile(tot_ref, i)[:, (i % 2) * HEAD_DIM:(i % 2) * HEAD_DIM + 1] for i in chains]
            gd = (qs * SB_QT) // nsub

            def logits(gi):
                kg = k_s[pl.ds(pl.multiple_of(gi * SB_KG, SB_KG), SB_KG), :]
                return [_dot_nt(qh[i], kg) for i in chains]

            def group(gi, st, masks, npiece=nsub):
                k0 = pl.multiple_of(gi * SB_KG, SB_KG)
                wide = npiece * BLOCK
                kg, vg = k_s[pl.ds(k0, wide), :], v_s[pl.ds(k0, wide), :]
                cp, cg, dq = list(st[:nc]), list(st[nc:2 * nc]), st[2 * nc:2 * nc + SB_QT]
                z = [zi[:, :wide] for zi in st[2 * nc + SB_QT:]]
                masked = lambda x, i: x if masks is None else jnp.where(masks[i // 2][:, :wide], x, 0.0)
                z_next = logits(jnp.minimum(gi + 1, gd))
                da = [_dot_nt(doh[i], vg) for i in chains]
                sp = [_softplus(z[i]) for i in chains]
                lrem = [masked(-sp[i], i) for i in chains]
                pre = prefixes(lrem)
                e, a, g = [], [], []
                for i in chains:
                    suffix, cp[i] = chain(pre[i], cp[i], tots[i])
                    e.append(z[i] - sp[i])
                    a.append(masked(jnp.exp(e[i] + suffix), i))
                    g.append(a[i] * da[i])
                gpre = prefixes(g)
                dz = []
                for i in chains:
                    ginc, cg[i] = chain(gpre[i], cg[i])
                    dz.append(masked(g[i] - jnp.exp(e[i]) * ginc, i).astype(BF16))
                ab = [a[i].astype(BF16) for i in chains]
                dq = [dq[t] + jnp.where(lo, _dot(dz[2 * t], kg), _dot(dz[2 * t + 1], kg)) for t in range(SB_QT)]
                rows_of = lambda x: jnp.concatenate(x, axis=0)
                dk_ref[pl.ds(k0, wide), :] += _dot_tn(rows_of(dz), rows_of(qh))
                dv_ref[pl.ds(k0, wide), :] += _dot_tn(rows_of(ab), rows_of(doh))
                return (*cp, *cg, *dq, *z_next)

            zc = [jnp.zeros((BLOCK, 1), F32)] * (2 * nc)
            zq = [jnp.zeros((BLOCK, LANES), F32)] * SB_QT
            st = lax.fori_loop(0, gd, lambda gi, st: group(gi, st, None), (*zc, *zq, *logits(0)))
            st = group(gd, st, [(gd * SB_KG + colg) < (q0 + t * BLOCK + rowg) for t in range(SB_QT)],
                       (phase + 1) * SB_QT)
            for t in range(SB_QT):
                dq_ref[pl.ds(q0 + t * BLOCK, BLOCK), :] = st[2 * nc + t] * scale

        steps_per_group = nsub // SB_QT

        def per_group(g, carry):
            for phase in range(steps_per_group):
                qloop(g * steps_per_group + phase, phase)
            return carry

        lax.fori_loop(0, nq // nsub, per_group, 0)
        p = pl.program_id(0)
        _store_slabs((dq_ref, dk_ref, dv_ref), stage, dproj_ref, sems, (qb0 + p, kb0 + p, vb0 + p))

    slab = lambda b0: pl.BlockSpec((s, LANES), lambda p: (0, b0 + p), pipeline_mode=pl.Buffered(1))
    own = pl.BlockSpec((s, LANES), lambda p: (0, p), pipeline_mode=pl.Buffered(1))
    held = pl.BlockSpec(memory_space=pl.ANY)
    return pl.pallas_call(
        body, name=name, grid=(n_slabs,),
        in_specs=[slab(qb0), slab(kb0), slab(vb0), own, own, held],
        out_specs=held, out_shape=jax.ShapeDtypeStruct(dproj.shape, dproj.dtype),
        input_output_aliases={5: 0},
        scratch_shapes=[pltpu.VMEM((s, LANES), BF16) for _ in range(6)]
        + [pltpu.VMEM((s, LANES), F32) for _ in range(3)]
        + [pltpu.VMEM((3, s, LANES), BF16), pltpu.SemaphoreType.DMA((3,))],
        compiler_params=_params(("arbitrary",), VMEM_LIMIT),
    )(proj, proj, proj, dout, tot, dproj)


def _place():
    x, y, c = lax.axis_index("x"), lax.axis_index("y"), lax.axis_index("c")
    return x, y, c


def gather_small(v, *, name):
    m_per, n = v.shape

    def body(x_ref, out_ref, send_sems, recv_sems, local_sem):
        x, y, c = _place()
        me, sibling = (x, y, c), (x, y, 1 - c)
        chips = [(1 - x, y), (x, 1 - y), (1 - x, 1 - y)]

        def rows(px, py, pc):
            return out_ref.at[pl.ds((4 * px + 2 * py + pc) * m_per, m_per), :]

        def copy(k, block, to, src=None):
            return pltpu.make_async_remote_copy(
                src_ref=rows(*block) if src is None else src, dst_ref=rows(*block),
                send_sem=send_sems.at[k], recv_sem=recv_sems.at[k], device_id=to, device_id_type=MESH)

        mine = pltpu.make_async_copy(x_ref, rows(*me), local_sem)
        mine.start()
        first = [copy(0, me, sibling, src=x_ref)]
        first += [copy(1 + j, me, (*chip, c), src=x_ref) for j, chip in enumerate(chips)]
        for cp in first:
            cp.start()
        passed = [copy(4 + j, (*chip, c), sibling) for j, chip in enumerate(chips)]
        for j, chip in enumerate(chips):
            copy(1 + j, (*chip, c), me).wait_recv()
            passed[j].start()
        copy(0, sibling, me).wait_recv()
        for j, chip in enumerate(chips):
            copy(4 + j, (*chip, 1 - c), me).wait_recv()
        for cp in first + passed:
            cp.wait_send()
        mine.wait()

    return pl.pallas_call(
        body, name=name,
        out_shape=jax.ShapeDtypeStruct((N_DEV * m_per, n), v.dtype),
        in_specs=[pl.BlockSpec(memory_space=pltpu.VMEM)],
        out_specs=pl.BlockSpec(memory_space=pltpu.VMEM),
        scratch_shapes=[pltpu.SemaphoreType.DMA((7,)), pltpu.SemaphoreType.DMA((7,)), pltpu.SemaphoreType.DMA],
        compiler_params=_params(None, VMEM_LIMIT),
    )(v)


_HBM = pl.BlockSpec(memory_space=pltpu.HBM)
_SEM = pl.BlockSpec(memory_space=pltpu.SEMAPHORE)
_EFFECT = pltpu.SideEffectType.DATAFLOW_SIDE_EFFECTING


def _peer_copies(src_refs, land_refs, send_sems, recv_sems, per_dest):
    x, y, c = _place()
    me = 4 * x + 2 * y + c
    copies = []
    for src, land, ssem, rsem in zip(src_refs, land_refs, send_sems, recv_sems):
        for k in (1, 2, 4, 3, 5, 6, 7):
            px, py, pc = x ^ (k >> 2 & 1), y ^ (k >> 1 & 1), c ^ (k & 1)
            copies.append(pltpu.make_async_remote_copy(
                src_ref=src.at[4 * px + 2 * py + pc] if per_dest else src, dst_ref=land.at[me],
                send_sem=ssem.at[k - 1], recv_sem=rsem.at[k - 1], device_id=(px, py, pc), device_id_type=MESH))
    return copies


def _own_copies(src_refs, land_refs, send_sems, per_dest):
    x, y, c = _place()
    me = 4 * x + 2 * y + c
    return [pltpu.make_async_copy(src.at[me] if per_dest else src, land.at[me], ssem.at[7])
            for src, land, ssem in zip(src_refs, land_refs, send_sems)]


def exchange_start(srcs, per_dest, *, name):
    n = len(srcs)
    lands = [lax.empty(a.shape if per_dest else (N_DEV,) + a.shape, a.dtype) for a in srcs]

    def body(*refs):
        src_refs, land_refs = refs[:n], refs[n:2 * n]
        send_sems, recv_sems = refs[2 * n:3 * n], refs[3 * n:4 * n]
        token = refs[-1]
        for cp in _peer_copies(src_refs, land_refs, send_sems, recv_sems, per_dest):
            cp.start()
        for cp in _own_copies(src_refs, land_refs, send_sems, per_dest):
            cp.start()
        token[...] = jnp.zeros_like(token)

    hbm = lambda a: pltpu.HBM(a.shape, a.dtype)
    res = pl.pallas_call(
        body, name=name,
        out_shape=(*[pltpu.SemaphoreType.DMA((8,))] * n, *[pltpu.SemaphoreType.DMA((7,))] * n,
                   *[hbm(a) for a in srcs], *[hbm(a) for a in lands], jax.ShapeDtypeStruct((8, LANES), F32)),
        in_specs=[_HBM] * (2 * n),
        out_specs=(*[_SEM] * (2 * n), *[_HBM] * (2 * n), pl.BlockSpec(memory_space=pltpu.VMEM)),
        input_output_aliases={i: 2 * n + i for i in range(2 * n)},
        compiler_params=pltpu.CompilerParams(has_side_effects=_EFFECT),
    )(*[pltpu.with_memory_space_constraint(a, pltpu.HBM) for a in (*srcs, *lands)])
    handles = [(res[a], res[n + a], res[2 * n + a], res[3 * n + a]) for a in range(n)]
    return handles, res[-1]


def exchange_wait(handles, per_dest, after, *, name):
    n = len(handles)

    def body(*refs):
        src_refs, land_refs = refs[:n], refs[n:2 * n]
        send_sems, recv_sems = refs[2 * n:3 * n], refs[3 * n:4 * n]
        for cp in _peer_copies(src_refs, land_refs, send_sems, recv_sems, per_dest):
            cp.wait_send()
            cp.wait_recv()
        for cp in _own_copies(src_refs, land_refs, send_sems, per_dest):
            cp.wait()

    srcs, lands = [h[2] for h in handles], [h[3] for h in handles]
    hbm = lambda a: pltpu.HBM(a.shape, a.dtype)
    res = pl.pallas_call(
        body, name=name,
        out_shape=(*[hbm(a) for a in srcs], *[hbm(a) for a in lands]),
        in_specs=[*[_HBM] * (2 * n), *[_SEM] * (2 * n), pl.BlockSpec(memory_space=pl.ANY)],
        out_specs=tuple([_HBM] * (2 * n)),
        input_output_aliases={i: i for i in range(2 * n)},
        compiler_params=pltpu.CompilerParams(has_side_effects=_EFFECT),
    )(*srcs, *lands, *[h[0] for h in handles], *[h[1] for h in handles], after)
    return res[n:]


def _adamw_math(w, g, m, v):
    m = ADAM_B1 * m + (1.0 - ADAM_B1) * g
    v = ADAM_B2 * v + (1.0 - ADAM_B2) * (g * g)
    m_hat = m / (1.0 - ADAM_B1 ** ADAM_STEP)
    v_hat = v / (1.0 - ADAM_B2 ** ADAM_STEP)
    delta = -ADAM_LR * (m_hat / (jnp.sqrt(v_hat) + ADAM_EPS) + ADAM_WD * w)
    return delta, m, v


def adamw_parts(parts, w, m, v, layer, outs, *, name):
    depth, r, cdim = w.shape
    n_parts = parts.shape[0]
    tr = _pick(r, [t for t in (512, 256, 128, 64, 32, 16) if t * cdim <= 256 * 1024])

    def body(p_ref, w_ref, m_ref, v_ref, g0, d0, nm0, nv0, g_ref, d_ref, nm_ref, nv_ref):
        g = p_ref[0].astype(F32)
        for q in range(1, n_parts):
            g = g + p_ref[q].astype(F32)
        delta, nm, nv = _adamw_math(w_ref[...], g, m_ref[...], v_ref[...])
        g_ref[...], d_ref[...], nm_ref[...], nv_ref[...] = g, delta, nm, nv

    t = pl.BlockSpec((None, tr, cdim), lambda i: (layer, i, 0))
    held = pl.BlockSpec(memory_space=pl.ANY)
    return pl.pallas_call(
        body, name=name, grid=(r // tr,),
        in_specs=[pl.BlockSpec((n_parts, tr, cdim), lambda i: (0, i, 0)), t, t, t, held, held, held, held],
        out_specs=[t, t, t, t],
        out_shape=[jax.ShapeDtypeStruct((depth, r, cdim), F32)] * 4,
        input_output_aliases={4: 0, 5: 1, 6: 2, 7: 3},
        compiler_params=_params(("parallel",), VMEM_LIMIT),
    )(parts, w, m, v, *outs)


def sum_devices(gathered, *, name):
    m_rows = gathered.shape[1]

    def body(ga_ref, g_ref):
        g = ga_ref[0]
        for dev in range(1, N_DEV):
            g = g + ga_ref[dev]
        g_ref[...] = g

    return pl.pallas_call(
        body, name=name, out_shape=jax.ShapeDtypeStruct((m_rows, LANES), F32),
        compiler_params=_params(None, VMEM_LIMIT),
    )(gathered)


def adamw_small(g, w, m, v, *, name):
    m_rows = w.shape[0]

    def body(g_ref, w_ref, m_ref, v_ref, d_ref, nm_ref, nv_ref):
        d_ref[...], nm_ref[...], nv_ref[...] = _adamw_math(w_ref[...], g_ref[...], m_ref[...], v_ref[...])

    return pl.pallas_call(
        body, name=name, out_shape=[jax.ShapeDtypeStruct((m_rows, LANES), F32)] * 3,
        compiler_params=_params(None, VMEM_LIMIT),
    )(g, w, m, v)


def _t5_bucket(dist):
    max_exact = N_BUCKETS // 2
    d = jnp.maximum(dist, 0)
    large = max_exact + (jnp.log(jnp.maximum(d, 1).astype(F32) / max_exact)
                         / math.log(T5_MAX_DIST / max_exact) * (N_BUCKETS - max_exact)).astype(I32)
    large = jnp.minimum(large, N_BUCKETS - 1)
    return jnp.where(d < max_exact, d, large)


def _rel():
    return jnp.arange(BLOCK)[:, None] + BLOCK - jnp.arange(2 * BLOCK)[None, :]


def _band_bias(table, dils, max_dists):
    rel = _rel()
    biases, buckets = [], []
    for d, md in zip(dils, max_dists):
        bk = _t5_bucket(rel * d)
        vis = (rel >= 0) & (rel <= md)
        looked_up = jnp.zeros((table.shape[1],) + rel.shape, F32)
        for b in range(N_BUCKETS):
            looked_up = jnp.where((bk == b)[None], table[b][:, None, None], looked_up)
        with_prev = jnp.where(vis[None], looked_up, NEG_INF)
        first = jnp.arange(2 * BLOCK)[None, None, :] >= BLOCK
        biases.append(jnp.stack([with_prev, jnp.where(first, with_prev, NEG_INF)]))
        buckets.append(bk.astype(I32))
    return jnp.stack(biases), jnp.stack(buckets)


def _pack(pieces, rows):
    flat = jnp.concatenate([p.reshape(-1) for p in pieces])
    return jnp.pad(flat, (0, rows * LANES - flat.shape[0])).reshape(rows, LANES)


def _unpack(packed, shapes):
    flat = packed.reshape(-1)
    out, off = [], 0
    for sh in shapes:
        n = math.prod(sh)
        out.append(flat[off:off + n].reshape(sh))
        off += n
    return out


def _tile2(g):
    return jnp.concatenate([g, g])


def kernel(x, attn_norm, w_in, a_q_gain, a_k_gain, a_sinks, c_q_gain, c_k_gain, rel_bias_table, mix_out_gain, w_out, ffn_norm, w_up, conv_w, conv_b, w_down, loss_target, m_attn_norm, m_w_in, m_a_q_gain, m_a_k_gain, m_a_sinks, m_c_q_gain, m_c_k_gain, m_rel_bias_table, m_mix_out_gain, m_w_out, m_ffn_norm, m_w_up, m_conv_w, m_conv_b, m_w_down, v_attn_norm, v_w_in, v_a_q_gain, v_a_k_gain, v_a_sinks, v_c_q_gain, v_c_k_gain, v_rel_bias_table, v_mix_out_gain, v_w_out, v_ffn_norm, v_w_up, v_conv_w, v_conv_b, v_w_down):
    depth, d_model, in_shard = w_in.shape
    ff2_shard = w_up.shape[2]
    s = x.shape[1]
    in_width, ff2 = N_DEV * in_shard, N_DEV * ff2_shard
    n_heads = d_model // HEAD_DIM
    ha, hb, hc = n_heads // 4, n_heads // 4, n_heads // 2
    sa, sb, sc = ha // 2, hb // 2, hc // 2
    kv_a = ha // 4
    assert kv_a == 2 and BLOCK == LANES
    cb_aq, cb_ak, cb_av = 0, sa, sa + 1
    cb_bq = sa + 2
    cb_bk, cb_bv = cb_bq + sb, cb_bq + 2 * sb
    cb_cq = cb_bq + 3 * sb
    cb_ck, cb_cv = cb_cq + sc, cb_cq + 2 * sc
    assert (cb_cv + sc) * LANES == in_width
    dev = 4 * lax.axis_index("x") + 2 * lax.axis_index("y") + lax.axis_index("c")

    per_array = 3
    wnames = ("w_in", "w_out", "w_up", "w_down", "conv_w")
    cols_to_rows = lambda g: jnp.transpose(g, (1, 0, 2)).reshape(g.shape[1], N_DEV * g.shape[2])
    whole = dict(w_in=cols_to_rows, w_up=cols_to_rows, conv_w=cols_to_rows,
                 w_out=lambda g: g.reshape(d_model, d_model), w_down=lambda g: g.reshape(ff2 // 2, d_model))
    gathers = {}
    token = jnp.zeros((8, LANES), F32)
    for l in range(depth):
        for gi, group in enumerate([[n] for n in wnames] if l < per_array else [wnames]):
            srcs = [(dict(w_in=w_in, w_out=w_out, w_up=w_up, w_down=w_down, conv_w=conv_w)[n][l] + token[0, 0])
                    .astype(F32 if n == "conv_w" else BF16) for n in group]
            handles, token = exchange_start(srcs, False, name=f"gather_start_{l}_{gi}")
            gathers.update({(l, n): h for n, h in zip(group, handles)})

    def gathered(l, names, after):
        landed = exchange_wait([gathers[l, n] for n in names], False, after,
                               name=f"gather_wait_{l}_{wnames.index(names[0])}")
        return {n: whole[n](g) for n, g in zip(names, landed)}

    bias_a, buckets_a = _band_bias(rel_bias_table[:, :ha], (1,), (WINDOW_A - 1,))
    bias_c, buckets_c = _band_bias(rel_bias_table[:, ha:], DILATIONS, (BLOCK,) * len(DILATIONS))

    xs = x[0]
    saved = []
    wi, wo, wu, wd, cw = ([None] * depth for _ in range(5))
    for l in range(depth):
        if l < per_array:
            need = lambda n, after, l=l: gathered(l, (n,), after)[n]
        else:
            layer_w = gathered(l, wnames, xs)
            need = lambda n, after: layer_w[n]
        wi[l] = need("w_in", token if l == 0 else xs)
        h1 = rmsnorm_fwd(xs, attn_norm[l], name="attn_norm_fwd")
        proj = matmul(h1, wi[l], name="in_proj")
        sinks = jnp.repeat(a_sinks[l], HEAD_DIM).reshape(sa, 1, LANES)
        gaq, gak = _tile2(a_q_gain[l]), _tile2(a_k_gain[l])
        gcq, gck = _tile2(c_q_gain[l]), _tile2(c_k_gain[l])
        out_a, lse_a = banded_fwd(proj, cb_aq, cb_ak, cb_av, sa, gaq, gak, bias_a, (1,), sinks, True, name="swa_fwd")
        out_b, tot_b = sb_fwd(proj, cb_bq, cb_bk, cb_bv, sb, name="stick_fwd")
        out_c, lse_c = banded_fwd(proj, cb_cq, cb_ck, cb_cv, sc, gcq, gck, bias_c, DILATIONS, None, False,
                                  name="dilated_fwd")
        mix = mixnorm_fwd([out_a, out_b, out_c], mix_out_gain[l], name="mix_norm_fwd")
        wo[l] = need("w_out", mix)
        x_mid = matmul(mix, wo[l], res=xs, name="out_proj")
        h2 = rmsnorm_fwd(x_mid, ffn_norm[l], name="ffn_norm_fwd")
        wu[l] = need("w_up", h2)
        p = matmul(h2, wu[l], name="up_proj")
        cw[l] = need("conv_w", p)
        act = ffn_act_fwd(p, cw[l], conv_b[l], name="ffn_act_fwd")
        wd[l] = need("w_down", act)
        x_out = matmul(act, wd[l], res=x_mid, name="down_proj")
        saved.append(dict(x_in=xs, h1=h1, proj=proj, out_a=out_a, lse_a=lse_a, out_b=out_b, tot_b=tot_b,
                          out_c=out_c, lse_c=lse_c, mix=mix, x_mid=x_mid, h2=h2, p=p, act=act,
                          sinks=sinks, gains=(gaq, gak, gcq, gck)))
        xs = x_out

    dx, dx_b, loss_part = loss_head(xs, loss_target[0], name="loss_head")

    small = {k: [None] * depth for k in ("attn_norm", "a_q_gain", "a_k_gain", "a_sinks", "c_q_gain", "c_k_gain",
                                         "mix_out_gain", "ffn_norm", "conv_w", "conv_b")}
    big = {k: [None] * depth for k in ("w_in", "w_out", "w_up", "w_down")}
    dbias_a = dbias_c = None
    scatters = {}
    token = jnp.zeros((8, LANES), F32)
    names_big = ("w_in", "w_out", "w_up", "w_down")

    def scatter(l, names):
        parts = [big[n][l] for n in names]
        handles, tok = exchange_start(parts, True, name=f"scatter_start_{l}_{names_big.index(names[0])}")
        scatters.update({(l, n): h for n, h in zip(names, handles)})
        return tok

    by_cols = lambda a: jnp.transpose(a.reshape(a.shape[0], N_DEV, a.shape[1] // N_DEV), (1, 0, 2))
    by_rows = lambda a: a.reshape(N_DEV, a.shape[0] // N_DEV, a.shape[1])
    for l in reversed(range(depth)):
        each = l == 0
        sv = saved[l]
        gaq, gak, gcq, gck = sv["gains"]
        da = matmul(dx_b, wd[l], trans_b=True, name="down_proj_dx")
        big["w_down"][l] = by_rows(matmul(sv["act"], dx_b, trans_a=True, out_dtype=BF16, name="down_proj_dw"))
        if each:
            token = scatter(l, ("w_down",))
        dp, small["conv_w"][l], small["conv_b"][l] = ffn_act_bwd(da, sv["p"], cw[l], conv_b[l] + token[0, 0],
                                                                 name="ffn_act_bwd")
        dh2 = matmul(dp, wu[l], trans_b=True, name="up_proj_dx")
        big["w_up"][l] = matmul(sv["h2"], dp, trans_a=True, out_dtype=BF16, col_blocks=N_DEV, name="up_proj_dw")
        if each:
            token = scatter(l, ("w_up",))
        dx_mid, dx_mid_b, small["ffn_norm"][l] = rmsnorm_bwd(dh2, sv["x_mid"], ffn_norm[l] + token[0, 0], dx,
                                                   name="ffn_norm_bwd")
        dmix = matmul(dx_mid_b, wo[l], trans_b=True, name="out_proj_dx")
        big["w_out"][l] = by_rows(matmul(sv["mix"], dx_mid_b, trans_a=True, out_dtype=BF16, name="out_proj_dw"))
        if each:
            token = scatter(l, ("w_out",))
        (d_oa, d_ob, d_oc), small["mix_out_gain"][l] = mixnorm_bwd(
            dmix, [sv["out_a"], sv["out_b"], sv["out_c"]], mix_out_gain[l] + token[0, 0], name="mix_norm_bwd")
        dproj = lax.empty((s, in_width), BF16)
        dproj, db_a, dgq_a, dgk_a, dsink = banded_bwd(
            sv["proj"], cb_aq, cb_ak, cb_av, sa, gaq, gak, bias_a, (1,), sv["sinks"], True,
            d_oa, sv["out_a"], sv["lse_a"], dproj, name="swa_bwd")
        dproj = sb_bwd(sv["proj"], cb_bq, cb_bk, cb_bv, sb, d_ob, sv["tot_b"], dproj, name="stick_bwd")
        dproj, db_c, dgq_c, dgk_c = banded_bwd(
            sv["proj"], cb_cq, cb_ck, cb_cv, sc, gcq, gck, bias_c, DILATIONS, None, False,
            d_oc, sv["out_c"], sv["lse_c"], dproj, name="dilated_bwd")
        fold = lambda g: g.reshape(-1, HEAD_DIM).sum(axis=0)
        small["a_q_gain"][l], small["a_k_gain"][l] = fold(dgq_a), fold(dgk_a)
        small["c_q_gain"][l], small["c_k_gain"][l] = fold(dgq_c), fold(dgk_c)
        small["a_sinks"][l] = dsink[:, ::HEAD_DIM].reshape(-1)
        dbias_a = db_a if dbias_a is None else dbias_a + db_a
        dbias_c = db_c if dbias_c is None else dbias_c + db_c
        big["w_in"][l] = by_cols(matmul(sv["h1"], dproj, trans_a=True, out_dtype=BF16, name="in_proj_dw"))
        if not each:
            token = scatter(l, names_big)
        dh1 = matmul(dproj, wi[l], trans_b=True, name="in_proj_dx")
        dx, dx_b, small["attn_norm"][l] = rmsnorm_bwd(dh1, sv["x_in"], attn_norm[l] + token[0, 0], dx_mid,
                                                name="attn_norm_bwd")

    dtable = jnp.concatenate([bias_bwd(dbias_a, buckets_a, name="swa_bias_bwd"),
                              bias_bwd(dbias_c, buckets_c, name="dilated_bias_bwd")], axis=1)

    order = ("attn_norm", "a_q_gain", "a_k_gain", "a_sinks", "c_q_gain", "c_k_gain", "rel_bias_table",
             "mix_out_gain", "ffn_norm", "conv_w", "conv_b")
    partial = {k: jnp.stack(v) for k, v in small.items()}
    partial["rel_bias_table"] = dtable
    pieces = [partial[k] for k in order] + [loss_part.reshape(1)]
    n_small = sum(math.prod(pc.shape) for pc in pieces)
    rows = -(-n_small // (8 * LANES)) * 8
    gathered = gather_small(_pack(pieces, rows), name="gather_small_grads")
    big["w_in"][0], gathered = lax.optimization_barrier((big["w_in"][0], gathered))
    scatter(0, ("w_in",))
    summed = _unpack(sum_devices(gathered.reshape(N_DEV, rows, LANES), name="sum_small_grads"),
                     [pc.shape for pc in pieces])
    g_small = dict(zip(order, summed[:-1]))
    loss = summed[-1][0]
    g_small["conv_w"] = lax.dynamic_slice_in_dim(g_small["conv_w"], dev * ff2_shard, ff2_shard, axis=2)

    w_small = dict(attn_norm=attn_norm, a_q_gain=a_q_gain, a_k_gain=a_k_gain, a_sinks=a_sinks, c_q_gain=c_q_gain,
                   c_k_gain=c_k_gain, rel_bias_table=rel_bias_table, mix_out_gain=mix_out_gain, ffn_norm=ffn_norm,
                   conv_w=conv_w, conv_b=conv_b)
    m_small = dict(attn_norm=m_attn_norm, a_q_gain=m_a_q_gain, a_k_gain=m_a_k_gain, a_sinks=m_a_sinks,
                   c_q_gain=m_c_q_gain, c_k_gain=m_c_k_gain, rel_bias_table=m_rel_bias_table,
                   mix_out_gain=m_mix_out_gain, ffn_norm=m_ffn_norm, conv_w=m_conv_w, conv_b=m_conv_b)
    v_small = dict(attn_norm=v_attn_norm, a_q_gain=v_a_q_gain, a_k_gain=v_a_k_gain, a_sinks=v_a_sinks,
                   c_q_gain=v_c_q_gain, c_k_gain=v_c_k_gain, rel_bias_table=v_rel_bias_table,
                   mix_out_gain=v_mix_out_gain, ffn_norm=v_ffn_norm, conv_w=v_conv_w, conv_b=v_conv_b)
    shapes = [w_small[k].shape for k in order]
    n_upd = sum(math.prod(sh) for sh in shapes)
    urows = -(-n_upd // (8 * LANES)) * 8
    packs = [_pack([d[k] for k in order], urows) for d in (g_small, w_small, m_small, v_small)]
    upd = adamw_small(*packs, name="adamw_small")
    delta_s, newm_s, newv_s = [dict(zip(order, _unpack(u, shapes))) for u in upd]

    w_big = dict(w_in=(w_in, m_w_in, v_w_in), w_out=(w_out, m_w_out, v_w_out), w_up=(w_up, m_w_up, v_w_up),
                 w_down=(w_down, m_w_down, v_w_down))
    results = {k: [lax.empty(w_big[k][0].shape, F32) for _ in range(4)] for k in names_big}
    after = upd[0]
    for l, names in [(l, names_big) for l in reversed(range(1, depth))] + [(0, names_big[1:]), (0, names_big[:1])]:
        landed = exchange_wait([scatters[l, n] for n in names], True, after,
                               name=f"scatter_wait_{l}_{names_big.index(names[0])}")
        for k, parts in zip(names, landed):
            results[k] = adamw_parts(parts, *w_big[k], l, results[k], name="adamw_large")
        after = results[names[-1]][0]
    g_big, delta_b, newm_b, newv_b = [{k: results[k][i] for k in names_big} for i in range(4)]

    all_names = ("attn_norm", "w_in", "a_q_gain", "a_k_gain", "a_sinks", "c_q_gain", "c_k_gain", "rel_bias_table",
                 "mix_out_gain", "w_out", "ffn_norm", "w_up", "conv_w", "conv_b", "w_down")
    pick = lambda sm, bg: [bg[k] if k in bg else sm[k] for k in all_names]
    return (loss, dx[None], *pick(g_small, g_big), *pick(delta_s, delta_b), *pick(newm_s, newm_b),
            *pick(newv_s, newv_b))
```

```python
import functools
import math

import jax
import jax.numpy as jnp
from jax import lax
from jax.experimental import pallas as pl
from jax.experimental.pallas import tpu as pltpu

F32, BF16, I32 = jnp.float32, jnp.bfloat16, jnp.int32
MESH = pl.DeviceIdType.MESH

HEAD_DIM = 64
LANES = 128
BLOCK = 128
EPS = 1e-6
NEG_INF = -1e30
N_BUCKETS = 32
T5_MAX_DIST = 2048
WINDOW_A = 128
DILATIONS = (1, 4, 16)
N_DEV = 8
VMEM_LIMIT = 56 * 1024 * 1024
MATMUL_VMEM = 46 * 1024 * 1024

ADAM_LR, ADAM_B1, ADAM_B2, ADAM_EPS, ADAM_WD, ADAM_STEP = 0.001, 0.9, 0.999, 1e-08, 0.01, 10


def _params(sem=None, vmem=None):
    return pltpu.CompilerParams(dimension_semantics=sem, vmem_limit_bytes=vmem)


def _pick(n, cands):
    for c in cands:
        if n % c == 0:
            return c
    raise ValueError(f"no tile for {n}")


def _dot(a, b):
    return lax.dot_general(a, b, (((1,), (0,)), ((), ())), preferred_element_type=F32)


def _dot_nt(a, b):
    return lax.dot_general(a, b, (((1,), (1,)), ((), ())), preferred_element_type=F32)


def _dot_tn(a, b):
    return lax.dot_general(a, b, (((0,), (0,)), ((), ())), preferred_element_type=F32)


def matmul(a, b, *, trans_a=False, trans_b=False, out_dtype=F32, res=None, col_blocks=None, name):
    a_halves, b_halves = a.ndim == 3, b.ndim == 3
    assert not (a_halves and trans_a) and not (b_halves and trans_b)
    m, k = (a.shape[1], 2 * a.shape[2]) if a_halves else (a.shape[1], a.shape[0]) if trans_a else a.shape
    n = 2 * b.shape[2] if b_halves else b.shape[0] if trans_b else b.shape[1]
    k_unit, n_unit = (k // 2 if a_halves else k), (n // 2 if b_halves else n)
    tm = _pick(m, (1408, 1024, 512, 256))
    tn_cands = ((n // col_blocks,) if col_blocks
                else tuple(t for t in (1024, 1408, 768, 512, 256, 128) if n_unit % t == 0))

    def footprint(tk, tn):
        tiles = 2 * (tm * tk * a.dtype.itemsize + tk * tn * b.dtype.itemsize)
        return tiles + tm * tn * (4 + 2 * jnp.dtype(out_dtype).itemsize + (8 if res is not None else 0))

    tk, tn = next((tk, tn) for tk in (5632, 5376, 4096, 2816, 2048, 1792, 1024, 768, 512, 256) if k_unit % tk == 0
                  for tn in tn_cands if footprint(tk, tn) <= MATMUL_VMEM)
    nk = k // tk
    nk_half, nj_half = k_unit // tk, n_unit // tn
    dn = (((0 if trans_a else 1,), (1 if trans_b else 0,)), ((), ()))

    def body(*refs):
        if res is None:
            a_ref, b_ref, o_ref, acc = refs
        else:
            a_ref, b_ref, r_ref, o_ref, acc = refs
        kk = pl.program_id(2)

        @pl.when(kk == 0)
        def _():
            acc[...] = jnp.zeros_like(acc)

        acc[...] += lax.dot_general(a_ref[...].astype(BF16), b_ref[...].astype(BF16), dn,
                                    preferred_element_type=F32)

        @pl.when(kk == nk - 1)
        def _():
            r = acc[...]
            if res is not None:
                r = r_ref[...] + r
            o_ref[...] = r.astype(out_dtype)

    b_spec = (pl.BlockSpec((tn, tk), lambda i, j, kk: (j, kk)) if trans_b
              else pl.BlockSpec((None, tk, tn), lambda i, j, kk: (j // nj_half, kk, j % nj_half)) if b_halves
              else pl.BlockSpec((tk, tn), lambda i, j, kk: (kk, j)))
    a_spec = (pl.BlockSpec((tk, tm), lambda i, j, kk: (kk, i)) if trans_a
              else pl.BlockSpec((None, tm, tk), lambda i, j, kk: (kk // nk_half, i, kk % nk_half)) if a_halves
              else pl.BlockSpec((tm, tk), lambda i, j, kk: (i, kk)))
    in_specs = [a_spec, b_spec]
    args = [a, b]
    if res is not None:
        in_specs.append(pl.BlockSpec((tm, tn), lambda i, j, kk: (i, j)))
        args.append(res)
    if col_blocks:
        out_spec = pl.BlockSpec((None, tm, tn), lambda i, j, kk: (j, i, 0))
        out_shape = jax.ShapeDtypeStruct((col_blocks, m, tn), out_dtype)
    else:
        out_spec = pl.BlockSpec((tm, tn), lambda i, j, kk: (i, j))
        out_shape = jax.ShapeDtypeStruct((m, n), out_dtype)
    return pl.pallas_call(
        body, name=name, grid=(m // tm, n // tn, nk),
        in_specs=in_specs, out_specs=out_spec, out_shape=out_shape,
        scratch_shapes=[pltpu.VMEM((tm, tn), F32)],
        compiler_params=_params(("parallel", "parallel", "arbitrary"), VMEM_LIMIT),
    )(*args)


def rmsnorm_fwd(x, g, *, name):
    s, d = x.shape
    tm = 512

    def body(x_ref, g_ref, o_ref):
        xv = x_ref[...]
        r = lax.rsqrt(jnp.mean(xv * xv, axis=-1, keepdims=True) + EPS)
        o_ref[...] = (xv * r * g_ref[...]).astype(BF16)

    return pl.pallas_call(
        body, name=name, grid=(s // tm,),
        in_specs=[pl.BlockSpec((tm, d), lambda i: (i, 0)), pl.BlockSpec((1, d), lambda i: (0, 0))],
        out_specs=pl.BlockSpec((tm, d), lambda i: (i, 0)),
        out_shape=jax.ShapeDtypeStruct((s, d), BF16),
        compiler_params=_params(("parallel",)),
    )(x, g.reshape(1, d))


def rmsnorm_bwd(dh, x, g, dres, *, name):
    s, d = x.shape
    tm = 256

    def body(dh_ref, x_ref, g_ref, dres_ref, dx_ref, dxb_ref, dg_ref):
        @pl.when(pl.program_id(0) == 0)
        def _():
            dg_ref[...] = jnp.zeros_like(dg_ref)

        xv, dhv = x_ref[...], dh_ref[...]
        r = lax.rsqrt(jnp.mean(xv * xv, axis=-1, keepdims=True) + EPS)
        gd = dhv * g_ref[...]
        dot = jnp.mean(gd * xv, axis=-1, keepdims=True)
        dx = dres_ref[...] + (r * gd - xv * (r * r * r * dot))
        dx_ref[...] = dx
        dxb_ref[...] = dx.astype(BF16)
        dg_ref[...] += jnp.sum(dhv * (xv * r), axis=0, keepdims=True)

    row = pl.BlockSpec((tm, d), lambda i: (i, 0))
    dx, dxb, dg = pl.pallas_call(
        body, name=name, grid=(s // tm,),
        in_specs=[row, row, pl.BlockSpec((1, d), lambda i: (0, 0)), row],
        out_specs=[row, row, pl.BlockSpec((1, d), lambda i: (0, 0))],
        out_shape=[jax.ShapeDtypeStruct((s, d), F32), jax.ShapeDtypeStruct((s, d), BF16),
                   jax.ShapeDtypeStruct((1, d), F32)],
        compiler_params=_params(("arbitrary",)),
    )(dh, x, g.reshape(1, d), dres)
    return dx, dxb, dg[0]


def loss_head(y, target, *, name):
    s, d = y.shape
    tm = 512

    def body(y_ref, t_ref, dy_ref, dyb_ref, l_ref):
        @pl.when(pl.program_id(0) == 0)
        def _():
            l_ref[...] = jnp.zeros_like(l_ref)

        e = y_ref[...] - t_ref[...]
        dy = e / float(d)
        dy_ref[...] = dy
        dyb_ref[...] = dy.astype(BF16)
        per_tok = jnp.mean(e * e, axis=-1, keepdims=True)
        l_ref[...] += 0.5 * jnp.sum(per_tok, axis=0, keepdims=True)

    row = pl.BlockSpec((tm, d), lambda i: (i, 0))
    dy, dyb, l = pl.pallas_call(
        body, name=name, grid=(s // tm,),
        in_specs=[row, row],
        out_specs=[row, row, pl.BlockSpec((8, LANES), lambda i: (0, 0))],
        out_shape=[jax.ShapeDtypeStruct((s, d), F32), jax.ShapeDtypeStruct((s, d), BF16),
                   jax.ShapeDtypeStruct((8, LANES), F32)],
        compiler_params=_params(("arbitrary",)),
    )(y, target)
    return dy, dyb, l[0, 0]


FFN_TN = 256
FFN_CH = 256


def _rows_before(ref, r0, first):
    if first:
        cur = ref[pl.ds(0, FFN_CH), :]
        row = lax.broadcasted_iota(I32, cur.shape, 0)
        sh1 = jnp.where(row < 1, 0.0, pltpu.roll(cur, 1, axis=0))
        sh2 = jnp.where(row < 2, 0.0, pltpu.roll(cur, 2, axis=0))
        return cur, sh1, sh2
    ext = ref[pl.ds(pl.multiple_of(r0 - 8, 8), FFN_CH + 8), :]
    return ext[8:], pltpu.roll(ext, 1, axis=0)[8:], pltpu.roll(ext, 2, axis=0)[8:]


def _rows_after(ref, r0, last):
    if last:
        cur = ref[pl.ds(r0, FFN_CH), :]
        row = lax.broadcasted_iota(I32, cur.shape, 0)
        up1 = jnp.where(row >= FFN_CH - 1, 0.0, pltpu.roll(cur, FFN_CH - 1, axis=0))
        up2 = jnp.where(row >= FFN_CH - 2, 0.0, pltpu.roll(cur, FFN_CH - 2, axis=0))
        return cur, up1, up2
    n = FFN_CH + 8
    ext = ref[pl.ds(r0, n), :]
    return ext[:FFN_CH], pltpu.roll(ext, n - 1, axis=0)[:FFN_CH], pltpu.roll(ext, n - 2, axis=0)[:FFN_CH]


def _sigmoid(x):
    return 0.5 * jnp.tanh(0.5 * x) + 0.5


def ffn_act_fwd(p, conv_w, conv_b, *, name):
    s, f2 = p.shape
    f = f2 // 2
    nj = f // FFN_TN
    nch = s // FFN_CH

    def body(pg_ref, pu_ref, wg_ref, wu_ref, bg_ref, bu_ref, a_ref):
        def conv(ref, w_ref, b_ref, r0, first):
            cur, sh1, sh2 = _rows_before(ref, r0, first)
            return ((b_ref[...] + w_ref[0:1, :] * sh2) + w_ref[1:2, :] * sh1) + w_ref[2:3, :] * cur

        def chunk(r0, first):
            gate = conv(pg_ref, wg_ref, bg_ref, r0, first)
            up = conv(pu_ref, wu_ref, bu_ref, r0, first)
            a_ref[pl.ds(r0, FFN_CH), :] = (gate * _sigmoid(gate) * up).astype(BF16)

        chunk(0, True)

        def step(c, carry):
            chunk(pl.multiple_of(c * FFN_CH, FFN_CH), False)
            return carry

        lax.fori_loop(1, nch, step, 0)

    col = lambda off: pl.BlockSpec((s, FFN_TN), lambda j: (0, j + off))
    wcol = lambda off: pl.BlockSpec((3, FFN_TN), lambda j: (0, j + off))
    bcol = lambda off: pl.BlockSpec((1, FFN_TN), lambda j: (0, j + off))
    return pl.pallas_call(
        body, name=name, grid=(nj,),
        in_specs=[col(0), col(nj), wcol(0), wcol(nj), bcol(0), bcol(nj)],
        out_specs=pl.BlockSpec((s, FFN_TN), lambda j: (0, j)),
        out_shape=jax.ShapeDtypeStruct((s, f), BF16),
        compiler_params=_params(("parallel",), VMEM_LIMIT),
    )(p, p, conv_w, conv_w, conv_b.reshape(1, f2), conv_b.reshape(1, f2))


def ffn_act_bwd(da, p, conv_w, conv_b, *, name):
    s, f2 = p.shape
    f = f2 // 2
    nj = f // FFN_TN
    nch = s // FFN_CH

    def body(da_ref, pg_ref, pu_ref, wg_ref, wu_ref, bg_ref, bu_ref,
             dp_ref, dwg_ref, dwu_ref, dbg_ref, dbu_ref, dug_s, duu_s):
        dpg_ref, dpu_ref = dp_ref.at[0], dp_ref.at[1]
        def conv(ref, w_ref, b_ref, r0, first):
            cur, sh1, sh2 = _rows_before(ref, r0, first)
            u = ((b_ref[...] + w_ref[0:1, :] * sh2) + w_ref[1:2, :] * sh1) + w_ref[2:3, :] * cur
            return u, (sh2, sh1, cur)

        def taps_sum(du, taps):
            return jnp.concatenate([jnp.sum(du * t, axis=0, keepdims=True) for t in taps], axis=0)

        def chunk(r0, first, acc):
            dwg, dwu, dbg, dbu = acc
            gate, tg = conv(pg_ref, wg_ref, bg_ref, r0, first)
            up, tu = conv(pu_ref, wu_ref, bu_ref, r0, first)
            dav = da_ref[pl.ds(r0, FFN_CH), :]
            sg = _sigmoid(gate)
            dgate = dav * up * (sg * (1.0 + gate * (1.0 - sg)))
            dup = dav * (gate * sg)
            dug_s[pl.ds(r0, FFN_CH), :] = dgate
            duu_s[pl.ds(r0, FFN_CH), :] = dup
            return (dwg + taps_sum(dgate, tg), dwu + taps_sum(dup, tu),
                    dbg + jnp.sum(dgate, axis=0, keepdims=True), dbu + jnp.sum(dup, axis=0, keepdims=True))

        z3 = jnp.zeros((3, FFN_TN), F32)
        z1 = jnp.zeros((1, FFN_TN), F32)
        acc = chunk(0, True, (z3, z3, z1, z1))
        acc = lax.fori_loop(1, nch, lambda c, a: chunk(pl.multiple_of(c * FFN_CH, FFN_CH), False, a), acc)
        dwg_ref[...], dwu_ref[...], dbg_ref[...], dbu_ref[...] = acc

        def back(src, w_ref, dst, r0, last):
            cur, up1, up2 = _rows_after(src, r0, last)
            dst[pl.ds(r0, FFN_CH), :] = (w_ref[2:3, :] * cur + w_ref[1:2, :] * up1 + w_ref[0:1, :] * up2).astype(BF16)

        def step(c, carry):
            r0 = pl.multiple_of(c * FFN_CH, FFN_CH)
            back(dug_s, wg_ref, dpg_ref, r0, False)
            back(duu_s, wu_ref, dpu_ref, r0, False)
            return carry

        lax.fori_loop(0, nch - 1, step, 0)
        back(dug_s, wg_ref, dpg_ref, (nch - 1) * FFN_CH, True)
        back(duu_s, wu_ref, dpu_ref, (nch - 1) * FFN_CH, True)

    col = lambda off: pl.BlockSpec((s, FFN_TN), lambda j: (0, j + off))
    wcol = lambda off: pl.BlockSpec((3, FFN_TN), lambda j: (0, j + off))
    bcol = lambda off: pl.BlockSpec((1, FFN_TN), lambda j: (0, j + off))
    outs = pl.pallas_call(
        body, name=name, grid=(nj,),
        in_specs=[col(0), col(0), col(nj), wcol(0), wcol(nj), bcol(0), bcol(nj)],
        out_specs=[pl.BlockSpec((2, s, FFN_TN), lambda j: (0, 0, j)), wcol(0), wcol(0), bcol(0), bcol(0)],
        out_shape=[jax.ShapeDtypeStruct((2, s, f), BF16),
                   jax.ShapeDtypeStruct((3, f), F32), jax.ShapeDtypeStruct((3, f), F32),
                   jax.ShapeDtypeStruct((1, f), F32), jax.ShapeDtypeStruct((1, f), F32)],
        scratch_shapes=[pltpu.VMEM((s, FFN_TN), F32), pltpu.VMEM((s, FFN_TN), F32)],
        compiler_params=_params(("parallel",), VMEM_LIMIT),
    )(da, p, p, conv_w, conv_w, conv_b.reshape(1, f2), conv_b.reshape(1, f2))
    dp, dwg, dwu, dbg, dbu = outs
    return dp, jnp.concatenate([dwg, dwu], axis=1), jnp.concatenate([dbg, dbu], axis=1)[0]


def mixnorm_fwd(outs, gain, *, name):
    s = outs[0].shape[0]
    widths = [o.shape[1] for o in outs]
    total = sum(widths)
    tm = 512

    def body(*refs):
        o_refs, g_ref, m_ref = refs[:-2], refs[-2], refs[-1]
        off = 0
        for o_ref, w in zip(o_refs, widths):
            xv = o_ref[...]
            r = lax.rsqrt(jnp.mean(xv * xv, axis=-1, keepdims=True) + EPS)
            m_ref[:, off:off + w] = (xv * r * g_ref[:, off:off + w]).astype(BF16)
            off += w

    return pl.pallas_call(
        body, name=name, grid=(s // tm,),
        in_specs=[pl.BlockSpec((tm, w), lambda i: (i, 0)) for w in widths] + [pl.BlockSpec((1, total), lambda i: (0, 0))],
        out_specs=pl.BlockSpec((tm, total), lambda i: (i, 0)),
        out_shape=jax.ShapeDtypeStruct((s, total), BF16),
        compiler_params=_params(("parallel",)),
    )(*outs, gain.reshape(1, total))


def mixnorm_bwd(dmix, outs, gain, *, name):
    s = outs[0].shape[0]
    widths = [o.shape[1] for o in outs]
    total = sum(widths)
    n = len(outs)
    tm = 256

    def body(*refs):
        dm_ref, o_refs, g_ref = refs[0], refs[1:1 + n], refs[1 + n]
        d_refs, dg_ref = refs[2 + n:2 + 2 * n], refs[2 + 2 * n]

        @pl.when(pl.program_id(0) == 0)
        def _():
            dg_ref[...] = jnp.zeros_like(dg_ref)

        off = 0
        for o_ref, d_ref, w in zip(o_refs, d_refs, widths):
            xv = o_ref[...]
            dhv = dm_ref[:, off:off + w]
            r = lax.rsqrt(jnp.mean(xv * xv, axis=-1, keepdims=True) + EPS)
            gd = dhv * g_ref[:, off:off + w]
            dot = jnp.mean(gd * xv, axis=-1, keepdims=True)
            d_ref[...] = r * gd - xv * (r * r * r * dot)
            dg_ref[:, off:off + w] += jnp.sum(dhv * (xv * r), axis=0, keepdims=True)
            off += w

    res = pl.pallas_call(
        body, name=name, grid=(s // tm,),
        in_specs=[pl.BlockSpec((tm, total), lambda i: (i, 0))]
        + [pl.BlockSpec((tm, w), lambda i: (i, 0)) for w in widths] + [pl.BlockSpec((1, total), lambda i: (0, 0))],
        out_specs=[pl.BlockSpec((tm, w), lambda i: (i, 0)) for w in widths] + [pl.BlockSpec((1, total), lambda i: (0, 0))],
        out_shape=[jax.ShapeDtypeStruct((s, w), F32) for w in widths] + [jax.ShapeDtypeStruct((1, total), F32)],
        compiler_params=_params(("arbitrary",)),
    )(dmix, *outs, gain.reshape(1, total))
    return res[:n], res[n][0]


NORM_CH = 512
FWD_TILES = 4
BWD_TILES = 4


def _lo_mask(shape):
    return lax.broadcasted_iota(I32, shape, 1) < HEAD_DIM


def _head_sum(x, lo):
    del lo
    i = lax.broadcasted_iota(I32, (LANES, LANES), 0) // HEAD_DIM
    j = lax.broadcasted_iota(I32, (LANES, LANES), 1) // HEAD_DIM
    return _split_dot(x, _twice(i == j))


def _head_stats(x, lo):
    return lax.rsqrt(_head_sum(x * x, lo) * (1.0 / HEAD_DIM) + EPS)


def _swap_halves(x):
    return pltpu.roll(x, HEAD_DIM, axis=1)


def _replicate_head(x, lo, use_lo_head):
    sw = _swap_halves(x)
    return jnp.where(use_lo_head, jnp.where(lo, x, sw), jnp.where(lo, sw, x))


def _tile_rows(i, s, d):
    nb = s // (BLOCK * d)
    r = i // nb
    b = i % nb
    start = r + (BLOCK * d) * b
    prev = start - (BLOCK * d) * jnp.minimum(b, 1)
    return start, prev, b > 0


def _rows(ref, start, d):
    if d == 1:
        return ref[pl.ds(pl.multiple_of(start, BLOCK), BLOCK), :]
    return ref[pl.ds(start, BLOCK, stride=d), :]


def _set_rows(ref, start, d, val):
    if d == 1:
        ref[pl.ds(pl.multiple_of(start, BLOCK), BLOCK), :] = val
    else:
        ref[pl.ds(start, BLOCK, stride=d), :] = val


def banded_fwd(proj, qb0, kb0, vb0, n_slabs, gq, gk, bias, dils, sinks, gqa, *, name):
    s = proj.shape[0]
    nbr = len(dils)
    nt = s // BLOCK
    nch = s // NORM_CH
    has_sink = sinks is not None

    def body(*refs):
        q_ref, k_ref, v_ref, gq_ref, gk_ref, b_ref = refs[:6]
        rest = refs[6:]
        if has_sink:
            sink_ref, rest = rest[0], rest[1:]
        out_ref, lse_ref, qn_s, kn_s, vv_s, o_s, l_s = rest
        p = pl.program_id(0)
        use_lo = (p // 2) == 0

        def prep(c, carry):
            rows = pl.ds(pl.multiple_of(c * NORM_CH, NORM_CH), NORM_CH)
            lo = _lo_mask((NORM_CH, LANES))
            qv, kv, vv = q_ref[rows, :], k_ref[rows, :], v_ref[rows, :]
            qn_s[rows, :] = qv * _head_stats(qv, lo) * gq_ref[...] * (HEAD_DIM ** -0.5)
            kn = kv * _head_stats(kv, lo) * gk_ref[...]
            if gqa:
                kn = _replicate_head(kn, lo, use_lo)
                vv = _replicate_head(vv, lo, use_lo)
            kn_s[rows, :] = kn
            vv_s[rows, :] = vv
            return carry

        lax.fori_loop(0, nch, prep, 0)

        lo = _lo_mask((BLOCK, LANES))
        hms = [lo, jnp.logical_not(lo)]
        heads, tiles = range(2), range(FWD_TILES)
        for br, d in enumerate(dils):
            def step(ii, carry, br=br, d=d):
                pos = [_tile_rows(ii * FWD_TILES + u, s, d) for u in tiles]
                kc = [carry[0]] + [_rows(kn_s, pos[u][0], d).astype(BF16) for u in tiles]
                vc = [carry[1]] + [_rows(vv_s, pos[u][0], d).astype(BF16) for u in tiles]
                kcat = [jnp.concatenate([kc[u], kc[u + 1]], axis=0) for u in tiles]
                vcat = [jnp.concatenate([vc[u], vc[u + 1]], axis=0) for u in tiles]
                qt = [_rows(qn_s, pos[u][0], d) for u in tiles]
                sc = [[_dot_nt(jnp.where(hms[h], qt[u], 0.0).astype(BF16), kcat[u])
                       + b_ref[br, jnp.where(pos[u][2], 0, 1), h] for h in heads] for u in tiles]
                m = [[jnp.max(sc[u][h], axis=1, keepdims=True) for h in heads] for u in tiles]
                pe = [[jnp.exp(sc[u][h] - m[u][h]) for h in heads] for u in tiles]
                den = [[jnp.sum(pe[u][h], axis=1, keepdims=True) for h in heads] for u in tiles]
                o = [[_dot(pe[u][h].astype(BF16), vcat[u]) * (1.0 / den[u][h]) for h in heads] for u in tiles]
                for u in tiles:
                    _set_rows(o_s.at[br], pos[u][0], d, jnp.where(lo, o[u][0], o[u][1]))
                    _set_rows(l_s.at[br], pos[u][0], d,
                              jnp.where(lo, m[u][0] + jnp.log(den[u][0]), m[u][1] + jnp.log(den[u][1])))
                return kc[-1], vc[-1]

            none_yet = jnp.zeros((BLOCK, LANES), BF16)
            lax.fori_loop(0, nt // FWD_TILES, step, (none_yet, none_yet))

        def combine(c, carry):
            rows = pl.ds(pl.multiple_of(c * NORM_CH, NORM_CH), NORM_CH)
            ls = [l_s[br, rows, :] for br in range(nbr)]
            mx = functools.reduce(jnp.maximum, ls)
            if has_sink:
                mx = jnp.maximum(mx, sink_ref[...])
            tot = functools.reduce(jnp.add, [jnp.exp(l - mx) for l in ls])
            if has_sink:
                tot = tot + jnp.exp(sink_ref[...] - mx)
            lse = mx + jnp.log(tot)
            acc = jnp.exp(ls[0] - lse) * o_s[0, rows, :]
            for br in range(1, nbr):
                acc = acc + jnp.exp(ls[br] - lse) * o_s[br, rows, :]
            out_ref[rows, :] = acc
            lse_ref[rows, :] = lse
            return carry

        lax.fori_loop(0, nch, combine, 0)

    slab = lambda b0, shared: pl.BlockSpec((s, LANES), (lambda p: (0, b0)) if shared else (lambda p: (0, b0 + p)),
                                           pipeline_mode=pl.Buffered(1))
    vec = pl.BlockSpec((1, LANES), lambda p: (0, 0))
    in_specs = [slab(qb0, False), slab(kb0, gqa), slab(vb0, gqa), vec, vec,
                pl.BlockSpec((nbr, 2, 2, BLOCK, 2 * BLOCK), lambda p: (0, 0, p, 0, 0))]
    args = [proj, proj, proj, gq.reshape(1, LANES), gk.reshape(1, LANES), bias]
    if has_sink:
        in_specs.append(pl.BlockSpec((None, 1, LANES), lambda p: (p, 0, 0)))
        args.append(sinks)
    w = LANES * n_slabs
    return pl.pallas_call(
        body, name=name, grid=(n_slabs,),
        in_specs=in_specs,
        out_specs=[pl.BlockSpec((s, LANES), lambda p: (0, p)), pl.BlockSpec((s, LANES), lambda p: (0, p))],
        out_shape=[jax.ShapeDtypeStruct((s, w), F32), jax.ShapeDtypeStruct((s, w), F32)],
        scratch_shapes=[pltpu.VMEM((s, LANES), F32), pltpu.VMEM((s, LANES), F32), pltpu.VMEM((s, LANES), F32),
                        pltpu.VMEM((nbr, s, LANES), F32), pltpu.VMEM((nbr, s, LANES), F32)],
        compiler_params=_params(("parallel",), VMEM_LIMIT),
    )(*args)


def banded_bwd(proj, qb0, kb0, vb0, n_slabs, gq, gk, bias, dils, sinks, gqa, dout, out, lse, dproj, *, name):
    s = proj.shape[0]
    nbr = len(dils)
    nt = s // BLOCK
    nch = s // NORM_CH
    has_sink = sinks is not None
    scale = HEAD_DIM ** -0.5

    def body(*refs):
        q_ref, k_ref, v_ref, gq_ref, gk_ref, b_ref, do_ref, o_ref, lse_ref = refs[:9]
        rest = refs[9:]
        if has_sink:
            sink_ref, rest = rest[0], rest[1:]
        dproj_ref, db_ref, dgq_ref, dgk_ref = rest[1:5]
        rest = rest[5:]
        if has_sink:
            dsink_ref, rest = rest[0], rest[1:]
        qn_s, kn_s, vv_s, dl_s, dqn_s, dkn_s, dvv_s, dq_ref, dk_ref, dv_ref, stage, sems = rest
        p = pl.program_id(0)
        use_lo = (p // 2) == 0

        def prep(c, carry):
            rows = pl.ds(pl.multiple_of(c * NORM_CH, NORM_CH), NORM_CH)
            lo = _lo_mask((NORM_CH, LANES))
            qv, kv, vv = q_ref[rows, :], k_ref[rows, :], v_ref[rows, :]
            qn_s[rows, :] = qv * _head_stats(qv, lo) * gq_ref[...] * scale
            kn = kv * _head_stats(kv, lo) * gk_ref[...]
            if gqa:
                kn = _replicate_head(kn, lo, use_lo)
                vv = _replicate_head(vv, lo, use_lo)
            kn_s[rows, :] = kn
            vv_s[rows, :] = vv
            delta = _head_sum(do_ref[rows, :] * o_ref[rows, :], lo)
            odd = lax.broadcasted_iota(I32, (NORM_CH, LANES), 1) % 2 == 1
            dl_s[rows, :] = jnp.where(odd, delta, lse_ref[rows, :])
            z = jnp.zeros((NORM_CH, LANES), F32)
            dqn_s[rows, :] = z
            dkn_s[rows, :] = z
            dvv_s[rows, :] = z
            if has_sink:
                ps = jnp.exp(sink_ref[...] - lse_ref[rows, :])
                return carry - jnp.sum(ps * delta, axis=0, keepdims=True)
            return carry

        dsink = lax.fori_loop(0, nch, prep, jnp.zeros((1, LANES), F32))
        if has_sink:
            dsink_ref[...] = jnp.broadcast_to(dsink, (8, LANES))

        lo = _lo_mask((BLOCK, LANES))
        hms = [lo, jnp.logical_not(lo)]
        heads, tiles = range(2), range(BWD_TILES)
        for br, d in enumerate(dils):
            db_ref[br] = jnp.zeros((2, BLOCK, 2 * BLOCK), F32)

            def step(ii, carry, br=br, d=d):
                pos = [_tile_rows(ii * BWD_TILES + u, s, d) for u in tiles]
                kc = [carry[0]] + [_rows(kn_s, pos[u][0], d).astype(BF16) for u in tiles]
                vc = [carry[1]] + [_rows(vv_s, pos[u][0], d).astype(BF16) for u in tiles]
                kcat = [jnp.concatenate([kc[u], kc[u + 1]], axis=0) for u in tiles]
                vcat = [jnp.concatenate([vc[u], vc[u + 1]], axis=0) for u in tiles]
                qt = [_rows(qn_s, pos[u][0], d) for u in tiles]
                dot_ = [_rows(do_ref, pos[u][0], d) for u in tiles]
                st_t = [_rows(dl_s, pos[u][0], d) for u in tiles]
                qh = [[jnp.where(hms[h], qt[u], 0.0).astype(BF16) for h in heads] for u in tiles]
                doh = [[jnp.where(hms[h], dot_[u], 0.0).astype(BF16) for h in heads] for u in tiles]
                sc = [[_dot_nt(qh[u][h], kcat[u]) + b_ref[br, jnp.where(pos[u][2], 0, 1), h] for h in heads]
                      for u in tiles]
                dp = [[_dot_nt(doh[u][h], vcat[u]) for h in heads] for u in tiles]
                lane0 = [0, HEAD_DIM]
                pr = [[jnp.exp(sc[u][h] - st_t[u][:, lane0[h]:lane0[h] + 1]) for h in heads] for u in tiles]
                dlog = [[pr[u][h] * (dp[u][h] - st_t[u][:, lane0[h] + 1:lane0[h] + 2]) for h in heads] for u in tiles]
                for h in heads:
                    db_ref[br, h] += functools.reduce(jnp.add, [dlog[u][h] for u in tiles])
                dlb = [[dlog[u][h].astype(BF16) for h in heads] for u in tiles]
                prb = [[pr[u][h].astype(BF16) for h in heads] for u in tiles]
                dq_t = [jnp.where(lo, _dot(dlb[u][0], kcat[u]), _dot(dlb[u][1], kcat[u])) * scale for u in tiles]
                rows2 = lambda x: jnp.concatenate(x, axis=0)
                dk_t = [_dot_tn(rows2(dlb[u]), rows2(qh[u])) for u in tiles]
                dv_t = [_dot_tn(rows2(prb[u]), rows2(doh[u])) for u in tiles]
                for u in tiles:
                    start, prev = pos[u][0], pos[u][1]
                    _set_rows(dqn_s, start, d, _rows(dqn_s, start, d) + dq_t[u])
                    _set_rows(dkn_s, prev, d, _rows(dkn_s, prev, d) + dk_t[u][:BLOCK])
                    _set_rows(dkn_s, start, d, _rows(dkn_s, start, d) + dk_t[u][BLOCK:])
                    _set_rows(dvv_s, prev, d, _rows(dvv_s, prev, d) + dv_t[u][:BLOCK])
                    _set_rows(dvv_s, start, d, _rows(dvv_s, start, d) + dv_t[u][BLOCK:])
                return kc[-1], vc[-1]

            none_yet = jnp.zeros((BLOCK, LANES), BF16)
            lax.fori_loop(0, nt // BWD_TILES, step, (none_yet, none_yet))

        if gqa:
            @pl.when(p == 0)
            def _():
                dk_ref[...] = jnp.zeros_like(dk_ref)
                dv_ref[...] = jnp.zeros_like(dv_ref)

        def finish(c, carry):
            dgq, dgk = carry
            rows = pl.ds(pl.multiple_of(c * NORM_CH, NORM_CH), NORM_CH)
            lo = _lo_mask((NORM_CH, LANES))

            def norm_bwd(xv, dn, g_ref):
                r = _head_stats(xv, lo)
                gd = dn * g_ref[...]
                dot = _head_sum(gd * xv, lo) * (1.0 / HEAD_DIM)
                return r * gd - xv * (r * r * r * dot), dn * (xv * r)

            dq, gq_part = norm_bwd(q_ref[rows, :], dqn_s[rows, :], gq_ref)
            dq_ref[rows, :] = dq
            dgq = dgq + jnp.sum(gq_part, axis=0, keepdims=True)
            kv, dkn, dvv = k_ref[rows, :], dkn_s[rows, :], dvv_s[rows, :]
            if gqa:
                kv = _replicate_head(kv, lo, use_lo)
                dkn = dkn + _swap_halves(dkn)
                dvv = dvv + _swap_halves(dvv)
                lane = lax.broadcasted_iota(I32, (NORM_CH, LANES), 1)
                mine = (lane // HEAD_DIM) == (p // 2)
                dk, gk_part = norm_bwd(kv, dkn, gk_ref)
                dk_ref[rows, :] += jnp.where(mine, dk, 0.0)
                dv_ref[rows, :] += jnp.where(mine, dvv, 0.0)
                gk_part = jnp.where(lo, gk_part, 0.0)
            else:
                dk, gk_part = norm_bwd(kv, dkn, gk_ref)
                dk_ref[rows, :] = dk
                dv_ref[rows, :] = dvv
            dgk = dgk + jnp.sum(gk_part, axis=0, keepdims=True)
            return dgq, dgk

        z = jnp.zeros((1, LANES), F32)
        dgq, dgk = lax.fori_loop(0, nch, finish, (z, z))
        dgq_ref[...] = jnp.broadcast_to(dgq, (8, LANES))
        dgk_ref[...] = jnp.broadcast_to(dgk, (8, LANES))
        if gqa:
            _store_slabs((dq_ref,), stage, dproj_ref, sems, (qb0 + p,))

            @pl.when(p == n_slabs - 1)
            def _():
                _store_slabs((dk_ref, dv_ref), stage, dproj_ref, sems, (kb0, vb0))
        else:
            _store_slabs((dq_ref, dk_ref, dv_ref), stage, dproj_ref, sems, (qb0 + p, kb0 + p, vb0 + p))

    def slab_of(width_blocks, b0, shared):
        return pl.BlockSpec((s, LANES), (lambda p: (0, b0)) if shared else (lambda p: (0, b0 + p)),
                            pipeline_mode=pl.Buffered(1))

    vec = pl.BlockSpec((1, LANES), lambda p: (0, 0))
    own = pl.BlockSpec((s, LANES), lambda p: (0, p), pipeline_mode=pl.Buffered(1))
    in_specs = [slab_of(0, qb0, False), slab_of(0, kb0, gqa), slab_of(0, vb0, gqa), vec, vec,
                pl.BlockSpec((nbr, 2, 2, BLOCK, 2 * BLOCK), lambda p: (0, 0, p, 0, 0)), own, own, own]
    args = [proj, proj, proj, gq.reshape(1, LANES), gk.reshape(1, LANES), bias, dout, out, lse]
    if has_sink:
        in_specs.append(pl.BlockSpec((None, 1, LANES), lambda p: (p, 0, 0)))
        args.append(sinks)
    held = pl.BlockSpec(memory_space=pl.ANY)
    in_specs.append(held)
    args.append(dproj)
    part = pl.BlockSpec((None, 8, LANES), lambda p: (p, 0, 0))
    out_specs = [held, pl.BlockSpec((nbr, 2, BLOCK, 2 * BLOCK), lambda p: (0, p, 0, 0)), part, part]
    out_shape = [jax.ShapeDtypeStruct(dproj.shape, dproj.dtype),
                 jax.ShapeDtypeStruct((nbr, 2 * n_slabs, BLOCK, 2 * BLOCK), F32),
                 jax.ShapeDtypeStruct((n_slabs, 8, LANES), F32), jax.ShapeDtypeStruct((n_slabs, 8, LANES), F32)]
    if has_sink:
        out_specs.append(part)
        out_shape.append(jax.ShapeDtypeStruct((n_slabs, 8, LANES), F32))
    res = pl.pallas_call(
        body, name=name, grid=(n_slabs,),
        in_specs=in_specs, out_specs=out_specs, out_shape=out_shape,
        input_output_aliases={len(args) - 1: 0},
        scratch_shapes=[pltpu.VMEM((s, LANES), F32) for _ in range(10)]
        + [pltpu.VMEM((3, s, LANES), BF16), pltpu.SemaphoreType.DMA((3,))],
        compiler_params=_params(("arbitrary",), VMEM_LIMIT),
    )(*args)
    outs = [res[0], res[1], res[2][:, 0, :], res[3][:, 0, :]]
    if has_sink:
        outs.append(res[4][:, 0, :])
    return outs


def bias_bwd(dbias, buckets, *, name):
    nbr, h = dbias.shape[:2]

    def body(db_ref, bk_ref, o_ref):
        lane = lax.broadcasted_iota(I32, (1, LANES), 1)
        acc = jnp.zeros((1, LANES), F32)
        for b in range(N_BUCKETS):
            tot = jnp.zeros((1, 1), F32)
            for br in range(nbr):
                sel = jnp.where(bk_ref[br] == b, db_ref[br], 0.0)
                tot = tot + jnp.sum(jnp.sum(sel, axis=0, keepdims=True), axis=1, keepdims=True)
            acc = jnp.where(lane == b, tot, acc)
        o_ref[...] = jnp.broadcast_to(acc, (8, LANES))

    res = pl.pallas_call(
        body, name=name, grid=(h,),
        in_specs=[pl.BlockSpec((nbr, None, BLOCK, 2 * BLOCK), lambda i: (0, i, 0, 0)),
                  pl.BlockSpec((nbr, BLOCK, 2 * BLOCK), lambda i: (0, 0, 0))],
        out_specs=pl.BlockSpec((None, 8, LANES), lambda i: (i, 0, 0)),
        out_shape=jax.ShapeDtypeStruct((h, 8, LANES), F32),
        compiler_params=_params(("parallel",)),
    )(dbias, buckets)
    return res[:, 0, :N_BUCKETS].T


SB_KG = 512
SB_QT = 2


def _softplus(z):
    return jnp.maximum(z, 0.0) + jnp.log(1.0 + jnp.exp(-jnp.abs(z)))


def _twice(t):
    t = t.astype(BF16)
    return jnp.concatenate([t, t], axis=0)


def _split_dot(x, t2):
    hi = x.astype(BF16)
    lo = (x - hi.astype(F32)).astype(BF16)
    return _dot(jnp.concatenate([hi, lo], axis=1), t2)


def sb_fwd(proj, qb0, kb0, vb0, n_slabs, *, name):
    s = proj.shape[0]
    nq = s // BLOCK
    nch = s // NORM_CH
    scale = HEAD_DIM ** -0.5

    def body(q_ref, k_ref, v_ref, o_ref, tot_ref, qlo_s, qhi_s, k_s, v_s):
        def prep(c, carry):
            rows = pl.ds(pl.multiple_of(c * NORM_CH, NORM_CH), NORM_CH)
            lo = _lo_mask((NORM_CH, LANES))
            qv = q_ref[rows, :] * scale
            qlo_s[rows, :] = jnp.where(lo, qv, 0.0).astype(BF16)
            qhi_s[rows, :] = jnp.where(lo, 0.0, qv).astype(BF16)
            k_s[rows, :] = k_ref[rows, :].astype(BF16)
            v_s[rows, :] = v_ref[rows, :].astype(BF16)
            return carry

        lax.fori_loop(0, nch, prep, 0)

        row = lax.broadcasted_iota(I32, (BLOCK, BLOCK), 0)
        col = lax.broadcasted_iota(I32, (BLOCK, BLOCK), 1)
        lo = col < HEAD_DIM
        t_ge = _twice(row >= col)
        rowg = lax.broadcasted_iota(I32, (BLOCK, SB_KG), 0)
        colg = lax.broadcasted_iota(I32, (BLOCK, SB_KG), 1)

        nsub = SB_KG // BLOCK
        chains = range(2 * SB_QT)
        nc = len(chains)

        def qloop(qs, phase):
            q0 = pl.multiple_of(qs * (SB_QT * BLOCK), SB_QT * BLOCK)
            qh = [(qlo_s, qhi_s)[i % 2][pl.ds(q0 + (i // 2) * BLOCK, BLOCK), :] for i in chains]
            gd = (qs * SB_QT) // nsub

            def logits(gi):
                k0 = pl.multiple_of(gi * SB_KG, SB_KG)
                kg = k_s[pl.ds(k0, SB_KG), :]
                return [_dot_nt(qh[i], kg) for i in chains]

            def group(gi, st, masks, npiece=nsub):
                k0 = pl.multiple_of(gi * SB_KG, SB_KG)
                vg = v_s[pl.ds(k0, npiece * BLOCK), :]
                c, o, z = list(st[:nc]), st[nc:2 * nc], st[2 * nc:]
                z_next = logits(jnp.maximum(gi - 1, 0))
                piece = lambda x, j: x[:, j * BLOCK:(j + 1) * BLOCK]
                a = [[None] * npiece for _ in chains]
                for j in reversed(range(npiece)):
                    zj = [piece(z[i], j) for i in chains]
                    lrem = [-_softplus(zj[i]) for i in chains]
                    if masks is not None:
                        lrem = [jnp.where(piece(masks[i // 2], j), lrem[i], 0.0) for i in chains]
                    incl = [_split_dot(lrem[i], t_ge) for i in chains]
                    for i in chains:
                        aij = jnp.exp(zj[i] + (c[i] + incl[i]))
                        if masks is not None:
                            aij = jnp.where(piece(masks[i // 2], j), aij, 0.0)
                        a[i][j] = aij.astype(BF16)
                        c[i] = c[i] + incl[i][:, 0:1]
                o = [o[i] + _dot(jnp.concatenate(a[i], axis=1), vg) for i in chains]
                return (*c, *o, *z_next)

            zc = [jnp.zeros((BLOCK, 1), F32)] * nc
            zo = [jnp.zeros((BLOCK, LANES), F32)] * nc
            masks = [(gd * SB_KG + colg) < (q0 + t * BLOCK + rowg) for t in range(SB_QT)]
            st = group(gd, (*zc, *zo, *logits(gd)), masks, (phase + 1) * SB_QT)
            st = lax.fori_loop(0, gd, lambda t, st: group(gd - 1 - t, st, None), st)
            for t in range(SB_QT):
                rows = pl.ds(q0 + t * BLOCK, BLOCK)
                o_ref[rows, :] = jnp.where(lo, st[nc + 2 * t], st[nc + 2 * t + 1])
                tot_ref[rows, :] = jnp.where(lo, st[2 * t], st[2 * t + 1])

        steps_per_group = nsub // SB_QT

        def per_group(g, carry):
            for phase in range(steps_per_group):
                qloop(g * steps_per_group + phase, phase)
            return carry

        lax.fori_loop(0, nq // nsub, per_group, 0)

    slab = lambda b0: pl.BlockSpec((s, LANES), lambda p: (0, b0 + p), pipeline_mode=pl.Buffered(1))
    w = LANES * n_slabs
    return pl.pallas_call(
        body, name=name, grid=(n_slabs,),
        in_specs=[slab(qb0), slab(kb0), slab(vb0)],
        out_specs=[pl.BlockSpec((s, LANES), lambda p: (0, p)), pl.BlockSpec((s, LANES), lambda p: (0, p))],
        out_shape=[jax.ShapeDtypeStruct((s, w), F32), jax.ShapeDtypeStruct((s, w), F32)],
        scratch_shapes=[pltpu.VMEM((s, LANES), BF16) for _ in range(4)],
        compiler_params=_params(("parallel",), VMEM_LIMIT),
    )(proj, proj, proj)


def _store_slabs(slabs, stage, dproj_ref, sems, blocks):
    s = stage.shape[1]

    def cast(c, carry):
        rows = pl.ds(pl.multiple_of(c * NORM_CH, NORM_CH), NORM_CH)
        for i, slab in enumerate(slabs):
            stage[i, rows, :] = slab[rows, :].astype(BF16)
        return carry

    lax.fori_loop(0, s // NORM_CH, cast, 0)
    copies = [pltpu.make_async_copy(stage.at[i], dproj_ref.at[:, pl.ds(pl.multiple_of(b * LANES, LANES), LANES)],
                                    sems.at[i]) for i, b in enumerate(blocks)]
    for cp in copies:
        cp.start()
    for cp in copies:
        cp.wait()


def sb_bwd(proj, qb0, kb0, vb0, n_slabs, dout, tot, dproj, *, name):
    s = proj.shape[0]
    nq = s // BLOCK
    nch = s // NORM_CH
    nsub = SB_KG // BLOCK
    scale = HEAD_DIM ** -0.5

    def body(q_ref, k_ref, v_ref, do_ref, tot_ref, dproj_in, dproj_ref,
             qlo_s, qhi_s, k_s, v_s, dlo_s, dhi_s, dq_ref, dk_ref, dv_ref, stage, sems):
        del dproj_in
        def prep(c, carry):
            rows = pl.ds(pl.multiple_of(c * NORM_CH, NORM_CH), NORM_CH)
            lo = _lo_mask((NORM_CH, LANES))
            qv = q_ref[rows, :] * scale
            dv = do_ref[rows, :]
            qlo_s[rows, :] = jnp.where(lo, qv, 0.0).astype(BF16)
            qhi_s[rows, :] = jnp.where(lo, 0.0, qv).astype(BF16)
            dlo_s[rows, :] = jnp.where(lo, dv, 0.0).astype(BF16)
            dhi_s[rows, :] = jnp.where(lo, 0.0, dv).astype(BF16)
            k_s[rows, :] = k_ref[rows, :].astype(BF16)
            v_s[rows, :] = v_ref[rows, :].astype(BF16)
            z = jnp.zeros((NORM_CH, LANES), F32)
            dk_ref[rows, :] = z
            dv_ref[rows, :] = z
            return carry

        lax.fori_loop(0, nch, prep, 0)

        row = lax.broadcasted_iota(I32, (BLOCK, BLOCK), 0)
        col = lax.broadcasted_iota(I32, (BLOCK, BLOCK), 1)
        lo = col < HEAD_DIM
        t_le = _twice(row <= col)
        rowg = lax.broadcasted_iota(I32, (BLOCK, SB_KG), 0)
        colg = lax.broadcasted_iota(I32, (BLOCK, SB_KG), 1)

        piece = lambda x, j: x[:, j * BLOCK:(j + 1) * BLOCK]
        chains = range(2 * SB_QT)
        nc = len(chains)

        def prefixes(x):
            return [[_split_dot(piece(x[i], j), t_le) for j in range(x[i].shape[1] // BLOCK)] for i in chains]

        def chain(pre, run, total=None):
            out = []
            for pj in pre:
                out.append(run + pj if total is None else total - run - pj)
                run = run + pj[:, BLOCK - 1:BLOCK]
            return jnp.concatenate(out, axis=1), run

        def qloop(qs, phase):
            q0 = pl.multiple_of(qs * (SB_QT * BLOCK), SB_QT * BLOCK)
            tile = lambda ref, i: ref[pl.ds(q0 + (i // 2) * BLOCK, BLOCK), :]
            qh = [tile((qlo_s, qhi_s)[i % 2], i) for i in chains]
            doh = [tile((dlo_s, dhi_s)[i % 2], i) for i in chains]
            tots = [tile(tot_ref, i)[:, (i % 2) * HEAD_DIM:(i % 2) * HEAD_DIM + 1] for i in chains]
            gd = (qs * SB_QT) // nsub

            def logits(gi):
                kg = k_s[pl.ds(pl.multiple_of(gi * SB_KG, SB_KG), SB_KG), :]
                return [_dot_nt(qh[i], kg) for i in chains]

            def group(gi, st, masks, npiece=nsub):
                k0 = pl.multiple_of(gi * SB_KG, SB_KG)
                wide = npiece * BLOCK
                kg, vg = k_s[pl.ds(k0, wide), :], v_s[pl.ds(k0, wide), :]
                cp, cg, dq = list(st[:nc]), list(st[nc:2 * nc]), st[2 * nc:2 * nc + SB_QT]
                z = [zi[:, :wide] for zi in st[2 * nc + SB_QT:]]
                masked = lambda x, i: x if masks is None else jnp.where(masks[i // 2][:, :wide], x, 0.0)
                z_next = logits(jnp.minimum(gi + 1, gd))
                da = [_dot_nt(doh[i], vg) for i in chains]
                sp = [_softplus(z[i]) for i in chains]
                lrem = [masked(-sp[i], i) for i in chains]
                pre = prefixes(lrem)
                e, a, g = [], [], []
                for i in chains:
                    suffix, cp[i] = chain(pre[i], cp[i], tots[i])
                    e.append(z[i] - sp[i])
                    a.append(masked(jnp.exp(e[i] + suffix), i))
                    g.append(a[i] * da[i])
                gpre = prefixes(g)
                dz = []
                for i in chains:
                    ginc, cg[i] = chain(gpre[i], cg[i])
                    dz.append(masked(g[i] - jnp.exp(e[i]) * ginc, i).astype(BF16))
                ab = [a[i].astype(BF16) for i in chains]
                dq = [dq[t] + jnp.where(lo, _dot(dz[2 * t], kg), _dot(dz[2 * t + 1], kg)) for t in range(SB_QT)]
                rows_of = lambda x: jnp.concatenate(x, axis=0)
                dk_ref[pl.ds(k0, wide), :] += _dot_tn(rows_of(dz), rows_of(qh))
                dv_ref[pl.ds(k0, wide), :] += _dot_tn(rows_of(ab), rows_of(doh))
                return (*cp, *cg, *dq, *z_next)

            zc = [jnp.zeros((BLOCK, 1), F32)] * (2 * nc)
            zq = [jnp.zeros((BLOCK, LANES), F32)] * SB_QT
            st = lax.fori_loop(0, gd, lambda gi, st: group(gi, st, None), (*zc, *zq, *logits(0)))
            st = group(gd, st, [(gd * SB_KG + colg) < (q0 + t * BLOCK + rowg) for t in range(SB_QT)],
                       (phase + 1) * SB_QT)
            for t in range(SB_QT):
                dq_ref[pl.ds(q0 + t * BLOCK, BLOCK), :] = st[2 * nc + t] * scale

        steps_per_group = nsub // SB_QT

        def per_group(g, carry):
            for phase in range(steps_per_group):
                qloop(g * steps_per_group + phase, phase)
            return carry

        lax.fori_loop(0, nq // nsub, per_group, 0)
        p = pl.program_id(0)
        _store_slabs((dq_ref, dk_ref, dv_ref), stage, dproj_ref, sems, (qb0 + p, kb0 + p, vb0 + p))

    slab = lambda b0: pl.BlockSpec((s, LANES), lambda p: (0, b0 + p), pipeline_mode=pl.Buffered(1))
    own = pl.BlockSpec((s, LANES), lambda p: (0, p), pipeline_mode=pl.Buffered(1))
    held = pl.BlockSpec(memory_space=pl.ANY)
    return pl.pallas_call(
        body, name=name, grid=(n_slabs,),
        in_specs=[slab(qb0), slab(kb0), slab(vb0), own, own, held],
        out_specs=held, out_shape=jax.ShapeDtypeStruct(dproj.shape, dproj.dtype),
        input_output_aliases={5: 0},
        scratch_shapes=[pltpu.VMEM((s, LANES), BF16) for _ in range(6)]
        + [pltpu.VMEM((s, LANES), F32) for _ in range(3)]
        + [pltpu.VMEM((3, s, LANES), BF16), pltpu.SemaphoreType.DMA((3,))],
        compiler_params=_params(("arbitrary",), VMEM_LIMIT),
    )(proj, proj, proj, dout, tot, dproj)


def _place():
    x, y, c = lax.axis_index("x"), lax.axis_index("y"), lax.axis_index("c")
    return x, y, c


def gather_small(v, *, name):
    m_per, n = v.shape

    def body(x_ref, out_ref, send_sems, recv_sems, local_sem):
        x, y, c = _place()
        me, sibling = (x, y, c), (x, y, 1 - c)
        chips = [(1 - x, y), (x, 1 - y), (1 - x, 1 - y)]

        def rows(px, py, pc):
            return out_ref.at[pl.ds((4 * px + 2 * py + pc) * m_per, m_per), :]

        def copy(k, block, to, src=None):
            return pltpu.make_async_remote_copy(
                src_ref=rows(*block) if src is None else src, dst_ref=rows(*block),
                send_sem=send_sems.at[k], recv_sem=recv_sems.at[k], device_id=to, device_id_type=MESH)

        mine = pltpu.make_async_copy(x_ref, rows(*me), local_sem)
        mine.start()
        first = [copy(0, me, sibling, src=x_ref)]
        first += [copy(1 + j, me, (*chip, c), src=x_ref) for j, chip in enumerate(chips)]
        for cp in first:
            cp.start()
        passed = [copy(4 + j, (*chip, c), sibling) for j, chip in enumerate(chips)]
        for j, chip in enumerate(chips):
            copy(1 + j, (*chip, c), me).wait_recv()
            passed[j].start()
        copy(0, sibling, me).wait_recv()
        for j, chip in enumerate(chips):
            copy(4 + j, (*chip, 1 - c), me).wait_recv()
        for cp in first + passed:
            cp.wait_send()
        mine.wait()

    return pl.pallas_call(
        body, name=name,
        out_shape=jax.ShapeDtypeStruct((N_DEV * m_per, n), v.dtype),
        in_specs=[pl.BlockSpec(memory_space=pltpu.VMEM)],
        out_specs=pl.BlockSpec(memory_space=pltpu.VMEM),
        scratch_shapes=[pltpu.SemaphoreType.DMA((7,)), pltpu.SemaphoreType.DMA((7,)), pltpu.SemaphoreType.DMA],
        compiler_params=_params(None, VMEM_LIMIT),
    )(v)


_HBM = pl.BlockSpec(memory_space=pltpu.HBM)
_SEM = pl.BlockSpec(memory_space=pltpu.SEMAPHORE)
_EFFECT = pltpu.SideEffectType.DATAFLOW_SIDE_EFFECTING


def _peer_copies(src_refs, land_refs, send_sems, recv_sems, per_dest):
    x, y, c = _place()
    me = 4 * x + 2 * y + c
    copies = []
    for src, land, ssem, rsem in zip(src_refs, land_refs, send_sems, recv_sems):
        for k in (1, 2, 4, 3, 5, 6, 7):
            px, py, pc = x ^ (k >> 2 & 1), y ^ (k >> 1 & 1), c ^ (k & 1)
            copies.append(pltpu.make_async_remote_copy(
                src_ref=src.at[4 * px + 2 * py + pc] if per_dest else src, dst_ref=land.at[me],
                send_sem=ssem.at[k - 1], recv_sem=rsem.at[k - 1], device_id=(px, py, pc), device_id_type=MESH))
    return copies


def _own_copies(src_refs, land_refs, send_sems, per_dest):
    x, y, c = _place()
    me = 4 * x + 2 * y + c
    return [pltpu.make_async_copy(src.at[me] if per_dest else src, land.at[me], ssem.at[7])
            for src, land, ssem in zip(src_refs, land_refs, send_sems)]


def exchange_start(srcs, per_dest, *, name):
    n = len(srcs)
    lands = [lax.empty(a.shape if per_dest else (N_DEV,) + a.shape, a.dtype) for a in srcs]

    def body(*refs):
        src_refs, land_refs = refs[:n], refs[n:2 * n]
        send_sems, recv_sems = refs[2 * n:3 * n], refs[3 * n:4 * n]
        token = refs[-1]
        for cp in _peer_copies(src_refs, land_refs, send_sems, recv_sems, per_dest):
            cp.start()
        for cp in _own_copies(src_refs, land_refs, send_sems, per_dest):
            cp.start()
        token[...] = jnp.zeros_like(token)

    hbm = lambda a: pltpu.HBM(a.shape, a.dtype)
    res = pl.pallas_call(
        body, name=name,
        out_shape=(*[pltpu.SemaphoreType.DMA((8,))] * n, *[pltpu.SemaphoreType.DMA((7,))] * n,
                   *[hbm(a) for a in srcs], *[hbm(a) for a in lands], jax.ShapeDtypeStruct((8, LANES), F32)),
        in_specs=[_HBM] * (2 * n),
        out_specs=(*[_SEM] * (2 * n), *[_HBM] * (2 * n), pl.BlockSpec(memory_space=pltpu.VMEM)),
        input_output_aliases={i: 2 * n + i for i in range(2 * n)},
        compiler_params=pltpu.CompilerParams(has_side_effects=_EFFECT),
    )(*[pltpu.with_memory_space_constraint(a, pltpu.HBM) for a in (*srcs, *lands)])
    handles = [(res[a], res[n + a], res[2 * n + a], res[3 * n + a]) for a in range(n)]
    return handles, res[-1]


def exchange_wait(handles, per_dest, after, *, name):
    n = len(handles)

    def body(*refs):
        src_refs, land_refs = refs[:n], refs[n:2 * n]
        send_sems, recv_sems = refs[2 * n:3 * n], refs[3 * n:4 * n]
        for cp in _peer_copies(src_refs, land_refs, send_sems, recv_sems, per_dest):
            cp.wait_send()
            cp.wait_recv()
        for cp in _own_copies(src_refs, land_refs, send_sems, per_dest):
            cp.wait()

    srcs, lands = [h[2] for h in handles], [h[3] for h in handles]
    hbm = lambda a: pltpu.HBM(a.shape, a.dtype)
    res = pl.pallas_call(
        body, name=name,
        out_shape=(*[hbm(a) for a in srcs], *[hbm(a) for a in lands]),
        in_specs=[*[_HBM] * (2 * n), *[_SEM] * (2 * n), pl.BlockSpec(memory_space=pl.ANY)],
        out_specs=tuple([_HBM] * (2 * n)),
        input_output_aliases={i: i for i in range(2 * n)},
        compiler_params=pltpu.CompilerParams(has_side_effects=_EFFECT),
    )(*srcs, *lands, *[h[0] for h in handles], *[h[1] for h in handles], after)
    return res[n:]


def _adamw_math(w, g, m, v):
    m = ADAM_B1 * m + (1.0 - ADAM_B1) * g
    v = ADAM_B2 * v + (1.0 - ADAM_B2) * (g * g)
    m_hat = m / (1.0 - ADAM_B1 ** ADAM_STEP)
    v_hat = v / (1.0 - ADAM_B2 ** ADAM_STEP)
    delta = -ADAM_LR * (m_hat / (jnp.sqrt(v_hat) + ADAM_EPS) + ADAM_WD * w)
    return delta, m, v


def adamw_parts(parts, w, m, v, layer, outs, *, name):
    depth, r, cdim = w.shape
    n_parts = parts.shape[0]
    tr = _pick(r, [t for t in (512, 256, 128, 64, 32, 16) if t * cdim <= 256 * 1024])

    def body(p_ref, w_ref, m_ref, v_ref, g0, d0, nm0, nv0, g_ref, d_ref, nm_ref, nv_ref):
        g = p_ref[0].astype(F32)
        for q in range(1, n_parts):
            g = g + p_ref[q].astype(F32)
        delta, nm, nv = _adamw_math(w_ref[...], g, m_ref[...], v_ref[...])
        g_ref[...], d_ref[...], nm_ref[...], nv_ref[...] = g, delta, nm, nv

    t = pl.BlockSpec((None, tr, cdim), lambda i: (layer, i, 0))
    held = pl.BlockSpec(memory_space=pl.ANY)
    return pl.pallas_call(
        body, name=name, grid=(r // tr,),
        in_specs=[pl.BlockSpec((n_parts, tr, cdim), lambda i: (0, i, 0)), t, t, t, held, held, held, held],
        out_specs=[t, t, t, t],
        out_shape=[jax.ShapeDtypeStruct((depth, r, cdim), F32)] * 4,
        input_output_aliases={4: 0, 5: 1, 6: 2, 7: 3},
        compiler_params=_params(("parallel",), VMEM_LIMIT),
    )(parts, w, m, v, *outs)


def sum_devices(gathered, *, name):
    m_rows = gathered.shape[1]

    def body(ga_ref, g_ref):
        g = ga_ref[0]
        for dev in range(1, N_DEV):
            g = g + ga_ref[dev]
        g_ref[...] = g

    return pl.pallas_call(
        body, name=name, out_shape=jax.ShapeDtypeStruct((m_rows, LANES), F32),
        compiler_params=_params(None, VMEM_LIMIT),
    )(gathered)


def adamw_small(g, w, m, v, *, name):
    m_rows = w.shape[0]

    def body(g_ref, w_ref, m_ref, v_ref, d_ref, nm_ref, nv_ref):
        d_ref[...], nm_ref[...], nv_ref[...] = _adamw_math(w_ref[...], g_ref[...], m_ref[...], v_ref[...])

    return pl.pallas_call(
        body, name=name, out_shape=[jax.ShapeDtypeStruct((m_rows, LANES), F32)] * 3,
        compiler_params=_params(None, VMEM_LIMIT),
    )(g, w, m, v)


def _t5_bucket(dist):
    max_exact = N_BUCKETS // 2
    d = jnp.maximum(dist, 0)
    large = max_exact + (jnp.log(jnp.maximum(d, 1).astype(F32) / max_exact)
                         / math.log(T5_MAX_DIST / max_exact) * (N_BUCKETS - max_exact)).astype(I32)
    large = jnp.minimum(large, N_BUCKETS - 1)
    return jnp.where(d < max_exact, d, large)


def _rel():
    return jnp.arange(BLOCK)[:, None] + BLOCK - jnp.arange(2 * BLOCK)[None, :]


def _band_bias(table, dils, max_dists):
    rel = _rel()
    biases, buckets = [], []
    for d, md in zip(dils, max_dists):
        bk = _t5_bucket(rel * d)
        vis = (rel >= 0) & (rel <= md)
        looked_up = jnp.zeros((table.shape[1],) + rel.shape, F32)
        for b in range(N_BUCKETS):
            looked_up = jnp.where((bk == b)[None], table[b][:, None, None], looked_up)
        with_prev = jnp.where(vis[None], looked_up, NEG_INF)
        first = jnp.arange(2 * BLOCK)[None, None, :] >= BLOCK
        biases.append(jnp.stack([with_prev, jnp.where(first, with_prev, NEG_INF)]))
        buckets.append(bk.astype(I32))
    return jnp.stack(biases), jnp.stack(buckets)


def _pack(pieces, rows):
    flat = jnp.concatenate([p.reshape(-1) for p in pieces])
    return jnp.pad(flat, (0, rows * LANES - flat.shape[0])).reshape(rows, LANES)


def _unpack(packed, shapes):
    flat = packed.reshape(-1)
    out, off = [], 0
    for sh in shapes:
        n = math.prod(sh)
        out.append(flat[off:off + n].reshape(sh))
        off += n
    return out


def _tile2(g):
    return jnp.concatenate([g, g])


def kernel(x, attn_norm, w_in, a_q_gain, a_k_gain, a_sinks, c_q_gain, c_k_gain, rel_bias_table, mix_out_gain, w_out, ffn_norm, w_up, conv_w, conv_b, w_down, loss_target, m_attn_norm, m_w_in, m_a_q_gain, m_a_k_gain, m_a_sinks, m_c_q_gain, m_c_k_gain, m_rel_bias_table, m_mix_out_gain, m_w_out, m_ffn_norm, m_w_up, m_conv_w, m_conv_b, m_w_down, v_attn_norm, v_w_in, v_a_q_gain, v_a_k_gain, v_a_sinks, v_c_q_gain, v_c_k_gain, v_rel_bias_table, v_mix_out_gain, v_w_out, v_ffn_norm, v_w_up, v_conv_w, v_conv_b, v_w_down):
    depth, d_model, in_shard = w_in.shape
    ff2_shard = w_up.shape[2]
    s = x.shape[1]
    in_width, ff2 = N_DEV * in_shard, N_DEV * ff2_shard
    n_heads = d_model // HEAD_DIM
    ha, hb, hc = n_heads // 4, n_heads // 4, n_heads // 2
    sa, sb, sc = ha // 2, hb // 2, hc // 2
    kv_a = ha // 4
    assert kv_a == 2 and BLOCK == LANES
    cb_aq, cb_ak, cb_av = 0, sa, sa + 1
    cb_bq = sa + 2
    cb_bk, cb_bv = cb_bq + sb, cb_bq + 2 * sb
    cb_cq = cb_bq + 3 * sb
    cb_ck, cb_cv = cb_cq + sc, cb_cq + 2 * sc
    assert (cb_cv + sc) * LANES == in_width
    dev = 4 * lax.axis_index("x") + 2 * lax.axis_index("y") + lax.axis_index("c")

    per_array = 3
    wnames = ("w_in", "w_out", "w_up", "w_down", "conv_w")
    cols_to_rows = lambda g: jnp.transpose(g, (1, 0, 2)).reshape(g.shape[1], N_DEV * g.shape[2])
    whole = dict(w_in=cols_to_rows, w_up=cols_to_rows, conv_w=cols_to_rows,
                 w_out=lambda g: g.reshape(d_model, d_model), w_down=lambda g: g.reshape(ff2 // 2, d_model))
    gathers = {}
    token = jnp.zeros((8, LANES), F32)
    for l in range(depth):
        for gi, group in enumerate([[n] for n in wnames] if l < per_array else [wnames]):
            srcs = [(dict(w_in=w_in, w_out=w_out, w_up=w_up, w_down=w_down, conv_w=conv_w)[n][l] + token[0, 0])
                    .astype(F32 if n == "conv_w" else BF16) for n in group]
            handles, token = exchange_start(srcs, False, name=f"gather_start_{l}_{gi}")
            gathers.update({(l, n): h for n, h in zip(group, handles)})

    def gathered(l, names, after):
        landed = exchange_wait([gathers[l, n] for n in names], False, after,
                               name=f"gather_wait_{l}_{wnames.index(names[0])}")
        return {n: whole[n](g) for n, g in zip(names, landed)}

    bias_a, buckets_a = _band_bias(rel_bias_table[:, :ha], (1,), (WINDOW_A - 1,))
    bias_c, buckets_c = _band_bias(rel_bias_table[:, ha:], DILATIONS, (BLOCK,) * len(DILATIONS))

    xs = x[0]
    saved = []
    wi, wo, wu, wd, cw = ([None] * depth for _ in range(5))
    for l in range(depth):
        if l < per_array:
            need = lambda n, after, l=l: gathered(l, (n,), after)[n]
        else:
            layer_w = gathered(l, wnames, xs)
            need = lambda n, after: layer_w[n]
        wi[l] = need("w_in", token if l == 0 else xs)
        h1 = rmsnorm_fwd(xs, attn_norm[l], name="attn_norm_fwd")
        proj = matmul(h1, wi[l], name="in_proj")
        sinks = jnp.repeat(a_sinks[l], HEAD_DIM).reshape(sa, 1, LANES)
        gaq, gak = _tile2(a_q_gain[l]), _tile2(a_k_gain[l])
        gcq, gck = _tile2(c_q_gain[l]), _tile2(c_k_gain[l])
        out_a, lse_a = banded_fwd(proj, cb_aq, cb_ak, cb_av, sa, gaq, gak, bias_a, (1,), sinks, True, name="swa_fwd")
        out_b, tot_b = sb_fwd(proj, cb_bq, cb_bk, cb_bv, sb, name="stick_fwd")
        out_c, lse_c = banded_fwd(proj, cb_cq, cb_ck, cb_cv, sc, gcq, gck, bias_c, DILATIONS, None, False,
                                  name="dilated_fwd")
        mix = mixnorm_fwd([out_a, out_b, out_c], mix_out_gain[l], name="mix_norm_fwd")
        wo[l] = need("w_out", mix)
        x_mid = matmul(mix, wo[l], res=xs, name="out_proj")
        h2 = rmsnorm_fwd(x_mid, ffn_norm[l], name="ffn_norm_fwd")
        wu[l] = need("w_up", h2)
        p = matmul(h2, wu[l], name="up_proj")
        cw[l] = need("conv_w", p)
        act = ffn_act_fwd(p, cw[l], conv_b[l], name="ffn_act_fwd")
        wd[l] = need("w_down", act)
        x_out = matmul(act, wd[l], res=x_mid, name="down_proj")
        saved.append(dict(x_in=xs, h1=h1, proj=proj, out_a=out_a, lse_a=lse_a, out_b=out_b, tot_b=tot_b,
                          out_c=out_c, lse_c=lse_c, mix=mix, x_mid=x_mid, h2=h2, p=p, act=act,
                          sinks=sinks, gains=(gaq, gak, gcq, gck)))
        xs = x_out

    dx, dx_b, loss_part = loss_head(xs, loss_target[0], name="loss_head")

    small = {k: [None] * depth for k in ("attn_norm", "a_q_gain", "a_k_gain", "a_sinks", "c_q_gain", "c_k_gain",
                                         "mix_out_gain", "ffn_norm", "conv_w", "conv_b")}
    big = {k: [None] * depth for k in ("w_in", "w_out", "w_up", "w_down")}
    dbias_a = dbias_c = None
    scatters = {}
    token = jnp.zeros((8, LANES), F32)
    names_big = ("w_in", "w_out", "w_up", "w_down")

    def scatter(l, names):
        parts = [big[n][l] for n in names]
        handles, tok = exchange_start(parts, True, name=f"scatter_start_{l}_{names_big.index(names[0])}")
        scatters.update({(l, n): h for n, h in zip(names, handles)})
        return tok

    by_cols = lambda a: jnp.transpose(a.reshape(a.shape[0], N_DEV, a.shape[1] // N_DEV), (1, 0, 2))
    by_rows = lambda a: a.reshape(N_DEV, a.shape[0] // N_DEV, a.shape[1])
    for l in reversed(range(depth)):
        each = l == 0
        sv = saved[l]
        gaq, gak, gcq, gck = sv["gains"]
        da = matmul(dx_b, wd[l], trans_b=True, name="down_proj_dx")
        big["w_down"][l] = by_rows(matmul(sv["act"], dx_b, trans_a=True, out_dtype=BF16, name="down_proj_dw"))
        if each:
            token = scatter(l, ("w_down",))
        dp, small["conv_w"][l], small["conv_b"][l] = ffn_act_bwd(da, sv["p"], cw[l], conv_b[l] + token[0, 0],
                                                                 name="ffn_act_bwd")
        dh2 = matmul(dp, wu[l], trans_b=True, name="up_proj_dx")
        big["w_up"][l] = matmul(sv["h2"], dp, trans_a=True, out_dtype=BF16, col_blocks=N_DEV, name="up_proj_dw")
        if each:
            token = scatter(l, ("w_up",))
        dx_mid, dx_mid_b, small["ffn_norm"][l] = rmsnorm_bwd(dh2, sv["x_mid"], ffn_norm[l] + token[0, 0], dx,
                                                   name="ffn_norm_bwd")
        dmix = matmul(dx_mid_b, wo[l], trans_b=True, name="out_proj_dx")
        big["w_out"][l] = by_rows(matmul(sv["mix"], dx_mid_b, trans_a=True, out_dtype=BF16, name="out_proj_dw"))
        if each:
            token = scatter(l, ("w_out",))
        (d_oa, d_ob, d_oc), small["mix_out_gain"][l] = mixnorm_bwd(
            dmix, [sv["out_a"], sv["out_b"], sv["out_c"]], mix_out_gain[l] + token[0, 0], name="mix_norm_bwd")
        dproj = lax.empty((s, in_width), BF16)
        dproj, db_a, dgq_a, dgk_a, dsink = banded_bwd(
            sv["proj"], cb_aq, cb_ak, cb_av, sa, gaq, gak, bias_a, (1,), sv["sinks"], True,
            d_oa, sv["out_a"], sv["lse_a"], dproj, name="swa_bwd")
        dproj = sb_bwd(sv["proj"], cb_bq, cb_bk, cb_bv, sb, d_ob, sv["tot_b"], dproj, name="stick_bwd")
        dproj, db_c, dgq_c, dgk_c = banded_bwd(
            sv["proj"], cb_cq, cb_ck, cb_cv, sc, gcq, gck, bias_c, DILATIONS, None, False,
            d_oc, sv["out_c"], sv["lse_c"], dproj, name="dilated_bwd")
        fold = lambda g: g.reshape(-1, HEAD_DIM).sum(axis=0)
        small["a_q_gain"][l], small["a_k_gain"][l] = fold(dgq_a), fold(dgk_a)
        small["c_q_gain"][l], small["c_k_gain"][l] = fold(dgq_c), fold(dgk_c)
        small["a_sinks"][l] = dsink[:, ::HEAD_DIM].reshape(-1)
        dbias_a = db_a if dbias_a is None else dbias_a + db_a
        dbias_c = db_c if dbias_c is None else dbias_c + db_c
        big["w_in"][l] = by_cols(matmul(sv["h1"], dproj, trans_a=True, out_dtype=BF16, name="in_proj_dw"))
        if not each:
            token = scatter(l, names_big)
        dh1 = matmul(dproj, wi[l], trans_b=True, name="in_proj_dx")
        dx, dx_b, small["attn_norm"][l] = rmsnorm_bwd(dh1, sv["x_in"], attn_norm[l] + token[0, 0], dx_mid,
                                                name="attn_norm_bwd")

    dtable = jnp.concatenate([bias_bwd(dbias_a, buckets_a, name="swa_bias_bwd"),
                              bias_bwd(dbias_c, buckets_c, name="dilated_bias_bwd")], axis=1)

    order = ("attn_norm", "a_q_gain", "a_k_gain", "a_sinks", "c_q_gain", "c_k_gain", "rel_bias_table",
             "mix_out_gain", "ffn_norm", "conv_w", "conv_b")
    partial = {k: jnp.stack(v) for k, v in small.items()}
    partial["rel_bias_table"] = dtable
    pieces = [partial[k] for k in order] + [loss_part.reshape(1)]
    n_small = sum(math.prod(pc.shape) for pc in pieces)
    rows = -(-n_small // (8 * LANES)) * 8
    gathered = gather_small(_pack(pieces, rows), name="gather_small_grads")
    big["w_in"][0], gathered = lax.optimization_barrier((big["w_in"][0], gathered))
    scatter(0, ("w_in",))
    summed = _unpack(sum_devices(gathered.reshape(N_DEV, rows, LANES), name="sum_small_grads"),
                     [pc.shape for pc in pieces])
    g_small = dict(zip(order, summed[:-1]))
    loss = summed[-1][0]
    g_small["conv_w"] = lax.dynamic_slice_in_dim(g_small["conv_w"], dev * ff2_shard, ff2_shard, axis=2)

    w_small = dict(attn_norm=attn_norm, a_q_gain=a_q_gain, a_k_gain=a_k_gain, a_sinks=a_sinks, c_q_gain=c_q_gain,
                   c_k_gain=c_k_gain, rel_bias_table=rel_bias_table, mix_out_gain=mix_out_gain, ffn_norm=ffn_norm,
                   conv_w=conv_w, conv_b=conv_b)
    m_small = dict(attn_norm=m_attn_norm, a_q_gain=m_a_q_gain, a_k_gain=m_a_k_gain, a_sinks=m_a_sinks,
                   c_q_gain=m_c_q_gain, c_k_gain=m_c_k_gain, rel_bias_table=m_rel_bias_table,
                   mix_out_gain=m_mix_out_gain, ffn_norm=m_ffn_norm, conv_w=m_conv_w, conv_b=m_conv_b)
    v_small = dict(attn_norm=v_attn_norm, a_q_gain=v_a_q_gain, a_k_gain=v_a_k_gain, a_sinks=v_a_sinks,
                   c_q_gain=v_c_q_gain, c_k_gain=v_c_k_gain, rel_bias_table=v_rel_bias_table,
                   mix_out_gain=v_mix_out_gain, ffn_norm=v_ffn_norm, conv_w=v_conv_w, conv_b=v_conv_b)
    shapes = [w_small[k].shape for k in order]
    n_upd = sum(math.prod(sh) for sh in shapes)
    urows = -(-n_upd // (8 * LANES)) * 8
    packs = [_pack([d[k] for k in order], urows) for d in (g_small, w_small, m_small, v_small)]
    upd = adamw_small(*packs, name="adamw_small")
    delta_s, newm_s, newv_s = [dict(zip(order, _unpack(u, shapes))) for u in upd]

    w_big = dict(w_in=(w_in, m_w_in, v_w_in), w_out=(w_out, m_w_out, v_w_out), w_up=(w_up, m_w_up, v_w_up),
                 w_down=(w_down, m_w_down, v_w_down))
    results = {k: [lax.empty(w_big[k][0].shape, F32) for _ in range(4)] for k in names_big}
    after = upd[0]
    for l, names in [(l, names_big) for l in reversed(range(1, depth))] + [(0, names_big[1:]), (0, names_big[:1])]:
        landed = exchange_wait([scatters[l, n] for n in names], True, after,
                               name=f"scatter_wait_{l}_{names_big.index(names[0])}")
        for k, parts in zip(names, landed):
            results[k] = adamw_parts(parts, *w_big[k], l, results[k], name="adamw_large")
        after = results[names[-1]][0]
    g_big, delta_b, newm_b, newv_b = [{k: results[k][i] for k in names_big} for i in range(4)]

    all_names = ("attn_norm", "w_in", "a_q_gain", "a_k_gain", "a_sinks", "c_q_gain", "c_k_gain", "rel_bias_table",
                 "mix_out_gain", "w_out", "ffn_norm", "w_up", "conv_w", "conv_b", "w_down")
    pick = lambda sm, bg: [bg[k] if k in bg else sm[k] for k in all_names]
    return (loss, dx[None], *pick(g_small, g_big), *pick(delta_s, delta_b), *pick(newm_s, newm_b),
            *pick(newv_s, newv_b))
```

```python
import functools
import math

import jax
import jax.numpy as jnp
from jax import lax
from jax.experimental import pallas as pl
from jax.experimental.pallas import tpu as pltpu

F32, BF16, I32 = jnp.float32, jnp.bfloat16, jnp.int32
MESH = pl.DeviceIdType.MESH

HEAD_DIM = 64
LANES = 128
BLOCK = 128
EPS = 1e-6
NEG_INF = -1e30
N_BUCKETS = 32
T5_MAX_DIST = 2048
WINDOW_A = 128
DILATIONS = (1, 4, 16)
N_DEV = 8
VMEM_LIMIT = 56 * 1024 * 1024
MATMUL_VMEM = 46 * 1024 * 1024

ADAM_LR, ADAM_B1, ADAM_B2, ADAM_EPS, ADAM_WD, ADAM_STEP = 0.001, 0.9, 0.999, 1e-08, 0.01, 10


def _params(sem=None, vmem=None):
    return pltpu.CompilerParams(dimension_semantics=sem, vmem_limit_bytes=vmem)


def _pick(n, cands):
    for c in cands:
        if n % c == 0:
            return c
    raise ValueError(f"no tile for {n}")


def _dot(a, b):
    return lax.dot_general(a, b, (((1,), (0,)), ((), ())), preferred_element_type=F32)


def _dot_nt(a, b):
    return lax.dot_general(a, b, (((1,), (1,)), ((), ())), preferred_element_type=F32)


def _dot_tn(a, b):
    return lax.dot_general(a, b, (((0,), (0,)), ((), ())), preferred_element_type=F32)


def matmul(a, b, *, trans_a=False, trans_b=False, out_dtype=F32, res=None, col_blocks=None, name):
    a_halves, b_halves = a.ndim == 3, b.ndim == 3
    assert not (a_halves and trans_a) and not (b_halves and trans_b)
    m, k = (a.shape[1], 2 * a.shape[2]) if a_halves else (a.shape[1], a.shape[0]) if trans_a else a.shape
    n = 2 * b.shape[2] if b_halves else b.shape[0] if trans_b else b.shape[1]
    k_unit, n_unit = (k // 2 if a_halves else k), (n // 2 if b_halves else n)
    tm = _pick(m, (1408, 1024, 896, 512, 256))
    tn_cands = ((n // col_blocks,) if col_blocks
                else tuple(t for t in (1024, 1408, 768, 512, 256, 128) if n_unit % t == 0))

    def footprint(tk, tn):
        tiles = 2 * (tm * tk * a.dtype.itemsize + tk * tn * b.dtype.itemsize)
        return tiles + tm * tn * (4 + 2 * jnp.dtype(out_dtype).itemsize + (8 if res is not None else 0))

    tk, tn = next((tk, tn) for tk in (5376, 4096, 2816, 2048, 1792, 1024, 768, 512, 256) if k_unit % tk == 0
                  for tn in tn_cands if footprint(tk, tn) <= MATMUL_VMEM)
    nk = k // tk
    nk_half, nj_half = k_unit // tk, n_unit // tn
    dn = (((0 if trans_a else 1,), (1 if trans_b else 0,)), ((), ()))

    def body(*refs):
        if res is None:
            a_ref, b_ref, o_ref, acc = refs
        else:
            a_ref, b_ref, r_ref, o_ref, acc = refs
        kk = pl.program_id(2)

        @pl.when(kk == 0)
        def _():
            acc[...] = jnp.zeros_like(acc)

        acc[...] += lax.dot_general(a_ref[...].astype(BF16), b_ref[...].astype(BF16), dn,
                                    preferred_element_type=F32)

        @pl.when(kk == nk - 1)
        def _():
            r = acc[...]
            if res is not None:
                r = r_ref[...] + r
            o_ref[...] = r.astype(out_dtype)

    b_spec = (pl.BlockSpec((tn, tk), lambda i, j, kk: (j, kk)) if trans_b
              else pl.BlockSpec((None, tk, tn), lambda i, j, kk: (j // nj_half, kk, j % nj_half)) if b_halves
              else pl.BlockSpec((tk, tn), lambda i, j, kk: (kk, j)))
    a_spec = (pl.BlockSpec((tk, tm), lambda i, j, kk: (kk, i)) if trans_a
              else pl.BlockSpec((None, tm, tk), lambda i, j, kk: (kk // nk_half, i, kk % nk_half)) if a_halves
              else pl.BlockSpec((tm, tk), lambda i, j, kk: (i, kk)))
    in_specs = [a_spec, b_spec]
    args = [a, b]
    if res is not None:
        in_specs.append(pl.BlockSpec((tm, tn), lambda i, j, kk: (i, j)))
        args.append(res)
    if col_blocks:
        out_spec = pl.BlockSpec((None, tm, tn), lambda i, j, kk: (j, i, 0))
        out_shape = jax.ShapeDtypeStruct((col_blocks, m, tn), out_dtype)
    else:
        out_spec = pl.BlockSpec((tm, tn), lambda i, j, kk: (i, j))
        out_shape = jax.ShapeDtypeStruct((m, n), out_dtype)
    return pl.pallas_call(
        body, name=name, grid=(m // tm, n // tn, nk),
        in_specs=in_specs, out_specs=out_spec, out_shape=out_shape,
        scratch_shapes=[pltpu.VMEM((tm, tn), F32)],
        compiler_params=_params(("parallel", "parallel", "arbitrary"), VMEM_LIMIT),
    )(*args)


def rmsnorm_fwd(x, g, *, name):
    s, d = x.shape
    tm = 512

    def body(x_ref, g_ref, o_ref):
        xv = x_ref[...]
        r = lax.rsqrt(jnp.mean(xv * xv, axis=-1, keepdims=True) + EPS)
        o_ref[...] = (xv * r * g_ref[...]).astype(BF16)

    return pl.pallas_call(
        body, name=name, grid=(s // tm,),
        in_specs=[pl.BlockSpec((tm, d), lambda i: (i, 0)), pl.BlockSpec((1, d), lambda i: (0, 0))],
        out_specs=pl.BlockSpec((tm, d), lambda i: (i, 0)),
        out_shape=jax.ShapeDtypeStruct((s, d), BF16),
        compiler_params=_params(("parallel",)),
    )(x, g.reshape(1, d))


def rmsnorm_bwd(dh, x, g, dres, *, name):
    s, d = x.shape
    tm = 256

    def body(dh_ref, x_ref, g_ref, dres_ref, dx_ref, dxb_ref, dg_ref):
        @pl.when(pl.program_id(0) == 0)
        def _():
            dg_ref[...] = jnp.zeros_like(dg_ref)

        xv, dhv = x_ref[...], dh_ref[...]
        r = lax.rsqrt(jnp.mean(xv * xv, axis=-1, keepdims=True) + EPS)
        gd = dhv * g_ref[...]
        dot = jnp.mean(gd * xv, axis=-1, keepdims=True)
        dx = dres_ref[...] + (r * gd - xv * (r * r * r * dot))
        dx_ref[...] = dx
        dxb_ref[...] = dx.astype(BF16)
        dg_ref[...] += jnp.sum(dhv * (xv * r), axis=0, keepdims=True)

    row = pl.BlockSpec((tm, d), lambda i: (i, 0))
    dx, dxb, dg = pl.pallas_call(
        body, name=name, grid=(s // tm,),
        in_specs=[row, row, pl.BlockSpec((1, d), lambda i: (0, 0)), row],
        out_specs=[row, row, pl.BlockSpec((1, d), lambda i: (0, 0))],
        out_shape=[jax.ShapeDtypeStruct((s, d), F32), jax.ShapeDtypeStruct((s, d), BF16),
                   jax.ShapeDtypeStruct((1, d), F32)],
        compiler_params=_params(("arbitrary",)),
    )(dh, x, g.reshape(1, d), dres)
    return dx, dxb, dg[0]


def loss_head(y, target, *, name):
    s, d = y.shape
    tm = 512

    def body(y_ref, t_ref, dy_ref, dyb_ref, l_ref):
        @pl.when(pl.program_id(0) == 0)
        def _():
            l_ref[...] = jnp.zeros_like(l_ref)

        e = y_ref[...] - t_ref[...]
        dy = e / float(d)
        dy_ref[...] = dy
        dyb_ref[...] = dy.astype(BF16)
        per_tok = jnp.mean(e * e, axis=-1, keepdims=True)
        l_ref[...] += 0.5 * jnp.sum(per_tok, axis=0, keepdims=True)

    row = pl.BlockSpec((tm, d), lambda i: (i, 0))
    dy, dyb, l = pl.pallas_call(
        body, name=name, grid=(s // tm,),
        in_specs=[row, row],
        out_specs=[row, row, pl.BlockSpec((8, LANES), lambda i: (0, 0))],
        out_shape=[jax.ShapeDtypeStruct((s, d), F32), jax.ShapeDtypeStruct((s, d), BF16),
                   jax.ShapeDtypeStruct((8, LANES), F32)],
        compiler_params=_params(("arbitrary",)),
    )(y, target)
    return dy, dyb, l[0, 0]


FFN_TN = 256
FFN_CH = 256


def _rows_before(ref, r0, first):
    if first:
        cur = ref[pl.ds(0, FFN_CH), :]
        row = lax.broadcasted_iota(I32, cur.shape, 0)
        sh1 = jnp.where(row < 1, 0.0, pltpu.roll(cur, 1, axis=0))
        sh2 = jnp.where(row < 2, 0.0, pltpu.roll(cur, 2, axis=0))
        return cur, sh1, sh2
    ext = ref[pl.ds(pl.multiple_of(r0 - 8, 8), FFN_CH + 8), :]
    return ext[8:], pltpu.roll(ext, 1, axis=0)[8:], pltpu.roll(ext, 2, axis=0)[8:]


def _rows_after(ref, r0, last):
    if last:
        cur = ref[pl.ds(r0, FFN_CH), :]
        row = lax.broadcasted_iota(I32, cur.shape, 0)
        up1 = jnp.where(row >= FFN_CH - 1, 0.0, pltpu.roll(cur, FFN_CH - 1, axis=0))
        up2 = jnp.where(row >= FFN_CH - 2, 0.0, pltpu.roll(cur, FFN_CH - 2, axis=0))
        return cur, up1, up2
    n = FFN_CH + 8
    ext = ref[pl.ds(r0, n), :]
    return ext[:FFN_CH], pltpu.roll(ext, n - 1, axis=0)[:FFN_CH], pltpu.roll(ext, n - 2, axis=0)[:FFN_CH]


def _sigmoid(x):
    return 0.5 * jnp.tanh(0.5 * x) + 0.5


def ffn_act_fwd(p, conv_w, conv_b, *, name):
    s, f2 = p.shape
    f = f2 // 2
    nj = f // FFN_TN
    nch = s // FFN_CH

    def body(pg_ref, pu_ref, wg_ref, wu_ref, bg_ref, bu_ref, a_ref):
        def conv(ref, w_ref, b_ref, r0, first):
            cur, sh1, sh2 = _rows_before(ref, r0, first)
            return ((b_ref[...] + w_ref[0:1, :] * sh2) + w_ref[1:2, :] * sh1) + w_ref[2:3, :] * cur

        def chunk(r0, first):
            gate = conv(pg_ref, wg_ref, bg_ref, r0, first)
            up = conv(pu_ref, wu_ref, bu_ref, r0, first)
            a_ref[pl.ds(r0, FFN_CH), :] = (gate * _sigmoid(gate) * up).astype(BF16)

        chunk(0, True)

        def step(c, carry):
            chunk(pl.multiple_of(c * FFN_CH, FFN_CH), False)
            return carry

        lax.fori_loop(1, nch, step, 0)

    col = lambda off: pl.BlockSpec((s, FFN_TN), lambda j: (0, j + off))
    wcol = lambda off: pl.BlockSpec((3, FFN_TN), lambda j: (0, j + off))
    bcol = lambda off: pl.BlockSpec((1, FFN_TN), lambda j: (0, j + off))
    return pl.pallas_call(
        body, name=name, grid=(nj,),
        in_specs=[col(0), col(nj), wcol(0), wcol(nj), bcol(0), bcol(nj)],
        out_specs=pl.BlockSpec((s, FFN_TN), lambda j: (0, j)),
        out_shape=jax.ShapeDtypeStruct((s, f), BF16),
        compiler_params=_params(("parallel",), VMEM_LIMIT),
    )(p, p, conv_w, conv_w, conv_b.reshape(1, f2), conv_b.reshape(1, f2))


def ffn_act_bwd(da, p, conv_w, conv_b, *, name):
    s, f2 = p.shape
    f = f2 // 2
    nj = f // FFN_TN
    nch = s // FFN_CH

    def body(da_ref, pg_ref, pu_ref, wg_ref, wu_ref, bg_ref, bu_ref,
             dp_ref, dwg_ref, dwu_ref, dbg_ref, dbu_ref, dug_s, duu_s):
        dpg_ref, dpu_ref = dp_ref.at[0], dp_ref.at[1]
        def conv(ref, w_ref, b_ref, r0, first):
            cur, sh1, sh2 = _rows_before(ref, r0, first)
            u = ((b_ref[...] + w_ref[0:1, :] * sh2) + w_ref[1:2, :] * sh1) + w_ref[2:3, :] * cur
            return u, (sh2, sh1, cur)

        def taps_sum(du, taps):
            return jnp.concatenate([jnp.sum(du * t, axis=0, keepdims=True) for t in taps], axis=0)

        def chunk(r0, first, acc):
            dwg, dwu, dbg, dbu = acc
            gate, tg = conv(pg_ref, wg_ref, bg_ref, r0, first)
            up, tu = conv(pu_ref, wu_ref, bu_ref, r0, first)
            dav = da_ref[pl.ds(r0, FFN_CH), :]
            sg = _sigmoid(gate)
            dgate = dav * up * (sg * (1.0 + gate * (1.0 - sg)))
            dup = dav * (gate * sg)
            dug_s[pl.ds(r0, FFN_CH), :] = dgate
            duu_s[pl.ds(r0, FFN_CH), :] = dup
            return (dwg + taps_sum(dgate, tg), dwu + taps_sum(dup, tu),
                    dbg + jnp.sum(dgate, axis=0, keepdims=True), dbu + jnp.sum(dup, axis=0, keepdims=True))

        z3 = jnp.zeros((3, FFN_TN), F32)
        z1 = jnp.zeros((1, FFN_TN), F32)
        acc = chunk(0, True, (z3, z3, z1, z1))
        acc = lax.fori_loop(1, nch, lambda c, a: chunk(pl.multiple_of(c * FFN_CH, FFN_CH), False, a), acc)
        dwg_ref[...], dwu_ref[...], dbg_ref[...], dbu_ref[...] = acc

        def back(src, w_ref, dst, r0, last):
            cur, up1, up2 = _rows_after(src, r0, last)
            dst[pl.ds(r0, FFN_CH), :] = (w_ref[2:3, :] * cur + w_ref[1:2, :] * up1 + w_ref[0:1, :] * up2).astype(BF16)

        def step(c, carry):
            r0 = pl.multiple_of(c * FFN_CH, FFN_CH)
            back(dug_s, wg_ref, dpg_ref, r0, False)
            back(duu_s, wu_ref, dpu_ref, r0, False)
            return carry

        lax.fori_loop(0, nch - 1, step, 0)
        back(dug_s, wg_ref, dpg_ref, (nch - 1) * FFN_CH, True)
        back(duu_s, wu_ref, dpu_ref, (nch - 1) * FFN_CH, True)

    col = lambda off: pl.BlockSpec((s, FFN_TN), lambda j: (0, j + off))
    wcol = lambda off: pl.BlockSpec((3, FFN_TN), lambda j: (0, j + off))
    bcol = lambda off: pl.BlockSpec((1, FFN_TN), lambda j: (0, j + off))
    outs = pl.pallas_call(
        body, name=name, grid=(nj,),
        in_specs=[col(0), col(0), col(nj), wcol(0), wcol(nj), bcol(0), bcol(nj)],
        out_specs=[pl.BlockSpec((2, s, FFN_TN), lambda j: (0, 0, j)), wcol(0), wcol(0), bcol(0), bcol(0)],
        out_shape=[jax.ShapeDtypeStruct((2, s, f), BF16),
                   jax.ShapeDtypeStruct((3, f), F32), jax.ShapeDtypeStruct((3, f), F32),
                   jax.ShapeDtypeStruct((1, f), F32), jax.ShapeDtypeStruct((1, f), F32)],
        scratch_shapes=[pltpu.VMEM((s, FFN_TN), F32), pltpu.VMEM((s, FFN_TN), F32)],
        compiler_params=_params(("parallel",), VMEM_LIMIT),
    )(da, p, p, conv_w, conv_w, conv_b.reshape(1, f2), conv_b.reshape(1, f2))
    dp, dwg, dwu, dbg, dbu = outs
    return dp, jnp.concatenate([dwg, dwu], axis=1), jnp.concatenate([dbg, dbu], axis=1)[0]


def mixnorm_fwd(outs, gain, *, name):
    s = outs[0].shape[0]
    widths = [o.shape[1] for o in outs]
    total = sum(widths)
    tm = 512

    def body(*refs):
        o_refs, g_ref, m_ref = refs[:-2], refs[-2], refs[-1]
        off = 0
        for o_ref, w in zip(o_refs, widths):
            xv = o_ref[...]
            r = lax.rsqrt(jnp.mean(xv * xv, axis=-1, keepdims=True) + EPS)
            m_ref[:, off:off + w] = (xv * r * g_ref[:, off:off + w]).astype(BF16)
            off += w

    return pl.pallas_call(
        body, name=name, grid=(s // tm,),
        in_specs=[pl.BlockSpec((tm, w), lambda i: (i, 0)) for w in widths] + [pl.BlockSpec((1, total), lambda i: (0, 0))],
        out_specs=pl.BlockSpec((tm, total), lambda i: (i, 0)),
        out_shape=jax.ShapeDtypeStruct((s, total), BF16),
        compiler_params=_params(("parallel",)),
    )(*outs, gain.reshape(1, total))


def mixnorm_bwd(dmix, outs, gain, *, name):
    s = outs[0].shape[0]
    widths = [o.shape[1] for o in outs]
    total = sum(widths)
    n = len(outs)
    tm = 256

    def body(*refs):
        dm_ref, o_refs, g_ref = refs[0], refs[1:1 + n], refs[1 + n]
        d_refs, dg_ref = refs[2 + n:2 + 2 * n], refs[2 + 2 * n]

        @pl.when(pl.program_id(0) == 0)
        def _():
            dg_ref[...] = jnp.zeros_like(dg_ref)

        off = 0
        for o_ref, d_ref, w in zip(o_refs, d_refs, widths):
            xv = o_ref[...]
            dhv = dm_ref[:, off:off + w]
            r = lax.rsqrt(jnp.mean(xv * xv, axis=-1, keepdims=True) + EPS)
            gd = dhv * g_ref[:, off:off + w]
            dot = jnp.mean(gd * xv, axis=-1, keepdims=True)
            d_ref[...] = r * gd - xv * (r * r * r * dot)
            dg_ref[:, off:off + w] += jnp.sum(dhv * (xv * r), axis=0, keepdims=True)
            off += w

    res = pl.pallas_call(
        body, name=name, grid=(s // tm,),
        in_specs=[pl.BlockSpec((tm, total), lambda i: (i, 0))]
        + [pl.BlockSpec((tm, w), lambda i: (i, 0)) for w in widths] + [pl.BlockSpec((1, total), lambda i: (0, 0))],
        out_specs=[pl.BlockSpec((tm, w), lambda i: (i, 0)) for w in widths] + [pl.BlockSpec((1, total), lambda i: (0, 0))],
        out_shape=[jax.ShapeDtypeStruct((s, w), F32) for w in widths] + [jax.ShapeDtypeStruct((1, total), F32)],
        compiler_params=_params(("arbitrary",)),
    )(dmix, *outs, gain.reshape(1, total))
    return res[:n], res[n][0]


NORM_CH = 512
FWD_TILES = 4
BWD_TILES = 4


def _lo_mask(shape):
    return lax.broadcasted_iota(I32, shape, 1) < HEAD_DIM


def _head_sum(x, lo):
    del lo
    i = lax.broadcasted_iota(I32, (LANES, LANES), 0) // HEAD_DIM
    j = lax.broadcasted_iota(I32, (LANES, LANES), 1) // HEAD_DIM
    return _split_dot(x, _twice(i == j))


def _head_stats(x, lo):
    return lax.rsqrt(_head_sum(x * x, lo) * (1.0 / HEAD_DIM) + EPS)


def _swap_halves(x):
    return pltpu.roll(x, HEAD_DIM, axis=1)


def _replicate_head(x, lo, use_lo_head):
    sw = _swap_halves(x)
    return jnp.where(use_lo_head, jnp.where(lo, x, sw), jnp.where(lo, sw, x))


def _tile_rows(i, s, d):
    nb = s // (BLOCK * d)
    r = i // nb
    b = i % nb
    start = r + (BLOCK * d) * b
    prev = start - (BLOCK * d) * jnp.minimum(b, 1)
    return start, prev, b > 0


def _rows(ref, start, d):
    if d == 1:
        return ref[pl.ds(pl.multiple_of(start, BLOCK), BLOCK), :]
    return ref[pl.ds(start, BLOCK, stride=d), :]


def _set_rows(ref, start, d, val):
    if d == 1:
        ref[pl.ds(pl.multiple_of(start, BLOCK), BLOCK), :] = val
    else:
        ref[pl.ds(start, BLOCK, stride=d), :] = val


def banded_fwd(proj, qb0, kb0, vb0, n_slabs, gq, gk, bias, dils, sinks, gqa, *, name):
    s = proj.shape[0]
    nbr = len(dils)
    nt = s // BLOCK
    nch = s // NORM_CH
    has_sink = sinks is not None

    def body(*refs):
        q_ref, k_ref, v_ref, gq_ref, gk_ref, b_ref = refs[:6]
        rest = refs[6:]
        if has_sink:
            sink_ref, rest = rest[0], rest[1:]
        out_ref, lse_ref, qn_s, kn_s, vv_s, o_s, l_s = rest
        p = pl.program_id(0)
        use_lo = (p // 2) == 0

        def prep(c, carry):
            rows = pl.ds(pl.multiple_of(c * NORM_CH, NORM_CH), NORM_CH)
            lo = _lo_mask((NORM_CH, LANES))
            qv, kv, vv = q_ref[rows, :], k_ref[rows, :], v_ref[rows, :]
            qn_s[rows, :] = qv * _head_stats(qv, lo) * gq_ref[...] * (HEAD_DIM ** -0.5)
            kn = kv * _head_stats(kv, lo) * gk_ref[...]
            if gqa:
                kn = _replicate_head(kn, lo, use_lo)
                vv = _replicate_head(vv, lo, use_lo)
            kn_s[rows, :] = kn
            vv_s[rows, :] = vv
            return carry

        lax.fori_loop(0, nch, prep, 0)

        lo = _lo_mask((BLOCK, LANES))
        hms = [lo, jnp.logical_not(lo)]
        heads, tiles = range(2), range(FWD_TILES)
        for br, d in enumerate(dils):
            def step(ii, carry, br=br, d=d):
                pos = [_tile_rows(ii * FWD_TILES + u, s, d) for u in tiles]
                kc = [carry[0]] + [_rows(kn_s, pos[u][0], d).astype(BF16) for u in tiles]
                vc = [carry[1]] + [_rows(vv_s, pos[u][0], d).astype(BF16) for u in tiles]
                kcat = [jnp.concatenate([kc[u], kc[u + 1]], axis=0) for u in tiles]
                vcat = [jnp.concatenate([vc[u], vc[u + 1]], axis=0) for u in tiles]
                qt = [_rows(qn_s, pos[u][0], d) for u in tiles]
                sc = [[_dot_nt(jnp.where(hms[h], qt[u], 0.0).astype(BF16), kcat[u])
                       + b_ref[br, jnp.where(pos[u][2], 0, 1), h] for h in heads] for u in tiles]
                m = [[jnp.max(sc[u][h], axis=1, keepdims=True) for h in heads] for u in tiles]
                pe = [[jnp.exp(sc[u][h] - m[u][h]) for h in heads] for u in tiles]
                den = [[jnp.sum(pe[u][h], axis=1, keepdims=True) for h in heads] for u in tiles]
                o = [[_dot(pe[u][h].astype(BF16), vcat[u]) * (1.0 / den[u][h]) for h in heads] for u in tiles]
                for u in tiles:
                    _set_rows(o_s.at[br], pos[u][0], d, jnp.where(lo, o[u][0], o[u][1]))
                    _set_rows(l_s.at[br], pos[u][0], d,
                              jnp.where(lo, m[u][0] + jnp.log(den[u][0]), m[u][1] + jnp.log(den[u][1])))
                return kc[-1], vc[-1]

            none_yet = jnp.zeros((BLOCK, LANES), BF16)
            lax.fori_loop(0, nt // FWD_TILES, step, (none_yet, none_yet))

        def combine(c, carry):
            rows = pl.ds(pl.multiple_of(c * NORM_CH, NORM_CH), NORM_CH)
            ls = [l_s[br, rows, :] for br in range(nbr)]
            mx = functools.reduce(jnp.maximum, ls)
            if has_sink:
                mx = jnp.maximum(mx, sink_ref[...])
            tot = functools.reduce(jnp.add, [jnp.exp(l - mx) for l in ls])
            if has_sink:
                tot = tot + jnp.exp(sink_ref[...] - mx)
            lse = mx + jnp.log(tot)
            acc = jnp.exp(ls[0] - lse) * o_s[0, rows, :]
            for br in range(1, nbr):
                acc = acc + jnp.exp(ls[br] - lse) * o_s[br, rows, :]
            out_ref[rows, :] = acc
            lse_ref[rows, :] = lse
            return carry

        lax.fori_loop(0, nch, combine, 0)

    slab = lambda b0, shared: pl.BlockSpec((s, LANES), (lambda p: (0, b0)) if shared else (lambda p: (0, b0 + p)),
                                           pipeline_mode=pl.Buffered(1))
    vec = pl.BlockSpec((1, LANES), lambda p: (0, 0))
    in_specs = [slab(qb0, False), slab(kb0, gqa), slab(vb0, gqa), vec, vec,
                pl.BlockSpec((nbr, 2, 2, BLOCK, 2 * BLOCK), lambda p: (0, 0, p, 0, 0))]
    args = [proj, proj, proj, gq.reshape(1, LANES), gk.reshape(1, LANES), bias]
    if has_sink:
        in_specs.append(pl.BlockSpec((None, 1, LANES), lambda p: (p, 0, 0)))
        args.append(sinks)
    w = LANES * n_slabs
    return pl.pallas_call(
        body, name=name, grid=(n_slabs,),
        in_specs=in_specs,
        out_specs=[pl.BlockSpec((s, LANES), lambda p: (0, p)), pl.BlockSpec((s, LANES), lambda p: (0, p))],
        out_shape=[jax.ShapeDtypeStruct((s, w), F32), jax.ShapeDtypeStruct((s, w), F32)],
        scratch_shapes=[pltpu.VMEM((s, LANES), F32), pltpu.VMEM((s, LANES), F32), pltpu.VMEM((s, LANES), F32),
                        pltpu.VMEM((nbr, s, LANES), F32), pltpu.VMEM((nbr, s, LANES), F32)],
        compiler_params=_params(("parallel",), VMEM_LIMIT),
    )(*args)


def banded_bwd(proj, qb0, kb0, vb0, n_slabs, gq, gk, bias, dils, sinks, gqa, dout, out, lse, dproj, *, name):
    s = proj.shape[0]
    nbr = len(dils)
    nt = s // BLOCK
    nch = s // NORM_CH
    has_sink = sinks is not None
    scale = HEAD_DIM ** -0.5

    def body(*refs):
        q_ref, k_ref, v_ref, gq_ref, gk_ref, b_ref, do_ref, o_ref, lse_ref = refs[:9]
        rest = refs[9:]
        if has_sink:
            sink_ref, rest = rest[0], rest[1:]
        dproj_ref, db_ref, dgq_ref, dgk_ref = rest[1:5]
        rest = rest[5:]
        if has_sink:
            dsink_ref, rest = rest[0], rest[1:]
        qn_s, kn_s, vv_s, dl_s, dqn_s, dkn_s, dvv_s, dq_ref, dk_ref, dv_ref, stage, sems = rest
        p = pl.program_id(0)
        use_lo = (p // 2) == 0

        def prep(c, carry):
            rows = pl.ds(pl.multiple_of(c * NORM_CH, NORM_CH), NORM_CH)
            lo = _lo_mask((NORM_CH, LANES))
            qv, kv, vv = q_ref[rows, :], k_ref[rows, :], v_ref[rows, :]
            qn_s[rows, :] = qv * _head_stats(qv, lo) * gq_ref[...] * scale
            kn = kv * _head_stats(kv, lo) * gk_ref[...]
            if gqa:
                kn = _replicate_head(kn, lo, use_lo)
                vv = _replicate_head(vv, lo, use_lo)
            kn_s[rows, :] = kn
            vv_s[rows, :] = vv
            delta = _head_sum(do_ref[rows, :] * o_ref[rows, :], lo)
            odd = lax.broadcasted_iota(I32, (NORM_CH, LANES), 1) % 2 == 1
            dl_s[rows, :] = jnp.where(odd, delta, lse_ref[rows, :])
            z = jnp.zeros((NORM_CH, LANES), F32)
            dqn_s[rows, :] = z
            dkn_s[rows, :] = z
            dvv_s[rows, :] = z
            if has_sink:
                ps = jnp.exp(sink_ref[...] - lse_ref[rows, :])
                return carry - jnp.sum(ps * delta, axis=0, keepdims=True)
            return carry

        dsink = lax.fori_loop(0, nch, prep, jnp.zeros((1, LANES), F32))
        if has_sink:
            dsink_ref[...] = jnp.broadcast_to(dsink, (8, LANES))

        lo = _lo_mask((BLOCK, LANES))
        hms = [lo, jnp.logical_not(lo)]
        heads, tiles = range(2), range(BWD_TILES)
        for br, d in enumerate(dils):
            db_ref[br] = jnp.zeros((2, BLOCK, 2 * BLOCK), F32)

            def step(ii, carry, br=br, d=d):
                pos = [_tile_rows(ii * BWD_TILES + u, s, d) for u in tiles]
                kc = [carry[0]] + [_rows(kn_s, pos[u][0], d).astype(BF16) for u in tiles]
                vc = [carry[1]] + [_rows(vv_s, pos[u][0], d).astype(BF16) for u in tiles]
                kcat = [jnp.concatenate([kc[u], kc[u + 1]], axis=0) for u in tiles]
                vcat = [jnp.concatenate([vc[u], vc[u + 1]], axis=0) for u in tiles]
                qt = [_rows(qn_s, pos[u][0], d) for u in tiles]
                dot_ = [_rows(do_ref, pos[u][0], d) for u in tiles]
                st_t = [_rows(dl_s, pos[u][0], d) for u in tiles]
                qh = [[jnp.where(hms[h], qt[u], 0.0).astype(BF16) for h in heads] for u in tiles]
                doh = [[jnp.where(hms[h], dot_[u], 0.0).astype(BF16) for h in heads] for u in tiles]
                sc = [[_dot_nt(qh[u][h], kcat[u]) + b_ref[br, jnp.where(pos[u][2], 0, 1), h] for h in heads]
                      for u in tiles]
                dp = [[_dot_nt(doh[u][h], vcat[u]) for h in heads] for u in tiles]
                lane0 = [0, HEAD_DIM]
                pr = [[jnp.exp(sc[u][h] - st_t[u][:, lane0[h]:lane0[h] + 1]) for h in heads] for u in tiles]
                dlog = [[pr[u][h] * (dp[u][h] - st_t[u][:, lane0[h] + 1:lane0[h] + 2]) for h in heads] for u in tiles]
                for h in heads:
                    db_ref[br, h] += functools.reduce(jnp.add, [dlog[u][h] for u in tiles])
                dlb = [[dlog[u][h].astype(BF16) for h in heads] for u in tiles]
                prb = [[pr[u][h].astype(BF16) for h in heads] for u in tiles]
                dq_t = [jnp.where(lo, _dot(dlb[u][0], kcat[u]), _dot(dlb[u][1], kcat[u])) * scale for u in tiles]
                rows2 = lambda x: jnp.concatenate(x, axis=0)
                dk_t = [_dot_tn(rows2(dlb[u]), rows2(qh[u])) for u in tiles]
                dv_t = [_dot_tn(rows2(prb[u]), rows2(doh[u])) for u in tiles]
                for u in tiles:
                    start, prev = pos[u][0], pos[u][1]
                    _set_rows(dqn_s, start, d, _rows(dqn_s, start, d) + dq_t[u])
                    _set_rows(dkn_s, prev, d, _rows(dkn_s, prev, d) + dk_t[u][:BLOCK])
                    _set_rows(dkn_s, start, d, _rows(dkn_s, start, d) + dk_t[u][BLOCK:])
                    _set_rows(dvv_s, prev, d, _rows(dvv_s, prev, d) + dv_t[u][:BLOCK])
                    _set_rows(dvv_s, start, d, _rows(dvv_s, start, d) + dv_t[u][BLOCK:])
                return kc[-1], vc[-1]

            none_yet = jnp.zeros((BLOCK, LANES), BF16)
            lax.fori_loop(0, nt // BWD_TILES, step, (none_yet, none_yet))

        if gqa:
            @pl.when(p == 0)
            def _():
                dk_ref[...] = jnp.zeros_like(dk_ref)
                dv_ref[...] = jnp.zeros_like(dv_ref)

        def finish(c, carry):
            dgq, dgk = carry
            rows = pl.ds(pl.multiple_of(c * NORM_CH, NORM_CH), NORM_CH)
            lo = _lo_mask((NORM_CH, LANES))

            def norm_bwd(xv, dn, g_ref):
                r = _head_stats(xv, lo)
                gd = dn * g_ref[...]
                dot = _head_sum(gd * xv, lo) * (1.0 / HEAD_DIM)
                return r * gd - xv * (r * r * r * dot), dn * (xv * r)

            dq, gq_part = norm_bwd(q_ref[rows, :], dqn_s[rows, :], gq_ref)
            dq_ref[rows, :] = dq
            dgq = dgq + jnp.sum(gq_part, axis=0, keepdims=True)
            kv, dkn, dvv = k_ref[rows, :], dkn_s[rows, :], dvv_s[rows, :]
            if gqa:
                kv = _replicate_head(kv, lo, use_lo)
                dkn = dkn + _swap_halves(dkn)
                dvv = dvv + _swap_halves(dvv)
                lane = lax.broadcasted_iota(I32, (NORM_CH, LANES), 1)
                mine = (lane // HEAD_DIM) == (p // 2)
                dk, gk_part = norm_bwd(kv, dkn, gk_ref)
                dk_ref[rows, :] += jnp.where(mine, dk, 0.0)
                dv_ref[rows, :] += jnp.where(mine, dvv, 0.0)
                gk_part = jnp.where(lo, gk_part, 0.0)
            else:
                dk, gk_part = norm_bwd(kv, dkn, gk_ref)
                dk_ref[rows, :] = dk
                dv_ref[rows, :] = dvv
            dgk = dgk + jnp.sum(gk_part, axis=0, keepdims=True)
            return dgq, dgk

        z = jnp.zeros((1, LANES), F32)
        dgq, dgk = lax.fori_loop(0, nch, finish, (z, z))
        dgq_ref[...] = jnp.broadcast_to(dgq, (8, LANES))
        dgk_ref[...] = jnp.broadcast_to(dgk, (8, LANES))
        if gqa:
            _store_slabs((dq_ref,), stage, dproj_ref, sems, (qb0 + p,))

            @pl.when(p == n_slabs - 1)
            def _():
                _store_slabs((dk_ref, dv_ref), stage, dproj_ref, sems, (kb0, vb0))
        else:
            _store_slabs((dq_ref, dk_ref, dv_ref), stage, dproj_ref, sems, (qb0 + p, kb0 + p, vb0 + p))

    def slab_of(width_blocks, b0, shared):
        return pl.BlockSpec((s, LANES), (lambda p: (0, b0)) if shared else (lambda p: (0, b0 + p)),
                            pipeline_mode=pl.Buffered(1))

    vec = pl.BlockSpec((1, LANES), lambda p: (0, 0))
    own = pl.BlockSpec((s, LANES), lambda p: (0, p), pipeline_mode=pl.Buffered(1))
    in_specs = [slab_of(0, qb0, False), slab_of(0, kb0, gqa), slab_of(0, vb0, gqa), vec, vec,
                pl.BlockSpec((nbr, 2, 2, BLOCK, 2 * BLOCK), lambda p: (0, 0, p, 0, 0)), own, own, own]
    args = [proj, proj, proj, gq.reshape(1, LANES), gk.reshape(1, LANES), bias, dout, out, lse]
    if has_sink:
        in_specs.append(pl.BlockSpec((None, 1, LANES), lambda p: (p, 0, 0)))
        args.append(sinks)
    held = pl.BlockSpec(memory_space=pl.ANY)
    in_specs.append(held)
    args.append(dproj)
    part = pl.BlockSpec((None, 8, LANES), lambda p: (p, 0, 0))
    out_specs = [held, pl.BlockSpec((nbr, 2, BLOCK, 2 * BLOCK), lambda p: (0, p, 0, 0)), part, part]
    out_shape = [jax.ShapeDtypeStruct(dproj.shape, dproj.dtype),
                 jax.ShapeDtypeStruct((nbr, 2 * n_slabs, BLOCK, 2 * BLOCK), F32),
                 jax.ShapeDtypeStruct((n_slabs, 8, LANES), F32), jax.ShapeDtypeStruct((n_slabs, 8, LANES), F32)]
    if has_sink:
        out_specs.append(part)
        out_shape.append(jax.ShapeDtypeStruct((n_slabs, 8, LANES), F32))
    res = pl.pallas_call(
        body, name=name, grid=(n_slabs,),
        in_specs=in_specs, out_specs=out_specs, out_shape=out_shape,
        input_output_aliases={len(args) - 1: 0},
        scratch_shapes=[pltpu.VMEM((s, LANES), F32) for _ in range(10)]
        + [pltpu.VMEM((3, s, LANES), BF16), pltpu.SemaphoreType.DMA((3,))],
        compiler_params=_params(("arbitrary",), VMEM_LIMIT),
    )(*args)
    outs = [res[0], res[1], res[2][:, 0, :], res[3][:, 0, :]]
    if has_sink:
        outs.append(res[4][:, 0, :])
    return outs


def bias_bwd(dbias, buckets, *, name):
    nbr, h = dbias.shape[:2]

    def body(db_ref, bk_ref, o_ref):
        lane = lax.broadcasted_iota(I32, (1, LANES), 1)
        acc = jnp.zeros((1, LANES), F32)
        for b in range(N_BUCKETS):
            tot = jnp.zeros((1, 1), F32)
            for br in range(nbr):
                sel = jnp.where(bk_ref[br] == b, db_ref[br], 0.0)
                tot = tot + jnp.sum(jnp.sum(sel, axis=0, keepdims=True), axis=1, keepdims=True)
            acc = jnp.where(lane == b, tot, acc)
        o_ref[...] = jnp.broadcast_to(acc, (8, LANES))

    res = pl.pallas_call(
        body, name=name, grid=(h,),
        in_specs=[pl.BlockSpec((nbr, None, BLOCK, 2 * BLOCK), lambda i: (0, i, 0, 0)),
                  pl.BlockSpec((nbr, BLOCK, 2 * BLOCK), lambda i: (0, 0, 0))],
        out_specs=pl.BlockSpec((None, 8, LANES), lambda i: (i, 0, 0)),
        out_shape=jax.ShapeDtypeStruct((h, 8, LANES), F32),
        compiler_params=_params(("parallel",)),
    )(dbias, buckets)
    return res[:, 0, :N_BUCKETS].T


SB_KG = 512
SB_QT = 2


def _softplus(z):
    return jnp.maximum(z, 0.0) + jnp.log(1.0 + jnp.exp(-jnp.abs(z)))


def _twice(t):
    t = t.astype(BF16)
    return jnp.concatenate([t, t], axis=0)


def _split_dot(x, t2):
    hi = x.astype(BF16)
    lo = (x - hi.astype(F32)).astype(BF16)
    return _dot(jnp.concatenate([hi, lo], axis=1), t2)


def sb_fwd(proj, qb0, kb0, vb0, n_slabs, *, name):
    s = proj.shape[0]
    nq = s // BLOCK
    nch = s // NORM_CH
    scale = HEAD_DIM ** -0.5

    def body(q_ref, k_ref, v_ref, o_ref, tot_ref, qlo_s, qhi_s, k_s, v_s):
        def prep(c, carry):
            rows = pl.ds(pl.multiple_of(c * NORM_CH, NORM_CH), NORM_CH)
            lo = _lo_mask((NORM_CH, LANES))
            qv = q_ref[rows, :] * scale
            qlo_s[rows, :] = jnp.where(lo, qv, 0.0).astype(BF16)
            qhi_s[rows, :] = jnp.where(lo, 0.0, qv).astype(BF16)
            k_s[rows, :] = k_ref[rows, :].astype(BF16)
            v_s[rows, :] = v_ref[rows, :].astype(BF16)
            return carry

        lax.fori_loop(0, nch, prep, 0)

        row = lax.broadcasted_iota(I32, (BLOCK, BLOCK), 0)
        col = lax.broadcasted_iota(I32, (BLOCK, BLOCK), 1)
        lo = col < HEAD_DIM
        t_ge = _twice(row >= col)
        rowg = lax.broadcasted_iota(I32, (BLOCK, SB_KG), 0)
        colg = lax.broadcasted_iota(I32, (BLOCK, SB_KG), 1)

        nsub = SB_KG // BLOCK
        chains = range(2 * SB_QT)
        nc = len(chains)

        def qloop(qs, phase):
            q0 = pl.multiple_of(qs * (SB_QT * BLOCK), SB_QT * BLOCK)
            qh = [(qlo_s, qhi_s)[i % 2][pl.ds(q0 + (i // 2) * BLOCK, BLOCK), :] for i in chains]
            gd = (qs * SB_QT) // nsub

            def logits(gi):
                k0 = pl.multiple_of(gi * SB_KG, SB_KG)
                kg = k_s[pl.ds(k0, SB_KG), :]
                return [_dot_nt(qh[i], kg) for i in chains]

            def group(gi, st, masks, npiece=nsub):
                k0 = pl.multiple_of(gi * SB_KG, SB_KG)
                vg = v_s[pl.ds(k0, npiece * BLOCK), :]
                c, o, z = list(st[:nc]), st[nc:2 * nc], st[2 * nc:]
                z_next = logits(jnp.maximum(gi - 1, 0))
                piece = lambda x, j: x[:, j * BLOCK:(j + 1) * BLOCK]
                a = [[None] * npiece for _ in chains]
                for j in reversed(range(npiece)):
                    zj = [piece(z[i], j) for i in chains]
                    lrem = [-_softplus(zj[i]) for i in chains]
                    if masks is not None:
                        lrem = [jnp.where(piece(masks[i // 2], j), lrem[i], 0.0) for i in chains]
                    incl = [_split_dot(lrem[i], t_ge) for i in chains]
                    for i in chains:
                        aij = jnp.exp(zj[i] + (c[i] + incl[i]))
                        if masks is not None:
                            aij = jnp.where(piece(masks[i // 2], j), aij, 0.0)
                        a[i][j] = aij.astype(BF16)
                        c[i] = c[i] + incl[i][:, 0:1]
                o = [o[i] + _dot(jnp.concatenate(a[i], axis=1), vg) for i in chains]
                return (*c, *o, *z_next)

            zc = [jnp.zeros((BLOCK, 1), F32)] * nc
            zo = [jnp.zeros((BLOCK, LANES), F32)] * nc
            masks = [(gd * SB_KG + colg) < (q0 + t * BLOCK + rowg) for t in range(SB_QT)]
            st = group(gd, (*zc, *zo, *logits(gd)), masks, (phase + 1) * SB_QT)
            st = lax.fori_loop(0, gd, lambda t, st: group(gd - 1 - t, st, None), st)
            for t in range(SB_QT):
                rows = pl.ds(q0 + t * BLOCK, BLOCK)
                o_ref[rows, :] = jnp.where(lo, st[nc + 2 * t], st[nc + 2 * t + 1])
                tot_ref[rows, :] = jnp.where(lo, st[2 * t], st[2 * t + 1])

        steps_per_group = nsub // SB_QT

        def per_group(g, carry):
            for phase in range(steps_per_group):
                qloop(g * steps_per_group + phase, phase)
            return carry

        lax.fori_loop(0, nq // nsub, per_group, 0)

    slab = lambda b0: pl.BlockSpec((s, LANES), lambda p: (0, b0 + p), pipeline_mode=pl.Buffered(1))
    w = LANES * n_slabs
    return pl.pallas_call(
        body, name=name, grid=(n_slabs,),
        in_specs=[slab(qb0), slab(kb0), slab(vb0)],
        out_specs=[pl.BlockSpec((s, LANES), lambda p: (0, p)), pl.BlockSpec((s, LANES), lambda p: (0, p))],
        out_shape=[jax.ShapeDtypeStruct((s, w), F32), jax.ShapeDtypeStruct((s, w), F32)],
        scratch_shapes=[pltpu.VMEM((s, LANES), BF16) for _ in range(4)],
        compiler_params=_params(("parallel",), VMEM_LIMIT),
    )(proj, proj, proj)


def _store_slabs(slabs, stage, dproj_ref, sems, blocks):
    s = stage.shape[1]

    def cast(c, carry):
        rows = pl.ds(pl.multiple_of(c * NORM_CH, NORM_CH), NORM_CH)
        for i, slab in enumerate(slabs):
            stage[i, rows, :] = slab[rows, :].astype(BF16)
        return carry

    lax.fori_loop(0, s // NORM_CH, cast, 0)
    copies = [pltpu.make_async_copy(stage.at[i], dproj_ref.at[:, pl.ds(pl.multiple_of(b * LANES, LANES), LANES)],
                                    sems.at[i]) for i, b in enumerate(blocks)]
    for cp in copies:
        cp.start()
    for cp in copies:
        cp.wait()


def sb_bwd(proj, qb0, kb0, vb0, n_slabs, dout, tot, dproj, *, name):
    s = proj.shape[0]
    nq = s // BLOCK
    nch = s // NORM_CH
    nsub = SB_KG // BLOCK
    scale = HEAD_DIM ** -0.5

    def body(q_ref, k_ref, v_ref, do_ref, tot_ref, dproj_in, dproj_ref,
             qlo_s, qhi_s, k_s, v_s, dlo_s, dhi_s, dq_ref, dk_ref, dv_ref, stage, sems):
        del dproj_in
        def prep(c, carry):
            rows = pl.ds(pl.multiple_of(c * NORM_CH, NORM_CH), NORM_CH)
            lo = _lo_mask((NORM_CH, LANES))
            qv = q_ref[rows, :] * scale
            dv = do_ref[rows, :]
            qlo_s[rows, :] = jnp.where(lo, qv, 0.0).astype(BF16)
            qhi_s[rows, :] = jnp.where(lo, 0.0, qv).astype(BF16)
            dlo_s[rows, :] = jnp.where(lo, dv, 0.0).astype(BF16)
            dhi_s[rows, :] = jnp.where(lo, 0.0, dv).astype(BF16)
            k_s[rows, :] = k_ref[rows, :].astype(BF16)
            v_s[rows, :] = v_ref[rows, :].astype(BF16)
            z = jnp.zeros((NORM_CH, LANES), F32)
            dk_ref[rows, :] = z
            dv_ref[rows, :] = z
            return carry

        lax.fori_loop(0, nch, prep, 0)

        row = lax.broadcasted_iota(I32, (BLOCK, BLOCK), 0)
        col = lax.broadcasted_iota(I32, (BLOCK, BLOCK), 1)
        lo = col < HEAD_DIM
        t_le = _twice(row <= col)
        rowg = lax.broadcasted_iota(I32, (BLOCK, SB_KG), 0)
        colg = lax.broadcasted_iota(I32, (BLOCK, SB_KG), 1)

        piece = lambda x, j: x[:, j * BLOCK:(j + 1) * BLOCK]
        chains = range(2 * SB_QT)
        nc = len(chains)

        def prefixes(x):
            return [[_split_dot(piece(x[i], j), t_le) for j in range(x[i].shape[1] // BLOCK)] for i in chains]

        def chain(pre, run, total=None):
            out = []
            for pj in pre:
                out.append(run + pj if total is None else total - run - pj)
                run = run + pj[:, BLOCK - 1:BLOCK]
            return jnp.concatenate(out, axis=1), run

        def qloop(qs, phase):
            q0 = pl.multiple_of(qs * (SB_QT * BLOCK), SB_QT * BLOCK)
            tile = lambda ref, i: ref[pl.ds(q0 + (i // 2) * BLOCK, BLOCK), :]
            qh = [tile((qlo_s, qhi_s)[i % 2], i) for i in chains]
            doh = [tile((dlo_s, dhi_s)[i % 2], i) for i in chains]
            tots = [tile(tot_ref, i)[:, (i % 2) * HEAD_DIM:(i % 2) * HEAD_DIM + 1] for i in chains]
            gd = (qs * SB_QT) // nsub

            def logits(gi):
                kg = k_s[pl.ds(pl.multiple_of(gi * SB_KG, SB_KG), SB_KG), :]
                return [_dot_nt(qh[i], kg) for i in chains]

            def group(gi, st, masks, npiece=nsub):
                k0 = pl.multiple_of(gi * SB_KG, SB_KG)
                wide = npiece * BLOCK
                kg, vg = k_s[pl.ds(k0, wide), :], v_s[pl.ds(k0, wide), :]
                cp, cg, dq = list(st[:nc]), list(st[nc:2 * nc]), st[2 * nc:2 * nc + SB_QT]
                z = [zi[:, :wide] for zi in st[2 * nc + SB_QT:]]
                masked = lambda x, i: x if masks is None else jnp.where(masks[i // 2][:, :wide], x, 0.0)
                z_next = logits(jnp.minimum(gi + 1, gd))
                da = [_dot_nt(doh[i], vg) for i in chains]
                sp = [_softplus(z[i]) for i in chains]
                lrem = [masked(-sp[i], i) for i in chains]
                pre = prefixes(lrem)
                e, a, g = [], [], []
                for i in chains:
                    suffix, cp[i] = chain(pre[i], cp[i], tots[i])
                    e.append(z[i] - sp[i])
                    a.append(masked(jnp.exp(e[i] + suffix), i))
                    g.append(a[i] * da[i])
                gpre = prefixes(g)
                dz = []
                for i in chains:
                    ginc, cg[i] = chain(gpre[i], cg[i])
                    dz.append(masked(g[i] - jnp.exp(e[i]) * ginc, i).astype(BF16))
                ab = [a[i].astype(BF16) for i in chains]
                dq = [dq[t] + jnp.where(lo, _dot(dz[2 * t], kg), _dot(dz[2 * t + 1], kg)) for t in range(SB_QT)]
                rows_of = lambda x: jnp.concatenate(x, axis=0)
                dk_ref[pl.ds(k0, wide), :] += _dot_tn(rows_of(dz), rows_of(qh))
                dv_ref[pl.ds(k0, wide), :] += _dot_tn(rows_of(ab), rows_of(doh))
                return (*cp, *cg, *dq, *z_next)

            zc = [jnp.zeros((BLOCK, 1), F32)] * (2 * nc)
            zq = [jnp.zeros((BLOCK, LANES), F32)] * SB_QT
            st = lax.fori_loop(0, gd, lambda gi, st: group(gi, st, None), (*zc, *zq, *logits(0)))
            st = group(gd, st, [(gd * SB_KG + colg) < (q0 + t * BLOCK + rowg) for t in range(SB_QT)],
                       (phase + 1) * SB_QT)
            for t in range(SB_QT):
                dq_ref[pl.ds(q0 + t * BLOCK, BLOCK), :] = st[2 * nc + t] * scale

        steps_per_group = nsub // SB_QT

        def per_group(g, carry):
            for phase in range(steps_per_group):
                qloop(g * steps_per_group + phase, phase)
            return carry

        lax.fori_loop(0, nq // nsub, per_group, 0)
        p = pl.program_id(0)
        _store_slabs((dq_ref, dk_ref, dv_ref), stage, dproj_ref, sems, (qb0 + p, kb0 + p, vb0 + p))

    slab = lambda b0: pl.BlockSpec((s, LANES), lambda p: (0, b0 + p), pipeline_mode=pl.Buffered(1))
    own = pl.BlockSpec((s, LANES), lambda p: (0, p), pipeline_mode=pl.Buffered(1))
    held = pl.BlockSpec(memory_space=pl.ANY)
    return pl.pallas_call(
        body, name=name, grid=(n_slabs,),
        in_specs=[slab(qb0), slab(kb0), slab(vb0), own, own, held],
        out_specs=held, out_shape=jax.ShapeDtypeStruct(dproj.shape, dproj.dtype),
        input_output_aliases={5: 0},
        scratch_shapes=[pltpu.VMEM((s, LANES), BF16) for _ in range(6)]
        + [pltpu.VMEM((s, LANES), F32) for _ in range(3)]
        + [pltpu.VMEM((3, s, LANES), BF16), pltpu.SemaphoreType.DMA((3,))],
        compiler_params=_params(("arbitrary",), VMEM_LIMIT),
    )(proj, proj, proj, dout, tot, dproj)


def _place():
    x, y, c = lax.axis_index("x"), lax.axis_index("y"), lax.axis_index("c")
    return x, y, c


def gather_small(v, *, name):
    m_per, n = v.shape

    def body(x_ref, out_ref, send_sems, recv_sems, local_sem):
        x, y, c = _place()
        me, sibling = (x, y, c), (x, y, 1 - c)
        chips = [(1 - x, y), (x, 1 - y), (1 - x, 1 - y)]

        def rows(px, py, pc):
            return out_ref.at[pl.ds((4 * px + 2 * py + pc) * m_per, m_per), :]

        def copy(k, block, to, src=None):
            return pltpu.make_async_remote_copy(
                src_ref=rows(*block) if src is None else src, dst_ref=rows(*block),
                send_sem=send_sems.at[k], recv_sem=recv_sems.at[k], device_id=to, device_id_type=MESH)

        mine = pltpu.make_async_copy(x_ref, rows(*me), local_sem)
        mine.start()
        first = [copy(0, me, sibling, src=x_ref)]
        first += [copy(1 + j, me, (*chip, c), src=x_ref) for j, chip in enumerate(chips)]
        for cp in first:
            cp.start()
        passed = [copy(4 + j, (*chip, c), sibling) for j, chip in enumerate(chips)]
        for j, chip in enumerate(chips):
            copy(1 + j, (*chip, c), me).wait_recv()
            passed[j].start()
        copy(0, sibling, me).wait_recv()
        for j, chip in enumerate(chips):
            copy(4 + j, (*chip, 1 - c), me).wait_recv()
        for cp in first + passed:
            cp.wait_send()
        mine.wait()

    return pl.pallas_call(
        body, name=name,
        out_shape=jax.ShapeDtypeStruct((N_DEV * m_per, n), v.dtype),
        in_specs=[pl.BlockSpec(memory_space=pltpu.VMEM)],
        out_specs=pl.BlockSpec(memory_space=pltpu.VMEM),
        scratch_shapes=[pltpu.SemaphoreType.DMA((7,)), pltpu.SemaphoreType.DMA((7,)), pltpu.SemaphoreType.DMA],
        compiler_params=_params(None, VMEM_LIMIT),
    )(v)


_HBM = pl.BlockSpec(memory_space=pltpu.HBM)
_SEM = pl.BlockSpec(memory_space=pltpu.SEMAPHORE)
_EFFECT = pltpu.SideEffectType.DATAFLOW_SIDE_EFFECTING


def _peer_copies(src_refs, land_refs, send_sems, recv_sems, per_dest):
    x, y, c = _place()
    me = 4 * x + 2 * y + c
    copies = []
    for src, land, ssem, rsem in zip(src_refs, land_refs, send_sems, recv_sems):
        for k in (1, 2, 4, 3, 5, 6, 7):
            px, py, pc = x ^ (k >> 2 & 1), y ^ (k >> 1 & 1), c ^ (k & 1)
            copies.append(pltpu.make_async_remote_copy(
                src_ref=src.at[4 * px + 2 * py + pc] if per_dest else src, dst_ref=land.at[me],
                send_sem=ssem.at[k - 1], recv_sem=rsem.at[k - 1], device_id=(px, py, pc), device_id_type=MESH))
    return copies


def _own_copies(src_refs, land_refs, send_sems, per_dest):
    x, y, c = _place()
    me = 4 * x + 2 * y + c
    return [pltpu.make_async_copy(src.at[me] if per_dest else src, land.at[me], ssem.at[7])
            for src, land, ssem in zip(src_refs, land_refs, send_sems)]


def exchange_start(srcs, per_dest, *, name):
    n = len(srcs)
    lands = [lax.empty(a.shape if per_dest else (N_DEV,) + a.shape, a.dtype) for a in srcs]

    def body(*refs):
        src_refs, land_refs = refs[:n], refs[n:2 * n]
        send_sems, recv_sems = refs[2 * n:3 * n], refs[3 * n:4 * n]
        token = refs[-1]
        for cp in _peer_copies(src_refs, land_refs, send_sems, recv_sems, per_dest):
            cp.start()
        for cp in _own_copies(src_refs, land_refs, send_sems, per_dest):
            cp.start()
        token[...] = jnp.zeros_like(token)

    hbm = lambda a: pltpu.HBM(a.shape, a.dtype)
    res = pl.pallas_call(
        body, name=name,
        out_shape=(*[pltpu.SemaphoreType.DMA((8,))] * n, *[pltpu.SemaphoreType.DMA((7,))] * n,
                   *[hbm(a) for a in srcs], *[hbm(a) for a in lands], jax.ShapeDtypeStruct((8, LANES), F32)),
        in_specs=[_HBM] * (2 * n),
        out_specs=(*[_SEM] * (2 * n), *[_HBM] * (2 * n), pl.BlockSpec(memory_space=pltpu.VMEM)),
        input_output_aliases={i: 2 * n + i for i in range(2 * n)},
        compiler_params=pltpu.CompilerParams(has_side_effects=_EFFECT),
    )(*[pltpu.with_memory_space_constraint(a, pltpu.HBM) for a in (*srcs, *lands)])
    handles = [(res[a], res[n + a], res[2 * n + a], res[3 * n + a]) for a in range(n)]
    return handles, res[-1]


def exchange_wait(handles, per_dest, after, *, name):
    n = len(handles)

    def body(*refs):
        src_refs, land_refs = refs[:n], refs[n:2 * n]
        send_sems, recv_sems = refs[2 * n:3 * n], refs[3 * n:4 * n]
        for cp in _peer_copies(src_refs, land_refs, send_sems, recv_sems, per_dest):
            cp.wait_send()
            cp.wait_recv()
        for cp in _own_copies(src_refs, land_refs, send_sems, per_dest):
            cp.wait()

    srcs, lands = [h[2] for h in handles], [h[3] for h in handles]
    hbm = lambda a: pltpu.HBM(a.shape, a.dtype)
    res = pl.pallas_call(
        body, name=name,
        out_shape=(*[hbm(a) for a in srcs], *[hbm(a) for a in lands]),
        in_specs=[*[_HBM] * (2 * n), *[_SEM] * (2 * n), pl.BlockSpec(memory_space=pl.ANY)],
        out_specs=tuple([_HBM] * (2 * n)),
        input_output_aliases={i: i for i in range(2 * n)},
        compiler_params=pltpu.CompilerParams(has_side_effects=_EFFECT),
    )(*srcs, *lands, *[h[0] for h in handles], *[h[1] for h in handles], after)
    return res[n:]


def _adamw_math(w, g, m, v):
    m = ADAM_B1 * m + (1.0 - ADAM_B1) * g
    v = ADAM_B2 * v + (1.0 - ADAM_B2) * (g * g)
    m_hat = m / (1.0 - ADAM_B1 ** ADAM_STEP)
    v_hat = v / (1.0 - ADAM_B2 ** ADAM_STEP)
    delta = -ADAM_LR * (m_hat / (jnp.sqrt(v_hat) + ADAM_EPS) + ADAM_WD * w)
    return delta, m, v


def adamw_parts(parts, w, m, v, layer, outs, *, name):
    depth, r, cdim = w.shape
    n_parts = parts.shape[0]
    tr = _pick(r, [t for t in (512, 256, 128, 112, 64, 32, 16) if t * cdim <= 256 * 1024])

    def body(p_ref, w_ref, m_ref, v_ref, g0, d0, nm0, nv0, g_ref, d_ref, nm_ref, nv_ref):
        g = p_ref[0].astype(F32)
        for q in range(1, n_parts):
            g = g + p_ref[q].astype(F32)
        delta, nm, nv = _adamw_math(w_ref[...], g, m_ref[...], v_ref[...])
        g_ref[...], d_ref[...], nm_ref[...], nv_ref[...] = g, delta, nm, nv

    t = pl.BlockSpec((None, tr, cdim), lambda i: (layer, i, 0))
    held = pl.BlockSpec(memory_space=pl.ANY)
    return pl.pallas_call(
        body, name=name, grid=(r // tr,),
        in_specs=[pl.BlockSpec((n_parts, tr, cdim), lambda i: (0, i, 0)), t, t, t, held, held, held, held],
        out_specs=[t, t, t, t],
        out_shape=[jax.ShapeDtypeStruct((depth, r, cdim), F32)] * 4,
        input_output_aliases={4: 0, 5: 1, 6: 2, 7: 3},
        compiler_params=_params(("parallel",), VMEM_LIMIT),
    )(parts, w, m, v, *outs)


def sum_devices(gathered, *, name):
    m_rows = gathered.shape[1]

    def body(ga_ref, g_ref):
        g = ga_ref[0]
        for dev in range(1, N_DEV):
            g = g + ga_ref[dev]
        g_ref[...] = g

    return pl.pallas_call(
        body, name=name, out_shape=jax.ShapeDtypeStruct((m_rows, LANES), F32),
        compiler_params=_params(None, VMEM_LIMIT),
    )(gathered)


def adamw_small(g, w, m, v, *, name):
    m_rows = w.shape[0]

    def body(g_ref, w_ref, m_ref, v_ref, d_ref, nm_ref, nv_ref):
        d_ref[...], nm_ref[...], nv_ref[...] = _adamw_math(w_ref[...], g_ref[...], m_ref[...], v_ref[...])

    return pl.pallas_call(
        body, name=name, out_shape=[jax.ShapeDtypeStruct((m_rows, LANES), F32)] * 3,
        compiler_params=_params(None, VMEM_LIMIT),
    )(g, w, m, v)


def _t5_bucket(dist):
    max_exact = N_BUCKETS // 2
    d = jnp.maximum(dist, 0)
    large = max_exact + (jnp.log(jnp.maximum(d, 1).astype(F32) / max_exact)
                         / math.log(T5_MAX_DIST / max_exact) * (N_BUCKETS - max_exact)).astype(I32)
    large = jnp.minimum(large, N_BUCKETS - 1)
    return jnp.where(d < max_exact, d, large)


def _rel():
    return jnp.arange(BLOCK)[:, None] + BLOCK - jnp.arange(2 * BLOCK)[None, :]


def _band_bias(table, dils, max_dists):
    rel = _rel()
    biases, buckets = [], []
    for d, md in zip(dils, max_dists):
        bk = _t5_bucket(rel * d)
        vis = (rel >= 0) & (rel <= md)
        looked_up = jnp.zeros((table.shape[1],) + rel.shape, F32)
        for b in range(N_BUCKETS):
            looked_up = jnp.where((bk == b)[None], table[b][:, None, None], looked_up)
        with_prev = jnp.where(vis[None], looked_up, NEG_INF)
        first = jnp.arange(2 * BLOCK)[None, None, :] >= BLOCK
        biases.append(jnp.stack([with_prev, jnp.where(first, with_prev, NEG_INF)]))
        buckets.append(bk.astype(I32))
    return jnp.stack(biases), jnp.stack(buckets)


def _pack(pieces, rows):
    flat = jnp.concatenate([p.reshape(-1) for p in pieces])
    return jnp.pad(flat, (0, rows * LANES - flat.shape[0])).reshape(rows, LANES)


def _unpack(packed, shapes):
    flat = packed.reshape(-1)
    out, off = [], 0
    for sh in shapes:
        n = math.prod(sh)
        out.append(flat[off:off + n].reshape(sh))
        off += n
    return out


def _tile2(g):
    return jnp.concatenate([g, g])


def kernel(x, attn_norm, w_in, a_q_gain, a_k_gain, a_sinks, c_q_gain, c_k_gain, rel_bias_table, mix_out_gain, w_out, ffn_norm, w_up, conv_w, conv_b, w_down, loss_target, m_attn_norm, m_w_in, m_a_q_gain, m_a_k_gain, m_a_sinks, m_c_q_gain, m_c_k_gain, m_rel_bias_table, m_mix_out_gain, m_w_out, m_ffn_norm, m_w_up, m_conv_w, m_conv_b, m_w_down, v_attn_norm, v_w_in, v_a_q_gain, v_a_k_gain, v_a_sinks, v_c_q_gain, v_c_k_gain, v_rel_bias_table, v_mix_out_gain, v_w_out, v_ffn_norm, v_w_up, v_conv_w, v_conv_b, v_w_down):
    depth, d_model, in_shard = w_in.shape
    ff2_shard = w_up.shape[2]
    s = x.shape[1]
    in_width, ff2 = N_DEV * in_shard, N_DEV * ff2_shard
    n_heads = d_model // HEAD_DIM
    ha, hb, hc = n_heads // 4, n_heads // 4, n_heads // 2
    sa, sb, sc = ha // 2, hb // 2, hc // 2
    kv_a = ha // 4
    assert kv_a == 2 and BLOCK == LANES
    cb_aq, cb_ak, cb_av = 0, sa, sa + 1
    cb_bq = sa + 2
    cb_bk, cb_bv = cb_bq + sb, cb_bq + 2 * sb
    cb_cq = cb_bq + 3 * sb
    cb_ck, cb_cv = cb_cq + sc, cb_cq + 2 * sc
    assert (cb_cv + sc) * LANES == in_width
    dev = 4 * lax.axis_index("x") + 2 * lax.axis_index("y") + lax.axis_index("c")

    per_array = 3
    wnames = ("w_in", "w_out", "w_up", "w_down", "conv_w")
    cols_to_rows = lambda g: jnp.transpose(g, (1, 0, 2)).reshape(g.shape[1], N_DEV * g.shape[2])
    whole = dict(w_in=cols_to_rows, w_up=cols_to_rows, conv_w=cols_to_rows,
                 w_out=lambda g: g.reshape(d_model, d_model), w_down=lambda g: g.reshape(ff2 // 2, d_model))
    gathers = {}
    token = jnp.zeros((8, LANES), F32)
    for l in range(depth):
        for gi, group in enumerate([[n] for n in wnames] if l < per_array else [wnames]):
            srcs = [(dict(w_in=w_in, w_out=w_out, w_up=w_up, w_down=w_down, conv_w=conv_w)[n][l] + token[0, 0])
                    .astype(F32 if n == "conv_w" else BF16) for n in group]
            handles, token = exchange_start(srcs, False, name=f"gather_start_{l}_{gi}")
            gathers.update({(l, n): h for n, h in zip(group, handles)})

    def gathered(l, names, after):
        landed = exchange_wait([gathers[l, n] for n in names], False, after,
                               name=f"gather_wait_{l}_{wnames.index(names[0])}")
        return {n: whole[n](g) for n, g in zip(names, landed)}

    bias_a, buckets_a = _band_bias(rel_bias_table[:, :ha], (1,), (WINDOW_A - 1,))
    bias_c, buckets_c = _band_bias(rel_bias_table[:, ha:], DILATIONS, (BLOCK,) * len(DILATIONS))

    xs = x[0]
    saved = []
    wi, wo, wu, wd, cw = ([None] * depth for _ in range(5))
    for l in range(depth):
        if l < per_array:
            need = lambda n, after, l=l: gathered(l, (n,), after)[n]
        else:
            layer_w = gathered(l, wnames, xs)
            need = lambda n, after: layer_w[n]
        wi[l] = need("w_in", token if l == 0 else xs)
        h1 = rmsnorm_fwd(xs, attn_norm[l], name="attn_norm_fwd")
        proj = matmul(h1, wi[l], name="in_proj")
        sinks = jnp.repeat(a_sinks[l], HEAD_DIM).reshape(sa, 1, LANES)
        gaq, gak = _tile2(a_q_gain[l]), _tile2(a_k_gain[l])
        gcq, gck = _tile2(c_q_gain[l]), _tile2(c_k_gain[l])
        out_a, lse_a = banded_fwd(proj, cb_aq, cb_ak, cb_av, sa, gaq, gak, bias_a, (1,), sinks, True, name="swa_fwd")
        out_b, tot_b = sb_fwd(proj, cb_bq, cb_bk, cb_bv, sb, name="stick_fwd")
        out_c, lse_c = banded_fwd(proj, cb_cq, cb_ck, cb_cv, sc, gcq, gck, bias_c, DILATIONS, None, False,
                                  name="dilated_fwd")
        mix = mixnorm_fwd([out_a, out_b, out_c], mix_out_gain[l], name="mix_norm_fwd")
        wo[l] = need("w_out", mix)
        x_mid = matmul(mix, wo[l], res=xs, name="out_proj")
        h2 = rmsnorm_fwd(x_mid, ffn_norm[l], name="ffn_norm_fwd")
        wu[l] = need("w_up", h2)
        p = matmul(h2, wu[l], name="up_proj")
        cw[l] = need("conv_w", p)
        act = ffn_act_fwd(p, cw[l], conv_b[l], name="ffn_act_fwd")
        wd[l] = need("w_down", act)
        x_out = matmul(act, wd[l], res=x_mid, name="down_proj")
        saved.append(dict(x_in=xs, h1=h1, proj=proj, out_a=out_a, lse_a=lse_a, out_b=out_b, tot_b=tot_b,
                          out_c=out_c, lse_c=lse_c, mix=mix, x_mid=x_mid, h2=h2, p=p, act=act,
                          sinks=sinks, gains=(gaq, gak, gcq, gck)))
        xs = x_out

    dx, dx_b, loss_part = loss_head(xs, loss_target[0], name="loss_head")

    small = {k: [None] * depth for k in ("attn_norm", "a_q_gain", "a_k_gain", "a_sinks", "c_q_gain", "c_k_gain",
                                         "mix_out_gain", "ffn_norm", "conv_w", "conv_b")}
    big = {k: [None] * depth for k in ("w_in", "w_out", "w_up", "w_down")}
    dbias_a = dbias_c = None
    scatters = {}
    token = jnp.zeros((8, LANES), F32)
    names_big = ("w_in", "w_out", "w_up", "w_down")

    def scatter(l, names):
        parts = [big[n][l] for n in names]
        handles, tok = exchange_start(parts, True, name=f"scatter_start_{l}_{names_big.index(names[0])}")
        scatters.update({(l, n): h for n, h in zip(names, handles)})
        return tok

    by_rows = lambda a: a.reshape(N_DEV, a.shape[0] // N_DEV, a.shape[1])
    for l in reversed(range(depth)):
        each = l == 0
        sv = saved[l]
        gaq, gak, gcq, gck = sv["gains"]
        da = matmul(dx_b, wd[l], trans_b=True, name="down_proj_dx")
        big["w_down"][l] = by_rows(matmul(sv["act"], dx_b, trans_a=True, out_dtype=BF16, name="down_proj_dw"))
        if each:
            token = scatter(l, ("w_down",))
        dp, small["conv_w"][l], small["conv_b"][l] = ffn_act_bwd(da, sv["p"], cw[l], conv_b[l] + token[0, 0],
                                                                 name="ffn_act_bwd")
        dh2 = matmul(dp, wu[l], trans_b=True, name="up_proj_dx")
        big["w_up"][l] = matmul(sv["h2"], dp, trans_a=True, out_dtype=BF16, col_blocks=N_DEV, name="up_proj_dw")
        if each:
            token = scatter(l, ("w_up",))
        dx_mid, dx_mid_b, small["ffn_norm"][l] = rmsnorm_bwd(dh2, sv["x_mid"], ffn_norm[l] + token[0, 0], dx,
                                                   name="ffn_norm_bwd")
        dmix = matmul(dx_mid_b, wo[l], trans_b=True, name="out_proj_dx")
        big["w_out"][l] = by_rows(matmul(sv["mix"], dx_mid_b, trans_a=True, out_dtype=BF16, name="out_proj_dw"))
        if each:
            token = scatter(l, ("w_out",))
        (d_oa, d_ob, d_oc), small["mix_out_gain"][l] = mixnorm_bwd(
            dmix, [sv["out_a"], sv["out_b"], sv["out_c"]], mix_out_gain[l] + token[0, 0], name="mix_norm_bwd")
        dproj = lax.empty((s, in_width), BF16)
        dproj, db_a, dgq_a, dgk_a, dsink = banded_bwd(
            sv["proj"], cb_aq, cb_ak, cb_av, sa, gaq, gak, bias_a, (1,), sv["sinks"], True,
            d_oa, sv["out_a"], sv["lse_a"], dproj, name="swa_bwd")
        dproj = sb_bwd(sv["proj"], cb_bq, cb_bk, cb_bv, sb, d_ob, sv["tot_b"], dproj, name="stick_bwd")
        dproj, db_c, dgq_c, dgk_c = banded_bwd(
            sv["proj"], cb_cq, cb_ck, cb_cv, sc, gcq, gck, bias_c, DILATIONS, None, False,
            d_oc, sv["out_c"], sv["lse_c"], dproj, name="dilated_bwd")
        fold = lambda g: g.reshape(-1, HEAD_DIM).sum(axis=0)
        small["a_q_gain"][l], small["a_k_gain"][l] = fold(dgq_a), fold(dgk_a)
        small["c_q_gain"][l], small["c_k_gain"][l] = fold(dgq_c), fold(dgk_c)
        small["a_sinks"][l] = dsink[:, ::HEAD_DIM].reshape(-1)
        dbias_a = db_a if dbias_a is None else dbias_a + db_a
        dbias_c = db_c if dbias_c is None else dbias_c + db_c
        big["w_in"][l] = by_rows(matmul(dproj, sv["h1"], trans_a=True, out_dtype=BF16, name="in_proj_dw"))
        if not each:
            token = scatter(l, names_big)
        dh1 = matmul(dproj, wi[l], trans_b=True, name="in_proj_dx")
        dx, dx_b, small["attn_norm"][l] = rmsnorm_bwd(dh1, sv["x_in"], attn_norm[l] + token[0, 0], dx_mid,
                                                name="attn_norm_bwd")

    dtable = jnp.concatenate([bias_bwd(dbias_a, buckets_a, name="swa_bias_bwd"),
                              bias_bwd(dbias_c, buckets_c, name="dilated_bias_bwd")], axis=1)

    order = ("attn_norm", "a_q_gain", "a_k_gain", "a_sinks", "c_q_gain", "c_k_gain", "rel_bias_table",
             "mix_out_gain", "ffn_norm", "conv_w", "conv_b")
    partial = {k: jnp.stack(v) for k, v in small.items()}
    partial["rel_bias_table"] = dtable
    pieces = [partial[k] for k in order] + [loss_part.reshape(1)]
    n_small = sum(math.prod(pc.shape) for pc in pieces)
    rows = -(-n_small // (8 * LANES)) * 8
    gathered = gather_small(_pack(pieces, rows), name="gather_small_grads")
    big["w_in"][0], gathered = lax.optimization_barrier((big["w_in"][0], gathered))
    scatter(0, ("w_in",))
    summed = _unpack(sum_devices(gathered.reshape(N_DEV, rows, LANES), name="sum_small_grads"),
                     [pc.shape for pc in pieces])
    g_small = dict(zip(order, summed[:-1]))
    loss = summed[-1][0]
    g_small["conv_w"] = lax.dynamic_slice_in_dim(g_small["conv_w"], dev * ff2_shard, ff2_shard, axis=2)

    w_small = dict(attn_norm=attn_norm, a_q_gain=a_q_gain, a_k_gain=a_k_gain, a_sinks=a_sinks, c_q_gain=c_q_gain,
                   c_k_gain=c_k_gain, rel_bias_table=rel_bias_table, mix_out_gain=mix_out_gain, ffn_norm=ffn_norm,
                   conv_w=conv_w, conv_b=conv_b)
    m_small = dict(attn_norm=m_attn_norm, a_q_gain=m_a_q_gain, a_k_gain=m_a_k_gain, a_sinks=m_a_sinks,
                   c_q_gain=m_c_q_gain, c_k_gain=m_c_k_gain, rel_bias_table=m_rel_bias_table,
                   mix_out_gain=m_mix_out_gain, ffn_norm=m_ffn_norm, conv_w=m_conv_w, conv_b=m_conv_b)
    v_small = dict(attn_norm=v_attn_norm, a_q_gain=v_a_q_gain, a_k_gain=v_a_k_gain, a_sinks=v_a_sinks,
                   c_q_gain=v_c_q_gain, c_k_gain=v_c_k_gain, rel_bias_table=v_rel_bias_table,
                   mix_out_gain=v_mix_out_gain, ffn_norm=v_ffn_norm, conv_w=v_conv_w, conv_b=v_conv_b)
    shapes = [w_small[k].shape for k in order]
    n_upd = sum(math.prod(sh) for sh in shapes)
    urows = -(-n_upd // (8 * LANES)) * 8
    packs = [_pack([d[k] for k in order], urows) for d in (g_small, w_small, m_small, v_small)]
    upd = adamw_small(*packs, name="adamw_small")
    delta_s, newm_s, newv_s = [dict(zip(order, _unpack(u, shapes))) for u in upd]

    flip = lambda t: jnp.swapaxes(t, 1, 2)
    w_big = dict(w_in=(flip(w_in), flip(m_w_in), flip(v_w_in)), w_out=(w_out, m_w_out, v_w_out),
                 w_up=(w_up, m_w_up, v_w_up), w_down=(w_down, m_w_down, v_w_down))
    results = {k: [lax.empty(w_big[k][0].shape, F32) for _ in range(4)] for k in names_big}
    after = upd[0]
    for l, names in [(l, names_big) for l in reversed(range(1, depth))] + [(0, names_big[1:]), (0, names_big[:1])]:
        landed = exchange_wait([scatters[l, n] for n in names], True, after,
                               name=f"scatter_wait_{l}_{names_big.index(names[0])}")
        for k, parts in zip(names, landed):
            results[k] = adamw_parts(parts, *w_big[k], l, results[k], name="adamw_large")
        after = results[names[-1]][0]
    results["w_in"] = [flip(t) for t in results["w_in"]]
    g_big, delta_b, newm_b, newv_b = [{k: results[k][i] for k in names_big} for i in range(4)]

    all_names = ("attn_norm", "w_in", "a_q_gain", "a_k_gain", "a_sinks", "c_q_gain", "c_k_gain", "rel_bias_table",
                 "mix_out_gain", "w_out", "ffn_norm", "w_up", "conv_w", "conv_b", "w_down")
    pick = lambda sm, bg: [bg[k] if k in bg else sm[k] for k in all_names]
    return (loss, dx[None], *pick(g_small, g_big), *pick(delta_s, delta_b), *pick(newm_s, newm_b),
            *pick(newv_s, newv_b))
```

```python
import functools
import math

import jax
import jax.numpy as jnp
from jax import lax
from jax.experimental import pallas as pl
from jax.experimental.pallas import tpu as pltpu

F32, BF16, I32 = jnp.float32, jnp.bfloat16, jnp.int32
MESH = pl.DeviceIdType.MESH

HEAD_DIM = 64
LANES = 128
BLOCK = 128
EPS = 1e-6
NEG_INF = -1e30
N_BUCKETS = 32
T5_MAX_DIST = 2048
WINDOW_A = 128
DILATIONS = (1, 4, 16)
N_DEV = 8
VMEM_LIMIT = 56 * 1024 * 1024
MATMUL_VMEM = 46 * 1024 * 1024

ADAM_LR, ADAM_B1, ADAM_B2, ADAM_EPS, ADAM_WD, ADAM_STEP = 0.001, 0.9, 0.999, 1e-08, 0.01, 10


def _params(sem=None, vmem=None):
    return pltpu.CompilerParams(dimension_semantics=sem, vmem_limit_bytes=vmem)


def _pick(n, cands):
    for c in cands:
        if n % c == 0:
            return c
    raise ValueError(f"no tile for {n}")


def _dot(a, b):
    return lax.dot_general(a, b, (((1,), (0,)), ((), ())), preferred_element_type=F32)


def _dot_nt(a, b):
    return lax.dot_general(a, b, (((1,), (1,)), ((), ())), preferred_element_type=F32)


def _dot_tn(a, b):
    return lax.dot_general(a, b, (((0,), (0,)), ((), ())), preferred_element_type=F32)


def matmul(a, b, *, trans_a=False, trans_b=False, out_dtype=F32, res=None, col_blocks=None, name):
    a_halves, b_halves = a.ndim == 3, b.ndim == 3
    assert not (a_halves and trans_a) and not (b_halves and trans_b)
    m, k = (a.shape[1], 2 * a.shape[2]) if a_halves else (a.shape[1], a.shape[0]) if trans_a else a.shape
    n = 2 * b.shape[2] if b_halves else b.shape[0] if trans_b else b.shape[1]
    k_unit, n_unit = (k // 2 if a_halves else k), (n // 2 if b_halves else n)
    tm = _pick(m, (1408, 1024, 896, 512, 256))
    tn_cands = ((n // col_blocks,) if col_blocks
                else tuple(t for t in (1024, 1408, 768, 512, 256, 128) if n_unit % t == 0))

    def footprint(tk, tn):
        tiles = 2 * (tm * tk * a.dtype.itemsize + tk * tn * b.dtype.itemsize)
        return tiles + tm * tn * (4 + 2 * jnp.dtype(out_dtype).itemsize + (8 if res is not None else 0))

    tk, tn = next((tk, tn) for tk in (5376, 4096, 2816, 2048, 1792, 1024, 768, 512, 256) if k_unit % tk == 0
                  for tn in tn_cands if footprint(tk, tn) <= MATMUL_VMEM)
    nk = k // tk
    nk_half, nj_half = k_unit // tk, n_unit // tn
    dn = (((0 if trans_a else 1,), (1 if trans_b else 0,)), ((), ()))

    def body(*refs):
        if res is None:
            a_ref, b_ref, o_ref, acc = refs
        else:
            a_ref, b_ref, r_ref, o_ref, acc = refs
        kk = pl.program_id(2)

        @pl.when(kk == 0)
        def _():
            acc[...] = jnp.zeros_like(acc)

        acc[...] += lax.dot_general(a_ref[...].astype(BF16), b_ref[...].astype(BF16), dn,
                                    preferred_element_type=F32)

        @pl.when(kk == nk - 1)
        def _():
            r = acc[...]
            if res is not None:
                r = r_ref[...] + r
            o_ref[...] = r.astype(out_dtype)

    b_spec = (pl.BlockSpec((tn, tk), lambda i, j, kk: (j, kk)) if trans_b
              else pl.BlockSpec((None, tk, tn), lambda i, j, kk: (j // nj_half, kk, j % nj_half)) if b_halves
              else pl.BlockSpec((tk, tn), lambda i, j, kk: (kk, j)))
    a_spec = (pl.BlockSpec((tk, tm), lambda i, j, kk: (kk, i)) if trans_a
              else pl.BlockSpec((None, tm, tk), lambda i, j, kk: (kk // nk_half, i, kk % nk_half)) if a_halves
              else pl.BlockSpec((tm, tk), lambda i, j, kk: (i, kk)))
    in_specs = [a_spec, b_spec]
    args = [a, b]
    if res is not None:
        in_specs.append(pl.BlockSpec((tm, tn), lambda i, j, kk: (i, j)))
        args.append(res)
    if col_blocks:
        out_spec = pl.BlockSpec((None, tm, tn), lambda i, j, kk: (j, i, 0))
        out_shape = jax.ShapeDtypeStruct((col_blocks, m, tn), out_dtype)
    else:
        out_spec = pl.BlockSpec((tm, tn), lambda i, j, kk: (i, j))
        out_shape = jax.ShapeDtypeStruct((m, n), out_dtype)
    return pl.pallas_call(
        body, name=name, grid=(m // tm, n // tn, nk),
        in_specs=in_specs, out_specs=out_spec, out_shape=out_shape,
        scratch_shapes=[pltpu.VMEM((tm, tn), F32)],
        compiler_params=_params(("parallel", "parallel", "arbitrary"), VMEM_LIMIT),
    )(*args)


def rmsnorm_fwd(x, g, *, name):
    s, d = x.shape
    tm = 512

    def body(x_ref, g_ref, o_ref):
        xv = x_ref[...]
        r = lax.rsqrt(jnp.mean(xv * xv, axis=-1, keepdims=True) + EPS)
        o_ref[...] = (xv * r * g_ref[...]).astype(BF16)

    return pl.pallas_call(
        body, name=name, grid=(s // tm,),
        in_specs=[pl.BlockSpec((tm, d), lambda i: (i, 0)), pl.BlockSpec((1, d), lambda i: (0, 0))],
        out_specs=pl.BlockSpec((tm, d), lambda i: (i, 0)),
        out_shape=jax.ShapeDtypeStruct((s, d), BF16),
        compiler_params=_params(("parallel",)),
    )(x, g.reshape(1, d))


def rmsnorm_bwd(dh, x, g, dres, *, name):
    s, d = x.shape
    tm = 256

    def body(dh_ref, x_ref, g_ref, dres_ref, dx_ref, dxb_ref, dg_ref):
        @pl.when(pl.program_id(0) == 0)
        def _():
            dg_ref[...] = jnp.zeros_like(dg_ref)

        xv, dhv = x_ref[...], dh_ref[...]
        r = lax.rsqrt(jnp.mean(xv * xv, axis=-1, keepdims=True) + EPS)
        gd = dhv * g_ref[...]
        dot = jnp.mean(gd * xv, axis=-1, keepdims=True)
        dx = dres_ref[...] + (r * gd - xv * (r * r * r * dot))
        dx_ref[...] = dx
        dxb_ref[...] = dx.astype(BF16)
        dg_ref[...] += jnp.sum(dhv * (xv * r), axis=0, keepdims=True)

    row = pl.BlockSpec((tm, d), lambda i: (i, 0))
    dx, dxb, dg = pl.pallas_call(
        body, name=name, grid=(s // tm,),
        in_specs=[row, row, pl.BlockSpec((1, d), lambda i: (0, 0)), row],
        out_specs=[row, row, pl.BlockSpec((1, d), lambda i: (0, 0))],
        out_shape=[jax.ShapeDtypeStruct((s, d), F32), jax.ShapeDtypeStruct((s, d), BF16),
                   jax.ShapeDtypeStruct((1, d), F32)],
        compiler_params=_params(("arbitrary",)),
    )(dh, x, g.reshape(1, d), dres)
    return dx, dxb, dg[0]


def loss_head(y, target, *, name):
    s, d = y.shape
    tm = 512

    def body(y_ref, t_ref, dy_ref, dyb_ref, l_ref):
        @pl.when(pl.program_id(0) == 0)
        def _():
            l_ref[...] = jnp.zeros_like(l_ref)

        e = y_ref[...] - t_ref[...]
        dy = e / float(d)
        dy_ref[...] = dy
        dyb_ref[...] = dy.astype(BF16)
        per_tok = jnp.mean(e * e, axis=-1, keepdims=True)
        l_ref[...] += 0.5 * jnp.sum(per_tok, axis=0, keepdims=True)

    row = pl.BlockSpec((tm, d), lambda i: (i, 0))
    dy, dyb, l = pl.pallas_call(
        body, name=name, grid=(s // tm,),
        in_specs=[row, row],
        out_specs=[row, row, pl.BlockSpec((8, LANES), lambda i: (0, 0))],
        out_shape=[jax.ShapeDtypeStruct((s, d), F32), jax.ShapeDtypeStruct((s, d), BF16),
                   jax.ShapeDtypeStruct((8, LANES), F32)],
        compiler_params=_params(("arbitrary",)),
    )(y, target)
    return dy, dyb, l[0, 0]


FFN_TN = 256
FFN_CH = 256


def _rows_before(ref, r0, first):
    if first:
        cur = ref[pl.ds(0, FFN_CH), :]
        row = lax.broadcasted_iota(I32, cur.shape, 0)
        sh1 = jnp.where(row < 1, 0.0, pltpu.roll(cur, 1, axis=0))
        sh2 = jnp.where(row < 2, 0.0, pltpu.roll(cur, 2, axis=0))
        return cur, sh1, sh2
    ext = ref[pl.ds(pl.multiple_of(r0 - 8, 8), FFN_CH + 8), :]
    return ext[8:], pltpu.roll(ext, 1, axis=0)[8:], pltpu.roll(ext, 2, axis=0)[8:]


def _rows_after(ref, r0, last):
    if last:
        cur = ref[pl.ds(r0, FFN_CH), :]
        row = lax.broadcasted_iota(I32, cur.shape, 0)
        up1 = jnp.where(row >= FFN_CH - 1, 0.0, pltpu.roll(cur, FFN_CH - 1, axis=0))
        up2 = jnp.where(row >= FFN_CH - 2, 0.0, pltpu.roll(cur, FFN_CH - 2, axis=0))
        return cur, up1, up2
    n = FFN_CH + 8
    ext = ref[pl.ds(r0, n), :]
    return ext[:FFN_CH], pltpu.roll(ext, n - 1, axis=0)[:FFN_CH], pltpu.roll(ext, n - 2, axis=0)[:FFN_CH]


def _sigmoid(x):
    return 0.5 * jnp.tanh(0.5 * x) + 0.5


def ffn_act_fwd(p, conv_w, conv_b, *, name):
    s, f2 = p.shape
    f = f2 // 2
    nj = f // FFN_TN
    nch = s // FFN_CH

    def body(pg_ref, pu_ref, wg_ref, wu_ref, bg_ref, bu_ref, a_ref):
        def conv(ref, w_ref, b_ref, r0, first):
            cur, sh1, sh2 = _rows_before(ref, r0, first)
            return ((b_ref[...] + w_ref[0:1, :] * sh2) + w_ref[1:2, :] * sh1) + w_ref[2:3, :] * cur

        def chunk(r0, first):
            gate = conv(pg_ref, wg_ref, bg_ref, r0, first)
            up = conv(pu_ref, wu_ref, bu_ref, r0, first)
            a_ref[pl.ds(r0, FFN_CH), :] = (gate * _sigmoid(gate) * up).astype(BF16)

        chunk(0, True)

        def step(c, carry):
            chunk(pl.multiple_of(c * FFN_CH, FFN_CH), False)
            return carry

        lax.fori_loop(1, nch, step, 0)

    col = lambda off: pl.BlockSpec((s, FFN_TN), lambda j: (0, j + off))
    wcol = lambda off: pl.BlockSpec((3, FFN_TN), lambda j: (0, j + off))
    bcol = lambda off: pl.BlockSpec((1, FFN_TN), lambda j: (0, j + off))
    return pl.pallas_call(
        body, name=name, grid=(nj,),
        in_specs=[col(0), col(nj), wcol(0), wcol(nj), bcol(0), bcol(nj)],
        out_specs=pl.BlockSpec((s, FFN_TN), lambda j: (0, j)),
        out_shape=jax.ShapeDtypeStruct((s, f), BF16),
        compiler_params=_params(("parallel",), VMEM_LIMIT),
    )(p, p, conv_w, conv_w, conv_b.reshape(1, f2), conv_b.reshape(1, f2))


def ffn_act_bwd(da, p, conv_w, conv_b, *, name):
    s, f2 = p.shape
    f = f2 // 2
    nj = f // FFN_TN
    nch = s // FFN_CH

    def body(da_ref, pg_ref, pu_ref, wg_ref, wu_ref, bg_ref, bu_ref,
             dp_ref, dwg_ref, dwu_ref, dbg_ref, dbu_ref, dug_s, duu_s):
        dpg_ref, dpu_ref = dp_ref.at[0], dp_ref.at[1]
        def conv(ref, w_ref, b_ref, r0, first):
            cur, sh1, sh2 = _rows_before(ref, r0, first)
            u = ((b_ref[...] + w_ref[0:1, :] * sh2) + w_ref[1:2, :] * sh1) + w_ref[2:3, :] * cur
            return u, (sh2, sh1, cur)

        def taps_sum(du, taps):
            return jnp.concatenate([jnp.sum(du * t, axis=0, keepdims=True) for t in taps], axis=0)

        def chunk(r0, first, acc):
            dwg, dwu, dbg, dbu = acc
            gate, tg = conv(pg_ref, wg_ref, bg_ref, r0, first)
            up, tu = conv(pu_ref, wu_ref, bu_ref, r0, first)
            dav = da_ref[pl.ds(r0, FFN_CH), :]
            sg = _sigmoid(gate)
            dgate = dav * up * (sg * (1.0 + gate * (1.0 - sg)))
            dup = dav * (gate * sg)
            dug_s[pl.ds(r0, FFN_CH), :] = dgate
            duu_s[pl.ds(r0, FFN_CH), :] = dup
            return (dwg + taps_sum(dgate, tg), dwu + taps_sum(dup, tu),
                    dbg + jnp.sum(dgate, axis=0, keepdims=True), dbu + jnp.sum(dup, axis=0, keepdims=True))

        z3 = jnp.zeros((3, FFN_TN), F32)
        z1 = jnp.zeros((1, FFN_TN), F32)
        acc = chunk(0, True, (z3, z3, z1, z1))
        acc = lax.fori_loop(1, nch, lambda c, a: chunk(pl.multiple_of(c * FFN_CH, FFN_CH), False, a), acc)
        dwg_ref[...], dwu_ref[...], dbg_ref[...], dbu_ref[...] = acc

        def back(src, w_ref, dst, r0, last):
            cur, up1, up2 = _rows_after(src, r0, last)
            dst[pl.ds(r0, FFN_CH), :] = (w_ref[2:3, :] * cur + w_ref[1:2, :] * up1 + w_ref[0:1, :] * up2).astype(BF16)

        def step(c, carry):
            r0 = pl.multiple_of(c * FFN_CH, FFN_CH)
            back(dug_s, wg_ref, dpg_ref, r0, False)
            back(duu_s, wu_ref, dpu_ref, r0, False)
            return carry

        lax.fori_loop(0, nch - 1, step, 0)
        back(dug_s, wg_ref, dpg_ref, (nch - 1) * FFN_CH, True)
        back(duu_s, wu_ref, dpu_ref, (nch - 1) * FFN_CH, True)

    col = lambda off: pl.BlockSpec((s, FFN_TN), lambda j: (0, j + off))
    wcol = lambda off: pl.BlockSpec((3, FFN_TN), lambda j: (0, j + off))
    bcol = lambda off: pl.BlockSpec((1, FFN_TN), lambda j: (0, j + off))
    outs = pl.pallas_call(
        body, name=name, grid=(nj,),
        in_specs=[col(0), col(0), col(nj), wcol(0), wcol(nj), bcol(0), bcol(nj)],
        out_specs=[pl.BlockSpec((2, s, FFN_TN), lambda j: (0, 0, j)), wcol(0), wcol(0), bcol(0), bcol(0)],
        out_shape=[jax.ShapeDtypeStruct((2, s, f), BF16),
                   jax.ShapeDtypeStruct((3, f), F32), jax.ShapeDtypeStruct((3, f), F32),
                   jax.ShapeDtypeStruct((1, f), F32), jax.ShapeDtypeStruct((1, f), F32)],
        scratch_shapes=[pltpu.VMEM((s, FFN_TN), F32), pltpu.VMEM((s, FFN_TN), F32)],
        compiler_params=_params(("parallel",), VMEM_LIMIT),
    )(da, p, p, conv_w, conv_w, conv_b.reshape(1, f2), conv_b.reshape(1, f2))
    dp, dwg, dwu, dbg, dbu = outs
    return dp, jnp.concatenate([dwg, dwu], axis=1), jnp.concatenate([dbg, dbu], axis=1)[0]


def mixnorm_fwd(outs, gain, *, name):
    s = outs[0].shape[0]
    widths = [o.shape[1] for o in outs]
    total = sum(widths)
    tm = 512

    def body(*refs):
        o_refs, g_ref, m_ref = refs[:-2], refs[-2], refs[-1]
        off = 0
        for o_ref, w in zip(o_refs, widths):
            xv = o_ref[...]
            r = lax.rsqrt(jnp.mean(xv * xv, axis=-1, keepdims=True) + EPS)
            m_ref[:, off:off + w] = (xv * r * g_ref[:, off:off + w]).astype(BF16)
            off += w

    return pl.pallas_call(
        body, name=name, grid=(s // tm,),
        in_specs=[pl.BlockSpec((tm, w), lambda i: (i, 0)) for w in widths] + [pl.BlockSpec((1, total), lambda i: (0, 0))],
        out_specs=pl.BlockSpec((tm, total), lambda i: (i, 0)),
        out_shape=jax.ShapeDtypeStruct((s, total), BF16),
        compiler_params=_params(("parallel",)),
    )(*outs, gain.reshape(1, total))


def mixnorm_bwd(dmix, outs, gain, *, name):
    s = outs[0].shape[0]
    widths = [o.shape[1] for o in outs]
    total = sum(widths)
    n = len(outs)
    tm = 256

    def body(*refs):
        dm_ref, o_refs, g_ref = refs[0], refs[1:1 + n], refs[1 + n]
        d_refs, dg_ref = refs[2 + n:2 + 2 * n], refs[2 + 2 * n]

        @pl.when(pl.program_id(0) == 0)
        def _():
            dg_ref[...] = jnp.zeros_like(dg_ref)

        off = 0
        for o_ref, d_ref, w in zip(o_refs, d_refs, widths):
            xv = o_ref[...]
            dhv = dm_ref[:, off:off + w]
            r = lax.rsqrt(jnp.mean(xv * xv, axis=-1, keepdims=True) + EPS)
            gd = dhv * g_ref[:, off:off + w]
            dot = jnp.mean(gd * xv, axis=-1, keepdims=True)
            d_ref[...] = r * gd - xv * (r * r * r * dot)
            dg_ref[:, off:off + w] += jnp.sum(dhv * (xv * r), axis=0, keepdims=True)
            off += w

    res = pl.pallas_call(
        body, name=name, grid=(s // tm,),
        in_specs=[pl.BlockSpec((tm, total), lambda i: (i, 0))]
        + [pl.BlockSpec((tm, w), lambda i: (i, 0)) for w in widths] + [pl.BlockSpec((1, total), lambda i: (0, 0))],
        out_specs=[pl.BlockSpec((tm, w), lambda i: (i, 0)) for w in widths] + [pl.BlockSpec((1, total), lambda i: (0, 0))],
        out_shape=[jax.ShapeDtypeStruct((s, w), F32) for w in widths] + [jax.ShapeDtypeStruct((1, total), F32)],
        compiler_params=_params(("arbitrary",)),
    )(dmix, *outs, gain.reshape(1, total))
    return res[:n], res[n][0]


NORM_CH = 512
FWD_TILES = 4
BWD_TILES = 4


def _lo_mask(shape):
    return lax.broadcasted_iota(I32, shape, 1) < HEAD_DIM


def _head_sum(x, lo):
    del lo
    i = lax.broadcasted_iota(I32, (LANES, LANES), 0) // HEAD_DIM
    j = lax.broadcasted_iota(I32, (LANES, LANES), 1) // HEAD_DIM
    return _split_dot(x, _twice(i == j))


def _head_stats(x, lo):
    return lax.rsqrt(_head_sum(x * x, lo) * (1.0 / HEAD_DIM) + EPS)


def _swap_halves(x):
    return pltpu.roll(x, HEAD_DIM, axis=1)


def _replicate_head(x, lo, use_lo_head):
    sw = _swap_halves(x)
    return jnp.where(use_lo_head, jnp.where(lo, x, sw), jnp.where(lo, sw, x))


def _tile_rows(i, s, d):
    nb = s // (BLOCK * d)
    r = i // nb
    b = i % nb
    start = r + (BLOCK * d) * b
    prev = start - (BLOCK * d) * jnp.minimum(b, 1)
    return start, prev, b > 0


def _rows(ref, start, d):
    if d == 1:
        return ref[pl.ds(pl.multiple_of(start, BLOCK), BLOCK), :]
    return ref[pl.ds(start, BLOCK, stride=d), :]


def _set_rows(ref, start, d, val):
    if d == 1:
        ref[pl.ds(pl.multiple_of(start, BLOCK), BLOCK), :] = val
    else:
        ref[pl.ds(start, BLOCK, stride=d), :] = val


def banded_fwd(proj, qb0, kb0, vb0, n_slabs, gq, gk, bias, dils, sinks, gqa, *, name):
    s = proj.shape[0]
    nbr = len(dils)
    nt = s // BLOCK
    nch = s // NORM_CH
    has_sink = sinks is not None

    def body(*refs):
        q_ref, k_ref, v_ref, gq_ref, gk_ref, b_ref = refs[:6]
        rest = refs[6:]
        if has_sink:
            sink_ref, rest = rest[0], rest[1:]
        out_ref, lse_ref, qn_s, kn_s, vv_s, o_s, l_s = rest
        p = pl.program_id(0)
        use_lo = (p // 2) == 0

        def prep(c, carry):
            rows = pl.ds(pl.multiple_of(c * NORM_CH, NORM_CH), NORM_CH)
            lo = _lo_mask((NORM_CH, LANES))
            qv, kv, vv = q_ref[rows, :], k_ref[rows, :], v_ref[rows, :]
            qn_s[rows, :] = qv * _head_stats(qv, lo) * gq_ref[...] * (HEAD_DIM ** -0.5)
            kn = kv * _head_stats(kv, lo) * gk_ref[...]
            if gqa:
                kn = _replicate_head(kn, lo, use_lo)
                vv = _replicate_head(vv, lo, use_lo)
            kn_s[rows, :] = kn
            vv_s[rows, :] = vv
            return carry

        lax.fori_loop(0, nch, prep, 0)

        lo = _lo_mask((BLOCK, LANES))
        hms = [lo, jnp.logical_not(lo)]
        heads, tiles = range(2), range(FWD_TILES)
        for br, d in enumerate(dils):
            def step(ii, carry, br=br, d=d):
                pos = [_tile_rows(ii * FWD_TILES + u, s, d) for u in tiles]
                kc = [carry[0]] + [_rows(kn_s, pos[u][0], d).astype(BF16) for u in tiles]
                vc = [carry[1]] + [_rows(vv_s, pos[u][0], d).astype(BF16) for u in tiles]
                kcat = [jnp.concatenate([kc[u], kc[u + 1]], axis=0) for u in tiles]
                vcat = [jnp.concatenate([vc[u], vc[u + 1]], axis=0) for u in tiles]
                qt = [_rows(qn_s, pos[u][0], d) for u in tiles]
                sc = [[_dot_nt(jnp.where(hms[h], qt[u], 0.0).astype(BF16), kcat[u])
                       + b_ref[br, jnp.where(pos[u][2], 0, 1), h] for h in heads] for u in tiles]
                m = [[jnp.max(sc[u][h], axis=1, keepdims=True) for h in heads] for u in tiles]
                pe = [[jnp.exp(sc[u][h] - m[u][h]) for h in heads] for u in tiles]
                den = [[jnp.sum(pe[u][h], axis=1, keepdims=True) for h in heads] for u in tiles]
                o = [[_dot(pe[u][h].astype(BF16), vcat[u]) * (1.0 / den[u][h]) for h in heads] for u in tiles]
                for u in tiles:
                    _set_rows(o_s.at[br], pos[u][0], d, jnp.where(lo, o[u][0], o[u][1]))
                    _set_rows(l_s.at[br], pos[u][0], d,
                              jnp.where(lo, m[u][0] + jnp.log(den[u][0]), m[u][1] + jnp.log(den[u][1])))
                return kc[-1], vc[-1]

            none_yet = jnp.zeros((BLOCK, LANES), BF16)
            lax.fori_loop(0, nt // FWD_TILES, step, (none_yet, none_yet))

        def combine(c, carry):
            rows = pl.ds(pl.multiple_of(c * NORM_CH, NORM_CH), NORM_CH)
            ls = [l_s[br, rows, :] for br in range(nbr)]
            mx = functools.reduce(jnp.maximum, ls)
            if has_sink:
                mx = jnp.maximum(mx, sink_ref[...])
            tot = functools.reduce(jnp.add, [jnp.exp(l - mx) for l in ls])
            if has_sink:
                tot = tot + jnp.exp(sink_ref[...] - mx)
            lse = mx + jnp.log(tot)
            acc = jnp.exp(ls[0] - lse) * o_s[0, rows, :]
            for br in range(1, nbr):
                acc = acc + jnp.exp(ls[br] - lse) * o_s[br, rows, :]
            out_ref[rows, :] = acc
            lse_ref[rows, :] = lse
            return carry

        lax.fori_loop(0, nch, combine, 0)

    slab = lambda b0, shared: pl.BlockSpec((s, LANES), (lambda p: (0, b0)) if shared else (lambda p: (0, b0 + p)),
                                           pipeline_mode=pl.Buffered(1))
    vec = pl.BlockSpec((1, LANES), lambda p: (0, 0))
    in_specs = [slab(qb0, False), slab(kb0, gqa), slab(vb0, gqa), vec, vec,
                pl.BlockSpec((nbr, 2, 2, BLOCK, 2 * BLOCK), lambda p: (0, 0, p, 0, 0))]
    args = [proj, proj, proj, gq.reshape(1, LANES), gk.reshape(1, LANES), bias]
    if has_sink:
        in_specs.append(pl.BlockSpec((None, 1, LANES), lambda p: (p, 0, 0)))
        args.append(sinks)
    w = LANES * n_slabs
    return pl.pallas_call(
        body, name=name, grid=(n_slabs,),
        in_specs=in_specs,
        out_specs=[pl.BlockSpec((s, LANES), lambda p: (0, p)), pl.BlockSpec((s, LANES), lambda p: (0, p))],
        out_shape=[jax.ShapeDtypeStruct((s, w), F32), jax.ShapeDtypeStruct((s, w), F32)],
        scratch_shapes=[pltpu.VMEM((s, LANES), F32), pltpu.VMEM((s, LANES), F32), pltpu.VMEM((s, LANES), F32),
                        pltpu.VMEM((nbr, s, LANES), F32), pltpu.VMEM((nbr, s, LANES), F32)],
        compiler_params=_params(("parallel",), VMEM_LIMIT),
    )(*args)


def banded_bwd(proj, qb0, kb0, vb0, n_slabs, gq, gk, bias, dils, sinks, gqa, dout, out, lse, dproj, *, name):
    s = proj.shape[0]
    nbr = len(dils)
    nt = s // BLOCK
    nch = s // NORM_CH
    has_sink = sinks is not None
    scale = HEAD_DIM ** -0.5

    def body(*refs):
        q_ref, k_ref, v_ref, gq_ref, gk_ref, b_ref, do_ref, o_ref, lse_ref = refs[:9]
        rest = refs[9:]
        if has_sink:
            sink_ref, rest = rest[0], rest[1:]
        dproj_ref, db_ref, dgq_ref, dgk_ref = rest[1:5]
        rest = rest[5:]
        if has_sink:
            dsink_ref, rest = rest[0], rest[1:]
        qn_s, kn_s, vv_s, dl_s, dqn_s, dkn_s, dvv_s, dq_ref, dk_ref, dv_ref, stage, sems = rest
        p = pl.program_id(0)
        use_lo = (p // 2) == 0

        def prep(c, carry):
            rows = pl.ds(pl.multiple_of(c * NORM_CH, NORM_CH), NORM_CH)
            lo = _lo_mask((NORM_CH, LANES))
            qv, kv, vv = q_ref[rows, :], k_ref[rows, :], v_ref[rows, :]
            qn_s[rows, :] = qv * _head_stats(qv, lo) * gq_ref[...] * scale
            kn = kv * _head_stats(kv, lo) * gk_ref[...]
            if gqa:
                kn = _replicate_head(kn, lo, use_lo)
                vv = _replicate_head(vv, lo, use_lo)
            kn_s[rows, :] = kn
            vv_s[rows, :] = vv
            delta = _head_sum(do_ref[rows, :] * o_ref[rows, :], lo)
            odd = lax.broadcasted_iota(I32, (NORM_CH, LANES), 1) % 2 == 1
            dl_s[rows, :] = jnp.where(odd, delta, lse_ref[rows, :])
            z = jnp.zeros((NORM_CH, LANES), F32)
            dqn_s[rows, :] = z
            dkn_s[rows, :] = z
            dvv_s[rows, :] = z
            if has_sink:
                ps = jnp.exp(sink_ref[...] - lse_ref[rows, :])
                return carry - jnp.sum(ps * delta, axis=0, keepdims=True)
            return carry

        dsink = lax.fori_loop(0, nch, prep, jnp.zeros((1, LANES), F32))
        if has_sink:
            dsink_ref[...] = jnp.broadcast_to(dsink, (8, LANES))

        lo = _lo_mask((BLOCK, LANES))
        hms = [lo, jnp.logical_not(lo)]
        heads, tiles = range(2), range(BWD_TILES)
        for br, d in enumerate(dils):
            db_ref[br] = jnp.zeros((2, BLOCK, 2 * BLOCK), F32)

            def step(ii, carry, br=br, d=d):
                pos = [_tile_rows(ii * BWD_TILES + u, s, d) for u in tiles]
                kc = [carry[0]] + [_rows(kn_s, pos[u][0], d).astype(BF16) for u in tiles]
                vc = [carry[1]] + [_rows(vv_s, pos[u][0], d).astype(BF16) for u in tiles]
                kcat = [jnp.concatenate([kc[u], kc[u + 1]], axis=0) for u in tiles]
                vcat = [jnp.concatenate([vc[u], vc[u + 1]], axis=0) for u in tiles]
                qt = [_rows(qn_s, pos[u][0], d) for u in tiles]
                dot_ = [_rows(do_ref, pos[u][0], d) for u in tiles]
                st_t = [_rows(dl_s, pos[u][0], d) for u in tiles]
                qh = [[jnp.where(hms[h], qt[u], 0.0).astype(BF16) for h in heads] for u in tiles]
                doh = [[jnp.where(hms[h], dot_[u], 0.0).astype(BF16) for h in heads] for u in tiles]
                sc = [[_dot_nt(qh[u][h], kcat[u]) + b_ref[br, jnp.where(pos[u][2], 0, 1), h] for h in heads]
                      for u in tiles]
                dp = [[_dot_nt(doh[u][h], vcat[u]) for h in heads] for u in tiles]
                lane0 = [0, HEAD_DIM]
                pr = [[jnp.exp(sc[u][h] - st_t[u][:, lane0[h]:lane0[h] + 1]) for h in heads] for u in tiles]
                dlog = [[pr[u][h] * (dp[u][h] - st_t[u][:, lane0[h] + 1:lane0[h] + 2]) for h in heads] for u in tiles]
                for h in heads:
                    db_ref[br, h] += functools.reduce(jnp.add, [dlog[u][h] for u in tiles])
                dlb = [[dlog[u][h].astype(BF16) for h in heads] for u in tiles]
                prb = [[pr[u][h].astype(BF16) for h in heads] for u in tiles]
                dq_t = [jnp.where(lo, _dot(dlb[u][0], kcat[u]), _dot(dlb[u][1], kcat[u])) * scale for u in tiles]
                rows2 = lambda x: jnp.concatenate(x, axis=0)
                dk_t = [_dot_tn(rows2(dlb[u]), rows2(qh[u])) for u in tiles]
                dv_t = [_dot_tn(rows2(prb[u]), rows2(doh[u])) for u in tiles]
                for u in tiles:
                    start, prev = pos[u][0], pos[u][1]
                    _set_rows(dqn_s, start, d, _rows(dqn_s, start, d) + dq_t[u])
                    _set_rows(dkn_s, prev, d, _rows(dkn_s, prev, d) + dk_t[u][:BLOCK])
                    _set_rows(dkn_s, start, d, _rows(dkn_s, start, d) + dk_t[u][BLOCK:])
                    _set_rows(dvv_s, prev, d, _rows(dvv_s, prev, d) + dv_t[u][:BLOCK])
                    _set_rows(dvv_s, start, d, _rows(dvv_s, start, d) + dv_t[u][BLOCK:])
                return kc[-1], vc[-1]

            none_yet = jnp.zeros((BLOCK, LANES), BF16)
            lax.fori_loop(0, nt // BWD_TILES, step, (none_yet, none_yet))

        if gqa:
            @pl.when(p == 0)
            def _():
                dk_ref[...] = jnp.zeros_like(dk_ref)
                dv_ref[...] = jnp.zeros_like(dv_ref)

        def finish(c, carry):
            dgq, dgk = carry
            rows = pl.ds(pl.multiple_of(c * NORM_CH, NORM_CH), NORM_CH)
            lo = _lo_mask((NORM_CH, LANES))

            def norm_bwd(xv, dn, g_ref):
                r = _head_stats(xv, lo)
                gd = dn * g_ref[...]
                dot = _head_sum(gd * xv, lo) * (1.0 / HEAD_DIM)
                return r * gd - xv * (r * r * r * dot), dn * (xv * r)

            dq, gq_part = norm_bwd(q_ref[rows, :], dqn_s[rows, :], gq_ref)
            dq_ref[rows, :] = dq
            dgq = dgq + jnp.sum(gq_part, axis=0, keepdims=True)
            kv, dkn, dvv = k_ref[rows, :], dkn_s[rows, :], dvv_s[rows, :]
            if gqa:
                kv = _replicate_head(kv, lo, use_lo)
                dkn = dkn + _swap_halves(dkn)
                dvv = dvv + _swap_halves(dvv)
                lane = lax.broadcasted_iota(I32, (NORM_CH, LANES), 1)
                mine = (lane // HEAD_DIM) == (p // 2)
                dk, gk_part = norm_bwd(kv, dkn, gk_ref)
                dk_ref[rows, :] += jnp.where(mine, dk, 0.0)
                dv_ref[rows, :] += jnp.where(mine, dvv, 0.0)
                gk_part = jnp.where(lo, gk_part, 0.0)
            else:
                dk, gk_part = norm_bwd(kv, dkn, gk_ref)
                dk_ref[rows, :] = dk
                dv_ref[rows, :] = dvv
            dgk = dgk + jnp.sum(gk_part, axis=0, keepdims=True)
            return dgq, dgk

        z = jnp.zeros((1, LANES), F32)
        dgq, dgk = lax.fori_loop(0, nch, finish, (z, z))
        dgq_ref[...] = jnp.broadcast_to(dgq, (8, LANES))
        dgk_ref[...] = jnp.broadcast_to(dgk, (8, LANES))
        if gqa:
            _store_slabs((dq_ref,), stage, dproj_ref, sems, (qb0 + p,))

            @pl.when(p == n_slabs - 1)
            def _():
                _store_slabs((dk_ref, dv_ref), stage, dproj_ref, sems, (kb0, vb0))
        else:
            _store_slabs((dq_ref, dk_ref, dv_ref), stage, dproj_ref, sems, (qb0 + p, kb0 + p, vb0 + p))

    def slab_of(width_blocks, b0, shared):
        return pl.BlockSpec((s, LANES), (lambda p: (0, b0)) if shared else (lambda p: (0, b0 + p)),
                            pipeline_mode=pl.Buffered(1))

    vec = pl.BlockSpec((1, LANES), lambda p: (0, 0))
    own = pl.BlockSpec((s, LANES), lambda p: (0, p), pipeline_mode=pl.Buffered(1))
    in_specs = [slab_of(0, qb0, False), slab_of(0, kb0, gqa), slab_of(0, vb0, gqa), vec, vec,
                pl.BlockSpec((nbr, 2, 2, BLOCK, 2 * BLOCK), lambda p: (0, 0, p, 0, 0)), own, own, own]
    args = [proj, proj, proj, gq.reshape(1, LANES), gk.reshape(1, LANES), bias, dout, out, lse]
    if has_sink:
        in_specs.append(pl.BlockSpec((None, 1, LANES), lambda p: (p, 0, 0)))
        args.append(sinks)
    held = pl.BlockSpec(memory_space=pl.ANY)
    in_specs.append(held)
    args.append(dproj)
    part = pl.BlockSpec((None, 8, LANES), lambda p: (p, 0, 0))
    out_specs = [held, pl.BlockSpec((nbr, 2, BLOCK, 2 * BLOCK), lambda p: (0, p, 0, 0)), part, part]
    out_shape = [jax.ShapeDtypeStruct(dproj.shape, dproj.dtype),
                 jax.ShapeDtypeStruct((nbr, 2 * n_slabs, BLOCK, 2 * BLOCK), F32),
                 jax.ShapeDtypeStruct((n_slabs, 8, LANES), F32), jax.ShapeDtypeStruct((n_slabs, 8, LANES), F32)]
    if has_sink:
        out_specs.append(part)
        out_shape.append(jax.ShapeDtypeStruct((n_slabs, 8, LANES), F32))
    res = pl.pallas_call(
        body, name=name, grid=(n_slabs,),
        in_specs=in_specs, out_specs=out_specs, out_shape=out_shape,
        input_output_aliases={len(args) - 1: 0},
        scratch_shapes=[pltpu.VMEM((s, LANES), F32) for _ in range(10)]
        + [pltpu.VMEM((3, s, LANES), BF16), pltpu.SemaphoreType.DMA((3,))],
        compiler_params=_params(("arbitrary",), VMEM_LIMIT),
    )(*args)
    outs = [res[0], res[1], res[2][:, 0, :], res[3][:, 0, :]]
    if has_sink:
        outs.append(res[4][:, 0, :])
    return outs


def bias_bwd(dbias, buckets, *, name):
    nbr, h = dbias.shape[:2]

    def body(db_ref, bk_ref, o_ref):
        lane = lax.broadcasted_iota(I32, (1, LANES), 1)
        acc = jnp.zeros((1, LANES), F32)
        for b in range(N_BUCKETS):
            tot = jnp.zeros((1, 1), F32)
            for br in range(nbr):
                sel = jnp.where(bk_ref[br] == b, db_ref[br], 0.0)
                tot = tot + jnp.sum(jnp.sum(sel, axis=0, keepdims=True), axis=1, keepdims=True)
            acc = jnp.where(lane == b, tot, acc)
        o_ref[...] = jnp.broadcast_to(acc, (8, LANES))

    res = pl.pallas_call(
        body, name=name, grid=(h,),
        in_specs=[pl.BlockSpec((nbr, None, BLOCK, 2 * BLOCK), lambda i: (0, i, 0, 0)),
                  pl.BlockSpec((nbr, BLOCK, 2 * BLOCK), lambda i: (0, 0, 0))],
        out_specs=pl.BlockSpec((None, 8, LANES), lambda i: (i, 0, 0)),
        out_shape=jax.ShapeDtypeStruct((h, 8, LANES), F32),
        compiler_params=_params(("parallel",)),
    )(dbias, buckets)
    return res[:, 0, :N_BUCKETS].T


SB_KG = 512
SB_QT = 2


def _softplus(z):
    return jnp.maximum(z, 0.0) + jnp.log(1.0 + jnp.exp(-jnp.abs(z)))


def _twice(t):
    t = t.astype(BF16)
    return jnp.concatenate([t, t], axis=0)


def _split_dot(x, t2):
    hi = x.astype(BF16)
    lo = (x - hi.astype(F32)).astype(BF16)
    return _dot(jnp.concatenate([hi, lo], axis=1), t2)


def sb_fwd(proj, qb0, kb0, vb0, n_slabs, *, name):
    s = proj.shape[0]
    nq = s // BLOCK
    nch = s // NORM_CH
    scale = HEAD_DIM ** -0.5

    def body(q_ref, k_ref, v_ref, o_ref, tot_ref, qlo_s, qhi_s, k_s, v_s):
        def prep(c, carry):
            rows = pl.ds(pl.multiple_of(c * NORM_CH, NORM_CH), NORM_CH)
            lo = _lo_mask((NORM_CH, LANES))
            qv = q_ref[rows, :] * scale
            qlo_s[rows, :] = jnp.where(lo, qv, 0.0).astype(BF16)
            qhi_s[rows, :] = jnp.where(lo, 0.0, qv).astype(BF16)
            k_s[rows, :] = k_ref[rows, :].astype(BF16)
            v_s[rows, :] = v_ref[rows, :].astype(BF16)
            return carry

        lax.fori_loop(0, nch, prep, 0)

        row = lax.broadcasted_iota(I32, (BLOCK, BLOCK), 0)
        col = lax.broadcasted_iota(I32, (BLOCK, BLOCK), 1)
        lo = col < HEAD_DIM
        t_ge = _twice(row >= col)
        rowg = lax.broadcasted_iota(I32, (BLOCK, SB_KG), 0)
        colg = lax.broadcasted_iota(I32, (BLOCK, SB_KG), 1)

        nsub = SB_KG // BLOCK
        chains = range(2 * SB_QT)
        nc = len(chains)

        def qloop(qs, phase):
            q0 = pl.multiple_of(qs * (SB_QT * BLOCK), SB_QT * BLOCK)
            qh = [(qlo_s, qhi_s)[i % 2][pl.ds(q0 + (i // 2) * BLOCK, BLOCK), :] for i in chains]
            gd = (qs * SB_QT) // nsub

            def logits(gi):
                k0 = pl.multiple_of(gi * SB_KG, SB_KG)
                kg = k_s[pl.ds(k0, SB_KG), :]
                return [_dot_nt(qh[i], kg) for i in chains]

            def group(gi, st, masks, npiece=nsub):
                k0 = pl.multiple_of(gi * SB_KG, SB_KG)
                vg = v_s[pl.ds(k0, npiece * BLOCK), :]
                c, o, z = list(st[:nc]), st[nc:2 * nc], st[2 * nc:]
                z_next = logits(jnp.maximum(gi - 1, 0))
                piece = lambda x, j: x[:, j * BLOCK:(j + 1) * BLOCK]
                a = [[None] * npiece for _ in chains]
                for j in reversed(range(npiece)):
                    zj = [piece(z[i], j) for i in chains]
                    lrem = [-_softplus(zj[i]) for i in chains]
                    if masks is not None:
                        lrem = [jnp.where(piece(masks[i // 2], j), lrem[i], 0.0) for i in chains]
                    incl = [_split_dot(lrem[i], t_ge) for i in chains]
                    for i in chains:
                        aij = jnp.exp(zj[i] + (c[i] + incl[i]))
                        if masks is not None:
                            aij = jnp.where(piece(masks[i // 2], j), aij, 0.0)
                        a[i][j] = aij.astype(BF16)
                        c[i] = c[i] + incl[i][:, 0:1]
                o = [o[i] + _dot(jnp.concatenate(a[i], axis=1), vg) for i in chains]
                return (*c, *o, *z_next)

            zc = [jnp.zeros((BLOCK, 1), F32)] * nc
            zo = [jnp.zeros((BLOCK, LANES), F32)] * nc
            masks = [(gd * SB_KG + colg) < (q0 + t * BLOCK + rowg) for t in range(SB_QT)]
            st = group(gd, (*zc, *zo, *logits(gd)), masks, (phase + 1) * SB_QT)
            st = lax.fori_loop(0, gd, lambda t, st: group(gd - 1 - t, st, None), st)
            for t in range(SB_QT):
                rows = pl.ds(q0 + t * BLOCK, BLOCK)
                o_ref[rows, :] = jnp.where(lo, st[nc + 2 * t], st[nc + 2 * t + 1])
                tot_ref[rows, :] = jnp.where(lo, st[2 * t], st[2 * t + 1])

        steps_per_group = nsub // SB_QT

        def per_group(g, carry):
            for phase in range(steps_per_group):
                qloop(g * steps_per_group + phase, phase)
            return carry

        lax.fori_loop(0, nq // nsub, per_group, 0)

    slab = lambda b0: pl.BlockSpec((s, LANES), lambda p: (0, b0 + p), pipeline_mode=pl.Buffered(1))
    w = LANES * n_slabs
    return pl.pallas_call(
        body, name=name, grid=(n_slabs,),
        in_specs=[slab(qb0), slab(kb0), slab(vb0)],
        out_specs=[pl.BlockSpec((s, LANES), lambda p: (0, p)), pl.BlockSpec((s, LANES), lambda p: (0, p))],
        out_shape=[jax.ShapeDtypeStruct((s, w), F32), jax.ShapeDtypeStruct((s, w), F32)],
        scratch_shapes=[pltpu.VMEM((s, LANES), BF16) for _ in range(4)],
        compiler_params=_params(("parallel",), VMEM_LIMIT),
    )(proj, proj, proj)


def _store_slabs(slabs, stage, dproj_ref, sems, blocks):
    s = stage.shape[1]

    def cast(c, carry):
        rows = pl.ds(pl.multiple_of(c * NORM_CH, NORM_CH), NORM_CH)
        for i, slab in enumerate(slabs):
            stage[i, rows, :] = slab[rows, :].astype(BF16)
        return carry

    lax.fori_loop(0, s // NORM_CH, cast, 0)
    copies = [pltpu.make_async_copy(stage.at[i], dproj_ref.at[:, pl.ds(pl.multiple_of(b * LANES, LANES), LANES)],
                                    sems.at[i]) for i, b in enumerate(blocks)]
    for cp in copies:
        cp.start()
    for cp in copies:
        cp.wait()


def sb_bwd(proj, qb0, kb0, vb0, n_slabs, dout, tot, dproj, *, name):
    s = proj.shape[0]
    nq = s // BLOCK
    nch = s // NORM_CH
    nsub = SB_KG // BLOCK
    scale = HEAD_DIM ** -0.5

    def body(q_ref, k_ref, v_ref, do_ref, tot_ref, dproj_in, dproj_ref,
             qlo_s, qhi_s, k_s, v_s, dlo_s, dhi_s, dq_ref, dk_ref, dv_ref, stage, sems):
        del dproj_in
        def prep(c, carry):
            rows = pl.ds(pl.multiple_of(c * NORM_CH, NORM_CH), NORM_CH)
            lo = _lo_mask((NORM_CH, LANES))
            qv = q_ref[rows, :] * scale
            dv = do_ref[rows, :]
            qlo_s[rows, :] = jnp.where(lo, qv, 0.0).astype(BF16)
            qhi_s[rows, :] = jnp.where(lo, 0.0, qv).astype(BF16)
            dlo_s[rows, :] = jnp.where(lo, dv, 0.0).astype(BF16)
            dhi_s[rows, :] = jnp.where(lo, 0.0, dv).astype(BF16)
            k_s[rows, :] = k_ref[rows, :].astype(BF16)
            v_s[rows, :] = v_ref[rows, :].astype(BF16)
            z = jnp.zeros((NORM_CH, LANES), F32)
            dk_ref[rows, :] = z
            dv_ref[rows, :] = z
            return carry

        lax.fori_loop(0, nch, prep, 0)

        row = lax.broadcasted_iota(I32, (BLOCK, BLOCK), 0)
        col = lax.broadcasted_iota(I32, (BLOCK, BLOCK), 1)
        lo = col < HEAD_DIM
        t_le = _twice(row <= col)
        rowg = lax.broadcasted_iota(I32, (BLOCK, SB_KG), 0)
        colg = lax.broadcasted_iota(I32, (BLOCK, SB_KG), 1)

        piece = lambda x, j: x[:, j * BLOCK:(j + 1) * BLOCK]
        chains = range(2 * SB_QT)
        nc = len(chains)

        def prefixes(x):
            return [[_split_dot(piece(x[i], j), t_le) for j in range(x[i].shape[1] // BLOCK)] for i in chains]

        def chain(pre, run, total=None):
            out = []
            for pj in pre:
                out.append(run + pj if total is None else total - run - pj)
                run = run + pj[:, BLOCK - 1:BLOCK]
            return jnp.concatenate(out, axis=1), run

        def qloop(qs, phase):
            q0 = pl.multiple_of(qs * (SB_QT * BLOCK), SB_QT * BLOCK)
            tile = lambda ref, i: ref[pl.ds(q0 + (i // 2) * BLOCK, BLOCK), :]
            qh = [tile((qlo_s, qhi_s)[i % 2], i) for i in chains]
            doh = [tile((dlo_s, dhi_s)[i % 2], i) for i in chains]
            tots = [tile(tot_ref, i)[:, (i % 2) * HEAD_DIM:(i % 2) * HEAD_DIM + 1] for i in chains]
            gd = (qs * SB_QT) // nsub

            def logits(gi):
                kg = k_s[pl.ds(pl.multiple_of(gi * SB_KG, SB_KG), SB_KG), :]
                return [_dot_nt(qh[i], kg) for i in chains]

            def group(gi, st, masks, npiece=nsub):
                k0 = pl.multiple_of(gi * SB_KG, SB_KG)
                wide = npiece * BLOCK
                kg, vg = k_s[pl.ds(k0, wide), :], v_s[pl.ds(k0, wide), :]
                cp, cg, dq = list(st[:nc]), list(st[nc:2 * nc]), st[2 * nc:2 * nc + SB_QT]
                z = [zi[:, :wide] for zi in st[2 * nc + SB_QT:]]
                masked = lambda x, i: x if masks is None else jnp.where(masks[i // 2][:, :wide], x, 0.0)
                z_next = logits(jnp.minimum(gi + 1, gd))
                da = [_dot_nt(doh[i], vg) for i in chains]
                sp = [_softplus(z[i]) for i in chains]
                lrem = [masked(-sp[i], i) for i in chains]
                pre = prefixes(lrem)
                e, a, g = [], [], []
                for i in chains:
                    suffix, cp[i] = chain(pre[i], cp[i], tots[i])
                    e.append(z[i] - sp[i])
                    a.append(masked(jnp.exp(e[i] + suffix), i))
                    g.append(a[i] * da[i])
                gpre = prefixes(g)
                dz = []
                for i in chains:
                    ginc, cg[i] = chain(gpre[i], cg[i])
                    dz.append(masked(g[i] - jnp.exp(e[i]) * ginc, i).astype(BF16))
                ab = [a[i].astype(BF16) for i in chains]
                dq = [dq[t] + jnp.where(lo, _dot(dz[2 * t], kg), _dot(dz[2 * t + 1], kg)) for t in range(SB_QT)]
                rows_of = lambda x: jnp.concatenate(x, axis=0)
                dk_ref[pl.ds(k0, wide), :] += _dot_tn(rows_of(dz), rows_of(qh))
                dv_ref[pl.ds(k0, wide), :] += _dot_tn(rows_of(ab), rows_of(doh))
                return (*cp, *cg, *dq, *z_next)

            zc = [jnp.zeros((BLOCK, 1), F32)] * (2 * nc)
            zq = [jnp.zeros((BLOCK, LANES), F32)] * SB_QT
            st = lax.fori_loop(0, gd, lambda gi, st: group(gi, st, None), (*zc, *zq, *logits(0)))
            st = group(gd, st, [(gd * SB_KG + colg) < (q0 + t * BLOCK + rowg) for t in range(SB_QT)],
                       (phase + 1) * SB_QT)
            for t in range(SB_QT):
                dq_ref[pl.ds(q0 + t * BLOCK, BLOCK), :] = st[2 * nc + t] * scale

        steps_per_group = nsub // SB_QT

        def per_group(g, carry):
            for phase in range(steps_per_group):
                qloop(g * steps_per_group + phase, phase)
            return carry

        lax.fori_loop(0, nq // nsub, per_group, 0)
        p = pl.program_id(0)
        _store_slabs((dq_ref, dk_ref, dv_ref), stage, dproj_ref, sems, (qb0 + p, kb0 + p, vb0 + p))

    slab = lambda b0: pl.BlockSpec((s, LANES), lambda p: (0, b0 + p), pipeline_mode=pl.Buffered(1))
    own = pl.BlockSpec((s, LANES), lambda p: (0, p), pipeline_mode=pl.Buffered(1))
    held = pl.BlockSpec(memory_space=pl.ANY)
    return pl.pallas_call(
        body, name=name, grid=(n_slabs,),
        in_specs=[slab(qb0), slab(kb0), slab(vb0), own, own, held],
        out_specs=held, out_shape=jax.ShapeDtypeStruct(dproj.shape, dproj.dtype),
        input_output_aliases={5: 0},
        scratch_shapes=[pltpu.VMEM((s, LANES), BF16) for _ in range(6)]
        + [pltpu.VMEM((s, LANES), F32) for _ in range(3)]
        + [pltpu.VMEM((3, s, LANES), BF16), pltpu.SemaphoreType.DMA((3,))],
        compiler_params=_params(("arbitrary",), VMEM_LIMIT),
    )(proj, proj, proj, dout, tot, dproj)


def _place():
    x, y, c = lax.axis_index("x"), lax.axis_index("y"), lax.axis_index("c")
    return x, y, c


def gather_small(v, *, name):
    m_per, n = v.shape

    def body(x_ref, out_ref, send_sems, recv_sems, local_sem):
        x, y, c = _place()
        me, sibling = (x, y, c), (x, y, 1 - c)
        chips = [(1 - x, y), (x, 1 - y), (1 - x, 1 - y)]

        def rows(px, py, pc):
            return out_ref.at[pl.ds((4 * px + 2 * py + pc) * m_per, m_per), :]

        def copy(k, block, to, src=None):
            return pltpu.make_async_remote_copy(
                src_ref=rows(*block) if src is None else src, dst_ref=rows(*block),
                send_sem=send_sems.at[k], recv_sem=recv_sems.at[k], device_id=to, device_id_type=MESH)

        mine = pltpu.make_async_copy(x_ref, rows(*me), local_sem)
        mine.start()
        first = [copy(0, me, sibling, src=x_ref)]
        first += [copy(1 + j, me, (*chip, c), src=x_ref) for j, chip in enumerate(chips)]
        for cp in first:
            cp.start()
        passed = [copy(4 + j, (*chip, c), sibling) for j, chip in enumerate(chips)]
        for j, chip in enumerate(chips):
            copy(1 + j, (*chip, c), me).wait_recv()
            passed[j].start()
        copy(0, sibling, me).wait_recv()
        for j, chip in enumerate(chips):
            copy(4 + j, (*chip, 1 - c), me).wait_recv()
        for cp in first + passed:
            cp.wait_send()
        mine.wait()

    return pl.pallas_call(
        body, name=name,
        out_shape=jax.ShapeDtypeStruct((N_DEV * m_per, n), v.dtype),
        in_specs=[pl.BlockSpec(memory_space=pltpu.VMEM)],
        out_specs=pl.BlockSpec(memory_space=pltpu.VMEM),
        scratch_shapes=[pltpu.SemaphoreType.DMA((7,)), pltpu.SemaphoreType.DMA((7,)), pltpu.SemaphoreType.DMA],
        compiler_params=_params(None, VMEM_LIMIT),
    )(v)


_HBM = pl.BlockSpec(memory_space=pltpu.HBM)
_SEM = pl.BlockSpec(memory_space=pltpu.SEMAPHORE)
_EFFECT = pltpu.SideEffectType.DATAFLOW_SIDE_EFFECTING


def _peer_copies(src_refs, land_refs, send_sems, recv_sems, per_dest):
    x, y, c = _place()
    me = 4 * x + 2 * y + c
    copies = []
    for src, land, ssem, rsem in zip(src_refs, land_refs, send_sems, recv_sems):
        for k in (1, 2, 4, 3, 5, 6, 7):
            px, py, pc = x ^ (k >> 2 & 1), y ^ (k >> 1 & 1), c ^ (k & 1)
            copies.append(pltpu.make_async_remote_copy(
                src_ref=src.at[4 * px + 2 * py + pc] if per_dest else src, dst_ref=land.at[me],
                send_sem=ssem.at[k - 1], recv_sem=rsem.at[k - 1], device_id=(px, py, pc), device_id_type=MESH))
    return copies


def _own_copies(src_refs, land_refs, send_sems, per_dest):
    x, y, c = _place()
    me = 4 * x + 2 * y + c
    return [pltpu.make_async_copy(src.at[me] if per_dest else src, land.at[me], ssem.at[7])
            for src, land, ssem in zip(src_refs, land_refs, send_sems)]


def exchange_start(srcs, per_dest, *, name):
    n = len(srcs)
    lands = [lax.empty(a.shape if per_dest else (N_DEV,) + a.shape, a.dtype) for a in srcs]

    def body(*refs):
        src_refs, land_refs = refs[:n], refs[n:2 * n]
        send_sems, recv_sems = refs[2 * n:3 * n], refs[3 * n:4 * n]
        token = refs[-1]
        for cp in _peer_copies(src_refs, land_refs, send_sems, recv_sems, per_dest):
            cp.start()
        for cp in _own_copies(src_refs, land_refs, send_sems, per_dest):
            cp.start()
        token[...] = jnp.zeros_like(token)

    hbm = lambda a: pltpu.HBM(a.shape, a.dtype)
    res = pl.pallas_call(
        body, name=name,
        out_shape=(*[pltpu.SemaphoreType.DMA((8,))] * n, *[pltpu.SemaphoreType.DMA((7,))] * n,
                   *[hbm(a) for a in srcs], *[hbm(a) for a in lands], jax.ShapeDtypeStruct((8, LANES), F32)),
        in_specs=[_HBM] * (2 * n),
        out_specs=(*[_SEM] * (2 * n), *[_HBM] * (2 * n), pl.BlockSpec(memory_space=pltpu.VMEM)),
        input_output_aliases={i: 2 * n + i for i in range(2 * n)},
        compiler_params=pltpu.CompilerParams(has_side_effects=_EFFECT),
    )(*[pltpu.with_memory_space_constraint(a, pltpu.HBM) for a in (*srcs, *lands)])
    handles = [(res[a], res[n + a], res[2 * n + a], res[3 * n + a]) for a in range(n)]
    return handles, res[-1]


def exchange_wait(handles, per_dest, after, *, name):
    n = len(handles)

    def body(*refs):
        src_refs, land_refs = refs[:n], refs[n:2 * n]
        send_sems, recv_sems = refs[2 * n:3 * n], refs[3 * n:4 * n]
        for cp in _peer_copies(src_refs, land_refs, send_sems, recv_sems, per_dest):
            cp.wait_send()
            cp.wait_recv()
        for cp in _own_copies(src_refs, land_refs, send_sems, per_dest):
            cp.wait()

    srcs, lands = [h[2] for h in handles], [h[3] for h in handles]
    hbm = lambda a: pltpu.HBM(a.shape, a.dtype)
    res = pl.pallas_call(
        body, name=name,
        out_shape=(*[hbm(a) for a in srcs], *[hbm(a) for a in lands]),
        in_specs=[*[_HBM] * (2 * n), *[_SEM] * (2 * n), pl.BlockSpec(memory_space=pl.ANY)],
        out_specs=tuple([_HBM] * (2 * n)),
        input_output_aliases={i: i for i in range(2 * n)},
        compiler_params=pltpu.CompilerParams(has_side_effects=_EFFECT),
    )(*srcs, *lands, *[h[0] for h in handles], *[h[1] for h in handles], after)
    return res[n:]


def _adamw_math(w, g, m, v):
    m = ADAM_B1 * m + (1.0 - ADAM_B1) * g
    v = ADAM_B2 * v + (1.0 - ADAM_B2) * (g * g)
    m_hat = m / (1.0 - ADAM_B1 ** ADAM_STEP)
    v_hat = v / (1.0 - ADAM_B2 ** ADAM_STEP)
    delta = -ADAM_LR * (m_hat / (jnp.sqrt(v_hat) + ADAM_EPS) + ADAM_WD * w)
    return delta, m, v


def adamw_parts(parts, w, m, v, layer, outs, *, name):
    depth, r, cdim = w.shape
    n_parts = parts.shape[0]
    tr = _pick(r, [t for t in (512, 256, 128, 112, 64, 32, 16) if t * cdim <= 256 * 1024])

    def body(p_ref, w_ref, m_ref, v_ref, g0, d0, nm0, nv0, g_ref, d_ref, nm_ref, nv_ref):
        g = p_ref[0].astype(F32)
        for q in range(1, n_parts):
            g = g + p_ref[q].astype(F32)
        delta, nm, nv = _adamw_math(w_ref[...], g, m_ref[...], v_ref[...])
        g_ref[...], d_ref[...], nm_ref[...], nv_ref[...] = g, delta, nm, nv

    t = pl.BlockSpec((None, tr, cdim), lambda i: (layer, i, 0))
    held = pl.BlockSpec(memory_space=pl.ANY)
    return pl.pallas_call(
        body, name=name, grid=(r // tr,),
        in_specs=[pl.BlockSpec((n_parts, tr, cdim), lambda i: (0, i, 0)), t, t, t, held, held, held, held],
        out_specs=[t, t, t, t],
        out_shape=[jax.ShapeDtypeStruct((depth, r, cdim), F32)] * 4,
        input_output_aliases={4: 0, 5: 1, 6: 2, 7: 3},
        compiler_params=_params(("parallel",), VMEM_LIMIT),
    )(parts, w, m, v, *outs)


def sum_devices(gathered, *, name):
    m_rows = gathered.shape[1]

    def body(ga_ref, g_ref):
        g = ga_ref[0]
        for dev in range(1, N_DEV):
            g = g + ga_ref[dev]
        g_ref[...] = g

    return pl.pallas_call(
        body, name=name, out_shape=jax.ShapeDtypeStruct((m_rows, LANES), F32),
        compiler_params=_params(None, VMEM_LIMIT),
    )(gathered)


def adamw_small(g, w, m, v, *, name):
    m_rows = w.shape[0]

    def body(g_ref, w_ref, m_ref, v_ref, d_ref, nm_ref, nv_ref):
        d_ref[...], nm_ref[...], nv_ref[...] = _adamw_math(w_ref[...], g_ref[...], m_ref[...], v_ref[...])

    return pl.pallas_call(
        body, name=name, out_shape=[jax.ShapeDtypeStruct((m_rows, LANES), F32)] * 3,
        compiler_params=_params(None, VMEM_LIMIT),
    )(g, w, m, v)


def _t5_bucket(dist):
    max_exact = N_BUCKETS // 2
    d = jnp.maximum(dist, 0)
    large = max_exact + (jnp.log(jnp.maximum(d, 1).astype(F32) / max_exact)
                         / math.log(T5_MAX_DIST / max_exact) * (N_BUCKETS - max_exact)).astype(I32)
    large = jnp.minimum(large, N_BUCKETS - 1)
    return jnp.where(d < max_exact, d, large)


def _rel():
    return jnp.arange(BLOCK)[:, None] + BLOCK - jnp.arange(2 * BLOCK)[None, :]


def _band_bias(table, dils, max_dists):
    rel = _rel()
    biases, buckets = [], []
    for d, md in zip(dils, max_dists):
        bk = _t5_bucket(rel * d)
        vis = (rel >= 0) & (rel <= md)
        looked_up = jnp.zeros((table.shape[1],) + rel.shape, F32)
        for b in range(N_BUCKETS):
            looked_up = jnp.where((bk == b)[None], table[b][:, None, None], looked_up)
        with_prev = jnp.where(vis[None], looked_up, NEG_INF)
        first = jnp.arange(2 * BLOCK)[None, None, :] >= BLOCK
        biases.append(jnp.stack([with_prev, jnp.where(first, with_prev, NEG_INF)]))
        buckets.append(bk.astype(I32))
    return jnp.stack(biases), jnp.stack(buckets)


def _pack(pieces, rows):
    flat = jnp.concatenate([p.reshape(-1) for p in pieces])
    return jnp.pad(flat, (0, rows * LANES - flat.shape[0])).reshape(rows, LANES)


def _unpack(packed, shapes):
    flat = packed.reshape(-1)
    out, off = [], 0
    for sh in shapes:
        n = math.prod(sh)
        out.append(flat[off:off + n].reshape(sh))
        off += n
    return out


def _tile2(g):
    return jnp.concatenate([g, g])


def kernel(x, attn_norm, w_in, a_q_gain, a_k_gain, a_sinks, c_q_gain, c_k_gain, rel_bias_table, mix_out_gain, w_out, ffn_norm, w_up, conv_w, conv_b, w_down, loss_target, m_attn_norm, m_w_in, m_a_q_gain, m_a_k_gain, m_a_sinks, m_c_q_gain, m_c_k_gain, m_rel_bias_table, m_mix_out_gain, m_w_out, m_ffn_norm, m_w_up, m_conv_w, m_conv_b, m_w_down, v_attn_norm, v_w_in, v_a_q_gain, v_a_k_gain, v_a_sinks, v_c_q_gain, v_c_k_gain, v_rel_bias_table, v_mix_out_gain, v_w_out, v_ffn_norm, v_w_up, v_conv_w, v_conv_b, v_w_down):
    depth, d_model, in_shard = w_in.shape
    ff2_shard = w_up.shape[2]
    s = x.shape[1]
    in_width, ff2 = N_DEV * in_shard, N_DEV * ff2_shard
    n_heads = d_model // HEAD_DIM
    ha, hb, hc = n_heads // 4, n_heads // 4, n_heads // 2
    sa, sb, sc = ha // 2, hb // 2, hc // 2
    kv_a = ha // 4
    assert kv_a == 2 and BLOCK == LANES
    cb_aq, cb_ak, cb_av = 0, sa, sa + 1
    cb_bq = sa + 2
    cb_bk, cb_bv = cb_bq + sb, cb_bq + 2 * sb
    cb_cq = cb_bq + 3 * sb
    cb_ck, cb_cv = cb_cq + sc, cb_cq + 2 * sc
    assert (cb_cv + sc) * LANES == in_width
    dev = 4 * lax.axis_index("x") + 2 * lax.axis_index("y") + lax.axis_index("c")

    per_array = 3
    wnames = ("w_in", "w_out", "w_up", "w_down", "conv_w")
    sent = dict(w_in=lambda w: w.T, w_up=lambda w: w.T, w_out=lambda w: w, w_down=lambda w: w, conv_w=lambda w: w)
    rows = lambda g: g.reshape(N_DEV * g.shape[1], g.shape[2])
    whole = dict(w_in=rows, w_up=rows, w_out=rows, w_down=rows,
                 conv_w=lambda g: jnp.transpose(g, (1, 0, 2)).reshape(g.shape[1], N_DEV * g.shape[2]))
    gathers = {}
    token = jnp.zeros((8, LANES), F32)
    for l in range(depth):
        for gi, group in enumerate([[n] for n in wnames] if l < per_array else [wnames]):
            srcs = [(sent[n](dict(w_in=w_in, w_out=w_out, w_up=w_up, w_down=w_down, conv_w=conv_w)[n][l])
                     + token[0, 0]).astype(F32 if n == "conv_w" else BF16) for n in group]
            handles, token = exchange_start(srcs, False, name=f"gather_start_{l}_{gi}")
            gathers.update({(l, n): h for n, h in zip(group, handles)})

    def gathered(l, names, after):
        landed = exchange_wait([gathers[l, n] for n in names], False, after,
                               name=f"gather_wait_{l}_{wnames.index(names[0])}")
        return {n: whole[n](g) for n, g in zip(names, landed)}

    bias_a, buckets_a = _band_bias(rel_bias_table[:, :ha], (1,), (WINDOW_A - 1,))
    bias_c, buckets_c = _band_bias(rel_bias_table[:, ha:], DILATIONS, (BLOCK,) * len(DILATIONS))

    xs = x[0]
    saved = []
    wi, wo, wu, wd, cw = ([None] * depth for _ in range(5))
    for l in range(depth):
        if l < per_array:
            need = lambda n, after, l=l: gathered(l, (n,), after)[n]
        else:
            layer_w = gathered(l, wnames, xs)
            need = lambda n, after: layer_w[n]
        wi[l] = need("w_in", token if l == 0 else xs)
        h1 = rmsnorm_fwd(xs, attn_norm[l], name="attn_norm_fwd")
        proj = matmul(h1, wi[l], trans_b=True, name="in_proj")
        sinks = jnp.repeat(a_sinks[l], HEAD_DIM).reshape(sa, 1, LANES)
        gaq, gak = _tile2(a_q_gain[l]), _tile2(a_k_gain[l])
        gcq, gck = _tile2(c_q_gain[l]), _tile2(c_k_gain[l])
        out_a, lse_a = banded_fwd(proj, cb_aq, cb_ak, cb_av, sa, gaq, gak, bias_a, (1,), sinks, True, name="swa_fwd")
        out_b, tot_b = sb_fwd(proj, cb_bq, cb_bk, cb_bv, sb, name="stick_fwd")
        out_c, lse_c = banded_fwd(proj, cb_cq, cb_ck, cb_cv, sc, gcq, gck, bias_c, DILATIONS, None, False,
                                  name="dilated_fwd")
        mix = mixnorm_fwd([out_a, out_b, out_c], mix_out_gain[l], name="mix_norm_fwd")
        wo[l] = need("w_out", mix)
        x_mid = matmul(mix, wo[l], res=xs, name="out_proj")
        h2 = rmsnorm_fwd(x_mid, ffn_norm[l], name="ffn_norm_fwd")
        wu[l] = need("w_up", h2)
        p = matmul(h2, wu[l], trans_b=True, name="up_proj")
        cw[l] = need("conv_w", p)
        act = ffn_act_fwd(p, cw[l], conv_b[l], name="ffn_act_fwd")
        wd[l] = need("w_down", act)
        x_out = matmul(act, wd[l], res=x_mid, name="down_proj")
        saved.append(dict(x_in=xs, h1=h1, proj=proj, out_a=out_a, lse_a=lse_a, out_b=out_b, tot_b=tot_b,
                          out_c=out_c, lse_c=lse_c, mix=mix, x_mid=x_mid, h2=h2, p=p, act=act,
                          sinks=sinks, gains=(gaq, gak, gcq, gck)))
        xs = x_out

    dx, dx_b, loss_part = loss_head(xs, loss_target[0], name="loss_head")

    small = {k: [None] * depth for k in ("attn_norm", "a_q_gain", "a_k_gain", "a_sinks", "c_q_gain", "c_k_gain",
                                         "mix_out_gain", "ffn_norm", "conv_w", "conv_b")}
    big = {k: [None] * depth for k in ("w_in", "w_out", "w_up", "w_down")}
    dbias_a = dbias_c = None
    scatters = {}
    token = jnp.zeros((8, LANES), F32)
    names_big = ("w_in", "w_out", "w_up", "w_down")

    def scatter(l, names):
        parts = [big[n][l] for n in names]
        handles, tok = exchange_start(parts, True, name=f"scatter_start_{l}_{names_big.index(names[0])}")
        scatters.update({(l, n): h for n, h in zip(names, handles)})
        return tok

    by_rows = lambda a: a.reshape(N_DEV, a.shape[0] // N_DEV, a.shape[1])
    for l in reversed(range(depth)):
        each = l == 0
        sv = saved[l]
        gaq, gak, gcq, gck = sv["gains"]
        da = matmul(dx_b, wd[l], trans_b=True, name="down_proj_dx")
        big["w_down"][l] = by_rows(matmul(sv["act"], dx_b, trans_a=True, out_dtype=BF16, name="down_proj_dw"))
        if each:
            token = scatter(l, ("w_down",))
        dp, small["conv_w"][l], small["conv_b"][l] = ffn_act_bwd(da, sv["p"], cw[l], conv_b[l] + token[0, 0],
                                                                 name="ffn_act_bwd")
        dh2 = matmul(dp, wu[l], name="up_proj_dx")
        big["w_up"][l] = matmul(sv["h2"], dp, trans_a=True, out_dtype=BF16, col_blocks=N_DEV, name="up_proj_dw")
        if each:
            token = scatter(l, ("w_up",))
        dx_mid, dx_mid_b, small["ffn_norm"][l] = rmsnorm_bwd(dh2, sv["x_mid"], ffn_norm[l] + token[0, 0], dx,
                                                   name="ffn_norm_bwd")
        dmix = matmul(dx_mid_b, wo[l], trans_b=True, name="out_proj_dx")
        big["w_out"][l] = by_rows(matmul(sv["mix"], dx_mid_b, trans_a=True, out_dtype=BF16, name="out_proj_dw"))
        if each:
            token = scatter(l, ("w_out",))
        (d_oa, d_ob, d_oc), small["mix_out_gain"][l] = mixnorm_bwd(
            dmix, [sv["out_a"], sv["out_b"], sv["out_c"]], mix_out_gain[l] + token[0, 0], name="mix_norm_bwd")
        dproj = lax.empty((s, in_width), BF16)
        dproj, db_a, dgq_a, dgk_a, dsink = banded_bwd(
            sv["proj"], cb_aq, cb_ak, cb_av, sa, gaq, gak, bias_a, (1,), sv["sinks"], True,
            d_oa, sv["out_a"], sv["lse_a"], dproj, name="swa_bwd")
        dproj = sb_bwd(sv["proj"], cb_bq, cb_bk, cb_bv, sb, d_ob, sv["tot_b"], dproj, name="stick_bwd")
        dproj, db_c, dgq_c, dgk_c = banded_bwd(
            sv["proj"], cb_cq, cb_ck, cb_cv, sc, gcq, gck, bias_c, DILATIONS, None, False,
            d_oc, sv["out_c"], sv["lse_c"], dproj, name="dilated_bwd")
        fold = lambda g: g.reshape(-1, HEAD_DIM).sum(axis=0)
        small["a_q_gain"][l], small["a_k_gain"][l] = fold(dgq_a), fold(dgk_a)
        small["c_q_gain"][l], small["c_k_gain"][l] = fold(dgq_c), fold(dgk_c)
        small["a_sinks"][l] = dsink[:, ::HEAD_DIM].reshape(-1)
        dbias_a = db_a if dbias_a is None else dbias_a + db_a
        dbias_c = db_c if dbias_c is None else dbias_c + db_c
        big["w_in"][l] = by_rows(matmul(dproj, sv["h1"], trans_a=True, out_dtype=BF16, name="in_proj_dw"))
        if not each:
            token = scatter(l, names_big)
        dh1 = matmul(dproj, wi[l], name="in_proj_dx")
        dx, dx_b, small["attn_norm"][l] = rmsnorm_bwd(dh1, sv["x_in"], attn_norm[l] + token[0, 0], dx_mid,
                                                name="attn_norm_bwd")

    dtable = jnp.concatenate([bias_bwd(dbias_a, buckets_a, name="swa_bias_bwd"),
                              bias_bwd(dbias_c, buckets_c, name="dilated_bias_bwd")], axis=1)

    order = ("attn_norm", "a_q_gain", "a_k_gain", "a_sinks", "c_q_gain", "c_k_gain", "rel_bias_table",
             "mix_out_gain", "ffn_norm", "conv_w", "conv_b")
    partial = {k: jnp.stack(v) for k, v in small.items()}
    partial["rel_bias_table"] = dtable
    pieces = [partial[k] for k in order] + [loss_part.reshape(1)]
    n_small = sum(math.prod(pc.shape) for pc in pieces)
    rows = -(-n_small // (8 * LANES)) * 8
    gathered = gather_small(_pack(pieces, rows), name="gather_small_grads")
    big["w_in"][0], gathered = lax.optimization_barrier((big["w_in"][0], gathered))
    scatter(0, ("w_in",))
    summed = _unpack(sum_devices(gathered.reshape(N_DEV, rows, LANES), name="sum_small_grads"),
                     [pc.shape for pc in pieces])
    g_small = dict(zip(order, summed[:-1]))
    loss = summed[-1][0]
    g_small["conv_w"] = lax.dynamic_slice_in_dim(g_small["conv_w"], dev * ff2_shard, ff2_shard, axis=2)

    w_small = dict(attn_norm=attn_norm, a_q_gain=a_q_gain, a_k_gain=a_k_gain, a_sinks=a_sinks, c_q_gain=c_q_gain,
                   c_k_gain=c_k_gain, rel_bias_table=rel_bias_table, mix_out_gain=mix_out_gain, ffn_norm=ffn_norm,
                   conv_w=conv_w, conv_b=conv_b)
    m_small = dict(attn_norm=m_attn_norm, a_q_gain=m_a_q_gain, a_k_gain=m_a_k_gain, a_sinks=m_a_sinks,
                   c_q_gain=m_c_q_gain, c_k_gain=m_c_k_gain, rel_bias_table=m_rel_bias_table,
                   mix_out_gain=m_mix_out_gain, ffn_norm=m_ffn_norm, conv_w=m_conv_w, conv_b=m_conv_b)
    v_small = dict(attn_norm=v_attn_norm, a_q_gain=v_a_q_gain, a_k_gain=v_a_k_gain, a_sinks=v_a_sinks,
                   c_q_gain=v_c_q_gain, c_k_gain=v_c_k_gain, rel_bias_table=v_rel_bias_table,
                   mix_out_gain=v_mix_out_gain, ffn_norm=v_ffn_norm, conv_w=v_conv_w, conv_b=v_conv_b)
    shapes = [w_small[k].shape for k in order]
    n_upd = sum(math.prod(sh) for sh in shapes)
    urows = -(-n_upd // (8 * LANES)) * 8
    packs = [_pack([d[k] for k in order], urows) for d in (g_small, w_small, m_small, v_small)]
    upd = adamw_small(*packs, name="adamw_small")
    delta_s, newm_s, newv_s = [dict(zip(order, _unpack(u, shapes))) for u in upd]

    flip = lambda t: jnp.swapaxes(t, 1, 2)
    w_big = dict(w_in=(flip(w_in), flip(m_w_in), flip(v_w_in)), w_out=(w_out, m_w_out, v_w_out),
                 w_up=(w_up, m_w_up, v_w_up), w_down=(w_down, m_w_down, v_w_down))
    results = {k: [lax.empty(w_big[k][0].shape, F32) for _ in range(4)] for k in names_big}
    after = upd[0]
    for l, names in [(l, names_big) for l in reversed(range(1, depth))] + [(0, names_big[1:]), (0, names_big[:1])]:
        landed = exchange_wait([scatters[l, n] for n in names], True, after,
                               name=f"scatter_wait_{l}_{names_big.index(names[0])}")
        for k, parts in zip(names, landed):
            results[k] = adamw_parts(parts, *w_big[k], l, results[k], name="adamw_large")
        after = results[names[-1]][0]
    results["w_in"] = [flip(t) for t in results["w_in"]]
    g_big, delta_b, newm_b, newv_b = [{k: results[k][i] for k in names_big} for i in range(4)]

    all_names = ("attn_norm", "w_in", "a_q_gain", "a_k_gain", "a_sinks", "c_q_gain", "c_k_gain", "rel_bias_table",
                 "mix_out_gain", "w_out", "ffn_norm", "w_up", "conv_w", "conv_b", "w_down")
    pick = lambda sm, bg: [bg[k] if k in bg else sm[k] for k in all_names]
    return (loss, dx[None], *pick(g_small, g_big), *pick(delta_s, delta_b), *pick(newm_s, newm_b),
            *pick(newv_s, newv_b))
```

```python
import functools
import math

import jax
import jax.numpy as jnp
from jax import lax
from jax.experimental import pallas as pl
from jax.experimental.pallas import tpu as pltpu

F32, BF16, I32 = jnp.float32, jnp.bfloat16, jnp.int32
MESH = pl.DeviceIdType.MESH

HEAD_DIM = 64
LANES = 128
BLOCK = 128
EPS = 1e-6
NEG_INF = -1e30
N_BUCKETS = 32
T5_MAX_DIST = 2048
WINDOW_A = 128
DILATIONS = (1, 4, 16)
N_DEV = 8
VMEM_LIMIT = 56 * 1024 * 1024
MATMUL_VMEM = 46 * 1024 * 1024

ADAM_LR, ADAM_B1, ADAM_B2, ADAM_EPS, ADAM_WD, ADAM_STEP = 0.001, 0.9, 0.999, 1e-08, 0.01, 10


def _params(sem=None, vmem=None):
    return pltpu.CompilerParams(dimension_semantics=sem, vmem_limit_bytes=vmem)


def _pick(n, cands):
    for c in cands:
        if n % c == 0:
            return c
    raise ValueError(f"no tile for {n}")


def _dot(a, b):
    return lax.dot_general(a, b, (((1,), (0,)), ((), ())), preferred_element_type=F32)


def _dot_nt(a, b):
    return lax.dot_general(a, b, (((1,), (1,)), ((), ())), preferred_element_type=F32)


def _dot_tn(a, b):
    return lax.dot_general(a, b, (((0,), (0,)), ((), ())), preferred_element_type=F32)


def matmul(a, b, *, trans_a=False, trans_b=False, out_dtype=F32, res=None, col_blocks=None, name):
    a_halves, b_halves = a.ndim == 3, b.ndim == 3
    assert not (a_halves and trans_a) and not (b_halves and trans_b)
    m, k = (a.shape[1], 2 * a.shape[2]) if a_halves else (a.shape[1], a.shape[0]) if trans_a else a.shape
    n = 2 * b.shape[2] if b_halves else b.shape[0] if trans_b else b.shape[1]
    k_unit, n_unit = (k // 2 if a_halves else k), (n // 2 if b_halves else n)
    tm = _pick(m, (1408, 1024, 896, 512, 256))
    tn_cands = ((n // col_blocks,) if col_blocks
                else tuple(t for t in (1024, 1408, 768, 512, 256, 128) if n_unit % t == 0))

    def footprint(tk, tn):
        tiles = 2 * (tm * tk * a.dtype.itemsize + tk * tn * b.dtype.itemsize)
        return tiles + tm * tn * (4 + 2 * jnp.dtype(out_dtype).itemsize + (8 if res is not None else 0))

    tk, tn = next((tk, tn) for tk in (5376, 4096, 2816, 2048, 1792, 1024, 768, 512, 256) if k_unit % tk == 0
                  for tn in tn_cands if footprint(tk, tn) <= MATMUL_VMEM)
    nk = k // tk
    nk_half, nj_half = k_unit // tk, n_unit // tn
    dn = (((0 if trans_a else 1,), (1 if trans_b else 0,)), ((), ()))

    def body(*refs):
        if res is None:
            a_ref, b_ref, o_ref, acc = refs
        else:
            a_ref, b_ref, r_ref, o_ref, acc = refs
        kk = pl.program_id(2)

        @pl.when(kk == 0)
        def _():
            acc[...] = jnp.zeros_like(acc)

        acc[...] += lax.dot_general(a_ref[...].astype(BF16), b_ref[...].astype(BF16), dn,
                                    preferred_element_type=F32)

        @pl.when(kk == nk - 1)
        def _():
            r = acc[...]
            if res is not None:
                r = r_ref[...] + r
            o_ref[...] = r.astype(out_dtype)

    b_spec = (pl.BlockSpec((tn, tk), lambda i, j, kk: (j, kk)) if trans_b
              else pl.BlockSpec((None, tk, tn), lambda i, j, kk: (j // nj_half, kk, j % nj_half)) if b_halves
              else pl.BlockSpec((tk, tn), lambda i, j, kk: (kk, j)))
    a_spec = (pl.BlockSpec((tk, tm), lambda i, j, kk: (kk, i)) if trans_a
              else pl.BlockSpec((None, tm, tk), lambda i, j, kk: (kk // nk_half, i, kk % nk_half)) if a_halves
              else pl.BlockSpec((tm, tk), lambda i, j, kk: (i, kk)))
    in_specs = [a_spec, b_spec]
    args = [a, b]
    if res is not None:
        in_specs.append(pl.BlockSpec((tm, tn), lambda i, j, kk: (i, j)))
        args.append(res)
    if col_blocks:
        out_spec = pl.BlockSpec((None, tm, tn), lambda i, j, kk: (j, i, 0))
        out_shape = jax.ShapeDtypeStruct((col_blocks, m, tn), out_dtype)
    else:
        out_spec = pl.BlockSpec((tm, tn), lambda i, j, kk: (i, j))
        out_shape = jax.ShapeDtypeStruct((m, n), out_dtype)
    return pl.pallas_call(
        body, name=name, grid=(m // tm, n // tn, nk),
        in_specs=in_specs, out_specs=out_spec, out_shape=out_shape,
        scratch_shapes=[pltpu.VMEM((tm, tn), F32)],
        compiler_params=_params(("parallel", "parallel", "arbitrary"), VMEM_LIMIT),
    )(*args)


def rmsnorm_fwd(x, g, *, name):
    s, d = x.shape
    tm = 512

    def body(x_ref, g_ref, o_ref):
        xv = x_ref[...]
        r = lax.rsqrt(jnp.mean(xv * xv, axis=-1, keepdims=True) + EPS)
        o_ref[...] = (xv * r * g_ref[...]).astype(BF16)

    return pl.pallas_call(
        body, name=name, grid=(s // tm,),
        in_specs=[pl.BlockSpec((tm, d), lambda i: (i, 0)), pl.BlockSpec((1, d), lambda i: (0, 0))],
        out_specs=pl.BlockSpec((tm, d), lambda i: (i, 0)),
        out_shape=jax.ShapeDtypeStruct((s, d), BF16),
        compiler_params=_params(("parallel",)),
    )(x, g.reshape(1, d))


def rmsnorm_bwd(dh, x, g, dres, *, name):
    s, d = x.shape
    tm = 256

    def body(dh_ref, x_ref, g_ref, dres_ref, dx_ref, dxb_ref, dg_ref):
        @pl.when(pl.program_id(0) == 0)
        def _():
            dg_ref[...] = jnp.zeros_like(dg_ref)

        xv, dhv = x_ref[...], dh_ref[...]
        r = lax.rsqrt(jnp.mean(xv * xv, axis=-1, keepdims=True) + EPS)
        gd = dhv * g_ref[...]
        dot = jnp.mean(gd * xv, axis=-1, keepdims=True)
        dx = dres_ref[...] + (r * gd - xv * (r * r * r * dot))
        dx_ref[...] = dx
        dxb_ref[...] = dx.astype(BF16)
        dg_ref[...] += jnp.sum(dhv * (xv * r), axis=0, keepdims=True)

    row = pl.BlockSpec((tm, d), lambda i: (i, 0))
    dx, dxb, dg = pl.pallas_call(
        body, name=name, grid=(s // tm,),
        in_specs=[row, row, pl.BlockSpec((1, d), lambda i: (0, 0)), row],
        out_specs=[row, row, pl.BlockSpec((1, d), lambda i: (0, 0))],
        out_shape=[jax.ShapeDtypeStruct((s, d), F32), jax.ShapeDtypeStruct((s, d), BF16),
                   jax.ShapeDtypeStruct((1, d), F32)],
        compiler_params=_params(("arbitrary",)),
    )(dh, x, g.reshape(1, d), dres)
    return dx, dxb, dg[0]


def loss_head(y, target, *, name):
    s, d = y.shape
    tm = 512

    def body(y_ref, t_ref, dy_ref, dyb_ref, l_ref):
        @pl.when(pl.program_id(0) == 0)
        def _():
            l_ref[...] = jnp.zeros_like(l_ref)

        e = y_ref[...] - t_ref[...]
        dy = e / float(d)
        dy_ref[...] = dy
        dyb_ref[...] = dy.astype(BF16)
        per_tok = jnp.mean(e * e, axis=-1, keepdims=True)
        l_ref[...] += 0.5 * jnp.sum(per_tok, axis=0, keepdims=True)

    row = pl.BlockSpec((tm, d), lambda i: (i, 0))
    dy, dyb, l = pl.pallas_call(
        body, name=name, grid=(s // tm,),
        in_specs=[row, row],
        out_specs=[row, row, pl.BlockSpec((8, LANES), lambda i: (0, 0))],
        out_shape=[jax.ShapeDtypeStruct((s, d), F32), jax.ShapeDtypeStruct((s, d), BF16),
                   jax.ShapeDtypeStruct((8, LANES), F32)],
        compiler_params=_params(("arbitrary",)),
    )(y, target)
    return dy, dyb, l[0, 0]


FFN_TN = 256
FFN_CH = 256


def _rows_before(ref, r0, first):
    if first:
        cur = ref[pl.ds(0, FFN_CH), :]
        row = lax.broadcasted_iota(I32, cur.shape, 0)
        sh1 = jnp.where(row < 1, 0.0, pltpu.roll(cur, 1, axis=0))
        sh2 = jnp.where(row < 2, 0.0, pltpu.roll(cur, 2, axis=0))
        return cur, sh1, sh2
    ext = ref[pl.ds(pl.multiple_of(r0 - 8, 8), FFN_CH + 8), :]
    return ext[8:], pltpu.roll(ext, 1, axis=0)[8:], pltpu.roll(ext, 2, axis=0)[8:]


def _rows_after(ref, r0, last):
    if last:
        cur = ref[pl.ds(r0, FFN_CH), :]
        row = lax.broadcasted_iota(I32, cur.shape, 0)
        up1 = jnp.where(row >= FFN_CH - 1, 0.0, pltpu.roll(cur, FFN_CH - 1, axis=0))
        up2 = jnp.where(row >= FFN_CH - 2, 0.0, pltpu.roll(cur, FFN_CH - 2, axis=0))
        return cur, up1, up2
    n = FFN_CH + 8
    ext = ref[pl.ds(r0, n), :]
    return ext[:FFN_CH], pltpu.roll(ext, n - 1, axis=0)[:FFN_CH], pltpu.roll(ext, n - 2, axis=0)[:FFN_CH]


def _sigmoid(x):
    return 0.5 * jnp.tanh(0.5 * x) + 0.5


def ffn_act_fwd(p, conv_w, conv_b, *, name):
    s, f2 = p.shape
    f = f2 // 2
    nj = f // FFN_TN
    nch = s // FFN_CH

    def body(pg_ref, pu_ref, wg_ref, wu_ref, bg_ref, bu_ref, a_ref):
        def conv(ref, w_ref, b_ref, r0, first):
            cur, sh1, sh2 = _rows_before(ref, r0, first)
            return ((b_ref[...] + w_ref[0:1, :] * sh2) + w_ref[1:2, :] * sh1) + w_ref[2:3, :] * cur

        def chunk(r0, first):
            gate = conv(pg_ref, wg_ref, bg_ref, r0, first)
            up = conv(pu_ref, wu_ref, bu_ref, r0, first)
            a_ref[pl.ds(r0, FFN_CH), :] = (gate * _sigmoid(gate) * up).astype(BF16)

        chunk(0, True)

        def step(c, carry):
            chunk(pl.multiple_of(c * FFN_CH, FFN_CH), False)
            return carry

        lax.fori_loop(1, nch, step, 0)

    col = lambda off: pl.BlockSpec((s, FFN_TN), lambda j: (0, j + off))
    wcol = lambda off: pl.BlockSpec((3, FFN_TN), lambda j: (0, j + off))
    bcol = lambda off: pl.BlockSpec((1, FFN_TN), lambda j: (0, j + off))
    return pl.pallas_call(
        body, name=name, grid=(nj,),
        in_specs=[col(0), col(nj), wcol(0), wcol(nj), bcol(0), bcol(nj)],
        out_specs=pl.BlockSpec((s, FFN_TN), lambda j: (0, j)),
        out_shape=jax.ShapeDtypeStruct((s, f), BF16),
        compiler_params=_params(("parallel",), VMEM_LIMIT),
    )(p, p, conv_w, conv_w, conv_b.reshape(1, f2), conv_b.reshape(1, f2))


def ffn_act_bwd(da, p, conv_w, conv_b, *, name):
    s, f2 = p.shape
    f = f2 // 2
    nj = f // FFN_TN
    nch = s // FFN_CH

    def body(da_ref, pg_ref, pu_ref, wg_ref, wu_ref, bg_ref, bu_ref,
             dp_ref, dwg_ref, dwu_ref, dbg_ref, dbu_ref, dug_s, duu_s):
        dpg_ref, dpu_ref = dp_ref.at[0], dp_ref.at[1]
        def conv(ref, w_ref, b_ref, r0, first):
            cur, sh1, sh2 = _rows_before(ref, r0, first)
            u = ((b_ref[...] + w_ref[0:1, :] * sh2) + w_ref[1:2, :] * sh1) + w_ref[2:3, :] * cur
            return u, (sh2, sh1, cur)

        def taps_sum(du, taps):
            return jnp.concatenate([jnp.sum(du * t, axis=0, keepdims=True) for t in taps], axis=0)

        def chunk(r0, first, acc):
            dwg, dwu, dbg, dbu = acc
            gate, tg = conv(pg_ref, wg_ref, bg_ref, r0, first)
            up, tu = conv(pu_ref, wu_ref, bu_ref, r0, first)
            dav = da_ref[pl.ds(r0, FFN_CH), :]
            sg = _sigmoid(gate)
            dgate = dav * up * (sg * (1.0 + gate * (1.0 - sg)))
            dup = dav * (gate * sg)
            dug_s[pl.ds(r0, FFN_CH), :] = dgate
            duu_s[pl.ds(r0, FFN_CH), :] = dup
            return (dwg + taps_sum(dgate, tg), dwu + taps_sum(dup, tu),
                    dbg + jnp.sum(dgate, axis=0, keepdims=True), dbu + jnp.sum(dup, axis=0, keepdims=True))

        z3 = jnp.zeros((3, FFN_TN), F32)
        z1 = jnp.zeros((1, FFN_TN), F32)
        acc = chunk(0, True, (z3, z3, z1, z1))
        acc = lax.fori_loop(1, nch, lambda c, a: chunk(pl.multiple_of(c * FFN_CH, FFN_CH), False, a), acc)
        dwg_ref[...], dwu_ref[...], dbg_ref[...], dbu_ref[...] = acc

        def back(src, w_ref, dst, r0, last):
            cur, up1, up2 = _rows_after(src, r0, last)
            dst[pl.ds(r0, FFN_CH), :] = (w_ref[2:3, :] * cur + w_ref[1:2, :] * up1 + w_ref[0:1, :] * up2).astype(BF16)

        def step(c, carry):
            r0 = pl.multiple_of(c * FFN_CH, FFN_CH)
            back(dug_s, wg_ref, dpg_ref, r0, False)
            back(duu_s, wu_ref, dpu_ref, r0, False)
            return carry

        lax.fori_loop(0, nch - 1, step, 0)
        back(dug_s, wg_ref, dpg_ref, (nch - 1) * FFN_CH, True)
        back(duu_s, wu_ref, dpu_ref, (nch - 1) * FFN_CH, True)

    col = lambda off: pl.BlockSpec((s, FFN_TN), lambda j: (0, j + off))
    wcol = lambda off: pl.BlockSpec((3, FFN_TN), lambda j: (0, j + off))
    bcol = lambda off: pl.BlockSpec((1, FFN_TN), lambda j: (0, j + off))
    outs = pl.pallas_call(
        body, name=name, grid=(nj,),
        in_specs=[col(0), col(0), col(nj), wcol(0), wcol(nj), bcol(0), bcol(nj)],
        out_specs=[pl.BlockSpec((2, s, FFN_TN), lambda j: (0, 0, j)), wcol(0), wcol(0), bcol(0), bcol(0)],
        out_shape=[jax.ShapeDtypeStruct((2, s, f), BF16),
                   jax.ShapeDtypeStruct((3, f), F32), jax.ShapeDtypeStruct((3, f), F32),
                   jax.ShapeDtypeStruct((1, f), F32), jax.ShapeDtypeStruct((1, f), F32)],
        scratch_shapes=[pltpu.VMEM((s, FFN_TN), F32), pltpu.VMEM((s, FFN_TN), F32)],
        compiler_params=_params(("parallel",), VMEM_LIMIT),
    )(da, p, p, conv_w, conv_w, conv_b.reshape(1, f2), conv_b.reshape(1, f2))
    dp, dwg, dwu, dbg, dbu = outs
    return dp, jnp.concatenate([dwg, dwu], axis=1), jnp.concatenate([dbg, dbu], axis=1)[0]


def mixnorm_fwd(outs, gain, *, name):
    s = outs[0].shape[0]
    widths = [o.shape[1] for o in outs]
    total = sum(widths)
    tm = 512

    def body(*refs):
        o_refs, g_ref, m_ref = refs[:-2], refs[-2], refs[-1]
        off = 0
        for o_ref, w in zip(o_refs, widths):
            xv = o_ref[...]
            r = lax.rsqrt(jnp.mean(xv * xv, axis=-1, keepdims=True) + EPS)
            m_ref[:, off:off + w] = (xv * r * g_ref[:, off:off + w]).astype(BF16)
            off += w

    return pl.pallas_call(
        body, name=name, grid=(s // tm,),
        in_specs=[pl.BlockSpec((tm, w), lambda i: (i, 0)) for w in widths] + [pl.BlockSpec((1, total), lambda i: (0, 0))],
        out_specs=pl.BlockSpec((tm, total), lambda i: (i, 0)),
        out_shape=jax.ShapeDtypeStruct((s, total), BF16),
        compiler_params=_params(("parallel",)),
    )(*outs, gain.reshape(1, total))


def mixnorm_bwd(dmix, outs, gain, *, name):
    s = outs[0].shape[0]
    widths = [o.shape[1] for o in outs]
    total = sum(widths)
    n = len(outs)
    tm = 256

    def body(*refs):
        dm_ref, o_refs, g_ref = refs[0], refs[1:1 + n], refs[1 + n]
        d_refs, dg_ref = refs[2 + n:2 + 2 * n], refs[2 + 2 * n]

        @pl.when(pl.program_id(0) == 0)
        def _():
            dg_ref[...] = jnp.zeros_like(dg_ref)

        off = 0
        for o_ref, d_ref, w in zip(o_refs, d_refs, widths):
            xv = o_ref[...]
            dhv = dm_ref[:, off:off + w]
            r = lax.rsqrt(jnp.mean(xv * xv, axis=-1, keepdims=True) + EPS)
            gd = dhv * g_ref[:, off:off + w]
            dot = jnp.mean(gd * xv, axis=-1, keepdims=True)
            d_ref[...] = r * gd - xv * (r * r * r * dot)
            dg_ref[:, off:off + w] += jnp.sum(dhv * (xv * r), axis=0, keepdims=True)
            off += w

    res = pl.pallas_call(
        body, name=name, grid=(s // tm,),
        in_specs=[pl.BlockSpec((tm, total), lambda i: (i, 0))]
        + [pl.BlockSpec((tm, w), lambda i: (i, 0)) for w in widths] + [pl.BlockSpec((1, total), lambda i: (0, 0))],
        out_specs=[pl.BlockSpec((tm, w), lambda i: (i, 0)) for w in widths] + [pl.BlockSpec((1, total), lambda i: (0, 0))],
        out_shape=[jax.ShapeDtypeStruct((s, w), F32) for w in widths] + [jax.ShapeDtypeStruct((1, total), F32)],
        compiler_params=_params(("arbitrary",)),
    )(dmix, *outs, gain.reshape(1, total))
    return res[:n], res[n][0]


NORM_CH = 512
FWD_TILES = 4
BWD_TILES = 4


def _lo_mask(shape):
    return lax.broadcasted_iota(I32, shape, 1) < HEAD_DIM


def _head_sum(x, lo):
    del lo
    i = lax.broadcasted_iota(I32, (LANES, LANES), 0) // HEAD_DIM
    j = lax.broadcasted_iota(I32, (LANES, LANES), 1) // HEAD_DIM
    return _split_dot(x, _twice(i == j))


def _head_stats(x, lo):
    return lax.rsqrt(_head_sum(x * x, lo) * (1.0 / HEAD_DIM) + EPS)


def _swap_halves(x):
    return pltpu.roll(x, HEAD_DIM, axis=1)


def _replicate_head(x, lo, use_lo_head):
    sw = _swap_halves(x)
    return jnp.where(use_lo_head, jnp.where(lo, x, sw), jnp.where(lo, sw, x))


def _tile_rows(i, s, d):
    nb = s // (BLOCK * d)
    r = i // nb
    b = i % nb
    start = r + (BLOCK * d) * b
    prev = start - (BLOCK * d) * jnp.minimum(b, 1)
    return start, prev, b > 0


def _rows(ref, start, d):
    if d == 1:
        return ref[pl.ds(pl.multiple_of(start, BLOCK), BLOCK), :]
    return ref[pl.ds(start, BLOCK, stride=d), :]


def _set_rows(ref, start, d, val):
    if d == 1:
        ref[pl.ds(pl.multiple_of(start, BLOCK), BLOCK), :] = val
    else:
        ref[pl.ds(start, BLOCK, stride=d), :] = val


def banded_fwd(proj, qb0, kb0, vb0, n_slabs, gq, gk, bias, dils, sinks, gqa, *, name):
    s = proj.shape[0]
    nbr = len(dils)
    nt = s // BLOCK
    nch = s // NORM_CH
    has_sink = sinks is not None

    def body(*refs):
        q_ref, k_ref, v_ref, gq_ref, gk_ref, b_ref = refs[:6]
        rest = refs[6:]
        if has_sink:
            sink_ref, rest = rest[0], rest[1:]
        out_ref, lse_ref, qn_s, kn_s, vv_s, o_s, l_s = rest
        p = pl.program_id(0)
        use_lo = (p // 2) == 0

        def prep(c, carry):
            rows = pl.ds(pl.multiple_of(c * NORM_CH, NORM_CH), NORM_CH)
            lo = _lo_mask((NORM_CH, LANES))
            qv, kv, vv = q_ref[rows, :], k_ref[rows, :], v_ref[rows, :]
            qn_s[rows, :] = qv * _head_stats(qv, lo) * gq_ref[...] * (HEAD_DIM ** -0.5)
            kn = kv * _head_stats(kv, lo) * gk_ref[...]
            if gqa:
                kn = _replicate_head(kn, lo, use_lo)
                vv = _replicate_head(vv, lo, use_lo)
            kn_s[rows, :] = kn
            vv_s[rows, :] = vv
            return carry

        lax.fori_loop(0, nch, prep, 0)

        lo = _lo_mask((BLOCK, LANES))
        hms = [lo, jnp.logical_not(lo)]
        heads, tiles = range(2), range(FWD_TILES)
        for br, d in enumerate(dils):
            def step(ii, carry, br=br, d=d):
                pos = [_tile_rows(ii * FWD_TILES + u, s, d) for u in tiles]
                kc = [carry[0]] + [_rows(kn_s, pos[u][0], d).astype(BF16) for u in tiles]
                vc = [carry[1]] + [_rows(vv_s, pos[u][0], d).astype(BF16) for u in tiles]
                kcat = [jnp.concatenate([kc[u], kc[u + 1]], axis=0) for u in tiles]
                vcat = [jnp.concatenate([vc[u], vc[u + 1]], axis=0) for u in tiles]
                qt = [_rows(qn_s, pos[u][0], d) for u in tiles]
                sc = [[_dot_nt(jnp.where(hms[h], qt[u], 0.0).astype(BF16), kcat[u])
                       + b_ref[br, jnp.where(pos[u][2], 0, 1), h] for h in heads] for u in tiles]
                m = [[jnp.max(sc[u][h], axis=1, keepdims=True) for h in heads] for u in tiles]
                pe = [[jnp.exp(sc[u][h] - m[u][h]) for h in heads] for u in tiles]
                den = [[jnp.sum(pe[u][h], axis=1, keepdims=True) for h in heads] for u in tiles]
                o = [[_dot(pe[u][h].astype(BF16), vcat[u]) * (1.0 / den[u][h]) for h in heads] for u in tiles]
                for u in tiles:
                    _set_rows(o_s.at[br], pos[u][0], d, jnp.where(lo, o[u][0], o[u][1]))
                    _set_rows(l_s.at[br], pos[u][0], d,
                              jnp.where(lo, m[u][0] + jnp.log(den[u][0]), m[u][1] + jnp.log(den[u][1])))
                return kc[-1], vc[-1]

            none_yet = jnp.zeros((BLOCK, LANES), BF16)
            lax.fori_loop(0, nt // FWD_TILES, step, (none_yet, none_yet))

        def combine(c, carry):
            rows = pl.ds(pl.multiple_of(c * NORM_CH, NORM_CH), NORM_CH)
            ls = [l_s[br, rows, :] for br in range(nbr)]
            mx = functools.reduce(jnp.maximum, ls)
            if has_sink:
                mx = jnp.maximum(mx, sink_ref[...])
            tot = functools.reduce(jnp.add, [jnp.exp(l - mx) for l in ls])
            if has_sink:
                tot = tot + jnp.exp(sink_ref[...] - mx)
            lse = mx + jnp.log(tot)
            acc = jnp.exp(ls[0] - lse) * o_s[0, rows, :]
            for br in range(1, nbr):
                acc = acc + jnp.exp(ls[br] - lse) * o_s[br, rows, :]
            out_ref[rows, :] = acc
            lse_ref[rows, :] = lse
            return carry

        lax.fori_loop(0, nch, combine, 0)

    slab = lambda b0, shared: pl.BlockSpec((s, LANES), (lambda p: (0, b0)) if shared else (lambda p: (0, b0 + p)),
                                           pipeline_mode=pl.Buffered(1))
    vec = pl.BlockSpec((1, LANES), lambda p: (0, 0))
    in_specs = [slab(qb0, False), slab(kb0, gqa), slab(vb0, gqa), vec, vec,
                pl.BlockSpec((nbr, 2, 2, BLOCK, 2 * BLOCK), lambda p: (0, 0, p, 0, 0))]
    args = [proj, proj, proj, gq.reshape(1, LANES), gk.reshape(1, LANES), bias]
    if has_sink:
        in_specs.append(pl.BlockSpec((None, 1, LANES), lambda p: (p, 0, 0)))
        args.append(sinks)
    w = LANES * n_slabs
    return pl.pallas_call(
        body, name=name, grid=(n_slabs,),
        in_specs=in_specs,
        out_specs=[pl.BlockSpec((s, LANES), lambda p: (0, p)), pl.BlockSpec((s, LANES), lambda p: (0, p))],
        out_shape=[jax.ShapeDtypeStruct((s, w), F32), jax.ShapeDtypeStruct((s, w), F32)],
        scratch_shapes=[pltpu.VMEM((s, LANES), F32), pltpu.VMEM((s, LANES), F32), pltpu.VMEM((s, LANES), F32),
                        pltpu.VMEM((nbr, s, LANES), F32), pltpu.VMEM((nbr, s, LANES), F32)],
        compiler_params=_params(("parallel",), VMEM_LIMIT),
    )(*args)


def banded_bwd(proj, qb0, kb0, vb0, n_slabs, gq, gk, bias, dils, sinks, gqa, dout, out, lse, dproj, *, name):
    s = proj.shape[0]
    nbr = len(dils)
    nt = s // BLOCK
    nch = s // NORM_CH
    has_sink = sinks is not None
    scale = HEAD_DIM ** -0.5

    def body(*refs):
        q_ref, k_ref, v_ref, gq_ref, gk_ref, b_ref, do_ref, o_ref, lse_ref = refs[:9]
        rest = refs[9:]
        if has_sink:
            sink_ref, rest = rest[0], rest[1:]
        dproj_ref, db_ref, dgq_ref, dgk_ref = rest[1:5]
        rest = rest[5:]
        if has_sink:
            dsink_ref, rest = rest[0], rest[1:]
        qn_s, kn_s, vv_s, dl_s, dqn_s, dkn_s, dvv_s, dq_ref, dk_ref, dv_ref, stage, sems = rest
        p = pl.program_id(0)
        use_lo = (p // 2) == 0

        def prep(c, carry):
            rows = pl.ds(pl.multiple_of(c * NORM_CH, NORM_CH), NORM_CH)
            lo = _lo_mask((NORM_CH, LANES))
            qv, kv, vv = q_ref[rows, :], k_ref[rows, :], v_ref[rows, :]
            qn_s[rows, :] = qv * _head_stats(qv, lo) * gq_ref[...] * scale
            kn = kv * _head_stats(kv, lo) * gk_ref[...]
            if gqa:
                kn = _replicate_head(kn, lo, use_lo)
                vv = _replicate_head(vv, lo, use_lo)
            kn_s[rows, :] = kn
            vv_s[rows, :] = vv
            delta = _head_sum(do_ref[rows, :] * o_ref[rows, :], lo)
            odd = lax.broadcasted_iota(I32, (NORM_CH, LANES), 1) % 2 == 1
            dl_s[rows, :] = jnp.where(odd, delta, lse_ref[rows, :])
            z = jnp.zeros((NORM_CH, LANES), F32)
            dqn_s[rows, :] = z
            dkn_s[rows, :] = z
            dvv_s[rows, :] = z
            if has_sink:
                ps = jnp.exp(sink_ref[...] - lse_ref[rows, :])
                return carry - jnp.sum(ps * delta, axis=0, keepdims=True)
            return carry

        dsink = lax.fori_loop(0, nch, prep, jnp.zeros((1, LANES), F32))
        if has_sink:
            dsink_ref[...] = jnp.broadcast_to(dsink, (8, LANES))

        lo = _lo_mask((BLOCK, LANES))
        hms = [lo, jnp.logical_not(lo)]
        heads, tiles = range(2), range(BWD_TILES)
        for br, d in enumerate(dils):
            db_ref[br] = jnp.zeros((2, BLOCK, 2 * BLOCK), F32)

            def step(ii, carry, br=br, d=d):
                pos = [_tile_rows(ii * BWD_TILES + u, s, d) for u in tiles]
                kc = [carry[0]] + [_rows(kn_s, pos[u][0], d).astype(BF16) for u in tiles]
                vc = [carry[1]] + [_rows(vv_s, pos[u][0], d).astype(BF16) for u in tiles]
                kcat = [jnp.concatenate([kc[u], kc[u + 1]], axis=0) for u in tiles]
                vcat = [jnp.concatenate([vc[u], vc[u + 1]], axis=0) for u in tiles]
                qt = [_rows(qn_s, pos[u][0], d) for u in tiles]
                dot_ = [_rows(do_ref, pos[u][0], d) for u in tiles]
                st_t = [_rows(dl_s, pos[u][0], d) for u in tiles]
                qh = [[jnp.where(hms[h], qt[u], 0.0).astype(BF16) for h in heads] for u in tiles]
                doh = [[jnp.where(hms[h], dot_[u], 0.0).astype(BF16) for h in heads] for u in tiles]
                sc = [[_dot_nt(qh[u][h], kcat[u]) + b_ref[br, jnp.where(pos[u][2], 0, 1), h] for h in heads]
                      for u in tiles]
                dp = [[_dot_nt(doh[u][h], vcat[u]) for h in heads] for u in tiles]
                lane0 = [0, HEAD_DIM]
                pr = [[jnp.exp(sc[u][h] - st_t[u][:, lane0[h]:lane0[h] + 1]) for h in heads] for u in tiles]
                dlog = [[pr[u][h] * (dp[u][h] - st_t[u][:, lane0[h] + 1:lane0[h] + 2]) for h in heads] for u in tiles]
                for h in heads:
                    db_ref[br, h] += functools.reduce(jnp.add, [dlog[u][h] for u in tiles])
                dlb = [[dlog[u][h].astype(BF16) for h in heads] for u in tiles]
                prb = [[pr[u][h].astype(BF16) for h in heads] for u in tiles]
                dq_t = [jnp.where(lo, _dot(dlb[u][0], kcat[u]), _dot(dlb[u][1], kcat[u])) * scale for u in tiles]
                rows2 = lambda x: jnp.concatenate(x, axis=0)
                dk_t = [_dot_tn(rows2(dlb[u]), rows2(qh[u])) for u in tiles]
                dv_t = [_dot_tn(rows2(prb[u]), rows2(doh[u])) for u in tiles]
                for u in tiles:
                    start, prev = pos[u][0], pos[u][1]
                    _set_rows(dqn_s, start, d, _rows(dqn_s, start, d) + dq_t[u])
                    _set_rows(dkn_s, prev, d, _rows(dkn_s, prev, d) + dk_t[u][:BLOCK])
                    _set_rows(dkn_s, start, d, _rows(dkn_s, start, d) + dk_t[u][BLOCK:])
                    _set_rows(dvv_s, prev, d, _rows(dvv_s, prev, d) + dv_t[u][:BLOCK])
                    _set_rows(dvv_s, start, d, _rows(dvv_s, start, d) + dv_t[u][BLOCK:])
                return kc[-1], vc[-1]

            none_yet = jnp.zeros((BLOCK, LANES), BF16)
            lax.fori_loop(0, nt // BWD_TILES, step, (none_yet, none_yet))

        if gqa:
            @pl.when(p == 0)
            def _():
                dk_ref[...] = jnp.zeros_like(dk_ref)
                dv_ref[...] = jnp.zeros_like(dv_ref)

        def finish(c, carry):
            dgq, dgk = carry
            rows = pl.ds(pl.multiple_of(c * NORM_CH, NORM_CH), NORM_CH)
            lo = _lo_mask((NORM_CH, LANES))

            def norm_bwd(xv, dn, g_ref):
                r = _head_stats(xv, lo)
                gd = dn * g_ref[...]
                dot = _head_sum(gd * xv, lo) * (1.0 / HEAD_DIM)
                return r * gd - xv * (r * r * r * dot), dn * (xv * r)

            dq, gq_part = norm_bwd(q_ref[rows, :], dqn_s[rows, :], gq_ref)
            dq_ref[rows, :] = dq
            dgq = dgq + jnp.sum(gq_part, axis=0, keepdims=True)
            kv, dkn, dvv = k_ref[rows, :], dkn_s[rows, :], dvv_s[rows, :]
            if gqa:
                kv = _replicate_head(kv, lo, use_lo)
                dkn = dkn + _swap_halves(dkn)
                dvv = dvv + _swap_halves(dvv)
                lane = lax.broadcasted_iota(I32, (NORM_CH, LANES), 1)
                mine = (lane // HEAD_DIM) == (p // 2)
                dk, gk_part = norm_bwd(kv, dkn, gk_ref)
                dk_ref[rows, :] += jnp.where(mine, dk, 0.0)
                dv_ref[rows, :] += jnp.where(mine, dvv, 0.0)
                gk_part = jnp.where(lo, gk_part, 0.0)
            else:
                dk, gk_part = norm_bwd(kv, dkn, gk_ref)
                dk_ref[rows, :] = dk
                dv_ref[rows, :] = dvv
            dgk = dgk + jnp.sum(gk_part, axis=0, keepdims=True)
            return dgq, dgk

        z = jnp.zeros((1, LANES), F32)
        dgq, dgk = lax.fori_loop(0, nch, finish, (z, z))
        dgq_ref[...] = jnp.broadcast_to(dgq, (8, LANES))
        dgk_ref[...] = jnp.broadcast_to(dgk, (8, LANES))
        if gqa:
            _store_slabs((dq_ref,), stage, dproj_ref, sems, (qb0 + p,))

            @pl.when(p == n_slabs - 1)
            def _():
                _store_slabs((dk_ref, dv_ref), stage, dproj_ref, sems, (kb0, vb0))
        else:
            _store_slabs((dq_ref, dk_ref, dv_ref), stage, dproj_ref, sems, (qb0 + p, kb0 + p, vb0 + p))

    def slab_of(width_blocks, b0, shared):
        return pl.BlockSpec((s, LANES), (lambda p: (0, b0)) if shared else (lambda p: (0, b0 + p)),
                            pipeline_mode=pl.Buffered(1))

    vec = pl.BlockSpec((1, LANES), lambda p: (0, 0))
    own = pl.BlockSpec((s, LANES), lambda p: (0, p), pipeline_mode=pl.Buffered(1))
    in_specs = [slab_of(0, qb0, False), slab_of(0, kb0, gqa), slab_of(0, vb0, gqa), vec, vec,
                pl.BlockSpec((nbr, 2, 2, BLOCK, 2 * BLOCK), lambda p: (0, 0, p, 0, 0)), own, own, own]
    args = [proj, proj, proj, gq.reshape(1, LANES), gk.reshape(1, LANES), bias, dout, out, lse]
    if has_sink:
        in_specs.append(pl.BlockSpec((None, 1, LANES), lambda p: (p, 0, 0)))
        args.append(sinks)
    held = pl.BlockSpec(memory_space=pl.ANY)
    in_specs.append(held)
    args.append(dproj)
    part = pl.BlockSpec((None, 8, LANES), lambda p: (p, 0, 0))
    out_specs = [held, pl.BlockSpec((nbr, 2, BLOCK, 2 * BLOCK), lambda p: (0, p, 0, 0)), part, part]
    out_shape = [jax.ShapeDtypeStruct(dproj.shape, dproj.dtype),
                 jax.ShapeDtypeStruct((nbr, 2 * n_slabs, BLOCK, 2 * BLOCK), F32),
                 jax.ShapeDtypeStruct((n_slabs, 8, LANES), F32), jax.ShapeDtypeStruct((n_slabs, 8, LANES), F32)]
    if has_sink:
        out_specs.append(part)
        out_shape.append(jax.ShapeDtypeStruct((n_slabs, 8, LANES), F32))
    res = pl.pallas_call(
        body, name=name, grid=(n_slabs,),
        in_specs=in_specs, out_specs=out_specs, out_shape=out_shape,
        input_output_aliases={len(args) - 1: 0},
        scratch_shapes=[pltpu.VMEM((s, LANES), F32) for _ in range(10)]
        + [pltpu.VMEM((3, s, LANES), BF16), pltpu.SemaphoreType.DMA((3,))],
        compiler_params=_params(("arbitrary",), VMEM_LIMIT),
    )(*args)
    outs = [res[0], res[1], res[2][:, 0, :], res[3][:, 0, :]]
    if has_sink:
        outs.append(res[4][:, 0, :])
    return outs


def bias_bwd(dbias, buckets, *, name):
    nbr, h = dbias.shape[:2]

    def body(db_ref, bk_ref, o_ref):
        lane = lax.broadcasted_iota(I32, (1, LANES), 1)
        acc = jnp.zeros((1, LANES), F32)
        for b in range(N_BUCKETS):
            tot = jnp.zeros((1, 1), F32)
            for br in range(nbr):
                sel = jnp.where(bk_ref[br] == b, db_ref[br], 0.0)
                tot = tot + jnp.sum(jnp.sum(sel, axis=0, keepdims=True), axis=1, keepdims=True)
            acc = jnp.where(lane == b, tot, acc)
        o_ref[...] = jnp.broadcast_to(acc, (8, LANES))

    res = pl.pallas_call(
        body, name=name, grid=(h,),
        in_specs=[pl.BlockSpec((nbr, None, BLOCK, 2 * BLOCK), lambda i: (0, i, 0, 0)),
                  pl.BlockSpec((nbr, BLOCK, 2 * BLOCK), lambda i: (0, 0, 0))],
        out_specs=pl.BlockSpec((None, 8, LANES), lambda i: (i, 0, 0)),
        out_shape=jax.ShapeDtypeStruct((h, 8, LANES), F32),
        compiler_params=_params(("parallel",)),
    )(dbias, buckets)
    return res[:, 0, :N_BUCKETS].T


SB_KG = 512
SB_QT = 2


def _softplus(z):
    return jnp.maximum(z, 0.0) + jnp.log(1.0 + jnp.exp(-jnp.abs(z)))


def _twice(t):
    t = t.astype(BF16)
    return jnp.concatenate([t, t], axis=0)


def _split_dot(x, t2):
    hi = x.astype(BF16)
    lo = (x - hi.astype(F32)).astype(BF16)
    return _dot(jnp.concatenate([hi, lo], axis=1), t2)


def sb_fwd(proj, qb0, kb0, vb0, n_slabs, *, name):
    s = proj.shape[0]
    nq = s // BLOCK
    nch = s // NORM_CH
    scale = HEAD_DIM ** -0.5

    def body(q_ref, k_ref, v_ref, o_ref, tot_ref, qlo_s, qhi_s, k_s, v_s):
        def prep(c, carry):
            rows = pl.ds(pl.multiple_of(c * NORM_CH, NORM_CH), NORM_CH)
            lo = _lo_mask((NORM_CH, LANES))
            qv = q_ref[rows, :] * scale
            qlo_s[rows, :] = jnp.where(lo, qv, 0.0).astype(BF16)
            qhi_s[rows, :] = jnp.where(lo, 0.0, qv).astype(BF16)
            k_s[rows, :] = k_ref[rows, :].astype(BF16)
            v_s[rows, :] = v_ref[rows, :].astype(BF16)
            return carry

        lax.fori_loop(0, nch, prep, 0)

        row = lax.broadcasted_iota(I32, (BLOCK, BLOCK), 0)
        col = lax.broadcasted_iota(I32, (BLOCK, BLOCK), 1)
        lo = col < HEAD_DIM
        t_ge = _twice(row >= col)
        rowg = lax.broadcasted_iota(I32, (BLOCK, SB_KG), 0)
        colg = lax.broadcasted_iota(I32, (BLOCK, SB_KG), 1)

        nsub = SB_KG // BLOCK
        chains = range(2 * SB_QT)
        nc = len(chains)

        def qloop(qs, phase):
            q0 = pl.multiple_of(qs * (SB_QT * BLOCK), SB_QT * BLOCK)
            qh = [(qlo_s, qhi_s)[i % 2][pl.ds(q0 + (i // 2) * BLOCK, BLOCK), :] for i in chains]
            gd = (qs * SB_QT) // nsub

            def logits(gi):
                k0 = pl.multiple_of(gi * SB_KG, SB_KG)
                kg = k_s[pl.ds(k0, SB_KG), :]
                return [_dot_nt(qh[i], kg) for i in chains]

            def group(gi, st, masks, npiece=nsub):
                k0 = pl.multiple_of(gi * SB_KG, SB_KG)
                vg = v_s[pl.ds(k0, npiece * BLOCK), :]
                c, o, z = list(st[:nc]), st[nc:2 * nc], st[2 * nc:]
                z_next = logits(jnp.maximum(gi - 1, 0))
                piece = lambda x, j: x[:, j * BLOCK:(j + 1) * BLOCK]
                a = [[None] * npiece for _ in chains]
                for j in reversed(range(npiece)):
                    zj = [piece(z[i], j) for i in chains]
                    lrem = [-_softplus(zj[i]) for i in chains]
                    if masks is not None:
                        lrem = [jnp.where(piece(masks[i // 2], j), lrem[i], 0.0) for i in chains]
                    incl = [_split_dot(lrem[i], t_ge) for i in chains]
                    for i in chains:
                        aij = jnp.exp(zj[i] + (c[i] + incl[i]))
                        if masks is not None:
                            aij = jnp.where(piece(masks[i // 2], j), aij, 0.0)
                        a[i][j] = aij.astype(BF16)
                        c[i] = c[i] + incl[i][:, 0:1]
                o = [o[i] + _dot(jnp.concatenate(a[i], axis=1), vg) for i in chains]
                return (*c, *o, *z_next)

            zc = [jnp.zeros((BLOCK, 1), F32)] * nc
            zo = [jnp.zeros((BLOCK, LANES), F32)] * nc
            masks = [(gd * SB_KG + colg) < (q0 + t * BLOCK + rowg) for t in range(SB_QT)]
            st = group(gd, (*zc, *zo, *logits(gd)), masks, (phase + 1) * SB_QT)
            st = lax.fori_loop(0, gd, lambda t, st: group(gd - 1 - t, st, None), st)
            for t in range(SB_QT):
                rows = pl.ds(q0 + t * BLOCK, BLOCK)
                o_ref[rows, :] = jnp.where(lo, st[nc + 2 * t], st[nc + 2 * t + 1])
                tot_ref[rows, :] = jnp.where(lo, st[2 * t], st[2 * t + 1])

        steps_per_group = nsub // SB_QT

        def per_group(g, carry):
            for phase in range(steps_per_group):
                qloop(g * steps_per_group + phase, phase)
            return carry

        lax.fori_loop(0, nq // nsub, per_group, 0)

    slab = lambda b0: pl.BlockSpec((s, LANES), lambda p: (0, b0 + p), pipeline_mode=pl.Buffered(1))
    w = LANES * n_slabs
    return pl.pallas_call(
        body, name=name, grid=(n_slabs,),
        in_specs=[slab(qb0), slab(kb0), slab(vb0)],
        out_specs=[pl.BlockSpec((s, LANES), lambda p: (0, p)), pl.BlockSpec((s, LANES), lambda p: (0, p))],
        out_shape=[jax.ShapeDtypeStruct((s, w), F32), jax.ShapeDtypeStruct((s, w), F32)],
        scratch_shapes=[pltpu.VMEM((s, LANES), BF16) for _ in range(4)],
        compiler_params=_params(("parallel",), VMEM_LIMIT),
    )(proj, proj, proj)


def _store_slabs(slabs, stage, dproj_ref, sems, blocks):
    s = stage.shape[1]

    def cast(c, carry):
        rows = pl.ds(pl.multiple_of(c * NORM_CH, NORM_CH), NORM_CH)
        for i, slab in enumerate(slabs):
            stage[i, rows, :] = slab[rows, :].astype(BF16)
        return carry

    lax.fori_loop(0, s // NORM_CH, cast, 0)
    copies = [pltpu.make_async_copy(stage.at[i], dproj_ref.at[:, pl.ds(pl.multiple_of(b * LANES, LANES), LANES)],
                                    sems.at[i]) for i, b in enumerate(blocks)]
    for cp in copies:
        cp.start()
    for cp in copies:
        cp.wait()


def sb_bwd(proj, qb0, kb0, vb0, n_slabs, dout, tot, dproj, *, name):
    s = proj.shape[0]
    nq = s // BLOCK
    nch = s // NORM_CH
    nsub = SB_KG // BLOCK
    scale = HEAD_DIM ** -0.5

    def body(q_ref, k_ref, v_ref, do_ref, tot_ref, dproj_in, dproj_ref,
             qlo_s, qhi_s, k_s, v_s, dlo_s, dhi_s, dq_ref, dk_ref, dv_ref, stage, sems):
        del dproj_in
        def prep(c, carry):
            rows = pl.ds(pl.multiple_of(c * NORM_CH, NORM_CH), NORM_CH)
            lo = _lo_mask((NORM_CH, LANES))
            qv = q_ref[rows, :] * scale
            dv = do_ref[rows, :]
            qlo_s[rows, :] = jnp.where(lo, qv, 0.0).astype(BF16)
            qhi_s[rows, :] = jnp.where(lo, 0.0, qv).astype(BF16)
            dlo_s[rows, :] = jnp.where(lo, dv, 0.0).astype(BF16)
            dhi_s[rows, :] = jnp.where(lo, 0.0, dv).astype(BF16)
            k_s[rows, :] = k_ref[rows, :].astype(BF16)
            v_s[rows, :] = v_ref[rows, :].astype(BF16)
            z = jnp.zeros((NORM_CH, LANES), F32)
            dk_ref[rows, :] = z
            dv_ref[rows, :] = z
            return carry

        lax.fori_loop(0, nch, prep, 0)

        row = lax.broadcasted_iota(I32, (BLOCK, BLOCK), 0)
        col = lax.broadcasted_iota(I32, (BLOCK, BLOCK), 1)
        lo = col < HEAD_DIM
        t_le = _twice(row <= col)
        rowg = lax.broadcasted_iota(I32, (BLOCK, SB_KG), 0)
        colg = lax.broadcasted_iota(I32, (BLOCK, SB_KG), 1)

        piece = lambda x, j: x[:, j * BLOCK:(j + 1) * BLOCK]
        chains = range(2 * SB_QT)
        nc = len(chains)

        def prefixes(x):
            return [[_split_dot(piece(x[i], j), t_le) for j in range(x[i].shape[1] // BLOCK)] for i in chains]

        def chain(pre, run, total=None):
            out = []
            for pj in pre:
                out.append(run + pj if total is None else total - run - pj)
                run = run + pj[:, BLOCK - 1:BLOCK]
            return jnp.concatenate(out, axis=1), run

        def qloop(qs, phase):
            q0 = pl.multiple_of(qs * (SB_QT * BLOCK), SB_QT * BLOCK)
            tile = lambda ref, i: ref[pl.ds(q0 + (i // 2) * BLOCK, BLOCK), :]
            qh = [tile((qlo_s, qhi_s)[i % 2], i) for i in chains]
            doh = [tile((dlo_s, dhi_s)[i % 2], i) for i in chains]
            tots = [tile(tot_ref, i)[:, (i % 2) * HEAD_DIM:(i % 2) * HEAD_DIM + 1] for i in chains]
            gd = (qs * SB_QT) // nsub

            def logits(gi):
                kg = k_s[pl.ds(pl.multiple_of(gi * SB_KG, SB_KG), SB_KG), :]
                return [_dot_nt(qh[i], kg) for i in chains]

            def group(gi, st, masks, npiece=nsub):
                k0 = pl.multiple_of(gi * SB_KG, SB_KG)
                wide = npiece * BLOCK
                kg, vg = k_s[pl.ds(k0, wide), :], v_s[pl.ds(k0, wide), :]
                cp, cg, dq = list(st[:nc]), list(st[nc:2 * nc]), st[2 * nc:2 * nc + SB_QT]
                z = [zi[:, :wide] for zi in st[2 * nc + SB_QT:]]
                masked = lambda x, i: x if masks is None else jnp.where(masks[i // 2][:, :wide], x, 0.0)
                z_next = logits(jnp.minimum(gi + 1, gd))
                da = [_dot_nt(doh[i], vg) for i in chains]
                sp = [_softplus(z[i]) for i in chains]
                lrem = [masked(-sp[i], i) for i in chains]
                pre = prefixes(lrem)
                e, a, g = [], [], []
                for i in chains:
                    suffix, cp[i] = chain(pre[i], cp[i], tots[i])
                    e.append(z[i] - sp[i])
                    a.append(masked(jnp.exp(e[i] + suffix), i))
                    g.append(a[i] * da[i])
                gpre = prefixes(g)
                dz = []
                for i in chains:
                    ginc, cg[i] = chain(gpre[i], cg[i])
                    dz.append(masked(g[i] - jnp.exp(e[i]) * ginc, i).astype(BF16))
                ab = [a[i].astype(BF16) for i in chains]
                dq = [dq[t] + jnp.where(lo, _dot(dz[2 * t], kg), _dot(dz[2 * t + 1], kg)) for t in range(SB_QT)]
                rows_of = lambda x: jnp.concatenate(x, axis=0)
                dk_ref[pl.ds(k0, wide), :] += _dot_tn(rows_of(dz), rows_of(qh))
                dv_ref[pl.ds(k0, wide), :] += _dot_tn(rows_of(ab), rows_of(doh))
                return (*cp, *cg, *dq, *z_next)

            zc = [jnp.zeros((BLOCK, 1), F32)] * (2 * nc)
            zq = [jnp.zeros((BLOCK, LANES), F32)] * SB_QT
            st = lax.fori_loop(0, gd, lambda gi, st: group(gi, st, None), (*zc, *zq, *logits(0)))
            st = group(gd, st, [(gd * SB_KG + colg) < (q0 + t * BLOCK + rowg) for t in range(SB_QT)],
                       (phase + 1) * SB_QT)
            for t in range(SB_QT):
                dq_ref[pl.ds(q0 + t * BLOCK, BLOCK), :] = st[2 * nc + t] * scale

        steps_per_group = nsub // SB_QT

        def per_group(g, carry):
            for phase in range(steps_per_group):
                qloop(g * steps_per_group + phase, phase)
            return carry

        lax.fori_loop(0, nq // nsub, per_group, 0)
        p = pl.program_id(0)
        _store_slabs((dq_ref, dk_ref, dv_ref), stage, dproj_ref, sems, (qb0 + p, kb0 + p, vb0 + p))

    slab = lambda b0: pl.BlockSpec((s, LANES), lambda p: (0, b0 + p), pipeline_mode=pl.Buffered(1))
    own = pl.BlockSpec((s, LANES), lambda p: (0, p), pipeline_mode=pl.Buffered(1))
    held = pl.BlockSpec(memory_space=pl.ANY)
    return pl.pallas_call(
        body, name=name, grid=(n_slabs,),
        in_specs=[slab(qb0), slab(kb0), slab(vb0), own, own, held],
        out_specs=held, out_shape=jax.ShapeDtypeStruct(dproj.shape, dproj.dtype),
        input_output_aliases={5: 0},
        scratch_shapes=[pltpu.VMEM((s, LANES), BF16) for _ in range(6)]
        + [pltpu.VMEM((s, LANES), F32) for _ in range(3)]
        + [pltpu.VMEM((3, s, LANES), BF16), pltpu.SemaphoreType.DMA((3,))],
        compiler_params=_params(("arbitrary",), VMEM_LIMIT),
    )(proj, proj, proj, dout, tot, dproj)


def _place():
    x, y, c = lax.axis_index("x"), lax.axis_index("y"), lax.axis_index("c")
    return x, y, c


def gather_small(v, *, name):
    m_per, n = v.shape

    def body(x_ref, out_ref, send_sems, recv_sems, local_sem):
        x, y, c = _place()
        me, sibling = (x, y, c), (x, y, 1 - c)
        chips = [(1 - x, y), (x, 1 - y), (1 - x, 1 - y)]

        def rows(px, py, pc):
            return out_ref.at[pl.ds((4 * px + 2 * py + pc) * m_per, m_per), :]

        def copy(k, block, to, src=None):
            return pltpu.make_async_remote_copy(
                src_ref=rows(*block) if src is None else src, dst_ref=rows(*block),
                send_sem=send_sems.at[k], recv_sem=recv_sems.at[k], device_id=to, device_id_type=MESH)

        mine = pltpu.make_async_copy(x_ref, rows(*me), local_sem)
        mine.start()
        first = [copy(0, me, sibling, src=x_ref)]
        first += [copy(1 + j, me, (*chip, c), src=x_ref) for j, chip in enumerate(chips)]
        for cp in first:
            cp.start()
        passed = [copy(4 + j, (*chip, c), sibling) for j, chip in enumerate(chips)]
        for j, chip in enumerate(chips):
            copy(1 + j, (*chip, c), me).wait_recv()
            passed[j].start()
        copy(0, sibling, me).wait_recv()
        for j, chip in enumerate(chips):
            copy(4 + j, (*chip, 1 - c), me).wait_recv()
        for cp in first + passed:
            cp.wait_send()
        mine.wait()

    return pl.pallas_call(
        body, name=name,
        out_shape=jax.ShapeDtypeStruct((N_DEV * m_per, n), v.dtype),
        in_specs=[pl.BlockSpec(memory_space=pltpu.VMEM)],
        out_specs=pl.BlockSpec(memory_space=pltpu.VMEM),
        scratch_shapes=[pltpu.SemaphoreType.DMA((7,)), pltpu.SemaphoreType.DMA((7,)), pltpu.SemaphoreType.DMA],
        compiler_params=_params(None, VMEM_LIMIT),
    )(v)


_HBM = pl.BlockSpec(memory_space=pltpu.HBM)
_SEM = pl.BlockSpec(memory_space=pltpu.SEMAPHORE)
_EFFECT = pltpu.SideEffectType.DATAFLOW_SIDE_EFFECTING


def _peer_copies(src_refs, land_refs, send_sems, recv_sems, per_dest):
    x, y, c = _place()
    me = 4 * x + 2 * y + c
    copies = []
    for src, land, ssem, rsem in zip(src_refs, land_refs, send_sems, recv_sems):
        for k in (1, 2, 4, 3, 5, 6, 7):
            px, py, pc = x ^ (k >> 2 & 1), y ^ (k >> 1 & 1), c ^ (k & 1)
            copies.append(pltpu.make_async_remote_copy(
                src_ref=src.at[4 * px + 2 * py + pc] if per_dest else src, dst_ref=land.at[me],
                send_sem=ssem.at[k - 1], recv_sem=rsem.at[k - 1], device_id=(px, py, pc), device_id_type=MESH))
    return copies


def _own_copies(src_refs, land_refs, send_sems, per_dest):
    x, y, c = _place()
    me = 4 * x + 2 * y + c
    return [pltpu.make_async_copy(src.at[me] if per_dest else src, land.at[me], ssem.at[7])
            for src, land, ssem in zip(src_refs, land_refs, send_sems)]


def exchange_start(srcs, per_dest, *, name):
    n = len(srcs)
    lands = [lax.empty(a.shape if per_dest else (N_DEV,) + a.shape, a.dtype) for a in srcs]

    def body(*refs):
        src_refs, land_refs = refs[:n], refs[n:2 * n]
        send_sems, recv_sems = refs[2 * n:3 * n], refs[3 * n:4 * n]
        token = refs[-1]
        for cp in _peer_copies(src_refs, land_refs, send_sems, recv_sems, per_dest):
            cp.start()
        for cp in _own_copies(src_refs, land_refs, send_sems, per_dest):
            cp.start()
        token[...] = jnp.zeros_like(token)

    hbm = lambda a: pltpu.HBM(a.shape, a.dtype)
    res = pl.pallas_call(
        body, name=name,
        out_shape=(*[pltpu.SemaphoreType.DMA((8,))] * n, *[pltpu.SemaphoreType.DMA((7,))] * n,
                   *[hbm(a) for a in srcs], *[hbm(a) for a in lands], jax.ShapeDtypeStruct((8, LANES), F32)),
        in_specs=[_HBM] * (2 * n),
        out_specs=(*[_SEM] * (2 * n), *[_HBM] * (2 * n), pl.BlockSpec(memory_space=pltpu.VMEM)),
        input_output_aliases={i: 2 * n + i for i in range(2 * n)},
        compiler_params=pltpu.CompilerParams(has_side_effects=_EFFECT),
    )(*[pltpu.with_memory_space_constraint(a, pltpu.HBM) for a in (*srcs, *lands)])
    handles = [(res[a], res[n + a], res[2 * n + a], res[3 * n + a]) for a in range(n)]
    return handles, res[-1]


def exchange_wait(handles, per_dest, after, *, name):
    n = len(handles)

    def body(*refs):
        src_refs, land_refs = refs[:n], refs[n:2 * n]
        send_sems, recv_sems = refs[2 * n:3 * n], refs[3 * n:4 * n]
        for cp in _peer_copies(src_refs, land_refs, send_sems, recv_sems, per_dest):
            cp.wait_send()
            cp.wait_recv()
        for cp in _own_copies(src_refs, land_refs, send_sems, per_dest):
            cp.wait()

    srcs, lands = [h[2] for h in handles], [h[3] for h in handles]
    hbm = lambda a: pltpu.HBM(a.shape, a.dtype)
    res = pl.pallas_call(
        body, name=name,
        out_shape=(*[hbm(a) for a in srcs], *[hbm(a) for a in lands]),
        in_specs=[*[_HBM] * (2 * n), *[_SEM] * (2 * n), pl.BlockSpec(memory_space=pl.ANY)],
        out_specs=tuple([_HBM] * (2 * n)),
        input_output_aliases={i: i for i in range(2 * n)},
        compiler_params=pltpu.CompilerParams(has_side_effects=_EFFECT),
    )(*srcs, *lands, *[h[0] for h in handles], *[h[1] for h in handles], after)
    return res[n:]


def _adamw_math(w, g, m, v):
    m = ADAM_B1 * m + (1.0 - ADAM_B1) * g
    v = ADAM_B2 * v + (1.0 - ADAM_B2) * (g * g)
    m_hat = m / (1.0 - ADAM_B1 ** ADAM_STEP)
    v_hat = v / (1.0 - ADAM_B2 ** ADAM_STEP)
    delta = -ADAM_LR * (m_hat / (jnp.sqrt(v_hat) + ADAM_EPS) + ADAM_WD * w)
    return delta, m, v


def adamw_parts(parts, w, m, v, layer, outs, *, name):
    depth, r, cdim = w.shape
    n_parts = parts.shape[0]
    tr = _pick(r, [t for t in (512, 256, 128, 112, 64, 32, 16) if t * cdim <= 256 * 1024])

    def body(p_ref, w_ref, m_ref, v_ref, g0, d0, nm0, nv0, g_ref, d_ref, nm_ref, nv_ref):
        g = p_ref[0].astype(F32)
        for q in range(1, n_parts):
            g = g + p_ref[q].astype(F32)
        delta, nm, nv = _adamw_math(w_ref[...], g, m_ref[...], v_ref[...])
        g_ref[...], d_ref[...], nm_ref[...], nv_ref[...] = g, delta, nm, nv

    t = pl.BlockSpec((None, tr, cdim), lambda i: (layer, i, 0))
    held = pl.BlockSpec(memory_space=pl.ANY)
    return pl.pallas_call(
        body, name=name, grid=(r // tr,),
        in_specs=[pl.BlockSpec((n_parts, tr, cdim), lambda i: (0, i, 0)), t, t, t, held, held, held, held],
        out_specs=[t, t, t, t],
        out_shape=[jax.ShapeDtypeStruct((depth, r, cdim), F32)] * 4,
        input_output_aliases={4: 0, 5: 1, 6: 2, 7: 3},
        compiler_params=_params(("parallel",), VMEM_LIMIT),
    )(parts, w, m, v, *outs)


def sum_devices(gathered, *, name):
    m_rows = gathered.shape[1]

    def body(ga_ref, g_ref):
        g = ga_ref[0]
        for dev in range(1, N_DEV):
            g = g + ga_ref[dev]
        g_ref[...] = g

    return pl.pallas_call(
        body, name=name, out_shape=jax.ShapeDtypeStruct((m_rows, LANES), F32),
        compiler_params=_params(None, VMEM_LIMIT),
    )(gathered)


def adamw_small(g, w, m, v, *, name):
    m_rows = w.shape[0]

    def body(g_ref, w_ref, m_ref, v_ref, d_ref, nm_ref, nv_ref):
        d_ref[...], nm_ref[...], nv_ref[...] = _adamw_math(w_ref[...], g_ref[...], m_ref[...], v_ref[...])

    return pl.pallas_call(
        body, name=name, out_shape=[jax.ShapeDtypeStruct((m_rows, LANES), F32)] * 3,
        compiler_params=_params(None, VMEM_LIMIT),
    )(g, w, m, v)


def _t5_bucket(dist):
    max_exact = N_BUCKETS // 2
    d = jnp.maximum(dist, 0)
    large = max_exact + (jnp.log(jnp.maximum(d, 1).astype(F32) / max_exact)
                         / math.log(T5_MAX_DIST / max_exact) * (N_BUCKETS - max_exact)).astype(I32)
    large = jnp.minimum(large, N_BUCKETS - 1)
    return jnp.where(d < max_exact, d, large)


def _rel():
    return jnp.arange(BLOCK)[:, None] + BLOCK - jnp.arange(2 * BLOCK)[None, :]


def _band_bias(table, dils, max_dists):
    rel = _rel()
    biases, buckets = [], []
    for d, md in zip(dils, max_dists):
        bk = _t5_bucket(rel * d)
        vis = (rel >= 0) & (rel <= md)
        looked_up = jnp.zeros((table.shape[1],) + rel.shape, F32)
        for b in range(N_BUCKETS):
            looked_up = jnp.where((bk == b)[None], table[b][:, None, None], looked_up)
        with_prev = jnp.where(vis[None], looked_up, NEG_INF)
        first = jnp.arange(2 * BLOCK)[None, None, :] >= BLOCK
        biases.append(jnp.stack([with_prev, jnp.where(first, with_prev, NEG_INF)]))
        buckets.append(bk.astype(I32))
    return jnp.stack(biases), jnp.stack(buckets)


def _pack(pieces, rows):
    flat = jnp.concatenate([p.reshape(-1) for p in pieces])
    return jnp.pad(flat, (0, rows * LANES - flat.shape[0])).reshape(rows, LANES)


def _unpack(packed, shapes):
    flat = packed.reshape(-1)
    out, off = [], 0
    for sh in shapes:
        n = math.prod(sh)
        out.append(flat[off:off + n].reshape(sh))
        off += n
    return out


def _tile2(g):
    return jnp.concatenate([g, g])


def kernel(x, attn_norm, w_in, a_q_gain, a_k_gain, a_sinks, c_q_gain, c_k_gain, rel_bias_table, mix_out_gain, w_out, ffn_norm, w_up, conv_w, conv_b, w_down, loss_target, m_attn_norm, m_w_in, m_a_q_gain, m_a_k_gain, m_a_sinks, m_c_q_gain, m_c_k_gain, m_rel_bias_table, m_mix_out_gain, m_w_out, m_ffn_norm, m_w_up, m_conv_w, m_conv_b, m_w_down, v_attn_norm, v_w_in, v_a_q_gain, v_a_k_gain, v_a_sinks, v_c_q_gain, v_c_k_gain, v_rel_bias_table, v_mix_out_gain, v_w_out, v_ffn_norm, v_w_up, v_conv_w, v_conv_b, v_w_down):
    depth, d_model, in_shard = w_in.shape
    ff2_shard = w_up.shape[2]
    s = x.shape[1]
    in_width, ff2 = N_DEV * in_shard, N_DEV * ff2_shard
    n_heads = d_model // HEAD_DIM
    ha, hb, hc = n_heads // 4, n_heads // 4, n_heads // 2
    sa, sb, sc = ha // 2, hb // 2, hc // 2
    kv_a = ha // 4
    assert kv_a == 2 and BLOCK == LANES
    cb_aq, cb_ak, cb_av = 0, sa, sa + 1
    cb_bq = sa + 2
    cb_bk, cb_bv = cb_bq + sb, cb_bq + 2 * sb
    cb_cq = cb_bq + 3 * sb
    cb_ck, cb_cv = cb_cq + sc, cb_cq + 2 * sc
    assert (cb_cv + sc) * LANES == in_width
    dev = 4 * lax.axis_index("x") + 2 * lax.axis_index("y") + lax.axis_index("c")

    per_array = 3
    wnames = ("w_in", "w_out", "w_up", "w_down", "conv_w")
    sent = dict(w_in=lambda w: w.T, w_up=lambda w: w.T, w_out=lambda w: w, w_down=lambda w: w, conv_w=lambda w: w)
    rows = lambda g: g.reshape(N_DEV * g.shape[1], g.shape[2])
    whole = dict(w_in=rows, w_up=rows, w_out=rows, w_down=rows,
                 conv_w=lambda g: jnp.transpose(g, (1, 0, 2)).reshape(g.shape[1], N_DEV * g.shape[2]))
    gathers = {}
    token = jnp.zeros((8, LANES), F32)
    for l in range(depth):
        for gi, group in enumerate([[n] for n in wnames] if l < per_array else [wnames]):
            srcs = [(sent[n](dict(w_in=w_in, w_out=w_out, w_up=w_up, w_down=w_down, conv_w=conv_w)[n][l])
                     + token[0, 0]).astype(F32 if n == "conv_w" else BF16) for n in group]
            handles, token = exchange_start(srcs, False, name=f"gather_start_{l}_{gi}")
            gathers.update({(l, n): h for n, h in zip(group, handles)})

    def gathered(l, names, after):
        landed = exchange_wait([gathers[l, n] for n in names], False, after,
                               name=f"gather_wait_{l}_{wnames.index(names[0])}")
        return {n: whole[n](g) for n, g in zip(names, landed)}

    bias_a, buckets_a = _band_bias(rel_bias_table[:, :ha], (1,), (WINDOW_A - 1,))
    bias_c, buckets_c = _band_bias(rel_bias_table[:, ha:], DILATIONS, (BLOCK,) * len(DILATIONS))

    xs = x[0]
    saved = []
    wi, wo, wu, wd, cw = ([None] * depth for _ in range(5))
    for l in range(depth):
        if l < per_array:
            need = lambda n, after, l=l: gathered(l, (n,), after)[n]
        else:
            layer_w = gathered(l, wnames, xs)
            need = lambda n, after: layer_w[n]
        wi[l] = need("w_in", token if l == 0 else xs)
        h1 = rmsnorm_fwd(xs, attn_norm[l], name="attn_norm_fwd")
        proj = matmul(h1, wi[l], trans_b=True, name="in_proj")
        sinks = jnp.repeat(a_sinks[l], HEAD_DIM).reshape(sa, 1, LANES)
        gaq, gak = _tile2(a_q_gain[l]), _tile2(a_k_gain[l])
        gcq, gck = _tile2(c_q_gain[l]), _tile2(c_k_gain[l])
        out_a, lse_a = banded_fwd(proj, cb_aq, cb_ak, cb_av, sa, gaq, gak, bias_a, (1,), sinks, True, name="swa_fwd")
        out_b, tot_b = sb_fwd(proj, cb_bq, cb_bk, cb_bv, sb, name="stick_fwd")
        out_c, lse_c = banded_fwd(proj, cb_cq, cb_ck, cb_cv, sc, gcq, gck, bias_c, DILATIONS, None, False,
                                  name="dilated_fwd")
        mix = mixnorm_fwd([out_a, out_b, out_c], mix_out_gain[l], name="mix_norm_fwd")
        wo[l] = need("w_out", mix)
        x_mid = matmul(mix, wo[l], res=xs, name="out_proj")
        h2 = rmsnorm_fwd(x_mid, ffn_norm[l], name="ffn_norm_fwd")
        wu[l] = need("w_up", h2)
        p = matmul(h2, wu[l], trans_b=True, name="up_proj")
        cw[l] = need("conv_w", p)
        act = ffn_act_fwd(p, cw[l], conv_b[l], name="ffn_act_fwd")
        wd[l] = need("w_down", act)
        x_out = matmul(act, wd[l], res=x_mid, name="down_proj")
        saved.append(dict(x_in=xs, h1=h1, proj=proj, out_a=out_a, lse_a=lse_a, out_b=out_b, tot_b=tot_b,
                          out_c=out_c, lse_c=lse_c, mix=mix, x_mid=x_mid, h2=h2, p=p, act=act,
                          sinks=sinks, gains=(gaq, gak, gcq, gck)))
        xs = x_out

    dx, dx_b, loss_part = loss_head(xs, loss_target[0], name="loss_head")

    small = {k: [None] * depth for k in ("attn_norm", "a_q_gain", "a_k_gain", "a_sinks", "c_q_gain", "c_k_gain",
                                         "mix_out_gain", "ffn_norm", "conv_w", "conv_b")}
    big = {k: [None] * depth for k in ("w_in", "w_out", "w_up", "w_down")}
    dbias_a = dbias_c = None
    scatters = {}
    token = jnp.zeros((8, LANES), F32)
    names_big = ("w_in", "w_out", "w_up", "w_down")

    def scatter(l, names):
        parts = [big[n][l] for n in names]
        handles, tok = exchange_start(parts, True, name=f"scatter_start_{l}_{names_big.index(names[0])}")
        scatters.update({(l, n): h for n, h in zip(names, handles)})
        return tok

    by_rows = lambda a: a.reshape(N_DEV, a.shape[0] // N_DEV, a.shape[1])
    for l in reversed(range(depth)):
        each = l == 0
        sv = saved[l]
        gaq, gak, gcq, gck = sv["gains"]
        da = matmul(dx_b, wd[l], trans_b=True, name="down_proj_dx")
        big["w_down"][l] = by_rows(matmul(sv["act"], dx_b, trans_a=True, out_dtype=BF16, name="down_proj_dw"))
        if each:
            token = scatter(l, ("w_down",))
        dp, small["conv_w"][l], small["conv_b"][l] = ffn_act_bwd(da, sv["p"], cw[l], conv_b[l] + token[0, 0],
                                                                 name="ffn_act_bwd")
        dh2 = matmul(dp, wu[l], name="up_proj_dx")
        big["w_up"][l] = matmul(sv["h2"], dp, trans_a=True, out_dtype=BF16, col_blocks=N_DEV, name="up_proj_dw")
        if each:
            token = scatter(l, ("w_up",))
        dx_mid, dx_mid_b, small["ffn_norm"][l] = rmsnorm_bwd(dh2, sv["x_mid"], ffn_norm[l] + token[0, 0], dx,
                                                   name="ffn_norm_bwd")
        dmix = matmul(dx_mid_b, wo[l], trans_b=True, name="out_proj_dx")
        big["w_out"][l] = by_rows(matmul(sv["mix"], dx_mid_b, trans_a=True, out_dtype=BF16, name="out_proj_dw"))
        if each:
            token = scatter(l, ("w_out",))
        (d_oa, d_ob, d_oc), small["mix_out_gain"][l] = mixnorm_bwd(
            dmix, [sv["out_a"], sv["out_b"], sv["out_c"]], mix_out_gain[l] + token[0, 0], name="mix_norm_bwd")
        dproj = lax.empty((s, in_width), BF16)
        dproj, db_a, dgq_a, dgk_a, dsink = banded_bwd(
            sv["proj"], cb_aq, cb_ak, cb_av, sa, gaq, gak, bias_a, (1,), sv["sinks"], True,
            d_oa, sv["out_a"], sv["lse_a"], dproj, name="swa_bwd")
        dproj = sb_bwd(sv["proj"], cb_bq, cb_bk, cb_bv, sb, d_ob, sv["tot_b"], dproj, name="stick_bwd")
        dproj, db_c, dgq_c, dgk_c = banded_bwd(
            sv["proj"], cb_cq, cb_ck, cb_cv, sc, gcq, gck, bias_c, DILATIONS, None, False,
            d_oc, sv["out_c"], sv["lse_c"], dproj, name="dilated_bwd")
        fold = lambda g: g.reshape(-1, HEAD_DIM).sum(axis=0)
        small["a_q_gain"][l], small["a_k_gain"][l] = fold(dgq_a), fold(dgk_a)
        small["c_q_gain"][l], small["c_k_gain"][l] = fold(dgq_c), fold(dgk_c)
        small["a_sinks"][l] = dsink[:, ::HEAD_DIM].reshape(-1)
        dbias_a = db_a if dbias_a is None else dbias_a + db_a
        dbias_c = db_c if dbias_c is None else dbias_c + db_c
        big["w_in"][l] = by_rows(matmul(dproj, sv["h1"], trans_a=True, out_dtype=BF16, name="in_proj_dw"))
        token = scatter(l, ("w_in",) if each else names_big)
        dh1 = matmul(dproj, wi[l], name="in_proj_dx")
        dx, dx_b, small["attn_norm"][l] = rmsnorm_bwd(dh1, sv["x_in"], attn_norm[l] + token[0, 0], dx_mid,
                                                name="attn_norm_bwd")

    flip = lambda t: jnp.swapaxes(t, 1, 2)
    w_big = dict(w_in=(flip(w_in), flip(m_w_in), flip(v_w_in)), w_out=(w_out, m_w_out, v_w_out),
                 w_up=(w_up, m_w_up, v_w_up), w_down=(w_down, m_w_down, v_w_down))
    results = {k: [lax.empty(w_big[k][0].shape, F32) for _ in range(4)] for k in names_big}

    def update(l, names, after):
        landed = exchange_wait([scatters[l, n] for n in names], True, after,
                               name=f"scatter_wait_{l}_{names_big.index(names[0])}")
        for k, parts in zip(names, landed):
            results[k] = adamw_parts(parts, *w_big[k], l, results[k], name="adamw_large")
        return results[names[-1]][0]

    after = dx_b
    for l in reversed(range(1, depth)):
        after = update(l, names_big, after)

    dtable = jnp.concatenate([bias_bwd(dbias_a, buckets_a, name="swa_bias_bwd"),
                              bias_bwd(dbias_c, buckets_c, name="dilated_bias_bwd")], axis=1)

    order = ("attn_norm", "a_q_gain", "a_k_gain", "a_sinks", "c_q_gain", "c_k_gain", "rel_bias_table",
             "mix_out_gain", "ffn_norm", "conv_w", "conv_b")
    partial = {k: jnp.stack(v) for k, v in small.items()}
    partial["rel_bias_table"] = dtable
    pieces = [partial[k] for k in order] + [loss_part.reshape(1)]
    n_small = sum(math.prod(pc.shape) for pc in pieces)
    rows = -(-n_small // (8 * LANES)) * 8
    packed, after = lax.optimization_barrier((_pack(pieces, rows), after))
    gathered = gather_small(packed, name="gather_small_grads")
    summed = _unpack(sum_devices(gathered.reshape(N_DEV, rows, LANES), name="sum_small_grads"),
                     [pc.shape for pc in pieces])
    g_small = dict(zip(order, summed[:-1]))
    loss = summed[-1][0]
    g_small["conv_w"] = lax.dynamic_slice_in_dim(g_small["conv_w"], dev * ff2_shard, ff2_shard, axis=2)

    w_small = dict(attn_norm=attn_norm, a_q_gain=a_q_gain, a_k_gain=a_k_gain, a_sinks=a_sinks, c_q_gain=c_q_gain,
                   c_k_gain=c_k_gain, rel_bias_table=rel_bias_table, mix_out_gain=mix_out_gain, ffn_norm=ffn_norm,
                   conv_w=conv_w, conv_b=conv_b)
    m_small = dict(attn_norm=m_attn_norm, a_q_gain=m_a_q_gain, a_k_gain=m_a_k_gain, a_sinks=m_a_sinks,
                   c_q_gain=m_c_q_gain, c_k_gain=m_c_k_gain, rel_bias_table=m_rel_bias_table,
                   mix_out_gain=m_mix_out_gain, ffn_norm=m_ffn_norm, conv_w=m_conv_w, conv_b=m_conv_b)
    v_small = dict(attn_norm=v_attn_norm, a_q_gain=v_a_q_gain, a_k_gain=v_a_k_gain, a_sinks=v_a_sinks,
                   c_q_gain=v_c_q_gain, c_k_gain=v_c_k_gain, rel_bias_table=v_rel_bias_table,
                   mix_out_gain=v_mix_out_gain, ffn_norm=v_ffn_norm, conv_w=v_conv_w, conv_b=v_conv_b)
    shapes = [w_small[k].shape for k in order]
    n_upd = sum(math.prod(sh) for sh in shapes)
    urows = -(-n_upd // (8 * LANES)) * 8
    packs = [_pack([d[k] for k in order], urows) for d in (g_small, w_small, m_small, v_small)]
    upd = adamw_small(*packs, name="adamw_small")
    delta_s, newm_s, newv_s = [dict(zip(order, _unpack(u, shapes))) for u in upd]

    after = update(0, names_big[1:], upd[0])
    update(0, names_big[:1], after)
    results["w_in"] = [flip(t) for t in results["w_in"]]
    g_big, delta_b, newm_b, newv_b = [{k: results[k][i] for k in names_big} for i in range(4)]

    all_names = ("attn_norm", "w_in", "a_q_gain", "a_k_gain", "a_sinks", "c_q_gain", "c_k_gain", "rel_bias_table",
                 "mix_out_gain", "w_out", "ffn_norm", "w_up", "conv_w", "conv_b", "w_down")
    pick = lambda sm, bg: [bg[k] if k in bg else sm[k] for k in all_names]
    return (loss, dx[None], *pick(g_small, g_big), *pick(delta_s, delta_b), *pick(newm_s, newm_b),
            *pick(newv_s, newv_b))
```

```python
import functools
import math

import jax
import jax.numpy as jnp
from jax import lax
from jax.experimental import pallas as pl
from jax.experimental.pallas import tpu as pltpu

F32, BF16, I32 = jnp.float32, jnp.bfloat16, jnp.int32
MESH = pl.DeviceIdType.MESH

HEAD_DIM = 64
LANES = 128
BLOCK = 128
EPS = 1e-6
NEG_INF = -1e30
N_BUCKETS = 32
T5_MAX_DIST = 2048
WINDOW_A = 128
DILATIONS = (1, 4, 16)
N_DEV = 8
VMEM_LIMIT = 56 * 1024 * 1024
MATMUL_VMEM = 46 * 1024 * 1024

ADAM_LR, ADAM_B1, ADAM_B2, ADAM_EPS, ADAM_WD, ADAM_STEP = 0.001, 0.9, 0.999, 1e-08, 0.01, 10


def _params(sem=None, vmem=None):
    return pltpu.CompilerParams(dimension_semantics=sem, vmem_limit_bytes=vmem)


def _pick(n, cands):
    for c in cands:
        if n % c == 0:
            return c
    raise ValueError(f"no tile for {n}")


def _dot(a, b):
    return lax.dot_general(a, b, (((1,), (0,)), ((), ())), preferred_element_type=F32)


def _dot_nt(a, b):
    return lax.dot_general(a, b, (((1,), (1,)), ((), ())), preferred_element_type=F32)


def _dot_tn(a, b):
    return lax.dot_general(a, b, (((0,), (0,)), ((), ())), preferred_element_type=F32)


def matmul(a, b, *, trans_a=False, trans_b=False, out_dtype=F32, res=None, col_blocks=None, name):
    a_halves, b_halves = a.ndim == 3, b.ndim == 3
    assert not (a_halves and trans_a) and not (b_halves and trans_b)
    m, k = (a.shape[1], 2 * a.shape[2]) if a_halves else (a.shape[1], a.shape[0]) if trans_a else a.shape
    n = 2 * b.shape[2] if b_halves else b.shape[0] if trans_b else b.shape[1]
    k_unit, n_unit = (k // 2 if a_halves else k), (n // 2 if b_halves else n)
    tm = _pick(m, (1408, 1024, 896, 512, 256))
    tn_cands = ((n // col_blocks,) if col_blocks
                else tuple(t for t in (1024, 1408, 768, 512, 256, 128) if n_unit % t == 0))

    def footprint(tk, tn):
        tiles = 2 * (tm * tk * a.dtype.itemsize + tk * tn * b.dtype.itemsize)
        return tiles + tm * tn * (4 + 2 * jnp.dtype(out_dtype).itemsize + (8 if res is not None else 0))

    tk, tn = next((tk, tn) for tk in (5376, 4096, 2816, 2048, 1792, 1024, 768, 512, 256) if k_unit % tk == 0
                  for tn in tn_cands if footprint(tk, tn) <= MATMUL_VMEM)
    nk = k // tk
    nk_half, nj_half = k_unit // tk, n_unit // tn
    dn = (((0 if trans_a else 1,), (1 if trans_b else 0,)), ((), ()))

    def body(*refs):
        if res is None:
            a_ref, b_ref, o_ref, acc = refs
        else:
            a_ref, b_ref, r_ref, o_ref, acc = refs
        kk = pl.program_id(2)

        @pl.when(kk == 0)
        def _():
            acc[...] = jnp.zeros_like(acc)

        acc[...] += lax.dot_general(a_ref[...].astype(BF16), b_ref[...].astype(BF16), dn,
                                    preferred_element_type=F32)

        @pl.when(kk == nk - 1)
        def _():
            r = acc[...]
            if res is not None:
                r = r_ref[...] + r
            o_ref[...] = r.astype(out_dtype)

    b_spec = (pl.BlockSpec((tn, tk), lambda i, j, kk: (j, kk)) if trans_b
              else pl.BlockSpec((None, tk, tn), lambda i, j, kk: (j // nj_half, kk, j % nj_half)) if b_halves
              else pl.BlockSpec((tk, tn), lambda i, j, kk: (kk, j)))
    a_spec = (pl.BlockSpec((tk, tm), lambda i, j, kk: (kk, i)) if trans_a
              else pl.BlockSpec((None, tm, tk), lambda i, j, kk: (kk // nk_half, i, kk % nk_half)) if a_halves
              else pl.BlockSpec((tm, tk), lambda i, j, kk: (i, kk)))
    in_specs = [a_spec, b_spec]
    args = [a, b]
    if res is not None:
        in_specs.append(pl.BlockSpec((tm, tn), lambda i, j, kk: (i, j)))
        args.append(res)
    if col_blocks:
        out_spec = pl.BlockSpec((None, tm, tn), lambda i, j, kk: (j, i, 0))
        out_shape = jax.ShapeDtypeStruct((col_blocks, m, tn), out_dtype)
    else:
        out_spec = pl.BlockSpec((tm, tn), lambda i, j, kk: (i, j))
        out_shape = jax.ShapeDtypeStruct((m, n), out_dtype)
    return pl.pallas_call(
        body, name=name, grid=(m // tm, n // tn, nk),
        in_specs=in_specs, out_specs=out_spec, out_shape=out_shape,
        scratch_shapes=[pltpu.VMEM((tm, tn), F32)],
        compiler_params=_params(("parallel", "parallel", "arbitrary"), VMEM_LIMIT),
    )(*args)


def rmsnorm_fwd(x, g, *, name):
    s, d = x.shape
    tm = 512

    def body(x_ref, g_ref, o_ref):
        xv = x_ref[...]
        r = lax.rsqrt(jnp.mean(xv * xv, axis=-1, keepdims=True) + EPS)
        o_ref[...] = (xv * r * g_ref[...]).astype(BF16)

    return pl.pallas_call(
        body, name=name, grid=(s // tm,),
        in_specs=[pl.BlockSpec((tm, d), lambda i: (i, 0)), pl.BlockSpec((1, d), lambda i: (0, 0))],
        out_specs=pl.BlockSpec((tm, d), lambda i: (i, 0)),
        out_shape=jax.ShapeDtypeStruct((s, d), BF16),
        compiler_params=_params(("parallel",)),
    )(x, g.reshape(1, d))


def rmsnorm_bwd(dh, x, g, dres, *, name):
    s, d = x.shape
    tm = 512

    def body(dh_ref, x_ref, g_ref, dres_ref, dx_ref, dxb_ref, dg_ref):
        @pl.when(pl.program_id(0) == 0)
        def _():
            dg_ref[...] = jnp.zeros_like(dg_ref)

        xv, dhv = x_ref[...], dh_ref[...]
        r = lax.rsqrt(jnp.mean(xv * xv, axis=-1, keepdims=True) + EPS)
        gd = dhv * g_ref[...]
        dot = jnp.mean(gd * xv, axis=-1, keepdims=True)
        dx = dres_ref[...] + (r * gd - xv * (r * r * r * dot))
        dx_ref[...] = dx
        dxb_ref[...] = dx.astype(BF16)
        dg_ref[...] += jnp.sum(dhv * (xv * r), axis=0, keepdims=True)

    row = pl.BlockSpec((tm, d), lambda i: (i, 0))
    dx, dxb, dg = pl.pallas_call(
        body, name=name, grid=(s // tm,),
        in_specs=[row, row, pl.BlockSpec((1, d), lambda i: (0, 0)), row],
        out_specs=[row, row, pl.BlockSpec((1, d), lambda i: (0, 0))],
        out_shape=[jax.ShapeDtypeStruct((s, d), F32), jax.ShapeDtypeStruct((s, d), BF16),
                   jax.ShapeDtypeStruct((1, d), F32)],
        compiler_params=_params(("arbitrary",), VMEM_LIMIT),
    )(dh, x, g.reshape(1, d), dres)
    return dx, dxb, dg[0]


def loss_head(y, target, *, name):
    s, d = y.shape
    tm = 512

    def body(y_ref, t_ref, dy_ref, dyb_ref, l_ref):
        @pl.when(pl.program_id(0) == 0)
        def _():
            l_ref[...] = jnp.zeros_like(l_ref)

        e = y_ref[...] - t_ref[...]
        dy = e / float(d)
        dy_ref[...] = dy
        dyb_ref[...] = dy.astype(BF16)
        per_tok = jnp.mean(e * e, axis=-1, keepdims=True)
        l_ref[...] += 0.5 * jnp.sum(per_tok, axis=0, keepdims=True)

    row = pl.BlockSpec((tm, d), lambda i: (i, 0))
    dy, dyb, l = pl.pallas_call(
        body, name=name, grid=(s // tm,),
        in_specs=[row, row],
        out_specs=[row, row, pl.BlockSpec((8, LANES), lambda i: (0, 0))],
        out_shape=[jax.ShapeDtypeStruct((s, d), F32), jax.ShapeDtypeStruct((s, d), BF16),
                   jax.ShapeDtypeStruct((8, LANES), F32)],
        compiler_params=_params(("arbitrary",)),
    )(y, target)
    return dy, dyb, l[0, 0]


FFN_TN = 256
FFN_CH = 256


def _rows_before(ref, r0, first):
    if first:
        cur = ref[pl.ds(0, FFN_CH), :]
        row = lax.broadcasted_iota(I32, cur.shape, 0)
        sh1 = jnp.where(row < 1, 0.0, pltpu.roll(cur, 1, axis=0))
        sh2 = jnp.where(row < 2, 0.0, pltpu.roll(cur, 2, axis=0))
        return cur, sh1, sh2
    ext = ref[pl.ds(pl.multiple_of(r0 - 8, 8), FFN_CH + 8), :]
    return ext[8:], pltpu.roll(ext, 1, axis=0)[8:], pltpu.roll(ext, 2, axis=0)[8:]


def _rows_after(ref, r0, last):
    if last:
        cur = ref[pl.ds(r0, FFN_CH), :]
        row = lax.broadcasted_iota(I32, cur.shape, 0)
        up1 = jnp.where(row >= FFN_CH - 1, 0.0, pltpu.roll(cur, FFN_CH - 1, axis=0))
        up2 = jnp.where(row >= FFN_CH - 2, 0.0, pltpu.roll(cur, FFN_CH - 2, axis=0))
        return cur, up1, up2
    n = FFN_CH + 8
    ext = ref[pl.ds(r0, n), :]
    return ext[:FFN_CH], pltpu.roll(ext, n - 1, axis=0)[:FFN_CH], pltpu.roll(ext, n - 2, axis=0)[:FFN_CH]


def _sigmoid(x):
    return 0.5 * jnp.tanh(0.5 * x) + 0.5


def ffn_act_fwd(p, conv_w, conv_b, *, name):
    s, f2 = p.shape
    f = f2 // 2
    nj = f // FFN_TN
    nch = s // FFN_CH

    def body(pg_ref, pu_ref, wg_ref, wu_ref, bg_ref, bu_ref, a_ref):
        def conv(ref, w_ref, b_ref, r0, first):
            cur, sh1, sh2 = _rows_before(ref, r0, first)
            return ((b_ref[...] + w_ref[0:1, :] * sh2) + w_ref[1:2, :] * sh1) + w_ref[2:3, :] * cur

        def chunk(r0, first):
            gate = conv(pg_ref, wg_ref, bg_ref, r0, first)
            up = conv(pu_ref, wu_ref, bu_ref, r0, first)
            a_ref[pl.ds(r0, FFN_CH), :] = (gate * _sigmoid(gate) * up).astype(BF16)

        chunk(0, True)

        def step(c, carry):
            chunk(pl.multiple_of(c * FFN_CH, FFN_CH), False)
            return carry

        lax.fori_loop(1, nch, step, 0)

    col = lambda off: pl.BlockSpec((s, FFN_TN), lambda j: (0, j + off))
    wcol = lambda off: pl.BlockSpec((3, FFN_TN), lambda j: (0, j + off))
    bcol = lambda off: pl.BlockSpec((1, FFN_TN), lambda j: (0, j + off))
    return pl.pallas_call(
        body, name=name, grid=(nj,),
        in_specs=[col(0), col(nj), wcol(0), wcol(nj), bcol(0), bcol(nj)],
        out_specs=pl.BlockSpec((s, FFN_TN), lambda j: (0, j)),
        out_shape=jax.ShapeDtypeStruct((s, f), BF16),
        compiler_params=_params(("parallel",), VMEM_LIMIT),
    )(p, p, conv_w, conv_w, conv_b.reshape(1, f2), conv_b.reshape(1, f2))


def ffn_act_bwd(da, p, conv_w, conv_b, *, name):
    s, f2 = p.shape
    f = f2 // 2
    nj = f // FFN_TN
    nch = s // FFN_CH

    def body(da_ref, pg_ref, pu_ref, wg_ref, wu_ref, bg_ref, bu_ref,
             dp_ref, dwg_ref, dwu_ref, dbg_ref, dbu_ref, dug_s, duu_s):
        dpg_ref, dpu_ref = dp_ref.at[0], dp_ref.at[1]
        def conv(ref, w_ref, b_ref, r0, first):
            cur, sh1, sh2 = _rows_before(ref, r0, first)
            u = ((b_ref[...] + w_ref[0:1, :] * sh2) + w_ref[1:2, :] * sh1) + w_ref[2:3, :] * cur
            return u, (sh2, sh1, cur)

        def taps_sum(du, taps):
            return jnp.concatenate([jnp.sum(du * t, axis=0, keepdims=True) for t in taps], axis=0)

        def chunk(r0, first, acc):
            dwg, dwu, dbg, dbu = acc
            gate, tg = conv(pg_ref, wg_ref, bg_ref, r0, first)
            up, tu = conv(pu_ref, wu_ref, bu_ref, r0, first)
            dav = da_ref[pl.ds(r0, FFN_CH), :]
            sg = _sigmoid(gate)
            dgate = dav * up * (sg * (1.0 + gate * (1.0 - sg)))
            dup = dav * (gate * sg)
            dug_s[pl.ds(r0, FFN_CH), :] = dgate
            duu_s[pl.ds(r0, FFN_CH), :] = dup
            return (dwg + taps_sum(dgate, tg), dwu + taps_sum(dup, tu),
                    dbg + jnp.sum(dgate, axis=0, keepdims=True), dbu + jnp.sum(dup, axis=0, keepdims=True))

        z3 = jnp.zeros((3, FFN_TN), F32)
        z1 = jnp.zeros((1, FFN_TN), F32)
        acc = chunk(0, True, (z3, z3, z1, z1))
        acc = lax.fori_loop(1, nch, lambda c, a: chunk(pl.multiple_of(c * FFN_CH, FFN_CH), False, a), acc)
        dwg_ref[...], dwu_ref[...], dbg_ref[...], dbu_ref[...] = acc

        def back(src, w_ref, dst, r0, last):
            cur, up1, up2 = _rows_after(src, r0, last)
            dst[pl.ds(r0, FFN_CH), :] = (w_ref[2:3, :] * cur + w_ref[1:2, :] * up1 + w_ref[0:1, :] * up2).astype(BF16)

        def step(c, carry):
            r0 = pl.multiple_of(c * FFN_CH, FFN_CH)
            back(dug_s, wg_ref, dpg_ref, r0, False)
            back(duu_s, wu_ref, dpu_ref, r0, False)
            return carry

        lax.fori_loop(0, nch - 1, step, 0)
        back(dug_s, wg_ref, dpg_ref, (nch - 1) * FFN_CH, True)
        back(duu_s, wu_ref, dpu_ref, (nch - 1) * FFN_CH, True)

    col = lambda off: pl.BlockSpec((s, FFN_TN), lambda j: (0, j + off))
    wcol = lambda off: pl.BlockSpec((3, FFN_TN), lambda j: (0, j + off))
    bcol = lambda off: pl.BlockSpec((1, FFN_TN), lambda j: (0, j + off))
    outs = pl.pallas_call(
        body, name=name, grid=(nj,),
        in_specs=[col(0), col(0), col(nj), wcol(0), wcol(nj), bcol(0), bcol(nj)],
        out_specs=[pl.BlockSpec((2, s, FFN_TN), lambda j: (0, 0, j)), wcol(0), wcol(0), bcol(0), bcol(0)],
        out_shape=[jax.ShapeDtypeStruct((2, s, f), BF16),
                   jax.ShapeDtypeStruct((3, f), F32), jax.ShapeDtypeStruct((3, f), F32),
                   jax.ShapeDtypeStruct((1, f), F32), jax.ShapeDtypeStruct((1, f), F32)],
        scratch_shapes=[pltpu.VMEM((s, FFN_TN), F32), pltpu.VMEM((s, FFN_TN), F32)],
        compiler_params=_params(("parallel",), VMEM_LIMIT),
    )(da, p, p, conv_w, conv_w, conv_b.reshape(1, f2), conv_b.reshape(1, f2))
    dp, dwg, dwu, dbg, dbu = outs
    return dp, jnp.concatenate([dwg, dwu], axis=1), jnp.concatenate([dbg, dbu], axis=1)[0]


def mixnorm_fwd(outs, gain, *, name):
    s = outs[0].shape[0]
    widths = [o.shape[1] for o in outs]
    total = sum(widths)
    tm = 512

    def body(*refs):
        o_refs, g_ref, m_ref = refs[:-2], refs[-2], refs[-1]
        off = 0
        for o_ref, w in zip(o_refs, widths):
            xv = o_ref[...]
            r = lax.rsqrt(jnp.mean(xv * xv, axis=-1, keepdims=True) + EPS)
            m_ref[:, off:off + w] = (xv * r * g_ref[:, off:off + w]).astype(BF16)
            off += w

    return pl.pallas_call(
        body, name=name, grid=(s // tm,),
        in_specs=[pl.BlockSpec((tm, w), lambda i: (i, 0)) for w in widths] + [pl.BlockSpec((1, total), lambda i: (0, 0))],
        out_specs=pl.BlockSpec((tm, total), lambda i: (i, 0)),
        out_shape=jax.ShapeDtypeStruct((s, total), BF16),
        compiler_params=_params(("parallel",)),
    )(*outs, gain.reshape(1, total))


def mixnorm_bwd(dmix, outs, gain, *, name):
    s = outs[0].shape[0]
    widths = [o.shape[1] for o in outs]
    total = sum(widths)
    n = len(outs)
    tm = 512

    def body(*refs):
        dm_ref, o_refs, g_ref = refs[0], refs[1:1 + n], refs[1 + n]
        d_refs, dg_ref = refs[2 + n:2 + 2 * n], refs[2 + 2 * n]

        @pl.when(pl.program_id(0) == 0)
        def _():
            dg_ref[...] = jnp.zeros_like(dg_ref)

        off = 0
        for o_ref, d_ref, w in zip(o_refs, d_refs, widths):
            xv = o_ref[...]
            dhv = dm_ref[:, off:off + w]
            r = lax.rsqrt(jnp.mean(xv * xv, axis=-1, keepdims=True) + EPS)
            gd = dhv * g_ref[:, off:off + w]
            dot = jnp.mean(gd * xv, axis=-1, keepdims=True)
            d_ref[...] = r * gd - xv * (r * r * r * dot)
            dg_ref[:, off:off + w] += jnp.sum(dhv * (xv * r), axis=0, keepdims=True)
            off += w

    res = pl.pallas_call(
        body, name=name, grid=(s // tm,),
        in_specs=[pl.BlockSpec((tm, total), lambda i: (i, 0))]
        + [pl.BlockSpec((tm, w), lambda i: (i, 0)) for w in widths] + [pl.BlockSpec((1, total), lambda i: (0, 0))],
        out_specs=[pl.BlockSpec((tm, w), lambda i: (i, 0)) for w in widths] + [pl.BlockSpec((1, total), lambda i: (0, 0))],
        out_shape=[jax.ShapeDtypeStruct((s, w), F32) for w in widths] + [jax.ShapeDtypeStruct((1, total), F32)],
        compiler_params=_params(("arbitrary",)),
    )(dmix, *outs, gain.reshape(1, total))
    return res[:n], res[n][0]


NORM_CH = 512
FWD_TILES = 4
BWD_TILES = 4


def _lo_mask(shape):
    return lax.broadcasted_iota(I32, shape, 1) < HEAD_DIM


def _head_sum(x, lo):
    del lo
    i = lax.broadcasted_iota(I32, (LANES, LANES), 0) // HEAD_DIM
    j = lax.broadcasted_iota(I32, (LANES, LANES), 1) // HEAD_DIM
    return _split_dot(x, _twice(i == j))


def _head_stats(x, lo):
    return lax.rsqrt(_head_sum(x * x, lo) * (1.0 / HEAD_DIM) + EPS)


def _swap_halves(x):
    return pltpu.roll(x, HEAD_DIM, axis=1)


def _replicate_head(x, lo, use_lo_head):
    sw = _swap_halves(x)
    return jnp.where(use_lo_head, jnp.where(lo, x, sw), jnp.where(lo, sw, x))


def _tile_rows(i, s, d):
    nb = s // (BLOCK * d)
    r = i // nb
    b = i % nb
    start = r + (BLOCK * d) * b
    prev = start - (BLOCK * d) * jnp.minimum(b, 1)
    return start, prev, b > 0


def _rows(ref, start, d):
    if d == 1:
        return ref[pl.ds(pl.multiple_of(start, BLOCK), BLOCK), :]
    return ref[pl.ds(start, BLOCK, stride=d), :]


def _set_rows(ref, start, d, val):
    if d == 1:
        ref[pl.ds(pl.multiple_of(start, BLOCK), BLOCK), :] = val
    else:
        ref[pl.ds(start, BLOCK, stride=d), :] = val


def banded_fwd(proj, qb0, kb0, vb0, n_slabs, gq, gk, bias, dils, sinks, gqa, *, name):
    s = proj.shape[0]
    nbr = len(dils)
    nt = s // BLOCK
    nch = s // NORM_CH
    has_sink = sinks is not None

    def body(*refs):
        q_ref, k_ref, v_ref, gq_ref, gk_ref, b_ref = refs[:6]
        rest = refs[6:]
        if has_sink:
            sink_ref, rest = rest[0], rest[1:]
        out_ref, lse_ref, qn_s, kn_s, vv_s, o_s, l_s = rest
        p = pl.program_id(0)
        use_lo = (p // 2) == 0

        def prep(c, carry):
            rows = pl.ds(pl.multiple_of(c * NORM_CH, NORM_CH), NORM_CH)
            lo = _lo_mask((NORM_CH, LANES))
            qv, kv, vv = q_ref[rows, :], k_ref[rows, :], v_ref[rows, :]
            qn_s[rows, :] = qv * _head_stats(qv, lo) * gq_ref[...] * (HEAD_DIM ** -0.5)
            kn = kv * _head_stats(kv, lo) * gk_ref[...]
            if gqa:
                kn = _replicate_head(kn, lo, use_lo)
                vv = _replicate_head(vv, lo, use_lo)
            kn_s[rows, :] = kn
            vv_s[rows, :] = vv
            return carry

        lax.fori_loop(0, nch, prep, 0)

        lo = _lo_mask((BLOCK, LANES))
        hms = [lo, jnp.logical_not(lo)]
        heads, tiles = range(2), range(FWD_TILES)
        for br, d in enumerate(dils):
            def step(ii, carry, br=br, d=d):
                pos = [_tile_rows(ii * FWD_TILES + u, s, d) for u in tiles]
                kc = [carry[0]] + [_rows(kn_s, pos[u][0], d).astype(BF16) for u in tiles]
                vc = [carry[1]] + [_rows(vv_s, pos[u][0], d).astype(BF16) for u in tiles]
                kcat = [jnp.concatenate([kc[u], kc[u + 1]], axis=0) for u in tiles]
                vcat = [jnp.concatenate([vc[u], vc[u + 1]], axis=0) for u in tiles]
                qt = [_rows(qn_s, pos[u][0], d) for u in tiles]
                sc = [[_dot_nt(jnp.where(hms[h], qt[u], 0.0).astype(BF16), kcat[u])
                       + b_ref[br, jnp.where(pos[u][2], 0, 1), h] for h in heads] for u in tiles]
                m = [[jnp.max(sc[u][h], axis=1, keepdims=True) for h in heads] for u in tiles]
                pe = [[jnp.exp(sc[u][h] - m[u][h]) for h in heads] for u in tiles]
                den = [[jnp.sum(pe[u][h], axis=1, keepdims=True) for h in heads] for u in tiles]
                o = [[_dot(pe[u][h].astype(BF16), vcat[u]) * (1.0 / den[u][h]) for h in heads] for u in tiles]
                for u in tiles:
                    _set_rows(o_s.at[br], pos[u][0], d, jnp.where(lo, o[u][0], o[u][1]))
                    _set_rows(l_s.at[br], pos[u][0], d,
                              jnp.where(lo, m[u][0] + jnp.log(den[u][0]), m[u][1] + jnp.log(den[u][1])))
                return kc[-1], vc[-1]

            none_yet = jnp.zeros((BLOCK, LANES), BF16)
            lax.fori_loop(0, nt // FWD_TILES, step, (none_yet, none_yet))

        def combine(c, carry):
            rows = pl.ds(pl.multiple_of(c * NORM_CH, NORM_CH), NORM_CH)
            ls = [l_s[br, rows, :] for br in range(nbr)]
            mx = functools.reduce(jnp.maximum, ls)
            if has_sink:
                mx = jnp.maximum(mx, sink_ref[...])
            tot = functools.reduce(jnp.add, [jnp.exp(l - mx) for l in ls])
            if has_sink:
                tot = tot + jnp.exp(sink_ref[...] - mx)
            lse = mx + jnp.log(tot)
            acc = jnp.exp(ls[0] - lse) * o_s[0, rows, :]
            for br in range(1, nbr):
                acc = acc + jnp.exp(ls[br] - lse) * o_s[br, rows, :]
            out_ref[rows, :] = acc
            lse_ref[rows, :] = lse
            return carry

        lax.fori_loop(0, nch, combine, 0)

    slab = lambda b0, shared: pl.BlockSpec((s, LANES), (lambda p: (0, b0)) if shared else (lambda p: (0, b0 + p)),
                                           pipeline_mode=pl.Buffered(1))
    vec = pl.BlockSpec((1, LANES), lambda p: (0, 0))
    in_specs = [slab(qb0, False), slab(kb0, gqa), slab(vb0, gqa), vec, vec,
                pl.BlockSpec((nbr, 2, 2, BLOCK, 2 * BLOCK), lambda p: (0, 0, p, 0, 0))]
    args = [proj, proj, proj, gq.reshape(1, LANES), gk.reshape(1, LANES), bias]
    if has_sink:
        in_specs.append(pl.BlockSpec((None, 1, LANES), lambda p: (p, 0, 0)))
        args.append(sinks)
    w = LANES * n_slabs
    return pl.pallas_call(
        body, name=name, grid=(n_slabs,),
        in_specs=in_specs,
        out_specs=[pl.BlockSpec((s, LANES), lambda p: (0, p)), pl.BlockSpec((s, LANES), lambda p: (0, p))],
        out_shape=[jax.ShapeDtypeStruct((s, w), F32), jax.ShapeDtypeStruct((s, w), F32)],
        scratch_shapes=[pltpu.VMEM((s, LANES), F32), pltpu.VMEM((s, LANES), F32), pltpu.VMEM((s, LANES), F32),
                        pltpu.VMEM((nbr, s, LANES), F32), pltpu.VMEM((nbr, s, LANES), F32)],
        compiler_params=_params(("parallel",), VMEM_LIMIT),
    )(*args)


def banded_bwd(proj, qb0, kb0, vb0, n_slabs, gq, gk, bias, dils, sinks, gqa, dout, out, lse, dproj, *, name):
    s = proj.shape[0]
    nbr = len(dils)
    nt = s // BLOCK
    nch = s // NORM_CH
    has_sink = sinks is not None
    scale = HEAD_DIM ** -0.5

    def body(*refs):
        q_ref, k_ref, v_ref, gq_ref, gk_ref, b_ref, do_ref, o_ref, lse_ref = refs[:9]
        rest = refs[9:]
        if has_sink:
            sink_ref, rest = rest[0], rest[1:]
        dproj_ref, db_ref, dgq_ref, dgk_ref = rest[1:5]
        rest = rest[5:]
        if has_sink:
            dsink_ref, rest = rest[0], rest[1:]
        qn_s, kn_s, vv_s, dl_s, dqn_s, dkn_s, dvv_s, dq_ref, dk_ref, dv_ref, stage, sems = rest
        p = pl.program_id(0)
        use_lo = (p // 2) == 0

        def prep(c, carry):
            rows = pl.ds(pl.multiple_of(c * NORM_CH, NORM_CH), NORM_CH)
            lo = _lo_mask((NORM_CH, LANES))
            qv, kv, vv = q_ref[rows, :], k_ref[rows, :], v_ref[rows, :]
            qn_s[rows, :] = qv * _head_stats(qv, lo) * gq_ref[...] * scale
            kn = kv * _head_stats(kv, lo) * gk_ref[...]
            if gqa:
                kn = _replicate_head(kn, lo, use_lo)
                vv = _replicate_head(vv, lo, use_lo)
            kn_s[rows, :] = kn
            vv_s[rows, :] = vv
            delta = _head_sum(do_ref[rows, :] * o_ref[rows, :], lo)
            odd = lax.broadcasted_iota(I32, (NORM_CH, LANES), 1) % 2 == 1
            dl_s[rows, :] = jnp.where(odd, delta, lse_ref[rows, :])
            z = jnp.zeros((NORM_CH, LANES), F32)
            dqn_s[rows, :] = z
            dkn_s[rows, :] = z
            dvv_s[rows, :] = z
            if has_sink:
                ps = jnp.exp(sink_ref[...] - lse_ref[rows, :])
                return carry - jnp.sum(ps * delta, axis=0, keepdims=True)
            return carry

        dsink = lax.fori_loop(0, nch, prep, jnp.zeros((1, LANES), F32))
        if has_sink:
            dsink_ref[...] = jnp.broadcast_to(dsink, (8, LANES))

        lo = _lo_mask((BLOCK, LANES))
        hms = [lo, jnp.logical_not(lo)]
        heads, tiles = range(2), range(BWD_TILES)
        for br, d in enumerate(dils):
            db_ref[br] = jnp.zeros((2, BLOCK, 2 * BLOCK), F32)

            def step(ii, carry, br=br, d=d):
                pos = [_tile_rows(ii * BWD_TILES + u, s, d) for u in tiles]
                kc = [carry[0]] + [_rows(kn_s, pos[u][0], d).astype(BF16) for u in tiles]
                vc = [carry[1]] + [_rows(vv_s, pos[u][0], d).astype(BF16) for u in tiles]
                kcat = [jnp.concatenate([kc[u], kc[u + 1]], axis=0) for u in tiles]
                vcat = [jnp.concatenate([vc[u], vc[u + 1]], axis=0) for u in tiles]
                qt = [_rows(qn_s, pos[u][0], d) for u in tiles]
                dot_ = [_rows(do_ref, pos[u][0], d) for u in tiles]
                st_t = [_rows(dl_s, pos[u][0], d) for u in tiles]
                qh = [[jnp.where(hms[h], qt[u], 0.0).astype(BF16) for h in heads] for u in tiles]
                doh = [[jnp.where(hms[h], dot_[u], 0.0).astype(BF16) for h in heads] for u in tiles]
                sc = [[_dot_nt(qh[u][h], kcat[u]) + b_ref[br, jnp.where(pos[u][2], 0, 1), h] for h in heads]
                      for u in tiles]
                dp = [[_dot_nt(doh[u][h], vcat[u]) for h in heads] for u in tiles]
                lane0 = [0, HEAD_DIM]
                pr = [[jnp.exp(sc[u][h] - st_t[u][:, lane0[h]:lane0[h] + 1]) for h in heads] for u in tiles]
                dlog = [[pr[u][h] * (dp[u][h] - st_t[u][:, lane0[h] + 1:lane0[h] + 2]) for h in heads] for u in tiles]
                for h in heads:
                    db_ref[br, h] += functools.reduce(jnp.add, [dlog[u][h] for u in tiles])
                dlb = [[dlog[u][h].astype(BF16) for h in heads] for u in tiles]
                prb = [[pr[u][h].astype(BF16) for h in heads] for u in tiles]
                dq_t = [jnp.where(lo, _dot(dlb[u][0], kcat[u]), _dot(dlb[u][1], kcat[u])) * scale for u in tiles]
                rows2 = lambda x: jnp.concatenate(x, axis=0)
                dk_t = [_dot_tn(rows2(dlb[u]), rows2(qh[u])) for u in tiles]
                dv_t = [_dot_tn(rows2(prb[u]), rows2(doh[u])) for u in tiles]
                for u in tiles:
                    start, prev = pos[u][0], pos[u][1]
                    _set_rows(dqn_s, start, d, _rows(dqn_s, start, d) + dq_t[u])
                    _set_rows(dkn_s, prev, d, _rows(dkn_s, prev, d) + dk_t[u][:BLOCK])
                    _set_rows(dkn_s, start, d, _rows(dkn_s, start, d) + dk_t[u][BLOCK:])
                    _set_rows(dvv_s, prev, d, _rows(dvv_s, prev, d) + dv_t[u][:BLOCK])
                    _set_rows(dvv_s, start, d, _rows(dvv_s, start, d) + dv_t[u][BLOCK:])
                return kc[-1], vc[-1]

            none_yet = jnp.zeros((BLOCK, LANES), BF16)
            lax.fori_loop(0, nt // BWD_TILES, step, (none_yet, none_yet))

        if gqa:
            @pl.when(p == 0)
            def _():
                dk_ref[...] = jnp.zeros_like(dk_ref)
                dv_ref[...] = jnp.zeros_like(dv_ref)

        def finish(c, carry):
            dgq, dgk = carry
            rows = pl.ds(pl.multiple_of(c * NORM_CH, NORM_CH), NORM_CH)
            lo = _lo_mask((NORM_CH, LANES))

            def norm_bwd(xv, dn, g_ref):
                r = _head_stats(xv, lo)
                gd = dn * g_ref[...]
                dot = _head_sum(gd * xv, lo) * (1.0 / HEAD_DIM)
                return r * gd - xv * (r * r * r * dot), dn * (xv * r)

            dq, gq_part = norm_bwd(q_ref[rows, :], dqn_s[rows, :], gq_ref)
            dq_ref[rows, :] = dq
            dgq = dgq + jnp.sum(gq_part, axis=0, keepdims=True)
            kv, dkn, dvv = k_ref[rows, :], dkn_s[rows, :], dvv_s[rows, :]
            if gqa:
                kv = _replicate_head(kv, lo, use_lo)
                dkn = dkn + _swap_halves(dkn)
                dvv = dvv + _swap_halves(dvv)
                lane = lax.broadcasted_iota(I32, (NORM_CH, LANES), 1)
                mine = (lane // HEAD_DIM) == (p // 2)
                dk, gk_part = norm_bwd(kv, dkn, gk_ref)
                dk_ref[rows, :] += jnp.where(mine, dk, 0.0)
                dv_ref[rows, :] += jnp.where(mine, dvv, 0.0)
                gk_part = jnp.where(lo, gk_part, 0.0)
            else:
                dk, gk_part = norm_bwd(kv, dkn, gk_ref)
                dk_ref[rows, :] = dk
                dv_ref[rows, :] = dvv
            dgk = dgk + jnp.sum(gk_part, axis=0, keepdims=True)
            return dgq, dgk

        z = jnp.zeros((1, LANES), F32)
        dgq, dgk = lax.fori_loop(0, nch, finish, (z, z))
        dgq_ref[...] = jnp.broadcast_to(dgq, (8, LANES))
        dgk_ref[...] = jnp.broadcast_to(dgk, (8, LANES))
        if gqa:
            _store_slabs((dq_ref,), stage, dproj_ref, sems, (qb0 + p,))

            @pl.when(p == n_slabs - 1)
            def _():
                _store_slabs((dk_ref, dv_ref), stage, dproj_ref, sems, (kb0, vb0))
        else:
            _store_slabs((dq_ref, dk_ref, dv_ref), stage, dproj_ref, sems, (qb0 + p, kb0 + p, vb0 + p))

    def slab_of(b0, shared):
        return pl.BlockSpec((s, LANES), (lambda p: (0, b0)) if shared else (lambda p: (0, b0 + p)),
                            pipeline_mode=pl.Buffered(1))

    vec = pl.BlockSpec((1, LANES), lambda p: (0, 0))
    own = pl.BlockSpec((s, LANES), lambda p: (0, p), pipeline_mode=pl.Buffered(1))
    in_specs = [slab_of(qb0, False), slab_of(kb0, gqa), slab_of(vb0, gqa), vec, vec,
                pl.BlockSpec((nbr, 2, 2, BLOCK, 2 * BLOCK), lambda p: (0, 0, p, 0, 0)), own, own, own]
    args = [proj, proj, proj, gq.reshape(1, LANES), gk.reshape(1, LANES), bias, dout, out, lse]
    if has_sink:
        in_specs.append(pl.BlockSpec((None, 1, LANES), lambda p: (p, 0, 0)))
        args.append(sinks)
    held = pl.BlockSpec(memory_space=pl.ANY)
    in_specs.append(held)
    args.append(dproj)
    part = pl.BlockSpec((None, 8, LANES), lambda p: (p, 0, 0))
    out_specs = [held, pl.BlockSpec((nbr, 2, BLOCK, 2 * BLOCK), lambda p: (0, p, 0, 0)), part, part]
    out_shape = [jax.ShapeDtypeStruct(dproj.shape, dproj.dtype),
                 jax.ShapeDtypeStruct((nbr, 2 * n_slabs, BLOCK, 2 * BLOCK), F32),
                 jax.ShapeDtypeStruct((n_slabs, 8, LANES), F32), jax.ShapeDtypeStruct((n_slabs, 8, LANES), F32)]
    if has_sink:
        out_specs.append(part)
        out_shape.append(jax.ShapeDtypeStruct((n_slabs, 8, LANES), F32))
    res = pl.pallas_call(
        body, name=name, grid=(n_slabs,),
        in_specs=in_specs, out_specs=out_specs, out_shape=out_shape,
        input_output_aliases={len(args) - 1: 0},
        scratch_shapes=[pltpu.VMEM((s, LANES), F32) for _ in range(10)]
        + [pltpu.VMEM((3, s, LANES), BF16), pltpu.SemaphoreType.DMA((3,))],
        compiler_params=_params(("arbitrary",), VMEM_LIMIT),
    )(*args)
    outs = [res[0], res[1], res[2][:, 0, :], res[3][:, 0, :]]
    if has_sink:
        outs.append(res[4][:, 0, :])
    return outs


def bias_bwd(dbias, buckets, *, name):
    nbr, h = dbias.shape[:2]

    def body(db_ref, bk_ref, o_ref):
        lane = lax.broadcasted_iota(I32, (1, LANES), 1)
        acc = jnp.zeros((1, LANES), F32)
        for b in range(N_BUCKETS):
            tot = jnp.zeros((1, 1), F32)
            for br in range(nbr):
                sel = jnp.where(bk_ref[br] == b, db_ref[br], 0.0)
                tot = tot + jnp.sum(jnp.sum(sel, axis=0, keepdims=True), axis=1, keepdims=True)
            acc = jnp.where(lane == b, tot, acc)
        o_ref[...] = jnp.broadcast_to(acc, (8, LANES))

    res = pl.pallas_call(
        body, name=name, grid=(h,),
        in_specs=[pl.BlockSpec((nbr, None, BLOCK, 2 * BLOCK), lambda i: (0, i, 0, 0)),
                  pl.BlockSpec((nbr, BLOCK, 2 * BLOCK), lambda i: (0, 0, 0))],
        out_specs=pl.BlockSpec((None, 8, LANES), lambda i: (i, 0, 0)),
        out_shape=jax.ShapeDtypeStruct((h, 8, LANES), F32),
        compiler_params=_params(("parallel",)),
    )(dbias, buckets)
    return res[:, 0, :N_BUCKETS].T


SB_KG = 512
SB_QT = 2


def _softplus(z):
    return jnp.maximum(z, 0.0) + jnp.log(1.0 + jnp.exp(-jnp.abs(z)))


def _twice(t):
    t = t.astype(BF16)
    return jnp.concatenate([t, t], axis=0)


def _split_dot(x, t2):
    hi = x.astype(BF16)
    lo = (x - hi.astype(F32)).astype(BF16)
    return _dot(jnp.concatenate([hi, lo], axis=1), t2)


def sb_fwd(proj, qb0, kb0, vb0, n_slabs, *, name):
    s = proj.shape[0]
    nq = s // BLOCK
    nch = s // NORM_CH
    scale = HEAD_DIM ** -0.5

    def body(q_ref, k_ref, v_ref, o_ref, tot_ref, qlo_s, qhi_s, k_s, v_s):
        def prep(c, carry):
            rows = pl.ds(pl.multiple_of(c * NORM_CH, NORM_CH), NORM_CH)
            lo = _lo_mask((NORM_CH, LANES))
            qv = q_ref[rows, :] * scale
            qlo_s[rows, :] = jnp.where(lo, qv, 0.0).astype(BF16)
            qhi_s[rows, :] = jnp.where(lo, 0.0, qv).astype(BF16)
            k_s[rows, :] = k_ref[rows, :].astype(BF16)
            v_s[rows, :] = v_ref[rows, :].astype(BF16)
            return carry

        lax.fori_loop(0, nch, prep, 0)

        row = lax.broadcasted_iota(I32, (BLOCK, BLOCK), 0)
        col = lax.broadcasted_iota(I32, (BLOCK, BLOCK), 1)
        lo = col < HEAD_DIM
        t_ge = _twice(row >= col)
        rowg = lax.broadcasted_iota(I32, (BLOCK, SB_KG), 0)
        colg = lax.broadcasted_iota(I32, (BLOCK, SB_KG), 1)

        nsub = SB_KG // BLOCK
        chains = range(2 * SB_QT)
        nc = len(chains)

        def qloop(qs, phase):
            q0 = pl.multiple_of(qs * (SB_QT * BLOCK), SB_QT * BLOCK)
            qh = [(qlo_s, qhi_s)[i % 2][pl.ds(q0 + (i // 2) * BLOCK, BLOCK), :] for i in chains]
            gd = (qs * SB_QT) // nsub

            def logits(gi):
                k0 = pl.multiple_of(gi * SB_KG, SB_KG)
                kg = k_s[pl.ds(k0, SB_KG), :]
                return [_dot_nt(qh[i], kg) for i in chains]

            def group(gi, st, masks, npiece=nsub):
                k0 = pl.multiple_of(gi * SB_KG, SB_KG)
                vg = v_s[pl.ds(k0, npiece * BLOCK), :]
                c, o, z = list(st[:nc]), st[nc:2 * nc], st[2 * nc:]
                z_next = logits(jnp.maximum(gi - 1, 0))
                piece = lambda x, j: x[:, j * BLOCK:(j + 1) * BLOCK]
                a = [[None] * npiece for _ in chains]
                for j in reversed(range(npiece)):
                    zj = [piece(z[i], j) for i in chains]
                    lrem = [-_softplus(zj[i]) for i in chains]
                    if masks is not None:
                        lrem = [jnp.where(piece(masks[i // 2], j), lrem[i], 0.0) for i in chains]
                    incl = [_split_dot(lrem[i], t_ge) for i in chains]
                    for i in chains:
                        aij = jnp.exp(zj[i] + (c[i] + incl[i]))
                        if masks is not None:
                            aij = jnp.where(piece(masks[i // 2], j), aij, 0.0)
                        a[i][j] = aij.astype(BF16)
                        c[i] = c[i] + incl[i][:, 0:1]
                o = [o[i] + _dot(jnp.concatenate(a[i], axis=1), vg) for i in chains]
                return (*c, *o, *z_next)

            zc = [jnp.zeros((BLOCK, 1), F32)] * nc
            zo = [jnp.zeros((BLOCK, LANES), F32)] * nc
            masks = [(gd * SB_KG + colg) < (q0 + t * BLOCK + rowg) for t in range(SB_QT)]
            st = group(gd, (*zc, *zo, *logits(gd)), masks, (phase + 1) * SB_QT)
            st = lax.fori_loop(0, gd, lambda t, st: group(gd - 1 - t, st, None), st)
            for t in range(SB_QT):
                rows = pl.ds(q0 + t * BLOCK, BLOCK)
                o_ref[rows, :] = jnp.where(lo, st[nc + 2 * t], st[nc + 2 * t + 1])
                tot_ref[rows, :] = jnp.where(lo, st[2 * t], st[2 * t + 1])

        steps_per_group = nsub // SB_QT

        def per_group(g, carry):
            for phase in range(steps_per_group):
                qloop(g * steps_per_group + phase, phase)
            return carry

        lax.fori_loop(0, nq // nsub, per_group, 0)

    slab = lambda b0: pl.BlockSpec((s, LANES), lambda p: (0, b0 + p), pipeline_mode=pl.Buffered(1))
    w = LANES * n_slabs
    return pl.pallas_call(
        body, name=name, grid=(n_slabs,),
        in_specs=[slab(qb0), slab(kb0), slab(vb0)],
        out_specs=[pl.BlockSpec((s, LANES), lambda p: (0, p)), pl.BlockSpec((s, LANES), lambda p: (0, p))],
        out_shape=[jax.ShapeDtypeStruct((s, w), F32), jax.ShapeDtypeStruct((s, w), F32)],
        scratch_shapes=[pltpu.VMEM((s, LANES), BF16) for _ in range(4)],
        compiler_params=_params(("parallel",), VMEM_LIMIT),
    )(proj, proj, proj)


def _store_slabs(slabs, stage, dproj_ref, sems, blocks):
    s = stage.shape[1]

    def cast(c, carry):
        rows = pl.ds(pl.multiple_of(c * NORM_CH, NORM_CH), NORM_CH)
        for i, slab in enumerate(slabs):
            stage[i, rows, :] = slab[rows, :].astype(BF16)
        return carry

    lax.fori_loop(0, s // NORM_CH, cast, 0)
    copies = [pltpu.make_async_copy(stage.at[i], dproj_ref.at[:, pl.ds(pl.multiple_of(b * LANES, LANES), LANES)],
                                    sems.at[i]) for i, b in enumerate(blocks)]
    for cp in copies:
        cp.start()
    for cp in copies:
        cp.wait()


def sb_bwd(proj, qb0, kb0, vb0, n_slabs, dout, tot, dproj, *, name):
    s = proj.shape[0]
    nq = s // BLOCK
    nch = s // NORM_CH
    nsub = SB_KG // BLOCK
    scale = HEAD_DIM ** -0.5

    def body(q_ref, k_ref, v_ref, do_ref, tot_ref, dproj_in, dproj_ref,
             qlo_s, qhi_s, k_s, v_s, dlo_s, dhi_s, dq_ref, dk_ref, dv_ref, stage, sems):
        del dproj_in
        def prep(c, carry):
            rows = pl.ds(pl.multiple_of(c * NORM_CH, NORM_CH), NORM_CH)
            lo = _lo_mask((NORM_CH, LANES))
            qv = q_ref[rows, :] * scale
            dv = do_ref[rows, :]
            qlo_s[rows, :] = jnp.where(lo, qv, 0.0).astype(BF16)
            qhi_s[rows, :] = jnp.where(lo, 0.0, qv).astype(BF16)
            dlo_s[rows, :] = jnp.where(lo, dv, 0.0).astype(BF16)
            dhi_s[rows, :] = jnp.where(lo, 0.0, dv).astype(BF16)
            k_s[rows, :] = k_ref[rows, :].astype(BF16)
            v_s[rows, :] = v_ref[rows, :].astype(BF16)
            z = jnp.zeros((NORM_CH, LANES), F32)
            dk_ref[rows, :] = z
            dv_ref[rows, :] = z
            return carry

        lax.fori_loop(0, nch, prep, 0)

        row = lax.broadcasted_iota(I32, (BLOCK, BLOCK), 0)
        col = lax.broadcasted_iota(I32, (BLOCK, BLOCK), 1)
        lo = col < HEAD_DIM
        t_le = _twice(row <= col)
        rowg = lax.broadcasted_iota(I32, (BLOCK, SB_KG), 0)
        colg = lax.broadcasted_iota(I32, (BLOCK, SB_KG), 1)

        piece = lambda x, j: x[:, j * BLOCK:(j + 1) * BLOCK]
        chains = range(2 * SB_QT)
        nc = len(chains)

        def prefixes(x):
            return [[_split_dot(piece(x[i], j), t_le) for j in range(x[i].shape[1] // BLOCK)] for i in chains]

        def chain(pre, run, total=None):
            out = []
            for pj in pre:
                out.append(run + pj if total is None else total - run - pj)
                run = run + pj[:, BLOCK - 1:BLOCK]
            return jnp.concatenate(out, axis=1), run

        def qloop(qs, phase):
            q0 = pl.multiple_of(qs * (SB_QT * BLOCK), SB_QT * BLOCK)
            tile = lambda ref, i: ref[pl.ds(q0 + (i // 2) * BLOCK, BLOCK), :]
            qh = [tile((qlo_s, qhi_s)[i % 2], i) for i in chains]
            doh = [tile((dlo_s, dhi_s)[i % 2], i) for i in chains]
            tots = [tile(tot_ref, i)[:, (i % 2) * HEAD_DIM:(i % 2) * HEAD_DIM + 1] for i in chains]
            gd = (qs * SB_QT) // nsub

            def logits(gi):
                kg = k_s[pl.ds(pl.multiple_of(gi * SB_KG, SB_KG), SB_KG), :]
                return [_dot_nt(qh[i], kg) for i in chains]

            def group(gi, st, masks, npiece=nsub):
                k0 = pl.multiple_of(gi * SB_KG, SB_KG)
                wide = npiece * BLOCK
                kg, vg = k_s[pl.ds(k0, wide), :], v_s[pl.ds(k0, wide), :]
                cp, cg, dq = list(st[:nc]), list(st[nc:2 * nc]), st[2 * nc:2 * nc + SB_QT]
                z = [zi[:, :wide] for zi in st[2 * nc + SB_QT:]]
                masked = lambda x, i: x if masks is None else jnp.where(masks[i // 2][:, :wide], x, 0.0)
                z_next = logits(jnp.minimum(gi + 1, gd))
                da = [_dot_nt(doh[i], vg) for i in chains]
                sp = [_softplus(z[i]) for i in chains]
                lrem = [masked(-sp[i], i) for i in chains]
                pre = prefixes(lrem)
                e, a, g = [], [], []
                for i in chains:
                    suffix, cp[i] = chain(pre[i], cp[i], tots[i])
                    e.append(z[i] - sp[i])
                    a.append(masked(jnp.exp(e[i] + suffix), i))
                    g.append(a[i] * da[i])
                gpre = prefixes(g)
                dz = []
                for i in chains:
                    ginc, cg[i] = chain(gpre[i], cg[i])
                    dz.append(masked(g[i] - jnp.exp(e[i]) * ginc, i).astype(BF16))
                ab = [a[i].astype(BF16) for i in chains]
                dq = [dq[t] + jnp.where(lo, _dot(dz[2 * t], kg), _dot(dz[2 * t + 1], kg)) for t in range(SB_QT)]
                rows_of = lambda x: jnp.concatenate(x, axis=0)
                dk_ref[pl.ds(k0, wide), :] += _dot_tn(rows_of(dz), rows_of(qh))
                dv_ref[pl.ds(k0, wide), :] += _dot_tn(rows_of(ab), rows_of(doh))
                return (*cp, *cg, *dq, *z_next)

            zc = [jnp.zeros((BLOCK, 1), F32)] * (2 * nc)
            zq = [jnp.zeros((BLOCK, LANES), F32)] * SB_QT
            st = lax.fori_loop(0, gd, lambda gi, st: group(gi, st, None), (*zc, *zq, *logits(0)))
            st = group(gd, st, [(gd * SB_KG + colg) < (q0 + t * BLOCK + rowg) for t in range(SB_QT)],
                       (phase + 1) * SB_QT)
            for t in range(SB_QT):
                dq_ref[pl.ds(q0 + t * BLOCK, BLOCK), :] = st[2 * nc + t] * scale

        steps_per_group = nsub // SB_QT

        def per_group(g, carry):
            for phase in range(steps_per_group):
                qloop(g * steps_per_group + phase, phase)
            return carry

        lax.fori_loop(0, nq // nsub, per_group, 0)
        p = pl.program_id(0)
        _store_slabs((dq_ref, dk_ref, dv_ref), stage, dproj_ref, sems, (qb0 + p, kb0 + p, vb0 + p))

    slab = lambda b0: pl.BlockSpec((s, LANES), lambda p: (0, b0 + p), pipeline_mode=pl.Buffered(1))
    own = pl.BlockSpec((s, LANES), lambda p: (0, p), pipeline_mode=pl.Buffered(1))
    held = pl.BlockSpec(memory_space=pl.ANY)
    return pl.pallas_call(
        body, name=name, grid=(n_slabs,),
        in_specs=[slab(qb0), slab(kb0), slab(vb0), own, own, held],
        out_specs=held, out_shape=jax.ShapeDtypeStruct(dproj.shape, dproj.dtype),
        input_output_aliases={5: 0},
        scratch_shapes=[pltpu.VMEM((s, LANES), BF16) for _ in range(6)]
        + [pltpu.VMEM((s, LANES), F32) for _ in range(3)]
        + [pltpu.VMEM((3, s, LANES), BF16), pltpu.SemaphoreType.DMA((3,))],
        compiler_params=_params(("arbitrary",), VMEM_LIMIT),
    )(proj, proj, proj, dout, tot, dproj)


def _place():
    x, y, c = lax.axis_index("x"), lax.axis_index("y"), lax.axis_index("c")
    return x, y, c


def gather_small(v, *, name):
    m_per, n = v.shape

    def body(x_ref, out_ref, send_sems, recv_sems, local_sem):
        x, y, c = _place()
        me, sibling = (x, y, c), (x, y, 1 - c)
        chips = [(1 - x, y), (x, 1 - y), (1 - x, 1 - y)]

        def rows(px, py, pc):
            return out_ref.at[pl.ds((4 * px + 2 * py + pc) * m_per, m_per), :]

        def copy(k, block, to, src=None):
            return pltpu.make_async_remote_copy(
                src_ref=rows(*block) if src is None else src, dst_ref=rows(*block),
                send_sem=send_sems.at[k], recv_sem=recv_sems.at[k], device_id=to, device_id_type=MESH)

        mine = pltpu.make_async_copy(x_ref, rows(*me), local_sem)
        mine.start()
        first = [copy(0, me, sibling, src=x_ref)]
        first += [copy(1 + j, me, (*chip, c), src=x_ref) for j, chip in enumerate(chips)]
        for cp in first:
            cp.start()
        passed = [copy(4 + j, (*chip, c), sibling) for j, chip in enumerate(chips)]
        for j, chip in enumerate(chips):
            copy(1 + j, (*chip, c), me).wait_recv()
            passed[j].start()
        copy(0, sibling, me).wait_recv()
        for j, chip in enumerate(chips):
            copy(4 + j, (*chip, 1 - c), me).wait_recv()
        for cp in first + passed:
            cp.wait_send()
        mine.wait()

    return pl.pallas_call(
        body, name=name,
        out_shape=jax.ShapeDtypeStruct((N_DEV * m_per, n), v.dtype),
        in_specs=[pl.BlockSpec(memory_space=pltpu.VMEM)],
        out_specs=pl.BlockSpec(memory_space=pltpu.VMEM),
        scratch_shapes=[pltpu.SemaphoreType.DMA((7,)), pltpu.SemaphoreType.DMA((7,)), pltpu.SemaphoreType.DMA],
        compiler_params=_params(None, VMEM_LIMIT),
    )(v)


_HBM = pl.BlockSpec(memory_space=pltpu.HBM)
_SEM = pl.BlockSpec(memory_space=pltpu.SEMAPHORE)
_EFFECT = pltpu.SideEffectType.DATAFLOW_SIDE_EFFECTING


def _peer_copies(src_refs, land_refs, send_sems, recv_sems, per_dest):
    x, y, c = _place()
    me = 4 * x + 2 * y + c
    copies = []
    for src, land, ssem, rsem in zip(src_refs, land_refs, send_sems, recv_sems):
        for k in (1, 2, 4, 3, 5, 6, 7):
            px, py, pc = x ^ (k >> 2 & 1), y ^ (k >> 1 & 1), c ^ (k & 1)
            copies.append(pltpu.make_async_remote_copy(
                src_ref=src.at[4 * px + 2 * py + pc] if per_dest else src, dst_ref=land.at[me],
                send_sem=ssem.at[k - 1], recv_sem=rsem.at[k - 1], device_id=(px, py, pc), device_id_type=MESH))
    return copies


def _own_copies(src_refs, land_refs, send_sems, per_dest):
    x, y, c = _place()
    me = 4 * x + 2 * y + c
    return [pltpu.make_async_copy(src.at[me] if per_dest else src, land.at[me], ssem.at[7])
            for src, land, ssem in zip(src_refs, land_refs, send_sems)]


def exchange_start(srcs, per_dest, *, name):
    n = len(srcs)
    lands = [lax.empty(a.shape if per_dest else (N_DEV,) + a.shape, a.dtype) for a in srcs]

    def body(*refs):
        src_refs, land_refs = refs[:n], refs[n:2 * n]
        send_sems, recv_sems = refs[2 * n:3 * n], refs[3 * n:4 * n]
        token = refs[-1]
        for cp in _peer_copies(src_refs, land_refs, send_sems, recv_sems, per_dest):
            cp.start()
        for cp in _own_copies(src_refs, land_refs, send_sems, per_dest):
            cp.start()
        token[...] = jnp.zeros_like(token)

    hbm = lambda a: pltpu.HBM(a.shape, a.dtype)
    res = pl.pallas_call(
        body, name=name,
        out_shape=(*[pltpu.SemaphoreType.DMA((8,))] * n, *[pltpu.SemaphoreType.DMA((7,))] * n,
                   *[hbm(a) for a in srcs], *[hbm(a) for a in lands], jax.ShapeDtypeStruct((8, LANES), F32)),
        in_specs=[_HBM] * (2 * n),
        out_specs=(*[_SEM] * (2 * n), *[_HBM] * (2 * n), pl.BlockSpec(memory_space=pltpu.VMEM)),
        input_output_aliases={i: 2 * n + i for i in range(2 * n)},
        compiler_params=pltpu.CompilerParams(has_side_effects=_EFFECT),
    )(*[pltpu.with_memory_space_constraint(a, pltpu.HBM) for a in (*srcs, *lands)])
    handles = [(res[a], res[n + a], res[2 * n + a], res[3 * n + a]) for a in range(n)]
    return handles, res[-1]


def exchange_wait(handles, per_dest, after, *, name):
    n = len(handles)

    def body(*refs):
        src_refs, land_refs = refs[:n], refs[n:2 * n]
        send_sems, recv_sems = refs[2 * n:3 * n], refs[3 * n:4 * n]
        for cp in _peer_copies(src_refs, land_refs, send_sems, recv_sems, per_dest):
            cp.wait_send()
            cp.wait_recv()
        for cp in _own_copies(src_refs, land_refs, send_sems, per_dest):
            cp.wait()

    srcs, lands = [h[2] for h in handles], [h[3] for h in handles]
    hbm = lambda a: pltpu.HBM(a.shape, a.dtype)
    res = pl.pallas_call(
        body, name=name,
        out_shape=(*[hbm(a) for a in srcs], *[hbm(a) for a in lands]),
        in_specs=[*[_HBM] * (2 * n), *[_SEM] * (2 * n), pl.BlockSpec(memory_space=pl.ANY)],
        out_specs=tuple([_HBM] * (2 * n)),
        input_output_aliases={i: i for i in range(2 * n)},
        compiler_params=pltpu.CompilerParams(has_side_effects=_EFFECT),
    )(*srcs, *lands, *[h[0] for h in handles], *[h[1] for h in handles], after)
    return res[n:]


def _adamw_math(w, g, m, v):
    m = ADAM_B1 * m + (1.0 - ADAM_B1) * g
    v = ADAM_B2 * v + (1.0 - ADAM_B2) * (g * g)
    m_hat = m / (1.0 - ADAM_B1 ** ADAM_STEP)
    v_hat = v / (1.0 - ADAM_B2 ** ADAM_STEP)
    delta = -ADAM_LR * (m_hat / (jnp.sqrt(v_hat) + ADAM_EPS) + ADAM_WD * w)
    return delta, m, v


def adamw_parts(parts, w, m, v, layer, outs, *, name):
    depth, r, cdim = w.shape
    n_parts = parts.shape[0]
    tr = _pick(r, [t for t in (512, 256, 128, 112, 64, 32, 16) if t * cdim <= 256 * 1024])

    def body(p_ref, w_ref, m_ref, v_ref, g0, d0, nm0, nv0, g_ref, d_ref, nm_ref, nv_ref):
        g = p_ref[0].astype(F32)
        for q in range(1, n_parts):
            g = g + p_ref[q].astype(F32)
        delta, nm, nv = _adamw_math(w_ref[...], g, m_ref[...], v_ref[...])
        g_ref[...], d_ref[...], nm_ref[...], nv_ref[...] = g, delta, nm, nv

    t = pl.BlockSpec((None, tr, cdim), lambda i: (layer, i, 0))
    held = pl.BlockSpec(memory_space=pl.ANY)
    return pl.pallas_call(
        body, name=name, grid=(r // tr,),
        in_specs=[pl.BlockSpec((n_parts, tr, cdim), lambda i: (0, i, 0)), t, t, t, held, held, held, held],
        out_specs=[t, t, t, t],
        out_shape=[jax.ShapeDtypeStruct((depth, r, cdim), F32)] * 4,
        input_output_aliases={4: 0, 5: 1, 6: 2, 7: 3},
        compiler_params=_params(("parallel",), VMEM_LIMIT),
    )(parts, w, m, v, *outs)


def sum_devices(gathered, *, name):
    m_rows = gathered.shape[1]

    def body(ga_ref, g_ref):
        g = ga_ref[0]
        for dev in range(1, N_DEV):
            g = g + ga_ref[dev]
        g_ref[...] = g

    return pl.pallas_call(
        body, name=name, out_shape=jax.ShapeDtypeStruct((m_rows, LANES), F32),
        compiler_params=_params(None, VMEM_LIMIT),
    )(gathered)


def adamw_small(g, w, m, v, *, name):
    m_rows = w.shape[0]

    def body(g_ref, w_ref, m_ref, v_ref, d_ref, nm_ref, nv_ref):
        d_ref[...], nm_ref[...], nv_ref[...] = _adamw_math(w_ref[...], g_ref[...], m_ref[...], v_ref[...])

    return pl.pallas_call(
        body, name=name, out_shape=[jax.ShapeDtypeStruct((m_rows, LANES), F32)] * 3,
        compiler_params=_params(None, VMEM_LIMIT),
    )(g, w, m, v)


def _t5_bucket(dist):
    max_exact = N_BUCKETS // 2
    d = jnp.maximum(dist, 0)
    large = max_exact + (jnp.log(jnp.maximum(d, 1).astype(F32) / max_exact)
                         / math.log(T5_MAX_DIST / max_exact) * (N_BUCKETS - max_exact)).astype(I32)
    large = jnp.minimum(large, N_BUCKETS - 1)
    return jnp.where(d < max_exact, d, large)


def _rel():
    return jnp.arange(BLOCK)[:, None] + BLOCK - jnp.arange(2 * BLOCK)[None, :]


def _band_bias(table, dils, max_dists):
    rel = _rel()
    biases, buckets = [], []
    for d, md in zip(dils, max_dists):
        bk = _t5_bucket(rel * d)
        vis = (rel >= 0) & (rel <= md)
        looked_up = jnp.zeros((table.shape[1],) + rel.shape, F32)
        for b in range(N_BUCKETS):
            looked_up = jnp.where((bk == b)[None], table[b][:, None, None], looked_up)
        with_prev = jnp.where(vis[None], looked_up, NEG_INF)
        first = jnp.arange(2 * BLOCK)[None, None, :] >= BLOCK
        biases.append(jnp.stack([with_prev, jnp.where(first, with_prev, NEG_INF)]))
        buckets.append(bk.astype(I32))
    return jnp.stack(biases), jnp.stack(buckets)


def _pack(pieces, rows):
    flat = jnp.concatenate([p.reshape(-1) for p in pieces])
    return jnp.pad(flat, (0, rows * LANES - flat.shape[0])).reshape(rows, LANES)


def _unpack(packed, shapes):
    flat = packed.reshape(-1)
    out, off = [], 0
    for sh in shapes:
        n = math.prod(sh)
        out.append(flat[off:off + n].reshape(sh))
        off += n
    return out


def _tile2(g):
    return jnp.concatenate([g, g])


def kernel(x, attn_norm, w_in, a_q_gain, a_k_gain, a_sinks, c_q_gain, c_k_gain, rel_bias_table, mix_out_gain, w_out, ffn_norm, w_up, conv_w, conv_b, w_down, loss_target, m_attn_norm, m_w_in, m_a_q_gain, m_a_k_gain, m_a_sinks, m_c_q_gain, m_c_k_gain, m_rel_bias_table, m_mix_out_gain, m_w_out, m_ffn_norm, m_w_up, m_conv_w, m_conv_b, m_w_down, v_attn_norm, v_w_in, v_a_q_gain, v_a_k_gain, v_a_sinks, v_c_q_gain, v_c_k_gain, v_rel_bias_table, v_mix_out_gain, v_w_out, v_ffn_norm, v_w_up, v_conv_w, v_conv_b, v_w_down):
    depth, d_model, in_shard = w_in.shape
    ff2_shard = w_up.shape[2]
    s = x.shape[1]
    in_width, ff2 = N_DEV * in_shard, N_DEV * ff2_shard
    n_heads = d_model // HEAD_DIM
    ha, hb, hc = n_heads // 4, n_heads // 4, n_heads // 2
    sa, sb, sc = ha // 2, hb // 2, hc // 2
    kv_a = ha // 4
    assert kv_a == 2 and BLOCK == LANES
    cb_aq, cb_ak, cb_av = 0, sa, sa + 1
    cb_bq = sa + 2
    cb_bk, cb_bv = cb_bq + sb, cb_bq + 2 * sb
    cb_cq = cb_bq + 3 * sb
    cb_ck, cb_cv = cb_cq + sc, cb_cq + 2 * sc
    assert (cb_cv + sc) * LANES == in_width
    dev = 4 * lax.axis_index("x") + 2 * lax.axis_index("y") + lax.axis_index("c")

    per_array = 3
    wnames = ("w_in", "w_out", "w_up", "w_down", "conv_w")
    sent = dict(w_in=lambda w: w.T, w_up=lambda w: w.T, w_out=lambda w: w, w_down=lambda w: w, conv_w=lambda w: w)
    rows = lambda g: g.reshape(N_DEV * g.shape[1], g.shape[2])
    whole = dict(w_in=rows, w_up=rows, w_out=rows, w_down=rows,
                 conv_w=lambda g: jnp.transpose(g, (1, 0, 2)).reshape(g.shape[1], N_DEV * g.shape[2]))
    gathers = {}
    token = jnp.zeros((8, LANES), F32)
    for l in range(depth):
        for gi, group in enumerate([[n] for n in wnames] if l < per_array else [wnames]):
            srcs = [(sent[n](dict(w_in=w_in, w_out=w_out, w_up=w_up, w_down=w_down, conv_w=conv_w)[n][l])
                     + token[0, 0]).astype(F32 if n == "conv_w" else BF16) for n in group]
            handles, token = exchange_start(srcs, False, name=f"gather_start_{l}_{gi}")
            gathers.update({(l, n): h for n, h in zip(group, handles)})

    def gathered(l, names, after):
        landed = exchange_wait([gathers[l, n] for n in names], False, after,
                               name=f"gather_wait_{l}_{wnames.index(names[0])}")
        return {n: whole[n](g) for n, g in zip(names, landed)}

    bias_a, buckets_a = _band_bias(rel_bias_table[:, :ha], (1,), (WINDOW_A - 1,))
    bias_c, buckets_c = _band_bias(rel_bias_table[:, ha:], DILATIONS, (BLOCK,) * len(DILATIONS))

    xs = x[0]
    saved = []
    wi, wo, wu, wd, cw = ([None] * depth for _ in range(5))
    for l in range(depth):
        if l < per_array:
            need = lambda n, after, l=l: gathered(l, (n,), after)[n]
        else:
            layer_w = gathered(l, wnames, xs)
            need = lambda n, after: layer_w[n]
        wi[l] = need("w_in", token if l == 0 else xs)
        h1 = rmsnorm_fwd(xs, attn_norm[l], name="attn_norm_fwd")
        proj = matmul(h1, wi[l], trans_b=True, name="in_proj")
        sinks = jnp.repeat(a_sinks[l], HEAD_DIM).reshape(sa, 1, LANES)
        gaq, gak = _tile2(a_q_gain[l]), _tile2(a_k_gain[l])
        gcq, gck = _tile2(c_q_gain[l]), _tile2(c_k_gain[l])
        out_a, lse_a = banded_fwd(proj, cb_aq, cb_ak, cb_av, sa, gaq, gak, bias_a, (1,), sinks, True, name="swa_fwd")
        out_b, tot_b = sb_fwd(proj, cb_bq, cb_bk, cb_bv, sb, name="stick_fwd")
        out_c, lse_c = banded_fwd(proj, cb_cq, cb_ck, cb_cv, sc, gcq, gck, bias_c, DILATIONS, None, False,
                                  name="dilated_fwd")
        mix = mixnorm_fwd([out_a, out_b, out_c], mix_out_gain[l], name="mix_norm_fwd")
        wo[l] = need("w_out", mix)
        x_mid = matmul(mix, wo[l], res=xs, name="out_proj")
        h2 = rmsnorm_fwd(x_mid, ffn_norm[l], name="ffn_norm_fwd")
        wu[l] = need("w_up", h2)
        p = matmul(h2, wu[l], trans_b=True, name="up_proj")
        cw[l] = need("conv_w", p)
        act = ffn_act_fwd(p, cw[l], conv_b[l], name="ffn_act_fwd")
        wd[l] = need("w_down", act)
        x_out = matmul(act, wd[l], res=x_mid, name="down_proj")
        saved.append(dict(x_in=xs, h1=h1, proj=proj, out_a=out_a, lse_a=lse_a, out_b=out_b, tot_b=tot_b,
                          out_c=out_c, lse_c=lse_c, mix=mix, x_mid=x_mid, h2=h2, p=p, act=act,
                          sinks=sinks, gains=(gaq, gak, gcq, gck)))
        xs = x_out

    dx, dx_b, loss_part = loss_head(xs, loss_target[0], name="loss_head")

    small = {k: [None] * depth for k in ("attn_norm", "a_q_gain", "a_k_gain", "a_sinks", "c_q_gain", "c_k_gain",
                                         "mix_out_gain", "ffn_norm", "conv_w", "conv_b")}
    big = {k: [None] * depth for k in ("w_in", "w_out", "w_up", "w_down")}
    dbias_a = dbias_c = None
    scatters = {}
    token = jnp.zeros((8, LANES), F32)
    names_big = ("w_in", "w_out", "w_up", "w_down")

    def scatter(l, names):
        parts = [big[n][l] for n in names]
        handles, tok = exchange_start(parts, True, name=f"scatter_start_{l}_{names_big.index(names[0])}")
        scatters.update({(l, n): h for n, h in zip(names, handles)})
        return tok

    by_rows = lambda a: a.reshape(N_DEV, a.shape[0] // N_DEV, a.shape[1])
    for l in reversed(range(depth)):
        each = l == 0
        sv = saved[l]
        gaq, gak, gcq, gck = sv["gains"]
        da = matmul(dx_b, wd[l], trans_b=True, name="down_proj_dx")
        big["w_down"][l] = by_rows(matmul(sv["act"], dx_b, trans_a=True, out_dtype=BF16, name="down_proj_dw"))
        if each:
            token = scatter(l, ("w_down",))
        dp, small["conv_w"][l], small["conv_b"][l] = ffn_act_bwd(da, sv["p"], cw[l], conv_b[l] + token[0, 0],
                                                                 name="ffn_act_bwd")
        dh2 = matmul(dp, wu[l], name="up_proj_dx")
        big["w_up"][l] = matmul(sv["h2"], dp, trans_a=True, out_dtype=BF16, col_blocks=N_DEV, name="up_proj_dw")
        if each:
            token = scatter(l, ("w_up",))
        dx_mid, dx_mid_b, small["ffn_norm"][l] = rmsnorm_bwd(dh2, sv["x_mid"], ffn_norm[l] + token[0, 0], dx,
                                                   name="ffn_norm_bwd")
        dmix = matmul(dx_mid_b, wo[l], trans_b=True, name="out_proj_dx")
        big["w_out"][l] = by_rows(matmul(sv["mix"], dx_mid_b, trans_a=True, out_dtype=BF16, name="out_proj_dw"))
        if each:
            token = scatter(l, ("w_out",))
        (d_oa, d_ob, d_oc), small["mix_out_gain"][l] = mixnorm_bwd(
            dmix, [sv["out_a"], sv["out_b"], sv["out_c"]], mix_out_gain[l] + token[0, 0], name="mix_norm_bwd")
        dproj = lax.empty((s, in_width), BF16)
        dproj, db_a, dgq_a, dgk_a, dsink = banded_bwd(
            sv["proj"], cb_aq, cb_ak, cb_av, sa, gaq, gak, bias_a, (1,), sv["sinks"], True,
            d_oa, sv["out_a"], sv["lse_a"], dproj, name="swa_bwd")
        dproj = sb_bwd(sv["proj"], cb_bq, cb_bk, cb_bv, sb, d_ob, sv["tot_b"], dproj, name="stick_bwd")
        dproj, db_c, dgq_c, dgk_c = banded_bwd(
            sv["proj"], cb_cq, cb_ck, cb_cv, sc, gcq, gck, bias_c, DILATIONS, None, False,
            d_oc, sv["out_c"], sv["lse_c"], dproj, name="dilated_bwd")
        fold = lambda g: g.reshape(-1, HEAD_DIM).sum(axis=0)
        small["a_q_gain"][l], small["a_k_gain"][l] = fold(dgq_a), fold(dgk_a)
        small["c_q_gain"][l], small["c_k_gain"][l] = fold(dgq_c), fold(dgk_c)
        small["a_sinks"][l] = dsink[:, ::HEAD_DIM].reshape(-1)
        dbias_a = db_a if dbias_a is None else dbias_a + db_a
        dbias_c = db_c if dbias_c is None else dbias_c + db_c
        big["w_in"][l] = by_rows(matmul(dproj, sv["h1"], trans_a=True, out_dtype=BF16, name="in_proj_dw"))
        token = scatter(l, ("w_in",) if each else names_big)
        dh1 = matmul(dproj, wi[l], name="in_proj_dx")
        dx, dx_b, small["attn_norm"][l] = rmsnorm_bwd(dh1, sv["x_in"], attn_norm[l] + token[0, 0], dx_mid,
                                                name="attn_norm_bwd")

    flip = lambda t: jnp.swapaxes(t, 1, 2)
    w_big = dict(w_in=(flip(w_in), flip(m_w_in), flip(v_w_in)), w_out=(w_out, m_w_out, v_w_out),
                 w_up=(w_up, m_w_up, v_w_up), w_down=(w_down, m_w_down, v_w_down))
    results = {k: [lax.empty(w_big[k][0].shape, F32) for _ in range(4)] for k in names_big}

    def update(l, names, after):
        landed = exchange_wait([scatters[l, n] for n in names], True, after,
                               name=f"scatter_wait_{l}_{names_big.index(names[0])}")
        for k, parts in zip(names, landed):
            results[k] = adamw_parts(parts, *w_big[k], l, results[k], name="adamw_large")
        return results[names[-1]][0]

    after = dx_b
    for l in reversed(range(1, depth)):
        after = update(l, names_big, after)

    dtable = jnp.concatenate([bias_bwd(dbias_a, buckets_a, name="swa_bias_bwd"),
                              bias_bwd(dbias_c, buckets_c, name="dilated_bias_bwd")], axis=1)

    order = ("attn_norm", "a_q_gain", "a_k_gain", "a_sinks", "c_q_gain", "c_k_gain", "rel_bias_table",
             "mix_out_gain", "ffn_norm", "conv_w", "conv_b")
    partial = {k: jnp.stack(v) for k, v in small.items()}
    partial["rel_bias_table"] = dtable
    pieces = [partial[k] for k in order] + [loss_part.reshape(1)]
    n_small = sum(math.prod(pc.shape) for pc in pieces)
    rows = -(-n_small // (8 * LANES)) * 8
    packed, after = lax.optimization_barrier((_pack(pieces, rows), after))
    gathered = gather_small(packed, name="gather_small_grads")
    summed = _unpack(sum_devices(gathered.reshape(N_DEV, rows, LANES), name="sum_small_grads"),
                     [pc.shape for pc in pieces])
    g_small = dict(zip(order, summed[:-1]))
    loss = summed[-1][0]
    g_small["conv_w"] = lax.dynamic_slice_in_dim(g_small["conv_w"], dev * ff2_shard, ff2_shard, axis=2)

    w_small = dict(attn_norm=attn_norm, a_q_gain=a_q_gain, a_k_gain=a_k_gain, a_sinks=a_sinks, c_q_gain=c_q_gain,
                   c_k_gain=c_k_gain, rel_bias_table=rel_bias_table, mix_out_gain=mix_out_gain, ffn_norm=ffn_norm,
                   conv_w=conv_w, conv_b=conv_b)
    m_small = dict(attn_norm=m_attn_norm, a_q_gain=m_a_q_gain, a_k_gain=m_a_k_gain, a_sinks=m_a_sinks,
                   c_q_gain=m_c_q_gain, c_k_gain=m_c_k_gain, rel_bias_table=m_rel_bias_table,
                   mix_out_gain=m_mix_out_gain, ffn_norm=m_ffn_norm, conv_w=m_conv_w, conv_b=m_conv_b)
    v_small = dict(attn_norm=v_attn_norm, a_q_gain=v_a_q_gain, a_k_gain=v_a_k_gain, a_sinks=v_a_sinks,
                   c_q_gain=v_c_q_gain, c_k_gain=v_c_k_gain, rel_bias_table=v_rel_bias_table,
                   mix_out_gain=v_mix_out_gain, ffn_norm=v_ffn_norm, conv_w=v_conv_w, conv_b=v_conv_b)
    shapes = [w_small[k].shape for k in order]
    n_upd = sum(math.prod(sh) for sh in shapes)
    urows = -(-n_upd // (8 * LANES)) * 8
    packs = [_pack([d[k] for k in order], urows) for d in (g_small, w_small, m_small, v_small)]
    upd = adamw_small(*packs, name="adamw_small")
    delta_s, newm_s, newv_s = [dict(zip(order, _unpack(u, shapes))) for u in upd]

    after = update(0, names_big[1:], upd[0])
    update(0, names_big[:1], after)
    results["w_in"] = [flip(t) for t in results["w_in"]]
    g_big, delta_b, newm_b, newv_b = [{k: results[k][i] for k in names_big} for i in range(4)]

    all_names = ("attn_norm", "w_in", "a_q_gain", "a_k_gain", "a_sinks", "c_q_gain", "c_k_gain", "rel_bias_table",
                 "mix_out_gain", "w_out", "ffn_norm", "w_up", "conv_w", "conv_b", "w_down")
    pick = lambda sm, bg: [bg[k] if k in bg else sm[k] for k in all_names]
    return (loss, dx[None], *pick(g_small, g_big), *pick(delta_s, delta_b), *pick(newm_s, newm_b),
            *pick(newv_s, newv_b))
```

```python
import functools
import math

import jax
import jax.numpy as jnp
from jax import lax
from jax.experimental import pallas as pl
from jax.experimental.pallas import tpu as pltpu

F32, BF16, I32 = jnp.float32, jnp.bfloat16, jnp.int32
MESH = pl.DeviceIdType.MESH

HEAD_DIM = 64
LANES = 128
BLOCK = 128
EPS = 1e-6
NEG_INF = -1e30
N_BUCKETS = 32
T5_MAX_DIST = 2048
WINDOW_A = 128
DILATIONS = (1, 4, 16)
N_DEV = 8
VMEM_LIMIT = 56 * 1024 * 1024
MATMUL_VMEM = 46 * 1024 * 1024

ADAM_LR, ADAM_B1, ADAM_B2, ADAM_EPS, ADAM_WD, ADAM_STEP = 0.001, 0.9, 0.999, 1e-08, 0.01, 10


def _params(sem=None, vmem=None):
    return pltpu.CompilerParams(dimension_semantics=sem, vmem_limit_bytes=vmem)


def _pick(n, cands):
    for c in cands:
        if n % c == 0:
            return c
    raise ValueError(f"no tile for {n}")


def _dot(a, b):
    return lax.dot_general(a, b, (((1,), (0,)), ((), ())), preferred_element_type=F32)


def _dot_nt(a, b):
    return lax.dot_general(a, b, (((1,), (1,)), ((), ())), preferred_element_type=F32)


def _dot_tn(a, b):
    return lax.dot_general(a, b, (((0,), (0,)), ((), ())), preferred_element_type=F32)


def matmul(a, b, *, trans_a=False, trans_b=False, out_dtype=F32, res=None, col_blocks=None, name):
    a_halves, b_halves = a.ndim == 3, b.ndim == 3
    assert not (a_halves and trans_a) and not (b_halves and trans_b)
    m, k = (a.shape[1], 2 * a.shape[2]) if a_halves else (a.shape[1], a.shape[0]) if trans_a else a.shape
    n = 2 * b.shape[2] if b_halves else b.shape[0] if trans_b else b.shape[1]
    k_unit, n_unit = (k // 2 if a_halves else k), (n // 2 if b_halves else n)
    tm_cands = tuple(t for t in (1408, 1024, 896, 512, 256) if m % t == 0)
    tn_cands = ((n // col_blocks,) if col_blocks
                else tuple(t for t in (1024, 1408, 768, 512, 256, 128) if n_unit % t == 0))

    def footprint(tm, tk, tn):
        tiles = 2 * (tm * tk * a.dtype.itemsize + tk * tn * b.dtype.itemsize)
        return tiles + tm * tn * (4 + 2 * jnp.dtype(out_dtype).itemsize + (8 if res is not None else 0))

    tk, tm, tn = next((tk, tm, tn) for tk in (5376, 4096, 2816, 2048, 1792, 1024, 768, 512, 256) if k_unit % tk == 0
                      for tm in tm_cands if 2 * tm >= tm_cands[0]
                      for tn in tn_cands if footprint(tm, tk, tn) <= MATMUL_VMEM)
    nk = k // tk
    nk_half, nj_half = k_unit // tk, n_unit // tn
    dn = (((0 if trans_a else 1,), (1 if trans_b else 0,)), ((), ()))

    def body(*refs):
        if res is None:
            a_ref, b_ref, o_ref, acc = refs
        else:
            a_ref, b_ref, r_ref, o_ref, acc = refs
        kk = pl.program_id(2)

        @pl.when(kk == 0)
        def _():
            acc[...] = jnp.zeros_like(acc)

        acc[...] += lax.dot_general(a_ref[...].astype(BF16), b_ref[...].astype(BF16), dn,
                                    preferred_element_type=F32)

        @pl.when(kk == nk - 1)
        def _():
            r = acc[...]
            if res is not None:
                r = r_ref[...] + r
            o_ref[...] = r.astype(out_dtype)

    b_spec = (pl.BlockSpec((tn, tk), lambda i, j, kk: (j, kk)) if trans_b
              else pl.BlockSpec((None, tk, tn), lambda i, j, kk: (j // nj_half, kk, j % nj_half)) if b_halves
              else pl.BlockSpec((tk, tn), lambda i, j, kk: (kk, j)))
    a_spec = (pl.BlockSpec((tk, tm), lambda i, j, kk: (kk, i)) if trans_a
              else pl.BlockSpec((None, tm, tk), lambda i, j, kk: (kk // nk_half, i, kk % nk_half)) if a_halves
              else pl.BlockSpec((tm, tk), lambda i, j, kk: (i, kk)))
    in_specs = [a_spec, b_spec]
    args = [a, b]
    if res is not None:
        in_specs.append(pl.BlockSpec((tm, tn), lambda i, j, kk: (i, j)))
        args.append(res)
    if col_blocks:
        out_spec = pl.BlockSpec((None, tm, tn), lambda i, j, kk: (j, i, 0))
        out_shape = jax.ShapeDtypeStruct((col_blocks, m, tn), out_dtype)
    else:
        out_spec = pl.BlockSpec((tm, tn), lambda i, j, kk: (i, j))
        out_shape = jax.ShapeDtypeStruct((m, n), out_dtype)
    return pl.pallas_call(
        body, name=name, grid=(m // tm, n // tn, nk),
        in_specs=in_specs, out_specs=out_spec, out_shape=out_shape,
        scratch_shapes=[pltpu.VMEM((tm, tn), F32)],
        compiler_params=_params(("parallel", "parallel", "arbitrary"), VMEM_LIMIT),
    )(*args)


def rmsnorm_fwd(x, g, *, name):
    s, d = x.shape
    tm = 512

    def body(x_ref, g_ref, o_ref):
        xv = x_ref[...]
        r = lax.rsqrt(jnp.mean(xv * xv, axis=-1, keepdims=True) + EPS)
        o_ref[...] = (xv * r * g_ref[...]).astype(BF16)

    return pl.pallas_call(
        body, name=name, grid=(s // tm,),
        in_specs=[pl.BlockSpec((tm, d), lambda i: (i, 0)), pl.BlockSpec((1, d), lambda i: (0, 0))],
        out_specs=pl.BlockSpec((tm, d), lambda i: (i, 0)),
        out_shape=jax.ShapeDtypeStruct((s, d), BF16),
        compiler_params=_params(("parallel",)),
    )(x, g.reshape(1, d))


def rmsnorm_bwd(dh, x, g, dres, *, name):
    s, d = x.shape
    tm = 512

    def body(dh_ref, x_ref, g_ref, dres_ref, dx_ref, dxb_ref, dg_ref):
        @pl.when(pl.program_id(0) == 0)
        def _():
            dg_ref[...] = jnp.zeros_like(dg_ref)

        xv, dhv = x_ref[...], dh_ref[...]
        r = lax.rsqrt(jnp.mean(xv * xv, axis=-1, keepdims=True) + EPS)
        gd = dhv * g_ref[...]
        dot = jnp.mean(gd * xv, axis=-1, keepdims=True)
        dx = dres_ref[...] + (r * gd - xv * (r * r * r * dot))
        dx_ref[...] = dx
        dxb_ref[...] = dx.astype(BF16)
        dg_ref[...] += jnp.sum(dhv * (xv * r), axis=0, keepdims=True)

    row = pl.BlockSpec((tm, d), lambda i: (i, 0))
    dx, dxb, dg = pl.pallas_call(
        body, name=name, grid=(s // tm,),
        in_specs=[row, row, pl.BlockSpec((1, d), lambda i: (0, 0)), row],
        out_specs=[row, row, pl.BlockSpec((1, d), lambda i: (0, 0))],
        out_shape=[jax.ShapeDtypeStruct((s, d), F32), jax.ShapeDtypeStruct((s, d), BF16),
                   jax.ShapeDtypeStruct((1, d), F32)],
        compiler_params=_params(("arbitrary",), VMEM_LIMIT),
    )(dh, x, g.reshape(1, d), dres)
    return dx, dxb, dg[0]


def loss_head(y, target, *, name):
    s, d = y.shape
    tm = 512

    def body(y_ref, t_ref, dy_ref, dyb_ref, l_ref):
        @pl.when(pl.program_id(0) == 0)
        def _():
            l_ref[...] = jnp.zeros_like(l_ref)

        e = y_ref[...] - t_ref[...]
        dy = e / float(d)
        dy_ref[...] = dy
        dyb_ref[...] = dy.astype(BF16)
        per_tok = jnp.mean(e * e, axis=-1, keepdims=True)
        l_ref[...] += 0.5 * jnp.sum(per_tok, axis=0, keepdims=True)

    row = pl.BlockSpec((tm, d), lambda i: (i, 0))
    dy, dyb, l = pl.pallas_call(
        body, name=name, grid=(s // tm,),
        in_specs=[row, row],
        out_specs=[row, row, pl.BlockSpec((8, LANES), lambda i: (0, 0))],
        out_shape=[jax.ShapeDtypeStruct((s, d), F32), jax.ShapeDtypeStruct((s, d), BF16),
                   jax.ShapeDtypeStruct((8, LANES), F32)],
        compiler_params=_params(("arbitrary",)),
    )(y, target)
    return dy, dyb, l[0, 0]


FFN_TN = 256
FFN_CH = 256


def _rows_before(ref, r0, first):
    if first:
        cur = ref[pl.ds(0, FFN_CH), :]
        row = lax.broadcasted_iota(I32, cur.shape, 0)
        sh1 = jnp.where(row < 1, 0.0, pltpu.roll(cur, 1, axis=0))
        sh2 = jnp.where(row < 2, 0.0, pltpu.roll(cur, 2, axis=0))
        return cur, sh1, sh2
    ext = ref[pl.ds(pl.multiple_of(r0 - 8, 8), FFN_CH + 8), :]
    return ext[8:], pltpu.roll(ext, 1, axis=0)[8:], pltpu.roll(ext, 2, axis=0)[8:]


def _rows_after(ref, r0, last):
    if last:
        cur = ref[pl.ds(r0, FFN_CH), :]
        row = lax.broadcasted_iota(I32, cur.shape, 0)
        up1 = jnp.where(row >= FFN_CH - 1, 0.0, pltpu.roll(cur, FFN_CH - 1, axis=0))
        up2 = jnp.where(row >= FFN_CH - 2, 0.0, pltpu.roll(cur, FFN_CH - 2, axis=0))
        return cur, up1, up2
    n = FFN_CH + 8
    ext = ref[pl.ds(r0, n), :]
    return ext[:FFN_CH], pltpu.roll(ext, n - 1, axis=0)[:FFN_CH], pltpu.roll(ext, n - 2, axis=0)[:FFN_CH]


def _sigmoid(x):
    return 0.5 * jnp.tanh(0.5 * x) + 0.5


def ffn_act_fwd(p, conv_w, conv_b, *, name):
    s, f2 = p.shape
    f = f2 // 2
    nj = f // FFN_TN
    nch = s // FFN_CH

    def body(pg_ref, pu_ref, wg_ref, wu_ref, bg_ref, bu_ref, a_ref):
        def conv(ref, w_ref, b_ref, r0, first):
            cur, sh1, sh2 = _rows_before(ref, r0, first)
            return ((b_ref[...] + w_ref[0:1, :] * sh2) + w_ref[1:2, :] * sh1) + w_ref[2:3, :] * cur

        def chunk(r0, first):
            gate = conv(pg_ref, wg_ref, bg_ref, r0, first)
            up = conv(pu_ref, wu_ref, bu_ref, r0, first)
            a_ref[pl.ds(r0, FFN_CH), :] = (gate * _sigmoid(gate) * up).astype(BF16)

        chunk(0, True)

        def step(c, carry):
            chunk(pl.multiple_of(c * FFN_CH, FFN_CH), False)
            return carry

        lax.fori_loop(1, nch, step, 0)

    col = lambda off: pl.BlockSpec((s, FFN_TN), lambda j: (0, j + off))
    wcol = lambda off: pl.BlockSpec((3, FFN_TN), lambda j: (0, j + off))
    bcol = lambda off: pl.BlockSpec((1, FFN_TN), lambda j: (0, j + off))
    return pl.pallas_call(
        body, name=name, grid=(nj,),
        in_specs=[col(0), col(nj), wcol(0), wcol(nj), bcol(0), bcol(nj)],
        out_specs=pl.BlockSpec((s, FFN_TN), lambda j: (0, j)),
        out_shape=jax.ShapeDtypeStruct((s, f), BF16),
        compiler_params=_params(("parallel",), VMEM_LIMIT),
    )(p, p, conv_w, conv_w, conv_b.reshape(1, f2), conv_b.reshape(1, f2))


def ffn_act_bwd(da, p, conv_w, conv_b, *, name):
    s, f2 = p.shape
    f = f2 // 2
    nj = f // FFN_TN
    nch = s // FFN_CH

    def body(da_ref, pg_ref, pu_ref, wg_ref, wu_ref, bg_ref, bu_ref,
             dp_ref, dwg_ref, dwu_ref, dbg_ref, dbu_ref, dug_s, duu_s):
        dpg_ref, dpu_ref = dp_ref.at[0], dp_ref.at[1]
        def conv(ref, w_ref, b_ref, r0, first):
            cur, sh1, sh2 = _rows_before(ref, r0, first)
            u = ((b_ref[...] + w_ref[0:1, :] * sh2) + w_ref[1:2, :] * sh1) + w_ref[2:3, :] * cur
            return u, (sh2, sh1, cur)

        def taps_sum(du, taps):
            return jnp.concatenate([jnp.sum(du * t, axis=0, keepdims=True) for t in taps], axis=0)

        def chunk(r0, first, acc):
            dwg, dwu, dbg, dbu = acc
            gate, tg = conv(pg_ref, wg_ref, bg_ref, r0, first)
            up, tu = conv(pu_ref, wu_ref, bu_ref, r0, first)
            dav = da_ref[pl.ds(r0, FFN_CH), :]
            sg = _sigmoid(gate)
            dgate = dav * up * (sg * (1.0 + gate * (1.0 - sg)))
            dup = dav * (gate * sg)
            dug_s[pl.ds(r0, FFN_CH), :] = dgate
            duu_s[pl.ds(r0, FFN_CH), :] = dup
            return (dwg + taps_sum(dgate, tg), dwu + taps_sum(dup, tu),
                    dbg + jnp.sum(dgate, axis=0, keepdims=True), dbu + jnp.sum(dup, axis=0, keepdims=True))

        z3 = jnp.zeros((3, FFN_TN), F32)
        z1 = jnp.zeros((1, FFN_TN), F32)
        acc = chunk(0, True, (z3, z3, z1, z1))
        acc = lax.fori_loop(1, nch, lambda c, a: chunk(pl.multiple_of(c * FFN_CH, FFN_CH), False, a), acc)
        dwg_ref[...], dwu_ref[...], dbg_ref[...], dbu_ref[...] = acc

        def back(src, w_ref, dst, r0, last):
            cur, up1, up2 = _rows_after(src, r0, last)
            dst[pl.ds(r0, FFN_CH), :] = (w_ref[2:3, :] * cur + w_ref[1:2, :] * up1 + w_ref[0:1, :] * up2).astype(BF16)

        def step(c, carry):
            r0 = pl.multiple_of(c * FFN_CH, FFN_CH)
            back(dug_s, wg_ref, dpg_ref, r0, False)
            back(duu_s, wu_ref, dpu_ref, r0, False)
            return carry

        lax.fori_loop(0, nch - 1, step, 0)
        back(dug_s, wg_ref, dpg_ref, (nch - 1) * FFN_CH, True)
        back(duu_s, wu_ref, dpu_ref, (nch - 1) * FFN_CH, True)

    col = lambda off: pl.BlockSpec((s, FFN_TN), lambda j: (0, j + off))
    wcol = lambda off: pl.BlockSpec((3, FFN_TN), lambda j: (0, j + off))
    bcol = lambda off: pl.BlockSpec((1, FFN_TN), lambda j: (0, j + off))
    outs = pl.pallas_call(
        body, name=name, grid=(nj,),
        in_specs=[col(0), col(0), col(nj), wcol(0), wcol(nj), bcol(0), bcol(nj)],
        out_specs=[pl.BlockSpec((2, s, FFN_TN), lambda j: (0, 0, j)), wcol(0), wcol(0), bcol(0), bcol(0)],
        out_shape=[jax.ShapeDtypeStruct((2, s, f), BF16),
                   jax.ShapeDtypeStruct((3, f), F32), jax.ShapeDtypeStruct((3, f), F32),
                   jax.ShapeDtypeStruct((1, f), F32), jax.ShapeDtypeStruct((1, f), F32)],
        scratch_shapes=[pltpu.VMEM((s, FFN_TN), F32), pltpu.VMEM((s, FFN_TN), F32)],
        compiler_params=_params(("parallel",), VMEM_LIMIT),
    )(da, p, p, conv_w, conv_w, conv_b.reshape(1, f2), conv_b.reshape(1, f2))
    dp, dwg, dwu, dbg, dbu = outs
    return dp, jnp.concatenate([dwg, dwu], axis=1), jnp.concatenate([dbg, dbu], axis=1)[0]


def mixnorm_fwd(outs, gain, *, name):
    s = outs[0].shape[0]
    widths = [o.shape[1] for o in outs]
    total = sum(widths)
    tm = 512

    def body(*refs):
        o_refs, g_ref, m_ref = refs[:-2], refs[-2], refs[-1]
        off = 0
        for o_ref, w in zip(o_refs, widths):
            xv = o_ref[...]
            r = lax.rsqrt(jnp.mean(xv * xv, axis=-1, keepdims=True) + EPS)
            m_ref[:, off:off + w] = (xv * r * g_ref[:, off:off + w]).astype(BF16)
            off += w

    return pl.pallas_call(
        body, name=name, grid=(s // tm,),
        in_specs=[pl.BlockSpec((tm, w), lambda i: (i, 0)) for w in widths] + [pl.BlockSpec((1, total), lambda i: (0, 0))],
        out_specs=pl.BlockSpec((tm, total), lambda i: (i, 0)),
        out_shape=jax.ShapeDtypeStruct((s, total), BF16),
        compiler_params=_params(("parallel",)),
    )(*outs, gain.reshape(1, total))


def mixnorm_bwd(dmix, outs, gain, *, name):
    s = outs[0].shape[0]
    widths = [o.shape[1] for o in outs]
    total = sum(widths)
    n = len(outs)
    tm = 512

    def body(*refs):
        dm_ref, o_refs, g_ref = refs[0], refs[1:1 + n], refs[1 + n]
        d_refs, dg_ref = refs[2 + n:2 + 2 * n], refs[2 + 2 * n]

        @pl.when(pl.program_id(0) == 0)
        def _():
            dg_ref[...] = jnp.zeros_like(dg_ref)

        off = 0
        for o_ref, d_ref, w in zip(o_refs, d_refs, widths):
            xv = o_ref[...]
            dhv = dm_ref[:, off:off + w]
            r = lax.rsqrt(jnp.mean(xv * xv, axis=-1, keepdims=True) + EPS)
            gd = dhv * g_ref[:, off:off + w]
            dot = jnp.mean(gd * xv, axis=-1, keepdims=True)
            d_ref[...] = r * gd - xv * (r * r * r * dot)
            dg_ref[:, off:off + w] += jnp.sum(dhv * (xv * r), axis=0, keepdims=True)
            off += w

    res = pl.pallas_call(
        body, name=name, grid=(s // tm,),
        in_specs=[pl.BlockSpec((tm, total), lambda i: (i, 0))]
        + [pl.BlockSpec((tm, w), lambda i: (i, 0)) for w in widths] + [pl.BlockSpec((1, total), lambda i: (0, 0))],
        out_specs=[pl.BlockSpec((tm, w), lambda i: (i, 0)) for w in widths] + [pl.BlockSpec((1, total), lambda i: (0, 0))],
        out_shape=[jax.ShapeDtypeStruct((s, w), F32) for w in widths] + [jax.ShapeDtypeStruct((1, total), F32)],
        compiler_params=_params(("arbitrary",)),
    )(dmix, *outs, gain.reshape(1, total))
    return res[:n], res[n][0]


NORM_CH = 512
FWD_TILES = 4
BWD_TILES = 4


def _lo_mask(shape):
    return lax.broadcasted_iota(I32, shape, 1) < HEAD_DIM


def _head_sum(x, lo):
    del lo
    i = lax.broadcasted_iota(I32, (LANES, LANES), 0) // HEAD_DIM
    j = lax.broadcasted_iota(I32, (LANES, LANES), 1) // HEAD_DIM
    return _split_dot(x, _twice(i == j))


def _head_stats(x, lo):
    return lax.rsqrt(_head_sum(x * x, lo) * (1.0 / HEAD_DIM) + EPS)


def _swap_halves(x):
    return pltpu.roll(x, HEAD_DIM, axis=1)


def _replicate_head(x, lo, use_lo_head):
    sw = _swap_halves(x)
    return jnp.where(use_lo_head, jnp.where(lo, x, sw), jnp.where(lo, sw, x))


def _tile_rows(i, s, d):
    nb = s // (BLOCK * d)
    r = i // nb
    b = i % nb
    start = r + (BLOCK * d) * b
    prev = start - (BLOCK * d) * jnp.minimum(b, 1)
    return start, prev, b > 0


def _rows(ref, start, d):
    if d == 1:
        return ref[pl.ds(pl.multiple_of(start, BLOCK), BLOCK), :]
    return ref[pl.ds(start, BLOCK, stride=d), :]


def _set_rows(ref, start, d, val):
    if d == 1:
        ref[pl.ds(pl.multiple_of(start, BLOCK), BLOCK), :] = val
    else:
        ref[pl.ds(start, BLOCK, stride=d), :] = val


def banded_fwd(proj, qb0, kb0, vb0, n_slabs, gq, gk, bias, dils, sinks, gqa, *, name):
    s = proj.shape[0]
    nbr = len(dils)
    nt = s // BLOCK
    nch = s // NORM_CH
    has_sink = sinks is not None

    def body(*refs):
        q_ref, k_ref, v_ref, gq_ref, gk_ref, b_ref = refs[:6]
        rest = refs[6:]
        if has_sink:
            sink_ref, rest = rest[0], rest[1:]
        out_ref, lse_ref, qn_s, kn_s, vv_s, o_s, l_s = rest
        p = pl.program_id(0)
        use_lo = (p // 2) == 0

        def prep(c, carry):
            rows = pl.ds(pl.multiple_of(c * NORM_CH, NORM_CH), NORM_CH)
            lo = _lo_mask((NORM_CH, LANES))
            qv, kv, vv = q_ref[rows, :], k_ref[rows, :], v_ref[rows, :]
            qn_s[rows, :] = qv * _head_stats(qv, lo) * gq_ref[...] * (HEAD_DIM ** -0.5)
            kn = kv * _head_stats(kv, lo) * gk_ref[...]
            if gqa:
                kn = _replicate_head(kn, lo, use_lo)
                vv = _replicate_head(vv, lo, use_lo)
            kn_s[rows, :] = kn
            vv_s[rows, :] = vv
            return carry

        lax.fori_loop(0, nch, prep, 0)

        lo = _lo_mask((BLOCK, LANES))
        hms = [lo, jnp.logical_not(lo)]
        heads, tiles = range(2), range(FWD_TILES)
        for br, d in enumerate(dils):
            def step(ii, carry, br=br, d=d):
                pos = [_tile_rows(ii * FWD_TILES + u, s, d) for u in tiles]
                kc = [carry[0]] + [_rows(kn_s, pos[u][0], d).astype(BF16) for u in tiles]
                vc = [carry[1]] + [_rows(vv_s, pos[u][0], d).astype(BF16) for u in tiles]
                kcat = [jnp.concatenate([kc[u], kc[u + 1]], axis=0) for u in tiles]
                vcat = [jnp.concatenate([vc[u], vc[u + 1]], axis=0) for u in tiles]
                qt = [_rows(qn_s, pos[u][0], d) for u in tiles]
                sc = [[_dot_nt(jnp.where(hms[h], qt[u], 0.0).astype(BF16), kcat[u])
                       + b_ref[br, jnp.where(pos[u][2], 0, 1), h] for h in heads] for u in tiles]
                m = [[jnp.max(sc[u][h], axis=1, keepdims=True) for h in heads] for u in tiles]
                pe = [[jnp.exp(sc[u][h] - m[u][h]) for h in heads] for u in tiles]
                den = [[jnp.sum(pe[u][h], axis=1, keepdims=True) for h in heads] for u in tiles]
                o = [[_dot(pe[u][h].astype(BF16), vcat[u]) * (1.0 / den[u][h]) for h in heads] for u in tiles]
                for u in tiles:
                    _set_rows(o_s.at[br], pos[u][0], d, jnp.where(lo, o[u][0], o[u][1]))
                    _set_rows(l_s.at[br], pos[u][0], d,
                              jnp.where(lo, m[u][0] + jnp.log(den[u][0]), m[u][1] + jnp.log(den[u][1])))
                return kc[-1], vc[-1]

            none_yet = jnp.zeros((BLOCK, LANES), BF16)
            lax.fori_loop(0, nt // FWD_TILES, step, (none_yet, none_yet))

        def combine(c, carry):
            rows = pl.ds(pl.multiple_of(c * NORM_CH, NORM_CH), NORM_CH)
            ls = [l_s[br, rows, :] for br in range(nbr)]
            mx = functools.reduce(jnp.maximum, ls)
            if has_sink:
                mx = jnp.maximum(mx, sink_ref[...])
            tot = functools.reduce(jnp.add, [jnp.exp(l - mx) for l in ls])
            if has_sink:
                tot = tot + jnp.exp(sink_ref[...] - mx)
            lse = mx + jnp.log(tot)
            acc = jnp.exp(ls[0] - lse) * o_s[0, rows, :]
            for br in range(1, nbr):
                acc = acc + jnp.exp(ls[br] - lse) * o_s[br, rows, :]
            out_ref[rows, :] = acc
            lse_ref[rows, :] = lse
            return carry

        lax.fori_loop(0, nch, combine, 0)

    slab = lambda b0, shared: pl.BlockSpec((s, LANES), (lambda p: (0, b0)) if shared else (lambda p: (0, b0 + p)),
                                           pipeline_mode=pl.Buffered(1))
    vec = pl.BlockSpec((1, LANES), lambda p: (0, 0))
    in_specs = [slab(qb0, False), slab(kb0, gqa), slab(vb0, gqa), vec, vec,
                pl.BlockSpec((nbr, 2, 2, BLOCK, 2 * BLOCK), lambda p: (0, 0, p, 0, 0))]
    args = [proj, proj, proj, gq.reshape(1, LANES), gk.reshape(1, LANES), bias]
    if has_sink:
        in_specs.append(pl.BlockSpec((None, 1, LANES), lambda p: (p, 0, 0)))
        args.append(sinks)
    w = LANES * n_slabs
    return pl.pallas_call(
        body, name=name, grid=(n_slabs,),
        in_specs=in_specs,
        out_specs=[pl.BlockSpec((s, LANES), lambda p: (0, p)), pl.BlockSpec((s, LANES), lambda p: (0, p))],
        out_shape=[jax.ShapeDtypeStruct((s, w), F32), jax.ShapeDtypeStruct((s, w), F32)],
        scratch_shapes=[pltpu.VMEM((s, LANES), F32), pltpu.VMEM((s, LANES), F32), pltpu.VMEM((s, LANES), F32),
                        pltpu.VMEM((nbr, s, LANES), F32), pltpu.VMEM((nbr, s, LANES), F32)],
        compiler_params=_params(("parallel",), VMEM_LIMIT),
    )(*args)


def banded_bwd(proj, qb0, kb0, vb0, n_slabs, gq, gk, bias, dils, sinks, gqa, dout, out, lse, dproj, *, name):
    s = proj.shape[0]
    nbr = len(dils)
    nt = s // BLOCK
    nch = s // NORM_CH
    has_sink = sinks is not None
    scale = HEAD_DIM ** -0.5

    def body(*refs):
        q_ref, k_ref, v_ref, gq_ref, gk_ref, b_ref, do_ref, o_ref, lse_ref = refs[:9]
        rest = refs[9:]
        if has_sink:
            sink_ref, rest = rest[0], rest[1:]
        dproj_ref, db_ref, dgq_ref, dgk_ref = rest[1:5]
        rest = rest[5:]
        if has_sink:
            dsink_ref, rest = rest[0], rest[1:]
        qn_s, kn_s, vv_s, dl_s, dqn_s, dkn_s, dvv_s, dq_ref, dk_ref, dv_ref, stage, sems = rest
        p = pl.program_id(0)
        use_lo = (p // 2) == 0

        def prep(c, carry):
            rows = pl.ds(pl.multiple_of(c * NORM_CH, NORM_CH), NORM_CH)
            lo = _lo_mask((NORM_CH, LANES))
            qv, kv, vv = q_ref[rows, :], k_ref[rows, :], v_ref[rows, :]
            qn_s[rows, :] = qv * _head_stats(qv, lo) * gq_ref[...] * scale
            kn = kv * _head_stats(kv, lo) * gk_ref[...]
            if gqa:
                kn = _replicate_head(kn, lo, use_lo)
                vv = _replicate_head(vv, lo, use_lo)
            kn_s[rows, :] = kn
            vv_s[rows, :] = vv
            delta = _head_sum(do_ref[rows, :] * o_ref[rows, :], lo)
            odd = lax.broadcasted_iota(I32, (NORM_CH, LANES), 1) % 2 == 1
            dl_s[rows, :] = jnp.where(odd, delta, lse_ref[rows, :])
            z = jnp.zeros((NORM_CH, LANES), F32)
            dqn_s[rows, :] = z
            dkn_s[rows, :] = z
            dvv_s[rows, :] = z
            if has_sink:
                ps = jnp.exp(sink_ref[...] - lse_ref[rows, :])
                return carry - jnp.sum(ps * delta, axis=0, keepdims=True)
            return carry

        dsink = lax.fori_loop(0, nch, prep, jnp.zeros((1, LANES), F32))
        if has_sink:
            dsink_ref[...] = jnp.broadcast_to(dsink, (8, LANES))

        lo = _lo_mask((BLOCK, LANES))
        hms = [lo, jnp.logical_not(lo)]
        heads, tiles = range(2), range(BWD_TILES)
        for br, d in enumerate(dils):
            db_ref[br] = jnp.zeros((2, BLOCK, 2 * BLOCK), F32)

            def step(ii, carry, br=br, d=d):
                pos = [_tile_rows(ii * BWD_TILES + u, s, d) for u in tiles]
                kc = [carry[0]] + [_rows(kn_s, pos[u][0], d).astype(BF16) for u in tiles]
                vc = [carry[1]] + [_rows(vv_s, pos[u][0], d).astype(BF16) for u in tiles]
                kcat = [jnp.concatenate([kc[u], kc[u + 1]], axis=0) for u in tiles]
                vcat = [jnp.concatenate([vc[u], vc[u + 1]], axis=0) for u in tiles]
                qt = [_rows(qn_s, pos[u][0], d) for u in tiles]
                dot_ = [_rows(do_ref, pos[u][0], d) for u in tiles]
                st_t = [_rows(dl_s, pos[u][0], d) for u in tiles]
                qh = [[jnp.where(hms[h], qt[u], 0.0).astype(BF16) for h in heads] for u in tiles]
                doh = [[jnp.where(hms[h], dot_[u], 0.0).astype(BF16) for h in heads] for u in tiles]
                sc = [[_dot_nt(qh[u][h], kcat[u]) + b_ref[br, jnp.where(pos[u][2], 0, 1), h] for h in heads]
                      for u in tiles]
                dp = [[_dot_nt(doh[u][h], vcat[u]) for h in heads] for u in tiles]
                lane0 = [0, HEAD_DIM]
                pr = [[jnp.exp(sc[u][h] - st_t[u][:, lane0[h]:lane0[h] + 1]) for h in heads] for u in tiles]
                dlog = [[pr[u][h] * (dp[u][h] - st_t[u][:, lane0[h] + 1:lane0[h] + 2]) for h in heads] for u in tiles]
                for h in heads:
                    db_ref[br, h] += functools.reduce(jnp.add, [dlog[u][h] for u in tiles])
                dlb = [[dlog[u][h].astype(BF16) for h in heads] for u in tiles]
                prb = [[pr[u][h].astype(BF16) for h in heads] for u in tiles]
                dq_t = [jnp.where(lo, _dot(dlb[u][0], kcat[u]), _dot(dlb[u][1], kcat[u])) * scale for u in tiles]
                rows2 = lambda x: jnp.concatenate(x, axis=0)
                dk_t = [_dot_tn(rows2(dlb[u]), rows2(qh[u])) for u in tiles]
                dv_t = [_dot_tn(rows2(prb[u]), rows2(doh[u])) for u in tiles]
                for u in tiles:
                    start, prev = pos[u][0], pos[u][1]
                    _set_rows(dqn_s, start, d, _rows(dqn_s, start, d) + dq_t[u])
                    _set_rows(dkn_s, prev, d, _rows(dkn_s, prev, d) + dk_t[u][:BLOCK])
                    _set_rows(dkn_s, start, d, _rows(dkn_s, start, d) + dk_t[u][BLOCK:])
                    _set_rows(dvv_s, prev, d, _rows(dvv_s, prev, d) + dv_t[u][:BLOCK])
                    _set_rows(dvv_s, start, d, _rows(dvv_s, start, d) + dv_t[u][BLOCK:])
                return kc[-1], vc[-1]

            none_yet = jnp.zeros((BLOCK, LANES), BF16)
            lax.fori_loop(0, nt // BWD_TILES, step, (none_yet, none_yet))

        if gqa:
            @pl.when(p == 0)
            def _():
                dk_ref[...] = jnp.zeros_like(dk_ref)
                dv_ref[...] = jnp.zeros_like(dv_ref)

        def finish(c, carry):
            dgq, dgk = carry
            rows = pl.ds(pl.multiple_of(c * NORM_CH, NORM_CH), NORM_CH)
            lo = _lo_mask((NORM_CH, LANES))

            def norm_bwd(xv, dn, g_ref):
                r = _head_stats(xv, lo)
                gd = dn * g_ref[...]
                dot = _head_sum(gd * xv, lo) * (1.0 / HEAD_DIM)
                return r * gd - xv * (r * r * r * dot), dn * (xv * r)

            dq, gq_part = norm_bwd(q_ref[rows, :], dqn_s[rows, :], gq_ref)
            dq_ref[rows, :] = dq
            dgq = dgq + jnp.sum(gq_part, axis=0, keepdims=True)
            kv, dkn, dvv = k_ref[rows, :], dkn_s[rows, :], dvv_s[rows, :]
            if gqa:
                kv = _replicate_head(kv, lo, use_lo)
                dkn = dkn + _swap_halves(dkn)
                dvv = dvv + _swap_halves(dvv)
                lane = lax.broadcasted_iota(I32, (NORM_CH, LANES), 1)
                mine = (lane // HEAD_DIM) == (p // 2)
                dk, gk_part = norm_bwd(kv, dkn, gk_ref)
                dk_ref[rows, :] += jnp.where(mine, dk, 0.0)
                dv_ref[rows, :] += jnp.where(mine, dvv, 0.0)
                gk_part = jnp.where(lo, gk_part, 0.0)
            else:
                dk, gk_part = norm_bwd(kv, dkn, gk_ref)
                dk_ref[rows, :] = dk
                dv_ref[rows, :] = dvv
            dgk = dgk + jnp.sum(gk_part, axis=0, keepdims=True)
            return dgq, dgk

        z = jnp.zeros((1, LANES), F32)
        dgq, dgk = lax.fori_loop(0, nch, finish, (z, z))
        dgq_ref[...] = jnp.broadcast_to(dgq, (8, LANES))
        dgk_ref[...] = jnp.broadcast_to(dgk, (8, LANES))
        if gqa:
            _store_slabs((dq_ref,), stage, dproj_ref, sems, (qb0 + p,))

            @pl.when(p == n_slabs - 1)
            def _():
                _store_slabs((dk_ref, dv_ref), stage, dproj_ref, sems, (kb0, vb0))
        else:
            _store_slabs((dq_ref, dk_ref, dv_ref), stage, dproj_ref, sems, (qb0 + p, kb0 + p, vb0 + p))

    def slab_of(b0, shared):
        return pl.BlockSpec((s, LANES), (lambda p: (0, b0)) if shared else (lambda p: (0, b0 + p)),
                            pipeline_mode=pl.Buffered(1))

    vec = pl.BlockSpec((1, LANES), lambda p: (0, 0))
    own = pl.BlockSpec((s, LANES), lambda p: (0, p), pipeline_mode=pl.Buffered(1))
    in_specs = [slab_of(qb0, False), slab_of(kb0, gqa), slab_of(vb0, gqa), vec, vec,
                pl.BlockSpec((nbr, 2, 2, BLOCK, 2 * BLOCK), lambda p: (0, 0, p, 0, 0)), own, own, own]
    args = [proj, proj, proj, gq.reshape(1, LANES), gk.reshape(1, LANES), bias, dout, out, lse]
    if has_sink:
        in_specs.append(pl.BlockSpec((None, 1, LANES), lambda p: (p, 0, 0)))
        args.append(sinks)
    held = pl.BlockSpec(memory_space=pl.ANY)
    in_specs.append(held)
    args.append(dproj)
    part = pl.BlockSpec((None, 8, LANES), lambda p: (p, 0, 0))
    out_specs = [held, pl.BlockSpec((nbr, 2, BLOCK, 2 * BLOCK), lambda p: (0, p, 0, 0)), part, part]
    out_shape = [jax.ShapeDtypeStruct(dproj.shape, dproj.dtype),
                 jax.ShapeDtypeStruct((nbr, 2 * n_slabs, BLOCK, 2 * BLOCK), F32),
                 jax.ShapeDtypeStruct((n_slabs, 8, LANES), F32), jax.ShapeDtypeStruct((n_slabs, 8, LANES), F32)]
    if has_sink:
        out_specs.append(part)
        out_shape.append(jax.ShapeDtypeStruct((n_slabs, 8, LANES), F32))
    res = pl.pallas_call(
        body, name=name, grid=(n_slabs,),
        in_specs=in_specs, out_specs=out_specs, out_shape=out_shape,
        input_output_aliases={len(args) - 1: 0},
        scratch_shapes=[pltpu.VMEM((s, LANES), F32) for _ in range(10)]
        + [pltpu.VMEM((3, s, LANES), BF16), pltpu.SemaphoreType.DMA((3,))],
        compiler_params=_params(("arbitrary",), VMEM_LIMIT),
    )(*args)
    outs = [res[0], res[1], res[2][:, 0, :], res[3][:, 0, :]]
    if has_sink:
        outs.append(res[4][:, 0, :])
    return outs


def bias_bwd(dbias, buckets, *, name):
    nbr, h = dbias.shape[:2]

    def body(db_ref, bk_ref, o_ref):
        lane = lax.broadcasted_iota(I32, (1, LANES), 1)
        acc = jnp.zeros((1, LANES), F32)
        for b in range(N_BUCKETS):
            tot = jnp.zeros((1, 1), F32)
            for br in range(nbr):
                sel = jnp.where(bk_ref[br] == b, db_ref[br], 0.0)
                tot = tot + jnp.sum(jnp.sum(sel, axis=0, keepdims=True), axis=1, keepdims=True)
            acc = jnp.where(lane == b, tot, acc)
        o_ref[...] = jnp.broadcast_to(acc, (8, LANES))

    res = pl.pallas_call(
        body, name=name, grid=(h,),
        in_specs=[pl.BlockSpec((nbr, None, BLOCK, 2 * BLOCK), lambda i: (0, i, 0, 0)),
                  pl.BlockSpec((nbr, BLOCK, 2 * BLOCK), lambda i: (0, 0, 0))],
        out_specs=pl.BlockSpec((None, 8, LANES), lambda i: (i, 0, 0)),
        out_shape=jax.ShapeDtypeStruct((h, 8, LANES), F32),
        compiler_params=_params(("parallel",)),
    )(dbias, buckets)
    return res[:, 0, :N_BUCKETS].T


SB_KG = 512
SB_QT = 2


def _softplus(z):
    return jnp.maximum(z, 0.0) + jnp.log(1.0 + jnp.exp(-jnp.abs(z)))


def _twice(t):
    t = t.astype(BF16)
    return jnp.concatenate([t, t], axis=0)


def _split_dot(x, t2):
    hi = x.astype(BF16)
    lo = (x - hi.astype(F32)).astype(BF16)
    return _dot(jnp.concatenate([hi, lo], axis=1), t2)


def sb_fwd(proj, qb0, kb0, vb0, n_slabs, *, name):
    s = proj.shape[0]
    nq = s // BLOCK
    nch = s // NORM_CH
    scale = HEAD_DIM ** -0.5

    def body(q_ref, k_ref, v_ref, o_ref, tot_ref, qlo_s, qhi_s, k_s, v_s):
        def prep(c, carry):
            rows = pl.ds(pl.multiple_of(c * NORM_CH, NORM_CH), NORM_CH)
            lo = _lo_mask((NORM_CH, LANES))
            qv = q_ref[rows, :] * scale
            qlo_s[rows, :] = jnp.where(lo, qv, 0.0).astype(BF16)
            qhi_s[rows, :] = jnp.where(lo, 0.0, qv).astype(BF16)
            k_s[rows, :] = k_ref[rows, :].astype(BF16)
            v_s[rows, :] = v_ref[rows, :].astype(BF16)
            return carry

        lax.fori_loop(0, nch, prep, 0)

        row = lax.broadcasted_iota(I32, (BLOCK, BLOCK), 0)
        col = lax.broadcasted_iota(I32, (BLOCK, BLOCK), 1)
        lo = col < HEAD_DIM
        t_ge = _twice(row >= col)
        rowg = lax.broadcasted_iota(I32, (BLOCK, SB_KG), 0)
        colg = lax.broadcasted_iota(I32, (BLOCK, SB_KG), 1)

        nsub = SB_KG // BLOCK
        chains = range(2 * SB_QT)
        nc = len(chains)

        def qloop(qs, phase):
            q0 = pl.multiple_of(qs * (SB_QT * BLOCK), SB_QT * BLOCK)
            qh = [(qlo_s, qhi_s)[i % 2][pl.ds(q0 + (i // 2) * BLOCK, BLOCK), :] for i in chains]
            gd = (qs * SB_QT) // nsub

            def logits(gi):
                k0 = pl.multiple_of(gi * SB_KG, SB_KG)
                kg = k_s[pl.ds(k0, SB_KG), :]
                return [_dot_nt(qh[i], kg) for i in chains]

            def group(gi, st, masks, npiece=nsub):
                k0 = pl.multiple_of(gi * SB_KG, SB_KG)
                vg = v_s[pl.ds(k0, npiece * BLOCK), :]
                c, o, z = list(st[:nc]), st[nc:2 * nc], st[2 * nc:]
                z_next = logits(jnp.maximum(gi - 1, 0))
                piece = lambda x, j: x[:, j * BLOCK:(j + 1) * BLOCK]
                a = [[None] * npiece for _ in chains]
                for j in reversed(range(npiece)):
                    zj = [piece(z[i], j) for i in chains]
                    lrem = [-_softplus(zj[i]) for i in chains]
                    if masks is not None:
                        lrem = [jnp.where(piece(masks[i // 2], j), lrem[i], 0.0) for i in chains]
                    incl = [_split_dot(lrem[i], t_ge) for i in chains]
                    for i in chains:
                        aij = jnp.exp(zj[i] + (c[i] + incl[i]))
                        if masks is not None:
                            aij = jnp.where(piece(masks[i // 2], j), aij, 0.0)
                        a[i][j] = aij.astype(BF16)
                        c[i] = c[i] + incl[i][:, 0:1]
                o = [o[i] + _dot(jnp.concatenate(a[i], axis=1), vg) for i in chains]
                return (*c, *o, *z_next)

            zc = [jnp.zeros((BLOCK, 1), F32)] * nc
            zo = [jnp.zeros((BLOCK, LANES), F32)] * nc
            masks = [(gd * SB_KG + colg) < (q0 + t * BLOCK + rowg) for t in range(SB_QT)]
            st = group(gd, (*zc, *zo, *logits(gd)), masks, (phase + 1) * SB_QT)
            st = lax.fori_loop(0, gd, lambda t, st: group(gd - 1 - t, st, None), st)
            for t in range(SB_QT):
                rows = pl.ds(q0 + t * BLOCK, BLOCK)
                o_ref[rows, :] = jnp.where(lo, st[nc + 2 * t], st[nc + 2 * t + 1])
                tot_ref[rows, :] = jnp.where(lo, st[2 * t], st[2 * t + 1])

        steps_per_group = nsub // SB_QT

        def per_group(g, carry):
            for phase in range(steps_per_group):
                qloop(g * steps_per_group + phase, phase)
            return carry

        lax.fori_loop(0, nq // nsub, per_group, 0)

    slab = lambda b0: pl.BlockSpec((s, LANES), lambda p: (0, b0 + p), pipeline_mode=pl.Buffered(1))
    w = LANES * n_slabs
    return pl.pallas_call(
        body, name=name, grid=(n_slabs,),
        in_specs=[slab(qb0), slab(kb0), slab(vb0)],
        out_specs=[pl.BlockSpec((s, LANES), lambda p: (0, p)), pl.BlockSpec((s, LANES), lambda p: (0, p))],
        out_shape=[jax.ShapeDtypeStruct((s, w), F32), jax.ShapeDtypeStruct((s, w), F32)],
        scratch_shapes=[pltpu.VMEM((s, LANES), BF16) for _ in range(4)],
        compiler_params=_params(("parallel",), VMEM_LIMIT),
    )(proj, proj, proj)


def _store_slabs(slabs, stage, dproj_ref, sems, blocks):
    s = stage.shape[1]

    def cast(c, carry):
        rows = pl.ds(pl.multiple_of(c * NORM_CH, NORM_CH), NORM_CH)
        for i, slab in enumerate(slabs):
            stage[i, rows, :] = slab[rows, :].astype(BF16)
        return carry

    lax.fori_loop(0, s // NORM_CH, cast, 0)
    copies = [pltpu.make_async_copy(stage.at[i], dproj_ref.at[:, pl.ds(pl.multiple_of(b * LANES, LANES), LANES)],
                                    sems.at[i]) for i, b in enumerate(blocks)]
    for cp in copies:
        cp.start()
    for cp in copies:
        cp.wait()


def sb_bwd(proj, qb0, kb0, vb0, n_slabs, dout, tot, dproj, *, name):
    s = proj.shape[0]
    nq = s // BLOCK
    nch = s // NORM_CH
    nsub = SB_KG // BLOCK
    scale = HEAD_DIM ** -0.5

    def body(q_ref, k_ref, v_ref, do_ref, tot_ref, dproj_in, dproj_ref,
             qlo_s, qhi_s, k_s, v_s, dlo_s, dhi_s, dq_ref, dk_ref, dv_ref, stage, sems):
        del dproj_in
        def prep(c, carry):
            rows = pl.ds(pl.multiple_of(c * NORM_CH, NORM_CH), NORM_CH)
            lo = _lo_mask((NORM_CH, LANES))
            qv = q_ref[rows, :] * scale
            dv = do_ref[rows, :]
            qlo_s[rows, :] = jnp.where(lo, qv, 0.0).astype(BF16)
            qhi_s[rows, :] = jnp.where(lo, 0.0, qv).astype(BF16)
            dlo_s[rows, :] = jnp.where(lo, dv, 0.0).astype(BF16)
            dhi_s[rows, :] = jnp.where(lo, 0.0, dv).astype(BF16)
            k_s[rows, :] = k_ref[rows, :].astype(BF16)
            v_s[rows, :] = v_ref[rows, :].astype(BF16)
            z = jnp.zeros((NORM_CH, LANES), F32)
            dk_ref[rows, :] = z
            dv_ref[rows, :] = z
            return carry

        lax.fori_loop(0, nch, prep, 0)

        row = lax.broadcasted_iota(I32, (BLOCK, BLOCK), 0)
        col = lax.broadcasted_iota(I32, (BLOCK, BLOCK), 1)
        lo = col < HEAD_DIM
        t_le = _twice(row <= col)
        rowg = lax.broadcasted_iota(I32, (BLOCK, SB_KG), 0)
        colg = lax.broadcasted_iota(I32, (BLOCK, SB_KG), 1)

        piece = lambda x, j: x[:, j * BLOCK:(j + 1) * BLOCK]
        chains = range(2 * SB_QT)
        nc = len(chains)

        def prefixes(x):
            return [[_split_dot(piece(x[i], j), t_le) for j in range(x[i].shape[1] // BLOCK)] for i in chains]

        def chain(pre, run, total=None):
            out = []
            for pj in pre:
                out.append(run + pj if total is None else total - run - pj)
                run = run + pj[:, BLOCK - 1:BLOCK]
            return jnp.concatenate(out, axis=1), run

        def qloop(qs, phase):
            q0 = pl.multiple_of(qs * (SB_QT * BLOCK), SB_QT * BLOCK)
            tile = lambda ref, i: ref[pl.ds(q0 + (i // 2) * BLOCK, BLOCK), :]
            qh = [tile((qlo_s, qhi_s)[i % 2], i) for i in chains]
            doh = [tile((dlo_s, dhi_s)[i % 2], i) for i in chains]
            tots = [tile(tot_ref, i)[:, (i % 2) * HEAD_DIM:(i % 2) * HEAD_DIM + 1] for i in chains]
            gd = (qs * SB_QT) // nsub

            def logits(gi):
                kg = k_s[pl.ds(pl.multiple_of(gi * SB_KG, SB_KG), SB_KG), :]
                return [_dot_nt(qh[i], kg) for i in chains]

            def group(gi, st, masks, npiece=nsub):
                k0 = pl.multiple_of(gi * SB_KG, SB_KG)
                wide = npiece * BLOCK
                kg, vg = k_s[pl.ds(k0, wide), :], v_s[pl.ds(k0, wide), :]
                cp, cg, dq = list(st[:nc]), list(st[nc:2 * nc]), st[2 * nc:2 * nc + SB_QT]
                z = [zi[:, :wide] for zi in st[2 * nc + SB_QT:]]
                masked = lambda x, i: x if masks is None else jnp.where(masks[i // 2][:, :wide], x, 0.0)
                z_next = logits(jnp.minimum(gi + 1, gd))
                da = [_dot_nt(doh[i], vg) for i in chains]
                sp = [_softplus(z[i]) for i in chains]
                lrem = [masked(-sp[i], i) for i in chains]
                pre = prefixes(lrem)
                e, a, g = [], [], []
                for i in chains:
                    suffix, cp[i] = chain(pre[i], cp[i], tots[i])
                    e.append(z[i] - sp[i])
                    a.append(masked(jnp.exp(e[i] + suffix), i))
                    g.append(a[i] * da[i])
                gpre = prefixes(g)
                dz = []
                for i in chains:
                    ginc, cg[i] = chain(gpre[i], cg[i])
                    dz.append(masked(g[i] - jnp.exp(e[i]) * ginc, i).astype(BF16))
                ab = [a[i].astype(BF16) for i in chains]
                dq = [dq[t] + jnp.where(lo, _dot(dz[2 * t], kg), _dot(dz[2 * t + 1], kg)) for t in range(SB_QT)]
                rows_of = lambda x: jnp.concatenate(x, axis=0)
                dk_ref[pl.ds(k0, wide), :] += _dot_tn(rows_of(dz), rows_of(qh))
                dv_ref[pl.ds(k0, wide), :] += _dot_tn(rows_of(ab), rows_of(doh))
                return (*cp, *cg, *dq, *z_next)

            zc = [jnp.zeros((BLOCK, 1), F32)] * (2 * nc)
            zq = [jnp.zeros((BLOCK, LANES), F32)] * SB_QT
            st = lax.fori_loop(0, gd, lambda gi, st: group(gi, st, None), (*zc, *zq, *logits(0)))
            st = group(gd, st, [(gd * SB_KG + colg) < (q0 + t * BLOCK + rowg) for t in range(SB_QT)],
                       (phase + 1) * SB_QT)
            for t in range(SB_QT):
                dq_ref[pl.ds(q0 + t * BLOCK, BLOCK), :] = st[2 * nc + t] * scale

        steps_per_group = nsub // SB_QT

        def per_group(g, carry):
            for phase in range(steps_per_group):
                qloop(g * steps_per_group + phase, phase)
            return carry

        lax.fori_loop(0, nq // nsub, per_group, 0)
        p = pl.program_id(0)
        _store_slabs((dq_ref, dk_ref, dv_ref), stage, dproj_ref, sems, (qb0 + p, kb0 + p, vb0 + p))

    slab = lambda b0: pl.BlockSpec((s, LANES), lambda p: (0, b0 + p), pipeline_mode=pl.Buffered(1))
    own = pl.BlockSpec((s, LANES), lambda p: (0, p), pipeline_mode=pl.Buffered(1))
    held = pl.BlockSpec(memory_space=pl.ANY)
    return pl.pallas_call(
        body, name=name, grid=(n_slabs,),
        in_specs=[slab(qb0), slab(kb0), slab(vb0), own, own, held],
        out_specs=held, out_shape=jax.ShapeDtypeStruct(dproj.shape, dproj.dtype),
        input_output_aliases={5: 0},
        scratch_shapes=[pltpu.VMEM((s, LANES), BF16) for _ in range(6)]
        + [pltpu.VMEM((s, LANES), F32) for _ in range(3)]
        + [pltpu.VMEM((3, s, LANES), BF16), pltpu.SemaphoreType.DMA((3,))],
        compiler_params=_params(("arbitrary",), VMEM_LIMIT),
    )(proj, proj, proj, dout, tot, dproj)


def _place():
    x, y, c = lax.axis_index("x"), lax.axis_index("y"), lax.axis_index("c")
    return x, y, c


def gather_small(v, *, name):
    m_per, n = v.shape

    def body(x_ref, out_ref, send_sems, recv_sems, local_sem):
        x, y, c = _place()
        me, sibling = (x, y, c), (x, y, 1 - c)
        chips = [(1 - x, y), (x, 1 - y), (1 - x, 1 - y)]

        def rows(px, py, pc):
            return out_ref.at[pl.ds((4 * px + 2 * py + pc) * m_per, m_per), :]

        def copy(k, block, to, src=None):
            return pltpu.make_async_remote_copy(
                src_ref=rows(*block) if src is None else src, dst_ref=rows(*block),
                send_sem=send_sems.at[k], recv_sem=recv_sems.at[k], device_id=to, device_id_type=MESH)

        mine = pltpu.make_async_copy(x_ref, rows(*me), local_sem)
        mine.start()
        first = [copy(0, me, sibling, src=x_ref)]
        first += [copy(1 + j, me, (*chip, c), src=x_ref) for j, chip in enumerate(chips)]
        for cp in first:
            cp.start()
        passed = [copy(4 + j, (*chip, c), sibling) for j, chip in enumerate(chips)]
        for j, chip in enumerate(chips):
            copy(1 + j, (*chip, c), me).wait_recv()
            passed[j].start()
        copy(0, sibling, me).wait_recv()
        for j, chip in enumerate(chips):
            copy(4 + j, (*chip, 1 - c), me).wait_recv()
        for cp in first + passed:
            cp.wait_send()
        mine.wait()

    return pl.pallas_call(
        body, name=name,
        out_shape=jax.ShapeDtypeStruct((N_DEV * m_per, n), v.dtype),
        in_specs=[pl.BlockSpec(memory_space=pltpu.VMEM)],
        out_specs=pl.BlockSpec(memory_space=pltpu.VMEM),
        scratch_shapes=[pltpu.SemaphoreType.DMA((7,)), pltpu.SemaphoreType.DMA((7,)), pltpu.SemaphoreType.DMA],
        compiler_params=_params(None, VMEM_LIMIT),
    )(v)


_HBM = pl.BlockSpec(memory_space=pltpu.HBM)
_SEM = pl.BlockSpec(memory_space=pltpu.SEMAPHORE)
_EFFECT = pltpu.SideEffectType.DATAFLOW_SIDE_EFFECTING


def _peer_copies(src_refs, land_refs, send_sems, recv_sems, per_dest):
    x, y, c = _place()
    me = 4 * x + 2 * y + c
    copies = []
    for src, land, ssem, rsem in zip(src_refs, land_refs, send_sems, recv_sems):
        for k in (1, 2, 4, 3, 5, 6, 7):
            px, py, pc = x ^ (k >> 2 & 1), y ^ (k >> 1 & 1), c ^ (k & 1)
            copies.append(pltpu.make_async_remote_copy(
                src_ref=src.at[4 * px + 2 * py + pc] if per_dest else src, dst_ref=land.at[me],
                send_sem=ssem.at[k - 1], recv_sem=rsem.at[k - 1], device_id=(px, py, pc), device_id_type=MESH))
    return copies


def _own_copies(src_refs, land_refs, send_sems, per_dest):
    x, y, c = _place()
    me = 4 * x + 2 * y + c
    return [pltpu.make_async_copy(src.at[me] if per_dest else src, land.at[me], ssem.at[7])
            for src, land, ssem in zip(src_refs, land_refs, send_sems)]


def exchange_start(srcs, per_dest, *, name):
    n = len(srcs)
    lands = [lax.empty(a.shape if per_dest else (N_DEV,) + a.shape, a.dtype) for a in srcs]

    def body(*refs):
        src_refs, land_refs = refs[:n], refs[n:2 * n]
        send_sems, recv_sems = refs[2 * n:3 * n], refs[3 * n:4 * n]
        token = refs[-1]
        for cp in _peer_copies(src_refs, land_refs, send_sems, recv_sems, per_dest):
            cp.start()
        for cp in _own_copies(src_refs, land_refs, send_sems, per_dest):
            cp.start()
        token[...] = jnp.zeros_like(token)

    hbm = lambda a: pltpu.HBM(a.shape, a.dtype)
    res = pl.pallas_call(
        body, name=name,
        out_shape=(*[pltpu.SemaphoreType.DMA((8,))] * n, *[pltpu.SemaphoreType.DMA((7,))] * n,
                   *[hbm(a) for a in srcs], *[hbm(a) for a in lands], jax.ShapeDtypeStruct((8, LANES), F32)),
        in_specs=[_HBM] * (2 * n),
        out_specs=(*[_SEM] * (2 * n), *[_HBM] * (2 * n), pl.BlockSpec(memory_space=pltpu.VMEM)),
        input_output_aliases={i: 2 * n + i for i in range(2 * n)},
        compiler_params=pltpu.CompilerParams(has_side_effects=_EFFECT),
    )(*[pltpu.with_memory_space_constraint(a, pltpu.HBM) for a in (*srcs, *lands)])
    handles = [(res[a], res[n + a], res[2 * n + a], res[3 * n + a]) for a in range(n)]
    return handles, res[-1]


def exchange_wait(handles, per_dest, after, *, name):
    n = len(handles)

    def body(*refs):
        src_refs, land_refs = refs[:n], refs[n:2 * n]
        send_sems, recv_sems = refs[2 * n:3 * n], refs[3 * n:4 * n]
        for cp in _peer_copies(src_refs, land_refs, send_sems, recv_sems, per_dest):
            cp.wait_send()
            cp.wait_recv()
        for cp in _own_copies(src_refs, land_refs, send_sems, per_dest):
            cp.wait()

    srcs, lands = [h[2] for h in handles], [h[3] for h in handles]
    hbm = lambda a: pltpu.HBM(a.shape, a.dtype)
    res = pl.pallas_call(
        body, name=name,
        out_shape=(*[hbm(a) for a in srcs], *[hbm(a) for a in lands]),
        in_specs=[*[_HBM] * (2 * n), *[_SEM] * (2 * n), pl.BlockSpec(memory_space=pl.ANY)],
        out_specs=tuple([_HBM] * (2 * n)),
        input_output_aliases={i: i for i in range(2 * n)},
        compiler_params=pltpu.CompilerParams(has_side_effects=_EFFECT),
    )(*srcs, *lands, *[h[0] for h in handles], *[h[1] for h in handles], after)
    return res[n:]


def _adamw_math(w, g, m, v):
    m = ADAM_B1 * m + (1.0 - ADAM_B1) * g
    v = ADAM_B2 * v + (1.0 - ADAM_B2) * (g * g)
    m_hat = m / (1.0 - ADAM_B1 ** ADAM_STEP)
    v_hat = v / (1.0 - ADAM_B2 ** ADAM_STEP)
    delta = -ADAM_LR * (m_hat / (jnp.sqrt(v_hat) + ADAM_EPS) + ADAM_WD * w)
    return delta, m, v


def adamw_parts(parts, w, m, v, layer, outs, *, name):
    depth, r, cdim = w.shape
    n_parts = parts.shape[0]
    tr = _pick(r, [t for t in (512, 256, 128, 112, 64, 32, 16) if t * cdim <= 256 * 1024])

    def body(p_ref, w_ref, m_ref, v_ref, g0, d0, nm0, nv0, g_ref, d_ref, nm_ref, nv_ref):
        g = p_ref[0].astype(F32)
        for q in range(1, n_parts):
            g = g + p_ref[q].astype(F32)
        delta, nm, nv = _adamw_math(w_ref[...], g, m_ref[...], v_ref[...])
        g_ref[...], d_ref[...], nm_ref[...], nv_ref[...] = g, delta, nm, nv

    t = pl.BlockSpec((None, tr, cdim), lambda i: (layer, i, 0))
    held = pl.BlockSpec(memory_space=pl.ANY)
    return pl.pallas_call(
        body, name=name, grid=(r // tr,),
        in_specs=[pl.BlockSpec((n_parts, tr, cdim), lambda i: (0, i, 0)), t, t, t, held, held, held, held],
        out_specs=[t, t, t, t],
        out_shape=[jax.ShapeDtypeStruct((depth, r, cdim), F32)] * 4,
        input_output_aliases={4: 0, 5: 1, 6: 2, 7: 3},
        compiler_params=_params(("parallel",), VMEM_LIMIT),
    )(parts, w, m, v, *outs)


def sum_devices(gathered, *, name):
    m_rows = gathered.shape[1]

    def body(ga_ref, g_ref):
        g = ga_ref[0]
        for dev in range(1, N_DEV):
            g = g + ga_ref[dev]
        g_ref[...] = g

    return pl.pallas_call(
        body, name=name, out_shape=jax.ShapeDtypeStruct((m_rows, LANES), F32),
        compiler_params=_params(None, VMEM_LIMIT),
    )(gathered)


def adamw_small(g, w, m, v, *, name):
    m_rows = w.shape[0]

    def body(g_ref, w_ref, m_ref, v_ref, d_ref, nm_ref, nv_ref):
        d_ref[...], nm_ref[...], nv_ref[...] = _adamw_math(w_ref[...], g_ref[...], m_ref[...], v_ref[...])

    return pl.pallas_call(
        body, name=name, out_shape=[jax.ShapeDtypeStruct((m_rows, LANES), F32)] * 3,
        compiler_params=_params(None, VMEM_LIMIT),
    )(g, w, m, v)


def _t5_bucket(dist):
    max_exact = N_BUCKETS // 2
    d = jnp.maximum(dist, 0)
    large = max_exact + (jnp.log(jnp.maximum(d, 1).astype(F32) / max_exact)
                         / math.log(T5_MAX_DIST / max_exact) * (N_BUCKETS - max_exact)).astype(I32)
    large = jnp.minimum(large, N_BUCKETS - 1)
    return jnp.where(d < max_exact, d, large)


def _rel():
    return jnp.arange(BLOCK)[:, None] + BLOCK - jnp.arange(2 * BLOCK)[None, :]


def _band_bias(table, dils, max_dists):
    rel = _rel()
    biases, buckets = [], []
    for d, md in zip(dils, max_dists):
        bk = _t5_bucket(rel * d)
        vis = (rel >= 0) & (rel <= md)
        looked_up = jnp.zeros((table.shape[1],) + rel.shape, F32)
        for b in range(N_BUCKETS):
            looked_up = jnp.where((bk == b)[None], table[b][:, None, None], looked_up)
        with_prev = jnp.where(vis[None], looked_up, NEG_INF)
        first = jnp.arange(2 * BLOCK)[None, None, :] >= BLOCK
        biases.append(jnp.stack([with_prev, jnp.where(first, with_prev, NEG_INF)]))
        buckets.append(bk.astype(I32))
    return jnp.stack(biases), jnp.stack(buckets)


def _pack(pieces, rows):
    flat = jnp.concatenate([p.reshape(-1) for p in pieces])
    return jnp.pad(flat, (0, rows * LANES - flat.shape[0])).reshape(rows, LANES)


def _unpack(packed, shapes):
    flat = packed.reshape(-1)
    out, off = [], 0
    for sh in shapes:
        n = math.prod(sh)
        out.append(flat[off:off + n].reshape(sh))
        off += n
    return out


def _tile2(g):
    return jnp.concatenate([g, g])


def kernel(x, attn_norm, w_in, a_q_gain, a_k_gain, a_sinks, c_q_gain, c_k_gain, rel_bias_table, mix_out_gain, w_out, ffn_norm, w_up, conv_w, conv_b, w_down, loss_target, m_attn_norm, m_w_in, m_a_q_gain, m_a_k_gain, m_a_sinks, m_c_q_gain, m_c_k_gain, m_rel_bias_table, m_mix_out_gain, m_w_out, m_ffn_norm, m_w_up, m_conv_w, m_conv_b, m_w_down, v_attn_norm, v_w_in, v_a_q_gain, v_a_k_gain, v_a_sinks, v_c_q_gain, v_c_k_gain, v_rel_bias_table, v_mix_out_gain, v_w_out, v_ffn_norm, v_w_up, v_conv_w, v_conv_b, v_w_down):
    depth, d_model, in_shard = w_in.shape
    ff2_shard = w_up.shape[2]
    s = x.shape[1]
    in_width, ff2 = N_DEV * in_shard, N_DEV * ff2_shard
    n_heads = d_model // HEAD_DIM
    ha, hb, hc = n_heads // 4, n_heads // 4, n_heads // 2
    sa, sb, sc = ha // 2, hb // 2, hc // 2
    kv_a = ha // 4
    assert kv_a == 2 and BLOCK == LANES
    cb_aq, cb_ak, cb_av = 0, sa, sa + 1
    cb_bq = sa + 2
    cb_bk, cb_bv = cb_bq + sb, cb_bq + 2 * sb
    cb_cq = cb_bq + 3 * sb
    cb_ck, cb_cv = cb_cq + sc, cb_cq + 2 * sc
    assert (cb_cv + sc) * LANES == in_width
    dev = 4 * lax.axis_index("x") + 2 * lax.axis_index("y") + lax.axis_index("c")

    per_array = 3
    wnames = ("w_in", "w_out", "w_up", "w_down", "conv_w")
    sent = dict(w_in=lambda w: w.T, w_up=lambda w: w.T, w_out=lambda w: w, w_down=lambda w: w, conv_w=lambda w: w)
    rows = lambda g: g.reshape(N_DEV * g.shape[1], g.shape[2])
    whole = dict(w_in=rows, w_up=rows, w_out=rows, w_down=rows,
                 conv_w=lambda g: jnp.transpose(g, (1, 0, 2)).reshape(g.shape[1], N_DEV * g.shape[2]))
    gathers = {}
    token = jnp.zeros((8, LANES), F32)
    for l in range(depth):
        for gi, group in enumerate([[n] for n in wnames] if l < per_array else [wnames]):
            srcs = [(sent[n](dict(w_in=w_in, w_out=w_out, w_up=w_up, w_down=w_down, conv_w=conv_w)[n][l])
                     + token[0, 0]).astype(F32 if n == "conv_w" else BF16) for n in group]
            handles, token = exchange_start(srcs, False, name=f"gather_start_{l}_{gi}")
            gathers.update({(l, n): h for n, h in zip(group, handles)})

    def gathered(l, names, after):
        landed = exchange_wait([gathers[l, n] for n in names], False, after,
                               name=f"gather_wait_{l}_{wnames.index(names[0])}")
        return {n: whole[n](g) for n, g in zip(names, landed)}

    bias_a, buckets_a = _band_bias(rel_bias_table[:, :ha], (1,), (WINDOW_A - 1,))
    bias_c, buckets_c = _band_bias(rel_bias_table[:, ha:], DILATIONS, (BLOCK,) * len(DILATIONS))

    xs = x[0]
    saved = []
    wi, wo, wu, wd, cw = ([None] * depth for _ in range(5))
    for l in range(depth):
        if l < per_array:
            need = lambda n, after, l=l: gathered(l, (n,), after)[n]
        else:
            layer_w = gathered(l, wnames, xs)
            need = lambda n, after: layer_w[n]
        wi[l] = need("w_in", token if l == 0 else xs)
        h1 = rmsnorm_fwd(xs, attn_norm[l], name="attn_norm_fwd")
        proj = matmul(h1, wi[l], trans_b=True, name="in_proj")
        sinks = jnp.repeat(a_sinks[l], HEAD_DIM).reshape(sa, 1, LANES)
        gaq, gak = _tile2(a_q_gain[l]), _tile2(a_k_gain[l])
        gcq, gck = _tile2(c_q_gain[l]), _tile2(c_k_gain[l])
        out_a, lse_a = banded_fwd(proj, cb_aq, cb_ak, cb_av, sa, gaq, gak, bias_a, (1,), sinks, True, name="swa_fwd")
        out_b, tot_b = sb_fwd(proj, cb_bq, cb_bk, cb_bv, sb, name="stick_fwd")
        out_c, lse_c = banded_fwd(proj, cb_cq, cb_ck, cb_cv, sc, gcq, gck, bias_c, DILATIONS, None, False,
                                  name="dilated_fwd")
        mix = mixnorm_fwd([out_a, out_b, out_c], mix_out_gain[l], name="mix_norm_fwd")
        wo[l] = need("w_out", mix)
        x_mid = matmul(mix, wo[l], res=xs, name="out_proj")
        h2 = rmsnorm_fwd(x_mid, ffn_norm[l], name="ffn_norm_fwd")
        wu[l] = need("w_up", h2)
        p = matmul(h2, wu[l], trans_b=True, name="up_proj")
        cw[l] = need("conv_w", p)
        act = ffn_act_fwd(p, cw[l], conv_b[l], name="ffn_act_fwd")
        wd[l] = need("w_down", act)
        x_out = matmul(act, wd[l], res=x_mid, name="down_proj")
        saved.append(dict(x_in=xs, h1=h1, proj=proj, out_a=out_a, lse_a=lse_a, out_b=out_b, tot_b=tot_b,
                          out_c=out_c, lse_c=lse_c, mix=mix, x_mid=x_mid, h2=h2, p=p, act=act,
                          sinks=sinks, gains=(gaq, gak, gcq, gck)))
        xs = x_out

    dx, dx_b, loss_part = loss_head(xs, loss_target[0], name="loss_head")

    small = {k: [None] * depth for k in ("attn_norm", "a_q_gain", "a_k_gain", "a_sinks", "c_q_gain", "c_k_gain",
                                         "mix_out_gain", "ffn_norm", "conv_w", "conv_b")}
    big = {k: [None] * depth for k in ("w_in", "w_out", "w_up", "w_down")}
    dbias_a = dbias_c = None
    scatters = {}
    token = jnp.zeros((8, LANES), F32)
    names_big = ("w_in", "w_out", "w_up", "w_down")

    def scatter(l, names):
        parts = [big[n][l] for n in names]
        handles, tok = exchange_start(parts, True, name=f"scatter_start_{l}_{names_big.index(names[0])}")
        scatters.update({(l, n): h for n, h in zip(names, handles)})
        return tok

    by_rows = lambda a: a.reshape(N_DEV, a.shape[0] // N_DEV, a.shape[1])
    for l in reversed(range(depth)):
        each = l == 0
        sv = saved[l]
        gaq, gak, gcq, gck = sv["gains"]
        da = matmul(dx_b, wd[l], trans_b=True, name="down_proj_dx")
        big["w_down"][l] = by_rows(matmul(sv["act"], dx_b, trans_a=True, out_dtype=BF16, name="down_proj_dw"))
        if each:
            token = scatter(l, ("w_down",))
        dp, small["conv_w"][l], small["conv_b"][l] = ffn_act_bwd(da, sv["p"], cw[l], conv_b[l] + token[0, 0],
                                                                 name="ffn_act_bwd")
        dh2 = matmul(dp, wu[l], name="up_proj_dx")
        big["w_up"][l] = matmul(sv["h2"], dp, trans_a=True, out_dtype=BF16, col_blocks=N_DEV, name="up_proj_dw")
        if each:
            token = scatter(l, ("w_up",))
        dx_mid, dx_mid_b, small["ffn_norm"][l] = rmsnorm_bwd(dh2, sv["x_mid"], ffn_norm[l] + token[0, 0], dx,
                                                   name="ffn_norm_bwd")
        dmix = matmul(dx_mid_b, wo[l], trans_b=True, name="out_proj_dx")
        big["w_out"][l] = by_rows(matmul(sv["mix"], dx_mid_b, trans_a=True, out_dtype=BF16, name="out_proj_dw"))
        if each:
            token = scatter(l, ("w_out",))
        (d_oa, d_ob, d_oc), small["mix_out_gain"][l] = mixnorm_bwd(
            dmix, [sv["out_a"], sv["out_b"], sv["out_c"]], mix_out_gain[l] + token[0, 0], name="mix_norm_bwd")
        dproj = lax.empty((s, in_width), BF16)
        dproj, db_a, dgq_a, dgk_a, dsink = banded_bwd(
            sv["proj"], cb_aq, cb_ak, cb_av, sa, gaq, gak, bias_a, (1,), sv["sinks"], True,
            d_oa, sv["out_a"], sv["lse_a"], dproj, name="swa_bwd")
        dproj = sb_bwd(sv["proj"], cb_bq, cb_bk, cb_bv, sb, d_ob, sv["tot_b"], dproj, name="stick_bwd")
        dproj, db_c, dgq_c, dgk_c = banded_bwd(
            sv["proj"], cb_cq, cb_ck, cb_cv, sc, gcq, gck, bias_c, DILATIONS, None, False,
            d_oc, sv["out_c"], sv["lse_c"], dproj, name="dilated_bwd")
        fold = lambda g: g.reshape(-1, HEAD_DIM).sum(axis=0)
        small["a_q_gain"][l], small["a_k_gain"][l] = fold(dgq_a), fold(dgk_a)
        small["c_q_gain"][l], small["c_k_gain"][l] = fold(dgq_c), fold(dgk_c)
        small["a_sinks"][l] = dsink[:, ::HEAD_DIM].reshape(-1)
        dbias_a = db_a if dbias_a is None else dbias_a + db_a
        dbias_c = db_c if dbias_c is None else dbias_c + db_c
        big["w_in"][l] = by_rows(matmul(dproj, sv["h1"], trans_a=True, out_dtype=BF16, name="in_proj_dw"))
        token = scatter(l, ("w_in",) if each else names_big)
        dh1 = matmul(dproj, wi[l], name="in_proj_dx")
        dx, dx_b, small["attn_norm"][l] = rmsnorm_bwd(dh1, sv["x_in"], attn_norm[l] + token[0, 0], dx_mid,
                                                name="attn_norm_bwd")

    flip = lambda t: jnp.swapaxes(t, 1, 2)
    w_big = dict(w_in=(flip(w_in), flip(m_w_in), flip(v_w_in)), w_out=(w_out, m_w_out, v_w_out),
                 w_up=(w_up, m_w_up, v_w_up), w_down=(w_down, m_w_down, v_w_down))
    results = {k: [lax.empty(w_big[k][0].shape, F32) for _ in range(4)] for k in names_big}

    def update(l, names, after):
        landed = exchange_wait([scatters[l, n] for n in names], True, after,
                               name=f"scatter_wait_{l}_{names_big.index(names[0])}")
        for k, parts in zip(names, landed):
            results[k] = adamw_parts(parts, *w_big[k], l, results[k], name="adamw_large")
        return results[names[-1]][0]

    after = dx_b
    for l in reversed(range(1, depth)):
        after = update(l, names_big, after)

    dtable = jnp.concatenate([bias_bwd(dbias_a, buckets_a, name="swa_bias_bwd"),
                              bias_bwd(dbias_c, buckets_c, name="dilated_bias_bwd")], axis=1)

    order = ("attn_norm", "a_q_gain", "a_k_gain", "a_sinks", "c_q_gain", "c_k_gain", "rel_bias_table",
             "mix_out_gain", "ffn_norm", "conv_w", "conv_b")
    partial = {k: jnp.stack(v) for k, v in small.items()}
    partial["rel_bias_table"] = dtable
    pieces = [partial[k] for k in order] + [loss_part.reshape(1)]
    n_small = sum(math.prod(pc.shape) for pc in pieces)
    rows = -(-n_small // (8 * LANES)) * 8
    packed, after = lax.optimization_barrier((_pack(pieces, rows), after))
    gathered = gather_small(packed, name="gather_small_grads")
    summed = _unpack(sum_devices(gathered.reshape(N_DEV, rows, LANES), name="sum_small_grads"),
                     [pc.shape for pc in pieces])
    g_small = dict(zip(order, summed[:-1]))
    loss = summed[-1][0]
    g_small["conv_w"] = lax.dynamic_slice_in_dim(g_small["conv_w"], dev * ff2_shard, ff2_shard, axis=2)

    w_small = dict(attn_norm=attn_norm, a_q_gain=a_q_gain, a_k_gain=a_k_gain, a_sinks=a_sinks, c_q_gain=c_q_gain,
                   c_k_gain=c_k_gain, rel_bias_table=rel_bias_table, mix_out_gain=mix_out_gain, ffn_norm=ffn_norm,
                   conv_w=conv_w, conv_b=conv_b)
    m_small = dict(attn_norm=m_attn_norm, a_q_gain=m_a_q_gain, a_k_gain=m_a_k_gain, a_sinks=m_a_sinks,
                   c_q_gain=m_c_q_gain, c_k_gain=m_c_k_gain, rel_bias_table=m_rel_bias_table,
                   mix_out_gain=m_mix_out_gain, ffn_norm=m_ffn_norm, conv_w=m_conv_w, conv_b=m_conv_b)
    v_small = dict(attn_norm=v_attn_norm, a_q_gain=v_a_q_gain, a_k_gain=v_a_k_gain, a_sinks=v_a_sinks,
                   c_q_gain=v_c_q_gain, c_k_gain=v_c_k_gain, rel_bias_table=v_rel_bias_table,
                   mix_out_gain=v_mix_out_gain, ffn_norm=v_ffn_norm, conv_w=v_conv_w, conv_b=v_conv_b)
    shapes = [w_small[k].shape for k in order]
    n_upd = sum(math.prod(sh) for sh in shapes)
    urows = -(-n_upd // (8 * LANES)) * 8
    packs = [_pack([d[k] for k in order], urows) for d in (g_small, w_small, m_small, v_small)]
    upd = adamw_small(*packs, name="adamw_small")
    delta_s, newm_s, newv_s = [dict(zip(order, _unpack(u, shapes))) for u in upd]

    after = update(0, names_big[1:], upd[0])
    update(0, names_big[:1], after)
    results["w_in"] = [flip(t) for t in results["w_in"]]
    g_big, delta_b, newm_b, newv_b = [{k: results[k][i] for k in names_big} for i in range(4)]

    all_names = ("attn_norm", "w_in", "a_q_gain", "a_k_gain", "a_sinks", "c_q_gain", "c_k_gain", "rel_bias_table",
                 "mix_out_gain", "w_out", "ffn_norm", "w_up", "conv_w", "conv_b", "w_down")
    pick = lambda sm, bg: [bg[k] if k in bg else sm[k] for k in all_names]
    return (loss, dx[None], *pick(g_small, g_big), *pick(delta_s, delta_b), *pick(newm_s, newm_b),
            *pick(newv_s, newv_b))
```

```python
import functools
import math

import jax
import jax.numpy as jnp
from jax import lax
from jax.experimental import pallas as pl
from jax.experimental.pallas import tpu as pltpu

F32, BF16, I32 = jnp.float32, jnp.bfloat16, jnp.int32
MESH = pl.DeviceIdType.MESH

HEAD_DIM = 64
LANES = 128
BLOCK = 128
EPS = 1e-6
NEG_INF = -1e30
N_BUCKETS = 32
T5_MAX_DIST = 2048
WINDOW_A = 128
DILATIONS = (1, 4, 16)
N_DEV = 8
VMEM_LIMIT = 56 * 1024 * 1024
MATMUL_VMEM = 46 * 1024 * 1024

ADAM_LR, ADAM_B1, ADAM_B2, ADAM_EPS, ADAM_WD, ADAM_STEP = 0.001, 0.9, 0.999, 1e-08, 0.01, 10


def _params(sem=None, vmem=None):
    return pltpu.CompilerParams(dimension_semantics=sem, vmem_limit_bytes=vmem)


def _pick(n, cands):
    for c in cands:
        if n % c == 0:
            return c
    raise ValueError(f"no tile for {n}")


def _dot(a, b):
    return lax.dot_general(a, b, (((1,), (0,)), ((), ())), preferred_element_type=F32)


def _dot_nt(a, b):
    return lax.dot_general(a, b, (((1,), (1,)), ((), ())), preferred_element_type=F32)


def _dot_tn(a, b):
    return lax.dot_general(a, b, (((0,), (0,)), ((), ())), preferred_element_type=F32)


def matmul(a, b, *, trans_a=False, trans_b=False, out_dtype=F32, res=None, col_blocks=None, name):
    a_halves, b_halves = a.ndim == 3, b.ndim == 3
    assert not (a_halves and trans_a) and not (b_halves and trans_b)
    m, k = (a.shape[1], 2 * a.shape[2]) if a_halves else (a.shape[1], a.shape[0]) if trans_a else a.shape
    n = 2 * b.shape[2] if b_halves else b.shape[0] if trans_b else b.shape[1]
    k_unit, n_unit = (k // 2 if a_halves else k), (n // 2 if b_halves else n)
    tm_cands = tuple(t for t in (1408, 1024, 896, 512, 256) if m % t == 0)
    tn_cands = ((n // col_blocks,) if col_blocks
                else tuple(t for t in (1024, 1408, 768, 512, 256, 128) if n_unit % t == 0))

    def footprint(tm, tk, tn):
        tiles = 2 * (tm * tk * a.dtype.itemsize + tk * tn * b.dtype.itemsize)
        return tiles + tm * tn * (4 + 2 * jnp.dtype(out_dtype).itemsize + (8 if res is not None else 0))

    tk, tm, tn = next((tk, tm, tn) for tk in (5376, 4096, 2816, 2048, 1792, 1024, 768, 512, 256) if k_unit % tk == 0
                      for tm in tm_cands if 2 * tm >= tm_cands[0]
                      for tn in tn_cands if footprint(tm, tk, tn) <= MATMUL_VMEM)
    nk = k // tk
    nk_half, nj_half = k_unit // tk, n_unit // tn
    dn = (((0 if trans_a else 1,), (1 if trans_b else 0,)), ((), ()))

    def body(*refs):
        if res is None:
            a_ref, b_ref, o_ref, acc = refs
        else:
            a_ref, b_ref, r_ref, o_ref, acc = refs
        kk = pl.program_id(2)

        @pl.when(kk == 0)
        def _():
            acc[...] = jnp.zeros_like(acc)

        acc[...] += lax.dot_general(a_ref[...].astype(BF16), b_ref[...].astype(BF16), dn,
                                    preferred_element_type=F32)

        @pl.when(kk == nk - 1)
        def _():
            r = acc[...]
            if res is not None:
                r = r_ref[...] + r
            o_ref[...] = r.astype(out_dtype)

    b_spec = (pl.BlockSpec((tn, tk), lambda i, j, kk: (j, kk)) if trans_b
              else pl.BlockSpec((None, tk, tn), lambda i, j, kk: (j // nj_half, kk, j % nj_half)) if b_halves
              else pl.BlockSpec((tk, tn), lambda i, j, kk: (kk, j)))
    a_spec = (pl.BlockSpec((tk, tm), lambda i, j, kk: (kk, i)) if trans_a
              else pl.BlockSpec((None, tm, tk), lambda i, j, kk: (kk // nk_half, i, kk % nk_half)) if a_halves
              else pl.BlockSpec((tm, tk), lambda i, j, kk: (i, kk)))
    in_specs = [a_spec, b_spec]
    args = [a, b]
    if res is not None:
        in_specs.append(pl.BlockSpec((tm, tn), lambda i, j, kk: (i, j)))
        args.append(res)
    if col_blocks:
        out_spec = pl.BlockSpec((None, tm, tn), lambda i, j, kk: (j, i, 0))
        out_shape = jax.ShapeDtypeStruct((col_blocks, m, tn), out_dtype)
    else:
        out_spec = pl.BlockSpec((tm, tn), lambda i, j, kk: (i, j))
        out_shape = jax.ShapeDtypeStruct((m, n), out_dtype)
    return pl.pallas_call(
        body, name=name, grid=(m // tm, n // tn, nk),
        in_specs=in_specs, out_specs=out_spec, out_shape=out_shape,
        scratch_shapes=[pltpu.VMEM((tm, tn), F32)],
        compiler_params=_params(("parallel", "parallel", "arbitrary"), VMEM_LIMIT),
    )(*args)


def rmsnorm_fwd(x, g, *, name):
    s, d = x.shape
    tm = 512

    def body(x_ref, g_ref, o_ref):
        xv = x_ref[...]
        r = lax.rsqrt(jnp.mean(xv * xv, axis=-1, keepdims=True) + EPS)
        o_ref[...] = (xv * r * g_ref[...]).astype(BF16)

    return pl.pallas_call(
        body, name=name, grid=(s // tm,),
        in_specs=[pl.BlockSpec((tm, d), lambda i: (i, 0)), pl.BlockSpec((1, d), lambda i: (0, 0))],
        out_specs=pl.BlockSpec((tm, d), lambda i: (i, 0)),
        out_shape=jax.ShapeDtypeStruct((s, d), BF16),
        compiler_params=_params(("parallel",)),
    )(x, g.reshape(1, d))


def rmsnorm_bwd(dh, x, g, dres, *, name):
    s, d = x.shape
    tm = 512

    def body(dh_ref, x_ref, g_ref, dres_ref, dx_ref, dxb_ref, dg_ref):
        @pl.when(pl.program_id(0) == 0)
        def _():
            dg_ref[...] = jnp.zeros_like(dg_ref)

        xv, dhv = x_ref[...], dh_ref[...]
        r = lax.rsqrt(jnp.mean(xv * xv, axis=-1, keepdims=True) + EPS)
        gd = dhv * g_ref[...]
        dot = jnp.mean(gd * xv, axis=-1, keepdims=True)
        dx = dres_ref[...] + (r * gd - xv * (r * r * r * dot))
        dx_ref[...] = dx
        dxb_ref[...] = dx.astype(BF16)
        dg_ref[...] += jnp.sum(dhv * (xv * r), axis=0, keepdims=True)

    row = pl.BlockSpec((tm, d), lambda i: (i, 0))
    dx, dxb, dg = pl.pallas_call(
        body, name=name, grid=(s // tm,),
        in_specs=[row, row, pl.BlockSpec((1, d), lambda i: (0, 0)), row],
        out_specs=[row, row, pl.BlockSpec((1, d), lambda i: (0, 0))],
        out_shape=[jax.ShapeDtypeStruct((s, d), F32), jax.ShapeDtypeStruct((s, d), BF16),
                   jax.ShapeDtypeStruct((1, d), F32)],
        compiler_params=_params(("arbitrary",), VMEM_LIMIT),
    )(dh, x, g.reshape(1, d), dres)
    return dx, dxb, dg[0]


def loss_head(y, target, *, name):
    s, d = y.shape
    tm = 512

    def body(y_ref, t_ref, dy_ref, dyb_ref, l_ref):
        @pl.when(pl.program_id(0) == 0)
        def _():
            l_ref[...] = jnp.zeros_like(l_ref)

        e = y_ref[...] - t_ref[...]
        dy = e / float(d)
        dy_ref[...] = dy
        dyb_ref[...] = dy.astype(BF16)
        per_tok = jnp.mean(e * e, axis=-1, keepdims=True)
        l_ref[...] += 0.5 * jnp.sum(per_tok, axis=0, keepdims=True)

    row = pl.BlockSpec((tm, d), lambda i: (i, 0))
    dy, dyb, l = pl.pallas_call(
        body, name=name, grid=(s // tm,),
        in_specs=[row, row],
        out_specs=[row, row, pl.BlockSpec((8, LANES), lambda i: (0, 0))],
        out_shape=[jax.ShapeDtypeStruct((s, d), F32), jax.ShapeDtypeStruct((s, d), BF16),
                   jax.ShapeDtypeStruct((8, LANES), F32)],
        compiler_params=_params(("arbitrary",)),
    )(y, target)
    return dy, dyb, l[0, 0]


FFN_TN = 256
FFN_CH = 256


def _rows_before(ref, r0, first):
    if first:
        cur = ref[pl.ds(0, FFN_CH), :]
        row = lax.broadcasted_iota(I32, cur.shape, 0)
        sh1 = jnp.where(row < 1, 0.0, pltpu.roll(cur, 1, axis=0))
        sh2 = jnp.where(row < 2, 0.0, pltpu.roll(cur, 2, axis=0))
        return cur, sh1, sh2
    ext = ref[pl.ds(pl.multiple_of(r0 - 8, 8), FFN_CH + 8), :]
    return ext[8:], pltpu.roll(ext, 1, axis=0)[8:], pltpu.roll(ext, 2, axis=0)[8:]


def _rows_after(ref, r0, last):
    if last:
        cur = ref[pl.ds(r0, FFN_CH), :]
        row = lax.broadcasted_iota(I32, cur.shape, 0)
        up1 = jnp.where(row >= FFN_CH - 1, 0.0, pltpu.roll(cur, FFN_CH - 1, axis=0))
        up2 = jnp.where(row >= FFN_CH - 2, 0.0, pltpu.roll(cur, FFN_CH - 2, axis=0))
        return cur, up1, up2
    n = FFN_CH + 8
    ext = ref[pl.ds(r0, n), :]
    return ext[:FFN_CH], pltpu.roll(ext, n - 1, axis=0)[:FFN_CH], pltpu.roll(ext, n - 2, axis=0)[:FFN_CH]


def _sigmoid(x):
    return 0.5 * jnp.tanh(0.5 * x) + 0.5


def ffn_act_fwd(p, conv_w, conv_b, *, name):
    s, f2 = p.shape
    f = f2 // 2
    nj = f // FFN_TN
    nch = s // FFN_CH

    def body(pg_ref, pu_ref, wg_ref, wu_ref, bg_ref, bu_ref, a_ref):
        def conv(ref, w_ref, b_ref, r0, first):
            cur, sh1, sh2 = _rows_before(ref, r0, first)
            return ((b_ref[...] + w_ref[0:1, :] * sh2) + w_ref[1:2, :] * sh1) + w_ref[2:3, :] * cur

        def chunk(r0, first):
            gate = conv(pg_ref, wg_ref, bg_ref, r0, first)
            up = conv(pu_ref, wu_ref, bu_ref, r0, first)
            a_ref[pl.ds(r0, FFN_CH), :] = (gate * _sigmoid(gate) * up).astype(BF16)

        chunk(0, True)

        def step(c, carry):
            chunk(pl.multiple_of(c * FFN_CH, FFN_CH), False)
            return carry

        lax.fori_loop(1, nch, step, 0)

    col = lambda off: pl.BlockSpec((s, FFN_TN), lambda j: (0, j + off))
    wcol = lambda off: pl.BlockSpec((3, FFN_TN), lambda j: (0, j + off))
    bcol = lambda off: pl.BlockSpec((1, FFN_TN), lambda j: (0, j + off))
    return pl.pallas_call(
        body, name=name, grid=(nj,),
        in_specs=[col(0), col(nj), wcol(0), wcol(nj), bcol(0), bcol(nj)],
        out_specs=pl.BlockSpec((s, FFN_TN), lambda j: (0, j)),
        out_shape=jax.ShapeDtypeStruct((s, f), BF16),
        compiler_params=_params(("parallel",), VMEM_LIMIT),
    )(p, p, conv_w, conv_w, conv_b.reshape(1, f2), conv_b.reshape(1, f2))


def ffn_act_bwd(da, p, conv_w, conv_b, *, name):
    s, f2 = p.shape
    f = f2 // 2
    nj = f // FFN_TN
    nch = s // FFN_CH

    def body(da_ref, pg_ref, pu_ref, wg_ref, wu_ref, bg_ref, bu_ref,
             dp_ref, dwg_ref, dwu_ref, dbg_ref, dbu_ref, dug_s, duu_s):
        dpg_ref, dpu_ref = dp_ref.at[0], dp_ref.at[1]
        def conv(ref, w_ref, b_ref, r0, first):
            cur, sh1, sh2 = _rows_before(ref, r0, first)
            u = ((b_ref[...] + w_ref[0:1, :] * sh2) + w_ref[1:2, :] * sh1) + w_ref[2:3, :] * cur
            return u, (sh2, sh1, cur)

        def taps_sum(du, taps):
            return jnp.concatenate([jnp.sum(du * t, axis=0, keepdims=True) for t in taps], axis=0)

        def chunk(r0, first, acc):
            dwg, dwu, dbg, dbu = acc
            gate, tg = conv(pg_ref, wg_ref, bg_ref, r0, first)
            up, tu = conv(pu_ref, wu_ref, bu_ref, r0, first)
            dav = da_ref[pl.ds(r0, FFN_CH), :]
            sg = _sigmoid(gate)
            dgate = dav * up * (sg * (1.0 + gate * (1.0 - sg)))
            dup = dav * (gate * sg)
            dug_s[pl.ds(r0, FFN_CH), :] = dgate
            duu_s[pl.ds(r0, FFN_CH), :] = dup
            return (dwg + taps_sum(dgate, tg), dwu + taps_sum(dup, tu),
                    dbg + jnp.sum(dgate, axis=0, keepdims=True), dbu + jnp.sum(dup, axis=0, keepdims=True))

        z3 = jnp.zeros((3, FFN_TN), F32)
        z1 = jnp.zeros((1, FFN_TN), F32)
        acc = chunk(0, True, (z3, z3, z1, z1))
        acc = lax.fori_loop(1, nch, lambda c, a: chunk(pl.multiple_of(c * FFN_CH, FFN_CH), False, a), acc)
        dwg_ref[...], dwu_ref[...], dbg_ref[...], dbu_ref[...] = acc

        def back(src, w_ref, dst, r0, last):
            cur, up1, up2 = _rows_after(src, r0, last)
            dst[pl.ds(r0, FFN_CH), :] = (w_ref[2:3, :] * cur + w_ref[1:2, :] * up1 + w_ref[0:1, :] * up2).astype(BF16)

        def step(c, carry):
            r0 = pl.multiple_of(c * FFN_CH, FFN_CH)
            back(dug_s, wg_ref, dpg_ref, r0, False)
            back(duu_s, wu_ref, dpu_ref, r0, False)
            return carry

        lax.fori_loop(0, nch - 1, step, 0)
        back(dug_s, wg_ref, dpg_ref, (nch - 1) * FFN_CH, True)
        back(duu_s, wu_ref, dpu_ref, (nch - 1) * FFN_CH, True)

    col = lambda off: pl.BlockSpec((s, FFN_TN), lambda j: (0, j + off))
    wcol = lambda off: pl.BlockSpec((3, FFN_TN), lambda j: (0, j + off))
    bcol = lambda off: pl.BlockSpec((1, FFN_TN), lambda j: (0, j + off))
    outs = pl.pallas_call(
        body, name=name, grid=(nj,),
        in_specs=[col(0), col(0), col(nj), wcol(0), wcol(nj), bcol(0), bcol(nj)],
        out_specs=[pl.BlockSpec((2, s, FFN_TN), lambda j: (0, 0, j)), wcol(0), wcol(0), bcol(0), bcol(0)],
        out_shape=[jax.ShapeDtypeStruct((2, s, f), BF16),
                   jax.ShapeDtypeStruct((3, f), F32), jax.ShapeDtypeStruct((3, f), F32),
                   jax.ShapeDtypeStruct((1, f), F32), jax.ShapeDtypeStruct((1, f), F32)],
        scratch_shapes=[pltpu.VMEM((s, FFN_TN), F32), pltpu.VMEM((s, FFN_TN), F32)],
        compiler_params=_params(("parallel",), VMEM_LIMIT),
    )(da, p, p, conv_w, conv_w, conv_b.reshape(1, f2), conv_b.reshape(1, f2))
    dp, dwg, dwu, dbg, dbu = outs
    return dp, jnp.concatenate([dwg, dwu], axis=1), jnp.concatenate([dbg, dbu], axis=1)[0]


def mixnorm_fwd(outs, gain, *, name):
    s = outs[0].shape[0]
    widths = [o.shape[1] for o in outs]
    total = sum(widths)
    tm = 512

    def body(*refs):
        o_refs, g_ref, m_ref = refs[:-2], refs[-2], refs[-1]
        off = 0
        for o_ref, w in zip(o_refs, widths):
            xv = o_ref[...]
            r = lax.rsqrt(jnp.mean(xv * xv, axis=-1, keepdims=True) + EPS)
            m_ref[:, off:off + w] = (xv * r * g_ref[:, off:off + w]).astype(BF16)
            off += w

    return pl.pallas_call(
        body, name=name, grid=(s // tm,),
        in_specs=[pl.BlockSpec((tm, w), lambda i: (i, 0)) for w in widths] + [pl.BlockSpec((1, total), lambda i: (0, 0))],
        out_specs=pl.BlockSpec((tm, total), lambda i: (i, 0)),
        out_shape=jax.ShapeDtypeStruct((s, total), BF16),
        compiler_params=_params(("parallel",)),
    )(*outs, gain.reshape(1, total))


def mixnorm_bwd(dmix, outs, gain, *, name):
    s = outs[0].shape[0]
    widths = [o.shape[1] for o in outs]
    total = sum(widths)
    n = len(outs)
    tm = 512

    def body(*refs):
        dm_ref, o_refs, g_ref = refs[0], refs[1:1 + n], refs[1 + n]
        d_refs, dg_ref = refs[2 + n:2 + 2 * n], refs[2 + 2 * n]

        @pl.when(pl.program_id(0) == 0)
        def _():
            dg_ref[...] = jnp.zeros_like(dg_ref)

        off = 0
        for o_ref, d_ref, w in zip(o_refs, d_refs, widths):
            xv = o_ref[...]
            dhv = dm_ref[:, off:off + w]
            r = lax.rsqrt(jnp.mean(xv * xv, axis=-1, keepdims=True) + EPS)
            gd = dhv * g_ref[:, off:off + w]
            dot = jnp.mean(gd * xv, axis=-1, keepdims=True)
            d_ref[...] = r * gd - xv * (r * r * r * dot)
            dg_ref[:, off:off + w] += jnp.sum(dhv * (xv * r), axis=0, keepdims=True)
            off += w

    res = pl.pallas_call(
        body, name=name, grid=(s // tm,),
        in_specs=[pl.BlockSpec((tm, total), lambda i: (i, 0))]
        + [pl.BlockSpec((tm, w), lambda i: (i, 0)) for w in widths] + [pl.BlockSpec((1, total), lambda i: (0, 0))],
        out_specs=[pl.BlockSpec((tm, w), lambda i: (i, 0)) for w in widths] + [pl.BlockSpec((1, total), lambda i: (0, 0))],
        out_shape=[jax.ShapeDtypeStruct((s, w), F32) for w in widths] + [jax.ShapeDtypeStruct((1, total), F32)],
        compiler_params=_params(("arbitrary",)),
    )(dmix, *outs, gain.reshape(1, total))
    return res[:n], res[n][0]


NORM_CH = 512
FWD_TILES = 4
BWD_TILES = 4


def _lo_mask(shape):
    return lax.broadcasted_iota(I32, shape, 1) < HEAD_DIM


def _head_sum(x, lo):
    del lo
    i = lax.broadcasted_iota(I32, (LANES, LANES), 0) // HEAD_DIM
    j = lax.broadcasted_iota(I32, (LANES, LANES), 1) // HEAD_DIM
    return _split_dot(x, _twice(i == j))


def _head_stats(x, lo):
    return lax.rsqrt(_head_sum(x * x, lo) * (1.0 / HEAD_DIM) + EPS)


def _swap_halves(x):
    return pltpu.roll(x, HEAD_DIM, axis=1)


def _replicate_head(x, lo, use_lo_head):
    sw = _swap_halves(x)
    return jnp.where(use_lo_head, jnp.where(lo, x, sw), jnp.where(lo, sw, x))


def _tile_rows(i, s, d):
    nb = s // (BLOCK * d)
    r = i // nb
    b = i % nb
    start = r + (BLOCK * d) * b
    prev = start - (BLOCK * d) * jnp.minimum(b, 1)
    return start, prev, b > 0


def _rows(ref, start, d):
    if d == 1:
        return ref[pl.ds(pl.multiple_of(start, BLOCK), BLOCK), :]
    return ref[pl.ds(start, BLOCK, stride=d), :]


def _set_rows(ref, start, d, val):
    if d == 1:
        ref[pl.ds(pl.multiple_of(start, BLOCK), BLOCK), :] = val
    else:
        ref[pl.ds(start, BLOCK, stride=d), :] = val


def banded_fwd(proj, qb0, kb0, vb0, n_slabs, gq, gk, bias, dils, sinks, gqa, *, name):
    s = proj.shape[0]
    nbr = len(dils)
    nt = s // BLOCK
    nch = s // NORM_CH
    has_sink = sinks is not None

    def body(*refs):
        q_ref, k_ref, v_ref, gq_ref, gk_ref, b_ref = refs[:6]
        rest = refs[6:]
        if has_sink:
            sink_ref, rest = rest[0], rest[1:]
        out_ref, lse_ref, qn_s, kn_s, vv_s, o_s, l_s = rest
        p = pl.program_id(0)
        use_lo = (p // 2) == 0

        def prep(c, carry):
            rows = pl.ds(pl.multiple_of(c * NORM_CH, NORM_CH), NORM_CH)
            lo = _lo_mask((NORM_CH, LANES))
            qv, kv, vv = q_ref[rows, :], k_ref[rows, :], v_ref[rows, :]
            qn_s[rows, :] = qv * _head_stats(qv, lo) * gq_ref[...] * (HEAD_DIM ** -0.5)
            kn = kv * _head_stats(kv, lo) * gk_ref[...]
            if gqa:
                kn = _replicate_head(kn, lo, use_lo)
                vv = _replicate_head(vv, lo, use_lo)
            kn_s[rows, :] = kn
            vv_s[rows, :] = vv
            return carry

        lax.fori_loop(0, nch, prep, 0)

        lo = _lo_mask((BLOCK, LANES))
        hms = [lo, jnp.logical_not(lo)]
        heads, tiles = range(2), range(FWD_TILES)
        for br, d in enumerate(dils):
            def step(ii, carry, br=br, d=d):
                pos = [_tile_rows(ii * FWD_TILES + u, s, d) for u in tiles]
                kc = [carry[0]] + [_rows(kn_s, pos[u][0], d).astype(BF16) for u in tiles]
                vc = [carry[1]] + [_rows(vv_s, pos[u][0], d).astype(BF16) for u in tiles]
                kcat = [jnp.concatenate([kc[u], kc[u + 1]], axis=0) for u in tiles]
                vcat = [jnp.concatenate([vc[u], vc[u + 1]], axis=0) for u in tiles]
                qt = [_rows(qn_s, pos[u][0], d) for u in tiles]
                sc = [[_dot_nt(jnp.where(hms[h], qt[u], 0.0).astype(BF16), kcat[u])
                       + b_ref[br, jnp.where(pos[u][2], 0, 1), h] for h in heads] for u in tiles]
                m = [[jnp.max(sc[u][h], axis=1, keepdims=True) for h in heads] for u in tiles]
                pe = [[jnp.exp(sc[u][h] - m[u][h]) for h in heads] for u in tiles]
                den = [[jnp.sum(pe[u][h], axis=1, keepdims=True) for h in heads] for u in tiles]
                o = [[_dot(pe[u][h].astype(BF16), vcat[u]) * (1.0 / den[u][h]) for h in heads] for u in tiles]
                for u in tiles:
                    _set_rows(o_s.at[br], pos[u][0], d, jnp.where(lo, o[u][0], o[u][1]))
                    _set_rows(l_s.at[br], pos[u][0], d,
                              jnp.where(lo, m[u][0] + jnp.log(den[u][0]), m[u][1] + jnp.log(den[u][1])))
                return kc[-1], vc[-1]

            none_yet = jnp.zeros((BLOCK, LANES), BF16)
            lax.fori_loop(0, nt // FWD_TILES, step, (none_yet, none_yet))

        def combine(c, carry):
            rows = pl.ds(pl.multiple_of(c * NORM_CH, NORM_CH), NORM_CH)
            ls = [l_s[br, rows, :] for br in range(nbr)]
            mx = functools.reduce(jnp.maximum, ls)
            if has_sink:
                mx = jnp.maximum(mx, sink_ref[...])
            es = [jnp.exp(l - mx) for l in ls]
            tot = functools.reduce(jnp.add, es)
            if has_sink:
                tot = tot + jnp.exp(sink_ref[...] - mx)
            acc = es[0] * o_s[0, rows, :]
            for br in range(1, nbr):
                acc = acc + es[br] * o_s[br, rows, :]
            out_ref[rows, :] = acc * (1.0 / tot)
            lse_ref[rows, :] = mx + jnp.log(tot)
            return carry

        lax.fori_loop(0, nch, combine, 0)

    slab = lambda b0, shared: pl.BlockSpec((s, LANES), (lambda p: (0, b0)) if shared else (lambda p: (0, b0 + p)),
                                           pipeline_mode=pl.Buffered(1))
    vec = pl.BlockSpec((1, LANES), lambda p: (0, 0))
    in_specs = [slab(qb0, False), slab(kb0, gqa), slab(vb0, gqa), vec, vec,
                pl.BlockSpec((nbr, 2, 2, BLOCK, 2 * BLOCK), lambda p: (0, 0, p, 0, 0))]
    args = [proj, proj, proj, gq.reshape(1, LANES), gk.reshape(1, LANES), bias]
    if has_sink:
        in_specs.append(pl.BlockSpec((None, 1, LANES), lambda p: (p, 0, 0)))
        args.append(sinks)
    w = LANES * n_slabs
    return pl.pallas_call(
        body, name=name, grid=(n_slabs,),
        in_specs=in_specs,
        out_specs=[pl.BlockSpec((s, LANES), lambda p: (0, p)), pl.BlockSpec((s, LANES), lambda p: (0, p))],
        out_shape=[jax.ShapeDtypeStruct((s, w), F32), jax.ShapeDtypeStruct((s, w), F32)],
        scratch_shapes=[pltpu.VMEM((s, LANES), F32), pltpu.VMEM((s, LANES), F32), pltpu.VMEM((s, LANES), F32),
                        pltpu.VMEM((nbr, s, LANES), F32), pltpu.VMEM((nbr, s, LANES), F32)],
        compiler_params=_params(("parallel",), VMEM_LIMIT),
    )(*args)


def banded_bwd(proj, qb0, kb0, vb0, n_slabs, gq, gk, bias, dils, sinks, gqa, dout, out, lse, dproj, *, name):
    s = proj.shape[0]
    nbr = len(dils)
    nt = s // BLOCK
    nch = s // NORM_CH
    has_sink = sinks is not None
    scale = HEAD_DIM ** -0.5

    def body(*refs):
        q_ref, k_ref, v_ref, gq_ref, gk_ref, b_ref, do_ref, o_ref, lse_ref = refs[:9]
        rest = refs[9:]
        if has_sink:
            sink_ref, rest = rest[0], rest[1:]
        dproj_ref, db_ref, dgq_ref, dgk_ref = rest[1:5]
        rest = rest[5:]
        if has_sink:
            dsink_ref, rest = rest[0], rest[1:]
        qn_s, kn_s, vv_s, dl_s, dqn_s, dkn_s, dvv_s, dq_ref, dk_ref, dv_ref, stage, sems = rest
        p = pl.program_id(0)
        use_lo = (p // 2) == 0

        def prep(c, carry):
            rows = pl.ds(pl.multiple_of(c * NORM_CH, NORM_CH), NORM_CH)
            lo = _lo_mask((NORM_CH, LANES))
            qv, kv, vv = q_ref[rows, :], k_ref[rows, :], v_ref[rows, :]
            qn_s[rows, :] = qv * _head_stats(qv, lo) * gq_ref[...] * scale
            kn = kv * _head_stats(kv, lo) * gk_ref[...]
            if gqa:
                kn = _replicate_head(kn, lo, use_lo)
                vv = _replicate_head(vv, lo, use_lo)
            kn_s[rows, :] = kn
            vv_s[rows, :] = vv
            delta = _head_sum(do_ref[rows, :] * o_ref[rows, :], lo)
            odd = lax.broadcasted_iota(I32, (NORM_CH, LANES), 1) % 2 == 1
            dl_s[rows, :] = jnp.where(odd, delta, lse_ref[rows, :])
            z = jnp.zeros((NORM_CH, LANES), F32)
            dqn_s[rows, :] = z
            dkn_s[rows, :] = z
            dvv_s[rows, :] = z
            if has_sink:
                ps = jnp.exp(sink_ref[...] - lse_ref[rows, :])
                return carry - jnp.sum(ps * delta, axis=0, keepdims=True)
            return carry

        dsink = lax.fori_loop(0, nch, prep, jnp.zeros((1, LANES), F32))
        if has_sink:
            dsink_ref[...] = jnp.broadcast_to(dsink, (8, LANES))

        lo = _lo_mask((BLOCK, LANES))
        hms = [lo, jnp.logical_not(lo)]
        heads, tiles = range(2), range(BWD_TILES)
        for br, d in enumerate(dils):
            db_ref[br] = jnp.zeros((2, BLOCK, 2 * BLOCK), F32)

            def step(ii, carry, br=br, d=d):
                pos = [_tile_rows(ii * BWD_TILES + u, s, d) for u in tiles]
                kc = [carry[0]] + [_rows(kn_s, pos[u][0], d).astype(BF16) for u in tiles]
                vc = [carry[1]] + [_rows(vv_s, pos[u][0], d).astype(BF16) for u in tiles]
                kcat = [jnp.concatenate([kc[u], kc[u + 1]], axis=0) for u in tiles]
                vcat = [jnp.concatenate([vc[u], vc[u + 1]], axis=0) for u in tiles]
                qt = [_rows(qn_s, pos[u][0], d) for u in tiles]
                dot_ = [_rows(do_ref, pos[u][0], d) for u in tiles]
                st_t = [_rows(dl_s, pos[u][0], d) for u in tiles]
                qh = [[jnp.where(hms[h], qt[u], 0.0).astype(BF16) for h in heads] for u in tiles]
                doh = [[jnp.where(hms[h], dot_[u], 0.0).astype(BF16) for h in heads] for u in tiles]
                sc = [[_dot_nt(qh[u][h], kcat[u]) + b_ref[br, jnp.where(pos[u][2], 0, 1), h] for h in heads]
                      for u in tiles]
                dp = [[_dot_nt(doh[u][h], vcat[u]) for h in heads] for u in tiles]
                lane0 = [0, HEAD_DIM]
                pr = [[jnp.exp(sc[u][h] - st_t[u][:, lane0[h]:lane0[h] + 1]) for h in heads] for u in tiles]
                dlog = [[pr[u][h] * (dp[u][h] - st_t[u][:, lane0[h] + 1:lane0[h] + 2]) for h in heads] for u in tiles]
                for h in heads:
                    db_ref[br, h] += functools.reduce(jnp.add, [dlog[u][h] for u in tiles])
                dlb = [[dlog[u][h].astype(BF16) for h in heads] for u in tiles]
                prb = [[pr[u][h].astype(BF16) for h in heads] for u in tiles]
                dq_t = [jnp.where(lo, _dot(dlb[u][0], kcat[u]), _dot(dlb[u][1], kcat[u])) * scale for u in tiles]
                rows2 = lambda x: jnp.concatenate(x, axis=0)
                dk_t = [_dot_tn(rows2(dlb[u]), rows2(qh[u])) for u in tiles]
                dv_t = [_dot_tn(rows2(prb[u]), rows2(doh[u])) for u in tiles]
                for u in tiles:
                    start, prev = pos[u][0], pos[u][1]
                    _set_rows(dqn_s, start, d, _rows(dqn_s, start, d) + dq_t[u])
                    _set_rows(dkn_s, prev, d, _rows(dkn_s, prev, d) + dk_t[u][:BLOCK])
                    _set_rows(dkn_s, start, d, _rows(dkn_s, start, d) + dk_t[u][BLOCK:])
                    _set_rows(dvv_s, prev, d, _rows(dvv_s, prev, d) + dv_t[u][:BLOCK])
                    _set_rows(dvv_s, start, d, _rows(dvv_s, start, d) + dv_t[u][BLOCK:])
                return kc[-1], vc[-1]

            none_yet = jnp.zeros((BLOCK, LANES), BF16)
            lax.fori_loop(0, nt // BWD_TILES, step, (none_yet, none_yet))

        if gqa:
            @pl.when(p == 0)
            def _():
                dk_ref[...] = jnp.zeros_like(dk_ref)
                dv_ref[...] = jnp.zeros_like(dv_ref)

        def finish(c, carry):
            dgq, dgk = carry
            rows = pl.ds(pl.multiple_of(c * NORM_CH, NORM_CH), NORM_CH)
            lo = _lo_mask((NORM_CH, LANES))

            def norm_bwd(xv, dn, g_ref):
                r = _head_stats(xv, lo)
                gd = dn * g_ref[...]
                dot = _head_sum(gd * xv, lo) * (1.0 / HEAD_DIM)
                return r * gd - xv * (r * r * r * dot), dn * (xv * r)

            dq, gq_part = norm_bwd(q_ref[rows, :], dqn_s[rows, :], gq_ref)
            dq_ref[rows, :] = dq
            dgq = dgq + jnp.sum(gq_part, axis=0, keepdims=True)
            kv, dkn, dvv = k_ref[rows, :], dkn_s[rows, :], dvv_s[rows, :]
            if gqa:
                kv = _replicate_head(kv, lo, use_lo)
                dkn = dkn + _swap_halves(dkn)
                dvv = dvv + _swap_halves(dvv)
                lane = lax.broadcasted_iota(I32, (NORM_CH, LANES), 1)
                mine = (lane // HEAD_DIM) == (p // 2)
                dk, gk_part = norm_bwd(kv, dkn, gk_ref)
                dk_ref[rows, :] += jnp.where(mine, dk, 0.0)
                dv_ref[rows, :] += jnp.where(mine, dvv, 0.0)
                gk_part = jnp.where(lo, gk_part, 0.0)
            else:
                dk, gk_part = norm_bwd(kv, dkn, gk_ref)
                dk_ref[rows, :] = dk
                dv_ref[rows, :] = dvv
            dgk = dgk + jnp.sum(gk_part, axis=0, keepdims=True)
            return dgq, dgk

        z = jnp.zeros((1, LANES), F32)
        dgq, dgk = lax.fori_loop(0, nch, finish, (z, z))
        dgq_ref[...] = jnp.broadcast_to(dgq, (8, LANES))
        dgk_ref[...] = jnp.broadcast_to(dgk, (8, LANES))
        if gqa:
            _store_slabs((dq_ref,), stage, dproj_ref, sems, (qb0 + p,))

            @pl.when(p == n_slabs - 1)
            def _():
                _store_slabs((dk_ref, dv_ref), stage, dproj_ref, sems, (kb0, vb0))
        else:
            _store_slabs((dq_ref, dk_ref, dv_ref), stage, dproj_ref, sems, (qb0 + p, kb0 + p, vb0 + p))

    def slab_of(b0, shared):
        return pl.BlockSpec((s, LANES), (lambda p: (0, b0)) if shared else (lambda p: (0, b0 + p)),
                            pipeline_mode=pl.Buffered(1))

    vec = pl.BlockSpec((1, LANES), lambda p: (0, 0))
    own = pl.BlockSpec((s, LANES), lambda p: (0, p), pipeline_mode=pl.Buffered(1))
    in_specs = [slab_of(qb0, False), slab_of(kb0, gqa), slab_of(vb0, gqa), vec, vec,
                pl.BlockSpec((nbr, 2, 2, BLOCK, 2 * BLOCK), lambda p: (0, 0, p, 0, 0)), own, own, own]
    args = [proj, proj, proj, gq.reshape(1, LANES), gk.reshape(1, LANES), bias, dout, out, lse]
    if has_sink:
        in_specs.append(pl.BlockSpec((None, 1, LANES), lambda p: (p, 0, 0)))
        args.append(sinks)
    held = pl.BlockSpec(memory_space=pl.ANY)
    in_specs.append(held)
    args.append(dproj)
    part = pl.BlockSpec((None, 8, LANES), lambda p: (p, 0, 0))
    out_specs = [held, pl.BlockSpec((nbr, 2, BLOCK, 2 * BLOCK), lambda p: (0, p, 0, 0)), part, part]
    out_shape = [jax.ShapeDtypeStruct(dproj.shape, dproj.dtype),
                 jax.ShapeDtypeStruct((nbr, 2 * n_slabs, BLOCK, 2 * BLOCK), F32),
                 jax.ShapeDtypeStruct((n_slabs, 8, LANES), F32), jax.ShapeDtypeStruct((n_slabs, 8, LANES), F32)]
    if has_sink:
        out_specs.append(part)
        out_shape.append(jax.ShapeDtypeStruct((n_slabs, 8, LANES), F32))
    res = pl.pallas_call(
        body, name=name, grid=(n_slabs,),
        in_specs=in_specs, out_specs=out_specs, out_shape=out_shape,
        input_output_aliases={len(args) - 1: 0},
        scratch_shapes=[pltpu.VMEM((s, LANES), F32) for _ in range(10)]
        + [pltpu.VMEM((3, s, LANES), BF16), pltpu.SemaphoreType.DMA((3,))],
        compiler_params=_params(("arbitrary",), VMEM_LIMIT),
    )(*args)
    outs = [res[0], res[1], res[2][:, 0, :], res[3][:, 0, :]]
    if has_sink:
        outs.append(res[4][:, 0, :])
    return outs


def bias_bwd(dbias, buckets, *, name):
    nbr, h = dbias.shape[:2]

    def body(db_ref, bk_ref, o_ref):
        lane = lax.broadcasted_iota(I32, (1, LANES), 1)
        acc = jnp.zeros((1, LANES), F32)
        for b in range(N_BUCKETS):
            tot = jnp.zeros((1, 1), F32)
            for br in range(nbr):
                sel = jnp.where(bk_ref[br] == b, db_ref[br], 0.0)
                tot = tot + jnp.sum(jnp.sum(sel, axis=0, keepdims=True), axis=1, keepdims=True)
            acc = jnp.where(lane == b, tot, acc)
        o_ref[...] = jnp.broadcast_to(acc, (8, LANES))

    res = pl.pallas_call(
        body, name=name, grid=(h,),
        in_specs=[pl.BlockSpec((nbr, None, BLOCK, 2 * BLOCK), lambda i: (0, i, 0, 0)),
                  pl.BlockSpec((nbr, BLOCK, 2 * BLOCK), lambda i: (0, 0, 0))],
        out_specs=pl.BlockSpec((None, 8, LANES), lambda i: (i, 0, 0)),
        out_shape=jax.ShapeDtypeStruct((h, 8, LANES), F32),
        compiler_params=_params(("parallel",)),
    )(dbias, buckets)
    return res[:, 0, :N_BUCKETS].T


SB_KG = 512
SB_QT = 2


def _softplus(z):
    return jnp.maximum(z, 0.0) + jnp.log(1.0 + jnp.exp(-jnp.abs(z)))


def _twice(t):
    t = t.astype(BF16)
    return jnp.concatenate([t, t], axis=0)


def _split_dot(x, t2):
    hi = x.astype(BF16)
    lo = (x - hi.astype(F32)).astype(BF16)
    return _dot(jnp.concatenate([hi, lo], axis=1), t2)


def sb_fwd(proj, qb0, kb0, vb0, n_slabs, *, name):
    s = proj.shape[0]
    nq = s // BLOCK
    nch = s // NORM_CH
    scale = HEAD_DIM ** -0.5

    def body(q_ref, k_ref, v_ref, o_ref, tot_ref, qlo_s, qhi_s, k_s, v_s):
        def prep(c, carry):
            rows = pl.ds(pl.multiple_of(c * NORM_CH, NORM_CH), NORM_CH)
            lo = _lo_mask((NORM_CH, LANES))
            qv = q_ref[rows, :] * scale
            qlo_s[rows, :] = jnp.where(lo, qv, 0.0).astype(BF16)
            qhi_s[rows, :] = jnp.where(lo, 0.0, qv).astype(BF16)
            k_s[rows, :] = k_ref[rows, :].astype(BF16)
            v_s[rows, :] = v_ref[rows, :].astype(BF16)
            return carry

        lax.fori_loop(0, nch, prep, 0)

        row = lax.broadcasted_iota(I32, (BLOCK, BLOCK), 0)
        col = lax.broadcasted_iota(I32, (BLOCK, BLOCK), 1)
        lo = col < HEAD_DIM
        t_ge = _twice(row >= col)
        rowg = lax.broadcasted_iota(I32, (BLOCK, SB_KG), 0)
        colg = lax.broadcasted_iota(I32, (BLOCK, SB_KG), 1)

        nsub = SB_KG // BLOCK
        chains = range(2 * SB_QT)
        nc = len(chains)

        def qloop(qs, phase):
            q0 = pl.multiple_of(qs * (SB_QT * BLOCK), SB_QT * BLOCK)
            qh = [(qlo_s, qhi_s)[i % 2][pl.ds(q0 + (i // 2) * BLOCK, BLOCK), :] for i in chains]
            gd = (qs * SB_QT) // nsub

            def logits(gi):
                k0 = pl.multiple_of(gi * SB_KG, SB_KG)
                kg = k_s[pl.ds(k0, SB_KG), :]
                return [_dot_nt(qh[i], kg) for i in chains]

            def group(gi, st, masks, npiece=nsub):
                k0 = pl.multiple_of(gi * SB_KG, SB_KG)
                vg = v_s[pl.ds(k0, npiece * BLOCK), :]
                c, o, z = list(st[:nc]), st[nc:2 * nc], st[2 * nc:]
                z_next = logits(jnp.maximum(gi - 1, 0))
                piece = lambda x, j: x[:, j * BLOCK:(j + 1) * BLOCK]
                a = [[None] * npiece for _ in chains]
                for j in reversed(range(npiece)):
                    zj = [piece(z[i], j) for i in chains]
                    lrem = [-_softplus(zj[i]) for i in chains]
                    if masks is not None:
                        lrem = [jnp.where(piece(masks[i // 2], j), lrem[i], 0.0) for i in chains]
                    incl = [_split_dot(lrem[i], t_ge) for i in chains]
                    for i in chains:
                        aij = jnp.exp(zj[i] + (c[i] + incl[i]))
                        if masks is not None:
                            aij = jnp.where(piece(masks[i // 2], j), aij, 0.0)
                        a[i][j] = aij.astype(BF16)
                        c[i] = c[i] + incl[i][:, 0:1]
                o = [o[i] + _dot(jnp.concatenate(a[i], axis=1), vg) for i in chains]
                return (*c, *o, *z_next)

            zc = [jnp.zeros((BLOCK, 1), F32)] * nc
            zo = [jnp.zeros((BLOCK, LANES), F32)] * nc
            masks = [(gd * SB_KG + colg) < (q0 + t * BLOCK + rowg) for t in range(SB_QT)]
            st = group(gd, (*zc, *zo, *logits(gd)), masks, (phase + 1) * SB_QT)
            st = lax.fori_loop(0, gd, lambda t, st: group(gd - 1 - t, st, None), st)
            for t in range(SB_QT):
                rows = pl.ds(q0 + t * BLOCK, BLOCK)
                o_ref[rows, :] = jnp.where(lo, st[nc + 2 * t], st[nc + 2 * t + 1])
                tot_ref[rows, :] = jnp.where(lo, st[2 * t], st[2 * t + 1])

        steps_per_group = nsub // SB_QT

        def per_group(g, carry):
            for phase in range(steps_per_group):
                qloop(g * steps_per_group + phase, phase)
            return carry

        lax.fori_loop(0, nq // nsub, per_group, 0)

    slab = lambda b0: pl.BlockSpec((s, LANES), lambda p: (0, b0 + p), pipeline_mode=pl.Buffered(1))
    w = LANES * n_slabs
    return pl.pallas_call(
        body, name=name, grid=(n_slabs,),
        in_specs=[slab(qb0), slab(kb0), slab(vb0)],
        out_specs=[pl.BlockSpec((s, LANES), lambda p: (0, p)), pl.BlockSpec((s, LANES), lambda p: (0, p))],
        out_shape=[jax.ShapeDtypeStruct((s, w), F32), jax.ShapeDtypeStruct((s, w), F32)],
        scratch_shapes=[pltpu.VMEM((s, LANES), BF16) for _ in range(4)],
        compiler_params=_params(("parallel",), VMEM_LIMIT),
    )(proj, proj, proj)


def _store_slabs(slabs, stage, dproj_ref, sems, blocks):
    s = stage.shape[1]

    def cast(c, carry):
        rows = pl.ds(pl.multiple_of(c * NORM_CH, NORM_CH), NORM_CH)
        for i, slab in enumerate(slabs):
            stage[i, rows, :] = slab[rows, :].astype(BF16)
        return carry

    lax.fori_loop(0, s // NORM_CH, cast, 0)
    copies = [pltpu.make_async_copy(stage.at[i], dproj_ref.at[:, pl.ds(pl.multiple_of(b * LANES, LANES), LANES)],
                                    sems.at[i]) for i, b in enumerate(blocks)]
    for cp in copies:
        cp.start()
    for cp in copies:
        cp.wait()


def sb_bwd(proj, qb0, kb0, vb0, n_slabs, dout, tot, dproj, *, name):
    s = proj.shape[0]
    nq = s // BLOCK
    nch = s // NORM_CH
    nsub = SB_KG // BLOCK
    scale = HEAD_DIM ** -0.5

    def body(q_ref, k_ref, v_ref, do_ref, tot_ref, dproj_in, dproj_ref,
             qlo_s, qhi_s, k_s, v_s, dlo_s, dhi_s, dq_ref, dk_ref, dv_ref, stage, sems):
        del dproj_in
        def prep(c, carry):
            rows = pl.ds(pl.multiple_of(c * NORM_CH, NORM_CH), NORM_CH)
            lo = _lo_mask((NORM_CH, LANES))
            qv = q_ref[rows, :] * scale
            dv = do_ref[rows, :]
            qlo_s[rows, :] = jnp.where(lo, qv, 0.0).astype(BF16)
            qhi_s[rows, :] = jnp.where(lo, 0.0, qv).astype(BF16)
            dlo_s[rows, :] = jnp.where(lo, dv, 0.0).astype(BF16)
            dhi_s[rows, :] = jnp.where(lo, 0.0, dv).astype(BF16)
            k_s[rows, :] = k_ref[rows, :].astype(BF16)
            v_s[rows, :] = v_ref[rows, :].astype(BF16)
            z = jnp.zeros((NORM_CH, LANES), F32)
            dk_ref[rows, :] = z
            dv_ref[rows, :] = z
            return carry

        lax.fori_loop(0, nch, prep, 0)

        row = lax.broadcasted_iota(I32, (BLOCK, BLOCK), 0)
        col = lax.broadcasted_iota(I32, (BLOCK, BLOCK), 1)
        lo = col < HEAD_DIM
        t_le = _twice(row <= col)
        rowg = lax.broadcasted_iota(I32, (BLOCK, SB_KG), 0)
        colg = lax.broadcasted_iota(I32, (BLOCK, SB_KG), 1)

        piece = lambda x, j: x[:, j * BLOCK:(j + 1) * BLOCK]
        chains = range(2 * SB_QT)
        nc = len(chains)

        def prefixes(x):
            return [[_split_dot(piece(x[i], j), t_le) for j in range(x[i].shape[1] // BLOCK)] for i in chains]

        def chain(pre, run, total=None):
            out = []
            for pj in pre:
                out.append(run + pj if total is None else total - run - pj)
                run = run + pj[:, BLOCK - 1:BLOCK]
            return jnp.concatenate(out, axis=1), run

        def qloop(qs, phase):
            q0 = pl.multiple_of(qs * (SB_QT * BLOCK), SB_QT * BLOCK)
            tile = lambda ref, i: ref[pl.ds(q0 + (i // 2) * BLOCK, BLOCK), :]
            qh = [tile((qlo_s, qhi_s)[i % 2], i) for i in chains]
            doh = [tile((dlo_s, dhi_s)[i % 2], i) for i in chains]
            tots = [tile(tot_ref, i)[:, (i % 2) * HEAD_DIM:(i % 2) * HEAD_DIM + 1] for i in chains]
            gd = (qs * SB_QT) // nsub

            def logits(gi):
                kg = k_s[pl.ds(pl.multiple_of(gi * SB_KG, SB_KG), SB_KG), :]
                return [_dot_nt(qh[i], kg) for i in chains]

            def group(gi, st, masks, npiece=nsub):
                k0 = pl.multiple_of(gi * SB_KG, SB_KG)
                wide = npiece * BLOCK
                kg, vg = k_s[pl.ds(k0, wide), :], v_s[pl.ds(k0, wide), :]
                cp, cg, dq = list(st[:nc]), list(st[nc:2 * nc]), st[2 * nc:2 * nc + SB_QT]
                z = [zi[:, :wide] for zi in st[2 * nc + SB_QT:]]
                masked = lambda x, i: x if masks is None else jnp.where(masks[i // 2][:, :wide], x, 0.0)
                z_next = logits(jnp.minimum(gi + 1, gd))
                da = [_dot_nt(doh[i], vg) for i in chains]
                sp = [_softplus(z[i]) for i in chains]
                lrem = [masked(-sp[i], i) for i in chains]
                pre = prefixes(lrem)
                e, a, g = [], [], []
                for i in chains:
                    suffix, cp[i] = chain(pre[i], cp[i], tots[i])
                    e.append(z[i] - sp[i])
                    a.append(masked(jnp.exp(e[i] + suffix), i))
                    g.append(a[i] * da[i])
                gpre = prefixes(g)
                dz = []
                for i in chains:
                    ginc, cg[i] = chain(gpre[i], cg[i])
                    dz.append(masked(g[i] - jnp.exp(e[i]) * ginc, i).astype(BF16))
                ab = [a[i].astype(BF16) for i in chains]
                dq = [dq[t] + jnp.where(lo, _dot(dz[2 * t], kg), _dot(dz[2 * t + 1], kg)) for t in range(SB_QT)]
                rows_of = lambda x: jnp.concatenate(x, axis=0)
                dk_ref[pl.ds(k0, wide), :] += _dot_tn(rows_of(dz), rows_of(qh))
                dv_ref[pl.ds(k0, wide), :] += _dot_tn(rows_of(ab), rows_of(doh))
                return (*cp, *cg, *dq, *z_next)

            zc = [jnp.zeros((BLOCK, 1), F32)] * (2 * nc)
            zq = [jnp.zeros((BLOCK, LANES), F32)] * SB_QT
            st = lax.fori_loop(0, gd, lambda gi, st: group(gi, st, None), (*zc, *zq, *logits(0)))
            st = group(gd, st, [(gd * SB_KG + colg) < (q0 + t * BLOCK + rowg) for t in range(SB_QT)],
                       (phase + 1) * SB_QT)
            for t in range(SB_QT):
                dq_ref[pl.ds(q0 + t * BLOCK, BLOCK), :] = st[2 * nc + t] * scale

        steps_per_group = nsub // SB_QT

        def per_group(g, carry):
            for phase in range(steps_per_group):
                qloop(g * steps_per_group + phase, phase)
            return carry

        lax.fori_loop(0, nq // nsub, per_group, 0)
        p = pl.program_id(0)
        _store_slabs((dq_ref, dk_ref, dv_ref), stage, dproj_ref, sems, (qb0 + p, kb0 + p, vb0 + p))

    slab = lambda b0: pl.BlockSpec((s, LANES), lambda p: (0, b0 + p), pipeline_mode=pl.Buffered(1))
    own = pl.BlockSpec((s, LANES), lambda p: (0, p), pipeline_mode=pl.Buffered(1))
    held = pl.BlockSpec(memory_space=pl.ANY)
    return pl.pallas_call(
        body, name=name, grid=(n_slabs,),
        in_specs=[slab(qb0), slab(kb0), slab(vb0), own, own, held],
        out_specs=held, out_shape=jax.ShapeDtypeStruct(dproj.shape, dproj.dtype),
        input_output_aliases={5: 0},
        scratch_shapes=[pltpu.VMEM((s, LANES), BF16) for _ in range(6)]
        + [pltpu.VMEM((s, LANES), F32) for _ in range(3)]
        + [pltpu.VMEM((3, s, LANES), BF16), pltpu.SemaphoreType.DMA((3,))],
        compiler_params=_params(("arbitrary",), VMEM_LIMIT),
    )(proj, proj, proj, dout, tot, dproj)


def _place():
    x, y, c = lax.axis_index("x"), lax.axis_index("y"), lax.axis_index("c")
    return x, y, c


def gather_small(v, *, name):
    m_per, n = v.shape

    def body(x_ref, out_ref, send_sems, recv_sems, local_sem):
        x, y, c = _place()
        me, sibling = (x, y, c), (x, y, 1 - c)
        chips = [(1 - x, y), (x, 1 - y), (1 - x, 1 - y)]

        def rows(px, py, pc):
            return out_ref.at[pl.ds((4 * px + 2 * py + pc) * m_per, m_per), :]

        def copy(k, block, to, src=None):
            return pltpu.make_async_remote_copy(
                src_ref=rows(*block) if src is None else src, dst_ref=rows(*block),
                send_sem=send_sems.at[k], recv_sem=recv_sems.at[k], device_id=to, device_id_type=MESH)

        mine = pltpu.make_async_copy(x_ref, rows(*me), local_sem)
        mine.start()
        first = [copy(0, me, sibling, src=x_ref)]
        first += [copy(1 + j, me, (*chip, c), src=x_ref) for j, chip in enumerate(chips)]
        for cp in first:
            cp.start()
        passed = [copy(4 + j, (*chip, c), sibling) for j, chip in enumerate(chips)]
        for j, chip in enumerate(chips):
            copy(1 + j, (*chip, c), me).wait_recv()
            passed[j].start()
        copy(0, sibling, me).wait_recv()
        for j, chip in enumerate(chips):
            copy(4 + j, (*chip, 1 - c), me).wait_recv()
        for cp in first + passed:
            cp.wait_send()
        mine.wait()

    return pl.pallas_call(
        body, name=name,
        out_shape=jax.ShapeDtypeStruct((N_DEV * m_per, n), v.dtype),
        in_specs=[pl.BlockSpec(memory_space=pltpu.VMEM)],
        out_specs=pl.BlockSpec(memory_space=pltpu.VMEM),
        scratch_shapes=[pltpu.SemaphoreType.DMA((7,)), pltpu.SemaphoreType.DMA((7,)), pltpu.SemaphoreType.DMA],
        compiler_params=_params(None, VMEM_LIMIT),
    )(v)


_HBM = pl.BlockSpec(memory_space=pltpu.HBM)
_SEM = pl.BlockSpec(memory_space=pltpu.SEMAPHORE)
_EFFECT = pltpu.SideEffectType.DATAFLOW_SIDE_EFFECTING


def _peer_copies(src_refs, land_refs, send_sems, recv_sems, per_dest):
    x, y, c = _place()
    me = 4 * x + 2 * y + c
    copies = []
    for src, land, ssem, rsem in zip(src_refs, land_refs, send_sems, recv_sems):
        for k in (1, 2, 4, 3, 5, 6, 7):
            px, py, pc = x ^ (k >> 2 & 1), y ^ (k >> 1 & 1), c ^ (k & 1)
            copies.append(pltpu.make_async_remote_copy(
                src_ref=src.at[4 * px + 2 * py + pc] if per_dest else src, dst_ref=land.at[me],
                send_sem=ssem.at[k - 1], recv_sem=rsem.at[k - 1], device_id=(px, py, pc), device_id_type=MESH))
    return copies


def _own_copies(src_refs, land_refs, send_sems, per_dest):
    x, y, c = _place()
    me = 4 * x + 2 * y + c
    return [pltpu.make_async_copy(src.at[me] if per_dest else src, land.at[me], ssem.at[7])
            for src, land, ssem in zip(src_refs, land_refs, send_sems)]


def exchange_start(srcs, per_dest, *, name):
    n = len(srcs)
    lands = [lax.empty(a.shape if per_dest else (N_DEV,) + a.shape, a.dtype) for a in srcs]

    def body(*refs):
        src_refs, land_refs = refs[:n], refs[n:2 * n]
        send_sems, recv_sems = refs[2 * n:3 * n], refs[3 * n:4 * n]
        token = refs[-1]
        for cp in _peer_copies(src_refs, land_refs, send_sems, recv_sems, per_dest):
            cp.start()
        for cp in _own_copies(src_refs, land_refs, send_sems, per_dest):
            cp.start()
        token[...] = jnp.zeros_like(token)

    hbm = lambda a: pltpu.HBM(a.shape, a.dtype)
    res = pl.pallas_call(
        body, name=name,
        out_shape=(*[pltpu.SemaphoreType.DMA((8,))] * n, *[pltpu.SemaphoreType.DMA((7,))] * n,
                   *[hbm(a) for a in srcs], *[hbm(a) for a in lands], jax.ShapeDtypeStruct((8, LANES), F32)),
        in_specs=[_HBM] * (2 * n),
        out_specs=(*[_SEM] * (2 * n), *[_HBM] * (2 * n), pl.BlockSpec(memory_space=pltpu.VMEM)),
        input_output_aliases={i: 2 * n + i for i in range(2 * n)},
        compiler_params=pltpu.CompilerParams(has_side_effects=_EFFECT),
    )(*[pltpu.with_memory_space_constraint(a, pltpu.HBM) for a in (*srcs, *lands)])
    handles = [(res[a], res[n + a], res[2 * n + a], res[3 * n + a]) for a in range(n)]
    return handles, res[-1]


def exchange_wait(handles, per_dest, after, *, name):
    n = len(handles)

    def body(*refs):
        src_refs, land_refs = refs[:n], refs[n:2 * n]
        send_sems, recv_sems = refs[2 * n:3 * n], refs[3 * n:4 * n]
        for cp in _peer_copies(src_refs, land_refs, send_sems, recv_sems, per_dest):
            cp.wait_send()
            cp.wait_recv()
        for cp in _own_copies(src_refs, land_refs, send_sems, per_dest):
            cp.wait()

    srcs, lands = [h[2] for h in handles], [h[3] for h in handles]
    hbm = lambda a: pltpu.HBM(a.shape, a.dtype)
    res = pl.pallas_call(
        body, name=name,
        out_shape=(*[hbm(a) for a in srcs], *[hbm(a) for a in lands]),
        in_specs=[*[_HBM] * (2 * n), *[_SEM] * (2 * n), pl.BlockSpec(memory_space=pl.ANY)],
        out_specs=tuple([_HBM] * (2 * n)),
        input_output_aliases={i: i for i in range(2 * n)},
        compiler_params=pltpu.CompilerParams(has_side_effects=_EFFECT),
    )(*srcs, *lands, *[h[0] for h in handles], *[h[1] for h in handles], after)
    return res[n:]


def _adamw_math(w, g, m, v):
    m = ADAM_B1 * m + (1.0 - ADAM_B1) * g
    v = ADAM_B2 * v + (1.0 - ADAM_B2) * (g * g)
    m_hat = m / (1.0 - ADAM_B1 ** ADAM_STEP)
    v_hat = v / (1.0 - ADAM_B2 ** ADAM_STEP)
    delta = -ADAM_LR * (m_hat / (jnp.sqrt(v_hat) + ADAM_EPS) + ADAM_WD * w)
    return delta, m, v


def adamw_parts(parts, w, m, v, layer, outs, *, name):
    depth, r, cdim = w.shape
    n_parts = parts.shape[0]
    tr = _pick(r, [t for t in (512, 256, 128, 112, 64, 32, 16) if t * cdim <= 256 * 1024])

    def body(p_ref, w_ref, m_ref, v_ref, g0, d0, nm0, nv0, g_ref, d_ref, nm_ref, nv_ref):
        g = p_ref[0].astype(F32)
        for q in range(1, n_parts):
            g = g + p_ref[q].astype(F32)
        delta, nm, nv = _adamw_math(w_ref[...], g, m_ref[...], v_ref[...])
        g_ref[...], d_ref[...], nm_ref[...], nv_ref[...] = g, delta, nm, nv

    t = pl.BlockSpec((None, tr, cdim), lambda i: (layer, i, 0))
    held = pl.BlockSpec(memory_space=pl.ANY)
    return pl.pallas_call(
        body, name=name, grid=(r // tr,),
        in_specs=[pl.BlockSpec((n_parts, tr, cdim), lambda i: (0, i, 0)), t, t, t, held, held, held, held],
        out_specs=[t, t, t, t],
        out_shape=[jax.ShapeDtypeStruct((depth, r, cdim), F32)] * 4,
        input_output_aliases={4: 0, 5: 1, 6: 2, 7: 3},
        compiler_params=_params(("parallel",), VMEM_LIMIT),
    )(parts, w, m, v, *outs)


def sum_devices(gathered, *, name):
    m_rows = gathered.shape[1]

    def body(ga_ref, g_ref):
        g = ga_ref[0]
        for dev in range(1, N_DEV):
            g = g + ga_ref[dev]
        g_ref[...] = g

    return pl.pallas_call(
        body, name=name, out_shape=jax.ShapeDtypeStruct((m_rows, LANES), F32),
        compiler_params=_params(None, VMEM_LIMIT),
    )(gathered)


def adamw_small(g, w, m, v, *, name):
    m_rows = w.shape[0]

    def body(g_ref, w_ref, m_ref, v_ref, d_ref, nm_ref, nv_ref):
        d_ref[...], nm_ref[...], nv_ref[...] = _adamw_math(w_ref[...], g_ref[...], m_ref[...], v_ref[...])

    return pl.pallas_call(
        body, name=name, out_shape=[jax.ShapeDtypeStruct((m_rows, LANES), F32)] * 3,
        compiler_params=_params(None, VMEM_LIMIT),
    )(g, w, m, v)


def _t5_bucket(dist):
    max_exact = N_BUCKETS // 2
    d = jnp.maximum(dist, 0)
    large = max_exact + (jnp.log(jnp.maximum(d, 1).astype(F32) / max_exact)
                         / math.log(T5_MAX_DIST / max_exact) * (N_BUCKETS - max_exact)).astype(I32)
    large = jnp.minimum(large, N_BUCKETS - 1)
    return jnp.where(d < max_exact, d, large)


def _rel():
    return jnp.arange(BLOCK)[:, None] + BLOCK - jnp.arange(2 * BLOCK)[None, :]


def _band_bias(table, dils, max_dists):
    rel = _rel()
    biases, buckets = [], []
    for d, md in zip(dils, max_dists):
        bk = _t5_bucket(rel * d)
        vis = (rel >= 0) & (rel <= md)
        looked_up = jnp.zeros((table.shape[1],) + rel.shape, F32)
        for b in range(N_BUCKETS):
            looked_up = jnp.where((bk == b)[None], table[b][:, None, None], looked_up)
        with_prev = jnp.where(vis[None], looked_up, NEG_INF)
        first = jnp.arange(2 * BLOCK)[None, None, :] >= BLOCK
        biases.append(jnp.stack([with_prev, jnp.where(first, with_prev, NEG_INF)]))
        buckets.append(bk.astype(I32))
    return jnp.stack(biases), jnp.stack(buckets)


def _pack(pieces, rows):
    flat = jnp.concatenate([p.reshape(-1) for p in pieces])
    return jnp.pad(flat, (0, rows * LANES - flat.shape[0])).reshape(rows, LANES)


def _unpack(packed, shapes):
    flat = packed.reshape(-1)
    out, off = [], 0
    for sh in shapes:
        n = math.prod(sh)
        out.append(flat[off:off + n].reshape(sh))
        off += n
    return out


def _tile2(g):
    return jnp.concatenate([g, g])


def kernel(x, attn_norm, w_in, a_q_gain, a_k_gain, a_sinks, c_q_gain, c_k_gain, rel_bias_table, mix_out_gain, w_out, ffn_norm, w_up, conv_w, conv_b, w_down, loss_target, m_attn_norm, m_w_in, m_a_q_gain, m_a_k_gain, m_a_sinks, m_c_q_gain, m_c_k_gain, m_rel_bias_table, m_mix_out_gain, m_w_out, m_ffn_norm, m_w_up, m_conv_w, m_conv_b, m_w_down, v_attn_norm, v_w_in, v_a_q_gain, v_a_k_gain, v_a_sinks, v_c_q_gain, v_c_k_gain, v_rel_bias_table, v_mix_out_gain, v_w_out, v_ffn_norm, v_w_up, v_conv_w, v_conv_b, v_w_down):
    depth, d_model, in_shard = w_in.shape
    ff2_shard = w_up.shape[2]
    s = x.shape[1]
    in_width, ff2 = N_DEV * in_shard, N_DEV * ff2_shard
    n_heads = d_model // HEAD_DIM
    ha, hb, hc = n_heads // 4, n_heads // 4, n_heads // 2
    sa, sb, sc = ha // 2, hb // 2, hc // 2
    kv_a = ha // 4
    assert kv_a == 2 and BLOCK == LANES
    cb_aq, cb_ak, cb_av = 0, sa, sa + 1
    cb_bq = sa + 2
    cb_bk, cb_bv = cb_bq + sb, cb_bq + 2 * sb
    cb_cq = cb_bq + 3 * sb
    cb_ck, cb_cv = cb_cq + sc, cb_cq + 2 * sc
    assert (cb_cv + sc) * LANES == in_width
    dev = 4 * lax.axis_index("x") + 2 * lax.axis_index("y") + lax.axis_index("c")

    per_array = 3
    wnames = ("w_in", "w_out", "w_up", "w_down", "conv_w")
    sent = dict(w_in=lambda w: w.T, w_up=lambda w: w.T, w_out=lambda w: w, w_down=lambda w: w, conv_w=lambda w: w)
    rows = lambda g: g.reshape(N_DEV * g.shape[1], g.shape[2])
    whole = dict(w_in=rows, w_up=rows, w_out=rows, w_down=rows,
                 conv_w=lambda g: jnp.transpose(g, (1, 0, 2)).reshape(g.shape[1], N_DEV * g.shape[2]))
    gathers = {}
    token = jnp.zeros((8, LANES), F32)
    for l in range(depth):
        for gi, group in enumerate([[n] for n in wnames] if l < per_array else [wnames]):
            srcs = [(sent[n](dict(w_in=w_in, w_out=w_out, w_up=w_up, w_down=w_down, conv_w=conv_w)[n][l])
                     + token[0, 0]).astype(F32 if n == "conv_w" else BF16) for n in group]
            handles, token = exchange_start(srcs, False, name=f"gather_start_{l}_{gi}")
            gathers.update({(l, n): h for n, h in zip(group, handles)})

    def gathered(l, names, after):
        landed = exchange_wait([gathers[l, n] for n in names], False, after,
                               name=f"gather_wait_{l}_{wnames.index(names[0])}")
        return {n: whole[n](g) for n, g in zip(names, landed)}

    bias_a, buckets_a = _band_bias(rel_bias_table[:, :ha], (1,), (WINDOW_A - 1,))
    bias_c, buckets_c = _band_bias(rel_bias_table[:, ha:], DILATIONS, (BLOCK,) * len(DILATIONS))

    xs = x[0]
    saved = []
    wi, wo, wu, wd, cw = ([None] * depth for _ in range(5))
    for l in range(depth):
        if l < per_array:
            need = lambda n, after, l=l: gathered(l, (n,), after)[n]
        else:
            layer_w = gathered(l, wnames, xs)
            need = lambda n, after: layer_w[n]
        wi[l] = need("w_in", token if l == 0 else xs)
        h1 = rmsnorm_fwd(xs, attn_norm[l], name="attn_norm_fwd")
        proj = matmul(h1, wi[l], trans_b=True, name="in_proj")
        sinks = jnp.repeat(a_sinks[l], HEAD_DIM).reshape(sa, 1, LANES)
        gaq, gak = _tile2(a_q_gain[l]), _tile2(a_k_gain[l])
        gcq, gck = _tile2(c_q_gain[l]), _tile2(c_k_gain[l])
        out_a, lse_a = banded_fwd(proj, cb_aq, cb_ak, cb_av, sa, gaq, gak, bias_a, (1,), sinks, True, name="swa_fwd")
        out_b, tot_b = sb_fwd(proj, cb_bq, cb_bk, cb_bv, sb, name="stick_fwd")
        out_c, lse_c = banded_fwd(proj, cb_cq, cb_ck, cb_cv, sc, gcq, gck, bias_c, DILATIONS, None, False,
                                  name="dilated_fwd")
        mix = mixnorm_fwd([out_a, out_b, out_c], mix_out_gain[l], name="mix_norm_fwd")
        wo[l] = need("w_out", mix)
        x_mid = matmul(mix, wo[l], res=xs, name="out_proj")
        h2 = rmsnorm_fwd(x_mid, ffn_norm[l], name="ffn_norm_fwd")
        wu[l] = need("w_up", h2)
        p = matmul(h2, wu[l], trans_b=True, name="up_proj")
        cw[l] = need("conv_w", p)
        act = ffn_act_fwd(p, cw[l], conv_b[l], name="ffn_act_fwd")
        wd[l] = need("w_down", act)
        x_out = matmul(act, wd[l], res=x_mid, name="down_proj")
        saved.append(dict(x_in=xs, h1=h1, proj=proj, out_a=out_a, lse_a=lse_a, out_b=out_b, tot_b=tot_b,
                          out_c=out_c, lse_c=lse_c, mix=mix, x_mid=x_mid, h2=h2, p=p, act=act,
                          sinks=sinks, gains=(gaq, gak, gcq, gck)))
        xs = x_out

    dx, dx_b, loss_part = loss_head(xs, loss_target[0], name="loss_head")

    small = {k: [None] * depth for k in ("attn_norm", "a_q_gain", "a_k_gain", "a_sinks", "c_q_gain", "c_k_gain",
                                         "mix_out_gain", "ffn_norm", "conv_w", "conv_b")}
    big = {k: [None] * depth for k in ("w_in", "w_out", "w_up", "w_down")}
    dbias_a = dbias_c = None
    scatters = {}
    token = jnp.zeros((8, LANES), F32)
    names_big = ("w_in", "w_out", "w_up", "w_down")

    def scatter(l, names):
        parts = [big[n][l] for n in names]
        handles, tok = exchange_start(parts, True, name=f"scatter_start_{l}_{names_big.index(names[0])}")
        scatters.update({(l, n): h for n, h in zip(names, handles)})
        return tok

    by_rows = lambda a: a.reshape(N_DEV, a.shape[0] // N_DEV, a.shape[1])
    for l in reversed(range(depth)):
        each = l == 0
        sv = saved[l]
        gaq, gak, gcq, gck = sv["gains"]
        da = matmul(dx_b, wd[l], trans_b=True, name="down_proj_dx")
        big["w_down"][l] = by_rows(matmul(sv["act"], dx_b, trans_a=True, out_dtype=BF16, name="down_proj_dw"))
        if each:
            token = scatter(l, ("w_down",))
        dp, small["conv_w"][l], small["conv_b"][l] = ffn_act_bwd(da, sv["p"], cw[l], conv_b[l] + token[0, 0],
                                                                 name="ffn_act_bwd")
        dh2 = matmul(dp, wu[l], name="up_proj_dx")
        big["w_up"][l] = matmul(sv["h2"], dp, trans_a=True, out_dtype=BF16, col_blocks=N_DEV, name="up_proj_dw")
        if each:
            token = scatter(l, ("w_up",))
        dx_mid, dx_mid_b, small["ffn_norm"][l] = rmsnorm_bwd(dh2, sv["x_mid"], ffn_norm[l] + token[0, 0], dx,
                                                   name="ffn_norm_bwd")
        dmix = matmul(dx_mid_b, wo[l], trans_b=True, name="out_proj_dx")
        big["w_out"][l] = by_rows(matmul(sv["mix"], dx_mid_b, trans_a=True, out_dtype=BF16, name="out_proj_dw"))
        if each:
            token = scatter(l, ("w_out",))
        (d_oa, d_ob, d_oc), small["mix_out_gain"][l] = mixnorm_bwd(
            dmix, [sv["out_a"], sv["out_b"], sv["out_c"]], mix_out_gain[l] + token[0, 0], name="mix_norm_bwd")
        dproj = lax.empty((s, in_width), BF16)
        dproj, db_a, dgq_a, dgk_a, dsink = banded_bwd(
            sv["proj"], cb_aq, cb_ak, cb_av, sa, gaq, gak, bias_a, (1,), sv["sinks"], True,
            d_oa, sv["out_a"], sv["lse_a"], dproj, name="swa_bwd")
        dproj = sb_bwd(sv["proj"], cb_bq, cb_bk, cb_bv, sb, d_ob, sv["tot_b"], dproj, name="stick_bwd")
        dproj, db_c, dgq_c, dgk_c = banded_bwd(
            sv["proj"], cb_cq, cb_ck, cb_cv, sc, gcq, gck, bias_c, DILATIONS, None, False,
            d_oc, sv["out_c"], sv["lse_c"], dproj, name="dilated_bwd")
        fold = lambda g: g.reshape(-1, HEAD_DIM).sum(axis=0)
        small["a_q_gain"][l], small["a_k_gain"][l] = fold(dgq_a), fold(dgk_a)
        small["c_q_gain"][l], small["c_k_gain"][l] = fold(dgq_c), fold(dgk_c)
        small["a_sinks"][l] = dsink[:, ::HEAD_DIM].reshape(-1)
        dbias_a = db_a if dbias_a is None else dbias_a + db_a
        dbias_c = db_c if dbias_c is None else dbias_c + db_c
        big["w_in"][l] = by_rows(matmul(dproj, sv["h1"], trans_a=True, out_dtype=BF16, name="in_proj_dw"))
        token = scatter(l, ("w_in",) if each else names_big)
        dh1 = matmul(dproj, wi[l], name="in_proj_dx")
        dx, dx_b, small["attn_norm"][l] = rmsnorm_bwd(dh1, sv["x_in"], attn_norm[l] + token[0, 0], dx_mid,
                                                name="attn_norm_bwd")

    flip = lambda t: jnp.swapaxes(t, 1, 2)
    w_big = dict(w_in=(flip(w_in), flip(m_w_in), flip(v_w_in)), w_out=(w_out, m_w_out, v_w_out),
                 w_up=(w_up, m_w_up, v_w_up), w_down=(w_down, m_w_down, v_w_down))
    results = {k: [lax.empty(w_big[k][0].shape, F32) for _ in range(4)] for k in names_big}

    def update(l, names, after):
        landed = exchange_wait([scatters[l, n] for n in names], True, after,
                               name=f"scatter_wait_{l}_{names_big.index(names[0])}")
        for k, parts in zip(names, landed):
            results[k] = adamw_parts(parts, *w_big[k], l, results[k], name="adamw_large")
        return results[names[-1]][0]

    after = dx_b
    for l in reversed(range(1, depth)):
        after = update(l, names_big, after)

    dtable = jnp.concatenate([bias_bwd(dbias_a, buckets_a, name="swa_bias_bwd"),
                              bias_bwd(dbias_c, buckets_c, name="dilated_bias_bwd")], axis=1)

    order = ("attn_norm", "a_q_gain", "a_k_gain", "a_sinks", "c_q_gain", "c_k_gain", "rel_bias_table",
             "mix_out_gain", "ffn_norm", "conv_w", "conv_b")
    partial = {k: jnp.stack(v) for k, v in small.items()}
    partial["rel_bias_table"] = dtable
    pieces = [partial[k] for k in order] + [loss_part.reshape(1)]
    n_small = sum(math.prod(pc.shape) for pc in pieces)
    rows = -(-n_small // (8 * LANES)) * 8
    packed, after = lax.optimization_barrier((_pack(pieces, rows), after))
    gathered = gather_small(packed, name="gather_small_grads")
    summed = _unpack(sum_devices(gathered.reshape(N_DEV, rows, LANES), name="sum_small_grads"),
                     [pc.shape for pc in pieces])
    g_small = dict(zip(order, summed[:-1]))
    loss = summed[-1][0]
    g_small["conv_w"] = lax.dynamic_slice_in_dim(g_small["conv_w"], dev * ff2_shard, ff2_shard, axis=2)

    w_small = dict(attn_norm=attn_norm, a_q_gain=a_q_gain, a_k_gain=a_k_gain, a_sinks=a_sinks, c_q_gain=c_q_gain,
                   c_k_gain=c_k_gain, rel_bias_table=rel_bias_table, mix_out_gain=mix_out_gain, ffn_norm=ffn_norm,
                   conv_w=conv_w, conv_b=conv_b)
    m_small = dict(attn_norm=m_attn_norm, a_q_gain=m_a_q_gain, a_k_gain=m_a_k_gain, a_sinks=m_a_sinks,
                   c_q_gain=m_c_q_gain, c_k_gain=m_c_k_gain, rel_bias_table=m_rel_bias_table,
                   mix_out_gain=m_mix_out_gain, ffn_norm=m_ffn_norm, conv_w=m_conv_w, conv_b=m_conv_b)
    v_small = dict(attn_norm=v_attn_norm, a_q_gain=v_a_q_gain, a_k_gain=v_a_k_gain, a_sinks=v_a_sinks,
                   c_q_gain=v_c_q_gain, c_k_gain=v_c_k_gain, rel_bias_table=v_rel_bias_table,
                   mix_out_gain=v_mix_out_gain, ffn_norm=v_ffn_norm, conv_w=v_conv_w, conv_b=v_conv_b)
    shapes = [w_small[k].shape for k in order]
    n_upd = sum(math.prod(sh) for sh in shapes)
    urows = -(-n_upd // (8 * LANES)) * 8
    packs = [_pack([d[k] for k in order], urows) for d in (g_small, w_small, m_small, v_small)]
    upd = adamw_small(*packs, name="adamw_small")
    delta_s, newm_s, newv_s = [dict(zip(order, _unpack(u, shapes))) for u in upd]

    after = update(0, names_big[1:], upd[0])
    update(0, names_big[:1], after)
    results["w_in"] = [flip(t) for t in results["w_in"]]
    g_big, delta_b, newm_b, newv_b = [{k: results[k][i] for k in names_big} for i in range(4)]

    all_names = ("attn_norm", "w_in", "a_q_gain", "a_k_gain", "a_sinks", "c_q_gain", "c_k_gain", "rel_bias_table",
                 "mix_out_gain", "w_out", "ffn_norm", "w_up", "conv_w", "conv_b", "w_down")
    pick = lambda sm, bg: [bg[k] if k in bg else sm[k] for k in all_names]
    return (loss, dx[None], *pick(g_small, g_big), *pick(delta_s, delta_b), *pick(newm_s, newm_b),
            *pick(newv_s, newv_b))
```

```python
import functools
import math

import jax
import jax.numpy as jnp
from jax import lax
from jax.experimental import pallas as pl
from jax.experimental.pallas import tpu as pltpu

F32, BF16, I32 = jnp.float32, jnp.bfloat16, jnp.int32
MESH = pl.DeviceIdType.MESH

HEAD_DIM = 64
LANES = 128
BLOCK = 128
EPS = 1e-6
NEG_INF = -1e30
N_BUCKETS = 32
T5_MAX_DIST = 2048
WINDOW_A = 128
DILATIONS = (1, 4, 16)
N_DEV = 8
VMEM_LIMIT = 56 * 1024 * 1024
MATMUL_VMEM = 46 * 1024 * 1024

ADAM_LR, ADAM_B1, ADAM_B2, ADAM_EPS, ADAM_WD, ADAM_STEP = 0.001, 0.9, 0.999, 1e-08, 0.01, 10


def _params(sem=None, vmem=None):
    return pltpu.CompilerParams(dimension_semantics=sem, vmem_limit_bytes=vmem)


def _pick(n, cands):
    for c in cands:
        if n % c == 0:
            return c
    raise ValueError(f"no tile for {n}")


def _dot(a, b):
    return lax.dot_general(a, b, (((1,), (0,)), ((), ())), preferred_element_type=F32)


def _dot_nt(a, b):
    return lax.dot_general(a, b, (((1,), (1,)), ((), ())), preferred_element_type=F32)


def _dot_tn(a, b):
    return lax.dot_general(a, b, (((0,), (0,)), ((), ())), preferred_element_type=F32)


def matmul(a, b, *, trans_a=False, trans_b=False, out_dtype=F32, res=None, col_blocks=None, name):
    a_halves, b_halves = a.ndim == 3, b.ndim == 3
    assert not (a_halves and trans_a) and not (b_halves and trans_b)
    m, k = (a.shape[1], 2 * a.shape[2]) if a_halves else (a.shape[1], a.shape[0]) if trans_a else a.shape
    n = 2 * b.shape[2] if b_halves else b.shape[0] if trans_b else b.shape[1]
    k_unit, n_unit = (k // 2 if a_halves else k), (n // 2 if b_halves else n)
    tm_cands = tuple(t for t in (1408, 1024, 896, 512, 256) if m % t == 0)
    tn_cands = ((n // col_blocks,) if col_blocks
                else tuple(t for t in (1024, 1408, 768, 512, 256, 128) if n_unit % t == 0))

    def footprint(tm, tk, tn):
        tiles = 2 * (tm * tk * a.dtype.itemsize + tk * tn * b.dtype.itemsize)
        return tiles + tm * tn * (4 + 2 * jnp.dtype(out_dtype).itemsize + (8 if res is not None else 0))

    tk, tm, tn = next((tk, tm, tn) for tk in (5376, 4096, 2816, 2048, 1792, 1024, 768, 512, 256) if k_unit % tk == 0
                      for tm in tm_cands if 2 * tm >= tm_cands[0]
                      for tn in tn_cands if footprint(tm, tk, tn) <= MATMUL_VMEM)
    nk = k // tk
    nk_half, nj_half = k_unit // tk, n_unit // tn
    dn = (((0 if trans_a else 1,), (1 if trans_b else 0,)), ((), ()))

    def body(*refs):
        if res is None:
            a_ref, b_ref, o_ref, acc = refs
        else:
            a_ref, b_ref, r_ref, o_ref, acc = refs
        kk = pl.program_id(2)

        @pl.when(kk == 0)
        def _():
            acc[...] = jnp.zeros_like(acc)

        acc[...] += lax.dot_general(a_ref[...].astype(BF16), b_ref[...].astype(BF16), dn,
                                    preferred_element_type=F32)

        @pl.when(kk == nk - 1)
        def _():
            r = acc[...]
            if res is not None:
                r = r_ref[...] + r
            o_ref[...] = r.astype(out_dtype)

    b_spec = (pl.BlockSpec((tn, tk), lambda i, j, kk: (j, kk)) if trans_b
              else pl.BlockSpec((None, tk, tn), lambda i, j, kk: (j // nj_half, kk, j % nj_half)) if b_halves
              else pl.BlockSpec((tk, tn), lambda i, j, kk: (kk, j)))
    a_spec = (pl.BlockSpec((tk, tm), lambda i, j, kk: (kk, i)) if trans_a
              else pl.BlockSpec((None, tm, tk), lambda i, j, kk: (kk // nk_half, i, kk % nk_half)) if a_halves
              else pl.BlockSpec((tm, tk), lambda i, j, kk: (i, kk)))
    in_specs = [a_spec, b_spec]
    args = [a, b]
    if res is not None:
        in_specs.append(pl.BlockSpec((tm, tn), lambda i, j, kk: (i, j)))
        args.append(res)
    if col_blocks:
        out_spec = pl.BlockSpec((None, tm, tn), lambda i, j, kk: (j, i, 0))
        out_shape = jax.ShapeDtypeStruct((col_blocks, m, tn), out_dtype)
    else:
        out_spec = pl.BlockSpec((tm, tn), lambda i, j, kk: (i, j))
        out_shape = jax.ShapeDtypeStruct((m, n), out_dtype)
    return pl.pallas_call(
        body, name=name, grid=(m // tm, n // tn, nk),
        in_specs=in_specs, out_specs=out_spec, out_shape=out_shape,
        scratch_shapes=[pltpu.VMEM((tm, tn), F32)],
        compiler_params=_params(("parallel", "parallel", "arbitrary"), VMEM_LIMIT),
    )(*args)


def norm_matmul(x, g, bt, *, name):
    m, k = x.shape
    n = bt.shape[0]
    tm, tn = _pick(m, (1024, 512)), _pick(n, (1024, 768, 512, 256, 128))

    def body(x_ref, g_ref, b_ref, o_ref, h_ref):
        @pl.when(pl.program_id(1) == 0)
        def _():
            xv = x_ref[...]
            r = lax.rsqrt(jnp.mean(xv * xv, axis=-1, keepdims=True) + EPS)
            h_ref[...] = (xv * r * g_ref[...]).astype(BF16)

        o_ref[...] = _dot_nt(h_ref[...], b_ref[...])

    return pl.pallas_call(
        body, name=name, grid=(m // tm, n // tn),
        in_specs=[pl.BlockSpec((tm, k), lambda i, j: (i, 0)), pl.BlockSpec((1, k), lambda i, j: (0, 0)),
                  pl.BlockSpec((tn, k), lambda i, j: (j, 0))],
        out_specs=[pl.BlockSpec((tm, tn), lambda i, j: (i, j)), pl.BlockSpec((tm, k), lambda i, j: (i, 0))],
        out_shape=[jax.ShapeDtypeStruct((m, n), F32), jax.ShapeDtypeStruct((m, k), BF16)],
        compiler_params=_params(("parallel", "arbitrary"), VMEM_LIMIT),
    )(x, g.reshape(1, k), bt)


def rmsnorm_fwd(x, g, *, name):
    s, d = x.shape
    tm = 512

    def body(x_ref, g_ref, o_ref):
        xv = x_ref[...]
        r = lax.rsqrt(jnp.mean(xv * xv, axis=-1, keepdims=True) + EPS)
        o_ref[...] = (xv * r * g_ref[...]).astype(BF16)

    return pl.pallas_call(
        body, name=name, grid=(s // tm,),
        in_specs=[pl.BlockSpec((tm, d), lambda i: (i, 0)), pl.BlockSpec((1, d), lambda i: (0, 0))],
        out_specs=pl.BlockSpec((tm, d), lambda i: (i, 0)),
        out_shape=jax.ShapeDtypeStruct((s, d), BF16),
        compiler_params=_params(("parallel",)),
    )(x, g.reshape(1, d))


def rmsnorm_bwd(dh, x, g, dres, *, name):
    s, d = x.shape
    tm = 512

    def body(dh_ref, x_ref, g_ref, dres_ref, dx_ref, dxb_ref, dg_ref):
        @pl.when(pl.program_id(0) == 0)
        def _():
            dg_ref[...] = jnp.zeros_like(dg_ref)

        xv, dhv = x_ref[...], dh_ref[...]
        r = lax.rsqrt(jnp.mean(xv * xv, axis=-1, keepdims=True) + EPS)
        gd = dhv * g_ref[...]
        dot = jnp.mean(gd * xv, axis=-1, keepdims=True)
        dx = dres_ref[...] + (r * gd - xv * (r * r * r * dot))
        dx_ref[...] = dx
        dxb_ref[...] = dx.astype(BF16)
        dg_ref[...] += jnp.sum(dhv * (xv * r), axis=0, keepdims=True)

    row = pl.BlockSpec((tm, d), lambda i: (i, 0))
    dx, dxb, dg = pl.pallas_call(
        body, name=name, grid=(s // tm,),
        in_specs=[row, row, pl.BlockSpec((1, d), lambda i: (0, 0)), row],
        out_specs=[row, row, pl.BlockSpec((1, d), lambda i: (0, 0))],
        out_shape=[jax.ShapeDtypeStruct((s, d), F32), jax.ShapeDtypeStruct((s, d), BF16),
                   jax.ShapeDtypeStruct((1, d), F32)],
        compiler_params=_params(("arbitrary",), VMEM_LIMIT),
    )(dh, x, g.reshape(1, d), dres)
    return dx, dxb, dg[0]


def loss_head(y, target, *, name):
    s, d = y.shape
    tm = 512

    def body(y_ref, t_ref, dy_ref, dyb_ref, l_ref):
        @pl.when(pl.program_id(0) == 0)
        def _():
            l_ref[...] = jnp.zeros_like(l_ref)

        e = y_ref[...] - t_ref[...]
        dy = e / float(d)
        dy_ref[...] = dy
        dyb_ref[...] = dy.astype(BF16)
        per_tok = jnp.mean(e * e, axis=-1, keepdims=True)
        l_ref[...] += 0.5 * jnp.sum(per_tok, axis=0, keepdims=True)

    row = pl.BlockSpec((tm, d), lambda i: (i, 0))
    dy, dyb, l = pl.pallas_call(
        body, name=name, grid=(s // tm,),
        in_specs=[row, row],
        out_specs=[row, row, pl.BlockSpec((8, LANES), lambda i: (0, 0))],
        out_shape=[jax.ShapeDtypeStruct((s, d), F32), jax.ShapeDtypeStruct((s, d), BF16),
                   jax.ShapeDtypeStruct((8, LANES), F32)],
        compiler_params=_params(("arbitrary",)),
    )(y, target)
    return dy, dyb, l[0, 0]


FFN_TN = 256
FFN_CH = 256


def _rows_before(ref, r0, first):
    if first:
        cur = ref[pl.ds(0, FFN_CH), :]
        row = lax.broadcasted_iota(I32, cur.shape, 0)
        sh1 = jnp.where(row < 1, 0.0, pltpu.roll(cur, 1, axis=0))
        sh2 = jnp.where(row < 2, 0.0, pltpu.roll(cur, 2, axis=0))
        return cur, sh1, sh2
    ext = ref[pl.ds(pl.multiple_of(r0 - 8, 8), FFN_CH + 8), :]
    return ext[8:], pltpu.roll(ext, 1, axis=0)[8:], pltpu.roll(ext, 2, axis=0)[8:]


def _rows_after(ref, r0, last):
    if last:
        cur = ref[pl.ds(r0, FFN_CH), :]
        row = lax.broadcasted_iota(I32, cur.shape, 0)
        up1 = jnp.where(row >= FFN_CH - 1, 0.0, pltpu.roll(cur, FFN_CH - 1, axis=0))
        up2 = jnp.where(row >= FFN_CH - 2, 0.0, pltpu.roll(cur, FFN_CH - 2, axis=0))
        return cur, up1, up2
    n = FFN_CH + 8
    ext = ref[pl.ds(r0, n), :]
    return ext[:FFN_CH], pltpu.roll(ext, n - 1, axis=0)[:FFN_CH], pltpu.roll(ext, n - 2, axis=0)[:FFN_CH]


def _sigmoid(x):
    return 0.5 * jnp.tanh(0.5 * x) + 0.5


def ffn_act_fwd(p, conv_w, conv_b, *, name):
    s, f2 = p.shape
    f = f2 // 2
    nj = f // FFN_TN
    nch = s // FFN_CH

    def body(pg_ref, pu_ref, wg_ref, wu_ref, bg_ref, bu_ref, a_ref):
        def conv(ref, w_ref, b_ref, r0, first):
            cur, sh1, sh2 = _rows_before(ref, r0, first)
            return ((b_ref[...] + w_ref[0:1, :] * sh2) + w_ref[1:2, :] * sh1) + w_ref[2:3, :] * cur

        def chunk(r0, first):
            gate = conv(pg_ref, wg_ref, bg_ref, r0, first)
            up = conv(pu_ref, wu_ref, bu_ref, r0, first)
            a_ref[pl.ds(r0, FFN_CH), :] = (gate * _sigmoid(gate) * up).astype(BF16)

        chunk(0, True)

        def step(c, carry):
            chunk(pl.multiple_of(c * FFN_CH, FFN_CH), False)
            return carry

        lax.fori_loop(1, nch, step, 0)

    col = lambda off: pl.BlockSpec((s, FFN_TN), lambda j: (0, j + off))
    wcol = lambda off: pl.BlockSpec((3, FFN_TN), lambda j: (0, j + off))
    bcol = lambda off: pl.BlockSpec((1, FFN_TN), lambda j: (0, j + off))
    return pl.pallas_call(
        body, name=name, grid=(nj,),
        in_specs=[col(0), col(nj), wcol(0), wcol(nj), bcol(0), bcol(nj)],
        out_specs=pl.BlockSpec((s, FFN_TN), lambda j: (0, j)),
        out_shape=jax.ShapeDtypeStruct((s, f), BF16),
        compiler_params=_params(("parallel",), VMEM_LIMIT),
    )(p, p, conv_w, conv_w, conv_b.reshape(1, f2), conv_b.reshape(1, f2))


def ffn_act_bwd(da, p, conv_w, conv_b, *, name):
    s, f2 = p.shape
    f = f2 // 2
    nj = f // FFN_TN
    nch = s // FFN_CH

    def body(da_ref, pg_ref, pu_ref, wg_ref, wu_ref, bg_ref, bu_ref,
             dp_ref, dwg_ref, dwu_ref, dbg_ref, dbu_ref, dug_s, duu_s):
        dpg_ref, dpu_ref = dp_ref.at[0], dp_ref.at[1]
        def conv(ref, w_ref, b_ref, r0, first):
            cur, sh1, sh2 = _rows_before(ref, r0, first)
            u = ((b_ref[...] + w_ref[0:1, :] * sh2) + w_ref[1:2, :] * sh1) + w_ref[2:3, :] * cur
            return u, (sh2, sh1, cur)

        def taps_sum(du, taps):
            return jnp.concatenate([jnp.sum(du * t, axis=0, keepdims=True) for t in taps], axis=0)

        def chunk(r0, first, acc):
            dwg, dwu, dbg, dbu = acc
            gate, tg = conv(pg_ref, wg_ref, bg_ref, r0, first)
            up, tu = conv(pu_ref, wu_ref, bu_ref, r0, first)
            dav = da_ref[pl.ds(r0, FFN_CH), :]
            sg = _sigmoid(gate)
            dgate = dav * up * (sg * (1.0 + gate * (1.0 - sg)))
            dup = dav * (gate * sg)
            dug_s[pl.ds(r0, FFN_CH), :] = dgate
            duu_s[pl.ds(r0, FFN_CH), :] = dup
            return (dwg + taps_sum(dgate, tg), dwu + taps_sum(dup, tu),
                    dbg + jnp.sum(dgate, axis=0, keepdims=True), dbu + jnp.sum(dup, axis=0, keepdims=True))

        z3 = jnp.zeros((3, FFN_TN), F32)
        z1 = jnp.zeros((1, FFN_TN), F32)
        acc = chunk(0, True, (z3, z3, z1, z1))
        acc = lax.fori_loop(1, nch, lambda c, a: chunk(pl.multiple_of(c * FFN_CH, FFN_CH), False, a), acc)
        dwg_ref[...], dwu_ref[...], dbg_ref[...], dbu_ref[...] = acc

        def back(src, w_ref, dst, r0, last):
            cur, up1, up2 = _rows_after(src, r0, last)
            dst[pl.ds(r0, FFN_CH), :] = (w_ref[2:3, :] * cur + w_ref[1:2, :] * up1 + w_ref[0:1, :] * up2).astype(BF16)

        def step(c, carry):
            r0 = pl.multiple_of(c * FFN_CH, FFN_CH)
            back(dug_s, wg_ref, dpg_ref, r0, False)
            back(duu_s, wu_ref, dpu_ref, r0, False)
            return carry

        lax.fori_loop(0, nch - 1, step, 0)
        back(dug_s, wg_ref, dpg_ref, (nch - 1) * FFN_CH, True)
        back(duu_s, wu_ref, dpu_ref, (nch - 1) * FFN_CH, True)

    col = lambda off: pl.BlockSpec((s, FFN_TN), lambda j: (0, j + off))
    wcol = lambda off: pl.BlockSpec((3, FFN_TN), lambda j: (0, j + off))
    bcol = lambda off: pl.BlockSpec((1, FFN_TN), lambda j: (0, j + off))
    outs = pl.pallas_call(
        body, name=name, grid=(nj,),
        in_specs=[col(0), col(0), col(nj), wcol(0), wcol(nj), bcol(0), bcol(nj)],
        out_specs=[pl.BlockSpec((2, s, FFN_TN), lambda j: (0, 0, j)), wcol(0), wcol(0), bcol(0), bcol(0)],
        out_shape=[jax.ShapeDtypeStruct((2, s, f), BF16),
                   jax.ShapeDtypeStruct((3, f), F32), jax.ShapeDtypeStruct((3, f), F32),
                   jax.ShapeDtypeStruct((1, f), F32), jax.ShapeDtypeStruct((1, f), F32)],
        scratch_shapes=[pltpu.VMEM((s, FFN_TN), F32), pltpu.VMEM((s, FFN_TN), F32)],
        compiler_params=_params(("parallel",), VMEM_LIMIT),
    )(da, p, p, conv_w, conv_w, conv_b.reshape(1, f2), conv_b.reshape(1, f2))
    dp, dwg, dwu, dbg, dbu = outs
    return dp, jnp.concatenate([dwg, dwu], axis=1), jnp.concatenate([dbg, dbu], axis=1)[0]


def mixnorm_fwd(outs, gain, *, name):
    s = outs[0].shape[0]
    widths = [o.shape[1] for o in outs]
    total = sum(widths)
    tm = 512

    def body(*refs):
        o_refs, g_ref, m_ref = refs[:-2], refs[-2], refs[-1]
        off = 0
        for o_ref, w in zip(o_refs, widths):
            xv = o_ref[...]
            r = lax.rsqrt(jnp.mean(xv * xv, axis=-1, keepdims=True) + EPS)
            m_ref[:, off:off + w] = (xv * r * g_ref[:, off:off + w]).astype(BF16)
            off += w

    return pl.pallas_call(
        body, name=name, grid=(s // tm,),
        in_specs=[pl.BlockSpec((tm, w), lambda i: (i, 0)) for w in widths] + [pl.BlockSpec((1, total), lambda i: (0, 0))],
        out_specs=pl.BlockSpec((tm, total), lambda i: (i, 0)),
        out_shape=jax.ShapeDtypeStruct((s, total), BF16),
        compiler_params=_params(("parallel",)),
    )(*outs, gain.reshape(1, total))


def mixnorm_bwd(dmix, outs, gain, *, name):
    s = outs[0].shape[0]
    widths = [o.shape[1] for o in outs]
    total = sum(widths)
    n = len(outs)
    tm = 512

    def body(*refs):
        dm_ref, o_refs, g_ref = refs[0], refs[1:1 + n], refs[1 + n]
        d_refs, dg_ref = refs[2 + n:2 + 2 * n], refs[2 + 2 * n]

        @pl.when(pl.program_id(0) == 0)
        def _():
            dg_ref[...] = jnp.zeros_like(dg_ref)

        off = 0
        for o_ref, d_ref, w in zip(o_refs, d_refs, widths):
            xv = o_ref[...]
            dhv = dm_ref[:, off:off + w]
            r = lax.rsqrt(jnp.mean(xv * xv, axis=-1, keepdims=True) + EPS)
            gd = dhv * g_ref[:, off:off + w]
            dot = jnp.mean(gd * xv, axis=-1, keepdims=True)
            d_ref[...] = r * gd - xv * (r * r * r * dot)
            dg_ref[:, off:off + w] += jnp.sum(dhv * (xv * r), axis=0, keepdims=True)
            off += w

    res = pl.pallas_call(
        body, name=name, grid=(s // tm,),
        in_specs=[pl.BlockSpec((tm, total), lambda i: (i, 0))]
        + [pl.BlockSpec((tm, w), lambda i: (i, 0)) for w in widths] + [pl.BlockSpec((1, total), lambda i: (0, 0))],
        out_specs=[pl.BlockSpec((tm, w), lambda i: (i, 0)) for w in widths] + [pl.BlockSpec((1, total), lambda i: (0, 0))],
        out_shape=[jax.ShapeDtypeStruct((s, w), F32) for w in widths] + [jax.ShapeDtypeStruct((1, total), F32)],
        compiler_params=_params(("arbitrary",)),
    )(dmix, *outs, gain.reshape(1, total))
    return res[:n], res[n][0]


NORM_CH = 512
FWD_TILES = 4
BWD_TILES = 4


def _lo_mask(shape):
    return lax.broadcasted_iota(I32, shape, 1) < HEAD_DIM


def _head_sum(x, lo):
    del lo
    i = lax.broadcasted_iota(I32, (LANES, LANES), 0) // HEAD_DIM
    j = lax.broadcasted_iota(I32, (LANES, LANES), 1) // HEAD_DIM
    return _split_dot(x, _twice(i == j))


def _head_stats(x, lo):
    return lax.rsqrt(_head_sum(x * x, lo) * (1.0 / HEAD_DIM) + EPS)


def _swap_halves(x):
    return pltpu.roll(x, HEAD_DIM, axis=1)


def _replicate_head(x, lo, use_lo_head):
    sw = _swap_halves(x)
    return jnp.where(use_lo_head, jnp.where(lo, x, sw), jnp.where(lo, sw, x))


def _tile_rows(i, s, d):
    nb = s // (BLOCK * d)
    r = i // nb
    b = i % nb
    start = r + (BLOCK * d) * b
    prev = start - (BLOCK * d) * jnp.minimum(b, 1)
    return start, prev, b > 0


def _rows(ref, start, d):
    if d == 1:
        return ref[pl.ds(pl.multiple_of(start, BLOCK), BLOCK), :]
    return ref[pl.ds(start, BLOCK, stride=d), :]


def _set_rows(ref, start, d, val):
    if d == 1:
        ref[pl.ds(pl.multiple_of(start, BLOCK), BLOCK), :] = val
    else:
        ref[pl.ds(start, BLOCK, stride=d), :] = val


def banded_fwd(proj, qb0, kb0, vb0, n_slabs, gq, gk, bias, dils, sinks, gqa, *, name):
    s = proj.shape[0]
    nbr = len(dils)
    nt = s // BLOCK
    nch = s // NORM_CH
    has_sink = sinks is not None

    def body(*refs):
        q_ref, k_ref, v_ref, gq_ref, gk_ref, b_ref = refs[:6]
        rest = refs[6:]
        if has_sink:
            sink_ref, rest = rest[0], rest[1:]
        out_ref, lse_ref, qn_s, kn_s, vv_s, o_s, l_s = rest
        p = pl.program_id(0)
        use_lo = (p // 2) == 0

        def prep(c, carry):
            rows = pl.ds(pl.multiple_of(c * NORM_CH, NORM_CH), NORM_CH)
            lo = _lo_mask((NORM_CH, LANES))
            qv, kv, vv = q_ref[rows, :], k_ref[rows, :], v_ref[rows, :]
            qn_s[rows, :] = qv * _head_stats(qv, lo) * gq_ref[...] * (HEAD_DIM ** -0.5)
            kn = kv * _head_stats(kv, lo) * gk_ref[...]
            if gqa:
                kn = _replicate_head(kn, lo, use_lo)
                vv = _replicate_head(vv, lo, use_lo)
            kn_s[rows, :] = kn
            vv_s[rows, :] = vv
            return carry

        lax.fori_loop(0, nch, prep, 0)

        lo = _lo_mask((BLOCK, LANES))
        hms = [lo, jnp.logical_not(lo)]
        heads, tiles = range(2), range(FWD_TILES)
        for br, d in enumerate(dils):
            def step(ii, carry, br=br, d=d):
                pos = [_tile_rows(ii * FWD_TILES + u, s, d) for u in tiles]
                kc = [carry[0]] + [_rows(kn_s, pos[u][0], d).astype(BF16) for u in tiles]
                vc = [carry[1]] + [_rows(vv_s, pos[u][0], d).astype(BF16) for u in tiles]
                kcat = [jnp.concatenate([kc[u], kc[u + 1]], axis=0) for u in tiles]
                vcat = [jnp.concatenate([vc[u], vc[u + 1]], axis=0) for u in tiles]
                qt = [_rows(qn_s, pos[u][0], d) for u in tiles]
                sc = [[_dot_nt(jnp.where(hms[h], qt[u], 0.0).astype(BF16), kcat[u])
                       + b_ref[br, jnp.where(pos[u][2], 0, 1), h] for h in heads] for u in tiles]
                m = [[jnp.max(sc[u][h], axis=1, keepdims=True) for h in heads] for u in tiles]
                pe = [[jnp.exp(sc[u][h] - m[u][h]) for h in heads] for u in tiles]
                den = [[jnp.sum(pe[u][h], axis=1, keepdims=True) for h in heads] for u in tiles]
                o = [[_dot(pe[u][h].astype(BF16), vcat[u]) * (1.0 / den[u][h]) for h in heads] for u in tiles]
                for u in tiles:
                    _set_rows(o_s.at[br], pos[u][0], d, jnp.where(lo, o[u][0], o[u][1]))
                    _set_rows(l_s.at[br], pos[u][0], d,
                              jnp.where(lo, m[u][0] + jnp.log(den[u][0]), m[u][1] + jnp.log(den[u][1])))
                return kc[-1], vc[-1]

            none_yet = jnp.zeros((BLOCK, LANES), BF16)
            lax.fori_loop(0, nt // FWD_TILES, step, (none_yet, none_yet))

        def combine(c, carry):
            rows = pl.ds(pl.multiple_of(c * NORM_CH, NORM_CH), NORM_CH)
            ls = [l_s[br, rows, :] for br in range(nbr)]
            mx = functools.reduce(jnp.maximum, ls)
            if has_sink:
                mx = jnp.maximum(mx, sink_ref[...])
            es = [jnp.exp(l - mx) for l in ls]
            tot = functools.reduce(jnp.add, es)
            if has_sink:
                tot = tot + jnp.exp(sink_ref[...] - mx)
            acc = es[0] * o_s[0, rows, :]
            for br in range(1, nbr):
                acc = acc + es[br] * o_s[br, rows, :]
            out_ref[rows, :] = acc * (1.0 / tot)
            lse_ref[rows, :] = mx + jnp.log(tot)
            return carry

        lax.fori_loop(0, nch, combine, 0)

    slab = lambda b0, shared: pl.BlockSpec((s, LANES), (lambda p: (0, b0)) if shared else (lambda p: (0, b0 + p)),
                                           pipeline_mode=pl.Buffered(1))
    vec = pl.BlockSpec((1, LANES), lambda p: (0, 0))
    in_specs = [slab(qb0, False), slab(kb0, gqa), slab(vb0, gqa), vec, vec,
                pl.BlockSpec((nbr, 2, 2, BLOCK, 2 * BLOCK), lambda p: (0, 0, p, 0, 0))]
    args = [proj, proj, proj, gq.reshape(1, LANES), gk.reshape(1, LANES), bias]
    if has_sink:
        in_specs.append(pl.BlockSpec((None, 1, LANES), lambda p: (p, 0, 0)))
        args.append(sinks)
    w = LANES * n_slabs
    return pl.pallas_call(
        body, name=name, grid=(n_slabs,),
        in_specs=in_specs,
        out_specs=[pl.BlockSpec((s, LANES), lambda p: (0, p)), pl.BlockSpec((s, LANES), lambda p: (0, p))],
        out_shape=[jax.ShapeDtypeStruct((s, w), F32), jax.ShapeDtypeStruct((s, w), F32)],
        scratch_shapes=[pltpu.VMEM((s, LANES), F32), pltpu.VMEM((s, LANES), F32), pltpu.VMEM((s, LANES), F32),
                        pltpu.VMEM((nbr, s, LANES), F32), pltpu.VMEM((nbr, s, LANES), F32)],
        compiler_params=_params(("parallel",), VMEM_LIMIT),
    )(*args)


def banded_bwd(proj, qb0, kb0, vb0, n_slabs, gq, gk, bias, dils, sinks, gqa, dout, out, lse, dproj, *, name):
    s = proj.shape[0]
    nbr = len(dils)
    nt = s // BLOCK
    nch = s // NORM_CH
    has_sink = sinks is not None
    scale = HEAD_DIM ** -0.5

    def body(*refs):
        q_ref, k_ref, v_ref, gq_ref, gk_ref, b_ref, do_ref, o_ref, lse_ref = refs[:9]
        rest = refs[9:]
        if has_sink:
            sink_ref, rest = rest[0], rest[1:]
        dproj_ref, db_ref, dgq_ref, dgk_ref = rest[1:5]
        rest = rest[5:]
        if has_sink:
            dsink_ref, rest = rest[0], rest[1:]
        qn_s, kn_s, vv_s, dl_s, dqn_s, dkn_s, dvv_s, dq_ref, dk_ref, dv_ref, stage, sems = rest
        p = pl.program_id(0)
        use_lo = (p // 2) == 0

        def prep(c, carry):
            rows = pl.ds(pl.multiple_of(c * NORM_CH, NORM_CH), NORM_CH)
            lo = _lo_mask((NORM_CH, LANES))
            qv, kv, vv = q_ref[rows, :], k_ref[rows, :], v_ref[rows, :]
            qn_s[rows, :] = qv * _head_stats(qv, lo) * gq_ref[...] * scale
            kn = kv * _head_stats(kv, lo) * gk_ref[...]
            if gqa:
                kn = _replicate_head(kn, lo, use_lo)
                vv = _replicate_head(vv, lo, use_lo)
            kn_s[rows, :] = kn
            vv_s[rows, :] = vv
            delta = _head_sum(do_ref[rows, :] * o_ref[rows, :], lo)
            odd = lax.broadcasted_iota(I32, (NORM_CH, LANES), 1) % 2 == 1
            dl_s[rows, :] = jnp.where(odd, delta, lse_ref[rows, :])
            z = jnp.zeros((NORM_CH, LANES), F32)
            dqn_s[rows, :] = z
            dkn_s[rows, :] = z
            dvv_s[rows, :] = z
            if has_sink:
                ps = jnp.exp(sink_ref[...] - lse_ref[rows, :])
                return carry - jnp.sum(ps * delta, axis=0, keepdims=True)
            return carry

        dsink = lax.fori_loop(0, nch, prep, jnp.zeros((1, LANES), F32))
        if has_sink:
            dsink_ref[...] = jnp.broadcast_to(dsink, (8, LANES))

        lo = _lo_mask((BLOCK, LANES))
        hms = [lo, jnp.logical_not(lo)]
        heads, tiles = range(2), range(BWD_TILES)
        for br, d in enumerate(dils):
            db_ref[br] = jnp.zeros((2, BLOCK, 2 * BLOCK), F32)

            def step(ii, carry, br=br, d=d):
                pos = [_tile_rows(ii * BWD_TILES + u, s, d) for u in tiles]
                kc = [carry[0]] + [_rows(kn_s, pos[u][0], d).astype(BF16) for u in tiles]
                vc = [carry[1]] + [_rows(vv_s, pos[u][0], d).astype(BF16) for u in tiles]
                kcat = [jnp.concatenate([kc[u], kc[u + 1]], axis=0) for u in tiles]
                vcat = [jnp.concatenate([vc[u], vc[u + 1]], axis=0) for u in tiles]
                qt = [_rows(qn_s, pos[u][0], d) for u in tiles]
                dot_ = [_rows(do_ref, pos[u][0], d) for u in tiles]
                st_t = [_rows(dl_s, pos[u][0], d) for u in tiles]
                qh = [[jnp.where(hms[h], qt[u], 0.0).astype(BF16) for h in heads] for u in tiles]
                doh = [[jnp.where(hms[h], dot_[u], 0.0).astype(BF16) for h in heads] for u in tiles]
                sc = [[_dot_nt(qh[u][h], kcat[u]) + b_ref[br, jnp.where(pos[u][2], 0, 1), h] for h in heads]
                      for u in tiles]
                dp = [[_dot_nt(doh[u][h], vcat[u]) for h in heads] for u in tiles]
                lane0 = [0, HEAD_DIM]
                pr = [[jnp.exp(sc[u][h] - st_t[u][:, lane0[h]:lane0[h] + 1]) for h in heads] for u in tiles]
                dlog = [[pr[u][h] * (dp[u][h] - st_t[u][:, lane0[h] + 1:lane0[h] + 2]) for h in heads] for u in tiles]
                for h in heads:
                    db_ref[br, h] += functools.reduce(jnp.add, [dlog[u][h] for u in tiles])
                dlb = [[dlog[u][h].astype(BF16) for h in heads] for u in tiles]
                prb = [[pr[u][h].astype(BF16) for h in heads] for u in tiles]
                dq_t = [jnp.where(lo, _dot(dlb[u][0], kcat[u]), _dot(dlb[u][1], kcat[u])) * scale for u in tiles]
                rows2 = lambda x: jnp.concatenate(x, axis=0)
                dk_t = [_dot_tn(rows2(dlb[u]), rows2(qh[u])) for u in tiles]
                dv_t = [_dot_tn(rows2(prb[u]), rows2(doh[u])) for u in tiles]
                for u in tiles:
                    start, prev = pos[u][0], pos[u][1]
                    _set_rows(dqn_s, start, d, _rows(dqn_s, start, d) + dq_t[u])
                    _set_rows(dkn_s, prev, d, _rows(dkn_s, prev, d) + dk_t[u][:BLOCK])
                    _set_rows(dkn_s, start, d, _rows(dkn_s, start, d) + dk_t[u][BLOCK:])
                    _set_rows(dvv_s, prev, d, _rows(dvv_s, prev, d) + dv_t[u][:BLOCK])
                    _set_rows(dvv_s, start, d, _rows(dvv_s, start, d) + dv_t[u][BLOCK:])
                return kc[-1], vc[-1]

            none_yet = jnp.zeros((BLOCK, LANES), BF16)
            lax.fori_loop(0, nt // BWD_TILES, step, (none_yet, none_yet))

        if gqa:
            @pl.when(p == 0)
            def _():
                dk_ref[...] = jnp.zeros_like(dk_ref)
                dv_ref[...] = jnp.zeros_like(dv_ref)

        def finish(c, carry):
            dgq, dgk = carry
            rows = pl.ds(pl.multiple_of(c * NORM_CH, NORM_CH), NORM_CH)
            lo = _lo_mask((NORM_CH, LANES))

            def norm_bwd(xv, dn, g_ref):
                r = _head_stats(xv, lo)
                gd = dn * g_ref[...]
                dot = _head_sum(gd * xv, lo) * (1.0 / HEAD_DIM)
                return r * gd - xv * (r * r * r * dot), dn * (xv * r)

            dq, gq_part = norm_bwd(q_ref[rows, :], dqn_s[rows, :], gq_ref)
            dq_ref[rows, :] = dq
            dgq = dgq + jnp.sum(gq_part, axis=0, keepdims=True)
            kv, dkn, dvv = k_ref[rows, :], dkn_s[rows, :], dvv_s[rows, :]
            if gqa:
                kv = _replicate_head(kv, lo, use_lo)
                dkn = dkn + _swap_halves(dkn)
                dvv = dvv + _swap_halves(dvv)
                lane = lax.broadcasted_iota(I32, (NORM_CH, LANES), 1)
                mine = (lane // HEAD_DIM) == (p // 2)
                dk, gk_part = norm_bwd(kv, dkn, gk_ref)
                dk_ref[rows, :] += jnp.where(mine, dk, 0.0)
                dv_ref[rows, :] += jnp.where(mine, dvv, 0.0)
                gk_part = jnp.where(lo, gk_part, 0.0)
            else:
                dk, gk_part = norm_bwd(kv, dkn, gk_ref)
                dk_ref[rows, :] = dk
                dv_ref[rows, :] = dvv
            dgk = dgk + jnp.sum(gk_part, axis=0, keepdims=True)
            return dgq, dgk

        z = jnp.zeros((1, LANES), F32)
        dgq, dgk = lax.fori_loop(0, nch, finish, (z, z))
        dgq_ref[...] = jnp.broadcast_to(dgq, (8, LANES))
        dgk_ref[...] = jnp.broadcast_to(dgk, (8, LANES))
        if gqa:
            _store_slabs((dq_ref,), stage, dproj_ref, sems, (qb0 + p,))

            @pl.when(p == n_slabs - 1)
            def _():
                _store_slabs((dk_ref, dv_ref), stage, dproj_ref, sems, (kb0, vb0))
        else:
            _store_slabs((dq_ref, dk_ref, dv_ref), stage, dproj_ref, sems, (qb0 + p, kb0 + p, vb0 + p))

    def slab_of(b0, shared):
        return pl.BlockSpec((s, LANES), (lambda p: (0, b0)) if shared else (lambda p: (0, b0 + p)),
                            pipeline_mode=pl.Buffered(1))

    vec = pl.BlockSpec((1, LANES), lambda p: (0, 0))
    own = pl.BlockSpec((s, LANES), lambda p: (0, p), pipeline_mode=pl.Buffered(1))
    in_specs = [slab_of(qb0, False), slab_of(kb0, gqa), slab_of(vb0, gqa), vec, vec,
                pl.BlockSpec((nbr, 2, 2, BLOCK, 2 * BLOCK), lambda p: (0, 0, p, 0, 0)), own, own, own]
    args = [proj, proj, proj, gq.reshape(1, LANES), gk.reshape(1, LANES), bias, dout, out, lse]
    if has_sink:
        in_specs.append(pl.BlockSpec((None, 1, LANES), lambda p: (p, 0, 0)))
        args.append(sinks)
    held = pl.BlockSpec(memory_space=pl.ANY)
    in_specs.append(held)
    args.append(dproj)
    part = pl.BlockSpec((None, 8, LANES), lambda p: (p, 0, 0))
    out_specs = [held, pl.BlockSpec((nbr, 2, BLOCK, 2 * BLOCK), lambda p: (0, p, 0, 0)), part, part]
    out_shape = [jax.ShapeDtypeStruct(dproj.shape, dproj.dtype),
                 jax.ShapeDtypeStruct((nbr, 2 * n_slabs, BLOCK, 2 * BLOCK), F32),
                 jax.ShapeDtypeStruct((n_slabs, 8, LANES), F32), jax.ShapeDtypeStruct((n_slabs, 8, LANES), F32)]
    if has_sink:
        out_specs.append(part)
        out_shape.append(jax.ShapeDtypeStruct((n_slabs, 8, LANES), F32))
    res = pl.pallas_call(
        body, name=name, grid=(n_slabs,),
        in_specs=in_specs, out_specs=out_specs, out_shape=out_shape,
        input_output_aliases={len(args) - 1: 0},
        scratch_shapes=[pltpu.VMEM((s, LANES), F32) for _ in range(10)]
        + [pltpu.VMEM((3, s, LANES), BF16), pltpu.SemaphoreType.DMA((3,))],
        compiler_params=_params(("arbitrary",), VMEM_LIMIT),
    )(*args)
    outs = [res[0], res[1], res[2][:, 0, :], res[3][:, 0, :]]
    if has_sink:
        outs.append(res[4][:, 0, :])
    return outs


def bias_bwd(dbias, buckets, *, name):
    nbr, h = dbias.shape[:2]

    def body(db_ref, bk_ref, o_ref):
        lane = lax.broadcasted_iota(I32, (1, LANES), 1)
        acc = jnp.zeros((1, LANES), F32)
        for b in range(N_BUCKETS):
            tot = jnp.zeros((1, 1), F32)
            for br in range(nbr):
                sel = jnp.where(bk_ref[br] == b, db_ref[br], 0.0)
                tot = tot + jnp.sum(jnp.sum(sel, axis=0, keepdims=True), axis=1, keepdims=True)
            acc = jnp.where(lane == b, tot, acc)
        o_ref[...] = jnp.broadcast_to(acc, (8, LANES))

    res = pl.pallas_call(
        body, name=name, grid=(h,),
        in_specs=[pl.BlockSpec((nbr, None, BLOCK, 2 * BLOCK), lambda i: (0, i, 0, 0)),
                  pl.BlockSpec((nbr, BLOCK, 2 * BLOCK), lambda i: (0, 0, 0))],
        out_specs=pl.BlockSpec((None, 8, LANES), lambda i: (i, 0, 0)),
        out_shape=jax.ShapeDtypeStruct((h, 8, LANES), F32),
        compiler_params=_params(("parallel",)),
    )(dbias, buckets)
    return res[:, 0, :N_BUCKETS].T


SB_KG = 512
SB_QT = 2


def _softplus(z):
    return jnp.maximum(z, 0.0) + jnp.log(1.0 + jnp.exp(-jnp.abs(z)))


def _twice(t):
    t = t.astype(BF16)
    return jnp.concatenate([t, t], axis=0)


def _split_dot(x, t2):
    hi = x.astype(BF16)
    lo = (x - hi.astype(F32)).astype(BF16)
    return _dot(jnp.concatenate([hi, lo], axis=1), t2)


def sb_fwd(proj, qb0, kb0, vb0, n_slabs, *, name):
    s = proj.shape[0]
    nq = s // BLOCK
    nch = s // NORM_CH
    scale = HEAD_DIM ** -0.5

    def body(q_ref, k_ref, v_ref, o_ref, tot_ref, qlo_s, qhi_s, k_s, v_s):
        def prep(c, carry):
            rows = pl.ds(pl.multiple_of(c * NORM_CH, NORM_CH), NORM_CH)
            lo = _lo_mask((NORM_CH, LANES))
            qv = q_ref[rows, :] * scale
            qlo_s[rows, :] = jnp.where(lo, qv, 0.0).astype(BF16)
            qhi_s[rows, :] = jnp.where(lo, 0.0, qv).astype(BF16)
            k_s[rows, :] = k_ref[rows, :].astype(BF16)
            v_s[rows, :] = v_ref[rows, :].astype(BF16)
            return carry

        lax.fori_loop(0, nch, prep, 0)

        row = lax.broadcasted_iota(I32, (BLOCK, BLOCK), 0)
        col = lax.broadcasted_iota(I32, (BLOCK, BLOCK), 1)
        lo = col < HEAD_DIM
        t_ge = _twice(row >= col)
        rowg = lax.broadcasted_iota(I32, (BLOCK, SB_KG), 0)
        colg = lax.broadcasted_iota(I32, (BLOCK, SB_KG), 1)

        nsub = SB_KG // BLOCK
        chains = range(2 * SB_QT)
        nc = len(chains)

        def qloop(qs, phase):
            q0 = pl.multiple_of(qs * (SB_QT * BLOCK), SB_QT * BLOCK)
            qh = [(qlo_s, qhi_s)[i % 2][pl.ds(q0 + (i // 2) * BLOCK, BLOCK), :] for i in chains]
            gd = (qs * SB_QT) // nsub

            def logits(gi):
                k0 = pl.multiple_of(gi * SB_KG, SB_KG)
                kg = k_s[pl.ds(k0, SB_KG), :]
                return [_dot_nt(qh[i], kg) for i in chains]

            def group(gi, st, masks, npiece=nsub):
                k0 = pl.multiple_of(gi * SB_KG, SB_KG)
                vg = v_s[pl.ds(k0, npiece * BLOCK), :]
                c, o, z = list(st[:nc]), st[nc:2 * nc], st[2 * nc:]
                z_next = logits(jnp.maximum(gi - 1, 0))
                piece = lambda x, j: x[:, j * BLOCK:(j + 1) * BLOCK]
                a = [[None] * npiece for _ in chains]
                for j in reversed(range(npiece)):
                    zj = [piece(z[i], j) for i in chains]
                    lrem = [-_softplus(zj[i]) for i in chains]
                    if masks is not None:
                        lrem = [jnp.where(piece(masks[i // 2], j), lrem[i], 0.0) for i in chains]
                    incl = [_split_dot(lrem[i], t_ge) for i in chains]
                    for i in chains:
                        aij = jnp.exp(zj[i] + (c[i] + incl[i]))
                        if masks is not None:
                            aij = jnp.where(piece(masks[i // 2], j), aij, 0.0)
                        a[i][j] = aij.astype(BF16)
                        c[i] = c[i] + incl[i][:, 0:1]
                o = [o[i] + _dot(jnp.concatenate(a[i], axis=1), vg) for i in chains]
                return (*c, *o, *z_next)

            zc = [jnp.zeros((BLOCK, 1), F32)] * nc
            zo = [jnp.zeros((BLOCK, LANES), F32)] * nc
            masks = [(gd * SB_KG + colg) < (q0 + t * BLOCK + rowg) for t in range(SB_QT)]
            st = group(gd, (*zc, *zo, *logits(gd)), masks, (phase + 1) * SB_QT)
            st = lax.fori_loop(0, gd, lambda t, st: group(gd - 1 - t, st, None), st)
            for t in range(SB_QT):
                rows = pl.ds(q0 + t * BLOCK, BLOCK)
                o_ref[rows, :] = jnp.where(lo, st[nc + 2 * t], st[nc + 2 * t + 1])
                tot_ref[rows, :] = jnp.where(lo, st[2 * t], st[2 * t + 1])

        steps_per_group = nsub // SB_QT

        def per_group(g, carry):
            for phase in range(steps_per_group):
                qloop(g * steps_per_group + phase, phase)
            return carry

        lax.fori_loop(0, nq // nsub, per_group, 0)

    slab = lambda b0: pl.BlockSpec((s, LANES), lambda p: (0, b0 + p), pipeline_mode=pl.Buffered(1))
    w = LANES * n_slabs
    return pl.pallas_call(
        body, name=name, grid=(n_slabs,),
        in_specs=[slab(qb0), slab(kb0), slab(vb0)],
        out_specs=[pl.BlockSpec((s, LANES), lambda p: (0, p)), pl.BlockSpec((s, LANES), lambda p: (0, p))],
        out_shape=[jax.ShapeDtypeStruct((s, w), F32), jax.ShapeDtypeStruct((s, w), F32)],
        scratch_shapes=[pltpu.VMEM((s, LANES), BF16) for _ in range(4)],
        compiler_params=_params(("parallel",), VMEM_LIMIT),
    )(proj, proj, proj)


def _store_slabs(slabs, stage, dproj_ref, sems, blocks):
    s = stage.shape[1]

    def cast(c, carry):
        rows = pl.ds(pl.multiple_of(c * NORM_CH, NORM_CH), NORM_CH)
        for i, slab in enumerate(slabs):
            stage[i, rows, :] = slab[rows, :].astype(BF16)
        return carry

    lax.fori_loop(0, s // NORM_CH, cast, 0)
    copies = [pltpu.make_async_copy(stage.at[i], dproj_ref.at[:, pl.ds(pl.multiple_of(b * LANES, LANES), LANES)],
                                    sems.at[i]) for i, b in enumerate(blocks)]
    for cp in copies:
        cp.start()
    for cp in copies:
        cp.wait()


def sb_bwd(proj, qb0, kb0, vb0, n_slabs, dout, tot, dproj, *, name):
    s = proj.shape[0]
    nq = s // BLOCK
    nch = s // NORM_CH
    nsub = SB_KG // BLOCK
    scale = HEAD_DIM ** -0.5

    def body(q_ref, k_ref, v_ref, do_ref, tot_ref, dproj_in, dproj_ref,
             qlo_s, qhi_s, k_s, v_s, dlo_s, dhi_s, dq_ref, dk_ref, dv_ref, stage, sems):
        del dproj_in
        def prep(c, carry):
            rows = pl.ds(pl.multiple_of(c * NORM_CH, NORM_CH), NORM_CH)
            lo = _lo_mask((NORM_CH, LANES))
            qv = q_ref[rows, :] * scale
            dv = do_ref[rows, :]
            qlo_s[rows, :] = jnp.where(lo, qv, 0.0).astype(BF16)
            qhi_s[rows, :] = jnp.where(lo, 0.0, qv).astype(BF16)
            dlo_s[rows, :] = jnp.where(lo, dv, 0.0).astype(BF16)
            dhi_s[rows, :] = jnp.where(lo, 0.0, dv).astype(BF16)
            k_s[rows, :] = k_ref[rows, :].astype(BF16)
            v_s[rows, :] = v_ref[rows, :].astype(BF16)
            z = jnp.zeros((NORM_CH, LANES), F32)
            dk_ref[rows, :] = z
            dv_ref[rows, :] = z
            return carry

        lax.fori_loop(0, nch, prep, 0)

        row = lax.broadcasted_iota(I32, (BLOCK, BLOCK), 0)
        col = lax.broadcasted_iota(I32, (BLOCK, BLOCK), 1)
        lo = col < HEAD_DIM
        t_le = _twice(row <= col)
        rowg = lax.broadcasted_iota(I32, (BLOCK, SB_KG), 0)
        colg = lax.broadcasted_iota(I32, (BLOCK, SB_KG), 1)

        piece = lambda x, j: x[:, j * BLOCK:(j + 1) * BLOCK]
        chains = range(2 * SB_QT)
        nc = len(chains)

        def prefixes(x):
            return [[_split_dot(piece(x[i], j), t_le) for j in range(x[i].shape[1] // BLOCK)] for i in chains]

        def chain(pre, run, total=None):
            out = []
            for pj in pre:
                out.append(run + pj if total is None else total - run - pj)
                run = run + pj[:, BLOCK - 1:BLOCK]
            return jnp.concatenate(out, axis=1), run

        def qloop(qs, phase):
            q0 = pl.multiple_of(qs * (SB_QT * BLOCK), SB_QT * BLOCK)
            tile = lambda ref, i: ref[pl.ds(q0 + (i // 2) * BLOCK, BLOCK), :]
            qh = [tile((qlo_s, qhi_s)[i % 2], i) for i in chains]
            doh = [tile((dlo_s, dhi_s)[i % 2], i) for i in chains]
            tots = [tile(tot_ref, i)[:, (i % 2) * HEAD_DIM:(i % 2) * HEAD_DIM + 1] for i in chains]
            gd = (qs * SB_QT) // nsub

            def logits(gi):
                kg = k_s[pl.ds(pl.multiple_of(gi * SB_KG, SB_KG), SB_KG), :]
                return [_dot_nt(qh[i], kg) for i in chains]

            def group(gi, st, masks, npiece=nsub):
                k0 = pl.multiple_of(gi * SB_KG, SB_KG)
                wide = npiece * BLOCK
                kg, vg = k_s[pl.ds(k0, wide), :], v_s[pl.ds(k0, wide), :]
                cp, cg, dq = list(st[:nc]), list(st[nc:2 * nc]), st[2 * nc:2 * nc + SB_QT]
                z = [zi[:, :wide] for zi in st[2 * nc + SB_QT:]]
                masked = lambda x, i: x if masks is None else jnp.where(masks[i // 2][:, :wide], x, 0.0)
                z_next = logits(jnp.minimum(gi + 1, gd))
                da = [_dot_nt(doh[i], vg) for i in chains]
                sp = [_softplus(z[i]) for i in chains]
                lrem = [masked(-sp[i], i) for i in chains]
                pre = prefixes(lrem)
                e, a, g = [], [], []
                for i in chains:
                    suffix, cp[i] = chain(pre[i], cp[i], tots[i])
                    e.append(z[i] - sp[i])
                    a.append(masked(jnp.exp(e[i] + suffix), i))
                    g.append(a[i] * da[i])
                gpre = prefixes(g)
                dz = []
                for i in chains:
                    ginc, cg[i] = chain(gpre[i], cg[i])
                    dz.append(masked(g[i] - jnp.exp(e[i]) * ginc, i).astype(BF16))
                ab = [a[i].astype(BF16) for i in chains]
                dq = [dq[t] + jnp.where(lo, _dot(dz[2 * t], kg), _dot(dz[2 * t + 1], kg)) for t in range(SB_QT)]
                rows_of = lambda x: jnp.concatenate(x, axis=0)
                dk_ref[pl.ds(k0, wide), :] += _dot_tn(rows_of(dz), rows_of(qh))
                dv_ref[pl.ds(k0, wide), :] += _dot_tn(rows_of(ab), rows_of(doh))
                return (*cp, *cg, *dq, *z_next)

            zc = [jnp.zeros((BLOCK, 1), F32)] * (2 * nc)
            zq = [jnp.zeros((BLOCK, LANES), F32)] * SB_QT
            st = lax.fori_loop(0, gd, lambda gi, st: group(gi, st, None), (*zc, *zq, *logits(0)))
            st = group(gd, st, [(gd * SB_KG + colg) < (q0 + t * BLOCK + rowg) for t in range(SB_QT)],
                       (phase + 1) * SB_QT)
            for t in range(SB_QT):
                dq_ref[pl.ds(q0 + t * BLOCK, BLOCK), :] = st[2 * nc + t] * scale

        steps_per_group = nsub // SB_QT

        def per_group(g, carry):
            for phase in range(steps_per_group):
                qloop(g * steps_per_group + phase, phase)
            return carry

        lax.fori_loop(0, nq // nsub, per_group, 0)
        p = pl.program_id(0)
        _store_slabs((dq_ref, dk_ref, dv_ref), stage, dproj_ref, sems, (qb0 + p, kb0 + p, vb0 + p))

    slab = lambda b0: pl.BlockSpec((s, LANES), lambda p: (0, b0 + p), pipeline_mode=pl.Buffered(1))
    own = pl.BlockSpec((s, LANES), lambda p: (0, p), pipeline_mode=pl.Buffered(1))
    held = pl.BlockSpec(memory_space=pl.ANY)
    return pl.pallas_call(
        body, name=name, grid=(n_slabs,),
        in_specs=[slab(qb0), slab(kb0), slab(vb0), own, own, held],
        out_specs=held, out_shape=jax.ShapeDtypeStruct(dproj.shape, dproj.dtype),
        input_output_aliases={5: 0},
        scratch_shapes=[pltpu.VMEM((s, LANES), BF16) for _ in range(6)]
        + [pltpu.VMEM((s, LANES), F32) for _ in range(3)]
        + [pltpu.VMEM((3, s, LANES), BF16), pltpu.SemaphoreType.DMA((3,))],
        compiler_params=_params(("arbitrary",), VMEM_LIMIT),
    )(proj, proj, proj, dout, tot, dproj)


def _place():
    x, y, c = lax.axis_index("x"), lax.axis_index("y"), lax.axis_index("c")
    return x, y, c


def gather_small(v, *, name):
    m_per, n = v.shape

    def body(x_ref, out_ref, send_sems, recv_sems, local_sem):
        x, y, c = _place()
        me, sibling = (x, y, c), (x, y, 1 - c)
        chips = [(1 - x, y), (x, 1 - y), (1 - x, 1 - y)]

        def rows(px, py, pc):
            return out_ref.at[pl.ds((4 * px + 2 * py + pc) * m_per, m_per), :]

        def copy(k, block, to, src=None):
            return pltpu.make_async_remote_copy(
                src_ref=rows(*block) if src is None else src, dst_ref=rows(*block),
                send_sem=send_sems.at[k], recv_sem=recv_sems.at[k], device_id=to, device_id_type=MESH)

        mine = pltpu.make_async_copy(x_ref, rows(*me), local_sem)
        mine.start()
        first = [copy(0, me, sibling, src=x_ref)]
        first += [copy(1 + j, me, (*chip, c), src=x_ref) for j, chip in enumerate(chips)]
        for cp in first:
            cp.start()
        passed = [copy(4 + j, (*chip, c), sibling) for j, chip in enumerate(chips)]
        for j, chip in enumerate(chips):
            copy(1 + j, (*chip, c), me).wait_recv()
            passed[j].start()
        copy(0, sibling, me).wait_recv()
        for j, chip in enumerate(chips):
            copy(4 + j, (*chip, 1 - c), me).wait_recv()
        for cp in first + passed:
            cp.wait_send()
        mine.wait()

    return pl.pallas_call(
        body, name=name,
        out_shape=jax.ShapeDtypeStruct((N_DEV * m_per, n), v.dtype),
        in_specs=[pl.BlockSpec(memory_space=pltpu.VMEM)],
        out_specs=pl.BlockSpec(memory_space=pltpu.VMEM),
        scratch_shapes=[pltpu.SemaphoreType.DMA((7,)), pltpu.SemaphoreType.DMA((7,)), pltpu.SemaphoreType.DMA],
        compiler_params=_params(None, VMEM_LIMIT),
    )(v)


_HBM = pl.BlockSpec(memory_space=pltpu.HBM)
_SEM = pl.BlockSpec(memory_space=pltpu.SEMAPHORE)
_EFFECT = pltpu.SideEffectType.DATAFLOW_SIDE_EFFECTING


def _peer_copies(src_refs, land_refs, send_sems, recv_sems, per_dest):
    x, y, c = _place()
    me = 4 * x + 2 * y + c
    copies = []
    for src, land, ssem, rsem in zip(src_refs, land_refs, send_sems, recv_sems):
        for k in (1, 2, 4, 3, 5, 6, 7):
            px, py, pc = x ^ (k >> 2 & 1), y ^ (k >> 1 & 1), c ^ (k & 1)
            copies.append(pltpu.make_async_remote_copy(
                src_ref=src.at[4 * px + 2 * py + pc] if per_dest else src, dst_ref=land.at[me],
                send_sem=ssem.at[k - 1], recv_sem=rsem.at[k - 1], device_id=(px, py, pc), device_id_type=MESH))
    return copies


def _own_copies(src_refs, land_refs, send_sems, per_dest):
    x, y, c = _place()
    me = 4 * x + 2 * y + c
    return [pltpu.make_async_copy(src.at[me] if per_dest else src, land.at[me], ssem.at[7])
            for src, land, ssem in zip(src_refs, land_refs, send_sems)]


def exchange_start(srcs, per_dest, *, name):
    n = len(srcs)
    lands = [lax.empty(a.shape if per_dest else (N_DEV,) + a.shape, a.dtype) for a in srcs]

    def body(*refs):
        src_refs, land_refs = refs[:n], refs[n:2 * n]
        send_sems, recv_sems = refs[2 * n:3 * n], refs[3 * n:4 * n]
        token = refs[-1]
        for cp in _peer_copies(src_refs, land_refs, send_sems, recv_sems, per_dest):
            cp.start()
        for cp in _own_copies(src_refs, land_refs, send_sems, per_dest):
            cp.start()
        token[...] = jnp.zeros_like(token)

    hbm = lambda a: pltpu.HBM(a.shape, a.dtype)
    res = pl.pallas_call(
        body, name=name,
        out_shape=(*[pltpu.SemaphoreType.DMA((8,))] * n, *[pltpu.SemaphoreType.DMA((7,))] * n,
                   *[hbm(a) for a in srcs], *[hbm(a) for a in lands], jax.ShapeDtypeStruct((8, LANES), F32)),
        in_specs=[_HBM] * (2 * n),
        out_specs=(*[_SEM] * (2 * n), *[_HBM] * (2 * n), pl.BlockSpec(memory_space=pltpu.VMEM)),
        input_output_aliases={i: 2 * n + i for i in range(2 * n)},
        compiler_params=pltpu.CompilerParams(has_side_effects=_EFFECT),
    )(*[pltpu.with_memory_space_constraint(a, pltpu.HBM) for a in (*srcs, *lands)])
    handles = [(res[a], res[n + a], res[2 * n + a], res[3 * n + a]) for a in range(n)]
    return handles, res[-1]


def exchange_wait(handles, per_dest, after, *, name):
    n = len(handles)

    def body(*refs):
        src_refs, land_refs = refs[:n], refs[n:2 * n]
        send_sems, recv_sems = refs[2 * n:3 * n], refs[3 * n:4 * n]
        for cp in _peer_copies(src_refs, land_refs, send_sems, recv_sems, per_dest):
            cp.wait_send()
            cp.wait_recv()
        for cp in _own_copies(src_refs, land_refs, send_sems, per_dest):
            cp.wait()

    srcs, lands = [h[2] for h in handles], [h[3] for h in handles]
    hbm = lambda a: pltpu.HBM(a.shape, a.dtype)
    res = pl.pallas_call(
        body, name=name,
        out_shape=(*[hbm(a) for a in srcs], *[hbm(a) for a in lands]),
        in_specs=[*[_HBM] * (2 * n), *[_SEM] * (2 * n), pl.BlockSpec(memory_space=pl.ANY)],
        out_specs=tuple([_HBM] * (2 * n)),
        input_output_aliases={i: i for i in range(2 * n)},
        compiler_params=pltpu.CompilerParams(has_side_effects=_EFFECT),
    )(*srcs, *lands, *[h[0] for h in handles], *[h[1] for h in handles], after)
    return res[n:]


def _adamw_math(w, g, m, v):
    m = ADAM_B1 * m + (1.0 - ADAM_B1) * g
    v = ADAM_B2 * v + (1.0 - ADAM_B2) * (g * g)
    m_hat = m / (1.0 - ADAM_B1 ** ADAM_STEP)
    v_hat = v / (1.0 - ADAM_B2 ** ADAM_STEP)
    delta = -ADAM_LR * (m_hat / (jnp.sqrt(v_hat) + ADAM_EPS) + ADAM_WD * w)
    return delta, m, v


def adamw_parts(parts, w, m, v, layer, outs, *, name):
    depth, r, cdim = w.shape
    n_parts = parts.shape[0]
    tr = _pick(r, [t for t in (512, 256, 128, 112, 64, 32, 16) if t * cdim <= 256 * 1024])

    def body(p_ref, w_ref, m_ref, v_ref, g0, d0, nm0, nv0, g_ref, d_ref, nm_ref, nv_ref):
        g = p_ref[0].astype(F32)
        for q in range(1, n_parts):
            g = g + p_ref[q].astype(F32)
        delta, nm, nv = _adamw_math(w_ref[...], g, m_ref[...], v_ref[...])
        g_ref[...], d_ref[...], nm_ref[...], nv_ref[...] = g, delta, nm, nv

    t = pl.BlockSpec((None, tr, cdim), lambda i: (layer, i, 0))
    held = pl.BlockSpec(memory_space=pl.ANY)
    return pl.pallas_call(
        body, name=name, grid=(r // tr,),
        in_specs=[pl.BlockSpec((n_parts, tr, cdim), lambda i: (0, i, 0)), t, t, t, held, held, held, held],
        out_specs=[t, t, t, t],
        out_shape=[jax.ShapeDtypeStruct((depth, r, cdim), F32)] * 4,
        input_output_aliases={4: 0, 5: 1, 6: 2, 7: 3},
        compiler_params=_params(("parallel",), VMEM_LIMIT),
    )(parts, w, m, v, *outs)


def sum_devices(gathered, *, name):
    m_rows = gathered.shape[1]

    def body(ga_ref, g_ref):
        g = ga_ref[0]
        for dev in range(1, N_DEV):
            g = g + ga_ref[dev]
        g_ref[...] = g

    return pl.pallas_call(
        body, name=name, out_shape=jax.ShapeDtypeStruct((m_rows, LANES), F32),
        compiler_params=_params(None, VMEM_LIMIT),
    )(gathered)


def adamw_small(g, w, m, v, *, name):
    m_rows = w.shape[0]

    def body(g_ref, w_ref, m_ref, v_ref, d_ref, nm_ref, nv_ref):
        d_ref[...], nm_ref[...], nv_ref[...] = _adamw_math(w_ref[...], g_ref[...], m_ref[...], v_ref[...])

    return pl.pallas_call(
        body, name=name, out_shape=[jax.ShapeDtypeStruct((m_rows, LANES), F32)] * 3,
        compiler_params=_params(None, VMEM_LIMIT),
    )(g, w, m, v)


def _t5_bucket(dist):
    max_exact = N_BUCKETS // 2
    d = jnp.maximum(dist, 0)
    large = max_exact + (jnp.log(jnp.maximum(d, 1).astype(F32) / max_exact)
                         / math.log(T5_MAX_DIST / max_exact) * (N_BUCKETS - max_exact)).astype(I32)
    large = jnp.minimum(large, N_BUCKETS - 1)
    return jnp.where(d < max_exact, d, large)


def _rel():
    return jnp.arange(BLOCK)[:, None] + BLOCK - jnp.arange(2 * BLOCK)[None, :]


def _band_bias(table, dils, max_dists):
    rel = _rel()
    biases, buckets = [], []
    for d, md in zip(dils, max_dists):
        bk = _t5_bucket(rel * d)
        vis = (rel >= 0) & (rel <= md)
        looked_up = jnp.zeros((table.shape[1],) + rel.shape, F32)
        for b in range(N_BUCKETS):
            looked_up = jnp.where((bk == b)[None], table[b][:, None, None], looked_up)
        with_prev = jnp.where(vis[None], looked_up, NEG_INF)
        first = jnp.arange(2 * BLOCK)[None, None, :] >= BLOCK
        biases.append(jnp.stack([with_prev, jnp.where(first, with_prev, NEG_INF)]))
        buckets.append(bk.astype(I32))
    return jnp.stack(biases), jnp.stack(buckets)


def _pack(pieces, rows):
    flat = jnp.concatenate([p.reshape(-1) for p in pieces])
    return jnp.pad(flat, (0, rows * LANES - flat.shape[0])).reshape(rows, LANES)


def _unpack(packed, shapes):
    flat = packed.reshape(-1)
    out, off = [], 0
    for sh in shapes:
        n = math.prod(sh)
        out.append(flat[off:off + n].reshape(sh))
        off += n
    return out


def _tile2(g):
    return jnp.concatenate([g, g])


def kernel(x, attn_norm, w_in, a_q_gain, a_k_gain, a_sinks, c_q_gain, c_k_gain, rel_bias_table, mix_out_gain, w_out, ffn_norm, w_up, conv_w, conv_b, w_down, loss_target, m_attn_norm, m_w_in, m_a_q_gain, m_a_k_gain, m_a_sinks, m_c_q_gain, m_c_k_gain, m_rel_bias_table, m_mix_out_gain, m_w_out, m_ffn_norm, m_w_up, m_conv_w, m_conv_b, m_w_down, v_attn_norm, v_w_in, v_a_q_gain, v_a_k_gain, v_a_sinks, v_c_q_gain, v_c_k_gain, v_rel_bias_table, v_mix_out_gain, v_w_out, v_ffn_norm, v_w_up, v_conv_w, v_conv_b, v_w_down):
    depth, d_model, in_shard = w_in.shape
    ff2_shard = w_up.shape[2]
    s = x.shape[1]
    in_width, ff2 = N_DEV * in_shard, N_DEV * ff2_shard
    n_heads = d_model // HEAD_DIM
    ha, hb, hc = n_heads // 4, n_heads // 4, n_heads // 2
    sa, sb, sc = ha // 2, hb // 2, hc // 2
    kv_a = ha // 4
    assert kv_a == 2 and BLOCK == LANES
    cb_aq, cb_ak, cb_av = 0, sa, sa + 1
    cb_bq = sa + 2
    cb_bk, cb_bv = cb_bq + sb, cb_bq + 2 * sb
    cb_cq = cb_bq + 3 * sb
    cb_ck, cb_cv = cb_cq + sc, cb_cq + 2 * sc
    assert (cb_cv + sc) * LANES == in_width
    dev = 4 * lax.axis_index("x") + 2 * lax.axis_index("y") + lax.axis_index("c")

    per_array = 3
    wnames = ("w_in", "w_out", "w_up", "w_down", "conv_w")
    sent = dict(w_in=lambda w: w.T, w_up=lambda w: w.T, w_out=lambda w: w, w_down=lambda w: w, conv_w=lambda w: w)
    rows = lambda g: g.reshape(N_DEV * g.shape[1], g.shape[2])
    whole = dict(w_in=rows, w_up=rows, w_out=rows, w_down=rows,
                 conv_w=lambda g: jnp.transpose(g, (1, 0, 2)).reshape(g.shape[1], N_DEV * g.shape[2]))
    gathers = {}
    token = jnp.zeros((8, LANES), F32)
    for l in range(depth):
        for gi, group in enumerate([[n] for n in wnames] if l < per_array else [wnames]):
            srcs = [(sent[n](dict(w_in=w_in, w_out=w_out, w_up=w_up, w_down=w_down, conv_w=conv_w)[n][l])
                     + token[0, 0]).astype(F32 if n == "conv_w" else BF16) for n in group]
            handles, token = exchange_start(srcs, False, name=f"gather_start_{l}_{gi}")
            gathers.update({(l, n): h for n, h in zip(group, handles)})

    def gathered(l, names, after):
        landed = exchange_wait([gathers[l, n] for n in names], False, after,
                               name=f"gather_wait_{l}_{wnames.index(names[0])}")
        return {n: whole[n](g) for n, g in zip(names, landed)}

    bias_a, buckets_a = _band_bias(rel_bias_table[:, :ha], (1,), (WINDOW_A - 1,))
    bias_c, buckets_c = _band_bias(rel_bias_table[:, ha:], DILATIONS, (BLOCK,) * len(DILATIONS))

    xs = x[0]
    saved = []
    wi, wo, wu, wd, cw = ([None] * depth for _ in range(5))
    for l in range(depth):
        if l < per_array:
            need = lambda n, after, l=l: gathered(l, (n,), after)[n]
        else:
            layer_w = gathered(l, wnames, xs)
            need = lambda n, after: layer_w[n]
        wi[l] = need("w_in", token if l == 0 else xs)
        proj, h1 = norm_matmul(xs, attn_norm[l], wi[l], name="in_proj")
        sinks = jnp.repeat(a_sinks[l], HEAD_DIM).reshape(sa, 1, LANES)
        gaq, gak = _tile2(a_q_gain[l]), _tile2(a_k_gain[l])
        gcq, gck = _tile2(c_q_gain[l]), _tile2(c_k_gain[l])
        out_a, lse_a = banded_fwd(proj, cb_aq, cb_ak, cb_av, sa, gaq, gak, bias_a, (1,), sinks, True, name="swa_fwd")
        out_b, tot_b = sb_fwd(proj, cb_bq, cb_bk, cb_bv, sb, name="stick_fwd")
        out_c, lse_c = banded_fwd(proj, cb_cq, cb_ck, cb_cv, sc, gcq, gck, bias_c, DILATIONS, None, False,
                                  name="dilated_fwd")
        mix = mixnorm_fwd([out_a, out_b, out_c], mix_out_gain[l], name="mix_norm_fwd")
        wo[l] = need("w_out", mix)
        x_mid = matmul(mix, wo[l], res=xs, name="out_proj")
        h2 = rmsnorm_fwd(x_mid, ffn_norm[l], name="ffn_norm_fwd")
        wu[l] = need("w_up", h2)
        p = matmul(h2, wu[l], trans_b=True, name="up_proj")
        cw[l] = need("conv_w", p)
        act = ffn_act_fwd(p, cw[l], conv_b[l], name="ffn_act_fwd")
        wd[l] = need("w_down", act)
        x_out = matmul(act, wd[l], res=x_mid, name="down_proj")
        saved.append(dict(x_in=xs, h1=h1, proj=proj, out_a=out_a, lse_a=lse_a, out_b=out_b, tot_b=tot_b,
                          out_c=out_c, lse_c=lse_c, mix=mix, x_mid=x_mid, h2=h2, p=p, act=act,
                          sinks=sinks, gains=(gaq, gak, gcq, gck)))
        xs = x_out

    dx, dx_b, loss_part = loss_head(xs, loss_target[0], name="loss_head")

    small = {k: [None] * depth for k in ("attn_norm", "a_q_gain", "a_k_gain", "a_sinks", "c_q_gain", "c_k_gain",
                                         "mix_out_gain", "ffn_norm", "conv_w", "conv_b")}
    big = {k: [None] * depth for k in ("w_in", "w_out", "w_up", "w_down")}
    dbias_a = dbias_c = None
    scatters = {}
    token = jnp.zeros((8, LANES), F32)
    names_big = ("w_in", "w_out", "w_up", "w_down")

    def scatter(l, names):
        parts = [big[n][l] for n in names]
        handles, tok = exchange_start(parts, True, name=f"scatter_start_{l}_{names_big.index(names[0])}")
        scatters.update({(l, n): h for n, h in zip(names, handles)})
        return tok

    by_rows = lambda a: a.reshape(N_DEV, a.shape[0] // N_DEV, a.shape[1])
    for l in reversed(range(depth)):
        each = l == 0
        sv = saved[l]
        gaq, gak, gcq, gck = sv["gains"]
        da = matmul(dx_b, wd[l], trans_b=True, name="down_proj_dx")
        big["w_down"][l] = by_rows(matmul(sv["act"], dx_b, trans_a=True, out_dtype=BF16, name="down_proj_dw"))
        if each:
            token = scatter(l, ("w_down",))
        dp, small["conv_w"][l], small["conv_b"][l] = ffn_act_bwd(da, sv["p"], cw[l], conv_b[l] + token[0, 0],
                                                                 name="ffn_act_bwd")
        dh2 = matmul(dp, wu[l], name="up_proj_dx")
        big["w_up"][l] = matmul(sv["h2"], dp, trans_a=True, out_dtype=BF16, col_blocks=N_DEV, name="up_proj_dw")
        if each:
            token = scatter(l, ("w_up",))
        dx_mid, dx_mid_b, small["ffn_norm"][l] = rmsnorm_bwd(dh2, sv["x_mid"], ffn_norm[l] + token[0, 0], dx,
                                                   name="ffn_norm_bwd")
        dmix = matmul(dx_mid_b, wo[l], trans_b=True, name="out_proj_dx")
        big["w_out"][l] = by_rows(matmul(sv["mix"], dx_mid_b, trans_a=True, out_dtype=BF16, name="out_proj_dw"))
        if each:
            token = scatter(l, ("w_out",))
        (d_oa, d_ob, d_oc), small["mix_out_gain"][l] = mixnorm_bwd(
            dmix, [sv["out_a"], sv["out_b"], sv["out_c"]], mix_out_gain[l] + token[0, 0], name="mix_norm_bwd")
        dproj = lax.empty((s, in_width), BF16)
        dproj, db_a, dgq_a, dgk_a, dsink = banded_bwd(
            sv["proj"], cb_aq, cb_ak, cb_av, sa, gaq, gak, bias_a, (1,), sv["sinks"], True,
            d_oa, sv["out_a"], sv["lse_a"], dproj, name="swa_bwd")
        dproj = sb_bwd(sv["proj"], cb_bq, cb_bk, cb_bv, sb, d_ob, sv["tot_b"], dproj, name="stick_bwd")
        dproj, db_c, dgq_c, dgk_c = banded_bwd(
            sv["proj"], cb_cq, cb_ck, cb_cv, sc, gcq, gck, bias_c, DILATIONS, None, False,
            d_oc, sv["out_c"], sv["lse_c"], dproj, name="dilated_bwd")
        fold = lambda g: g.reshape(-1, HEAD_DIM).sum(axis=0)
        small["a_q_gain"][l], small["a_k_gain"][l] = fold(dgq_a), fold(dgk_a)
        small["c_q_gain"][l], small["c_k_gain"][l] = fold(dgq_c), fold(dgk_c)
        small["a_sinks"][l] = dsink[:, ::HEAD_DIM].reshape(-1)
        dbias_a = db_a if dbias_a is None else dbias_a + db_a
        dbias_c = db_c if dbias_c is None else dbias_c + db_c
        big["w_in"][l] = by_rows(matmul(dproj, sv["h1"], trans_a=True, out_dtype=BF16, name="in_proj_dw"))
        token = scatter(l, ("w_in",) if each else names_big)
        dh1 = matmul(dproj, wi[l], name="in_proj_dx")
        dx, dx_b, small["attn_norm"][l] = rmsnorm_bwd(dh1, sv["x_in"], attn_norm[l] + token[0, 0], dx_mid,
                                                name="attn_norm_bwd")

    flip = lambda t: jnp.swapaxes(t, 1, 2)
    w_big = dict(w_in=(flip(w_in), flip(m_w_in), flip(v_w_in)), w_out=(w_out, m_w_out, v_w_out),
                 w_up=(w_up, m_w_up, v_w_up), w_down=(w_down, m_w_down, v_w_down))
    results = {k: [lax.empty(w_big[k][0].shape, F32) for _ in range(4)] for k in names_big}

    def update(l, names, after):
        landed = exchange_wait([scatters[l, n] for n in names], True, after,
                               name=f"scatter_wait_{l}_{names_big.index(names[0])}")
        for k, parts in zip(names, landed):
            results[k] = adamw_parts(parts, *w_big[k], l, results[k], name="adamw_large")
        return results[names[-1]][0]

    after = dx_b
    for l in reversed(range(1, depth)):
        after = update(l, names_big, after)

    dtable = jnp.concatenate([bias_bwd(dbias_a, buckets_a, name="swa_bias_bwd"),
                              bias_bwd(dbias_c, buckets_c, name="dilated_bias_bwd")], axis=1)

    order = ("attn_norm", "a_q_gain", "a_k_gain", "a_sinks", "c_q_gain", "c_k_gain", "rel_bias_table",
             "mix_out_gain", "ffn_norm", "conv_w", "conv_b")
    partial = {k: jnp.stack(v) for k, v in small.items()}
    partial["rel_bias_table"] = dtable
    pieces = [partial[k] for k in order] + [loss_part.reshape(1)]
    n_small = sum(math.prod(pc.shape) for pc in pieces)
    rows = -(-n_small // (8 * LANES)) * 8
    packed, after = lax.optimization_barrier((_pack(pieces, rows), after))
    gathered = gather_small(packed, name="gather_small_grads")
    summed = _unpack(sum_devices(gathered.reshape(N_DEV, rows, LANES), name="sum_small_grads"),
                     [pc.shape for pc in pieces])
    g_small = dict(zip(order, summed[:-1]))
    loss = summed[-1][0]
    g_small["conv_w"] = lax.dynamic_slice_in_dim(g_small["conv_w"], dev * ff2_shard, ff2_shard, axis=2)

    w_small = dict(attn_norm=attn_norm, a_q_gain=a_q_gain, a_k_gain=a_k_gain, a_sinks=a_sinks, c_q_gain=c_q_gain,
                   c_k_gain=c_k_gain, rel_bias_table=rel_bias_table, mix_out_gain=mix_out_gain, ffn_norm=ffn_norm,
                   conv_w=conv_w, conv_b=conv_b)
    m_small = dict(attn_norm=m_attn_norm, a_q_gain=m_a_q_gain, a_k_gain=m_a_k_gain, a_sinks=m_a_sinks,
                   c_q_gain=m_c_q_gain, c_k_gain=m_c_k_gain, rel_bias_table=m_rel_bias_table,
                   mix_out_gain=m_mix_out_gain, ffn_norm=m_ffn_norm, conv_w=m_conv_w, conv_b=m_conv_b)
    v_small = dict(attn_norm=v_attn_norm, a_q_gain=v_a_q_gain, a_k_gain=v_a_k_gain, a_sinks=v_a_sinks,
                   c_q_gain=v_c_q_gain, c_k_gain=v_c_k_gain, rel_bias_table=v_rel_bias_table,
                   mix_out_gain=v_mix_out_gain, ffn_norm=v_ffn_norm, conv_w=v_conv_w, conv_b=v_conv_b)
    shapes = [w_small[k].shape for k in order]
    n_upd = sum(math.prod(sh) for sh in shapes)
    urows = -(-n_upd // (8 * LANES)) * 8
    packs = [_pack([d[k] for k in order], urows) for d in (g_small, w_small, m_small, v_small)]
    upd = adamw_small(*packs, name="adamw_small")
    delta_s, newm_s, newv_s = [dict(zip(order, _unpack(u, shapes))) for u in upd]

    after = update(0, names_big[1:], upd[0])
    update(0, names_big[:1], after)
    results["w_in"] = [flip(t) for t in results["w_in"]]
    g_big, delta_b, newm_b, newv_b = [{k: results[k][i] for k in names_big} for i in range(4)]

    all_names = ("attn_norm", "w_in", "a_q_gain", "a_k_gain", "a_sinks", "c_q_gain", "c_k_gain", "rel_bias_table",
                 "mix_out_gain", "w_out", "ffn_norm", "w_up", "conv_w", "conv_b", "w_down")
    pick = lambda sm, bg: [bg[k] if k in bg else sm[k] for k in all_names]
    return (loss, dx[None], *pick(g_small, g_big), *pick(delta_s, delta_b), *pick(newm_s, newm_b),
            *pick(newv_s, newv_b))
```
